```python
import math
import jax
import jax.numpy as jnp
from jax import lax
import numpy as np

D_MODEL = 1024
BATCH = 4
SEQ = 8192
DEPTH = 2

CTX_LEN = 256
GRID_W = 64

SSM_HEADS = 16
SSM_HEAD_DIM = 64
SSM_INNER = SSM_HEADS * SSM_HEAD_DIM
SSM_STATE = 128
SSM_GROUPS = 4
SSM_CONV = 5
SSD_CHUNK = 128
SSM_XBC = SSM_INNER + 2 * SSM_GROUPS * SSM_STATE

HY_WIDTH = D_MODEL
HY_ORDER = 2
HY_SHORT = 3
HY_BANDS = 16
HY_POS_DIM = 1 + 2 * HY_BANDS
HY_FFN = 64
HY_FAST_DECAY = 0.3
HY_SLOW_DECAY = 1.5
HY_DECAY_TARGET = 1e-2

HG_HEADS = 8
HG_KDIM = 128
HG_VDIM = D_MODEL // HG_HEADS
HG_QK = HG_HEADS * HG_KDIM
HG_V = HG_HEADS * HG_VDIM
HG_CHUNK = 64
F_FLOOR = 1e-20

N_EXPERTS = 32
TOP_K = 4
D_FF = D_MODEL
SWIGLU_LIMIT = 7.0
SWIGLU_ALPHA = 1.702
MOE_BLOCK = 256

N_BRANCHES = 3
IN_SIZES = (SSM_INNER, SSM_XBC, 2 * SSM_HEADS, (HY_ORDER + 1) * HY_WIDTH, HG_QK, 2 * HG_QK, HG_V, HG_V, N_BRANCHES * D_MODEL)
IN_DIM = sum(IN_SIZES)
EPS = 1e-6
F32 = jnp.float32

kernel_name = 'hybrid_ssd_hyena_hgrn2_moe_dit'


def split_cols(t, sizes):
    return jnp.split(t, np.cumsum(sizes)[:-1].tolist(), axis=-1)


def rms_norm(x, w):
    xf = x.astype(F32)
    y = xf * lax.rsqrt(jnp.mean(xf * xf, axis=-1, keepdims=True) + EPS)
    return (y * w.astype(F32)).astype(x.dtype)


def dwconv_centred(x, w, b):
    k = w.shape[0]
    y = lax.conv_general_dilated(x, w[:, None, :].astype(x.dtype), window_strides=(1,), padding=[(k // 2, k // 2)], dimension_numbers=('NWC', 'WIO', 'NWC'), feature_group_count=x.shape[-1])
    return y + b.astype(x.dtype)


def flip_seq(t):
    return jnp.flip(t, axis=1)


def to_col_major(t, rows):
    b, rest = t.shape[0], t.shape[2:]
    return jnp.swapaxes(t.reshape((b, rows, GRID_W) + rest), 1, 2).reshape((b, rows * GRID_W) + rest)


def from_col_major(t, rows):
    b, rest = t.shape[0], t.shape[2:]
    return jnp.swapaxes(t.reshape((b, GRID_W, rows) + rest), 1, 2).reshape((b, rows * GRID_W) + rest)


def segsum_exp(a):
    t = a.shape[-1]
    cs = jnp.cumsum(a, axis=-1)
    mask = jnp.tril(jnp.ones((t, t), dtype=bool))
    diff = jnp.where(mask, cs[..., :, None] - cs[..., None, :], 0.0)
    return jnp.where(mask, jnp.exp(diff), 0.0)


def ssd_chunked(xs, a, bm, cm, s0):
    b, L, h, p = xs.shape
    g, n = bm.shape[2], bm.shape[3]
    r = h // g
    c = L // SSD_CHUNK
    q = SSD_CHUNK
    X = xs.astype(F32).reshape(b, c, q, g, r, p)
    A = a.astype(F32).reshape(b, c, q, g, r).transpose(0, 3, 4, 1, 2)
    B = bm.astype(F32).reshape(b, c, q, g, n)
    C = cm.astype(F32).reshape(b, c, q, g, n)
    a_cs = jnp.cumsum(A, axis=-1)
    decay_in = segsum_exp(A)
    cb = jnp.einsum('bclgn,bcsgn->bcgls', C, B)
    y_diag = jnp.einsum('bcgls,bgrcls,bcsgrp->bclgrp', cb, decay_in, X)
    decay_states = jnp.exp(a_cs[..., -1:] - a_cs)
    states = jnp.einsum('bclgn,bgrcl,bclgrp->bcgrpn', B, decay_states, X)
    states = jnp.concatenate([s0.astype(F32).reshape(b, 1, g, r, p, n), states], axis=1)
    chunk_decay = segsum_exp(jnp.pad(a_cs[..., -1], ((0, 0), (0, 0), (0, 0), (1, 0))))
    states = jnp.einsum('bgrzc,bcgrpn->bzgrpn', chunk_decay, states)
    y_off = jnp.einsum('bclgn,bcgrpn,bgrcl->bclgrp', C, states[:, :-1], jnp.exp(a_cs))
    y = (y_diag + y_off).reshape(b, L, h, p)
    return y, states[:, -1].reshape(b, h, p, n)


def mamba_core(xbc, dt_raw, conv_w, conv_b, dt_bias, a_log, d_skip, s_f, s_b):
    b, L, _ = xbc.shape
    xbc = jax.nn.silu(dwconv_centred(xbc, conv_w, conv_b))
    xs, bm, cm = split_cols(xbc, (SSM_INNER, SSM_GROUPS * SSM_STATE, SSM_GROUPS * SSM_STATE))
    xs = xs.reshape(b, L, SSM_HEADS, SSM_HEAD_DIM)
    bm = bm.reshape(b, L, SSM_GROUPS, SSM_STATE)
    cm = cm.reshape(b, L, SSM_GROUPS, SSM_STATE)
    dt = jax.nn.softplus((dt_raw + dt_bias).astype(F32))
    a = -jnp.exp(a_log.astype(F32))
    dt_f, dt_b = dt[..., :SSM_HEADS], dt[..., SSM_HEADS:]
    y_f, s_f = ssd_chunked(xs * dt_f[..., None], dt_f * a[0], bm, cm, s_f)
    y_b, s_b = ssd_chunked(flip_seq(xs * dt_b[..., None]), flip_seq(dt_b * a[1]), flip_seq(bm), flip_seq(cm), s_b)
    y = y_f + flip_seq(y_b) + xs * d_skip[:, None]
    return y.reshape(b, L, SSM_INNER).astype(xbc.dtype), s_f, s_b


def hyena_filter_spectrum(L, w1, b1, w2, b2, w3, b3, freq):
    t = jnp.linspace(0.0, 1.0, L, dtype=F32)[:, None]
    w = 2.0 * math.pi * jnp.arange(L, dtype=F32)[:, None] / L
    bands = jnp.linspace(1e-4, HY_BANDS - 1, HY_BANDS, dtype=F32)
    feats = jnp.concatenate([t, jnp.cos(bands * w), -jnp.sin(bands * w)], axis=-1)
    h = jnp.sin(freq[0] * (feats @ w1 + b1))
    h = jnp.sin(freq[1] * (h @ w2 + b2))
    h = (h @ w3 + b3).astype(F32).reshape(L, 2, HY_ORDER, HY_WIDTH)
    max_decay = math.log(HY_DECAY_TARGET) / HY_FAST_DECAY
    min_decay = math.log(HY_DECAY_TARGET) / HY_SLOW_DECAY
    deltas = jnp.abs(jnp.linspace(min_decay, max_decay, HY_WIDTH, dtype=F32))
    h = h * jnp.exp(-t * deltas)[:, None, None, :]
    fwd, bwd = h[:, 0], h[:, 1]
    k = jnp.concatenate([fwd, jnp.zeros_like(fwd[:1]), jnp.flip(bwd[1:], axis=0)], axis=0)
    k = k * lax.rsqrt(jnp.sum(k * k, axis=0, keepdims=True) + EPS)
    return jnp.fft.rfft(k, axis=0)


def long_conv(u, k_spec, bias):
    L = u.shape[1]
    uf = u.astype(F32)
    y = jnp.fft.irfft(jnp.fft.rfft(uf, n=2 * L, axis=1) * k_spec[None], n=2 * L, axis=1)[:, :L]
    return (y + uf * bias.astype(F32)).astype(u.dtype)


def hyena_core(proj, conv_w, conv_b, w1, b1, w2, b2, w3, b3, freq, bias):
    L = proj.shape[1]
    proj = dwconv_centred(proj, conv_w, conv_b)
    v, x1, x2 = jnp.split(proj, HY_ORDER + 1, axis=-1)
    k_spec = hyena_filter_spectrum(L, w1, b1, w2, b2, w3, b3, freq)
    z = v
    for o, gate in enumerate((x1, x2)):
        z = gate * long_conv(z, k_spec[:, o], bias[o])
    return z


def hgrn_gates(a, lb):
    a = a.astype(F32)
    f = lb + (1.0 - lb) * jax.nn.sigmoid(a)
    log_f = jnp.log(jnp.maximum(f, F_FLOOR))
    k = (1.0 - lb) * jax.nn.sigmoid(-a)
    return log_f, k


def hgrn_prepare(hq, hf, hi, lb_f, lb_b):
    b, L, _ = hq.shape
    shp = (b, L, HG_HEADS, -1)
    q = jax.nn.silu(hq.astype(F32)).reshape(shp)
    a_f, a_b = jnp.split(hf, 2, axis=-1)
    lf_f, k_f = hgrn_gates(a_f, lb_f)
    lf_b, k_b = hgrn_gates(a_b, lb_b)
    v = hi.astype(F32).reshape(shp)
    return q, v, lf_f.reshape(shp), k_f.reshape(shp), lf_b.reshape(shp), k_b.reshape(shp)


def gla_chunked(q, k, v, log_f, s0):
    b, L, h, _ = q.shape
    dv = v.shape[-1]
    nc = L // HG_CHUNK

    def chunks(t):
        return t.reshape(b, nc, HG_CHUNK, h, t.shape[-1]).transpose(1, 0, 3, 2, 4)

    incl = jnp.tril(jnp.ones((HG_CHUNK, HG_CHUNK), dtype=bool))[:, :, None]

    def step(s, inp):
        qc, kc, vc, gc = inp
        g = jnp.cumsum(gc, axis=2)
        g_last = g[:, :, -1:, :]
        o = jnp.einsum('bhik,bhkv->bhiv', qc * jnp.exp(g), s)
        diff = jnp.where(incl, g[:, :, :, None, :] - g[:, :, None, :, :], 0.0)
        decay = jnp.where(incl, jnp.exp(diff), 0.0)
        attn = jnp.einsum('bhik,bhjk,bhijk->bhij', qc, kc, decay)
        o = o + jnp.einsum('bhij,bhjv->bhiv', attn, vc)
        s = jnp.exp(g_last[:, :, 0, :])[..., None] * s + jnp.einsum('bhjk,bhjv->bhkv', kc * jnp.exp(g_last - g), vc)
        return s, o

    s, o = lax.scan(step, s0, (chunks(q), chunks(k), chunks(v), chunks(log_f)))
    return o.transpose(1, 0, 3, 2, 4).reshape(b, L, h, dv), s


def hgrn_core(q, v, lf_f, k_f, lf_b, k_b, s_f, s_b):
    o_f, s_f = gla_chunked(q, k_f, v, lf_f, s_f)
    o_b, s_b = gla_chunked(flip_seq(q), flip_seq(k_b), flip_seq(v), flip_seq(lf_b), s_b)
    return o_f + flip_seq(o_b), s_f, s_b


def hgrn_readout(o, g, w):
    b, L = o.shape[0], o.shape[1]
    o = rms_norm(o, w.reshape(HG_HEADS, HG_VDIM))
    return (o.reshape(b, L, HG_V) * jax.nn.silu(g.astype(F32))).astype(g.dtype)


def branch_merge(ym, yh, yg, gate_pre, w_br_ssm, w_br_hy, w_br_hg, w_out):
    g_m, g_h, g_g = jnp.split(jax.nn.sigmoid(gate_pre), N_BRANCHES, axis=-1)
    merged = g_m * (ym @ w_br_ssm) + g_h * (yh @ w_br_hy) + g_g * (yg @ w_br_hg)
    return merged @ w_out


def hybrid_mixer(pc, px, lb_f, lb_b, ssm_conv_w, ssm_conv_b, ssm_dt_bias, ssm_a_log, ssm_d, ssm_norm, hy_conv_w, hy_conv_b, hy_w1, hy_b1, hy_w2, hy_b2, hy_w3, hy_b3, hy_freq, hy_bias, hg_norm, w_br_ssm, w_br_hy, w_br_hg, w_out):
    b = px.shape[0]
    rows = px.shape[1] // GRID_W
    zc, xbc_c, dt_c, hy_c, q_c, f_c, i_c, g_c, gate_c = split_cols(pc, IN_SIZES)
    zx, xbc_x, dt_x, hy_x, q_x, f_x, i_x, g_x, gate_x = split_cols(px, IN_SIZES)

    s0 = jnp.zeros((b, SSM_HEADS, SSM_HEAD_DIM, SSM_STATE), F32)
    ym_c, sf, sb = mamba_core(xbc_c, dt_c, ssm_conv_w, ssm_conv_b, ssm_dt_bias, ssm_a_log, ssm_d, s0, s0)
    ym_x, _, _ = mamba_core(xbc_x, dt_x, ssm_conv_w, ssm_conv_b, ssm_dt_bias, ssm_a_log, ssm_d, sf, sb)
    ym_c = rms_norm(ym_c * jax.nn.silu(zc), ssm_norm)
    ym_x = rms_norm(ym_x * jax.nn.silu(zx), ssm_norm)

    yh_c = hyena_core(hy_c, hy_conv_w, hy_conv_b, hy_w1, hy_b1, hy_w2, hy_b2, hy_w3, hy_b3, hy_freq, hy_bias)
    yh_x = hyena_core(hy_x, hy_conv_w, hy_conv_b, hy_w1, hy_b1, hy_w2, hy_b2, hy_w3, hy_b3, hy_freq, hy_bias)

    h0 = jnp.zeros((b, HG_HEADS, HG_KDIM, HG_VDIM), F32)
    og_c, hf, hb = hgrn_core(*hgrn_prepare(q_c, f_c, i_c, lb_f, lb_b), h0, h0)
    lat = [to_col_major(t, rows) for t in hgrn_prepare(q_x, f_x, i_x, lb_f, lb_b)]
    og_x, _, _ = hgrn_core(*lat, hf, hb)
    og_x = from_col_major(og_x, rows)
    yg_c = hgrn_readout(og_c, g_c, hg_norm)
    yg_x = hgrn_readout(og_x, g_x, hg_norm)

    out_c = branch_merge(ym_c, yh_c, yg_c, gate_c, w_br_ssm, w_br_hy, w_br_hg, w_out)
    out_x = branch_merge(ym_x, yh_x, yg_x, gate_x, w_br_ssm, w_br_hy, w_br_hg, w_out)
    return out_c, out_x


def moe_ffn(h, router_w, router_b, w1, b1, w2, b2):
    t, d = h.shape
    logits = (h @ router_w + router_b).astype(F32)
    top_v, top_e = lax.top_k(logits, TOP_K)
    gate_w = jax.nn.softmax(top_v, axis=-1)
    n = t * TOP_K
    flat_e = top_e.reshape(n)
    order = jnp.argsort(flat_e)
    se = flat_e[order]
    stok = (order // TOP_K).astype(jnp.int32)
    sw = gate_w.reshape(n)[order]
    counts = jnp.bincount(flat_e, length=N_EXPERTS)
    padded = (counts + MOE_BLOCK - 1) // MOE_BLOCK * MOE_BLOCK
    start = jnp.cumsum(counts) - counts
    pend = jnp.cumsum(padded)
    pstart = pend - padded
    dest = pstart[se] + jnp.arange(n, dtype=jnp.int32) - start[se]
    n_blocks = -(-n // MOE_BLOCK) + N_EXPERTS
    cap = n_blocks * MOE_BLOCK
    slot_tok = jnp.full((cap,), t, jnp.int32).at[dest].set(stok)
    slot_w = jnp.zeros((cap,), F32).at[dest].set(sw)
    block_e = jnp.minimum(jnp.searchsorted(pend, jnp.arange(n_blocks, dtype=jnp.int32) * MOE_BLOCK, side='right'), N_EXPERTS - 1)
    xb = jnp.concatenate([h, jnp.zeros((1, d), h.dtype)], axis=0)[slot_tok].reshape(n_blocks, MOE_BLOCK, d)

    def expert(args):
        xe, e = args
        hh = xe @ w1[e] + b1[e]
        g, u = jnp.split(hh, 2, axis=-1)
        g = jnp.minimum(g, SWIGLU_LIMIT)
        u = jnp.clip(u, -SWIGLU_LIMIT, SWIGLU_LIMIT)
        return ((u + 1.0) * g * jax.nn.sigmoid(SWIGLU_ALPHA * g)) @ w2[e] + b2[e]

    yb = lax.map(expert, (xb, block_e)).reshape(cap, d)
    y = jnp.zeros((t + 1, d), h.dtype).at[slot_tok].add((yb * slot_w[:, None]).astype(h.dtype))
    return y[:t]


def setup_inputs(seed: int = 0) -> dict:
    key = jax.random.key(seed)
    keys = iter(jax.random.split(key, 48))

    def nrm(shape, scale):
        return jax.random.normal(next(keys), shape, F32) * scale

    def gain(shape):
        return 1.0 + 0.05 * jax.random.normal(next(keys), shape, F32)

    d, n_l = D_MODEL, DEPTH
    hy_ch = (HY_ORDER + 1) * HY_WIDTH
    dt0 = jnp.exp(jax.random.uniform(next(keys), (n_l, 2 * SSM_HEADS), F32, math.log(1e-3), math.log(1e-1)))
    a0 = jax.random.uniform(next(keys), (n_l, 2, SSM_HEADS), F32, 1.0, 16.0)
    return {
        'x': nrm((BATCH, SEQ, d), 1.0),
        'c': nrm((BATCH, d), 1.0),
        'ctx': nrm((BATCH, CTX_LEN, d), 1.0),
        'c_ctx': nrm((d,), 1.0),
        'w_mod': nrm((n_l, d, 6 * d), 0.5 * d ** -0.5),
        'b_mod': nrm((n_l, 6 * d), 0.02),
        'norm_mix_pre': gain((n_l, d)),
        'norm_mix_post': gain((n_l, d)),
        'norm_ffn_pre': gain((n_l, d)),
        'norm_ffn_post': gain((n_l, d)),
        'w_in': nrm((n_l, d, IN_DIM), d ** -0.5),
        'ssm_conv_w': nrm((n_l, SSM_CONV, SSM_XBC), SSM_CONV ** -0.5),
        'ssm_conv_b': nrm((n_l, SSM_XBC), 0.02),
        'ssm_dt_bias': dt0 + jnp.log(-jnp.expm1(-dt0)),
        'ssm_a_log': jnp.log(a0),
        'ssm_d': gain((n_l, SSM_HEADS)),
        'ssm_norm': gain((n_l, SSM_INNER)),
        'hy_conv_w': nrm((n_l, HY_SHORT, hy_ch), HY_SHORT ** -0.5),
        'hy_conv_b': nrm((n_l, hy_ch), 0.02),
        'hy_w1': nrm((n_l, HY_POS_DIM, HY_FFN), HY_POS_DIM ** -0.5),
        'hy_b1': nrm((n_l, HY_FFN), 0.1),
        'hy_w2': nrm((n_l, HY_FFN, HY_FFN), HY_FFN ** -0.5),
        'hy_b2': nrm((n_l, HY_FFN), 0.1),
        'hy_w3': nrm((n_l, HY_FFN, 2 * HY_ORDER * HY_WIDTH), HY_FFN ** -0.5),
        'hy_b3': nrm((n_l, 2 * HY_ORDER * HY_WIDTH), 0.02),
        'hy_freq': gain((n_l, 2, HY_FFN)),
        'hy_bias': nrm((n_l, HY_ORDER, HY_WIDTH), 0.5),
        'hg_lb_logits': nrm((2, n_l, HG_QK), 0.5),
        'hg_norm': gain((n_l, HG_V)),
        'w_br_ssm': nrm((n_l, SSM_INNER, d), SSM_INNER ** -0.5),
        'w_br_hy': nrm((n_l, HY_WIDTH, d), HY_WIDTH ** -0.5),
        'w_br_hg': nrm((n_l, HG_V, d), HG_V ** -0.5),
        'w_out': nrm((n_l, d, d), d ** -0.5),
        'router_w': nrm((n_l, d, N_EXPERTS), d ** -0.5),
        'router_b': nrm((n_l, N_EXPERTS), 0.01),
        'exp_w1': nrm((n_l, N_EXPERTS, d, 2 * D_FF), d ** -0.5),
        'exp_b1': nrm((n_l, N_EXPERTS, 2 * D_FF), 0.01),
        'exp_w2': nrm((n_l, N_EXPERTS, D_FF, d), D_FF ** -0.5),
        'exp_b2': nrm((n_l, N_EXPERTS, d), 0.01),
    }


def reference(x, c, ctx, c_ctx, w_mod, b_mod, norm_mix_pre, norm_mix_post, norm_ffn_pre, norm_ffn_post, w_in, ssm_conv_w, ssm_conv_b, ssm_dt_bias, ssm_a_log, ssm_d, ssm_norm, hy_conv_w, hy_conv_b, hy_w1, hy_b1, hy_w2, hy_b2, hy_w3, hy_b3, hy_freq, hy_bias, hg_lb_logits, hg_norm, w_br_ssm, w_br_hy, w_br_hg, w_out, router_w, router_b, exp_w1, exp_b1, exp_w2, exp_b2):
    lb = jax.nn.softmax(hg_lb_logits.astype(F32), axis=1)
    lb = jnp.cumsum(lb, axis=1) - lb[:, :1]
    silu_c = jax.nn.silu(c)
    silu_cc = jax.nn.silu(c_ctx)
    for li in range(DEPTH):
        mx = jnp.split((silu_c @ w_mod[li] + b_mod[li])[:, None, :], 6, axis=-1)
        mc = jnp.split((silu_cc @ w_mod[li] + b_mod[li])[None, None, :], 6, axis=-1)
        hx = rms_norm(x, norm_mix_pre[li]) * (1.0 + mx[1]) + mx[0]
        hc = rms_norm(ctx, norm_mix_pre[li]) * (1.0 + mc[1]) + mc[0]
        mix_c, mix_x = hybrid_mixer(hc @ w_in[li], hx @ w_in[li], lb[0, li], lb[1, li], ssm_conv_w[li], ssm_conv_b[li], ssm_dt_bias[li], ssm_a_log[li], ssm_d[li], ssm_norm[li], hy_conv_w[li], hy_conv_b[li], hy_w1[li], hy_b1[li], hy_w2[li], hy_b2[li], hy_w3[li], hy_b3[li], hy_freq[li], hy_bias[li], hg_norm[li], w_br_ssm[li], w_br_hy[li], w_br_hg[li], w_out[li])
        x = x + mx[2] * rms_norm(mix_x, norm_mix_post[li])
        hx = rms_norm(x, norm_ffn_pre[li]) * (1.0 + mx[4]) + mx[3]
        if li < DEPTH - 1:
            ctx = ctx + mc[2] * rms_norm(mix_c, norm_mix_post[li])
            hc = rms_norm(ctx, norm_ffn_pre[li]) * (1.0 + mc[4]) + mc[3]
            n_ctx_tok = hc.shape[0] * hc.shape[1]
            f = moe_ffn(jnp.concatenate([hc.reshape(-1, D_MODEL), hx.reshape(-1, D_MODEL)], axis=0), router_w[li], router_b[li], exp_w1[li], exp_b1[li], exp_w2[li], exp_b2[li])
            ctx = ctx + mc[5] * rms_norm(f[:n_ctx_tok].reshape(ctx.shape), norm_ffn_post[li])
            fx = f[n_ctx_tok:]
        else:
            fx = moe_ffn(hx.reshape(-1, D_MODEL), router_w[li], router_b[li], exp_w1[li], exp_b1[li], exp_w2[li], exp_b2[li])
        x = x + mx[5] * rms_norm(fx.reshape(x.shape), norm_ffn_post[li])
    return x
```

```python
import functools
import math

import jax
import jax.numpy as jnp
import numpy as np
from jax import lax
from jax.experimental import pallas as pl
from jax.experimental.pallas import tpu as pltpu

D_MODEL = 1024
DEPTH = 2
GRID_W = 64

SSM_HEADS = 16
SSM_HEAD_DIM = 64
SSM_INNER = SSM_HEADS * SSM_HEAD_DIM
SSM_STATE = 128
SSM_GROUPS = 4
SSD_CHUNK = 128
SSM_XBC = SSM_INNER + 2 * SSM_GROUPS * SSM_STATE

HY_WIDTH = D_MODEL
HY_ORDER = 2
HY_BANDS = 16
HY_FAST_DECAY = 0.3
HY_SLOW_DECAY = 1.5
HY_DECAY_TARGET = 1e-2

HG_HEADS = 8
HG_KDIM = 128
HG_VDIM = D_MODEL // HG_HEADS
HG_QK = HG_HEADS * HG_KDIM
HG_V = HG_HEADS * HG_VDIM
HG_CHUNK = 64
F_FLOOR = 1e-20

N_EXPERTS = 32
TOP_K = 4
D_FF = D_MODEL
SWIGLU_LIMIT = 7.0
SWIGLU_ALPHA = 1.702
MOE_BLOCK = 256

N_BRANCHES = 3
IN_SIZES = (SSM_INNER, SSM_XBC, 2 * SSM_HEADS, (HY_ORDER + 1) * HY_WIDTH, HG_QK, 2 * HG_QK, HG_V, HG_V, N_BRANCHES * D_MODEL)
EPS = 1e-6
F32 = jnp.float32
BF16 = jnp.bfloat16

LANES = 128
VMEM_LIMIT = 56 * 1024 * 1024


def _mm_kernel(a_ref, b_ref, o_ref):
    o_ref[...] = jnp.dot(a_ref[...], b_ref[...], preferred_element_type=F32).astype(o_ref.dtype)


def _mm(a, b, out_dtype=F32, tm=512, tn=512):
    m, k = a.shape
    n = b.shape[1]
    tm = min(tm, m)
    tn = min(tn, n)
    assert m % tm == 0 and n % tn == 0, (m, n, tm, tn)
    return pl.pallas_call(
        _mm_kernel,
        grid=(n // tn, m // tm),
        in_specs=[pl.BlockSpec((tm, k), lambda j, i: (i, 0)), pl.BlockSpec((k, tn), lambda j, i: (0, j))],
        out_specs=pl.BlockSpec((tm, tn), lambda j, i: (i, j)),
        out_shape=jax.ShapeDtypeStruct((m, n), out_dtype),
        compiler_params=pltpu.CompilerParams(dimension_semantics=("arbitrary", "arbitrary"), vmem_limit_bytes=VMEM_LIMIT),
        name="dense_mm",
    )(a, b)


def _moe_kernel(be_ref, x_ref, w1_ref, b1_ref, w2_ref, b2_ref, sw_ref, o_ref):
    del be_ref
    hh = jnp.dot(x_ref[...], w1_ref[...], preferred_element_type=F32) + b1_ref[...]
    g = jnp.minimum(hh[:, :D_FF], SWIGLU_LIMIT)
    u = jnp.clip(hh[:, D_FF:], -SWIGLU_LIMIT, SWIGLU_LIMIT)
    act = (u + 1.0) * g * jax.nn.sigmoid(SWIGLU_ALPHA * g)
    y = jnp.dot(act.astype(BF16), w2_ref[...], preferred_element_type=F32) + b2_ref[...]
    o_ref[...] = (y * sw_ref[...]).astype(o_ref.dtype)


def _moe_experts(xb, block_e, slot_w, w1, b1, w2, b2):
    cap, d = xb.shape
    n_blocks = cap // MOE_BLOCK
    grid_spec = pltpu.PrefetchScalarGridSpec(
        num_scalar_prefetch=1,
        grid=(n_blocks,),
        in_specs=[
            pl.BlockSpec((MOE_BLOCK, d), lambda i, be: (i, 0)),
            pl.BlockSpec((None, d, 2 * D_FF), lambda i, be: (be[i], 0, 0)),
            pl.BlockSpec((None, 1, 2 * D_FF), lambda i, be: (be[i], 0, 0)),
            pl.BlockSpec((None, D_FF, d), lambda i, be: (be[i], 0, 0)),
            pl.BlockSpec((None, 1, d), lambda i, be: (be[i], 0, 0)),
            pl.BlockSpec((MOE_BLOCK, 1), lambda i, be: (i, 0)),
        ],
        out_specs=pl.BlockSpec((MOE_BLOCK, d), lambda i, be: (i, 0)),
    )
    return pl.pallas_call(
        _moe_kernel,
        grid_spec=grid_spec,
        out_shape=jax.ShapeDtypeStruct((cap, d), F32),
        compiler_params=pltpu.CompilerParams(dimension_semantics=("arbitrary",), vmem_limit_bytes=VMEM_LIMIT),
        name="moe_experts",
    )(block_e, xb, w1, b1.reshape(N_EXPERTS, 1, 2 * D_FF), w2, b2.reshape(N_EXPERTS, 1, d), slot_w.reshape(cap, 1))


def _split_cols(t, sizes):
    return jnp.split(t, np.cumsum(sizes)[:-1].tolist(), axis=-1)


def _rms_norm(x, w):
    xf = x.astype(F32)
    y = xf * lax.rsqrt(jnp.mean(xf * xf, axis=-1, keepdims=True) + EPS)
    return (y * w.astype(F32)).astype(x.dtype)


def _dwconv_centred(x, w, b):
    k = w.shape[0]
    y = lax.conv_general_dilated(x, w[:, None, :].astype(x.dtype), window_strides=(1,), padding=[(k // 2, k // 2)], dimension_numbers=('NWC', 'WIO', 'NWC'), feature_group_count=x.shape[-1], precision=lax.Precision.HIGHEST)
    return y + b.astype(x.dtype)


def _flip_seq(t):
    return jnp.flip(t, axis=1)


def _to_col_major(t, rows):
    b, rest = t.shape[0], t.shape[2:]
    return jnp.swapaxes(t.reshape((b, rows, GRID_W) + rest), 1, 2).reshape((b, rows * GRID_W) + rest)


def _from_col_major(t, rows):
    b, rest = t.shape[0], t.shape[2:]
    return jnp.swapaxes(t.reshape((b, GRID_W, rows) + rest), 1, 2).reshape((b, rows * GRID_W) + rest)


def _segsum_exp(a):
    t = a.shape[-1]
    cs = jnp.cumsum(a, axis=-1)
    mask = jnp.tril(jnp.ones((t, t), dtype=bool))
    diff = jnp.where(mask, cs[..., :, None] - cs[..., None, :], 0.0)
    return jnp.where(mask, jnp.exp(diff), 0.0)


def _ssd_chunked(xs, a, bm, cm, s0):
    b, L, h, p = xs.shape
    g, n = bm.shape[2], bm.shape[3]
    r = h // g
    c = L // SSD_CHUNK
    q = SSD_CHUNK
    X = xs.astype(F32).reshape(b, c, q, g, r, p)
    A = a.astype(F32).reshape(b, c, q, g, r).transpose(0, 3, 4, 1, 2)
    B = bm.astype(F32).reshape(b, c, q, g, n)
    C = cm.astype(F32).reshape(b, c, q, g, n)
    a_cs = jnp.cumsum(A, axis=-1)
    decay_in = _segsum_exp(A)
    cb = jnp.einsum('bclgn,bcsgn->bcgls', C, B)
    y_diag = jnp.einsum('bcgls,bgrcls,bcsgrp->bclgrp', cb, decay_in, X)
    decay_states = jnp.exp(a_cs[..., -1:] - a_cs)
    states = jnp.einsum('bclgn,bgrcl,bclgrp->bcgrpn', B, decay_states, X)
    states = jnp.concatenate([s0.astype(F32).reshape(b, 1, g, r, p, n), states], axis=1)
    chunk_decay = _segsum_exp(jnp.pad(a_cs[..., -1], ((0, 0), (0, 0), (0, 0), (1, 0))))
    states = jnp.einsum('bgrzc,bcgrpn->bzgrpn', chunk_decay, states)
    y_off = jnp.einsum('bclgn,bcgrpn,bgrcl->bclgrp', C, states[:, :-1], jnp.exp(a_cs))
    y = (y_diag + y_off).reshape(b, L, h, p)
    return y, states[:, -1].reshape(b, h, p, n)


def _mamba_core(xbc, dt_raw, conv_w, conv_b, dt_bias, a_log, d_skip, s_f, s_b):
    b, L, _ = xbc.shape
    xbc = jax.nn.silu(_dwconv_centred(xbc, conv_w, conv_b))
    xs, bm, cm = _split_cols(xbc, (SSM_INNER, SSM_GROUPS * SSM_STATE, SSM_GROUPS * SSM_STATE))
    xs = xs.reshape(b, L, SSM_HEADS, SSM_HEAD_DIM)
    bm = bm.reshape(b, L, SSM_GROUPS, SSM_STATE)
    cm = cm.reshape(b, L, SSM_GROUPS, SSM_STATE)
    dt = jax.nn.softplus((dt_raw + dt_bias).astype(F32))
    a = -jnp.exp(a_log.astype(F32))
    dt_f, dt_b = dt[..., :SSM_HEADS], dt[..., SSM_HEADS:]
    y_f, s_f = _ssd_chunked(xs * dt_f[..., None], dt_f * a[0], bm, cm, s_f)
    y_b, s_b = _ssd_chunked(_flip_seq(xs * dt_b[..., None]), _flip_seq(dt_b * a[1]), _flip_seq(bm), _flip_seq(cm), s_b)
    y = y_f + _flip_seq(y_b) + xs * d_skip[:, None]
    return y.reshape(b, L, SSM_INNER).astype(xbc.dtype), s_f, s_b


def _hyena_filter_spectrum(L, w1, b1, w2, b2, w3, b3, freq):
    hp = lax.Precision.HIGHEST
    t = jnp.linspace(0.0, 1.0, L, dtype=F32)[:, None]
    w = 2.0 * math.pi * jnp.arange(L, dtype=F32)[:, None] / L
    bands = jnp.linspace(1e-4, HY_BANDS - 1, HY_BANDS, dtype=F32)
    feats = jnp.concatenate([t, jnp.cos(bands * w), -jnp.sin(bands * w)], axis=-1)
    h = jnp.sin(freq[0] * (jnp.dot(feats, w1, precision=hp) + b1))
    h = jnp.sin(freq[1] * (jnp.dot(h, w2, precision=hp) + b2))
    h = (jnp.dot(h, w3, precision=hp) + b3).astype(F32).reshape(L, 2, HY_ORDER, HY_WIDTH)
    max_decay = math.log(HY_DECAY_TARGET) / HY_FAST_DECAY
    min_decay = math.log(HY_DECAY_TARGET) / HY_SLOW_DECAY
    deltas = jnp.abs(jnp.linspace(min_decay, max_decay, HY_WIDTH, dtype=F32))
    h = h * jnp.exp(-t * deltas)[:, None, None, :]
    fwd, bwd = h[:, 0], h[:, 1]
    k = jnp.concatenate([fwd, jnp.zeros_like(fwd[:1]), jnp.flip(bwd[1:], axis=0)], axis=0)
    k = k * lax.rsqrt(jnp.sum(k * k, axis=0, keepdims=True) + EPS)
    return jnp.fft.rfft(k, axis=0)


def _long_conv(u, k_spec, bias):
    L = u.shape[1]
    uf = u.astype(F32)
    y = jnp.fft.irfft(jnp.fft.rfft(uf, n=2 * L, axis=1) * k_spec[None], n=2 * L, axis=1)[:, :L]
    return (y + uf * bias.astype(F32)).astype(u.dtype)


def _hyena_core(proj, conv_w, conv_b, w1, b1, w2, b2, w3, b3, freq, bias):
    L = proj.shape[1]
    proj = _dwconv_centred(proj, conv_w, conv_b)
    v, x1, x2 = jnp.split(proj, HY_ORDER + 1, axis=-1)
    k_spec = _hyena_filter_spectrum(L, w1, b1, w2, b2, w3, b3, freq)
    z = v
    for o, gate in enumerate((x1, x2)):
        z = gate * _long_conv(z, k_spec[:, o], bias[o])
    return z


def _hgrn_gates(a, lb):
    a = a.astype(F32)
    f = lb + (1.0 - lb) * jax.nn.sigmoid(a)
    log_f = jnp.log(jnp.maximum(f, F_FLOOR))
    k = (1.0 - lb) * jax.nn.sigmoid(-a)
    return log_f, k


def _hgrn_prepare(hq, hf, hi, lb_f, lb_b):
    b, L, _ = hq.shape
    shp = (b, L, HG_HEADS, -1)
    q = jax.nn.silu(hq.astype(F32)).reshape(shp)
    a_f, a_b = jnp.split(hf, 2, axis=-1)
    lf_f, k_f = _hgrn_gates(a_f, lb_f)
    lf_b, k_b = _hgrn_gates(a_b, lb_b)
    v = hi.astype(F32).reshape(shp)
    return q, v, lf_f.reshape(shp), k_f.reshape(shp), lf_b.reshape(shp), k_b.reshape(shp)


def _gla_chunked(q, k, v, log_f, s0):
    b, L, h, _ = q.shape
    dv = v.shape[-1]
    nc = L // HG_CHUNK

    def chunks(t):
        return t.reshape(b, nc, HG_CHUNK, h, t.shape[-1]).transpose(1, 0, 3, 2, 4)

    incl = jnp.tril(jnp.ones((HG_CHUNK, HG_CHUNK), dtype=bool))[:, :, None]

    def step(s, inp):
        qc, kc, vc, gc = inp
        g = jnp.cumsum(gc, axis=2)
        g_last = g[:, :, -1:, :]
        o = jnp.einsum('bhik,bhkv->bhiv', qc * jnp.exp(g), s)
        diff = jnp.where(incl, g[:, :, :, None, :] - g[:, :, None, :, :], 0.0)
        decay = jnp.where(incl, jnp.exp(diff), 0.0)
        attn = jnp.einsum('bhik,bhjk,bhijk->bhij', qc, kc, decay)
        o = o + jnp.einsum('bhij,bhjv->bhiv', attn, vc)
        s = jnp.exp(g_last[:, :, 0, :])[..., None] * s + jnp.einsum('bhjk,bhjv->bhkv', kc * jnp.exp(g_last - g), vc)
        return s, o

    s, o = lax.scan(step, s0, (chunks(q), chunks(k), chunks(v), chunks(log_f)))
    return o.transpose(1, 0, 3, 2, 4).reshape(b, L, h, dv), s


def _hgrn_core(q, v, lf_f, k_f, lf_b, k_b, s_f, s_b):
    o_f, s_f = _gla_chunked(q, k_f, v, lf_f, s_f)
    o_b, s_b = _gla_chunked(_flip_seq(q), _flip_seq(k_b), _flip_seq(v), _flip_seq(lf_b), s_b)
    return o_f + _flip_seq(o_b), s_f, s_b


def _hgrn_readout(o, g, w):
    b, L = o.shape[0], o.shape[1]
    o = _rms_norm(o, w.reshape(HG_HEADS, HG_VDIM))
    return (o.reshape(b, L, HG_V) * jax.nn.silu(g.astype(F32))).astype(g.dtype)


def _branch_merge(ym, yh, yg, gate_pre, w_br_ssm, w_br_hy, w_br_hg, w_out):
    shp = ym.shape
    g_m, g_h, g_g = jnp.split(jax.nn.sigmoid(gate_pre), N_BRANCHES, axis=-1)

    def proj(y, w):
        return _mm(y.reshape(-1, y.shape[-1]).astype(BF16), w.astype(BF16)).reshape(shp[:-1] + (D_MODEL,))

    merged = g_m * proj(ym, w_br_ssm) + g_h * proj(yh, w_br_hy) + g_g * proj(yg, w_br_hg)
    return proj(merged, w_out)


def _in_proj(h, w_in):
    b, L, d = h.shape
    hb = h.reshape(b * L, d).astype(BF16)
    outs = []
    for wseg in _split_cols(w_in, IN_SIZES):
        n = wseg.shape[1]
        n_pad = -(-n // LANES) * LANES
        wb = jnp.pad(wseg, ((0, 0), (0, n_pad - n))).astype(BF16)
        outs.append(_mm(hb, wb)[:, :n].reshape(b, L, n))
    return outs


def _hybrid_mixer(hc, hx, w_in, lb_f, lb_b, ssm_conv_w, ssm_conv_b, ssm_dt_bias, ssm_a_log, ssm_d, ssm_norm, hy_conv_w, hy_conv_b, hy_w1, hy_b1, hy_w2, hy_b2, hy_w3, hy_b3, hy_freq, hy_bias, hg_norm, w_br_ssm, w_br_hy, w_br_hg, w_out):
    b = hx.shape[0]
    rows = hx.shape[1] // GRID_W
    zc, xbc_c, dt_c, hy_c, q_c, f_c, i_c, g_c, gate_c = _in_proj(hc, w_in)
    zx, xbc_x, dt_x, hy_x, q_x, f_x, i_x, g_x, gate_x = _in_proj(hx, w_in)

    s0 = jnp.zeros((b, SSM_HEADS, SSM_HEAD_DIM, SSM_STATE), F32)
    ym_c, sf, sb = _mamba_core(xbc_c, dt_c, ssm_conv_w, ssm_conv_b, ssm_dt_bias, ssm_a_log, ssm_d, s0, s0)
    ym_x, _, _ = _mamba_core(xbc_x, dt_x, ssm_conv_w, ssm_conv_b, ssm_dt_bias, ssm_a_log, ssm_d, sf, sb)
    ym_c = _rms_norm(ym_c * jax.nn.silu(zc), ssm_norm)
    ym_x = _rms_norm(ym_x * jax.nn.silu(zx), ssm_norm)

    yh_c = _hyena_core(hy_c, hy_conv_w, hy_conv_b, hy_w1, hy_b1, hy_w2, hy_b2, hy_w3, hy_b3, hy_freq, hy_bias)
    yh_x = _hyena_core(hy_x, hy_conv_w, hy_conv_b, hy_w1, hy_b1, hy_w2, hy_b2, hy_w3, hy_b3, hy_freq, hy_bias)

    h0 = jnp.zeros((b, HG_HEADS, HG_KDIM, HG_VDIM), F32)
    og_c, hf, hb = _hgrn_core(*_hgrn_prepare(q_c, f_c, i_c, lb_f, lb_b), h0, h0)
    lat = [_to_col_major(t, rows) for t in _hgrn_prepare(q_x, f_x, i_x, lb_f, lb_b)]
    og_x, _, _ = _hgrn_core(*lat, hf, hb)
    og_x = _from_col_major(og_x, rows)
    yg_c = _hgrn_readout(og_c, g_c, hg_norm)
    yg_x = _hgrn_readout(og_x, g_x, hg_norm)

    out_c = _branch_merge(ym_c, yh_c, yg_c, gate_c, w_br_ssm, w_br_hy, w_br_hg, w_out)
    out_x = _branch_merge(ym_x, yh_x, yg_x, gate_x, w_br_ssm, w_br_hy, w_br_hg, w_out)
    return out_c, out_x


def _moe_ffn(h, router_w, router_b, w1, b1, w2, b2):
    t, d = h.shape
    logits = (jnp.dot(h, router_w, precision=lax.Precision.HIGHEST) + router_b).astype(F32)
    top_v, top_e = lax.top_k(logits, TOP_K)
    gate_w = jax.nn.softmax(top_v, axis=-1)
    n = t * TOP_K
    flat_e = top_e.reshape(n)
    order = jnp.argsort(flat_e)
    se = flat_e[order]
    stok = (order // TOP_K).astype(jnp.int32)
    sw = gate_w.reshape(n)[order]
    counts = jnp.bincount(flat_e, length=N_EXPERTS)
    padded = (counts + MOE_BLOCK - 1) // MOE_BLOCK * MOE_BLOCK
    start = jnp.cumsum(counts) - counts
    pend = jnp.cumsum(padded)
    pstart = pend - padded
    dest = pstart[se] + jnp.arange(n, dtype=jnp.int32) - start[se]
    n_blocks = -(-n // MOE_BLOCK) + N_EXPERTS
    cap = n_blocks * MOE_BLOCK
    slot_tok = jnp.full((cap,), t, jnp.int32).at[dest].set(stok)
    slot_w = jnp.zeros((cap,), F32).at[dest].set(sw)
    block_e = jnp.minimum(jnp.searchsorted(pend, jnp.arange(n_blocks, dtype=jnp.int32) * MOE_BLOCK, side='right'), N_EXPERTS - 1).astype(jnp.int32)
    xb = jnp.concatenate([h.astype(BF16), jnp.zeros((1, d), BF16)], axis=0)[slot_tok]
    yb = _moe_experts(xb, block_e, slot_w, w1.astype(BF16), b1, w2.astype(BF16), b2)
    y = jnp.zeros((t + 1, d), h.dtype).at[slot_tok].add(yb.astype(h.dtype))
    return y[:t]


def kernel(x, c, ctx, c_ctx, w_mod, b_mod, norm_mix_pre, norm_mix_post, norm_ffn_pre, norm_ffn_post, w_in, ssm_conv_w, ssm_conv_b, ssm_dt_bias, ssm_a_log, ssm_d, ssm_norm, hy_conv_w, hy_conv_b, hy_w1, hy_b1, hy_w2, hy_b2, hy_w3, hy_b3, hy_freq, hy_bias, hg_lb_logits, hg_norm, w_br_ssm, w_br_hy, w_br_hg, w_out, router_w, router_b, exp_w1, exp_b1, exp_w2, exp_b2):
    hp = lax.Precision.HIGHEST
    lb = jax.nn.softmax(hg_lb_logits.astype(F32), axis=1)
    lb = jnp.cumsum(lb, axis=1) - lb[:, :1]
    silu_c = jax.nn.silu(c)
    silu_cc = jax.nn.silu(c_ctx)
    for li in range(DEPTH):
        mx = jnp.split((jnp.dot(silu_c, w_mod[li], precision=hp) + b_mod[li])[:, None, :], 6, axis=-1)
        mc = jnp.split((jnp.dot(silu_cc, w_mod[li], precision=hp) + b_mod[li])[None, None, :], 6, axis=-1)
        hx = _rms_norm(x, norm_mix_pre[li]) * (1.0 + mx[1]) + mx[0]
        hc = _rms_norm(ctx, norm_mix_pre[li]) * (1.0 + mc[1]) + mc[0]
        mix_c, mix_x = _hybrid_mixer(hc, hx, w_in[li], lb[0, li], lb[1, li], ssm_conv_w[li], ssm_conv_b[li], ssm_dt_bias[li], ssm_a_log[li], ssm_d[li], ssm_norm[li], hy_conv_w[li], hy_conv_b[li], hy_w1[li], hy_b1[li], hy_w2[li], hy_b2[li], hy_w3[li], hy_b3[li], hy_freq[li], hy_bias[li], hg_norm[li], w_br_ssm[li], w_br_hy[li], w_br_hg[li], w_out[li])
        x = x + mx[2] * _rms_norm(mix_x, norm_mix_post[li])
        hx = _rms_norm(x, norm_ffn_pre[li]) * (1.0 + mx[4]) + mx[3]
        if li < DEPTH - 1:
            ctx = ctx + mc[2] * _rms_norm(mix_c, norm_mix_post[li])
            hc = _rms_norm(ctx, norm_ffn_pre[li]) * (1.0 + mc[4]) + mc[3]
            n_ctx_tok = hc.shape[0] * hc.shape[1]
            f = _moe_ffn(jnp.concatenate([hc.reshape(-1, D_MODEL), hx.reshape(-1, D_MODEL)], axis=0), router_w[li], router_b[li], exp_w1[li], exp_b1[li], exp_w2[li], exp_b2[li])
            ctx = ctx + mc[5] * _rms_norm(f[:n_ctx_tok].reshape(ctx.shape), norm_ffn_post[li])
            fx = f[n_ctx_tok:]
        else:
            fx = _moe_ffn(hx.reshape(-1, D_MODEL), router_w[li], router_b[li], exp_w1[li], exp_b1[li], exp_w2[li], exp_b2[li])
        x = x + mx[5] * _rms_norm(fx.reshape(x.shape), norm_ffn_post[li])
    return x
```

```python
import functools
import math

import jax
import jax.numpy as jnp
import numpy as np
from jax import lax
from jax.experimental import pallas as pl
from jax.experimental.pallas import tpu as pltpu

D_MODEL = 1024
DEPTH = 2
GRID_W = 64

SSM_HEADS = 16
SSM_HEAD_DIM = 64
SSM_INNER = SSM_HEADS * SSM_HEAD_DIM
SSM_STATE = 128
SSM_GROUPS = 4
SSD_CHUNK = 128
SSM_XBC = SSM_INNER + 2 * SSM_GROUPS * SSM_STATE

HY_WIDTH = D_MODEL
HY_ORDER = 2
HY_BANDS = 16
HY_FAST_DECAY = 0.3
HY_SLOW_DECAY = 1.5
HY_DECAY_TARGET = 1e-2

HG_HEADS = 8
HG_KDIM = 128
HG_VDIM = D_MODEL // HG_HEADS
HG_QK = HG_HEADS * HG_KDIM
HG_V = HG_HEADS * HG_VDIM
HG_CHUNK = 64
F_FLOOR = 1e-20

N_EXPERTS = 32
TOP_K = 4
D_FF = D_MODEL
SWIGLU_LIMIT = 7.0
SWIGLU_ALPHA = 1.702
MOE_BLOCK = 256

N_BRANCHES = 3
IN_SIZES = (SSM_INNER, SSM_XBC, 2 * SSM_HEADS, (HY_ORDER + 1) * HY_WIDTH, HG_QK, 2 * HG_QK, HG_V, HG_V, N_BRANCHES * D_MODEL)
EPS = 1e-6
F32 = jnp.float32
BF16 = jnp.bfloat16

LANES = 128
VMEM_LIMIT = 56 * 1024 * 1024


def _mm_kernel(a_ref, b_ref, o_ref):
    o_ref[...] = jnp.dot(a_ref[...], b_ref[...], preferred_element_type=F32).astype(o_ref.dtype)


def _mm(a, b, out_dtype=F32, tm=512, tn=512):
    m, k = a.shape
    n = b.shape[1]
    tm = min(tm, m)
    tn = min(tn, n)
    assert m % tm == 0 and n % tn == 0, (m, n, tm, tn)
    return pl.pallas_call(
        _mm_kernel,
        grid=(n // tn, m // tm),
        in_specs=[pl.BlockSpec((tm, k), lambda j, i: (i, 0)), pl.BlockSpec((k, tn), lambda j, i: (0, j))],
        out_specs=pl.BlockSpec((tm, tn), lambda j, i: (i, j)),
        out_shape=jax.ShapeDtypeStruct((m, n), out_dtype),
        compiler_params=pltpu.CompilerParams(dimension_semantics=("arbitrary", "arbitrary"), vmem_limit_bytes=VMEM_LIMIT),
        name="dense_mm",
    )(a, b)


def _moe_kernel(be_ref, x_ref, w1_ref, b1_ref, w2_ref, b2_ref, sw_ref, o_ref):
    del be_ref
    hh = jnp.dot(x_ref[...], w1_ref[...], preferred_element_type=F32) + b1_ref[...]
    g = jnp.minimum(hh[:, :D_FF], SWIGLU_LIMIT)
    u = jnp.clip(hh[:, D_FF:], -SWIGLU_LIMIT, SWIGLU_LIMIT)
    act = (u + 1.0) * g * jax.nn.sigmoid(SWIGLU_ALPHA * g)
    y = jnp.dot(act.astype(BF16), w2_ref[...], preferred_element_type=F32) + b2_ref[...]
    o_ref[...] = (y * sw_ref[...]).astype(o_ref.dtype)


def _moe_experts(xb, block_e, slot_w, w1, b1, w2, b2):
    cap, d = xb.shape
    n_blocks = cap // MOE_BLOCK
    grid_spec = pltpu.PrefetchScalarGridSpec(
        num_scalar_prefetch=1,
        grid=(n_blocks,),
        in_specs=[
            pl.BlockSpec((MOE_BLOCK, d), lambda i, be: (i, 0)),
            pl.BlockSpec((None, d, 2 * D_FF), lambda i, be: (be[i], 0, 0)),
            pl.BlockSpec((None, 1, 2 * D_FF), lambda i, be: (be[i], 0, 0)),
            pl.BlockSpec((None, D_FF, d), lambda i, be: (be[i], 0, 0)),
            pl.BlockSpec((None, 1, d), lambda i, be: (be[i], 0, 0)),
            pl.BlockSpec((MOE_BLOCK, 1), lambda i, be: (i, 0)),
        ],
        out_specs=pl.BlockSpec((MOE_BLOCK, d), lambda i, be: (i, 0)),
    )
    return pl.pallas_call(
        _moe_kernel,
        grid_spec=grid_spec,
        out_shape=jax.ShapeDtypeStruct((cap, d), F32),
        compiler_params=pltpu.CompilerParams(dimension_semantics=("arbitrary",), vmem_limit_bytes=VMEM_LIMIT),
        name="moe_experts",
    )(block_e, xb, w1, b1.reshape(N_EXPERTS, 1, 2 * D_FF), w2, b2.reshape(N_EXPERTS, 1, d), slot_w.reshape(cap, 1))


SUBLANES = 8
NEG_BIG = -1e30
HIER_LEVELS = (64, 32, 16)


def _split3(x):
    h1 = x.astype(BF16)
    r1 = x - h1.astype(F32)
    h2 = r1.astype(BF16)
    h3 = (r1 - h2.astype(F32)).astype(BF16)
    return h1, h2, h3


def _dot_nt(a, b):
    return lax.dot_general(a, b, (((1,), (1,)), ((), ())), preferred_element_type=F32)


def _gla_kernel(q_ref, a_ref, v_ref, lb_ref, *rest, reverse):
    if reverse:
        of_ref, w_ref, o_ref, st_ref = rest
    else:
        o_ref, st_ref = rest
    Q = HG_CHUNK

    @pl.when(pl.program_id(1) == 0)
    def _():
        st_ref[...] = jnp.zeros_like(st_ref)

    row = lax.broadcasted_iota(jnp.int32, (Q, Q), 0)
    col = lax.broadcasted_iota(jnp.int32, (Q, Q), 1)
    tri = jnp.where((col >= row) if reverse else (col <= row), 1.0, 0.0).astype(BF16)
    rowk = lax.broadcasted_iota(jnp.int32, (Q, HG_KDIM), 0)
    sub3 = lax.broadcasted_iota(jnp.int32, (Q // SUBLANES, SUBLANES, HG_KDIM), 1)

    lb = lb_ref[...]
    a = a_ref[...].astype(F32)
    f = lb + (1.0 - lb) * jax.nn.sigmoid(a)
    logf = jnp.log(jnp.maximum(f, F_FLOOR))
    kk = (1.0 - lb) * jax.nn.sigmoid(-a)
    g_all = sum(jnp.dot(tri, p, preferred_element_type=F32) for p in _split3(logf))
    q_all = q_ref[...].astype(F32)
    q_all = q_all * jax.nn.sigmoid(q_all)
    v_all = v_ref[...].astype(F32)
    tot = 0 if reverse else Q - 1

    for h in range(HG_HEADS):
        sl = slice(h * HG_KDIM, (h + 1) * HG_KDIM)
        g, qh, kh, vh = g_all[:, sl], q_all[:, sl], kk[:, sl], v_all[:, sl]
        vb = vh.astype(BF16)
        g_tot = g[tot:tot + 1, :]
        st = st_ref[h]
        o = _dot_nt((qh * jnp.exp(g)).astype(BF16), st.astype(BF16))
        attn = jnp.zeros((Q, Q), F32)
        for s in HIER_LEVELS:
            half = s // 2
            m_off = half if reverse else half - 1
            gref = jnp.concatenate([jnp.broadcast_to(g[b0 + m_off:b0 + m_off + 1, :], (s, HG_KDIM)) for b0 in range(0, Q, s)], axis=0)
            upper = (rowk % s) >= half
            qmask = jnp.logical_not(upper) if reverse else upper
            eq = jnp.exp(jnp.where(qmask, g - gref, NEG_BIG))
            ek = jnp.exp(jnp.where(qmask, NEG_BIG, gref - g))
            lvl = _dot_nt((qh * eq).astype(BF16), (kh * ek).astype(BF16))
            if s < Q:
                lvl = jnp.where((row // s) == (col // s), lvl, 0.0)
            attn = attn + lvl
        o = o + jnp.dot(attn.astype(BF16), vb, preferred_element_type=F32)
        shp3 = (Q // SUBLANES, SUBLANES, HG_KDIM)
        g3, q3, k3, v3 = g.reshape(shp3), qh.reshape(shp3), kh.reshape(shp3), vb.astype(F32).reshape(shp3)
        o3 = jnp.zeros(shp3, F32)
        for r in range(SUBLANES):
            if r == 0:
                p = q3 * k3
                v_r = v3
            else:
                sh = (SUBLANES - r) if reverse else r
                g_r = pltpu.roll(g3, sh, 1)
                k_r = pltpu.roll(k3, sh, 1)
                v_r = pltpu.roll(v3, sh, 1)
                valid = (sub3 + r < SUBLANES) if reverse else (sub3 >= r)
                p = q3 * k_r * jnp.exp(jnp.where(valid, g3 - g_r, NEG_BIG))
            o3 = o3 + jnp.sum(p, axis=-1, keepdims=True) * v_r
        o = o + o3.reshape(Q, HG_VDIM)
        k_st = (kh * jnp.exp(g_tot - g)).astype(BF16)
        st_ref[h] = st * jnp.exp(g_tot) + jnp.dot(vh.T.astype(BF16), k_st, preferred_element_type=F32)
        if reverse:
            o = o + of_ref[:, sl]
            o = o * lax.rsqrt(jnp.mean(o * o, axis=-1, keepdims=True) + EPS) * w_ref[:, sl]
        o_ref[:, sl] = o.astype(o_ref.dtype)


def _hgrn_scan(q_raw, f_raw, i_raw, lb, norm_w, n_ctx):
    b, t, _ = q_raw.shape
    nc, ncc = t // HG_CHUNK, n_ctx // HG_CHUNK
    blk = (None, HG_CHUNK, HG_QK)
    scratch = [pltpu.VMEM((HG_HEADS, HG_VDIM, HG_KDIM), F32)]
    params = pltpu.CompilerParams(dimension_semantics=("arbitrary", "arbitrary"), vmem_limit_bytes=VMEM_LIMIT)
    row_spec = pl.BlockSpec((1, HG_QK), lambda bi, s: (0, 0))

    def fwd_chunk(s):
        return s

    def bwd_chunk(s):
        return jnp.where(s < ncc, ncc - 1 - s, nc + ncc - 1 - s)

    o_f = pl.pallas_call(
        functools.partial(_gla_kernel, reverse=False),
        grid=(b, nc),
        in_specs=[
            pl.BlockSpec(blk, lambda bi, s: (bi, fwd_chunk(s), 0)),
            pl.BlockSpec(blk, lambda bi, s: (bi, fwd_chunk(s), 0)),
            pl.BlockSpec(blk, lambda bi, s: (bi, fwd_chunk(s), 0)),
            row_spec,
        ],
        out_specs=pl.BlockSpec(blk, lambda bi, s: (bi, fwd_chunk(s), 0)),
        out_shape=jax.ShapeDtypeStruct((b, t, HG_V), F32),
        scratch_shapes=scratch,
        compiler_params=params,
        name="gla_fwd",
    )(q_raw, f_raw, i_raw, lb[0:1])
    return pl.pallas_call(
        functools.partial(_gla_kernel, reverse=True),
        grid=(b, nc),
        in_specs=[
            pl.BlockSpec(blk, lambda bi, s: (bi, bwd_chunk(s), 0)),
            pl.BlockSpec(blk, lambda bi, s: (bi, bwd_chunk(s), 1)),
            pl.BlockSpec(blk, lambda bi, s: (bi, bwd_chunk(s), 0)),
            row_spec,
            pl.BlockSpec(blk, lambda bi, s: (bi, bwd_chunk(s), 0)),
            row_spec,
        ],
        out_specs=pl.BlockSpec(blk, lambda bi, s: (bi, bwd_chunk(s), 0)),
        out_shape=jax.ShapeDtypeStruct((b, t, HG_V), BF16),
        scratch_shapes=scratch,
        compiler_params=params,
        name="gla_bwd",
    )(q_raw, f_raw, i_raw, lb[1:2], o_f, norm_w.reshape(1, HG_V))


def _split_cols(t, sizes):
    return jnp.split(t, np.cumsum(sizes)[:-1].tolist(), axis=-1)


def _rms_norm(x, w):
    xf = x.astype(F32)
    y = xf * lax.rsqrt(jnp.mean(xf * xf, axis=-1, keepdims=True) + EPS)
    return (y * w.astype(F32)).astype(x.dtype)


def _dwconv_centred(x, w, b):
    k = w.shape[0]
    y = lax.conv_general_dilated(x, w[:, None, :].astype(x.dtype), window_strides=(1,), padding=[(k // 2, k // 2)], dimension_numbers=('NWC', 'WIO', 'NWC'), feature_group_count=x.shape[-1], precision=lax.Precision.HIGHEST)
    return y + b.astype(x.dtype)


def _flip_seq(t):
    return jnp.flip(t, axis=1)


def _to_col_major(t, rows):
    b, rest = t.shape[0], t.shape[2:]
    return jnp.swapaxes(t.reshape((b, rows, GRID_W) + rest), 1, 2).reshape((b, rows * GRID_W) + rest)


def _from_col_major(t, rows):
    b, rest = t.shape[0], t.shape[2:]
    return jnp.swapaxes(t.reshape((b, GRID_W, rows) + rest), 1, 2).reshape((b, rows * GRID_W) + rest)


def _segsum_exp(a):
    t = a.shape[-1]
    cs = jnp.cumsum(a, axis=-1)
    mask = jnp.tril(jnp.ones((t, t), dtype=bool))
    diff = jnp.where(mask, cs[..., :, None] - cs[..., None, :], 0.0)
    return jnp.where(mask, jnp.exp(diff), 0.0)


def _ssd_chunked(xs, a, bm, cm, s0):
    b, L, h, p = xs.shape
    g, n = bm.shape[2], bm.shape[3]
    r = h // g
    c = L // SSD_CHUNK
    q = SSD_CHUNK
    X = xs.astype(F32).reshape(b, c, q, g, r, p)
    A = a.astype(F32).reshape(b, c, q, g, r).transpose(0, 3, 4, 1, 2)
    B = bm.astype(F32).reshape(b, c, q, g, n)
    C = cm.astype(F32).reshape(b, c, q, g, n)
    a_cs = jnp.cumsum(A, axis=-1)
    decay_in = _segsum_exp(A)
    cb = jnp.einsum('bclgn,bcsgn->bcgls', C, B)
    y_diag = jnp.einsum('bcgls,bgrcls,bcsgrp->bclgrp', cb, decay_in, X)
    decay_states = jnp.exp(a_cs[..., -1:] - a_cs)
    states = jnp.einsum('bclgn,bgrcl,bclgrp->bcgrpn', B, decay_states, X)
    states = jnp.concatenate([s0.astype(F32).reshape(b, 1, g, r, p, n), states], axis=1)
    chunk_decay = _segsum_exp(jnp.pad(a_cs[..., -1], ((0, 0), (0, 0), (0, 0), (1, 0))))
    states = jnp.einsum('bgrzc,bcgrpn->bzgrpn', chunk_decay, states)
    y_off = jnp.einsum('bclgn,bcgrpn,bgrcl->bclgrp', C, states[:, :-1], jnp.exp(a_cs))
    y = (y_diag + y_off).reshape(b, L, h, p)
    return y, states[:, -1].reshape(b, h, p, n)


def _mamba_core(xbc, dt_raw, conv_w, conv_b, dt_bias, a_log, d_skip, s_f, s_b):
    b, L, _ = xbc.shape
    xbc = jax.nn.silu(_dwconv_centred(xbc, conv_w, conv_b))
    xs, bm, cm = _split_cols(xbc, (SSM_INNER, SSM_GROUPS * SSM_STATE, SSM_GROUPS * SSM_STATE))
    xs = xs.reshape(b, L, SSM_HEADS, SSM_HEAD_DIM)
    bm = bm.reshape(b, L, SSM_GROUPS, SSM_STATE)
    cm = cm.reshape(b, L, SSM_GROUPS, SSM_STATE)
    dt = jax.nn.softplus((dt_raw + dt_bias).astype(F32))
    a = -jnp.exp(a_log.astype(F32))
    dt_f, dt_b = dt[..., :SSM_HEADS], dt[..., SSM_HEADS:]
    y_f, s_f = _ssd_chunked(xs * dt_f[..., None], dt_f * a[0], bm, cm, s_f)
    y_b, s_b = _ssd_chunked(_flip_seq(xs * dt_b[..., None]), _flip_seq(dt_b * a[1]), _flip_seq(bm), _flip_seq(cm), s_b)
    y = y_f + _flip_seq(y_b) + xs * d_skip[:, None]
    return y.reshape(b, L, SSM_INNER).astype(xbc.dtype), s_f, s_b


def _hyena_filter_spectrum(L, w1, b1, w2, b2, w3, b3, freq):
    hp = lax.Precision.HIGHEST
    t = jnp.linspace(0.0, 1.0, L, dtype=F32)[:, None]
    w = 2.0 * math.pi * jnp.arange(L, dtype=F32)[:, None] / L
    bands = jnp.linspace(1e-4, HY_BANDS - 1, HY_BANDS, dtype=F32)
    feats = jnp.concatenate([t, jnp.cos(bands * w), -jnp.sin(bands * w)], axis=-1)
    h = jnp.sin(freq[0] * (jnp.dot(feats, w1, precision=hp) + b1))
    h = jnp.sin(freq[1] * (jnp.dot(h, w2, precision=hp) + b2))
    h = (jnp.dot(h, w3, precision=hp) + b3).astype(F32).reshape(L, 2, HY_ORDER, HY_WIDTH)
    max_decay = math.log(HY_DECAY_TARGET) / HY_FAST_DECAY
    min_decay = math.log(HY_DECAY_TARGET) / HY_SLOW_DECAY
    deltas = jnp.abs(jnp.linspace(min_decay, max_decay, HY_WIDTH, dtype=F32))
    h = h * jnp.exp(-t * deltas)[:, None, None, :]
    fwd, bwd = h[:, 0], h[:, 1]
    k = jnp.concatenate([fwd, jnp.zeros_like(fwd[:1]), jnp.flip(bwd[1:], axis=0)], axis=0)
    k = k * lax.rsqrt(jnp.sum(k * k, axis=0, keepdims=True) + EPS)
    return jnp.fft.rfft(k, axis=0)


def _long_conv(u, k_spec, bias):
    L = u.shape[1]
    uf = u.astype(F32)
    y = jnp.fft.irfft(jnp.fft.rfft(uf, n=2 * L, axis=1) * k_spec[None], n=2 * L, axis=1)[:, :L]
    return (y + uf * bias.astype(F32)).astype(u.dtype)


def _hyena_core(proj, conv_w, conv_b, w1, b1, w2, b2, w3, b3, freq, bias):
    L = proj.shape[1]
    proj = _dwconv_centred(proj, conv_w, conv_b)
    v, x1, x2 = jnp.split(proj, HY_ORDER + 1, axis=-1)
    k_spec = _hyena_filter_spectrum(L, w1, b1, w2, b2, w3, b3, freq)
    z = v
    for o, gate in enumerate((x1, x2)):
        z = gate * _long_conv(z, k_spec[:, o], bias[o])
    return z


def _branch_merge(ym, yh, yg, gate_pre, w_br_ssm, w_br_hy, w_br_hg, w_out):
    shp = ym.shape
    g_m, g_h, g_g = jnp.split(jax.nn.sigmoid(gate_pre), N_BRANCHES, axis=-1)

    def proj(y, w):
        return _mm(y.reshape(-1, y.shape[-1]).astype(BF16), w.astype(BF16)).reshape(shp[:-1] + (D_MODEL,))

    merged = g_m * proj(ym, w_br_ssm) + g_h * proj(yh, w_br_hy) + g_g * proj(yg, w_br_hg)
    return proj(merged, w_out)


def _proj(hb, wseg, out_dtype=F32):
    n = wseg.shape[1]
    n_pad = -(-n // LANES) * LANES
    wb = jnp.pad(wseg, ((0, 0), (0, n_pad - n))).astype(BF16)
    return _mm(hb, wb, out_dtype)[:, :n]


def _hybrid_mixer(hc, hx, w_in, lb, ssm_conv_w, ssm_conv_b, ssm_dt_bias, ssm_a_log, ssm_d, ssm_norm, hy_conv_w, hy_conv_b, hy_w1, hy_b1, hy_w2, hy_b2, hy_w3, hy_b3, hy_freq, hy_bias, hg_norm, w_br_ssm, w_br_hy, w_br_hg, w_out):
    b, n_ctx, d = hc.shape
    n_lat = hx.shape[1]
    rows = n_lat // GRID_W
    w_z, w_xbc, w_dt, w_hy, w_q, w_f, w_i, w_g, w_gate = _split_cols(w_in, IN_SIZES)
    hc2 = hc.reshape(b * n_ctx, d).astype(BF16)
    hx2 = hx.reshape(b * n_lat, d).astype(BF16)

    def proj_pair(w):
        return _proj(hc2, w).reshape(b, n_ctx, -1), _proj(hx2, w).reshape(b, n_lat, -1)

    zc, zx = proj_pair(w_z)
    xbc_c, xbc_x = proj_pair(w_xbc)
    dt_c, dt_x = proj_pair(w_dt)
    hy_c, hy_x = proj_pair(w_hy)
    g_c, g_x = proj_pair(w_g)
    gate_c, gate_x = proj_pair(w_gate)

    s0 = jnp.zeros((b, SSM_HEADS, SSM_HEAD_DIM, SSM_STATE), F32)
    ym_c, sf, sb = _mamba_core(xbc_c, dt_c, ssm_conv_w, ssm_conv_b, ssm_dt_bias, ssm_a_log, ssm_d, s0, s0)
    ym_x, _, _ = _mamba_core(xbc_x, dt_x, ssm_conv_w, ssm_conv_b, ssm_dt_bias, ssm_a_log, ssm_d, sf, sb)
    ym_c = _rms_norm(ym_c * jax.nn.silu(zc), ssm_norm)
    ym_x = _rms_norm(ym_x * jax.nn.silu(zx), ssm_norm)

    yh_c = _hyena_core(hy_c, hy_conv_w, hy_conv_b, hy_w1, hy_b1, hy_w2, hy_b2, hy_w3, hy_b3, hy_freq, hy_bias)
    yh_x = _hyena_core(hy_x, hy_conv_w, hy_conv_b, hy_w1, hy_b1, hy_w2, hy_b2, hy_w3, hy_b3, hy_freq, hy_bias)

    t = n_ctx + n_lat
    hs2 = jnp.concatenate([hc, _to_col_major(hx, rows)], axis=1).astype(BF16).reshape(b * t, d)
    q_s = _proj(hs2, w_q, BF16).reshape(b, t, HG_QK)
    f_s = _proj(hs2, w_f, BF16).reshape(b, t, 2 * HG_QK)
    i_s = _proj(hs2, w_i, BF16).reshape(b, t, HG_V)
    og = _hgrn_scan(q_s, f_s, i_s, lb, hg_norm, n_ctx).astype(F32)
    yg_c = og[:, :n_ctx] * jax.nn.silu(g_c)
    yg_x = _from_col_major(og[:, n_ctx:], rows) * jax.nn.silu(g_x)

    out_c = _branch_merge(ym_c, yh_c, yg_c, gate_c, w_br_ssm, w_br_hy, w_br_hg, w_out)
    out_x = _branch_merge(ym_x, yh_x, yg_x, gate_x, w_br_ssm, w_br_hy, w_br_hg, w_out)
    return out_c, out_x


def _moe_ffn(h, router_w, router_b, w1, b1, w2, b2):
    t, d = h.shape
    logits = (jnp.dot(h, router_w, precision=lax.Precision.HIGHEST) + router_b).astype(F32)
    top_v, top_e = lax.top_k(logits, TOP_K)
    gate_w = jax.nn.softmax(top_v, axis=-1)
    n = t * TOP_K
    flat_e = top_e.reshape(n)
    order = jnp.argsort(flat_e)
    se = flat_e[order]
    stok = (order // TOP_K).astype(jnp.int32)
    sw = gate_w.reshape(n)[order]
    counts = jnp.bincount(flat_e, length=N_EXPERTS)
    padded = (counts + MOE_BLOCK - 1) // MOE_BLOCK * MOE_BLOCK
    start = jnp.cumsum(counts) - counts
    pend = jnp.cumsum(padded)
    pstart = pend - padded
    dest = pstart[se] + jnp.arange(n, dtype=jnp.int32) - start[se]
    n_blocks = -(-n // MOE_BLOCK) + N_EXPERTS
    cap = n_blocks * MOE_BLOCK
    slot_tok = jnp.full((cap,), t, jnp.int32).at[dest].set(stok)
    slot_w = jnp.zeros((cap,), F32).at[dest].set(sw)
    block_e = jnp.minimum(jnp.searchsorted(pend, jnp.arange(n_blocks, dtype=jnp.int32) * MOE_BLOCK, side='right'), N_EXPERTS - 1).astype(jnp.int32)
    xb = jnp.concatenate([h.astype(BF16), jnp.zeros((1, d), BF16)], axis=0)[slot_tok]
    yb = _moe_experts(xb, block_e, slot_w, w1.astype(BF16), b1, w2.astype(BF16), b2)
    y = jnp.zeros((t + 1, d), h.dtype).at[slot_tok].add(yb.astype(h.dtype))
    return y[:t]


def kernel(x, c, ctx, c_ctx, w_mod, b_mod, norm_mix_pre, norm_mix_post, norm_ffn_pre, norm_ffn_post, w_in, ssm_conv_w, ssm_conv_b, ssm_dt_bias, ssm_a_log, ssm_d, ssm_norm, hy_conv_w, hy_conv_b, hy_w1, hy_b1, hy_w2, hy_b2, hy_w3, hy_b3, hy_freq, hy_bias, hg_lb_logits, hg_norm, w_br_ssm, w_br_hy, w_br_hg, w_out, router_w, router_b, exp_w1, exp_b1, exp_w2, exp_b2):
    hp = lax.Precision.HIGHEST
    lb = jax.nn.softmax(hg_lb_logits.astype(F32), axis=1)
    lb = jnp.cumsum(lb, axis=1) - lb[:, :1]
    silu_c = jax.nn.silu(c)
    silu_cc = jax.nn.silu(c_ctx)
    for li in range(DEPTH):
        mx = jnp.split((jnp.dot(silu_c, w_mod[li], precision=hp) + b_mod[li])[:, None, :], 6, axis=-1)
        mc = jnp.split((jnp.dot(silu_cc, w_mod[li], precision=hp) + b_mod[li])[None, None, :], 6, axis=-1)
        hx = _rms_norm(x, norm_mix_pre[li]) * (1.0 + mx[1]) + mx[0]
        hc = _rms_norm(ctx, norm_mix_pre[li]) * (1.0 + mc[1]) + mc[0]
        mix_c, mix_x = _hybrid_mixer(hc, hx, w_in[li], lb[:, li],ssm_conv_w[li], ssm_conv_b[li], ssm_dt_bias[li], ssm_a_log[li], ssm_d[li], ssm_norm[li], hy_conv_w[li], hy_conv_b[li], hy_w1[li], hy_b1[li], hy_w2[li], hy_b2[li], hy_w3[li], hy_b3[li], hy_freq[li], hy_bias[li], hg_norm[li], w_br_ssm[li], w_br_hy[li], w_br_hg[li], w_out[li])
        x = x + mx[2] * _rms_norm(mix_x, norm_mix_post[li])
        hx = _rms_norm(x, norm_ffn_pre[li]) * (1.0 + mx[4]) + mx[3]
        if li < DEPTH - 1:
            ctx = ctx + mc[2] * _rms_norm(mix_c, norm_mix_post[li])
            hc = _rms_norm(ctx, norm_ffn_pre[li]) * (1.0 + mc[4]) + mc[3]
            n_ctx_tok = hc.shape[0] * hc.shape[1]
            f = _moe_ffn(jnp.concatenate([hc.reshape(-1, D_MODEL), hx.reshape(-1, D_MODEL)], axis=0), router_w[li], router_b[li], exp_w1[li], exp_b1[li], exp_w2[li], exp_b2[li])
            ctx = ctx + mc[5] * _rms_norm(f[:n_ctx_tok].reshape(ctx.shape), norm_ffn_post[li])
            fx = f[n_ctx_tok:]
        else:
            fx = _moe_ffn(hx.reshape(-1, D_MODEL), router_w[li], router_b[li], exp_w1[li], exp_b1[li], exp_w2[li], exp_b2[li])
        x = x + mx[5] * _rms_norm(fx.reshape(x.shape), norm_ffn_post[li])
    return x
```

```python
import functools
import math

import jax
import jax.numpy as jnp
import numpy as np
from jax import lax
from jax.experimental import pallas as pl
from jax.experimental.pallas import tpu as pltpu

D_MODEL = 1024
DEPTH = 2
GRID_W = 64

SSM_HEADS = 16
SSM_HEAD_DIM = 64
SSM_INNER = SSM_HEADS * SSM_HEAD_DIM
SSM_STATE = 128
SSM_GROUPS = 4
SSD_CHUNK = 128
SSM_XBC = SSM_INNER + 2 * SSM_GROUPS * SSM_STATE

HY_WIDTH = D_MODEL
HY_ORDER = 2
HY_BANDS = 16
HY_FAST_DECAY = 0.3
HY_SLOW_DECAY = 1.5
HY_DECAY_TARGET = 1e-2

HG_HEADS = 8
HG_KDIM = 128
HG_VDIM = D_MODEL // HG_HEADS
HG_QK = HG_HEADS * HG_KDIM
HG_V = HG_HEADS * HG_VDIM
HG_CHUNK = 64
F_FLOOR = 1e-20

N_EXPERTS = 32
TOP_K = 4
D_FF = D_MODEL
SWIGLU_LIMIT = 7.0
SWIGLU_ALPHA = 1.702
MOE_BLOCK = 256

N_BRANCHES = 3
IN_SIZES = (SSM_INNER, SSM_XBC, 2 * SSM_HEADS, (HY_ORDER + 1) * HY_WIDTH, HG_QK, 2 * HG_QK, HG_V, HG_V, N_BRANCHES * D_MODEL)
EPS = 1e-6
F32 = jnp.float32
BF16 = jnp.bfloat16

LANES = 128
VMEM_LIMIT = 56 * 1024 * 1024


def _mm_kernel(a_ref, b_ref, o_ref):
    o_ref[...] = jnp.dot(a_ref[...], b_ref[...], preferred_element_type=F32).astype(o_ref.dtype)


def _mm(a, b, out_dtype=F32, tm=512, tn=512):
    m, k = a.shape
    n = b.shape[1]
    tm = min(tm, m)
    tn = min(tn, n)
    assert m % tm == 0 and n % tn == 0, (m, n, tm, tn)
    return pl.pallas_call(
        _mm_kernel,
        grid=(n // tn, m // tm),
        in_specs=[pl.BlockSpec((tm, k), lambda j, i: (i, 0)), pl.BlockSpec((k, tn), lambda j, i: (0, j))],
        out_specs=pl.BlockSpec((tm, tn), lambda j, i: (i, j)),
        out_shape=jax.ShapeDtypeStruct((m, n), out_dtype),
        compiler_params=pltpu.CompilerParams(dimension_semantics=("arbitrary", "arbitrary"), vmem_limit_bytes=VMEM_LIMIT),
        name="dense_mm",
    )(a, b)


def _moe_kernel(tile_ref, exp_ref, lo_ref, hi_ref, first_ref, x_ref, sw_ref, w1_ref, b1_ref, w2_ref, b2_ref, o_ref):
    del exp_ref
    w = pl.program_id(0)
    lo, hi = lo_ref[w], hi_ref[w]

    @pl.when(hi > lo)
    def _():
        hh = jnp.dot(x_ref[...], w1_ref[...], preferred_element_type=F32) + b1_ref[...]
        g = jnp.minimum(hh[:, :D_FF], SWIGLU_LIMIT)
        u = jnp.clip(hh[:, D_FF:], -SWIGLU_LIMIT, SWIGLU_LIMIT)
        act = (u + 1.0) * g * jax.nn.sigmoid(SWIGLU_ALPHA * g)
        y = jnp.dot(act.astype(BF16), w2_ref[...], preferred_element_type=F32) + b2_ref[...]
        y = (y * sw_ref[...]).astype(o_ref.dtype)
        rows = tile_ref[w] * MOE_BLOCK + lax.broadcasted_iota(jnp.int32, (MOE_BLOCK, 1), 0)
        mine = jnp.logical_and(rows >= lo, rows < hi)

        @pl.when(first_ref[w] == 1)
        def _():
            o_ref[...] = jnp.where(mine, y, jnp.zeros_like(y))

        @pl.when(first_ref[w] != 1)
        def _():
            o_ref[...] = jnp.where(mine, y, o_ref[...])


def _moe_experts(xs, sw, tile_w, exp_w, lo, hi, first, w1, b1, w2, b2):
    n, d = xs.shape
    grid_spec = pltpu.PrefetchScalarGridSpec(
        num_scalar_prefetch=5,
        grid=(tile_w.shape[0],),
        in_specs=[
            pl.BlockSpec((MOE_BLOCK, d), lambda w, tl, ex, lo_, hi_, fi: (tl[w], 0)),
            pl.BlockSpec((MOE_BLOCK, 1), lambda w, tl, ex, lo_, hi_, fi: (tl[w], 0)),
            pl.BlockSpec((None, d, 2 * D_FF), lambda w, tl, ex, lo_, hi_, fi: (ex[w], 0, 0)),
            pl.BlockSpec((None, 1, 2 * D_FF), lambda w, tl, ex, lo_, hi_, fi: (ex[w], 0, 0)),
            pl.BlockSpec((None, D_FF, d), lambda w, tl, ex, lo_, hi_, fi: (ex[w], 0, 0)),
            pl.BlockSpec((None, 1, d), lambda w, tl, ex, lo_, hi_, fi: (ex[w], 0, 0)),
        ],
        out_specs=pl.BlockSpec((MOE_BLOCK, d), lambda w, tl, ex, lo_, hi_, fi: (tl[w], 0)),
    )
    return pl.pallas_call(
        _moe_kernel,
        grid_spec=grid_spec,
        out_shape=jax.ShapeDtypeStruct((n, d), BF16),
        compiler_params=pltpu.CompilerParams(dimension_semantics=("arbitrary",), vmem_limit_bytes=VMEM_LIMIT),
        name="moe_experts",
    )(tile_w, exp_w, lo, hi, first, xs, sw.reshape(n, 1), w1, b1.reshape(N_EXPERTS, 1, 2 * D_FF), w2, b2.reshape(N_EXPERTS, 1, d))


SUBLANES = 8
NEG_BIG = -1e30
HIER_LEVELS = (64, 32, 16)


def _split3(x):
    h1 = x.astype(BF16)
    r1 = x - h1.astype(F32)
    h2 = r1.astype(BF16)
    h3 = (r1 - h2.astype(F32)).astype(BF16)
    return h1, h2, h3


def _dot_nt(a, b):
    return lax.dot_general(a, b, (((1,), (1,)), ((), ())), preferred_element_type=F32)


def _gla_kernel(q_ref, a_ref, v_ref, lb_ref, *rest, reverse):
    if reverse:
        of_ref, w_ref, o_ref, st_ref = rest
    else:
        o_ref, st_ref = rest
    Q = HG_CHUNK

    @pl.when(pl.program_id(1) == 0)
    def _():
        st_ref[...] = jnp.zeros_like(st_ref)

    row = lax.broadcasted_iota(jnp.int32, (Q, Q), 0)
    col = lax.broadcasted_iota(jnp.int32, (Q, Q), 1)
    tri = jnp.where((col >= row) if reverse else (col <= row), 1.0, 0.0).astype(BF16)
    rowk = lax.broadcasted_iota(jnp.int32, (Q, HG_KDIM), 0)
    sub3 = lax.broadcasted_iota(jnp.int32, (Q // SUBLANES, SUBLANES, HG_KDIM), 1)

    lb = lb_ref[...]
    a = a_ref[...].astype(F32)
    f = lb + (1.0 - lb) * jax.nn.sigmoid(a)
    logf = jnp.log(jnp.maximum(f, F_FLOOR))
    kk = (1.0 - lb) * jax.nn.sigmoid(-a)
    g_all = sum(jnp.dot(tri, p, preferred_element_type=F32) for p in _split3(logf))
    q_all = q_ref[...].astype(F32)
    q_all = q_all * jax.nn.sigmoid(q_all)
    v_all = v_ref[...].astype(F32)
    tot = 0 if reverse else Q - 1

    for h in range(HG_HEADS):
        sl = slice(h * HG_KDIM, (h + 1) * HG_KDIM)
        g, qh, kh, vh = g_all[:, sl], q_all[:, sl], kk[:, sl], v_all[:, sl]
        vb = vh.astype(BF16)
        g_tot = g[tot:tot + 1, :]
        st = st_ref[h]
        o = _dot_nt((qh * jnp.exp(g)).astype(BF16), st.astype(BF16))
        attn = jnp.zeros((Q, Q), F32)
        for s in HIER_LEVELS:
            half = s // 2
            m_off = half if reverse else half - 1
            gref = jnp.concatenate([jnp.broadcast_to(g[b0 + m_off:b0 + m_off + 1, :], (s, HG_KDIM)) for b0 in range(0, Q, s)], axis=0)
            upper = (rowk % s) >= half
            qmask = jnp.logical_not(upper) if reverse else upper
            eq = jnp.exp(jnp.where(qmask, g - gref, NEG_BIG))
            ek = jnp.exp(jnp.where(qmask, NEG_BIG, gref - g))
            lvl = _dot_nt((qh * eq).astype(BF16), (kh * ek).astype(BF16))
            if s < Q:
                lvl = jnp.where((row // s) == (col // s), lvl, 0.0)
            attn = attn + lvl
        o = o + jnp.dot(attn.astype(BF16), vb, preferred_element_type=F32)
        shp3 = (Q // SUBLANES, SUBLANES, HG_KDIM)
        g3, q3, k3, v3 = g.reshape(shp3), qh.reshape(shp3), kh.reshape(shp3), vb.astype(F32).reshape(shp3)
        o3 = jnp.zeros(shp3, F32)
        for r in range(SUBLANES):
            if r == 0:
                p = q3 * k3
                v_r = v3
            else:
                sh = (SUBLANES - r) if reverse else r
                g_r = pltpu.roll(g3, sh, 1)
                k_r = pltpu.roll(k3, sh, 1)
                v_r = pltpu.roll(v3, sh, 1)
                valid = (sub3 + r < SUBLANES) if reverse else (sub3 >= r)
                p = q3 * k_r * jnp.exp(jnp.where(valid, g3 - g_r, NEG_BIG))
            o3 = o3 + jnp.sum(p, axis=-1, keepdims=True) * v_r
        o = o + o3.reshape(Q, HG_VDIM)
        k_st = (kh * jnp.exp(g_tot - g)).astype(BF16)
        st_ref[h] = st * jnp.exp(g_tot) + jnp.dot(vh.T.astype(BF16), k_st, preferred_element_type=F32)
        if reverse:
            o = o + of_ref[:, sl]
            o = o * lax.rsqrt(jnp.mean(o * o, axis=-1, keepdims=True) + EPS) * w_ref[:, sl]
        o_ref[:, sl] = o.astype(o_ref.dtype)


def _hgrn_scan(q_raw, f_raw, i_raw, lb, norm_w, n_ctx):
    b, t, _ = q_raw.shape
    nc, ncc = t // HG_CHUNK, n_ctx // HG_CHUNK
    blk = (None, HG_CHUNK, HG_QK)
    scratch = [pltpu.VMEM((HG_HEADS, HG_VDIM, HG_KDIM), F32)]
    params = pltpu.CompilerParams(dimension_semantics=("arbitrary", "arbitrary"), vmem_limit_bytes=VMEM_LIMIT)
    row_spec = pl.BlockSpec((1, HG_QK), lambda bi, s: (0, 0))

    def fwd_chunk(s):
        return s

    def bwd_chunk(s):
        return jnp.where(s < ncc, ncc - 1 - s, nc + ncc - 1 - s)

    o_f = pl.pallas_call(
        functools.partial(_gla_kernel, reverse=False),
        grid=(b, nc),
        in_specs=[
            pl.BlockSpec(blk, lambda bi, s: (bi, fwd_chunk(s), 0)),
            pl.BlockSpec(blk, lambda bi, s: (bi, fwd_chunk(s), 0)),
            pl.BlockSpec(blk, lambda bi, s: (bi, fwd_chunk(s), 0)),
            row_spec,
        ],
        out_specs=pl.BlockSpec(blk, lambda bi, s: (bi, fwd_chunk(s), 0)),
        out_shape=jax.ShapeDtypeStruct((b, t, HG_V), F32),
        scratch_shapes=scratch,
        compiler_params=params,
        name="gla_fwd",
    )(q_raw, f_raw, i_raw, lb[0:1])
    return pl.pallas_call(
        functools.partial(_gla_kernel, reverse=True),
        grid=(b, nc),
        in_specs=[
            pl.BlockSpec(blk, lambda bi, s: (bi, bwd_chunk(s), 0)),
            pl.BlockSpec(blk, lambda bi, s: (bi, bwd_chunk(s), 1)),
            pl.BlockSpec(blk, lambda bi, s: (bi, bwd_chunk(s), 0)),
            row_spec,
            pl.BlockSpec(blk, lambda bi, s: (bi, bwd_chunk(s), 0)),
            row_spec,
        ],
        out_specs=pl.BlockSpec(blk, lambda bi, s: (bi, bwd_chunk(s), 0)),
        out_shape=jax.ShapeDtypeStruct((b, t, HG_V), BF16),
        scratch_shapes=scratch,
        compiler_params=params,
        name="gla_bwd",
    )(q_raw, f_raw, i_raw, lb[1:2], o_f, norm_w.reshape(1, HG_V))


CONV_TILE = 256
HALO = 16


def _dwconv_kernel(prev_ref, cur_ref, next_ref, w_ref, b_ref, o_ref, *, taps, n_ctx_tiles, n_tiles, silu):
    i = pl.program_id(1)
    first = jnp.logical_or(i == 0, i == n_ctx_tiles)
    last = jnp.logical_or(i == n_ctx_tiles - 1, i == n_tiles - 1)
    pad = taps // 2
    xp = jnp.where(first, 0.0, prev_ref[...].astype(F32))
    xn = jnp.where(last, 0.0, next_ref[...].astype(F32))
    xcat = jnp.concatenate([xp, cur_ref[...].astype(F32), xn], axis=0)
    acc = jnp.broadcast_to(b_ref[...], cur_ref.shape).astype(F32)
    for k in range(taps):
        off = HALO - pad + k
        acc = acc + w_ref[k:k + 1, :] * xcat[off:off + CONV_TILE, :]
    if silu:
        acc = acc * jax.nn.sigmoid(acc)
    o_ref[...] = acc.astype(o_ref.dtype)


def _dwconv_stream(x, w, bias, n_ctx, silu, ct=1024):
    b, t, c = x.shape
    taps = w.shape[0]
    n_tiles = t // CONV_TILE
    hb = CONV_TILE // HALO
    n_halo = t // HALO
    kern = functools.partial(_dwconv_kernel, taps=taps, n_ctx_tiles=n_ctx // CONV_TILE, n_tiles=n_tiles, silu=silu)
    return pl.pallas_call(
        kern,
        grid=(b, n_tiles, c // ct),
        in_specs=[
            pl.BlockSpec((None, HALO, ct), lambda bi, i, j: (bi, jnp.maximum(i * hb - 1, 0), j)),
            pl.BlockSpec((None, CONV_TILE, ct), lambda bi, i, j: (bi, i, j)),
            pl.BlockSpec((None, HALO, ct), lambda bi, i, j: (bi, jnp.minimum((i + 1) * hb, n_halo - 1), j)),
            pl.BlockSpec((taps, ct), lambda bi, i, j: (0, j)),
            pl.BlockSpec((1, ct), lambda bi, i, j: (0, j)),
        ],
        out_specs=pl.BlockSpec((None, CONV_TILE, ct), lambda bi, i, j: (bi, i, j)),
        out_shape=jax.ShapeDtypeStruct((b, t, c), BF16),
        compiler_params=pltpu.CompilerParams(dimension_semantics=("arbitrary",) * 3, vmem_limit_bytes=VMEM_LIMIT),
        name="dwconv",
    )(x, x, x, w.astype(F32), bias.reshape(1, c).astype(F32))


SSM_GHEADS = SSM_HEADS // SSM_GROUPS
SSM_GP = SSM_GHEADS * SSM_HEAD_DIM


def _ssd_kernel(xbc_ref, dt_ref, dtb_ref, a_ref, *rest, reverse):
    if reverse:
        yf_ref, z_ref, dsk_ref, nw_ref, o_ref, st_ref = rest
    else:
        o_ref, st_ref = rest
    Q = SSD_CHUNK

    @pl.when(pl.program_id(1) == 0)
    def _():
        st_ref[...] = jnp.zeros_like(st_ref)

    row = lax.broadcasted_iota(jnp.int32, (Q, Q), 0)
    col = lax.broadcasted_iota(jnp.int32, (Q, Q), 1)
    keep = (col >= row) if reverse else (col <= row)
    tri = jnp.where(keep, 1.0, 0.0).astype(BF16)
    expand = jnp.where(lax.broadcasted_iota(jnp.int32, (LANES, SSM_INNER), 1) // SSM_HEAD_DIM == lax.broadcasted_iota(jnp.int32, (LANES, SSM_INNER), 0), 1.0, 0.0).astype(BF16)

    dt = jax.nn.softplus(dt_ref[...] + dtb_ref[...])
    a = dt * a_ref[...]
    cs = sum(jnp.dot(tri, p, preferred_element_type=F32) for p in _split3(a))
    tot = 0 if reverse else Q - 1
    cs_tot = cs[tot:tot + 1, :]
    cs_t = cs.T
    dt_e = jnp.dot(dt.astype(BF16), expand, preferred_element_type=F32)
    e_in = jnp.dot(jnp.exp(cs).astype(BF16), expand, preferred_element_type=F32)
    e_st = jnp.dot(jnp.exp(cs_tot - cs).astype(BF16), expand, preferred_element_type=F32)
    e_tot = jnp.dot(jnp.broadcast_to(jnp.exp(cs_tot), (SUBLANES, LANES)).astype(BF16), expand, preferred_element_type=F32)[0:1, :]

    xs = xbc_ref[:, :SSM_INNER].astype(F32)
    xd = (xs * dt_e).astype(BF16)
    xst = (xs * dt_e * e_st).astype(BF16)
    lane_lo = lax.broadcasted_iota(jnp.int32, (Q, LANES), 1) < SSM_HEAD_DIM
    ys = []
    for g in range(SSM_GROUPS):
        bm = xbc_ref[:, SSM_INNER + g * SSM_STATE:SSM_INNER + (g + 1) * SSM_STATE]
        cm = xbc_ref[:, SSM_INNER + (SSM_GROUPS + g) * SSM_STATE:SSM_INNER + (SSM_GROUPS + g + 1) * SSM_STATE]
        cb = _dot_nt(cm, bm)
        gl = slice(g * SSM_GP, (g + 1) * SSM_GP)
        st = st_ref[g]
        y_off = jnp.dot(cm, st.astype(BF16), preferred_element_type=F32) * e_in[:, gl]
        pieces = []
        for hp in range(SSM_GHEADS // 2):
            pair = []
            for h in (g * SSM_GHEADS + 2 * hp, g * SSM_GHEADS + 2 * hp + 1):
                diff = jnp.broadcast_to(cs[:, h:h + 1], (Q, Q)) - jnp.broadcast_to(cs_t[h:h + 1, :], (Q, Q))
                m = (cb * jnp.exp(jnp.where(keep, diff, NEG_BIG))).astype(BF16)
                pair.append(jnp.dot(m, xd[:, (h // 2) * LANES:(h // 2 + 1) * LANES], preferred_element_type=F32))
            pieces.append(jnp.where(lane_lo, pair[0], pair[1]))
        ys.append(jnp.concatenate(pieces, axis=1) + y_off)
        st_ref[g] = st * e_tot[:, gl] + jnp.dot(bm.astype(F32).T.astype(BF16), xst[:, gl], preferred_element_type=F32)
    y = jnp.concatenate(ys, axis=1)
    if reverse:
        y = (y + yf_ref[...] + xs * dsk_ref[...])
        zz = z_ref[...].astype(F32)
        y = y * (zz * jax.nn.sigmoid(zz))
        y = y * lax.rsqrt(jnp.mean(y * y, axis=-1, keepdims=True) + EPS) * nw_ref[...]
    o_ref[...] = y.astype(o_ref.dtype)


def _ssd_scan(xbc_act, dt2, z, dt_bias, a_log, d_skip, norm_w, n_ctx):
    b, t, _ = xbc_act.shape
    nc, ncc = t // SSD_CHUNK, n_ctx // SSD_CHUNK

    def pad_heads(v):
        return jnp.pad(v.astype(F32), ((0, 0), (0, LANES - SSM_HEADS)))

    dtb = pad_heads(dt_bias.reshape(2, SSM_HEADS))
    a_neg = pad_heads(-jnp.exp(a_log.astype(F32)))
    dsk = jnp.repeat(d_skip.astype(F32), SSM_HEAD_DIM).reshape(1, SSM_INNER)
    scratch = [pltpu.VMEM((SSM_GROUPS, SSM_STATE, SSM_GP), F32)]
    params = pltpu.CompilerParams(dimension_semantics=("arbitrary", "arbitrary"), vmem_limit_bytes=VMEM_LIMIT)

    def bwd_chunk(s):
        return jnp.where(s < ncc, ncc - 1 - s, nc + ncc - 1 - s)

    def specs(chunk, d):
        return [
            pl.BlockSpec((None, SSD_CHUNK, SSM_XBC), lambda bi, s: (bi, chunk(s), 0)),
            pl.BlockSpec((None, SSD_CHUNK, LANES), lambda bi, s: (bi, chunk(s), d)),
            pl.BlockSpec((1, LANES), lambda bi, s: (0, 0)),
            pl.BlockSpec((1, LANES), lambda bi, s: (0, 0)),
        ]

    def inner_spec(chunk):
        return pl.BlockSpec((None, SSD_CHUNK, SSM_INNER), lambda bi, s: (bi, chunk(s), 0))

    row_spec = pl.BlockSpec((1, SSM_INNER), lambda bi, s: (0, 0))
    y_f = pl.pallas_call(
        functools.partial(_ssd_kernel, reverse=False),
        grid=(b, nc),
        in_specs=specs(lambda s: s, 0),
        out_specs=inner_spec(lambda s: s),
        out_shape=jax.ShapeDtypeStruct((b, t, SSM_INNER), F32),
        scratch_shapes=scratch,
        compiler_params=params,
        name="ssd_fwd",
    )(xbc_act, dt2, dtb[0:1], a_neg[0:1])
    return pl.pallas_call(
        functools.partial(_ssd_kernel, reverse=True),
        grid=(b, nc),
        in_specs=specs(bwd_chunk, 1) + [inner_spec(bwd_chunk), inner_spec(bwd_chunk), row_spec, row_spec],
        out_specs=inner_spec(bwd_chunk),
        out_shape=jax.ShapeDtypeStruct((b, t, SSM_INNER), BF16),
        scratch_shapes=scratch,
        compiler_params=params,
        name="ssd_bwd",
    )(xbc_act, dt2, dtb[1:2], a_neg[1:2], y_f, z, dsk, norm_w.reshape(1, SSM_INNER).astype(F32))


def _split_cols(t, sizes):
    return jnp.split(t, np.cumsum(sizes)[:-1].tolist(), axis=-1)


def _rms_norm(x, w):
    xf = x.astype(F32)
    y = xf * lax.rsqrt(jnp.mean(xf * xf, axis=-1, keepdims=True) + EPS)
    return (y * w.astype(F32)).astype(x.dtype)


def _dwconv_centred(x, w, b):
    k = w.shape[0]
    y = lax.conv_general_dilated(x, w[:, None, :].astype(x.dtype), window_strides=(1,), padding=[(k // 2, k // 2)], dimension_numbers=('NWC', 'WIO', 'NWC'), feature_group_count=x.shape[-1], precision=lax.Precision.HIGHEST)
    return y + b.astype(x.dtype)


def _flip_seq(t):
    return jnp.flip(t, axis=1)


def _to_col_major(t, rows):
    b, rest = t.shape[0], t.shape[2:]
    return jnp.swapaxes(t.reshape((b, rows, GRID_W) + rest), 1, 2).reshape((b, rows * GRID_W) + rest)


def _from_col_major(t, rows):
    b, rest = t.shape[0], t.shape[2:]
    return jnp.swapaxes(t.reshape((b, GRID_W, rows) + rest), 1, 2).reshape((b, rows * GRID_W) + rest)


def _hyena_filter_spectrum(L, w1, b1, w2, b2, w3, b3, freq):
    hp = lax.Precision.HIGHEST
    t = jnp.linspace(0.0, 1.0, L, dtype=F32)[:, None]
    w = 2.0 * math.pi * jnp.arange(L, dtype=F32)[:, None] / L
    bands = jnp.linspace(1e-4, HY_BANDS - 1, HY_BANDS, dtype=F32)
    feats = jnp.concatenate([t, jnp.cos(bands * w), -jnp.sin(bands * w)], axis=-1)
    h = jnp.sin(freq[0] * (jnp.dot(feats, w1, precision=hp) + b1))
    h = jnp.sin(freq[1] * (jnp.dot(h, w2, precision=hp) + b2))
    h = (jnp.dot(h, w3, precision=hp) + b3).astype(F32).reshape(L, 2, HY_ORDER, HY_WIDTH)
    max_decay = math.log(HY_DECAY_TARGET) / HY_FAST_DECAY
    min_decay = math.log(HY_DECAY_TARGET) / HY_SLOW_DECAY
    deltas = jnp.abs(jnp.linspace(min_decay, max_decay, HY_WIDTH, dtype=F32))
    h = h * jnp.exp(-t * deltas)[:, None, None, :]
    fwd, bwd = h[:, 0], h[:, 1]
    k = jnp.concatenate([fwd, jnp.zeros_like(fwd[:1]), jnp.flip(bwd[1:], axis=0)], axis=0)
    k = k * lax.rsqrt(jnp.sum(k * k, axis=0, keepdims=True) + EPS)
    return jnp.fft.rfft(k, axis=0)


def _long_conv(u, k_spec, bias):
    L = u.shape[1]
    uf = u.astype(F32)
    y = jnp.fft.irfft(jnp.fft.rfft(uf, n=2 * L, axis=1) * k_spec[None], n=2 * L, axis=1)[:, :L]
    return (y + uf * bias.astype(F32)).astype(u.dtype)


def _hyena_core(proj, w1, b1, w2, b2, w3, b3, freq, bias):
    L = proj.shape[1]
    v, x1, x2 = jnp.split(proj.astype(F32), HY_ORDER + 1, axis=-1)
    k_spec = _hyena_filter_spectrum(L, w1, b1, w2, b2, w3, b3, freq)
    z = v
    for o, gate in enumerate((x1, x2)):
        z = gate * _long_conv(z, k_spec[:, o], bias[o])
    return z


def _proj(hb, wseg, out_dtype=BF16):
    return _mm(hb, wseg.astype(BF16), out_dtype)


def _mixer_branches(h_rm, w_in, lb, n_ctx, ssm_conv_w, ssm_conv_b, ssm_dt_bias, ssm_a_log, ssm_d, ssm_norm, hy_conv_w, hy_conv_b, hy_w1, hy_b1, hy_w2, hy_b2, hy_w3, hy_b3, hy_freq, hy_bias, hg_norm):
    b, t, d = h_rm.shape
    n_lat = t - n_ctx
    rows = n_lat // GRID_W
    w_z, w_xbc, w_dt, w_hy, w_q, w_f, w_i, w_g, w_gate = _split_cols(w_in, IN_SIZES)
    h2 = h_rm.reshape(b * t, d)

    def proj(w, dtype=BF16):
        return _proj(h2, w, dtype).reshape(b, t, -1)

    zero_pad = jnp.zeros((d, LANES - SSM_HEADS), F32)
    w_dt2 = jnp.concatenate([w_dt[:, :SSM_HEADS], zero_pad, w_dt[:, SSM_HEADS:], zero_pad], axis=1)
    xbc_act = _dwconv_stream(proj(w_xbc), ssm_conv_w, ssm_conv_b, n_ctx, True)
    ym = _ssd_scan(xbc_act, proj(w_dt2, F32), proj(w_z), ssm_dt_bias, ssm_a_log, ssm_d, ssm_norm, n_ctx)

    hy = _dwconv_stream(proj(w_hy), hy_conv_w, hy_conv_b, n_ctx, False)
    yh = jnp.concatenate([_hyena_core(seq, hy_w1, hy_b1, hy_w2, hy_b2, hy_w3, hy_b3, hy_freq, hy_bias) for seq in (hy[:, :n_ctx], hy[:, n_ctx:])], axis=1).astype(BF16)

    h_cm = jnp.concatenate([h_rm[:, :n_ctx], _to_col_major(h_rm[:, n_ctx:], rows)], axis=1).reshape(b * t, d)

    def proj_cm(w):
        return _proj(h_cm, w).reshape(b, t, -1)

    og = _hgrn_scan(proj_cm(w_q), proj_cm(w_f), proj_cm(w_i), lb, hg_norm, n_ctx)
    og = jnp.concatenate([og[:, :n_ctx], _from_col_major(og[:, n_ctx:], rows)], axis=1)
    return ym, yh, og, proj(w_g), proj(w_gate)


ROW_TILE = 256
MOD_ROWS = SUBLANES
M_SHIFT_MIX, M_SCALE_MIX, M_GATE_MIX, M_SHIFT_FFN, M_SCALE_FFN, M_GATE_FFN = range(6)
ROW_PARAMS = pltpu.CompilerParams(dimension_semantics=("arbitrary", "arbitrary"), vmem_limit_bytes=VMEM_LIMIT)


def _rms(x):
    return x * lax.rsqrt(jnp.mean(x * x, axis=-1, keepdims=True) + EPS)


def _mrow(m_ref, r):
    return m_ref[r:r + 1, :]


def _row_spec(width):
    return pl.BlockSpec((None, ROW_TILE, width), lambda bi, i: (bi, i, 0))


def _vec_spec(width):
    return pl.BlockSpec((1, width), lambda bi, i: (0, 0))


def _mat_spec(k, n):
    return pl.BlockSpec((k, n), lambda bi, i: (0, 0))


def _mod_spec(n_ctx):
    return pl.BlockSpec((None, None, MOD_ROWS, D_MODEL), lambda bi, i: (bi, jnp.where(i < n_ctx // ROW_TILE, 0, 1), 0, 0))


def _norm_mod_kernel(x_ref, w_ref, m_ref, o_ref):
    y = _rms(x_ref[...]) * w_ref[...]
    o_ref[...] = (y * (1.0 + _mrow(m_ref, M_SCALE_MIX)) + _mrow(m_ref, M_SHIFT_MIX)).astype(o_ref.dtype)


def _norm_mod(xs, w, mods, n_ctx):
    b, t, d = xs.shape
    return pl.pallas_call(
        _norm_mod_kernel,
        grid=(b, t // ROW_TILE),
        in_specs=[_row_spec(d), _vec_spec(d), _mod_spec(n_ctx)],
        out_specs=_row_spec(d),
        out_shape=jax.ShapeDtypeStruct((b, t, d), BF16),
        compiler_params=ROW_PARAMS,
        name="norm_mod",
    )(xs, w.reshape(1, d), mods)


def _merge_kernel(ym_ref, yh_ref, og_ref, g_ref, gate_ref, x_ref, m_ref, w1_ref, w2_ref, w3_ref, wo_ref, npost_ref, npre_ref, rw_ref, rb_ref, xo_ref, h_ref, lg_ref):
    d = D_MODEL
    gg = g_ref[...].astype(F32)
    yg = (og_ref[...].astype(F32) * (gg * jax.nn.sigmoid(gg))).astype(BF16)
    gate = jax.nn.sigmoid(gate_ref[...].astype(F32))
    merged = gate[:, :d] * jnp.dot(ym_ref[...], w1_ref[...], preferred_element_type=F32)
    merged = merged + gate[:, d:2 * d] * jnp.dot(yh_ref[...], w2_ref[...], preferred_element_type=F32)
    merged = merged + gate[:, 2 * d:] * jnp.dot(yg, w3_ref[...], preferred_element_type=F32)
    mix = jnp.dot(merged.astype(BF16), wo_ref[...], preferred_element_type=F32)
    x = x_ref[...] + _mrow(m_ref, M_GATE_MIX) * (_rms(mix) * npost_ref[...])
    xo_ref[...] = x
    h = (_rms(x) * npre_ref[...] * (1.0 + _mrow(m_ref, M_SCALE_FFN)) + _mrow(m_ref, M_SHIFT_FFN)).astype(BF16)
    h_ref[...] = h
    lg_ref[...] = jnp.dot(h, rw_ref[...], preferred_element_type=F32) + rb_ref[...]


def _merge(ym, yh, og, g, gate, xs, mods, w_br_ssm, w_br_hy, w_br_hg, w_out, norm_post, norm_ffn_pre, router_w, router_b, n_ctx):
    b, t, d = xs.shape
    rw = jnp.pad(router_w, ((0, 0), (0, LANES - N_EXPERTS))).astype(BF16)
    rb = jnp.pad(router_b, (0, LANES - N_EXPERTS)).reshape(1, LANES).astype(F32)
    return pl.pallas_call(
        _merge_kernel,
        grid=(b, t // ROW_TILE),
        in_specs=[_row_spec(d), _row_spec(d), _row_spec(d), _row_spec(d), _row_spec(N_BRANCHES * d), _row_spec(d), _mod_spec(n_ctx),
                  _mat_spec(d, d), _mat_spec(d, d), _mat_spec(d, d), _mat_spec(d, d), _vec_spec(d), _vec_spec(d), _mat_spec(d, LANES), _vec_spec(LANES)],
        out_specs=[_row_spec(d), _row_spec(d), _row_spec(LANES)],
        out_shape=[jax.ShapeDtypeStruct((b, t, d), F32), jax.ShapeDtypeStruct((b, t, d), BF16), jax.ShapeDtypeStruct((b, t, LANES), F32)],
        compiler_params=ROW_PARAMS,
        name="branch_merge",
    )(ym, yh, og, g, gate, xs, mods, w_br_ssm.astype(BF16), w_br_hy.astype(BF16), w_br_hg.astype(BF16), w_out.astype(BF16),
      norm_post.reshape(1, d), norm_ffn_pre.reshape(1, d), rw, rb)


def _post_ffn_kernel(y0_ref, y1_ref, y2_ref, y3_ref, x_ref, m_ref, w_ref, o_ref):
    f = y0_ref[...].astype(F32) + y1_ref[...].astype(F32) + y2_ref[...].astype(F32) + y3_ref[...].astype(F32)
    o_ref[...] = x_ref[...] + _mrow(m_ref, M_GATE_FFN) * (_rms(f) * w_ref[...])


def _post_ffn(f4, xs, mods, norm_post, n_ctx):
    b, t, d = xs.shape

    def k_spec(k):
        return pl.BlockSpec((None, None, ROW_TILE, d), lambda bi, i: (k, bi, i, 0))

    return pl.pallas_call(
        _post_ffn_kernel,
        grid=(b, t // ROW_TILE),
        in_specs=[k_spec(k) for k in range(TOP_K)] + [_row_spec(d), _mod_spec(n_ctx), _vec_spec(d)],
        out_specs=_row_spec(d),
        out_shape=jax.ShapeDtypeStruct((b, t, d), F32),
        compiler_params=ROW_PARAMS,
        name="post_ffn",
    )(f4, f4, f4, f4, xs, mods, norm_post.reshape(1, d))


def _moe_ffn(h2, logits, w1, b1, w2, b2):
    t, d = h2.shape
    n = t * TOP_K
    n_tiles = n // MOE_BLOCK
    top_v, top_e = lax.top_k(logits, TOP_K)
    gate_w = jax.nn.softmax(top_v, axis=-1)
    flat_e = top_e.reshape(n).astype(jnp.int32)
    iota = jnp.arange(n, dtype=jnp.int32)
    _, order, sw = lax.sort((flat_e, iota, gate_w.reshape(n)), num_keys=1, is_stable=True)
    _, inv = lax.sort((order, iota), num_keys=1)
    xs = h2[order // TOP_K]
    counts = jnp.sum((flat_e[:, None] == jnp.arange(N_EXPERTS, dtype=jnp.int32)[None, :]).astype(jnp.int32), axis=0)
    end = jnp.cumsum(counts)
    start = end - counts
    first_tile = start // MOE_BLOCK
    n_items = jnp.where(counts > 0, (end - 1) // MOE_BLOCK - first_tile + 1, 0)
    items_end = jnp.cumsum(n_items)
    w = jnp.arange(n_tiles + N_EXPERTS, dtype=jnp.int32)
    valid = w < items_end[-1]
    e_w = jnp.minimum(jnp.searchsorted(items_end, w, side='right'), N_EXPERTS - 1).astype(jnp.int32)
    tile_w = first_tile[e_w] + (w - (items_end[e_w] - n_items[e_w]))
    lo = jnp.where(valid, jnp.maximum(start[e_w], tile_w * MOE_BLOCK), 0)
    hi = jnp.where(valid, jnp.minimum(end[e_w], (tile_w + 1) * MOE_BLOCK), 0)
    tile_w = jnp.where(valid, tile_w, n_tiles - 1)
    first = jnp.concatenate([jnp.ones((1,), jnp.int32), (tile_w[1:] != tile_w[:-1]).astype(jnp.int32)])
    ys = _moe_experts(xs, sw, tile_w.astype(jnp.int32), e_w, lo.astype(jnp.int32), hi.astype(jnp.int32), first, w1.astype(BF16), b1, w2.astype(BF16), b2)
    return ys[inv.reshape(t, TOP_K).T]


def kernel(x, c, ctx, c_ctx, w_mod, b_mod, norm_mix_pre, norm_mix_post, norm_ffn_pre, norm_ffn_post, w_in, ssm_conv_w, ssm_conv_b, ssm_dt_bias, ssm_a_log, ssm_d, ssm_norm, hy_conv_w, hy_conv_b, hy_w1, hy_b1, hy_w2, hy_b2, hy_w3, hy_b3, hy_freq, hy_bias, hg_lb_logits, hg_norm, w_br_ssm, w_br_hy, w_br_hg, w_out, router_w, router_b, exp_w1, exp_b1, exp_w2, exp_b2):
    hp = lax.Precision.HIGHEST
    b, n_lat, d = x.shape
    n_ctx = ctx.shape[1]
    lb = jax.nn.softmax(hg_lb_logits.astype(F32), axis=1)
    lb = jnp.cumsum(lb, axis=1) - lb[:, :1]
    silu_c = jax.nn.silu(c)
    silu_cc = jax.nn.silu(c_ctx)
    xs = jnp.concatenate([ctx, x], axis=1)
    for li in range(DEPTH):
        mx = (jnp.dot(silu_c, w_mod[li], precision=hp) + b_mod[li]).reshape(b, 1, 6, d)
        mc = jnp.broadcast_to((jnp.dot(silu_cc, w_mod[li], precision=hp) + b_mod[li]).reshape(1, 1, 6, d), (b, 1, 6, d))
        mods = jnp.pad(jnp.concatenate([mc, mx], axis=1), ((0, 0), (0, 0), (0, MOD_ROWS - 6), (0, 0)))
        h = _norm_mod(xs, norm_mix_pre[li], mods, n_ctx)
        ym, yh, og, g, gate = _mixer_branches(h, w_in[li], lb[:, li], n_ctx, ssm_conv_w[li], ssm_conv_b[li], ssm_dt_bias[li], ssm_a_log[li], ssm_d[li], ssm_norm[li], hy_conv_w[li], hy_conv_b[li], hy_w1[li], hy_b1[li], hy_w2[li], hy_b2[li], hy_w3[li], hy_b3[li], hy_freq[li], hy_bias[li], hg_norm[li])
        xs, h_ffn, logits = _merge(ym, yh, og, g, gate, xs, mods, w_br_ssm[li], w_br_hy[li], w_br_hg[li], w_out[li], norm_mix_post[li], norm_ffn_pre[li], router_w[li], router_b[li], n_ctx)
        if li == DEPTH - 1:
            xs, h_ffn, logits, n_ctx = xs[:, n_ctx:], h_ffn[:, n_ctx:], logits[:, n_ctx:], 0
        t = xs.shape[1]
        f4 = _moe_ffn(h_ffn.reshape(b * t, d), logits.reshape(b * t, LANES)[:, :N_EXPERTS], exp_w1[li], exp_b1[li], exp_w2[li], exp_b2[li])
        xs = _post_ffn(f4.reshape(TOP_K, b, t, d), xs, mods, norm_ffn_post[li], n_ctx)
    return xs
```

```python
import functools
import math

import jax
import jax.numpy as jnp
import numpy as np
from jax import lax
from jax.experimental import pallas as pl
from jax.experimental.pallas import tpu as pltpu

D_MODEL = 1024
DEPTH = 2
GRID_W = 64

SSM_HEADS = 16
SSM_HEAD_DIM = 64
SSM_INNER = SSM_HEADS * SSM_HEAD_DIM
SSM_STATE = 128
SSM_GROUPS = 4
SSD_CHUNK = 128
SSM_XBC = SSM_INNER + 2 * SSM_GROUPS * SSM_STATE

HY_WIDTH = D_MODEL
HY_ORDER = 2
HY_BANDS = 16
HY_FAST_DECAY = 0.3
HY_SLOW_DECAY = 1.5
HY_DECAY_TARGET = 1e-2

HG_HEADS = 8
HG_KDIM = 128
HG_VDIM = D_MODEL // HG_HEADS
HG_QK = HG_HEADS * HG_KDIM
HG_V = HG_HEADS * HG_VDIM
HG_CHUNK = 64
F_FLOOR = 1e-20

N_EXPERTS = 32
TOP_K = 4
D_FF = D_MODEL
SWIGLU_LIMIT = 7.0
SWIGLU_ALPHA = 1.702
MOE_BLOCK = 256

N_BRANCHES = 3
IN_SIZES = (SSM_INNER, SSM_XBC, 2 * SSM_HEADS, (HY_ORDER + 1) * HY_WIDTH, HG_QK, 2 * HG_QK, HG_V, HG_V, N_BRANCHES * D_MODEL)
EPS = 1e-6
F32 = jnp.float32
BF16 = jnp.bfloat16

LANES = 128
VMEM_LIMIT = 56 * 1024 * 1024


def _mm_kernel(a_ref, b_ref, o_ref):
    o_ref[...] = jnp.dot(a_ref[...], b_ref[...], preferred_element_type=F32).astype(o_ref.dtype)


def _mm(a, b, out_dtype=F32, tm=512, tn=512):
    m, k = a.shape
    n = b.shape[1]
    tm = min(tm, m)
    tn = min(tn, n)
    assert m % tm == 0 and n % tn == 0, (m, n, tm, tn)
    return pl.pallas_call(
        _mm_kernel,
        grid=(n // tn, m // tm),
        in_specs=[pl.BlockSpec((tm, k), lambda j, i: (i, 0)), pl.BlockSpec((k, tn), lambda j, i: (0, j))],
        out_specs=pl.BlockSpec((tm, tn), lambda j, i: (i, j)),
        out_shape=jax.ShapeDtypeStruct((m, n), out_dtype),
        compiler_params=pltpu.CompilerParams(dimension_semantics=("arbitrary", "arbitrary"), vmem_limit_bytes=VMEM_LIMIT),
        name="dense_mm",
    )(a, b)


def _moe_kernel(tile_ref, exp_ref, lo_ref, hi_ref, first_ref, x_ref, sw_ref, w1_ref, b1_ref, w2_ref, b2_ref, o_ref):
    del exp_ref
    w = pl.program_id(0)
    lo, hi = lo_ref[w], hi_ref[w]

    @pl.when(hi > lo)
    def _():
        hh = jnp.dot(x_ref[...], w1_ref[...], preferred_element_type=F32) + b1_ref[...]
        g = jnp.minimum(hh[:, :D_FF], SWIGLU_LIMIT)
        u = jnp.clip(hh[:, D_FF:], -SWIGLU_LIMIT, SWIGLU_LIMIT)
        act = (u + 1.0) * g * jax.nn.sigmoid(SWIGLU_ALPHA * g)
        y = jnp.dot(act.astype(BF16), w2_ref[...], preferred_element_type=F32) + b2_ref[...]
        y = (y * sw_ref[...]).astype(o_ref.dtype)
        rows = tile_ref[w] * MOE_BLOCK + lax.broadcasted_iota(jnp.int32, (MOE_BLOCK, 1), 0)
        mine = jnp.logical_and(rows >= lo, rows < hi)

        @pl.when(first_ref[w] == 1)
        def _():
            o_ref[...] = jnp.where(mine, y, jnp.zeros_like(y))

        @pl.when(first_ref[w] != 1)
        def _():
            o_ref[...] = jnp.where(mine, y, o_ref[...])


def _moe_experts(xs, sw, tile_w, exp_w, lo, hi, first, w1, b1, w2, b2):
    n, d = xs.shape
    grid_spec = pltpu.PrefetchScalarGridSpec(
        num_scalar_prefetch=5,
        grid=(tile_w.shape[0],),
        in_specs=[
            pl.BlockSpec((MOE_BLOCK, d), lambda w, tl, ex, lo_, hi_, fi: (tl[w], 0)),
            pl.BlockSpec((MOE_BLOCK, 1), lambda w, tl, ex, lo_, hi_, fi: (tl[w], 0)),
            pl.BlockSpec((None, d, 2 * D_FF), lambda w, tl, ex, lo_, hi_, fi: (ex[w], 0, 0)),
            pl.BlockSpec((None, 1, 2 * D_FF), lambda w, tl, ex, lo_, hi_, fi: (ex[w], 0, 0)),
            pl.BlockSpec((None, D_FF, d), lambda w, tl, ex, lo_, hi_, fi: (ex[w], 0, 0)),
            pl.BlockSpec((None, 1, d), lambda w, tl, ex, lo_, hi_, fi: (ex[w], 0, 0)),
        ],
        out_specs=pl.BlockSpec((MOE_BLOCK, d), lambda w, tl, ex, lo_, hi_, fi: (tl[w], 0)),
    )
    return pl.pallas_call(
        _moe_kernel,
        grid_spec=grid_spec,
        out_shape=jax.ShapeDtypeStruct((n, d), BF16),
        compiler_params=pltpu.CompilerParams(dimension_semantics=("arbitrary",), vmem_limit_bytes=VMEM_LIMIT),
        name="moe_experts",
    )(tile_w, exp_w, lo, hi, first, xs, sw.reshape(n, 1), w1, b1.reshape(N_EXPERTS, 1, 2 * D_FF), w2, b2.reshape(N_EXPERTS, 1, d))


SUBLANES = 8
NEG_BIG = -1e30
HIER_LEVELS = (64, 32, 16)


def _split3(x):
    h1 = x.astype(BF16)
    r1 = x - h1.astype(F32)
    h2 = r1.astype(BF16)
    h3 = (r1 - h2.astype(F32)).astype(BF16)
    return h1, h2, h3


def _dot_nt(a, b):
    return lax.dot_general(a, b, (((1,), (1,)), ((), ())), preferred_element_type=F32)


def _gla_kernel(q_ref, a_ref, v_ref, lb_ref, *rest, reverse):
    if reverse:
        of_ref, w_ref, o_ref, st_ref = rest
    else:
        o_ref, st_ref = rest
    Q = HG_CHUNK

    @pl.when(pl.program_id(1) == 0)
    def _():
        st_ref[...] = jnp.zeros_like(st_ref)

    row = lax.broadcasted_iota(jnp.int32, (Q, Q), 0)
    col = lax.broadcasted_iota(jnp.int32, (Q, Q), 1)
    tri = jnp.where((col >= row) if reverse else (col <= row), 1.0, 0.0).astype(BF16)
    rowk = lax.broadcasted_iota(jnp.int32, (Q, HG_KDIM), 0)
    sub3 = lax.broadcasted_iota(jnp.int32, (Q // SUBLANES, SUBLANES, HG_KDIM), 1)

    lb = lb_ref[...]
    a = a_ref[...].astype(F32)
    f = lb + (1.0 - lb) * jax.nn.sigmoid(a)
    logf = jnp.log(jnp.maximum(f, F_FLOOR))
    kk = (1.0 - lb) * jax.nn.sigmoid(-a)
    g_all = sum(jnp.dot(tri, p, preferred_element_type=F32) for p in _split3(logf))
    q_all = q_ref[...].astype(F32)
    q_all = q_all * jax.nn.sigmoid(q_all)
    v_all = v_ref[...].astype(F32)
    tot = 0 if reverse else Q - 1

    for h in range(HG_HEADS):
        sl = slice(h * HG_KDIM, (h + 1) * HG_KDIM)
        g, qh, kh, vh = g_all[:, sl], q_all[:, sl], kk[:, sl], v_all[:, sl]
        vb = vh.astype(BF16)
        g_tot = g[tot:tot + 1, :]
        st = st_ref[h]
        o = _dot_nt((qh * jnp.exp(g)).astype(BF16), st.astype(BF16))
        attn = jnp.zeros((Q, Q), F32)
        for s in HIER_LEVELS:
            half = s // 2
            m_off = half if reverse else half - 1
            gref = jnp.concatenate([jnp.broadcast_to(g[b0 + m_off:b0 + m_off + 1, :], (s, HG_KDIM)) for b0 in range(0, Q, s)], axis=0)
            upper = (rowk % s) >= half
            qmask = jnp.logical_not(upper) if reverse else upper
            eq = jnp.exp(jnp.where(qmask, g - gref, NEG_BIG))
            ek = jnp.exp(jnp.where(qmask, NEG_BIG, gref - g))
            lvl = _dot_nt((qh * eq).astype(BF16), (kh * ek).astype(BF16))
            if s < Q:
                lvl = jnp.where((row // s) == (col // s), lvl, 0.0)
            attn = attn + lvl
        o = o + jnp.dot(attn.astype(BF16), vb, preferred_element_type=F32)
        shp3 = (Q // SUBLANES, SUBLANES, HG_KDIM)
        g3, q3, k3, v3 = g.reshape(shp3), qh.reshape(shp3), kh.reshape(shp3), vb.astype(F32).reshape(shp3)
        o3 = jnp.zeros(shp3, F32)
        for r in range(SUBLANES):
            if r == 0:
                p = q3 * k3
                v_r = v3
            else:
                sh = (SUBLANES - r) if reverse else r
                g_r = pltpu.roll(g3, sh, 1)
                k_r = pltpu.roll(k3, sh, 1)
                v_r = pltpu.roll(v3, sh, 1)
                valid = (sub3 + r < SUBLANES) if reverse else (sub3 >= r)
                p = q3 * k_r * jnp.exp(jnp.where(valid, g3 - g_r, NEG_BIG))
            o3 = o3 + jnp.sum(p, axis=-1, keepdims=True) * v_r
        o = o + o3.reshape(Q, HG_VDIM)
        k_st = (kh * jnp.exp(g_tot - g)).astype(BF16)
        st_ref[h] = st * jnp.exp(g_tot) + jnp.dot(vh.T.astype(BF16), k_st, preferred_element_type=F32)
        if reverse:
            o = o + of_ref[:, sl]
            o = o * lax.rsqrt(jnp.mean(o * o, axis=-1, keepdims=True) + EPS) * w_ref[:, sl]
        o_ref[:, sl] = o.astype(o_ref.dtype)


def _hgrn_scan(q_raw, f_raw, i_raw, lb, norm_w, n_ctx):
    b, t, _ = q_raw.shape
    nc, ncc = t // HG_CHUNK, n_ctx // HG_CHUNK
    blk = (None, HG_CHUNK, HG_QK)
    scratch = [pltpu.VMEM((HG_HEADS, HG_VDIM, HG_KDIM), F32)]
    params = pltpu.CompilerParams(dimension_semantics=("arbitrary", "arbitrary"), vmem_limit_bytes=VMEM_LIMIT)
    row_spec = pl.BlockSpec((1, HG_QK), lambda bi, s: (0, 0))

    def fwd_chunk(s):
        return s

    def bwd_chunk(s):
        return jnp.where(s < ncc, ncc - 1 - s, nc + ncc - 1 - s)

    o_f = pl.pallas_call(
        functools.partial(_gla_kernel, reverse=False),
        grid=(b, nc),
        in_specs=[
            pl.BlockSpec(blk, lambda bi, s: (bi, fwd_chunk(s), 0)),
            pl.BlockSpec(blk, lambda bi, s: (bi, fwd_chunk(s), 0)),
            pl.BlockSpec(blk, lambda bi, s: (bi, fwd_chunk(s), 0)),
            row_spec,
        ],
        out_specs=pl.BlockSpec(blk, lambda bi, s: (bi, fwd_chunk(s), 0)),
        out_shape=jax.ShapeDtypeStruct((b, t, HG_V), F32),
        scratch_shapes=scratch,
        compiler_params=params,
        name="gla_fwd",
    )(q_raw, f_raw, i_raw, lb[0:1])
    return pl.pallas_call(
        functools.partial(_gla_kernel, reverse=True),
        grid=(b, nc),
        in_specs=[
            pl.BlockSpec(blk, lambda bi, s: (bi, bwd_chunk(s), 0)),
            pl.BlockSpec(blk, lambda bi, s: (bi, bwd_chunk(s), 1)),
            pl.BlockSpec(blk, lambda bi, s: (bi, bwd_chunk(s), 0)),
            row_spec,
            pl.BlockSpec(blk, lambda bi, s: (bi, bwd_chunk(s), 0)),
            row_spec,
        ],
        out_specs=pl.BlockSpec(blk, lambda bi, s: (bi, bwd_chunk(s), 0)),
        out_shape=jax.ShapeDtypeStruct((b, t, HG_V), BF16),
        scratch_shapes=scratch,
        compiler_params=params,
        name="gla_bwd",
    )(q_raw, f_raw, i_raw, lb[1:2], o_f, norm_w.reshape(1, HG_V))


CONV_TILE = 256
HALO = 16


def _dwconv_kernel(prev_ref, cur_ref, next_ref, w_ref, b_ref, o_ref, *, taps, n_ctx_tiles, n_tiles, silu):
    i = pl.program_id(1)
    first = jnp.logical_or(i == 0, i == n_ctx_tiles)
    last = jnp.logical_or(i == n_ctx_tiles - 1, i == n_tiles - 1)
    pad = taps // 2
    xp = jnp.where(first, 0.0, prev_ref[...].astype(F32))
    xn = jnp.where(last, 0.0, next_ref[...].astype(F32))
    xcat = jnp.concatenate([xp, cur_ref[...].astype(F32), xn], axis=0)
    acc = jnp.broadcast_to(b_ref[...], cur_ref.shape).astype(F32)
    for k in range(taps):
        off = HALO - pad + k
        acc = acc + w_ref[k:k + 1, :] * xcat[off:off + CONV_TILE, :]
    if silu:
        acc = acc * jax.nn.sigmoid(acc)
    o_ref[...] = acc.astype(o_ref.dtype)


def _dwconv_stream(x, w, bias, n_ctx, silu, ct=1024):
    b, t, c = x.shape
    taps = w.shape[0]
    n_tiles = t // CONV_TILE
    hb = CONV_TILE // HALO
    n_halo = t // HALO
    kern = functools.partial(_dwconv_kernel, taps=taps, n_ctx_tiles=n_ctx // CONV_TILE, n_tiles=n_tiles, silu=silu)
    return pl.pallas_call(
        kern,
        grid=(b, n_tiles, c // ct),
        in_specs=[
            pl.BlockSpec((None, HALO, ct), lambda bi, i, j: (bi, jnp.maximum(i * hb - 1, 0), j)),
            pl.BlockSpec((None, CONV_TILE, ct), lambda bi, i, j: (bi, i, j)),
            pl.BlockSpec((None, HALO, ct), lambda bi, i, j: (bi, jnp.minimum((i + 1) * hb, n_halo - 1), j)),
            pl.BlockSpec((taps, ct), lambda bi, i, j: (0, j)),
            pl.BlockSpec((1, ct), lambda bi, i, j: (0, j)),
        ],
        out_specs=pl.BlockSpec((None, CONV_TILE, ct), lambda bi, i, j: (bi, i, j)),
        out_shape=jax.ShapeDtypeStruct((b, t, c), BF16),
        compiler_params=pltpu.CompilerParams(dimension_semantics=("arbitrary",) * 3, vmem_limit_bytes=VMEM_LIMIT),
        name="dwconv",
    )(x, x, x, w.astype(F32), bias.reshape(1, c).astype(F32))


SSM_GHEADS = SSM_HEADS // SSM_GROUPS
SSM_GP = SSM_GHEADS * SSM_HEAD_DIM


def _ssd_kernel(xbc_ref, dt_ref, dtb_ref, a_ref, *rest, reverse):
    if reverse:
        yf_ref, z_ref, dsk_ref, nw_ref, o_ref, st_ref = rest
    else:
        o_ref, st_ref = rest
    Q = SSD_CHUNK

    @pl.when(pl.program_id(1) == 0)
    def _():
        st_ref[...] = jnp.zeros_like(st_ref)

    row = lax.broadcasted_iota(jnp.int32, (Q, Q), 0)
    col = lax.broadcasted_iota(jnp.int32, (Q, Q), 1)
    keep = (col >= row) if reverse else (col <= row)
    tri = jnp.where(keep, 1.0, 0.0).astype(BF16)
    expand = jnp.where(lax.broadcasted_iota(jnp.int32, (LANES, SSM_INNER), 1) // SSM_HEAD_DIM == lax.broadcasted_iota(jnp.int32, (LANES, SSM_INNER), 0), 1.0, 0.0).astype(BF16)

    dt = jax.nn.softplus(dt_ref[...] + dtb_ref[...])
    a = dt * a_ref[...]
    cs = sum(jnp.dot(tri, p, preferred_element_type=F32) for p in _split3(a))
    tot = 0 if reverse else Q - 1
    cs_tot = cs[tot:tot + 1, :]
    cs_t = cs.T
    dt_e = jnp.dot(dt.astype(BF16), expand, preferred_element_type=F32)
    e_in = jnp.dot(jnp.exp(cs).astype(BF16), expand, preferred_element_type=F32)
    e_st = jnp.dot(jnp.exp(cs_tot - cs).astype(BF16), expand, preferred_element_type=F32)
    e_tot = jnp.dot(jnp.broadcast_to(jnp.exp(cs_tot), (SUBLANES, LANES)).astype(BF16), expand, preferred_element_type=F32)[0:1, :]

    xs = xbc_ref[:, :SSM_INNER].astype(F32)
    xd = (xs * dt_e).astype(BF16)
    xst = (xs * dt_e * e_st).astype(BF16)
    lane_lo = lax.broadcasted_iota(jnp.int32, (Q, LANES), 1) < SSM_HEAD_DIM
    ys = []
    for g in range(SSM_GROUPS):
        bm = xbc_ref[:, SSM_INNER + g * SSM_STATE:SSM_INNER + (g + 1) * SSM_STATE]
        cm = xbc_ref[:, SSM_INNER + (SSM_GROUPS + g) * SSM_STATE:SSM_INNER + (SSM_GROUPS + g + 1) * SSM_STATE]
        cb = _dot_nt(cm, bm)
        gl = slice(g * SSM_GP, (g + 1) * SSM_GP)
        st = st_ref[g]
        y_off = jnp.dot(cm, st.astype(BF16), preferred_element_type=F32) * e_in[:, gl]
        pieces = []
        for hp in range(SSM_GHEADS // 2):
            pair = []
            for h in (g * SSM_GHEADS + 2 * hp, g * SSM_GHEADS + 2 * hp + 1):
                diff = jnp.broadcast_to(cs[:, h:h + 1], (Q, Q)) - jnp.broadcast_to(cs_t[h:h + 1, :], (Q, Q))
                m = (cb * jnp.exp(jnp.where(keep, diff, NEG_BIG))).astype(BF16)
                pair.append(jnp.dot(m, xd[:, (h // 2) * LANES:(h // 2 + 1) * LANES], preferred_element_type=F32))
            pieces.append(jnp.where(lane_lo, pair[0], pair[1]))
        ys.append(jnp.concatenate(pieces, axis=1) + y_off)
        st_ref[g] = st * e_tot[:, gl] + jnp.dot(bm.astype(F32).T.astype(BF16), xst[:, gl], preferred_element_type=F32)
    y = jnp.concatenate(ys, axis=1)
    if reverse:
        y = (y + yf_ref[...] + xs * dsk_ref[...])
        zz = z_ref[...].astype(F32)
        y = y * (zz * jax.nn.sigmoid(zz))
        y = y * lax.rsqrt(jnp.mean(y * y, axis=-1, keepdims=True) + EPS) * nw_ref[...]
    o_ref[...] = y.astype(o_ref.dtype)


def _ssd_scan(xbc_act, dt2, z, dt_bias, a_log, d_skip, norm_w, n_ctx):
    b, t, _ = xbc_act.shape
    nc, ncc = t // SSD_CHUNK, n_ctx // SSD_CHUNK

    def pad_heads(v):
        return jnp.pad(v.astype(F32), ((0, 0), (0, LANES - SSM_HEADS)))

    dtb = pad_heads(dt_bias.reshape(2, SSM_HEADS))
    a_neg = pad_heads(-jnp.exp(a_log.astype(F32)))
    dsk = jnp.repeat(d_skip.astype(F32), SSM_HEAD_DIM).reshape(1, SSM_INNER)
    scratch = [pltpu.VMEM((SSM_GROUPS, SSM_STATE, SSM_GP), F32)]
    params = pltpu.CompilerParams(dimension_semantics=("arbitrary", "arbitrary"), vmem_limit_bytes=VMEM_LIMIT)

    def bwd_chunk(s):
        return jnp.where(s < ncc, ncc - 1 - s, nc + ncc - 1 - s)

    def specs(chunk, d):
        return [
            pl.BlockSpec((None, SSD_CHUNK, SSM_XBC), lambda bi, s: (bi, chunk(s), 0)),
            pl.BlockSpec((None, SSD_CHUNK, LANES), lambda bi, s: (bi, chunk(s), d)),
            pl.BlockSpec((1, LANES), lambda bi, s: (0, 0)),
            pl.BlockSpec((1, LANES), lambda bi, s: (0, 0)),
        ]

    def inner_spec(chunk):
        return pl.BlockSpec((None, SSD_CHUNK, SSM_INNER), lambda bi, s: (bi, chunk(s), 0))

    row_spec = pl.BlockSpec((1, SSM_INNER), lambda bi, s: (0, 0))
    y_f = pl.pallas_call(
        functools.partial(_ssd_kernel, reverse=False),
        grid=(b, nc),
        in_specs=specs(lambda s: s, 0),
        out_specs=inner_spec(lambda s: s),
        out_shape=jax.ShapeDtypeStruct((b, t, SSM_INNER), F32),
        scratch_shapes=scratch,
        compiler_params=params,
        name="ssd_fwd",
    )(xbc_act, dt2, dtb[0:1], a_neg[0:1])
    return pl.pallas_call(
        functools.partial(_ssd_kernel, reverse=True),
        grid=(b, nc),
        in_specs=specs(bwd_chunk, 1) + [inner_spec(bwd_chunk), inner_spec(bwd_chunk), row_spec, row_spec],
        out_specs=inner_spec(bwd_chunk),
        out_shape=jax.ShapeDtypeStruct((b, t, SSM_INNER), BF16),
        scratch_shapes=scratch,
        compiler_params=params,
        name="ssd_bwd",
    )(xbc_act, dt2, dtb[1:2], a_neg[1:2], y_f, z, dsk, norm_w.reshape(1, SSM_INNER).astype(F32))


HY_N2 = LANES
HY_CT = LANES
VMEM_LIMIT_HYENA = 60 * 1024 * 1024


def _hy_dims(L):
    n1 = 2 * L // HY_N2
    k1n = n1 // 2 + 1
    k1p = -(-k1n // SUBLANES) * SUBLANES
    return n1, k1n, k1p


def _hy_tables(L, n1_rows):
    n1, k1n, k1p = _hy_dims(L)
    n = 2 * L
    k1 = np.arange(k1n, dtype=np.float64)[None, :, None]
    nn = (HY_N2 * np.arange(n1_rows, dtype=np.float64)[None, None, :] + np.arange(HY_N2, dtype=np.float64)[:, None, None])
    ang = 2.0 * np.pi * ((k1 * nn) % n) / n
    m1 = np.zeros((HY_N2, 2 * k1p, n1_rows), np.float32)
    m1[:, :k1n] = np.cos(ang)
    m1[:, k1p:k1p + k1n] = -np.sin(ang)
    m4 = np.transpose(m1, (0, 2, 1))
    kk = np.arange(HY_N2, dtype=np.float64)
    a2 = 2.0 * np.pi * ((kk[:, None] * kk[None, :]) % HY_N2) / HY_N2
    c, s = np.cos(a2), np.sin(a2)
    f3 = np.block([[c, s], [-s, c]]).astype(np.float32)
    f3i = np.block([[c, -s], [s, c]]).astype(np.float32)
    return jnp.asarray(m1, BF16), jnp.asarray(m4, BF16), jnp.asarray(f3, BF16), jnp.asarray(f3i, BF16)


def _hy_stage1(u_ref, a_ref, m1_ref, n1_rows, k1p):
    def body(n2, carry):
        xs = u_ref[pl.ds(n2, n1_rows, stride=HY_N2), :].astype(BF16)
        a = jnp.dot(m1_ref[n2], xs, preferred_element_type=F32)
        a_ref[pl.ds(n2, k1p, stride=2 * HY_N2), :] = a[:k1p]
        a_ref[pl.ds(HY_N2 + n2, k1p, stride=2 * HY_N2), :] = a[k1p:]
        return carry

    lax.fori_loop(0, HY_N2, body, 0, unroll=2)


def _hy_spectrum_slab(a_ref, f3_ref, k1):
    blk = a_ref[pl.ds(pl.multiple_of(k1 * 2 * HY_N2, 2 * HY_N2), 2 * HY_N2), :].astype(BF16)
    return jnp.dot(f3_ref[...], blk, preferred_element_type=F32)


def _hy_conv(u_ref, yo_ref, a_ref, y_ref, h_ref, order, m1_ref, m4_ref, f3_ref, f3i_ref, n1_rows, k1n, k1p):
    _hy_stage1(u_ref, a_ref, m1_ref, n1_rows, k1p)

    def stage2(k1, carry):
        x = _hy_spectrum_slab(a_ref, f3_ref, k1)
        h = h_ref[order, pl.ds(pl.multiple_of(k1 * 2 * HY_N2, 2 * HY_N2), 2 * HY_N2), :].astype(F32)
        xr, xi, hr, hi = x[:HY_N2], x[HY_N2:], h[:HY_N2], h[HY_N2:]
        z = jnp.concatenate([xr * hr - xi * hi, xr * hi + xi * hr], axis=0).astype(BF16)
        c = jnp.dot(f3i_ref[...], z, preferred_element_type=F32)
        y_ref[pl.ds(k1, HY_N2, stride=2 * k1p), :] = c[:HY_N2]
        y_ref[pl.ds(k1p + k1, HY_N2, stride=2 * k1p), :] = c[HY_N2:]
        return carry

    lax.fori_loop(0, k1n, stage2, 0)

    def stage3(n2, carry):
        d = y_ref[pl.ds(pl.multiple_of(n2 * 2 * k1p, 2 * k1p), 2 * k1p), :].astype(BF16)
        yo_ref[pl.ds(n2, n1_rows, stride=HY_N2), :] = jnp.dot(m4_ref[n2], d, preferred_element_type=F32)
        return carry

    lax.fori_loop(0, HY_N2, stage3, 0, unroll=2)


def _hyena_kernel(v_ref, x1_ref, x2_ref, h_ref, m1_ref, m4_ref, f3_ref, f3i_ref, bias_ref, o_ref, a_ref, y_ref, u_ref, yo_ref, *, n1_rows, k1n, k1p):
    @pl.when(jnp.logical_and(pl.program_id(0) == 0, pl.program_id(1) == 0))
    def _():
        y_ref[...] = jnp.zeros_like(y_ref)

    u_ref[...] = v_ref[...].astype(F32)
    for order, gate_ref in enumerate((x1_ref, x2_ref)):
        _hy_conv(u_ref, yo_ref, a_ref, y_ref, h_ref, order, m1_ref, m4_ref, f3_ref, f3i_ref, n1_rows, k1n, k1p)
        z = gate_ref[...].astype(F32) * (yo_ref[...] + u_ref[...] * bias_ref[order:order + 1, :])
        if order == 0:
            u_ref[...] = z
        else:
            o_ref[...] = z.astype(o_ref.dtype)


def _hy_filter_kernel(k_ref, m1_ref, f3_ref, o_ref, a_ref, *, n1_rows, k1n, k1p, scale_mid, scale_edge):
    _hy_stage1(k_ref, a_ref, m1_ref, n1_rows, k1p)

    def body(k1, carry):
        x = _hy_spectrum_slab(a_ref, f3_ref, k1)
        w = jnp.where(jnp.logical_or(k1 == 0, k1 == k1n - 1), scale_edge, scale_mid)
        o_ref[pl.ds(pl.multiple_of(k1 * 2 * HY_N2, 2 * HY_N2), 2 * HY_N2), :] = (x * w).astype(o_ref.dtype)
        return carry

    lax.fori_loop(0, k1n, body, 0)


def _single(block_shape, index_map):
    return pl.BlockSpec(block_shape, index_map, pipeline_mode=pl.Buffered(1))


def _hyena_filter_spectrum_pallas(k_time):
    n_ord, n, c = k_time.shape
    L = n // 2
    n1, k1n, k1p = _hy_dims(L)
    m1, _, f3, _ = _hy_tables(L, n1)
    kern = functools.partial(_hy_filter_kernel, n1_rows=n1, k1n=k1n, k1p=k1p, scale_mid=2.0 / n, scale_edge=1.0 / n)
    return pl.pallas_call(
        kern,
        grid=(n_ord, c // HY_CT),
        in_specs=[
            pl.BlockSpec((None, n, HY_CT), lambda o, j: (o, 0, j)),
            _single((HY_N2, 2 * k1p, n1), lambda o, j: (0, 0, 0)),
            _single((2 * HY_N2, 2 * HY_N2), lambda o, j: (0, 0)),
        ],
        out_specs=pl.BlockSpec((None, k1n * 2 * HY_N2, HY_CT), lambda o, j: (o, 0, j)),
        out_shape=jax.ShapeDtypeStruct((n_ord, k1n * 2 * HY_N2, c), BF16),
        scratch_shapes=[pltpu.VMEM((k1p * 2 * HY_N2, HY_CT), F32)],
        compiler_params=pltpu.CompilerParams(dimension_semantics=("arbitrary", "arbitrary"), vmem_limit_bytes=VMEM_LIMIT_HYENA),
        name="hyena_filter_dft",
    )(k_time, m1, f3)


def _hyena_long(hy, h_spec, bias):
    b, L, c3 = hy.shape
    c = c3 // (HY_ORDER + 1)
    nct = c // HY_CT
    n1, k1n, k1p = _hy_dims(L)
    n1_rows = L // HY_N2
    m1, m4, f3, f3i = _hy_tables(L, n1_rows)
    kern = functools.partial(_hyena_kernel, n1_rows=n1_rows, k1n=k1n, k1p=k1p)

    def col(part):
        return _single((None, L, HY_CT), lambda j, bi: (bi, 0, part * nct + j))

    return pl.pallas_call(
        kern,
        grid=(nct, b),
        in_specs=[
            col(0), col(1), col(2),
            _single((HY_ORDER, k1n * 2 * HY_N2, HY_CT), lambda j, bi: (0, 0, j)),
            _single((HY_N2, 2 * k1p, n1_rows), lambda j, bi: (0, 0, 0)),
            _single((HY_N2, n1_rows, 2 * k1p), lambda j, bi: (0, 0, 0)),
            _single((2 * HY_N2, 2 * HY_N2), lambda j, bi: (0, 0)),
            _single((2 * HY_N2, 2 * HY_N2), lambda j, bi: (0, 0)),
            pl.BlockSpec((HY_ORDER, HY_CT), lambda j, bi: (0, j)),
        ],
        out_specs=pl.BlockSpec((None, L, HY_CT), lambda j, bi: (bi, 0, j)),
        out_shape=jax.ShapeDtypeStruct((b, L, c), BF16),
        scratch_shapes=[
            pltpu.VMEM((k1p * 2 * HY_N2, HY_CT), F32),
            pltpu.VMEM((HY_N2 * 2 * k1p, HY_CT), F32),
            pltpu.VMEM((L, HY_CT), F32),
            pltpu.VMEM((L, HY_CT), F32),
        ],
        compiler_params=pltpu.CompilerParams(dimension_semantics=("arbitrary", "arbitrary"), vmem_limit_bytes=VMEM_LIMIT_HYENA),
        name="hyena_long_conv",
    )(hy, hy, hy, h_spec, m1, m4, f3, f3i, bias.astype(F32))


def _split_cols(t, sizes):
    return jnp.split(t, np.cumsum(sizes)[:-1].tolist(), axis=-1)


def _rms_norm(x, w):
    xf = x.astype(F32)
    y = xf * lax.rsqrt(jnp.mean(xf * xf, axis=-1, keepdims=True) + EPS)
    return (y * w.astype(F32)).astype(x.dtype)


def _dwconv_centred(x, w, b):
    k = w.shape[0]
    y = lax.conv_general_dilated(x, w[:, None, :].astype(x.dtype), window_strides=(1,), padding=[(k // 2, k // 2)], dimension_numbers=('NWC', 'WIO', 'NWC'), feature_group_count=x.shape[-1], precision=lax.Precision.HIGHEST)
    return y + b.astype(x.dtype)


def _flip_seq(t):
    return jnp.flip(t, axis=1)


def _to_col_major(t, rows):
    b, rest = t.shape[0], t.shape[2:]
    return jnp.swapaxes(t.reshape((b, rows, GRID_W) + rest), 1, 2).reshape((b, rows * GRID_W) + rest)


def _from_col_major(t, rows):
    b, rest = t.shape[0], t.shape[2:]
    return jnp.swapaxes(t.reshape((b, GRID_W, rows) + rest), 1, 2).reshape((b, rows * GRID_W) + rest)


def _hyena_filter_time(L, w1, b1, w2, b2, w3, b3, freq):
    hp = lax.Precision.HIGHEST
    t = jnp.linspace(0.0, 1.0, L, dtype=F32)[:, None]
    w = 2.0 * math.pi * jnp.arange(L, dtype=F32)[:, None] / L
    bands = jnp.linspace(1e-4, HY_BANDS - 1, HY_BANDS, dtype=F32)
    feats = jnp.concatenate([t, jnp.cos(bands * w), -jnp.sin(bands * w)], axis=-1)
    h = jnp.sin(freq[0] * (jnp.dot(feats, w1, precision=hp) + b1))
    h = jnp.sin(freq[1] * (jnp.dot(h, w2, precision=hp) + b2))
    h = (jnp.dot(h, w3, precision=hp) + b3).astype(F32).reshape(L, 2, HY_ORDER, HY_WIDTH)
    max_decay = math.log(HY_DECAY_TARGET) / HY_FAST_DECAY
    min_decay = math.log(HY_DECAY_TARGET) / HY_SLOW_DECAY
    deltas = jnp.abs(jnp.linspace(min_decay, max_decay, HY_WIDTH, dtype=F32))
    h = h * jnp.exp(-t * deltas)[:, None, None, :]
    fwd, bwd = h[:, 0], h[:, 1]
    k = jnp.concatenate([fwd, jnp.zeros_like(fwd[:1]), jnp.flip(bwd[1:], axis=0)], axis=0)
    return k * lax.rsqrt(jnp.sum(k * k, axis=0, keepdims=True) + EPS)


def _long_conv(u, k_spec, bias):
    L = u.shape[1]
    uf = u.astype(F32)
    y = jnp.fft.irfft(jnp.fft.rfft(uf, n=2 * L, axis=1) * k_spec[None], n=2 * L, axis=1)[:, :L]
    return (y + uf * bias.astype(F32)).astype(u.dtype)


def _hyena_core(proj, w1, b1, w2, b2, w3, b3, freq, bias):
    L = proj.shape[1]
    v, x1, x2 = jnp.split(proj.astype(F32), HY_ORDER + 1, axis=-1)
    k_spec = jnp.fft.rfft(_hyena_filter_time(L, w1, b1, w2, b2, w3, b3, freq), axis=0)
    z = v
    for o, gate in enumerate((x1, x2)):
        z = gate * _long_conv(z, k_spec[:, o], bias[o])
    return z


def _proj(hb, wseg, out_dtype=BF16):
    return _mm(hb, wseg.astype(BF16), out_dtype)


def _mixer_branches(h_rm, w_in, lb, n_ctx, ssm_conv_w, ssm_conv_b, ssm_dt_bias, ssm_a_log, ssm_d, ssm_norm, hy_conv_w, hy_conv_b, hy_w1, hy_b1, hy_w2, hy_b2, hy_w3, hy_b3, hy_freq, hy_bias, hg_norm):
    b, t, d = h_rm.shape
    n_lat = t - n_ctx
    rows = n_lat // GRID_W
    w_z, w_xbc, w_dt, w_hy, w_q, w_f, w_i, w_g, w_gate = _split_cols(w_in, IN_SIZES)
    h2 = h_rm.reshape(b * t, d)

    def proj(w, dtype=BF16):
        return _proj(h2, w, dtype).reshape(b, t, -1)

    zero_pad = jnp.zeros((d, LANES - SSM_HEADS), F32)
    w_dt2 = jnp.concatenate([w_dt[:, :SSM_HEADS], zero_pad, w_dt[:, SSM_HEADS:], zero_pad], axis=1)
    xbc_act = _dwconv_stream(proj(w_xbc), ssm_conv_w, ssm_conv_b, n_ctx, True)
    ym = _ssd_scan(xbc_act, proj(w_dt2, F32), proj(w_z), ssm_dt_bias, ssm_a_log, ssm_d, ssm_norm, n_ctx)

    hy = _dwconv_stream(proj(w_hy), hy_conv_w, hy_conv_b, n_ctx, False)
    yh_ctx = _hyena_core(hy[:, :n_ctx], hy_w1, hy_b1, hy_w2, hy_b2, hy_w3, hy_b3, hy_freq, hy_bias).astype(BF16)
    k_time = jnp.transpose(_hyena_filter_time(n_lat, hy_w1, hy_b1, hy_w2, hy_b2, hy_w3, hy_b3, hy_freq), (1, 0, 2))
    yh_lat = _hyena_long(hy[:, n_ctx:], _hyena_filter_spectrum_pallas(k_time), hy_bias)
    yh = jnp.concatenate([yh_ctx, yh_lat], axis=1)

    h_cm = jnp.concatenate([h_rm[:, :n_ctx], _to_col_major(h_rm[:, n_ctx:], rows)], axis=1).reshape(b * t, d)

    def proj_cm(w):
        return _proj(h_cm, w).reshape(b, t, -1)

    og = _hgrn_scan(proj_cm(w_q), proj_cm(w_f), proj_cm(w_i), lb, hg_norm, n_ctx)
    og = jnp.concatenate([og[:, :n_ctx], _from_col_major(og[:, n_ctx:], rows)], axis=1)
    return ym, yh, og, proj(w_g), proj(w_gate)


ROW_TILE = 256
MOD_ROWS = SUBLANES
M_SHIFT_MIX, M_SCALE_MIX, M_GATE_MIX, M_SHIFT_FFN, M_SCALE_FFN, M_GATE_FFN = range(6)
ROW_PARAMS = pltpu.CompilerParams(dimension_semantics=("arbitrary", "arbitrary"), vmem_limit_bytes=VMEM_LIMIT)


def _rms(x):
    return x * lax.rsqrt(jnp.mean(x * x, axis=-1, keepdims=True) + EPS)


def _mrow(m_ref, r):
    return m_ref[r:r + 1, :]


def _row_spec(width):
    return pl.BlockSpec((None, ROW_TILE, width), lambda bi, i: (bi, i, 0))


def _vec_spec(width):
    return pl.BlockSpec((1, width), lambda bi, i: (0, 0))


def _mat_spec(k, n):
    return pl.BlockSpec((k, n), lambda bi, i: (0, 0))


def _mod_spec(n_ctx):
    return pl.BlockSpec((None, None, MOD_ROWS, D_MODEL), lambda bi, i: (bi, jnp.where(i < n_ctx // ROW_TILE, 0, 1), 0, 0))


def _norm_mod_kernel(x_ref, w_ref, m_ref, o_ref):
    y = _rms(x_ref[...]) * w_ref[...]
    o_ref[...] = (y * (1.0 + _mrow(m_ref, M_SCALE_MIX)) + _mrow(m_ref, M_SHIFT_MIX)).astype(o_ref.dtype)


def _norm_mod(xs, w, mods, n_ctx):
    b, t, d = xs.shape
    return pl.pallas_call(
        _norm_mod_kernel,
        grid=(b, t // ROW_TILE),
        in_specs=[_row_spec(d), _vec_spec(d), _mod_spec(n_ctx)],
        out_specs=_row_spec(d),
        out_shape=jax.ShapeDtypeStruct((b, t, d), BF16),
        compiler_params=ROW_PARAMS,
        name="norm_mod",
    )(xs, w.reshape(1, d), mods)


def _merge_kernel(ym_ref, yh_ref, og_ref, g_ref, gate_ref, x_ref, m_ref, w1_ref, w2_ref, w3_ref, wo_ref, npost_ref, npre_ref, rw_ref, rb_ref, xo_ref, h_ref, lg_ref):
    d = D_MODEL
    gg = g_ref[...].astype(F32)
    yg = (og_ref[...].astype(F32) * (gg * jax.nn.sigmoid(gg))).astype(BF16)
    gate = jax.nn.sigmoid(gate_ref[...].astype(F32))
    merged = gate[:, :d] * jnp.dot(ym_ref[...], w1_ref[...], preferred_element_type=F32)
    merged = merged + gate[:, d:2 * d] * jnp.dot(yh_ref[...], w2_ref[...], preferred_element_type=F32)
    merged = merged + gate[:, 2 * d:] * jnp.dot(yg, w3_ref[...], preferred_element_type=F32)
    mix = jnp.dot(merged.astype(BF16), wo_ref[...], preferred_element_type=F32)
    x = x_ref[...] + _mrow(m_ref, M_GATE_MIX) * (_rms(mix) * npost_ref[...])
    xo_ref[...] = x
    h = (_rms(x) * npre_ref[...] * (1.0 + _mrow(m_ref, M_SCALE_FFN)) + _mrow(m_ref, M_SHIFT_FFN)).astype(BF16)
    h_ref[...] = h
    lg_ref[...] = jnp.dot(h, rw_ref[...], preferred_element_type=F32) + rb_ref[...]


def _merge(ym, yh, og, g, gate, xs, mods, w_br_ssm, w_br_hy, w_br_hg, w_out, norm_post, norm_ffn_pre, router_w, router_b, n_ctx):
    b, t, d = xs.shape
    rw = jnp.pad(router_w, ((0, 0), (0, LANES - N_EXPERTS))).astype(BF16)
    rb = jnp.pad(router_b, (0, LANES - N_EXPERTS)).reshape(1, LANES).astype(F32)
    return pl.pallas_call(
        _merge_kernel,
        grid=(b, t // ROW_TILE),
        in_specs=[_row_spec(d), _row_spec(d), _row_spec(d), _row_spec(d), _row_spec(N_BRANCHES * d), _row_spec(d), _mod_spec(n_ctx),
                  _mat_spec(d, d), _mat_spec(d, d), _mat_spec(d, d), _mat_spec(d, d), _vec_spec(d), _vec_spec(d), _mat_spec(d, LANES), _vec_spec(LANES)],
        out_specs=[_row_spec(d), _row_spec(d), _row_spec(LANES)],
        out_shape=[jax.ShapeDtypeStruct((b, t, d), F32), jax.ShapeDtypeStruct((b, t, d), BF16), jax.ShapeDtypeStruct((b, t, LANES), F32)],
        compiler_params=ROW_PARAMS,
        name="branch_merge",
    )(ym, yh, og, g, gate, xs, mods, w_br_ssm.astype(BF16), w_br_hy.astype(BF16), w_br_hg.astype(BF16), w_out.astype(BF16),
      norm_post.reshape(1, d), norm_ffn_pre.reshape(1, d), rw, rb)


def _post_ffn_kernel(y0_ref, y1_ref, y2_ref, y3_ref, x_ref, m_ref, w_ref, o_ref):
    f = y0_ref[...].astype(F32) + y1_ref[...].astype(F32) + y2_ref[...].astype(F32) + y3_ref[...].astype(F32)
    o_ref[...] = x_ref[...] + _mrow(m_ref, M_GATE_FFN) * (_rms(f) * w_ref[...])


def _post_ffn(f4, xs, mods, norm_post, n_ctx):
    b, t, d = xs.shape

    def k_spec(k):
        return pl.BlockSpec((None, None, ROW_TILE, d), lambda bi, i: (k, bi, i, 0))

    return pl.pallas_call(
        _post_ffn_kernel,
        grid=(b, t // ROW_TILE),
        in_specs=[k_spec(k) for k in range(TOP_K)] + [_row_spec(d), _mod_spec(n_ctx), _vec_spec(d)],
        out_specs=_row_spec(d),
        out_shape=jax.ShapeDtypeStruct((b, t, d), F32),
        compiler_params=ROW_PARAMS,
        name="post_ffn",
    )(f4, f4, f4, f4, xs, mods, norm_post.reshape(1, d))


def _moe_ffn(h2, logits, w1, b1, w2, b2):
    t, d = h2.shape
    n = t * TOP_K
    n_tiles = n // MOE_BLOCK
    top_v, top_e = lax.top_k(logits, TOP_K)
    gate_w = jax.nn.softmax(top_v, axis=-1)
    flat_e = top_e.reshape(n).astype(jnp.int32)
    iota = jnp.arange(n, dtype=jnp.int32)
    _, order, sw = lax.sort((flat_e, iota, gate_w.reshape(n)), num_keys=1, is_stable=True)
    _, inv = lax.sort((order, iota), num_keys=1)
    xs = h2[order // TOP_K]
    counts = jnp.sum((flat_e[:, None] == jnp.arange(N_EXPERTS, dtype=jnp.int32)[None, :]).astype(jnp.int32), axis=0)
    end = jnp.cumsum(counts)
    start = end - counts
    first_tile = start // MOE_BLOCK
    n_items = jnp.where(counts > 0, (end - 1) // MOE_BLOCK - first_tile + 1, 0)
    items_end = jnp.cumsum(n_items)
    w = jnp.arange(n_tiles + N_EXPERTS, dtype=jnp.int32)
    valid = w < items_end[-1]
    e_w = jnp.minimum(jnp.searchsorted(items_end, w, side='right'), N_EXPERTS - 1).astype(jnp.int32)
    tile_w = first_tile[e_w] + (w - (items_end[e_w] - n_items[e_w]))
    lo = jnp.where(valid, jnp.maximum(start[e_w], tile_w * MOE_BLOCK), 0)
    hi = jnp.where(valid, jnp.minimum(end[e_w], (tile_w + 1) * MOE_BLOCK), 0)
    tile_w = jnp.where(valid, tile_w, n_tiles - 1)
    first = jnp.concatenate([jnp.ones((1,), jnp.int32), (tile_w[1:] != tile_w[:-1]).astype(jnp.int32)])
    ys = _moe_experts(xs, sw, tile_w.astype(jnp.int32), e_w, lo.astype(jnp.int32), hi.astype(jnp.int32), first, w1.astype(BF16), b1, w2.astype(BF16), b2)
    return ys[inv.reshape(t, TOP_K).T]


def kernel(x, c, ctx, c_ctx, w_mod, b_mod, norm_mix_pre, norm_mix_post, norm_ffn_pre, norm_ffn_post, w_in, ssm_conv_w, ssm_conv_b, ssm_dt_bias, ssm_a_log, ssm_d, ssm_norm, hy_conv_w, hy_conv_b, hy_w1, hy_b1, hy_w2, hy_b2, hy_w3, hy_b3, hy_freq, hy_bias, hg_lb_logits, hg_norm, w_br_ssm, w_br_hy, w_br_hg, w_out, router_w, router_b, exp_w1, exp_b1, exp_w2, exp_b2):
    hp = lax.Precision.HIGHEST
    b, n_lat, d = x.shape
    n_ctx = ctx.shape[1]
    lb = jax.nn.softmax(hg_lb_logits.astype(F32), axis=1)
    lb = jnp.cumsum(lb, axis=1) - lb[:, :1]
    silu_c = jax.nn.silu(c)
    silu_cc = jax.nn.silu(c_ctx)
    xs = jnp.concatenate([ctx, x], axis=1)
    for li in range(DEPTH):
        mx = (jnp.dot(silu_c, w_mod[li], precision=hp) + b_mod[li]).reshape(b, 1, 6, d)
        mc = jnp.broadcast_to((jnp.dot(silu_cc, w_mod[li], precision=hp) + b_mod[li]).reshape(1, 1, 6, d), (b, 1, 6, d))
        mods = jnp.pad(jnp.concatenate([mc, mx], axis=1), ((0, 0), (0, 0), (0, MOD_ROWS - 6), (0, 0)))
        h = _norm_mod(xs, norm_mix_pre[li], mods, n_ctx)
        ym, yh, og, g, gate = _mixer_branches(h, w_in[li], lb[:, li], n_ctx, ssm_conv_w[li], ssm_conv_b[li], ssm_dt_bias[li], ssm_a_log[li], ssm_d[li], ssm_norm[li], hy_conv_w[li], hy_conv_b[li], hy_w1[li], hy_b1[li], hy_w2[li], hy_b2[li], hy_w3[li], hy_b3[li], hy_freq[li], hy_bias[li], hg_norm[li])
        xs, h_ffn, logits = _merge(ym, yh, og, g, gate, xs, mods, w_br_ssm[li], w_br_hy[li], w_br_hg[li], w_out[li], norm_mix_post[li], norm_ffn_pre[li], router_w[li], router_b[li], n_ctx)
        if li == DEPTH - 1:
            xs, h_ffn, logits, n_ctx = xs[:, n_ctx:], h_ffn[:, n_ctx:], logits[:, n_ctx:], 0
        t = xs.shape[1]
        f4 = _moe_ffn(h_ffn.reshape(b * t, d), logits.reshape(b * t, LANES)[:, :N_EXPERTS], exp_w1[li], exp_b1[li], exp_w2[li], exp_b2[li])
        xs = _post_ffn(f4.reshape(TOP_K, b, t, d), xs, mods, norm_ffn_post[li], n_ctx)
    return xs
```

```python
import functools
import math

import jax
import jax.numpy as jnp
import numpy as np
from jax import lax
from jax.experimental import pallas as pl
from jax.experimental.pallas import tpu as pltpu

D_MODEL = 1024
DEPTH = 2
GRID_W = 64

SSM_HEADS = 16
SSM_HEAD_DIM = 64
SSM_INNER = SSM_HEADS * SSM_HEAD_DIM
SSM_STATE = 128
SSM_GROUPS = 4
SSD_CHUNK = 128
SSM_XBC = SSM_INNER + 2 * SSM_GROUPS * SSM_STATE

HY_WIDTH = D_MODEL
HY_ORDER = 2
HY_BANDS = 16
HY_FAST_DECAY = 0.3
HY_SLOW_DECAY = 1.5
HY_DECAY_TARGET = 1e-2

HG_HEADS = 8
HG_KDIM = 128
HG_VDIM = D_MODEL // HG_HEADS
HG_QK = HG_HEADS * HG_KDIM
HG_V = HG_HEADS * HG_VDIM
HG_CHUNK = 64
F_FLOOR = 1e-20

N_EXPERTS = 32
TOP_K = 4
D_FF = D_MODEL
SWIGLU_LIMIT = 7.0
SWIGLU_ALPHA = 1.702
MOE_BLOCK = 256

N_BRANCHES = 3
IN_SIZES = (SSM_INNER, SSM_XBC, 2 * SSM_HEADS, (HY_ORDER + 1) * HY_WIDTH, HG_QK, 2 * HG_QK, HG_V, HG_V, N_BRANCHES * D_MODEL)
EPS = 1e-6
F32 = jnp.float32
BF16 = jnp.bfloat16

LANES = 128
VMEM_LIMIT = 56 * 1024 * 1024


def _mm_kernel(a_ref, b_ref, o_ref):
    o_ref[...] = jnp.dot(a_ref[...], b_ref[...], preferred_element_type=F32).astype(o_ref.dtype)


def _mm(a, b, out_dtype=F32, tm=1024, tn=1024):
    m, k = a.shape
    n = b.shape[1]
    tm = min(tm, m)
    tn = min(tn, n)
    assert m % tm == 0 and n % tn == 0, (m, n, tm, tn)
    return pl.pallas_call(
        _mm_kernel,
        grid=(n // tn, m // tm),
        in_specs=[pl.BlockSpec((tm, k), lambda j, i: (i, 0)), pl.BlockSpec((k, tn), lambda j, i: (0, j))],
        out_specs=pl.BlockSpec((tm, tn), lambda j, i: (i, j)),
        out_shape=jax.ShapeDtypeStruct((m, n), out_dtype),
        compiler_params=pltpu.CompilerParams(dimension_semantics=("arbitrary", "arbitrary"), vmem_limit_bytes=VMEM_LIMIT),
        name="dense_mm",
    )(a, b)


def _moe_kernel(tile_ref, exp_ref, lo_ref, hi_ref, first_ref, x_ref, sw_ref, w1_ref, b1_ref, w2_ref, b2_ref, o_ref):
    del exp_ref
    w = pl.program_id(0)
    lo, hi = lo_ref[w], hi_ref[w]

    @pl.when(hi > lo)
    def _():
        hh = jnp.dot(x_ref[...], w1_ref[...], preferred_element_type=F32) + b1_ref[...]
        g = jnp.minimum(hh[:, :D_FF], SWIGLU_LIMIT)
        u = jnp.clip(hh[:, D_FF:], -SWIGLU_LIMIT, SWIGLU_LIMIT)
        act = (u + 1.0) * g * jax.nn.sigmoid(SWIGLU_ALPHA * g)
        y = jnp.dot(act.astype(BF16), w2_ref[...], preferred_element_type=F32) + b2_ref[...]
        y = (y * sw_ref[...]).astype(o_ref.dtype)
        rows = tile_ref[w] * MOE_BLOCK + lax.broadcasted_iota(jnp.int32, (MOE_BLOCK, 1), 0)
        mine = jnp.logical_and(rows >= lo, rows < hi)

        @pl.when(first_ref[w] == 1)
        def _():
            o_ref[...] = jnp.where(mine, y, jnp.zeros_like(y))

        @pl.when(first_ref[w] != 1)
        def _():
            o_ref[...] = jnp.where(mine, y, o_ref[...])


def _moe_experts(xs, sw, tile_w, exp_w, lo, hi, first, w1, b1, w2, b2):
    n, d = xs.shape
    grid_spec = pltpu.PrefetchScalarGridSpec(
        num_scalar_prefetch=5,
        grid=(tile_w.shape[0],),
        in_specs=[
            pl.BlockSpec((MOE_BLOCK, d), lambda w, tl, ex, lo_, hi_, fi: (tl[w], 0)),
            pl.BlockSpec((MOE_BLOCK, 1), lambda w, tl, ex, lo_, hi_, fi: (tl[w], 0)),
            pl.BlockSpec((None, d, 2 * D_FF), lambda w, tl, ex, lo_, hi_, fi: (ex[w], 0, 0)),
            pl.BlockSpec((None, 1, 2 * D_FF), lambda w, tl, ex, lo_, hi_, fi: (ex[w], 0, 0)),
            pl.BlockSpec((None, D_FF, d), lambda w, tl, ex, lo_, hi_, fi: (ex[w], 0, 0)),
            pl.BlockSpec((None, 1, d), lambda w, tl, ex, lo_, hi_, fi: (ex[w], 0, 0)),
        ],
        out_specs=pl.BlockSpec((MOE_BLOCK, d), lambda w, tl, ex, lo_, hi_, fi: (tl[w], 0)),
    )
    return pl.pallas_call(
        _moe_kernel,
        grid_spec=grid_spec,
        out_shape=jax.ShapeDtypeStruct((n, d), BF16),
        compiler_params=pltpu.CompilerParams(dimension_semantics=("arbitrary",), vmem_limit_bytes=VMEM_LIMIT),
        name="moe_experts",
    )(tile_w, exp_w, lo, hi, first, xs, sw.reshape(n, 1), w1, b1.reshape(N_EXPERTS, 1, 2 * D_FF), w2, b2.reshape(N_EXPERTS, 1, d))


SUBLANES = 8
NEG_BIG = -1e30
HIER_LEVELS = (64, 32, 16)


def _split3(x):
    h1 = x.astype(BF16)
    r1 = x - h1.astype(F32)
    h2 = r1.astype(BF16)
    h3 = (r1 - h2.astype(F32)).astype(BF16)
    return h1, h2, h3


def _dot_nt(a, b):
    return lax.dot_general(a, b, (((1,), (1,)), ((), ())), preferred_element_type=F32)


def _gla_kernel(q_ref, a_ref, v_ref, lb_ref, *rest, reverse):
    if reverse:
        of_ref, w_ref, o_ref, st_ref = rest
    else:
        o_ref, st_ref = rest
    Q = HG_CHUNK

    @pl.when(pl.program_id(1) == 0)
    def _():
        st_ref[...] = jnp.zeros_like(st_ref)

    row = lax.broadcasted_iota(jnp.int32, (Q, Q), 0)
    col = lax.broadcasted_iota(jnp.int32, (Q, Q), 1)
    tri = jnp.where((col >= row) if reverse else (col <= row), 1.0, 0.0).astype(BF16)
    rowk = lax.broadcasted_iota(jnp.int32, (Q, HG_KDIM), 0)
    sub3 = lax.broadcasted_iota(jnp.int32, (Q // SUBLANES, SUBLANES, HG_KDIM), 1)

    lb = lb_ref[...]
    a = a_ref[...].astype(F32)
    f = lb + (1.0 - lb) * jax.nn.sigmoid(a)
    logf = jnp.log(jnp.maximum(f, F_FLOOR))
    kk = (1.0 - lb) * jax.nn.sigmoid(-a)
    g_all = sum(jnp.dot(tri, p, preferred_element_type=F32) for p in _split3(logf))
    q_all = q_ref[...].astype(F32)
    q_all = q_all * jax.nn.sigmoid(q_all)
    v_all = v_ref[...].astype(F32)
    tot = 0 if reverse else Q - 1

    for h in range(HG_HEADS):
        sl = slice(h * HG_KDIM, (h + 1) * HG_KDIM)
        g, qh, kh, vh = g_all[:, sl], q_all[:, sl], kk[:, sl], v_all[:, sl]
        vb = vh.astype(BF16)
        g_tot = g[tot:tot + 1, :]
        st = st_ref[h]
        o = _dot_nt((qh * jnp.exp(g)).astype(BF16), st.astype(BF16))
        attn = jnp.zeros((Q, Q), F32)
        for s in HIER_LEVELS:
            half = s // 2
            m_off = half if reverse else half - 1
            gref = jnp.concatenate([jnp.broadcast_to(g[b0 + m_off:b0 + m_off + 1, :], (s, HG_KDIM)) for b0 in range(0, Q, s)], axis=0)
            upper = (rowk % s) >= half
            qmask = jnp.logical_not(upper) if reverse else upper
            eq = jnp.exp(jnp.where(qmask, g - gref, NEG_BIG))
            ek = jnp.exp(jnp.where(qmask, NEG_BIG, gref - g))
            lvl = _dot_nt((qh * eq).astype(BF16), (kh * ek).astype(BF16))
            if s < Q:
                lvl = jnp.where((row // s) == (col // s), lvl, 0.0)
            attn = attn + lvl
        o = o + jnp.dot(attn.astype(BF16), vb, preferred_element_type=F32)
        shp3 = (Q // SUBLANES, SUBLANES, HG_KDIM)
        g3, q3, k3, v3 = g.reshape(shp3), qh.reshape(shp3), kh.reshape(shp3), vb.astype(F32).reshape(shp3)
        o3 = jnp.zeros(shp3, F32)
        for r in range(SUBLANES):
            if r == 0:
                p = q3 * k3
                v_r = v3
            else:
                sh = (SUBLANES - r) if reverse else r
                g_r = pltpu.roll(g3, sh, 1)
                k_r = pltpu.roll(k3, sh, 1)
                v_r = pltpu.roll(v3, sh, 1)
                valid = (sub3 + r < SUBLANES) if reverse else (sub3 >= r)
                p = q3 * k_r * jnp.exp(jnp.where(valid, g3 - g_r, NEG_BIG))
            o3 = o3 + jnp.sum(p, axis=-1, keepdims=True) * v_r
        o = o + o3.reshape(Q, HG_VDIM)
        k_st = (kh * jnp.exp(g_tot - g)).astype(BF16)
        st_ref[h] = st * jnp.exp(g_tot) + jnp.dot(vh.T.astype(BF16), k_st, preferred_element_type=F32)
        if reverse:
            o = o + of_ref[:, sl]
            o = o * lax.rsqrt(jnp.mean(o * o, axis=-1, keepdims=True) + EPS) * w_ref[:, sl]
        o_ref[:, sl] = o.astype(o_ref.dtype)


def _hgrn_scan(q_raw, f_raw, i_raw, lb, norm_w, n_ctx):
    b, t, _ = q_raw.shape
    nc, ncc = t // HG_CHUNK, n_ctx // HG_CHUNK
    blk = (None, HG_CHUNK, HG_QK)
    scratch = [pltpu.VMEM((HG_HEADS, HG_VDIM, HG_KDIM), F32)]
    params = pltpu.CompilerParams(dimension_semantics=("arbitrary", "arbitrary"), vmem_limit_bytes=VMEM_LIMIT)
    row_spec = pl.BlockSpec((1, HG_QK), lambda bi, s: (0, 0))

    def fwd_chunk(s):
        return s

    def bwd_chunk(s):
        return jnp.where(s < ncc, ncc - 1 - s, nc + ncc - 1 - s)

    o_f = pl.pallas_call(
        functools.partial(_gla_kernel, reverse=False),
        grid=(b, nc),
        in_specs=[
            pl.BlockSpec(blk, lambda bi, s: (bi, fwd_chunk(s), 0)),
            pl.BlockSpec(blk, lambda bi, s: (bi, fwd_chunk(s), 0)),
            pl.BlockSpec(blk, lambda bi, s: (bi, fwd_chunk(s), 0)),
            row_spec,
        ],
        out_specs=pl.BlockSpec(blk, lambda bi, s: (bi, fwd_chunk(s), 0)),
        out_shape=jax.ShapeDtypeStruct((b, t, HG_V), F32),
        scratch_shapes=scratch,
        compiler_params=params,
        name="gla_fwd",
    )(q_raw, f_raw, i_raw, lb[0:1])
    return pl.pallas_call(
        functools.partial(_gla_kernel, reverse=True),
        grid=(b, nc),
        in_specs=[
            pl.BlockSpec(blk, lambda bi, s: (bi, bwd_chunk(s), 0)),
            pl.BlockSpec(blk, lambda bi, s: (bi, bwd_chunk(s), 1)),
            pl.BlockSpec(blk, lambda bi, s: (bi, bwd_chunk(s), 0)),
            row_spec,
            pl.BlockSpec(blk, lambda bi, s: (bi, bwd_chunk(s), 0)),
            row_spec,
        ],
        out_specs=pl.BlockSpec(blk, lambda bi, s: (bi, bwd_chunk(s), 0)),
        out_shape=jax.ShapeDtypeStruct((b, t, HG_V), BF16),
        scratch_shapes=scratch,
        compiler_params=params,
        name="gla_bwd",
    )(q_raw, f_raw, i_raw, lb[1:2], o_f, norm_w.reshape(1, HG_V))


CONV_TILE = 256
HALO = 16


def _dwconv_kernel(prev_ref, cur_ref, next_ref, w_ref, b_ref, o_ref, *, taps, n_ctx_tiles, n_tiles, silu):
    i = pl.program_id(1)
    first = jnp.logical_or(i == 0, i == n_ctx_tiles)
    last = jnp.logical_or(i == n_ctx_tiles - 1, i == n_tiles - 1)
    pad = taps // 2
    xp = jnp.where(first, 0.0, prev_ref[...].astype(F32))
    xn = jnp.where(last, 0.0, next_ref[...].astype(F32))
    xcat = jnp.concatenate([xp, cur_ref[...].astype(F32), xn], axis=0)
    acc = jnp.broadcast_to(b_ref[...], cur_ref.shape).astype(F32)
    for k in range(taps):
        off = HALO - pad + k
        acc = acc + w_ref[k:k + 1, :] * xcat[off:off + CONV_TILE, :]
    if silu:
        acc = acc * jax.nn.sigmoid(acc)
    o_ref[...] = acc.astype(o_ref.dtype)


def _dwconv_stream(x, w, bias, n_ctx, silu, ct=1024):
    b, t, c = x.shape
    taps = w.shape[0]
    n_tiles = t // CONV_TILE
    hb = CONV_TILE // HALO
    n_halo = t // HALO
    kern = functools.partial(_dwconv_kernel, taps=taps, n_ctx_tiles=n_ctx // CONV_TILE, n_tiles=n_tiles, silu=silu)
    return pl.pallas_call(
        kern,
        grid=(b, n_tiles, c // ct),
        in_specs=[
            pl.BlockSpec((None, HALO, ct), lambda bi, i, j: (bi, jnp.maximum(i * hb - 1, 0), j)),
            pl.BlockSpec((None, CONV_TILE, ct), lambda bi, i, j: (bi, i, j)),
            pl.BlockSpec((None, HALO, ct), lambda bi, i, j: (bi, jnp.minimum((i + 1) * hb, n_halo - 1), j)),
            pl.BlockSpec((taps, ct), lambda bi, i, j: (0, j)),
            pl.BlockSpec((1, ct), lambda bi, i, j: (0, j)),
        ],
        out_specs=pl.BlockSpec((None, CONV_TILE, ct), lambda bi, i, j: (bi, i, j)),
        out_shape=jax.ShapeDtypeStruct((b, t, c), BF16),
        compiler_params=pltpu.CompilerParams(dimension_semantics=("arbitrary",) * 3, vmem_limit_bytes=VMEM_LIMIT),
        name="dwconv",
    )(x, x, x, w.astype(F32), bias.reshape(1, c).astype(F32))


SSM_GHEADS = SSM_HEADS // SSM_GROUPS
SSM_GP = SSM_GHEADS * SSM_HEAD_DIM


def _ssd_kernel(xbc_ref, dt_ref, dtb_ref, a_ref, *rest, reverse):
    if reverse:
        yf_ref, z_ref, dsk_ref, nw_ref, o_ref, st_ref = rest
    else:
        o_ref, st_ref = rest
    Q = SSD_CHUNK

    @pl.when(pl.program_id(1) == 0)
    def _():
        st_ref[...] = jnp.zeros_like(st_ref)

    row = lax.broadcasted_iota(jnp.int32, (Q, Q), 0)
    col = lax.broadcasted_iota(jnp.int32, (Q, Q), 1)
    keep = (col >= row) if reverse else (col <= row)
    tri = jnp.where(keep, 1.0, 0.0).astype(BF16)
    expand = jnp.where(lax.broadcasted_iota(jnp.int32, (LANES, SSM_INNER), 1) // SSM_HEAD_DIM == lax.broadcasted_iota(jnp.int32, (LANES, SSM_INNER), 0), 1.0, 0.0).astype(BF16)

    dt = jax.nn.softplus(dt_ref[...] + dtb_ref[...])
    a = dt * a_ref[...]
    cs = sum(jnp.dot(tri, p, preferred_element_type=F32) for p in _split3(a))
    tot = 0 if reverse else Q - 1
    cs_tot = cs[tot:tot + 1, :]
    cs_t = cs.T
    dt_e = jnp.dot(dt.astype(BF16), expand, preferred_element_type=F32)
    e_in = jnp.dot(jnp.exp(cs).astype(BF16), expand, preferred_element_type=F32)
    e_st = jnp.dot(jnp.exp(cs_tot - cs).astype(BF16), expand, preferred_element_type=F32)
    e_tot = jnp.dot(jnp.broadcast_to(jnp.exp(cs_tot), (SUBLANES, LANES)).astype(BF16), expand, preferred_element_type=F32)[0:1, :]

    xs = xbc_ref[:, :SSM_INNER].astype(F32)
    xd = (xs * dt_e).astype(BF16)
    xst = (xs * dt_e * e_st).astype(BF16)
    lane_lo = lax.broadcasted_iota(jnp.int32, (Q, LANES), 1) < SSM_HEAD_DIM
    ys = []
    for g in range(SSM_GROUPS):
        bm = xbc_ref[:, SSM_INNER + g * SSM_STATE:SSM_INNER + (g + 1) * SSM_STATE]
        cm = xbc_ref[:, SSM_INNER + (SSM_GROUPS + g) * SSM_STATE:SSM_INNER + (SSM_GROUPS + g + 1) * SSM_STATE]
        cb = _dot_nt(cm, bm)
        gl = slice(g * SSM_GP, (g + 1) * SSM_GP)
        st = st_ref[g]
        y_off = jnp.dot(cm, st.astype(BF16), preferred_element_type=F32) * e_in[:, gl]
        pieces = []
        for hp in range(SSM_GHEADS // 2):
            pair = []
            for h in (g * SSM_GHEADS + 2 * hp, g * SSM_GHEADS + 2 * hp + 1):
                diff = jnp.broadcast_to(cs[:, h:h + 1], (Q, Q)) - jnp.broadcast_to(cs_t[h:h + 1, :], (Q, Q))
                m = (cb * jnp.exp(jnp.where(keep, diff, NEG_BIG))).astype(BF16)
                pair.append(jnp.dot(m, xd[:, (h // 2) * LANES:(h // 2 + 1) * LANES], preferred_element_type=F32))
            pieces.append(jnp.where(lane_lo, pair[0], pair[1]))
        ys.append(jnp.concatenate(pieces, axis=1) + y_off)
        st_ref[g] = st * e_tot[:, gl] + jnp.dot(bm.astype(F32).T.astype(BF16), xst[:, gl], preferred_element_type=F32)
    y = jnp.concatenate(ys, axis=1)
    if reverse:
        y = (y + yf_ref[...] + xs * dsk_ref[...])
        zz = z_ref[...].astype(F32)
        y = y * (zz * jax.nn.sigmoid(zz))
        y = y * lax.rsqrt(jnp.mean(y * y, axis=-1, keepdims=True) + EPS) * nw_ref[...]
    o_ref[...] = y.astype(o_ref.dtype)


def _ssd_scan(xbc_act, dt2, z, dt_bias, a_log, d_skip, norm_w, n_ctx):
    b, t, _ = xbc_act.shape
    nc, ncc = t // SSD_CHUNK, n_ctx // SSD_CHUNK

    def pad_heads(v):
        return jnp.pad(v.astype(F32), ((0, 0), (0, LANES - SSM_HEADS)))

    dtb = pad_heads(dt_bias.reshape(2, SSM_HEADS))
    a_neg = pad_heads(-jnp.exp(a_log.astype(F32)))
    dsk = jnp.repeat(d_skip.astype(F32), SSM_HEAD_DIM).reshape(1, SSM_INNER)
    scratch = [pltpu.VMEM((SSM_GROUPS, SSM_STATE, SSM_GP), F32)]
    params = pltpu.CompilerParams(dimension_semantics=("arbitrary", "arbitrary"), vmem_limit_bytes=VMEM_LIMIT)

    def bwd_chunk(s):
        return jnp.where(s < ncc, ncc - 1 - s, nc + ncc - 1 - s)

    def specs(chunk, d):
        return [
            pl.BlockSpec((None, SSD_CHUNK, SSM_XBC), lambda bi, s: (bi, chunk(s), 0)),
            pl.BlockSpec((None, SSD_CHUNK, LANES), lambda bi, s: (bi, chunk(s), d)),
            pl.BlockSpec((1, LANES), lambda bi, s: (0, 0)),
            pl.BlockSpec((1, LANES), lambda bi, s: (0, 0)),
        ]

    def inner_spec(chunk):
        return pl.BlockSpec((None, SSD_CHUNK, SSM_INNER), lambda bi, s: (bi, chunk(s), 0))

    row_spec = pl.BlockSpec((1, SSM_INNER), lambda bi, s: (0, 0))
    y_f = pl.pallas_call(
        functools.partial(_ssd_kernel, reverse=False),
        grid=(b, nc),
        in_specs=specs(lambda s: s, 0),
        out_specs=inner_spec(lambda s: s),
        out_shape=jax.ShapeDtypeStruct((b, t, SSM_INNER), F32),
        scratch_shapes=scratch,
        compiler_params=params,
        name="ssd_fwd",
    )(xbc_act, dt2, dtb[0:1], a_neg[0:1])
    return pl.pallas_call(
        functools.partial(_ssd_kernel, reverse=True),
        grid=(b, nc),
        in_specs=specs(bwd_chunk, 1) + [inner_spec(bwd_chunk), inner_spec(bwd_chunk), row_spec, row_spec],
        out_specs=inner_spec(bwd_chunk),
        out_shape=jax.ShapeDtypeStruct((b, t, SSM_INNER), BF16),
        scratch_shapes=scratch,
        compiler_params=params,
        name="ssd_bwd",
    )(xbc_act, dt2, dtb[1:2], a_neg[1:2], y_f, z, dsk, norm_w.reshape(1, SSM_INNER).astype(F32))


HY_N2 = LANES
HY_CT = LANES
VMEM_LIMIT_HYENA = 60 * 1024 * 1024


def _hy_dims(L):
    n1 = 2 * L // HY_N2
    k1n = n1 // 2 + 1
    k1p = -(-k1n // SUBLANES) * SUBLANES
    return n1, k1n, k1p


def _hy_tables(L, n1_rows):
    n1, k1n, k1p = _hy_dims(L)
    n = 2 * L
    k1 = np.arange(k1n, dtype=np.float64)[None, :, None]
    nn = (HY_N2 * np.arange(n1_rows, dtype=np.float64)[None, None, :] + np.arange(HY_N2, dtype=np.float64)[:, None, None])
    ang = 2.0 * np.pi * ((k1 * nn) % n) / n
    m1 = np.zeros((HY_N2, 2 * k1p, n1_rows), np.float32)
    m1[:, :k1n] = np.cos(ang)
    m1[:, k1p:k1p + k1n] = -np.sin(ang)
    m4 = np.transpose(m1, (0, 2, 1))
    kk = np.arange(HY_N2, dtype=np.float64)
    a2 = 2.0 * np.pi * ((kk[:, None] * kk[None, :]) % HY_N2) / HY_N2
    c, s = np.cos(a2), np.sin(a2)
    f3 = np.block([[c, s], [-s, c]]).astype(np.float32)
    f3i = np.block([[c, -s], [s, c]]).astype(np.float32)
    return jnp.asarray(m1, BF16), jnp.asarray(m4, BF16), jnp.asarray(f3, BF16), jnp.asarray(f3i, BF16)


def _hy_stage1(u_ref, a_ref, m1_ref, n1_rows, k1p):
    def body(n2, carry):
        xs = u_ref[pl.ds(n2, n1_rows, stride=HY_N2), :].astype(BF16)
        a = jnp.dot(m1_ref[n2], xs, preferred_element_type=F32)
        a_ref[pl.ds(n2, k1p, stride=2 * HY_N2), :] = a[:k1p]
        a_ref[pl.ds(HY_N2 + n2, k1p, stride=2 * HY_N2), :] = a[k1p:]
        return carry

    lax.fori_loop(0, HY_N2, body, 0, unroll=8)


def _hy_spectrum_slab(a_ref, f3_ref, k1):
    blk = a_ref[pl.ds(pl.multiple_of(k1 * 2 * HY_N2, 2 * HY_N2), 2 * HY_N2), :].astype(BF16)
    return jnp.dot(f3_ref[...], blk, preferred_element_type=F32)


def _hy_conv(u_ref, yo_ref, a_ref, y_ref, h_ref, order, m1_ref, m4_ref, f3_ref, f3i_ref, n1_rows, k1n, k1p):
    _hy_stage1(u_ref, a_ref, m1_ref, n1_rows, k1p)

    def stage2(k1, carry):
        x = _hy_spectrum_slab(a_ref, f3_ref, k1)
        h = h_ref[order, pl.ds(pl.multiple_of(k1 * 2 * HY_N2, 2 * HY_N2), 2 * HY_N2), :].astype(F32)
        xr, xi, hr, hi = x[:HY_N2], x[HY_N2:], h[:HY_N2], h[HY_N2:]
        z = jnp.concatenate([xr * hr - xi * hi, xr * hi + xi * hr], axis=0).astype(BF16)
        c = jnp.dot(f3i_ref[...], z, preferred_element_type=F32)
        y_ref[pl.ds(k1, HY_N2, stride=2 * k1p), :] = c[:HY_N2]
        y_ref[pl.ds(k1p + k1, HY_N2, stride=2 * k1p), :] = c[HY_N2:]
        return carry

    lax.fori_loop(0, k1n - 1, stage2, 0, unroll=4)
    stage2(k1n - 1, 0)

    def stage3(n2, carry):
        d = y_ref[pl.ds(pl.multiple_of(n2 * 2 * k1p, 2 * k1p), 2 * k1p), :].astype(BF16)
        yo_ref[pl.ds(n2, n1_rows, stride=HY_N2), :] = jnp.dot(m4_ref[n2], d, preferred_element_type=F32)
        return carry

    lax.fori_loop(0, HY_N2, stage3, 0, unroll=8)


def _hyena_kernel(v_ref, x1_ref, x2_ref, h_ref, m1_ref, m4_ref, f3_ref, f3i_ref, bias_ref, o_ref, a_ref, y_ref, u_ref, yo_ref, *, n1_rows, k1n, k1p):
    @pl.when(jnp.logical_and(pl.program_id(0) == 0, pl.program_id(1) == 0))
    def _():
        y_ref[...] = jnp.zeros_like(y_ref)

    u_ref[...] = v_ref[...].astype(F32)
    for order, gate_ref in enumerate((x1_ref, x2_ref)):
        _hy_conv(u_ref, yo_ref, a_ref, y_ref, h_ref, order, m1_ref, m4_ref, f3_ref, f3i_ref, n1_rows, k1n, k1p)
        z = gate_ref[...].astype(F32) * (yo_ref[...] + u_ref[...] * bias_ref[order:order + 1, :])
        if order == 0:
            u_ref[...] = z
        else:
            o_ref[...] = z.astype(o_ref.dtype)


def _hy_filter_kernel(fwd_ref, bwd_ref, m1_ref, f3_ref, o_ref, a_ref, sf_ref, ub_ref, *, n1_rows, k1n, k1p, scale_mid, scale_edge):
    slab = 2 * HY_N2
    ub_ref[...] = bwd_ref[...]
    ub_ref[0:1, :] = jnp.zeros((1, HY_CT), F32)
    energy = jnp.sum(fwd_ref[...] * fwd_ref[...], axis=0, keepdims=True) + jnp.sum(ub_ref[...] * ub_ref[...], axis=0, keepdims=True)
    norm = lax.rsqrt(energy + EPS)

    _hy_stage1(fwd_ref, a_ref, m1_ref, n1_rows, k1p)

    def keep_fwd(k1, carry):
        sf_ref[pl.ds(pl.multiple_of(k1 * slab, slab), slab), :] = _hy_spectrum_slab(a_ref, f3_ref, k1)
        return carry

    lax.fori_loop(0, k1n - 1, keep_fwd, 0, unroll=4)
    keep_fwd(k1n - 1, 0)
    _hy_stage1(ub_ref, a_ref, m1_ref, n1_rows, k1p)

    def combine(k1, carry):
        xb = _hy_spectrum_slab(a_ref, f3_ref, k1)
        xf = sf_ref[pl.ds(pl.multiple_of(k1 * slab, slab), slab), :]
        w = norm * jnp.where(jnp.logical_or(k1 == 0, k1 == k1n - 1), scale_edge, scale_mid)
        h = jnp.concatenate([xf[:HY_N2] + xb[:HY_N2], xf[HY_N2:] - xb[HY_N2:]], axis=0)
        o_ref[pl.ds(pl.multiple_of(k1 * slab, slab), slab), :] = (h * w).astype(o_ref.dtype)
        return carry

    lax.fori_loop(0, k1n - 1, combine, 0, unroll=4)
    combine(k1n - 1, 0)


def _single(block_shape, index_map):
    return pl.BlockSpec(block_shape, index_map, pipeline_mode=pl.Buffered(1))


def _hyena_filter_spectrum_pallas(taps):
    L = taps.shape[0]
    c = taps.shape[1] // (2 * HY_ORDER)
    nct = c // HY_CT
    n = 2 * L
    n1, k1n, k1p = _hy_dims(L)
    n1_rows = L // HY_N2
    m1, _, f3, _ = _hy_tables(L, n1_rows)
    kern = functools.partial(_hy_filter_kernel, n1_rows=n1_rows, k1n=k1n, k1p=k1p, scale_mid=2.0 / n, scale_edge=1.0 / n)
    return pl.pallas_call(
        kern,
        grid=(HY_ORDER, nct),
        in_specs=[
            pl.BlockSpec((L, HY_CT), lambda o, j: (0, o * nct + j)),
            pl.BlockSpec((L, HY_CT), lambda o, j: (0, (HY_ORDER + o) * nct + j)),
            _single((HY_N2, 2 * k1p, n1_rows), lambda o, j: (0, 0, 0)),
            _single((2 * HY_N2, 2 * HY_N2), lambda o, j: (0, 0)),
        ],
        out_specs=pl.BlockSpec((None, k1n * 2 * HY_N2, HY_CT), lambda o, j: (o, 0, j)),
        out_shape=jax.ShapeDtypeStruct((HY_ORDER, k1n * 2 * HY_N2, c), BF16),
        scratch_shapes=[
            pltpu.VMEM((k1p * 2 * HY_N2, HY_CT), F32),
            pltpu.VMEM((k1n * 2 * HY_N2, HY_CT), F32),
            pltpu.VMEM((L, HY_CT), F32),
        ],
        compiler_params=pltpu.CompilerParams(dimension_semantics=("arbitrary", "arbitrary"), vmem_limit_bytes=VMEM_LIMIT_HYENA),
        name="hyena_filter_dft",
    )(taps, taps, m1, f3)


def _hyena_long(hy, h_spec, bias):
    b, L, c3 = hy.shape
    c = c3 // (HY_ORDER + 1)
    nct = c // HY_CT
    n1, k1n, k1p = _hy_dims(L)
    n1_rows = L // HY_N2
    m1, m4, f3, f3i = _hy_tables(L, n1_rows)
    kern = functools.partial(_hyena_kernel, n1_rows=n1_rows, k1n=k1n, k1p=k1p)

    def col(part):
        return _single((None, L, HY_CT), lambda j, bi: (bi, 0, part * nct + j))

    return pl.pallas_call(
        kern,
        grid=(nct, b),
        in_specs=[
            col(0), col(1), col(2),
            _single((HY_ORDER, k1n * 2 * HY_N2, HY_CT), lambda j, bi: (0, 0, j)),
            _single((HY_N2, 2 * k1p, n1_rows), lambda j, bi: (0, 0, 0)),
            _single((HY_N2, n1_rows, 2 * k1p), lambda j, bi: (0, 0, 0)),
            _single((2 * HY_N2, 2 * HY_N2), lambda j, bi: (0, 0)),
            _single((2 * HY_N2, 2 * HY_N2), lambda j, bi: (0, 0)),
            pl.BlockSpec((HY_ORDER, HY_CT), lambda j, bi: (0, j)),
        ],
        out_specs=pl.BlockSpec((None, L, HY_CT), lambda j, bi: (bi, 0, j)),
        out_shape=jax.ShapeDtypeStruct((b, L, c), BF16),
        scratch_shapes=[
            pltpu.VMEM((k1p * 2 * HY_N2, HY_CT), F32),
            pltpu.VMEM((HY_N2 * 2 * k1p, HY_CT), F32),
            pltpu.VMEM((L, HY_CT), F32),
            pltpu.VMEM((L, HY_CT), F32),
        ],
        compiler_params=pltpu.CompilerParams(dimension_semantics=("arbitrary", "arbitrary"), vmem_limit_bytes=VMEM_LIMIT_HYENA),
        name="hyena_long_conv",
    )(hy, hy, hy, h_spec, m1, m4, f3, f3i, bias.astype(F32))


def _split_cols(t, sizes):
    return jnp.split(t, np.cumsum(sizes)[:-1].tolist(), axis=-1)


def _rms_norm(x, w):
    xf = x.astype(F32)
    y = xf * lax.rsqrt(jnp.mean(xf * xf, axis=-1, keepdims=True) + EPS)
    return (y * w.astype(F32)).astype(x.dtype)


def _dwconv_centred(x, w, b):
    k = w.shape[0]
    y = lax.conv_general_dilated(x, w[:, None, :].astype(x.dtype), window_strides=(1,), padding=[(k // 2, k // 2)], dimension_numbers=('NWC', 'WIO', 'NWC'), feature_group_count=x.shape[-1], precision=lax.Precision.HIGHEST)
    return y + b.astype(x.dtype)


def _flip_seq(t):
    return jnp.flip(t, axis=1)


def _to_col_major(t, rows):
    b, rest = t.shape[0], t.shape[2:]
    return jnp.swapaxes(t.reshape((b, rows, GRID_W) + rest), 1, 2).reshape((b, rows * GRID_W) + rest)


def _from_col_major(t, rows):
    b, rest = t.shape[0], t.shape[2:]
    return jnp.swapaxes(t.reshape((b, GRID_W, rows) + rest), 1, 2).reshape((b, rows * GRID_W) + rest)


def _hyena_filter_taps(L, w1, b1, w2, b2, w3, b3, freq):
    hp = lax.Precision.HIGHEST
    t = jnp.linspace(0.0, 1.0, L, dtype=F32)[:, None]
    w = 2.0 * math.pi * jnp.arange(L, dtype=F32)[:, None] / L
    bands = jnp.linspace(1e-4, HY_BANDS - 1, HY_BANDS, dtype=F32)
    feats = jnp.concatenate([t, jnp.cos(bands * w), -jnp.sin(bands * w)], axis=-1)
    h = jnp.sin(freq[0] * (jnp.dot(feats, w1, precision=hp) + b1))
    h = jnp.sin(freq[1] * (jnp.dot(h, w2, precision=hp) + b2))
    h = (jnp.dot(h, w3, precision=hp) + b3).astype(F32).reshape(L, 2, HY_ORDER, HY_WIDTH)
    max_decay = math.log(HY_DECAY_TARGET) / HY_FAST_DECAY
    min_decay = math.log(HY_DECAY_TARGET) / HY_SLOW_DECAY
    deltas = jnp.abs(jnp.linspace(min_decay, max_decay, HY_WIDTH, dtype=F32))
    return h * jnp.exp(-t * deltas)[:, None, None, :]


def _hyena_filter_time(L, w1, b1, w2, b2, w3, b3, freq):
    h = _hyena_filter_taps(L, w1, b1, w2, b2, w3, b3, freq)
    fwd, bwd = h[:, 0], h[:, 1]
    k = jnp.concatenate([fwd, jnp.zeros_like(fwd[:1]), jnp.flip(bwd[1:], axis=0)], axis=0)
    return k * lax.rsqrt(jnp.sum(k * k, axis=0, keepdims=True) + EPS)


def _long_conv(u, k_spec, bias):
    L = u.shape[1]
    uf = u.astype(F32)
    y = jnp.fft.irfft(jnp.fft.rfft(uf, n=2 * L, axis=1) * k_spec[None], n=2 * L, axis=1)[:, :L]
    return (y + uf * bias.astype(F32)).astype(u.dtype)


def _hyena_core(proj, w1, b1, w2, b2, w3, b3, freq, bias):
    L = proj.shape[1]
    v, x1, x2 = jnp.split(proj.astype(F32), HY_ORDER + 1, axis=-1)
    k_spec = jnp.fft.rfft(_hyena_filter_time(L, w1, b1, w2, b2, w3, b3, freq), axis=0)
    z = v
    for o, gate in enumerate((x1, x2)):
        z = gate * _long_conv(z, k_spec[:, o], bias[o])
    return z


def _proj(hb, wseg, out_dtype=BF16):
    return _mm(hb, wseg.astype(BF16), out_dtype)


def _mixer_branches(h_rm, w_in, lb, n_ctx, ssm_conv_w, ssm_conv_b, ssm_dt_bias, ssm_a_log, ssm_d, ssm_norm, hy_conv_w, hy_conv_b, hy_w1, hy_b1, hy_w2, hy_b2, hy_w3, hy_b3, hy_freq, hy_bias, hg_norm):
    b, t, d = h_rm.shape
    n_lat = t - n_ctx
    rows = n_lat // GRID_W
    w_z, w_xbc, w_dt, w_hy, w_q, w_f, w_i, w_g, w_gate = _split_cols(w_in, IN_SIZES)
    h2 = h_rm.reshape(b * t, d)

    def proj(w, dtype=BF16):
        return _proj(h2, w, dtype).reshape(b, t, -1)

    zero_pad = jnp.zeros((d, LANES - SSM_HEADS), F32)
    w_dt2 = jnp.concatenate([w_dt[:, :SSM_HEADS], zero_pad, w_dt[:, SSM_HEADS:], zero_pad], axis=1)
    xbc_act = _dwconv_stream(proj(w_xbc), ssm_conv_w, ssm_conv_b, n_ctx, True)
    ym = _ssd_scan(xbc_act, proj(w_dt2, F32), proj(w_z), ssm_dt_bias, ssm_a_log, ssm_d, ssm_norm, n_ctx)

    hy = _dwconv_stream(proj(w_hy), hy_conv_w, hy_conv_b, n_ctx, False)
    yh_ctx = _hyena_core(hy[:, :n_ctx], hy_w1, hy_b1, hy_w2, hy_b2, hy_w3, hy_b3, hy_freq, hy_bias).astype(BF16)
    taps = _hyena_filter_taps(n_lat, hy_w1, hy_b1, hy_w2, hy_b2, hy_w3, hy_b3, hy_freq).reshape(n_lat, 2 * HY_ORDER * HY_WIDTH)
    yh_lat = _hyena_long(hy[:, n_ctx:], _hyena_filter_spectrum_pallas(taps), hy_bias)
    yh = jnp.concatenate([yh_ctx, yh_lat], axis=1)

    h_cm = jnp.concatenate([h_rm[:, :n_ctx], _to_col_major(h_rm[:, n_ctx:], rows)], axis=1).reshape(b * t, d)

    def proj_cm(w):
        return _proj(h_cm, w).reshape(b, t, -1)

    og = _hgrn_scan(proj_cm(w_q), proj_cm(w_f), proj_cm(w_i), lb, hg_norm, n_ctx)
    og = jnp.concatenate([og[:, :n_ctx], _from_col_major(og[:, n_ctx:], rows)], axis=1)
    return ym, yh, og, proj(w_g), proj(w_gate)


ROW_TILE = 256
MOD_ROWS = SUBLANES
M_SHIFT_MIX, M_SCALE_MIX, M_GATE_MIX, M_SHIFT_FFN, M_SCALE_FFN, M_GATE_FFN = range(6)
ROW_PARAMS = pltpu.CompilerParams(dimension_semantics=("arbitrary", "arbitrary"), vmem_limit_bytes=VMEM_LIMIT)


def _rms(x):
    return x * lax.rsqrt(jnp.mean(x * x, axis=-1, keepdims=True) + EPS)


def _mrow(m_ref, r):
    return m_ref[r:r + 1, :]


def _row_spec(width):
    return pl.BlockSpec((None, ROW_TILE, width), lambda bi, i: (bi, i, 0))


def _vec_spec(width):
    return pl.BlockSpec((1, width), lambda bi, i: (0, 0))


def _mat_spec(k, n):
    return pl.BlockSpec((k, n), lambda bi, i: (0, 0))


def _mod_spec(n_ctx):
    return pl.BlockSpec((None, None, MOD_ROWS, D_MODEL), lambda bi, i: (bi, jnp.where(i < n_ctx // ROW_TILE, 0, 1), 0, 0))


def _norm_mod_kernel(x_ref, w_ref, m_ref, o_ref):
    y = _rms(x_ref[...]) * w_ref[...]
    o_ref[...] = (y * (1.0 + _mrow(m_ref, M_SCALE_MIX)) + _mrow(m_ref, M_SHIFT_MIX)).astype(o_ref.dtype)


def _norm_mod(xs, w, mods, n_ctx):
    b, t, d = xs.shape
    return pl.pallas_call(
        _norm_mod_kernel,
        grid=(b, t // ROW_TILE),
        in_specs=[_row_spec(d), _vec_spec(d), _mod_spec(n_ctx)],
        out_specs=_row_spec(d),
        out_shape=jax.ShapeDtypeStruct((b, t, d), BF16),
        compiler_params=ROW_PARAMS,
        name="norm_mod",
    )(xs, w.reshape(1, d), mods)


def _merge_kernel(ym_ref, yh_ref, og_ref, g_ref, gate_ref, x_ref, m_ref, w1_ref, w2_ref, w3_ref, wo_ref, npost_ref, npre_ref, rw_ref, rb_ref, xo_ref, h_ref, lg_ref):
    d = D_MODEL
    gg = g_ref[...].astype(F32)
    yg = (og_ref[...].astype(F32) * (gg * jax.nn.sigmoid(gg))).astype(BF16)
    gate = jax.nn.sigmoid(gate_ref[...].astype(F32))
    merged = gate[:, :d] * jnp.dot(ym_ref[...], w1_ref[...], preferred_element_type=F32)
    merged = merged + gate[:, d:2 * d] * jnp.dot(yh_ref[...], w2_ref[...], preferred_element_type=F32)
    merged = merged + gate[:, 2 * d:] * jnp.dot(yg, w3_ref[...], preferred_element_type=F32)
    mix = jnp.dot(merged.astype(BF16), wo_ref[...], preferred_element_type=F32)
    x = x_ref[...] + _mrow(m_ref, M_GATE_MIX) * (_rms(mix) * npost_ref[...])
    xo_ref[...] = x
    h = (_rms(x) * npre_ref[...] * (1.0 + _mrow(m_ref, M_SCALE_FFN)) + _mrow(m_ref, M_SHIFT_FFN)).astype(BF16)
    h_ref[...] = h
    lg_ref[...] = jnp.dot(h, rw_ref[...], preferred_element_type=F32) + rb_ref[...]


def _merge(ym, yh, og, g, gate, xs, mods, w_br_ssm, w_br_hy, w_br_hg, w_out, norm_post, norm_ffn_pre, router_w, router_b, n_ctx):
    b, t, d = xs.shape
    rw = jnp.pad(router_w, ((0, 0), (0, LANES - N_EXPERTS))).astype(BF16)
    rb = jnp.pad(router_b, (0, LANES - N_EXPERTS)).reshape(1, LANES).astype(F32)
    return pl.pallas_call(
        _merge_kernel,
        grid=(b, t // ROW_TILE),
        in_specs=[_row_spec(d), _row_spec(d), _row_spec(d), _row_spec(d), _row_spec(N_BRANCHES * d), _row_spec(d), _mod_spec(n_ctx),
                  _mat_spec(d, d), _mat_spec(d, d), _mat_spec(d, d), _mat_spec(d, d), _vec_spec(d), _vec_spec(d), _mat_spec(d, LANES), _vec_spec(LANES)],
        out_specs=[_row_spec(d), _row_spec(d), _row_spec(LANES)],
        out_shape=[jax.ShapeDtypeStruct((b, t, d), F32), jax.ShapeDtypeStruct((b, t, d), BF16), jax.ShapeDtypeStruct((b, t, LANES), F32)],
        compiler_params=ROW_PARAMS,
        name="branch_merge",
    )(ym, yh, og, g, gate, xs, mods, w_br_ssm.astype(BF16), w_br_hy.astype(BF16), w_br_hg.astype(BF16), w_out.astype(BF16),
      norm_post.reshape(1, d), norm_ffn_pre.reshape(1, d), rw, rb)


def _post_ffn_kernel(y0_ref, y1_ref, y2_ref, y3_ref, x_ref, m_ref, w_ref, o_ref):
    f = y0_ref[...].astype(F32) + y1_ref[...].astype(F32) + y2_ref[...].astype(F32) + y3_ref[...].astype(F32)
    o_ref[...] = x_ref[...] + _mrow(m_ref, M_GATE_FFN) * (_rms(f) * w_ref[...])


def _post_ffn(f4, xs, mods, norm_post, n_ctx):
    b, t, d = xs.shape

    def k_spec(k):
        return pl.BlockSpec((None, None, ROW_TILE, d), lambda bi, i: (k, bi, i, 0))

    return pl.pallas_call(
        _post_ffn_kernel,
        grid=(b, t // ROW_TILE),
        in_specs=[k_spec(k) for k in range(TOP_K)] + [_row_spec(d), _mod_spec(n_ctx), _vec_spec(d)],
        out_specs=_row_spec(d),
        out_shape=jax.ShapeDtypeStruct((b, t, d), F32),
        compiler_params=ROW_PARAMS,
        name="post_ffn",
    )(f4, f4, f4, f4, xs, mods, norm_post.reshape(1, d))


def _moe_ffn(h2, logits, w1, b1, w2, b2):
    t, d = h2.shape
    n = t * TOP_K
    n_tiles = n // MOE_BLOCK
    top_v, top_e = lax.top_k(logits, TOP_K)
    gate_w = jax.nn.softmax(top_v, axis=-1)
    flat_e = top_e.reshape(n).astype(jnp.int32)
    iota = jnp.arange(n, dtype=jnp.int32)
    _, order, sw = lax.sort((flat_e, iota, gate_w.reshape(n)), num_keys=1, is_stable=True)
    _, inv = lax.sort((order, iota), num_keys=1)
    xs = h2[order // TOP_K]
    counts = jnp.sum((flat_e[:, None] == jnp.arange(N_EXPERTS, dtype=jnp.int32)[None, :]).astype(jnp.int32), axis=0)
    end = jnp.cumsum(counts)
    start = end - counts
    first_tile = start // MOE_BLOCK
    n_items = jnp.where(counts > 0, (end - 1) // MOE_BLOCK - first_tile + 1, 0)
    items_end = jnp.cumsum(n_items)
    w = jnp.arange(n_tiles + N_EXPERTS, dtype=jnp.int32)
    valid = w < items_end[-1]
    e_w = jnp.minimum(jnp.searchsorted(items_end, w, side='right'), N_EXPERTS - 1).astype(jnp.int32)
    tile_w = first_tile[e_w] + (w - (items_end[e_w] - n_items[e_w]))
    lo = jnp.where(valid, jnp.maximum(start[e_w], tile_w * MOE_BLOCK), 0)
    hi = jnp.where(valid, jnp.minimum(end[e_w], (tile_w + 1) * MOE_BLOCK), 0)
    tile_w = jnp.where(valid, tile_w, n_tiles - 1)
    first = jnp.concatenate([jnp.ones((1,), jnp.int32), (tile_w[1:] != tile_w[:-1]).astype(jnp.int32)])
    ys = _moe_experts(xs, sw, tile_w.astype(jnp.int32), e_w, lo.astype(jnp.int32), hi.astype(jnp.int32), first, w1.astype(BF16), b1, w2.astype(BF16), b2)
    return ys[inv.reshape(t, TOP_K).T]


def kernel(x, c, ctx, c_ctx, w_mod, b_mod, norm_mix_pre, norm_mix_post, norm_ffn_pre, norm_ffn_post, w_in, ssm_conv_w, ssm_conv_b, ssm_dt_bias, ssm_a_log, ssm_d, ssm_norm, hy_conv_w, hy_conv_b, hy_w1, hy_b1, hy_w2, hy_b2, hy_w3, hy_b3, hy_freq, hy_bias, hg_lb_logits, hg_norm, w_br_ssm, w_br_hy, w_br_hg, w_out, router_w, router_b, exp_w1, exp_b1, exp_w2, exp_b2):
    hp = lax.Precision.HIGHEST
    b, n_lat, d = x.shape
    n_ctx = ctx.shape[1]
    lb = jax.nn.softmax(hg_lb_logits.astype(F32), axis=1)
    lb = jnp.cumsum(lb, axis=1) - lb[:, :1]
    silu_c = jax.nn.silu(c)
    silu_cc = jax.nn.silu(c_ctx)
    xs = jnp.concatenate([ctx, x], axis=1)
    for li in range(DEPTH):
        mx = (jnp.dot(silu_c, w_mod[li], precision=hp) + b_mod[li]).reshape(b, 1, 6, d)
        mc = jnp.broadcast_to((jnp.dot(silu_cc, w_mod[li], precision=hp) + b_mod[li]).reshape(1, 1, 6, d), (b, 1, 6, d))
        mods = jnp.pad(jnp.concatenate([mc, mx], axis=1), ((0, 0), (0, 0), (0, MOD_ROWS - 6), (0, 0)))
        h = _norm_mod(xs, norm_mix_pre[li], mods, n_ctx)
        ym, yh, og, g, gate = _mixer_branches(h, w_in[li], lb[:, li], n_ctx, ssm_conv_w[li], ssm_conv_b[li], ssm_dt_bias[li], ssm_a_log[li], ssm_d[li], ssm_norm[li], hy_conv_w[li], hy_conv_b[li], hy_w1[li], hy_b1[li], hy_w2[li], hy_b2[li], hy_w3[li], hy_b3[li], hy_freq[li], hy_bias[li], hg_norm[li])
        xs, h_ffn, logits = _merge(ym, yh, og, g, gate, xs, mods, w_br_ssm[li], w_br_hy[li], w_br_hg[li], w_out[li], norm_mix_post[li], norm_ffn_pre[li], router_w[li], router_b[li], n_ctx)
        if li == DEPTH - 1:
            xs, h_ffn, logits, n_ctx = xs[:, n_ctx:], h_ffn[:, n_ctx:], logits[:, n_ctx:], 0
        t = xs.shape[1]
        f4 = _moe_ffn(h_ffn.reshape(b * t, d), logits.reshape(b * t, LANES)[:, :N_EXPERTS], exp_w1[li], exp_b1[li], exp_w2[li], exp_b2[li])
        xs = _post_ffn(f4.reshape(TOP_K, b, t, d), xs, mods, norm_ffn_post[li], n_ctx)
    return xs
```

```python
import functools
import math

import jax
import jax.numpy as jnp
import numpy as np
from jax import lax
from jax.experimental import pallas as pl
from jax.experimental.pallas import tpu as pltpu

D_MODEL = 1024
DEPTH = 2
GRID_W = 64

SSM_HEADS = 16
SSM_HEAD_DIM = 64
SSM_INNER = SSM_HEADS * SSM_HEAD_DIM
SSM_STATE = 128
SSM_GROUPS = 4
SSD_CHUNK = 128
SSM_XBC = SSM_INNER + 2 * SSM_GROUPS * SSM_STATE

HY_WIDTH = D_MODEL
HY_ORDER = 2
HY_BANDS = 16
HY_FAST_DECAY = 0.3
HY_SLOW_DECAY = 1.5
HY_DECAY_TARGET = 1e-2

HG_HEADS = 8
HG_KDIM = 128
HG_VDIM = D_MODEL // HG_HEADS
HG_QK = HG_HEADS * HG_KDIM
HG_V = HG_HEADS * HG_VDIM
HG_CHUNK = 64
F_FLOOR = 1e-20

N_EXPERTS = 32
TOP_K = 4
D_FF = D_MODEL
SWIGLU_LIMIT = 7.0
SWIGLU_ALPHA = 1.702
MOE_BLOCK = 256

N_BRANCHES = 3
IN_SIZES = (SSM_INNER, SSM_XBC, 2 * SSM_HEADS, (HY_ORDER + 1) * HY_WIDTH, HG_QK, 2 * HG_QK, HG_V, HG_V, N_BRANCHES * D_MODEL)
EPS = 1e-6
F32 = jnp.float32
BF16 = jnp.bfloat16

LANES = 128
VMEM_LIMIT = 56 * 1024 * 1024


def _mm_kernel(a_ref, b_ref, o_ref):
    o_ref[...] = jnp.dot(a_ref[...], b_ref[...], preferred_element_type=F32).astype(o_ref.dtype)


def _mm(a, b, out_dtype=F32, tm=1024, tn=1024):
    m, k = a.shape
    n = b.shape[1]
    tm = math.gcd(m, tm)
    tn = math.gcd(n, tn)
    assert tm % SUBLANES == 0 and tn % LANES == 0, (m, n, tm, tn)
    return pl.pallas_call(
        _mm_kernel,
        grid=(n // tn, m // tm),
        in_specs=[pl.BlockSpec((tm, k), lambda j, i: (i, 0)), pl.BlockSpec((k, tn), lambda j, i: (0, j))],
        out_specs=pl.BlockSpec((tm, tn), lambda j, i: (i, j)),
        out_shape=jax.ShapeDtypeStruct((m, n), out_dtype),
        compiler_params=pltpu.CompilerParams(dimension_semantics=("arbitrary", "arbitrary"), vmem_limit_bytes=VMEM_LIMIT),
        name="dense_mm",
    )(a, b)


def _moe_kernel(tile_ref, exp_ref, lo_ref, hi_ref, first_ref, x_ref, sw_ref, w1_ref, b1_ref, w2_ref, b2_ref, o_ref):
    del exp_ref
    w = pl.program_id(0)
    lo, hi = lo_ref[w], hi_ref[w]

    @pl.when(hi > lo)
    def _():
        hh = jnp.dot(x_ref[...], w1_ref[...], preferred_element_type=F32) + b1_ref[...]
        g = jnp.minimum(hh[:, :D_FF], SWIGLU_LIMIT)
        u = jnp.clip(hh[:, D_FF:], -SWIGLU_LIMIT, SWIGLU_LIMIT)
        act = (u + 1.0) * g * jax.nn.sigmoid(SWIGLU_ALPHA * g)
        y = jnp.dot(act.astype(BF16), w2_ref[...], preferred_element_type=F32) + b2_ref[...]
        y = (y * sw_ref[...]).astype(o_ref.dtype)
        rows = tile_ref[w] * MOE_BLOCK + lax.broadcasted_iota(jnp.int32, (MOE_BLOCK, 1), 0)
        mine = jnp.logical_and(rows >= lo, rows < hi)

        @pl.when(first_ref[w] == 1)
        def _():
            o_ref[...] = jnp.where(mine, y, jnp.zeros_like(y))

        @pl.when(first_ref[w] != 1)
        def _():
            o_ref[...] = jnp.where(mine, y, o_ref[...])


def _moe_experts(xs, sw, tile_w, exp_w, lo, hi, first, w1, b1, w2, b2):
    n, d = xs.shape
    grid_spec = pltpu.PrefetchScalarGridSpec(
        num_scalar_prefetch=5,
        grid=(tile_w.shape[0],),
        in_specs=[
            pl.BlockSpec((MOE_BLOCK, d), lambda w, tl, ex, lo_, hi_, fi: (tl[w], 0)),
            pl.BlockSpec((MOE_BLOCK, 1), lambda w, tl, ex, lo_, hi_, fi: (tl[w], 0)),
            pl.BlockSpec((None, d, 2 * D_FF), lambda w, tl, ex, lo_, hi_, fi: (ex[w], 0, 0)),
            pl.BlockSpec((None, 1, 2 * D_FF), lambda w, tl, ex, lo_, hi_, fi: (ex[w], 0, 0)),
            pl.BlockSpec((None, D_FF, d), lambda w, tl, ex, lo_, hi_, fi: (ex[w], 0, 0)),
            pl.BlockSpec((None, 1, d), lambda w, tl, ex, lo_, hi_, fi: (ex[w], 0, 0)),
        ],
        out_specs=pl.BlockSpec((MOE_BLOCK, d), lambda w, tl, ex, lo_, hi_, fi: (tl[w], 0)),
    )
    return pl.pallas_call(
        _moe_kernel,
        grid_spec=grid_spec,
        out_shape=jax.ShapeDtypeStruct((n, d), BF16),
        compiler_params=pltpu.CompilerParams(dimension_semantics=("arbitrary",), vmem_limit_bytes=VMEM_LIMIT),
        name="moe_experts",
    )(tile_w, exp_w, lo, hi, first, xs, sw.reshape(n, 1), w1, b1.reshape(N_EXPERTS, 1, 2 * D_FF), w2, b2.reshape(N_EXPERTS, 1, d))


SUBLANES = 8
NEG_BIG = -1e30
HIER_LEVELS = (64, 32, 16)


def _split3(x):
    h1 = x.astype(BF16)
    r1 = x - h1.astype(F32)
    h2 = r1.astype(BF16)
    h3 = (r1 - h2.astype(F32)).astype(BF16)
    return h1, h2, h3


def _dot_nt(a, b):
    return lax.dot_general(a, b, (((1,), (1,)), ((), ())), preferred_element_type=F32)


def _gla_kernel(q_ref, a_ref, v_ref, lb_ref, *rest, reverse):
    if reverse:
        of_ref, w_ref, o_ref, st_ref = rest
    else:
        o_ref, st_ref = rest
    Q = HG_CHUNK

    @pl.when(pl.program_id(1) == 0)
    def _():
        st_ref[...] = jnp.zeros_like(st_ref)

    row = lax.broadcasted_iota(jnp.int32, (Q, Q), 0)
    col = lax.broadcasted_iota(jnp.int32, (Q, Q), 1)
    tri = jnp.where((col >= row) if reverse else (col <= row), 1.0, 0.0).astype(BF16)
    rowk = lax.broadcasted_iota(jnp.int32, (Q, HG_KDIM), 0)
    sub3 = lax.broadcasted_iota(jnp.int32, (Q // SUBLANES, SUBLANES, HG_KDIM), 1)

    lb = lb_ref[...]
    a = a_ref[...].astype(F32)
    f = lb + (1.0 - lb) * jax.nn.sigmoid(a)
    logf = jnp.log(jnp.maximum(f, F_FLOOR))
    kk = (1.0 - lb) * jax.nn.sigmoid(-a)
    g_all = sum(jnp.dot(tri, p, preferred_element_type=F32) for p in _split3(logf))
    q_all = q_ref[...].astype(F32)
    q_all = q_all * jax.nn.sigmoid(q_all)
    v_all = v_ref[...].astype(F32)
    tot = 0 if reverse else Q - 1

    for h in range(HG_HEADS):
        sl = slice(h * HG_KDIM, (h + 1) * HG_KDIM)
        g, qh, kh, vh = g_all[:, sl], q_all[:, sl], kk[:, sl], v_all[:, sl]
        vb = vh.astype(BF16)
        g_tot = g[tot:tot + 1, :]
        st = st_ref[h]
        o = _dot_nt((qh * jnp.exp(g)).astype(BF16), st.astype(BF16))
        attn = jnp.zeros((Q, Q), F32)
        for s in HIER_LEVELS:
            half = s // 2
            m_off = half if reverse else half - 1
            gref = jnp.concatenate([jnp.broadcast_to(g[b0 + m_off:b0 + m_off + 1, :], (s, HG_KDIM)) for b0 in range(0, Q, s)], axis=0)
            upper = (rowk % s) >= half
            qmask = jnp.logical_not(upper) if reverse else upper
            eq = jnp.exp(jnp.where(qmask, g - gref, NEG_BIG))
            ek = jnp.exp(jnp.where(qmask, NEG_BIG, gref - g))
            lvl = _dot_nt((qh * eq).astype(BF16), (kh * ek).astype(BF16))
            if s < Q:
                lvl = jnp.where((row // s) == (col // s), lvl, 0.0)
            attn = attn + lvl
        o = o + jnp.dot(attn.astype(BF16), vb, preferred_element_type=F32)
        shp3 = (Q // SUBLANES, SUBLANES, HG_KDIM)
        g3, q3, k3, v3 = g.reshape(shp3), qh.reshape(shp3), kh.reshape(shp3), vb.astype(F32).reshape(shp3)
        o3 = jnp.zeros(shp3, F32)
        for r in range(SUBLANES):
            if r == 0:
                p = q3 * k3
                v_r = v3
            else:
                sh = (SUBLANES - r) if reverse else r
                g_r = pltpu.roll(g3, sh, 1)
                k_r = pltpu.roll(k3, sh, 1)
                v_r = pltpu.roll(v3, sh, 1)
                valid = (sub3 + r < SUBLANES) if reverse else (sub3 >= r)
                p = q3 * k_r * jnp.exp(jnp.where(valid, g3 - g_r, NEG_BIG))
            o3 = o3 + jnp.sum(p, axis=-1, keepdims=True) * v_r
        o = o + o3.reshape(Q, HG_VDIM)
        k_st = (kh * jnp.exp(g_tot - g)).astype(BF16)
        st_ref[h] = st * jnp.exp(g_tot) + jnp.dot(vh.T.astype(BF16), k_st, preferred_element_type=F32)
        if reverse:
            o = o + of_ref[:, sl]
            o = o * lax.rsqrt(jnp.mean(o * o, axis=-1, keepdims=True) + EPS) * w_ref[:, sl]
        o_ref[:, sl] = o.astype(o_ref.dtype)


def _hgrn_scan(p_cm, lb, norm_w, n_ctx):
    b, t, _ = p_cm.shape
    col_q, col_f, col_i = 0, 1, 3
    nc, ncc = t // HG_CHUNK, n_ctx // HG_CHUNK
    blk = (None, HG_CHUNK, HG_QK)
    scratch = [pltpu.VMEM((HG_HEADS, HG_VDIM, HG_KDIM), F32)]
    params = pltpu.CompilerParams(dimension_semantics=("arbitrary", "arbitrary"), vmem_limit_bytes=VMEM_LIMIT)
    row_spec = pl.BlockSpec((1, HG_QK), lambda bi, s: (0, 0))

    def fwd_chunk(s):
        return s

    def bwd_chunk(s):
        return jnp.where(s < ncc, ncc - 1 - s, nc + ncc - 1 - s)

    o_f = pl.pallas_call(
        functools.partial(_gla_kernel, reverse=False),
        grid=(b, nc),
        in_specs=[
            pl.BlockSpec(blk, lambda bi, s: (bi, fwd_chunk(s), col_q)),
            pl.BlockSpec(blk, lambda bi, s: (bi, fwd_chunk(s), col_f)),
            pl.BlockSpec(blk, lambda bi, s: (bi, fwd_chunk(s), col_i)),
            row_spec,
        ],
        out_specs=pl.BlockSpec(blk, lambda bi, s: (bi, fwd_chunk(s), 0)),
        out_shape=jax.ShapeDtypeStruct((b, t, HG_V), F32),
        scratch_shapes=scratch,
        compiler_params=params,
        name="gla_fwd",
    )(p_cm, p_cm, p_cm, lb[0:1])
    return pl.pallas_call(
        functools.partial(_gla_kernel, reverse=True),
        grid=(b, nc),
        in_specs=[
            pl.BlockSpec(blk, lambda bi, s: (bi, bwd_chunk(s), col_q)),
            pl.BlockSpec(blk, lambda bi, s: (bi, bwd_chunk(s), col_f + 1)),
            pl.BlockSpec(blk, lambda bi, s: (bi, bwd_chunk(s), col_i)),
            row_spec,
            pl.BlockSpec(blk, lambda bi, s: (bi, bwd_chunk(s), 0)),
            row_spec,
        ],
        out_specs=pl.BlockSpec(blk, lambda bi, s: (bi, bwd_chunk(s), 0)),
        out_shape=jax.ShapeDtypeStruct((b, t, HG_V), BF16),
        scratch_shapes=scratch,
        compiler_params=params,
        name="gla_bwd",
    )(p_cm, p_cm, p_cm, lb[1:2], o_f, norm_w.reshape(1, HG_V))


CONV_TILE = 256
HALO = 16


def _dwconv_kernel(prev_ref, cur_ref, next_ref, w_ref, b_ref, o_ref, *, taps, n_ctx_tiles, n_tiles, silu):
    i = pl.program_id(1)
    first = jnp.logical_or(i == 0, i == n_ctx_tiles)
    last = jnp.logical_or(i == n_ctx_tiles - 1, i == n_tiles - 1)
    pad = taps // 2
    xp = jnp.where(first, 0.0, prev_ref[...].astype(F32))
    xn = jnp.where(last, 0.0, next_ref[...].astype(F32))
    xcat = jnp.concatenate([xp, cur_ref[...].astype(F32), xn], axis=0)
    acc = jnp.broadcast_to(b_ref[...], cur_ref.shape).astype(F32)
    for k in range(taps):
        off = HALO - pad + k
        acc = acc + w_ref[k:k + 1, :] * xcat[off:off + CONV_TILE, :]
    if silu:
        acc = acc * jax.nn.sigmoid(acc)
    o_ref[...] = acc.astype(o_ref.dtype)


def _dwconv_stream(x, w, bias, n_ctx, silu, c0=0, ct=1024):
    b, t, _ = x.shape
    taps, c = w.shape
    n_tiles = t // CONV_TILE
    hb = CONV_TILE // HALO
    n_halo = t // HALO
    j0 = c0 // ct
    kern = functools.partial(_dwconv_kernel, taps=taps, n_ctx_tiles=n_ctx // CONV_TILE, n_tiles=n_tiles, silu=silu)
    return pl.pallas_call(
        kern,
        grid=(b, n_tiles, c // ct),
        in_specs=[
            pl.BlockSpec((None, HALO, ct), lambda bi, i, j: (bi, jnp.maximum(i * hb - 1, 0), j0 + j)),
            pl.BlockSpec((None, CONV_TILE, ct), lambda bi, i, j: (bi, i, j0 + j)),
            pl.BlockSpec((None, HALO, ct), lambda bi, i, j: (bi, jnp.minimum((i + 1) * hb, n_halo - 1), j0 + j)),
            pl.BlockSpec((taps, ct), lambda bi, i, j: (0, j)),
            pl.BlockSpec((1, ct), lambda bi, i, j: (0, j)),
        ],
        out_specs=pl.BlockSpec((None, CONV_TILE, ct), lambda bi, i, j: (bi, i, j)),
        out_shape=jax.ShapeDtypeStruct((b, t, c), BF16),
        compiler_params=pltpu.CompilerParams(dimension_semantics=("arbitrary",) * 3, vmem_limit_bytes=VMEM_LIMIT),
        name="dwconv",
    )(x, x, x, w.astype(F32), bias.reshape(1, c).astype(F32))


SSM_GHEADS = SSM_HEADS // SSM_GROUPS
SSM_GP = SSM_GHEADS * SSM_HEAD_DIM


def _ssd_kernel(xbc_ref, dt_ref, dtb_ref, a_ref, *rest, reverse):
    if reverse:
        yf_ref, z_ref, dsk_ref, nw_ref, o_ref, st_ref = rest
    else:
        o_ref, st_ref = rest
    Q = SSD_CHUNK

    @pl.when(pl.program_id(1) == 0)
    def _():
        st_ref[...] = jnp.zeros_like(st_ref)

    row = lax.broadcasted_iota(jnp.int32, (Q, Q), 0)
    col = lax.broadcasted_iota(jnp.int32, (Q, Q), 1)
    keep = (col >= row) if reverse else (col <= row)
    tri = jnp.where(keep, 1.0, 0.0).astype(BF16)
    expand = jnp.where(lax.broadcasted_iota(jnp.int32, (LANES, SSM_INNER), 1) // SSM_HEAD_DIM == lax.broadcasted_iota(jnp.int32, (LANES, SSM_INNER), 0), 1.0, 0.0).astype(BF16)

    dt = jax.nn.softplus(dt_ref[...] + dtb_ref[...])
    a = dt * a_ref[...]
    cs = sum(jnp.dot(tri, p, preferred_element_type=F32) for p in _split3(a))
    tot = 0 if reverse else Q - 1
    cs_tot = cs[tot:tot + 1, :]
    cs_t = cs.T
    dt_e = jnp.dot(dt.astype(BF16), expand, preferred_element_type=F32)
    e_in = jnp.dot(jnp.exp(cs).astype(BF16), expand, preferred_element_type=F32)
    e_st = jnp.dot(jnp.exp(cs_tot - cs).astype(BF16), expand, preferred_element_type=F32)
    e_tot = jnp.dot(jnp.broadcast_to(jnp.exp(cs_tot), (SUBLANES, LANES)).astype(BF16), expand, preferred_element_type=F32)[0:1, :]

    xs = xbc_ref[:, :SSM_INNER].astype(F32)
    xd = (xs * dt_e).astype(BF16)
    xst = (xs * dt_e * e_st).astype(BF16)
    lane_lo = lax.broadcasted_iota(jnp.int32, (Q, LANES), 1) < SSM_HEAD_DIM
    ys = []
    for g in range(SSM_GROUPS):
        bm = xbc_ref[:, SSM_INNER + g * SSM_STATE:SSM_INNER + (g + 1) * SSM_STATE]
        cm = xbc_ref[:, SSM_INNER + (SSM_GROUPS + g) * SSM_STATE:SSM_INNER + (SSM_GROUPS + g + 1) * SSM_STATE]
        cb = _dot_nt(cm, bm)
        gl = slice(g * SSM_GP, (g + 1) * SSM_GP)
        st = st_ref[g]
        y_off = jnp.dot(cm, st.astype(BF16), preferred_element_type=F32) * e_in[:, gl]
        pieces = []
        for hp in range(SSM_GHEADS // 2):
            pair = []
            for h in (g * SSM_GHEADS + 2 * hp, g * SSM_GHEADS + 2 * hp + 1):
                diff = jnp.broadcast_to(cs[:, h:h + 1], (Q, Q)) - jnp.broadcast_to(cs_t[h:h + 1, :], (Q, Q))
                m = (cb * jnp.exp(jnp.where(keep, diff, NEG_BIG))).astype(BF16)
                pair.append(jnp.dot(m, xd[:, (h // 2) * LANES:(h // 2 + 1) * LANES], preferred_element_type=F32))
            pieces.append(jnp.where(lane_lo, pair[0], pair[1]))
        ys.append(jnp.concatenate(pieces, axis=1) + y_off)
        st_ref[g] = st * e_tot[:, gl] + jnp.dot(bm.astype(F32).T.astype(BF16), xst[:, gl], preferred_element_type=F32)
    y = jnp.concatenate(ys, axis=1)
    if reverse:
        y = (y + yf_ref[...] + xs * dsk_ref[...])
        zz = z_ref[...].astype(F32)
        y = y * (zz * jax.nn.sigmoid(zz))
        y = y * lax.rsqrt(jnp.mean(y * y, axis=-1, keepdims=True) + EPS) * nw_ref[...]
    o_ref[...] = y.astype(o_ref.dtype)


def _ssd_scan(xbc_act, dt2, p_rm, z_col, dt_bias, a_log, d_skip, norm_w, n_ctx):
    b, t, _ = xbc_act.shape
    nc, ncc = t // SSD_CHUNK, n_ctx // SSD_CHUNK

    def pad_heads(v):
        return jnp.pad(v.astype(F32), ((0, 0), (0, LANES - SSM_HEADS)))

    dtb = pad_heads(dt_bias.reshape(2, SSM_HEADS))
    a_neg = pad_heads(-jnp.exp(a_log.astype(F32)))
    dsk = jnp.repeat(d_skip.astype(F32), SSM_HEAD_DIM).reshape(1, SSM_INNER)
    scratch = [pltpu.VMEM((SSM_GROUPS, SSM_STATE, SSM_GP), F32)]
    params = pltpu.CompilerParams(dimension_semantics=("arbitrary", "arbitrary"), vmem_limit_bytes=VMEM_LIMIT)

    def bwd_chunk(s):
        return jnp.where(s < ncc, ncc - 1 - s, nc + ncc - 1 - s)

    def specs(chunk, d):
        return [
            pl.BlockSpec((None, SSD_CHUNK, SSM_XBC), lambda bi, s: (bi, chunk(s), 0)),
            pl.BlockSpec((None, SSD_CHUNK, LANES), lambda bi, s: (bi, chunk(s), d)),
            pl.BlockSpec((1, LANES), lambda bi, s: (0, 0)),
            pl.BlockSpec((1, LANES), lambda bi, s: (0, 0)),
        ]

    def inner_spec(chunk):
        return pl.BlockSpec((None, SSD_CHUNK, SSM_INNER), lambda bi, s: (bi, chunk(s), 0))

    row_spec = pl.BlockSpec((1, SSM_INNER), lambda bi, s: (0, 0))
    z_spec = pl.BlockSpec((None, SSD_CHUNK, SSM_INNER), lambda bi, s: (bi, bwd_chunk(s), z_col // SSM_INNER))
    y_f = pl.pallas_call(
        functools.partial(_ssd_kernel, reverse=False),
        grid=(b, nc),
        in_specs=specs(lambda s: s, 0),
        out_specs=inner_spec(lambda s: s),
        out_shape=jax.ShapeDtypeStruct((b, t, SSM_INNER), F32),
        scratch_shapes=scratch,
        compiler_params=params,
        name="ssd_fwd",
    )(xbc_act, dt2, dtb[0:1], a_neg[0:1])
    return pl.pallas_call(
        functools.partial(_ssd_kernel, reverse=True),
        grid=(b, nc),
        in_specs=specs(bwd_chunk, 1) + [inner_spec(bwd_chunk), z_spec, row_spec, row_spec],
        out_specs=inner_spec(bwd_chunk),
        out_shape=jax.ShapeDtypeStruct((b, t, SSM_INNER), BF16),
        scratch_shapes=scratch,
        compiler_params=params,
        name="ssd_bwd",
    )(xbc_act, dt2, dtb[1:2], a_neg[1:2], y_f, p_rm, dsk, norm_w.reshape(1, SSM_INNER).astype(F32))


HY_N2 = LANES
HY_CT = LANES
VMEM_LIMIT_HYENA = 60 * 1024 * 1024


def _hy_dims(L):
    n1 = 2 * L // HY_N2
    k1n = n1 // 2 + 1
    k1p = -(-k1n // SUBLANES) * SUBLANES
    return n1, k1n, k1p


def _hy_tables(L, n1_rows):
    n1, k1n, k1p = _hy_dims(L)
    n = 2 * L
    k1 = np.arange(k1n, dtype=np.float64)[None, :, None]
    nn = (HY_N2 * np.arange(n1_rows, dtype=np.float64)[None, None, :] + np.arange(HY_N2, dtype=np.float64)[:, None, None])
    ang = 2.0 * np.pi * ((k1 * nn) % n) / n
    m1 = np.zeros((HY_N2, 2 * k1p, n1_rows), np.float32)
    m1[:, :k1n] = np.cos(ang)
    m1[:, k1p:k1p + k1n] = -np.sin(ang)
    m4 = np.transpose(m1, (0, 2, 1))
    kk = np.arange(HY_N2, dtype=np.float64)
    a2 = 2.0 * np.pi * ((kk[:, None] * kk[None, :]) % HY_N2) / HY_N2
    c, s = np.cos(a2), np.sin(a2)
    f3 = np.block([[c, s], [-s, c]]).astype(np.float32)
    f3i = np.block([[c, -s], [s, c]]).astype(np.float32)
    return jnp.asarray(m1, BF16), jnp.asarray(m4, BF16), jnp.asarray(f3, BF16), jnp.asarray(f3i, BF16)


def _hy_stage1(u_ref, a_ref, m1_ref, n1_rows, k1p):
    def body(n2, carry):
        xs = u_ref[pl.ds(n2, n1_rows, stride=HY_N2), :].astype(BF16)
        a = jnp.dot(m1_ref[n2], xs, preferred_element_type=F32)
        a_ref[pl.ds(n2, k1p, stride=2 * HY_N2), :] = a[:k1p]
        a_ref[pl.ds(HY_N2 + n2, k1p, stride=2 * HY_N2), :] = a[k1p:]
        return carry

    lax.fori_loop(0, HY_N2, body, 0, unroll=8)


def _hy_spectrum_slab(a_ref, f3_ref, k1):
    blk = a_ref[pl.ds(pl.multiple_of(k1 * 2 * HY_N2, 2 * HY_N2), 2 * HY_N2), :].astype(BF16)
    return jnp.dot(f3_ref[...], blk, preferred_element_type=F32)


def _hy_conv(u_ref, yo_ref, a_ref, y_ref, h_ref, order, m1_ref, m4_ref, f3_ref, f3i_ref, n1_rows, k1n, k1p):
    _hy_stage1(u_ref, a_ref, m1_ref, n1_rows, k1p)

    def stage2(k1, carry):
        x = _hy_spectrum_slab(a_ref, f3_ref, k1)
        h = h_ref[order, pl.ds(pl.multiple_of(k1 * 2 * HY_N2, 2 * HY_N2), 2 * HY_N2), :].astype(F32)
        xr, xi, hr, hi = x[:HY_N2], x[HY_N2:], h[:HY_N2], h[HY_N2:]
        z = jnp.concatenate([xr * hr - xi * hi, xr * hi + xi * hr], axis=0).astype(BF16)
        c = jnp.dot(f3i_ref[...], z, preferred_element_type=F32)
        y_ref[pl.ds(k1, HY_N2, stride=2 * k1p), :] = c[:HY_N2]
        y_ref[pl.ds(k1p + k1, HY_N2, stride=2 * k1p), :] = c[HY_N2:]
        return carry

    lax.fori_loop(0, k1n - 1, stage2, 0, unroll=4)
    stage2(k1n - 1, 0)

    def stage3(n2, carry):
        d = y_ref[pl.ds(pl.multiple_of(n2 * 2 * k1p, 2 * k1p), 2 * k1p), :].astype(BF16)
        yo_ref[pl.ds(n2, n1_rows, stride=HY_N2), :] = jnp.dot(m4_ref[n2], d, preferred_element_type=F32)
        return carry

    lax.fori_loop(0, HY_N2, stage3, 0, unroll=8)


def _hyena_kernel(v_ref, x1_ref, x2_ref, h_ref, m1_ref, m4_ref, f3_ref, f3i_ref, bias_ref, o_ref, a_ref, y_ref, u_ref, yo_ref, *, n1_rows, k1n, k1p):
    @pl.when(jnp.logical_and(pl.program_id(0) == 0, pl.program_id(1) == 0))
    def _():
        y_ref[...] = jnp.zeros_like(y_ref)

    u_ref[...] = v_ref[...].astype(F32)
    for order, gate_ref in enumerate((x1_ref, x2_ref)):
        _hy_conv(u_ref, yo_ref, a_ref, y_ref, h_ref, order, m1_ref, m4_ref, f3_ref, f3i_ref, n1_rows, k1n, k1p)
        z = gate_ref[...].astype(F32) * (yo_ref[...] + u_ref[...] * bias_ref[order:order + 1, :])
        if order == 0:
            u_ref[...] = z
        else:
            o_ref[...] = z.astype(o_ref.dtype)


def _hy_filter_kernel(fwd_ref, bwd_ref, m1_ref, f3_ref, o_ref, a_ref, sf_ref, ub_ref, *, n1_rows, k1n, k1p, scale_mid, scale_edge):
    slab = 2 * HY_N2
    ub_ref[...] = bwd_ref[...]
    ub_ref[0:1, :] = jnp.zeros((1, HY_CT), F32)
    energy = jnp.sum(fwd_ref[...] * fwd_ref[...], axis=0, keepdims=True) + jnp.sum(ub_ref[...] * ub_ref[...], axis=0, keepdims=True)
    norm = lax.rsqrt(energy + EPS)

    _hy_stage1(fwd_ref, a_ref, m1_ref, n1_rows, k1p)

    def keep_fwd(k1, carry):
        sf_ref[pl.ds(pl.multiple_of(k1 * slab, slab), slab), :] = _hy_spectrum_slab(a_ref, f3_ref, k1)
        return carry

    lax.fori_loop(0, k1n - 1, keep_fwd, 0, unroll=4)
    keep_fwd(k1n - 1, 0)
    _hy_stage1(ub_ref, a_ref, m1_ref, n1_rows, k1p)

    def combine(k1, carry):
        xb = _hy_spectrum_slab(a_ref, f3_ref, k1)
        xf = sf_ref[pl.ds(pl.multiple_of(k1 * slab, slab), slab), :]
        w = norm * jnp.where(jnp.logical_or(k1 == 0, k1 == k1n - 1), scale_edge, scale_mid)
        h = jnp.concatenate([xf[:HY_N2] + xb[:HY_N2], xf[HY_N2:] - xb[HY_N2:]], axis=0)
        o_ref[pl.ds(pl.multiple_of(k1 * slab, slab), slab), :] = (h * w).astype(o_ref.dtype)
        return carry

    lax.fori_loop(0, k1n - 1, combine, 0, unroll=4)
    combine(k1n - 1, 0)


def _single(block_shape, index_map):
    return pl.BlockSpec(block_shape, index_map, pipeline_mode=pl.Buffered(1))


def _hyena_filter_spectrum_pallas(taps):
    L = taps.shape[0]
    c = taps.shape[1] // (2 * HY_ORDER)
    nct = c // HY_CT
    n = 2 * L
    n1, k1n, k1p = _hy_dims(L)
    n1_rows = L // HY_N2
    m1, _, f3, _ = _hy_tables(L, n1_rows)
    kern = functools.partial(_hy_filter_kernel, n1_rows=n1_rows, k1n=k1n, k1p=k1p, scale_mid=2.0 / n, scale_edge=1.0 / n)
    return pl.pallas_call(
        kern,
        grid=(HY_ORDER, nct),
        in_specs=[
            pl.BlockSpec((L, HY_CT), lambda o, j: (0, o * nct + j)),
            pl.BlockSpec((L, HY_CT), lambda o, j: (0, (HY_ORDER + o) * nct + j)),
            _single((HY_N2, 2 * k1p, n1_rows), lambda o, j: (0, 0, 0)),
            _single((2 * HY_N2, 2 * HY_N2), lambda o, j: (0, 0)),
        ],
        out_specs=pl.BlockSpec((None, k1n * 2 * HY_N2, HY_CT), lambda o, j: (o, 0, j)),
        out_shape=jax.ShapeDtypeStruct((HY_ORDER, k1n * 2 * HY_N2, c), BF16),
        scratch_shapes=[
            pltpu.VMEM((k1p * 2 * HY_N2, HY_CT), F32),
            pltpu.VMEM((k1n * 2 * HY_N2, HY_CT), F32),
            pltpu.VMEM((L, HY_CT), F32),
        ],
        compiler_params=pltpu.CompilerParams(dimension_semantics=("arbitrary", "arbitrary"), vmem_limit_bytes=VMEM_LIMIT_HYENA),
        name="hyena_filter_dft",
    )(taps, taps, m1, f3)


def _hyena_long(hy, h_spec, bias):
    b, L, c3 = hy.shape
    c = c3 // (HY_ORDER + 1)
    nct = c // HY_CT
    n1, k1n, k1p = _hy_dims(L)
    n1_rows = L // HY_N2
    m1, m4, f3, f3i = _hy_tables(L, n1_rows)
    kern = functools.partial(_hyena_kernel, n1_rows=n1_rows, k1n=k1n, k1p=k1p)

    def col(part):
        return _single((None, L, HY_CT), lambda j, bi: (bi, 0, part * nct + j))

    return pl.pallas_call(
        kern,
        grid=(nct, b),
        in_specs=[
            col(0), col(1), col(2),
            _single((HY_ORDER, k1n * 2 * HY_N2, HY_CT), lambda j, bi: (0, 0, j)),
            _single((HY_N2, 2 * k1p, n1_rows), lambda j, bi: (0, 0, 0)),
            _single((HY_N2, n1_rows, 2 * k1p), lambda j, bi: (0, 0, 0)),
            _single((2 * HY_N2, 2 * HY_N2), lambda j, bi: (0, 0)),
            _single((2 * HY_N2, 2 * HY_N2), lambda j, bi: (0, 0)),
            pl.BlockSpec((HY_ORDER, HY_CT), lambda j, bi: (0, j)),
        ],
        out_specs=pl.BlockSpec((None, L, HY_CT), lambda j, bi: (bi, 0, j)),
        out_shape=jax.ShapeDtypeStruct((b, L, c), BF16),
        scratch_shapes=[
            pltpu.VMEM((k1p * 2 * HY_N2, HY_CT), F32),
            pltpu.VMEM((HY_N2 * 2 * k1p, HY_CT), F32),
            pltpu.VMEM((L, HY_CT), F32),
            pltpu.VMEM((L, HY_CT), F32),
        ],
        compiler_params=pltpu.CompilerParams(dimension_semantics=("arbitrary", "arbitrary"), vmem_limit_bytes=VMEM_LIMIT_HYENA),
        name="hyena_long_conv",
    )(hy, hy, hy, h_spec, m1, m4, f3, f3i, bias.astype(F32))


def _split_cols(t, sizes):
    return jnp.split(t, np.cumsum(sizes)[:-1].tolist(), axis=-1)


def _rms_norm(x, w):
    xf = x.astype(F32)
    y = xf * lax.rsqrt(jnp.mean(xf * xf, axis=-1, keepdims=True) + EPS)
    return (y * w.astype(F32)).astype(x.dtype)


def _dwconv_centred(x, w, b):
    k = w.shape[0]
    y = lax.conv_general_dilated(x, w[:, None, :].astype(x.dtype), window_strides=(1,), padding=[(k // 2, k // 2)], dimension_numbers=('NWC', 'WIO', 'NWC'), feature_group_count=x.shape[-1], precision=lax.Precision.HIGHEST)
    return y + b.astype(x.dtype)


def _flip_seq(t):
    return jnp.flip(t, axis=1)


def _to_col_major(t, rows):
    b, rest = t.shape[0], t.shape[2:]
    return jnp.swapaxes(t.reshape((b, rows, GRID_W) + rest), 1, 2).reshape((b, rows * GRID_W) + rest)


def _from_col_major(t, rows):
    b, rest = t.shape[0], t.shape[2:]
    return jnp.swapaxes(t.reshape((b, GRID_W, rows) + rest), 1, 2).reshape((b, rows * GRID_W) + rest)


def _hyena_filter_taps(L, w1, b1, w2, b2, w3, b3, freq):
    hp = lax.Precision.HIGHEST
    t = jnp.linspace(0.0, 1.0, L, dtype=F32)[:, None]
    w = 2.0 * math.pi * jnp.arange(L, dtype=F32)[:, None] / L
    bands = jnp.linspace(1e-4, HY_BANDS - 1, HY_BANDS, dtype=F32)
    feats = jnp.concatenate([t, jnp.cos(bands * w), -jnp.sin(bands * w)], axis=-1)
    h = jnp.sin(freq[0] * (jnp.dot(feats, w1, precision=hp) + b1))
    h = jnp.sin(freq[1] * (jnp.dot(h, w2, precision=hp) + b2))
    h = (jnp.dot(h, w3, precision=hp) + b3).astype(F32).reshape(L, 2, HY_ORDER, HY_WIDTH)
    max_decay = math.log(HY_DECAY_TARGET) / HY_FAST_DECAY
    min_decay = math.log(HY_DECAY_TARGET) / HY_SLOW_DECAY
    deltas = jnp.abs(jnp.linspace(min_decay, max_decay, HY_WIDTH, dtype=F32))
    return h * jnp.exp(-t * deltas)[:, None, None, :]


def _hyena_filter_time(L, w1, b1, w2, b2, w3, b3, freq):
    h = _hyena_filter_taps(L, w1, b1, w2, b2, w3, b3, freq)
    fwd, bwd = h[:, 0], h[:, 1]
    k = jnp.concatenate([fwd, jnp.zeros_like(fwd[:1]), jnp.flip(bwd[1:], axis=0)], axis=0)
    return k * lax.rsqrt(jnp.sum(k * k, axis=0, keepdims=True) + EPS)


def _long_conv(u, k_spec, bias):
    L = u.shape[1]
    uf = u.astype(F32)
    y = jnp.fft.irfft(jnp.fft.rfft(uf, n=2 * L, axis=1) * k_spec[None], n=2 * L, axis=1)[:, :L]
    return (y + uf * bias.astype(F32)).astype(u.dtype)


def _hyena_core(proj, w1, b1, w2, b2, w3, b3, freq, bias):
    L = proj.shape[1]
    v, x1, x2 = jnp.split(proj.astype(F32), HY_ORDER + 1, axis=-1)
    k_spec = jnp.fft.rfft(_hyena_filter_time(L, w1, b1, w2, b2, w3, b3, freq), axis=0)
    z = v
    for o, gate in enumerate((x1, x2)):
        z = gate * _long_conv(z, k_spec[:, o], bias[o])
    return z


def _mixer_branches(h_rm, w_in, lb, n_ctx, ssm_conv_w, ssm_conv_b, ssm_dt_bias, ssm_a_log, ssm_d, ssm_norm, hy_conv_w, hy_conv_b, hy_w1, hy_b1, hy_w2, hy_b2, hy_w3, hy_b3, hy_freq, hy_bias, hg_norm):
    b, t, d = h_rm.shape
    n_lat = t - n_ctx
    rows = n_lat // GRID_W
    w_z, w_xbc, w_dt, w_hy, w_q, w_f, w_i, w_g, w_gate = _split_cols(w_in, IN_SIZES)
    h2 = h_rm.reshape(b * t, d)
    rm_parts = (w_z, w_xbc, w_hy, w_g, w_gate)
    col_z, col_xbc, col_hy, col_g, col_gate = np.cumsum([0] + [w.shape[1] for w in rm_parts[:-1]]).tolist()
    p_rm = _mm(h2, jnp.concatenate(rm_parts, axis=1).astype(BF16), BF16).reshape(b, t, -1)

    zero_pad = jnp.zeros((d, LANES - SSM_HEADS), F32)
    w_dt2 = jnp.concatenate([w_dt[:, :SSM_HEADS], zero_pad, w_dt[:, SSM_HEADS:], zero_pad], axis=1)
    dt2 = _mm(h2, w_dt2.astype(BF16), F32).reshape(b, t, 2 * LANES)
    xbc_act = _dwconv_stream(p_rm, ssm_conv_w, ssm_conv_b, n_ctx, True, c0=col_xbc)
    ym = _ssd_scan(xbc_act, dt2, p_rm, col_z, ssm_dt_bias, ssm_a_log, ssm_d, ssm_norm, n_ctx)

    hy = _dwconv_stream(p_rm, hy_conv_w, hy_conv_b, n_ctx, False, c0=col_hy)
    yh_ctx = _hyena_core(hy[:, :n_ctx], hy_w1, hy_b1, hy_w2, hy_b2, hy_w3, hy_b3, hy_freq, hy_bias).astype(BF16)
    taps = _hyena_filter_taps(n_lat, hy_w1, hy_b1, hy_w2, hy_b2, hy_w3, hy_b3, hy_freq).reshape(n_lat, 2 * HY_ORDER * HY_WIDTH)
    yh_lat = _hyena_long(hy[:, n_ctx:], _hyena_filter_spectrum_pallas(taps), hy_bias)
    yh = jnp.concatenate([yh_ctx, yh_lat], axis=1)

    h_cm = jnp.concatenate([h_rm[:, :n_ctx], _to_col_major(h_rm[:, n_ctx:], rows)], axis=1).reshape(b * t, d)

    p_cm = _mm(h_cm, jnp.concatenate([w_q, w_f, w_i], axis=1).astype(BF16), BF16).reshape(b, t, -1)
    og = _hgrn_scan(p_cm, lb, hg_norm, n_ctx)
    og = jnp.concatenate([og[:, :n_ctx], _from_col_major(og[:, n_ctx:], rows)], axis=1)
    return ym, yh, og, p_rm, col_g, col_gate


ROW_TILE = 256
MOD_ROWS = SUBLANES
M_SHIFT_MIX, M_SCALE_MIX, M_GATE_MIX, M_SHIFT_FFN, M_SCALE_FFN, M_GATE_FFN = range(6)
ROW_PARAMS = pltpu.CompilerParams(dimension_semantics=("arbitrary", "arbitrary"), vmem_limit_bytes=VMEM_LIMIT)


def _rms(x):
    return x * lax.rsqrt(jnp.mean(x * x, axis=-1, keepdims=True) + EPS)


def _mrow(m_ref, r):
    return m_ref[r:r + 1, :]


def _row_spec(width):
    return pl.BlockSpec((None, ROW_TILE, width), lambda bi, i: (bi, i, 0))


def _vec_spec(width):
    return pl.BlockSpec((1, width), lambda bi, i: (0, 0))


def _mat_spec(k, n):
    return pl.BlockSpec((k, n), lambda bi, i: (0, 0))


def _mod_spec(n_ctx):
    return pl.BlockSpec((None, None, MOD_ROWS, D_MODEL), lambda bi, i: (bi, jnp.where(i < n_ctx // ROW_TILE, 0, 1), 0, 0))


def _norm_mod_kernel(x_ref, w_ref, m_ref, o_ref):
    y = _rms(x_ref[...]) * w_ref[...]
    o_ref[...] = (y * (1.0 + _mrow(m_ref, M_SCALE_MIX)) + _mrow(m_ref, M_SHIFT_MIX)).astype(o_ref.dtype)


def _norm_mod(xs, w, mods, n_ctx):
    b, t, d = xs.shape
    return pl.pallas_call(
        _norm_mod_kernel,
        grid=(b, t // ROW_TILE),
        in_specs=[_row_spec(d), _vec_spec(d), _mod_spec(n_ctx)],
        out_specs=_row_spec(d),
        out_shape=jax.ShapeDtypeStruct((b, t, d), BF16),
        compiler_params=ROW_PARAMS,
        name="norm_mod",
    )(xs, w.reshape(1, d), mods)


def _merge_kernel(ym_ref, yh_ref, og_ref, g_ref, gm_ref, gh_ref, gg_ref, x_ref, m_ref, w1_ref, w2_ref, w3_ref, wo_ref, npost_ref, npre_ref, rw_ref, rb_ref, xo_ref, h_ref, lg_ref):
    def sig(ref):
        return jax.nn.sigmoid(ref[...].astype(F32))

    go = g_ref[...].astype(F32)
    yg = (og_ref[...].astype(F32) * (go * jax.nn.sigmoid(go))).astype(BF16)
    merged = sig(gm_ref) * jnp.dot(ym_ref[...], w1_ref[...], preferred_element_type=F32)
    merged = merged + sig(gh_ref) * jnp.dot(yh_ref[...], w2_ref[...], preferred_element_type=F32)
    merged = merged + sig(gg_ref) * jnp.dot(yg, w3_ref[...], preferred_element_type=F32)
    mix = jnp.dot(merged.astype(BF16), wo_ref[...], preferred_element_type=F32)
    x = x_ref[...] + _mrow(m_ref, M_GATE_MIX) * (_rms(mix) * npost_ref[...])
    xo_ref[...] = x
    h = (_rms(x) * npre_ref[...] * (1.0 + _mrow(m_ref, M_SCALE_FFN)) + _mrow(m_ref, M_SHIFT_FFN)).astype(BF16)
    h_ref[...] = h
    lg_ref[...] = jnp.dot(h, rw_ref[...], preferred_element_type=F32) + rb_ref[...]


def _merge(ym, yh, og, p_rm, col_g, col_gate, xs, mods, w_br_ssm, w_br_hy, w_br_hg, w_out, norm_post, norm_ffn_pre, router_w, router_b, n_ctx):
    b, t, d = xs.shape
    rw = jnp.pad(router_w, ((0, 0), (0, LANES - N_EXPERTS))).astype(BF16)
    rb = jnp.pad(router_b, (0, LANES - N_EXPERTS)).reshape(1, LANES).astype(F32)

    def col_spec(col):
        return pl.BlockSpec((None, ROW_TILE, d), lambda bi, i: (bi, i, col // d))

    return pl.pallas_call(
        _merge_kernel,
        grid=(b, t // ROW_TILE),
        in_specs=[_row_spec(d), _row_spec(d), _row_spec(d), col_spec(col_g), col_spec(col_gate), col_spec(col_gate + d), col_spec(col_gate + 2 * d), _row_spec(d), _mod_spec(n_ctx),
                  _mat_spec(d, d), _mat_spec(d, d), _mat_spec(d, d), _mat_spec(d, d), _vec_spec(d), _vec_spec(d), _mat_spec(d, LANES), _vec_spec(LANES)],
        out_specs=[_row_spec(d), _row_spec(d), _row_spec(LANES)],
        out_shape=[jax.ShapeDtypeStruct((b, t, d), F32), jax.ShapeDtypeStruct((b, t, d), BF16), jax.ShapeDtypeStruct((b, t, LANES), F32)],
        compiler_params=ROW_PARAMS,
        name="branch_merge",
    )(ym, yh, og, p_rm, p_rm, p_rm, p_rm, xs, mods, w_br_ssm.astype(BF16), w_br_hy.astype(BF16), w_br_hg.astype(BF16), w_out.astype(BF16),
      norm_post.reshape(1, d), norm_ffn_pre.reshape(1, d), rw, rb)


def _post_ffn_kernel(y0_ref, y1_ref, y2_ref, y3_ref, x_ref, m_ref, w_ref, o_ref):
    f = y0_ref[...].astype(F32) + y1_ref[...].astype(F32) + y2_ref[...].astype(F32) + y3_ref[...].astype(F32)
    o_ref[...] = x_ref[...] + _mrow(m_ref, M_GATE_FFN) * (_rms(f) * w_ref[...])


def _post_ffn(f4, xs, mods, norm_post, n_ctx, skip):
    b, _, d = xs.shape
    t = f4.shape[2]
    i0 = skip // ROW_TILE

    def k_spec(k):
        return pl.BlockSpec((None, None, ROW_TILE, d), lambda bi, i: (k, bi, i, 0))

    return pl.pallas_call(
        _post_ffn_kernel,
        grid=(b, t // ROW_TILE),
        in_specs=[k_spec(k) for k in range(TOP_K)] + [
            pl.BlockSpec((None, ROW_TILE, d), lambda bi, i: (bi, i0 + i, 0)),
            pl.BlockSpec((None, None, MOD_ROWS, d), lambda bi, i: (bi, jnp.where(i0 + i < n_ctx // ROW_TILE, 0, 1), 0, 0)),
            _vec_spec(d)],
        out_specs=_row_spec(d),
        out_shape=jax.ShapeDtypeStruct((b, t, d), F32),
        compiler_params=ROW_PARAMS,
        name="post_ffn",
    )(f4, f4, f4, f4, xs, mods, norm_post.reshape(1, d))


def _moe_ffn(h2, logits, t_per_b, skip, w1, b1, w2, b2):
    t = logits.shape[0]
    d = h2.shape[1]
    n = t * TOP_K
    n_tiles = n // MOE_BLOCK
    top_v, top_e = lax.top_k(logits, TOP_K)
    gate_w = jax.nn.softmax(top_v, axis=-1)
    flat_e = top_e.reshape(n).astype(jnp.int32)
    iota = jnp.arange(n, dtype=jnp.int32)
    _, order, sw = lax.sort((flat_e, iota, gate_w.reshape(n)), num_keys=1, is_stable=True)
    _, inv = lax.sort((order, iota), num_keys=1)
    tok = order // TOP_K
    xs = h2[tok + (tok // t_per_b + 1) * skip]
    counts = jnp.sum((flat_e[:, None] == jnp.arange(N_EXPERTS, dtype=jnp.int32)[None, :]).astype(jnp.int32), axis=0)
    end = jnp.cumsum(counts)
    start = end - counts
    first_tile = start // MOE_BLOCK
    n_items = jnp.where(counts > 0, (end - 1) // MOE_BLOCK - first_tile + 1, 0)
    items_end = jnp.cumsum(n_items)
    w = jnp.arange(n_tiles + N_EXPERTS, dtype=jnp.int32)
    valid = w < items_end[-1]
    e_w = jnp.minimum(jnp.sum((w[:, None] >= items_end[None, :]).astype(jnp.int32), axis=1), N_EXPERTS - 1)
    tile_w = first_tile[e_w] + (w - (items_end[e_w] - n_items[e_w]))
    lo = jnp.where(valid, jnp.maximum(start[e_w], tile_w * MOE_BLOCK), 0)
    hi = jnp.where(valid, jnp.minimum(end[e_w], (tile_w + 1) * MOE_BLOCK), 0)
    tile_w = jnp.where(valid, tile_w, n_tiles - 1)
    first = jnp.concatenate([jnp.ones((1,), jnp.int32), (tile_w[1:] != tile_w[:-1]).astype(jnp.int32)])
    ys = _moe_experts(xs, sw, tile_w.astype(jnp.int32), e_w, lo.astype(jnp.int32), hi.astype(jnp.int32), first, w1.astype(BF16), b1, w2.astype(BF16), b2)
    return ys[inv.reshape(t, TOP_K).T]


def kernel(x, c, ctx, c_ctx, w_mod, b_mod, norm_mix_pre, norm_mix_post, norm_ffn_pre, norm_ffn_post, w_in, ssm_conv_w, ssm_conv_b, ssm_dt_bias, ssm_a_log, ssm_d, ssm_norm, hy_conv_w, hy_conv_b, hy_w1, hy_b1, hy_w2, hy_b2, hy_w3, hy_b3, hy_freq, hy_bias, hg_lb_logits, hg_norm, w_br_ssm, w_br_hy, w_br_hg, w_out, router_w, router_b, exp_w1, exp_b1, exp_w2, exp_b2):
    hp = lax.Precision.HIGHEST
    b, n_lat, d = x.shape
    n_ctx = ctx.shape[1]
    lb = jax.nn.softmax(hg_lb_logits.astype(F32), axis=1)
    lb = jnp.cumsum(lb, axis=1) - lb[:, :1]
    silu_c = jax.nn.silu(c)
    silu_cc = jax.nn.silu(c_ctx)
    xs = jnp.concatenate([ctx, x], axis=1)
    for li in range(DEPTH):
        mx = (jnp.dot(silu_c, w_mod[li], precision=hp) + b_mod[li]).reshape(b, 1, 6, d)
        mc = jnp.broadcast_to((jnp.dot(silu_cc, w_mod[li], precision=hp) + b_mod[li]).reshape(1, 1, 6, d), (b, 1, 6, d))
        mods = jnp.pad(jnp.concatenate([mc, mx], axis=1), ((0, 0), (0, 0), (0, MOD_ROWS - 6), (0, 0)))
        h = _norm_mod(xs, norm_mix_pre[li], mods, n_ctx)
        ym, yh, og, p_rm, col_g, col_gate = _mixer_branches(h, w_in[li], lb[:, li], n_ctx, ssm_conv_w[li], ssm_conv_b[li], ssm_dt_bias[li], ssm_a_log[li], ssm_d[li], ssm_norm[li], hy_conv_w[li], hy_conv_b[li], hy_w1[li], hy_b1[li], hy_w2[li], hy_b2[li], hy_w3[li], hy_b3[li], hy_freq[li], hy_bias[li], hg_norm[li])
        xs, h_ffn, logits = _merge(ym, yh, og, p_rm, col_g, col_gate, xs, mods, w_br_ssm[li], w_br_hy[li], w_br_hg[li], w_out[li], norm_mix_post[li], norm_ffn_pre[li], router_w[li], router_b[li], n_ctx)
        skip = n_ctx if li == DEPTH - 1 else 0
        t = xs.shape[1] - skip
        f4 = _moe_ffn(h_ffn.reshape(-1, d), logits[:, skip:, :N_EXPERTS].reshape(b * t, N_EXPERTS), t, skip, exp_w1[li], exp_b1[li], exp_w2[li], exp_b2[li])
        xs = _post_ffn(f4.reshape(TOP_K, b, t, d), xs, mods, norm_ffn_post[li], n_ctx, skip)
    return xs
```

```python
import functools
import math

import jax
import jax.numpy as jnp
import numpy as np
from jax import lax
from jax.experimental import pallas as pl
from jax.experimental.pallas import tpu as pltpu

D_MODEL = 1024
DEPTH = 2
GRID_W = 64

SSM_HEADS = 16
SSM_HEAD_DIM = 64
SSM_INNER = SSM_HEADS * SSM_HEAD_DIM
SSM_STATE = 128
SSM_GROUPS = 4
SSD_CHUNK = 128
SSM_XBC = SSM_INNER + 2 * SSM_GROUPS * SSM_STATE

HY_WIDTH = D_MODEL
HY_ORDER = 2
HY_BANDS = 16
HY_FAST_DECAY = 0.3
HY_SLOW_DECAY = 1.5
HY_DECAY_TARGET = 1e-2

HG_HEADS = 8
HG_KDIM = 128
HG_VDIM = D_MODEL // HG_HEADS
HG_QK = HG_HEADS * HG_KDIM
HG_V = HG_HEADS * HG_VDIM
HG_CHUNK = 64
F_FLOOR = 1e-20

N_EXPERTS = 32
TOP_K = 4
D_FF = D_MODEL
SWIGLU_LIMIT = 7.0
SWIGLU_ALPHA = 1.702
MOE_BLOCK = 256

N_BRANCHES = 3
IN_SIZES = (SSM_INNER, SSM_XBC, 2 * SSM_HEADS, (HY_ORDER + 1) * HY_WIDTH, HG_QK, 2 * HG_QK, HG_V, HG_V, N_BRANCHES * D_MODEL)
EPS = 1e-6
F32 = jnp.float32
BF16 = jnp.bfloat16

LANES = 128
VMEM_LIMIT = 56 * 1024 * 1024


def _mm_kernel(a_ref, b_ref, o_ref):
    o_ref[...] = jnp.dot(a_ref[...], b_ref[...], preferred_element_type=F32).astype(o_ref.dtype)


def _mm(a, b, out_dtype=F32, tm=1024, tn=1024):
    m, k = a.shape
    n = b.shape[1]
    tm = math.gcd(m, tm)
    tn = math.gcd(n, tn)
    assert tm % SUBLANES == 0 and tn % LANES == 0, (m, n, tm, tn)
    return pl.pallas_call(
        _mm_kernel,
        grid=(n // tn, m // tm),
        in_specs=[pl.BlockSpec((tm, k), lambda j, i: (i, 0)), pl.BlockSpec((k, tn), lambda j, i: (0, j))],
        out_specs=pl.BlockSpec((tm, tn), lambda j, i: (i, j)),
        out_shape=jax.ShapeDtypeStruct((m, n), out_dtype),
        compiler_params=pltpu.CompilerParams(dimension_semantics=("arbitrary", "arbitrary"), vmem_limit_bytes=VMEM_LIMIT),
        name="dense_mm",
    )(a, b)


def _moe_kernel(tile_ref, exp_ref, lo_ref, hi_ref, first_ref, newexp_ref, x_ref, sw_ref, w1_ref, b1_ref, w2_ref, b2_ref, o_ref, w1b_ref, w2b_ref):
    del exp_ref
    w = pl.program_id(0)
    lo, hi = lo_ref[w], hi_ref[w]

    @pl.when(newexp_ref[w] == 1)
    def _():
        def cast_rows(i, carry):
            rows = pl.ds(pl.multiple_of(i * LANES, LANES), LANES)
            w1b_ref[rows, :] = w1_ref[rows, :].astype(BF16)
            w2b_ref[rows, :] = w2_ref[rows, :].astype(BF16)
            return carry

        lax.fori_loop(0, D_MODEL // LANES, cast_rows, 0)

    @pl.when(hi > lo)
    def _():
        hh = jnp.dot(x_ref[...], w1b_ref[...], preferred_element_type=F32) + b1_ref[...]
        g = jnp.minimum(hh[:, :D_FF], SWIGLU_LIMIT)
        u = jnp.clip(hh[:, D_FF:], -SWIGLU_LIMIT, SWIGLU_LIMIT)
        act = (u + 1.0) * g * jax.nn.sigmoid(SWIGLU_ALPHA * g)
        y = jnp.dot(act.astype(BF16), w2b_ref[...], preferred_element_type=F32) + b2_ref[...]
        y = (y * sw_ref[...]).astype(o_ref.dtype)
        rows = tile_ref[w] * MOE_BLOCK + lax.broadcasted_iota(jnp.int32, (MOE_BLOCK, 1), 0)
        mine = jnp.logical_and(rows >= lo, rows < hi)

        @pl.when(first_ref[w] == 1)
        def _():
            o_ref[...] = jnp.where(mine, y, jnp.zeros_like(y))

        @pl.when(first_ref[w] != 1)
        def _():
            o_ref[...] = jnp.where(mine, y, o_ref[...])


def _moe_experts(xs, sw, tile_w, exp_w, lo, hi, first, newexp, w1, b1, w2, b2):
    n, d = xs.shape
    grid_spec = pltpu.PrefetchScalarGridSpec(
        num_scalar_prefetch=6,
        grid=(tile_w.shape[0],),
        in_specs=[
            pl.BlockSpec((MOE_BLOCK, d), lambda w, tl, ex, lo_, hi_, fi, ne: (tl[w], 0)),
            pl.BlockSpec((MOE_BLOCK, 1), lambda w, tl, ex, lo_, hi_, fi, ne: (tl[w], 0)),
            pl.BlockSpec((None, d, 2 * D_FF), lambda w, tl, ex, lo_, hi_, fi, ne: (ex[w], 0, 0)),
            pl.BlockSpec((None, 1, 2 * D_FF), lambda w, tl, ex, lo_, hi_, fi, ne: (ex[w], 0, 0)),
            pl.BlockSpec((None, D_FF, d), lambda w, tl, ex, lo_, hi_, fi, ne: (ex[w], 0, 0)),
            pl.BlockSpec((None, 1, d), lambda w, tl, ex, lo_, hi_, fi, ne: (ex[w], 0, 0)),
        ],
        out_specs=pl.BlockSpec((MOE_BLOCK, d), lambda w, tl, ex, lo_, hi_, fi, ne: (tl[w], 0)),
        scratch_shapes=[pltpu.VMEM((d, 2 * D_FF), BF16), pltpu.VMEM((D_FF, d), BF16)],
    )
    return pl.pallas_call(
        _moe_kernel,
        grid_spec=grid_spec,
        out_shape=jax.ShapeDtypeStruct((n, d), BF16),
        compiler_params=pltpu.CompilerParams(dimension_semantics=("arbitrary",), vmem_limit_bytes=VMEM_LIMIT),
        name="moe_experts",
    )(tile_w, exp_w, lo, hi, first, newexp, xs, sw.reshape(n, 1), w1, b1.reshape(N_EXPERTS, 1, 2 * D_FF), w2, b2.reshape(N_EXPERTS, 1, d))


SUBLANES = 8
NEG_BIG = -1e30
HIER_LEVELS = (64, 32, 16, 8, 4)
LOG2E = math.log2(math.e)


def _split3(x):
    h1 = x.astype(BF16)
    r1 = x - h1.astype(F32)
    h2 = r1.astype(BF16)
    h3 = (r1 - h2.astype(F32)).astype(BF16)
    return h1, h2, h3


def _dot_nt(a, b):
    return lax.dot_general(a, b, (((1,), (1,)), ((), ())), preferred_element_type=F32)


def _gla_kernel(q_ref, a_ref, v_ref, lb_ref, *rest, reverse):
    if reverse:
        of_ref, w_ref, o_ref, st_ref, at_ref = rest
    else:
        o_ref, st_ref, at_ref = rest
    Q = HG_CHUNK

    @pl.when(pl.program_id(1) == 0)
    def _():
        st_ref[...] = jnp.zeros_like(st_ref)

    row = lax.broadcasted_iota(jnp.int32, (Q, Q), 0)
    col = lax.broadcasted_iota(jnp.int32, (Q, Q), 1)
    tri = jnp.where((col >= row) if reverse else (col <= row), 1.0, 0.0).astype(BF16)
    same_block = {s: (row // s) == (col // s) for s in HIER_LEVELS[1:] + (2, 1)}
    rowk = lax.broadcasted_iota(jnp.int32, (Q, HG_KDIM), 0)
    q_rows = {s: ((rowk % s) < s // 2) if reverse else ((rowk % s) >= s // 2) for s in HIER_LEVELS + (2,)}
    shp3 = (Q // SUBLANES, SUBLANES, HG_KDIM)
    sub3 = lax.broadcasted_iota(jnp.int32, shp3, 1)

    lb = lb_ref[...]
    a = a_ref[...].astype(F32)
    f_all = jnp.maximum(lb + (1.0 - lb) * jax.nn.sigmoid(a), F_FLOOR)
    kk = (1.0 - lb) * jax.nn.sigmoid(-a)
    g_all = sum(jnp.dot(tri, p, preferred_element_type=F32) for p in _split3(jnp.log(f_all) * LOG2E))
    q_all = q_ref[...].astype(F32)
    q_all = q_all * jax.nn.sigmoid(q_all)
    v_all = v_ref[...]
    tot = 0 if reverse else Q - 1

    def level_ref(g, s):
        half = s // 2
        m_off = half if reverse else half - 1
        if s >= 2 * SUBLANES:
            return jnp.concatenate([jnp.broadcast_to(g[b0 + m_off:b0 + m_off + 1, :], (s, HG_KDIM)) for b0 in range(0, Q, s)], axis=0)
        g3 = g.reshape(shp3)
        ref = jnp.broadcast_to(g3[:, m_off:m_off + 1, :], shp3)
        for b0 in range(s, SUBLANES, s):
            ref = jnp.where(sub3 >= b0, jnp.broadcast_to(g3[:, b0 + m_off:b0 + m_off + 1, :], shp3), ref)
        return ref.reshape(Q, HG_KDIM)

    for h in range(HG_HEADS):
        sl = slice(h * HG_KDIM, (h + 1) * HG_KDIM)
        g, qh, kh, fh = g_all[:, sl], q_all[:, sl], kk[:, sl], f_all[:, sl]
        attn = None
        for s in HIER_LEVELS:
            gref = level_ref(g, s)
            eq = jnp.exp2(jnp.where(q_rows[s], g - gref, NEG_BIG))
            ek = jnp.exp2(jnp.where(q_rows[s], NEG_BIG, gref - g))
            lvl = _dot_nt((qh * eq).astype(BF16), (kh * ek).astype(BF16))
            attn = lvl if attn is None else jnp.where(same_block[s], lvl, attn)
        lvl = _dot_nt(jnp.where(q_rows[2], qh * fh, 0.0).astype(BF16), jnp.where(q_rows[2], 0.0, kh).astype(BF16))
        attn = jnp.where(same_block[2], lvl, attn)
        attn = jnp.where(same_block[1], _dot_nt(qh.astype(BF16), kh.astype(BF16)), attn)
        at_ref[h] = attn.astype(BF16)

    for h in range(HG_HEADS):
        sl = slice(h * HG_KDIM, (h + 1) * HG_KDIM)
        g, qh, kh, vb = g_all[:, sl], q_all[:, sl], kk[:, sl], v_all[:, sl]
        g_tot = g[tot:tot + 1, :]
        st = st_ref[h]
        o = _dot_nt((qh * jnp.exp2(g)).astype(BF16), st.astype(BF16))
        o = o + jnp.dot(at_ref[h], vb, preferred_element_type=F32)
        k_st = (kh * jnp.exp2(g_tot - g)).astype(BF16)
        st_ref[h] = st * jnp.exp2(g_tot) + jnp.dot(vb.astype(F32).T.astype(BF16), k_st, preferred_element_type=F32)
        if reverse:
            o = o + of_ref[:, sl]
            o = o * lax.rsqrt(jnp.mean(o * o, axis=-1, keepdims=True) + EPS) * w_ref[:, sl]
        o_ref[:, sl] = o.astype(o_ref.dtype)


def _hgrn_scan(p_cm, lb, norm_w, n_ctx):
    b, t, _ = p_cm.shape
    col_q, col_f, col_i = 0, 1, 3
    nc, ncc = t // HG_CHUNK, n_ctx // HG_CHUNK
    blk = (None, HG_CHUNK, HG_QK)
    scratch = [pltpu.VMEM((HG_HEADS, HG_VDIM, HG_KDIM), F32), pltpu.VMEM((HG_HEADS, HG_CHUNK, HG_CHUNK), BF16)]
    params = pltpu.CompilerParams(dimension_semantics=("arbitrary", "arbitrary"), vmem_limit_bytes=VMEM_LIMIT)
    row_spec = pl.BlockSpec((1, HG_QK), lambda bi, s: (0, 0))

    def fwd_chunk(s):
        return s

    def bwd_chunk(s):
        return jnp.where(s < ncc, ncc - 1 - s, nc + ncc - 1 - s)

    o_f = pl.pallas_call(
        functools.partial(_gla_kernel, reverse=False),
        grid=(b, nc),
        in_specs=[
            pl.BlockSpec(blk, lambda bi, s: (bi, fwd_chunk(s), col_q)),
            pl.BlockSpec(blk, lambda bi, s: (bi, fwd_chunk(s), col_f)),
            pl.BlockSpec(blk, lambda bi, s: (bi, fwd_chunk(s), col_i)),
            row_spec,
        ],
        out_specs=pl.BlockSpec(blk, lambda bi, s: (bi, fwd_chunk(s), 0)),
        out_shape=jax.ShapeDtypeStruct((b, t, HG_V), F32),
        scratch_shapes=scratch,
        compiler_params=params,
        name="gla_fwd",
    )(p_cm, p_cm, p_cm, lb[0:1])
    return pl.pallas_call(
        functools.partial(_gla_kernel, reverse=True),
        grid=(b, nc),
        in_specs=[
            pl.BlockSpec(blk, lambda bi, s: (bi, bwd_chunk(s), col_q)),
            pl.BlockSpec(blk, lambda bi, s: (bi, bwd_chunk(s), col_f + 1)),
            pl.BlockSpec(blk, lambda bi, s: (bi, bwd_chunk(s), col_i)),
            row_spec,
            pl.BlockSpec(blk, lambda bi, s: (bi, bwd_chunk(s), 0)),
            row_spec,
        ],
        out_specs=pl.BlockSpec(blk, lambda bi, s: (bi, bwd_chunk(s), 0)),
        out_shape=jax.ShapeDtypeStruct((b, t, HG_V), BF16),
        scratch_shapes=scratch,
        compiler_params=params,
        name="gla_bwd",
    )(p_cm, p_cm, p_cm, lb[1:2], o_f, norm_w.reshape(1, HG_V))


CONV_TILE = 256
HALO = 16


def _dwconv_kernel(prev_ref, cur_ref, next_ref, w_ref, b_ref, o_ref, *, taps, n_ctx_tiles, n_tiles, silu):
    i = pl.program_id(1)
    first = jnp.logical_or(i == 0, i == n_ctx_tiles)
    last = jnp.logical_or(i == n_ctx_tiles - 1, i == n_tiles - 1)
    pad = taps // 2
    xp = jnp.where(first, 0.0, prev_ref[...].astype(F32))
    xn = jnp.where(last, 0.0, next_ref[...].astype(F32))
    xcat = jnp.concatenate([xp, cur_ref[...].astype(F32), xn], axis=0)
    acc = jnp.broadcast_to(b_ref[...], cur_ref.shape).astype(F32)
    for k in range(taps):
        off = HALO - pad + k
        acc = acc + w_ref[k:k + 1, :] * xcat[off:off + CONV_TILE, :]
    if silu:
        acc = acc * jax.nn.sigmoid(acc)
    o_ref[...] = acc.astype(o_ref.dtype)


def _dwconv_stream(x, w, bias, n_ctx, silu, c0=0, ct=1024):
    b, t, _ = x.shape
    taps, c = w.shape
    n_tiles = t // CONV_TILE
    hb = CONV_TILE // HALO
    n_halo = t // HALO
    j0 = c0 // ct
    kern = functools.partial(_dwconv_kernel, taps=taps, n_ctx_tiles=n_ctx // CONV_TILE, n_tiles=n_tiles, silu=silu)
    return pl.pallas_call(
        kern,
        grid=(b, n_tiles, c // ct),
        in_specs=[
            pl.BlockSpec((None, HALO, ct), lambda bi, i, j: (bi, jnp.maximum(i * hb - 1, 0), j0 + j)),
            pl.BlockSpec((None, CONV_TILE, ct), lambda bi, i, j: (bi, i, j0 + j)),
            pl.BlockSpec((None, HALO, ct), lambda bi, i, j: (bi, jnp.minimum((i + 1) * hb, n_halo - 1), j0 + j)),
            pl.BlockSpec((taps, ct), lambda bi, i, j: (0, j)),
            pl.BlockSpec((1, ct), lambda bi, i, j: (0, j)),
        ],
        out_specs=pl.BlockSpec((None, CONV_TILE, ct), lambda bi, i, j: (bi, i, j)),
        out_shape=jax.ShapeDtypeStruct((b, t, c), BF16),
        compiler_params=pltpu.CompilerParams(dimension_semantics=("arbitrary",) * 3, vmem_limit_bytes=VMEM_LIMIT),
        name="dwconv",
    )(x, x, x, w.astype(F32), bias.reshape(1, c).astype(F32))


SSM_GHEADS = SSM_HEADS // SSM_GROUPS
SSM_GP = SSM_GHEADS * SSM_HEAD_DIM


def _ssd_kernel(xbc_ref, dt_ref, dtb_ref, a_ref, *rest, reverse):
    if reverse:
        yf_ref, z_ref, dsk_ref, nw_ref, o_ref, st_ref = rest
    else:
        o_ref, st_ref = rest
    Q = SSD_CHUNK

    @pl.when(pl.program_id(1) == 0)
    def _():
        st_ref[...] = jnp.zeros_like(st_ref)

    row = lax.broadcasted_iota(jnp.int32, (Q, Q), 0)
    col = lax.broadcasted_iota(jnp.int32, (Q, Q), 1)
    keep = (col >= row) if reverse else (col <= row)
    tri = jnp.where(keep, 1.0, 0.0).astype(BF16)
    expand = jnp.where(lax.broadcasted_iota(jnp.int32, (LANES, SSM_INNER), 1) // SSM_HEAD_DIM == lax.broadcasted_iota(jnp.int32, (LANES, SSM_INNER), 0), 1.0, 0.0).astype(BF16)

    dt = jax.nn.softplus(dt_ref[...] + dtb_ref[...])
    a = dt * a_ref[...]
    cs = sum(jnp.dot(tri, p, preferred_element_type=F32) for p in _split3(a))
    tot = 0 if reverse else Q - 1
    cs_tot = cs[tot:tot + 1, :]
    cs_t = cs.T
    dt_e = jnp.dot(dt.astype(BF16), expand, preferred_element_type=F32)
    e_in = jnp.dot(jnp.exp(cs).astype(BF16), expand, preferred_element_type=F32)
    e_st = jnp.dot(jnp.exp(cs_tot - cs).astype(BF16), expand, preferred_element_type=F32)
    e_tot = jnp.dot(jnp.broadcast_to(jnp.exp(cs_tot), (SUBLANES, LANES)).astype(BF16), expand, preferred_element_type=F32)[0:1, :]

    xs = xbc_ref[:, :SSM_INNER].astype(F32)
    xd = (xs * dt_e).astype(BF16)
    xst = (xs * dt_e * e_st).astype(BF16)
    lane_lo = lax.broadcasted_iota(jnp.int32, (Q, LANES), 1) < SSM_HEAD_DIM
    ys = []
    for g in range(SSM_GROUPS):
        bm = xbc_ref[:, SSM_INNER + g * SSM_STATE:SSM_INNER + (g + 1) * SSM_STATE]
        cm = xbc_ref[:, SSM_INNER + (SSM_GROUPS + g) * SSM_STATE:SSM_INNER + (SSM_GROUPS + g + 1) * SSM_STATE]
        cb = _dot_nt(cm, bm)
        gl = slice(g * SSM_GP, (g + 1) * SSM_GP)
        st = st_ref[g]
        y_off = jnp.dot(cm, st.astype(BF16), preferred_element_type=F32) * e_in[:, gl]
        pieces = []
        for hp in range(SSM_GHEADS // 2):
            pair = []
            for h in (g * SSM_GHEADS + 2 * hp, g * SSM_GHEADS + 2 * hp + 1):
                diff = jnp.broadcast_to(cs[:, h:h + 1], (Q, Q)) - jnp.broadcast_to(cs_t[h:h + 1, :], (Q, Q))
                m = (cb * jnp.exp(jnp.where(keep, diff, NEG_BIG))).astype(BF16)
                pair.append(jnp.dot(m, xd[:, (h // 2) * LANES:(h // 2 + 1) * LANES], preferred_element_type=F32))
            pieces.append(jnp.where(lane_lo, pair[0], pair[1]))
        ys.append(jnp.concatenate(pieces, axis=1) + y_off)
        st_ref[g] = st * e_tot[:, gl] + jnp.dot(bm.astype(F32).T.astype(BF16), xst[:, gl], preferred_element_type=F32)
    y = jnp.concatenate(ys, axis=1)
    if reverse:
        y = (y + yf_ref[...] + xs * dsk_ref[...])
        zz = z_ref[...].astype(F32)
        y = y * (zz * jax.nn.sigmoid(zz))
        y = y * lax.rsqrt(jnp.mean(y * y, axis=-1, keepdims=True) + EPS) * nw_ref[...]
    o_ref[...] = y.astype(o_ref.dtype)


def _ssd_scan(xbc_act, dt2, p_rm, z_col, dt_bias, a_log, d_skip, norm_w, n_ctx):
    b, t, _ = xbc_act.shape
    nc, ncc = t // SSD_CHUNK, n_ctx // SSD_CHUNK

    def pad_heads(v):
        return jnp.pad(v.astype(F32), ((0, 0), (0, LANES - SSM_HEADS)))

    dtb = pad_heads(dt_bias.reshape(2, SSM_HEADS))
    a_neg = pad_heads(-jnp.exp(a_log.astype(F32)))
    dsk = jnp.repeat(d_skip.astype(F32), SSM_HEAD_DIM).reshape(1, SSM_INNER)
    scratch = [pltpu.VMEM((SSM_GROUPS, SSM_STATE, SSM_GP), F32)]
    params = pltpu.CompilerParams(dimension_semantics=("arbitrary", "arbitrary"), vmem_limit_bytes=VMEM_LIMIT)

    def bwd_chunk(s):
        return jnp.where(s < ncc, ncc - 1 - s, nc + ncc - 1 - s)

    def specs(chunk, d):
        return [
            pl.BlockSpec((None, SSD_CHUNK, SSM_XBC), lambda bi, s: (bi, chunk(s), 0)),
            pl.BlockSpec((None, SSD_CHUNK, LANES), lambda bi, s: (bi, chunk(s), d)),
            pl.BlockSpec((1, LANES), lambda bi, s: (0, 0)),
            pl.BlockSpec((1, LANES), lambda bi, s: (0, 0)),
        ]

    def inner_spec(chunk):
        return pl.BlockSpec((None, SSD_CHUNK, SSM_INNER), lambda bi, s: (bi, chunk(s), 0))

    row_spec = pl.BlockSpec((1, SSM_INNER), lambda bi, s: (0, 0))
    z_spec = pl.BlockSpec((None, SSD_CHUNK, SSM_INNER), lambda bi, s: (bi, bwd_chunk(s), z_col // SSM_INNER))
    y_f = pl.pallas_call(
        functools.partial(_ssd_kernel, reverse=False),
        grid=(b, nc),
        in_specs=specs(lambda s: s, 0),
        out_specs=inner_spec(lambda s: s),
        out_shape=jax.ShapeDtypeStruct((b, t, SSM_INNER), F32),
        scratch_shapes=scratch,
        compiler_params=params,
        name="ssd_fwd",
    )(xbc_act, dt2, dtb[0:1], a_neg[0:1])
    return pl.pallas_call(
        functools.partial(_ssd_kernel, reverse=True),
        grid=(b, nc),
        in_specs=specs(bwd_chunk, 1) + [inner_spec(bwd_chunk), z_spec, row_spec, row_spec],
        out_specs=inner_spec(bwd_chunk),
        out_shape=jax.ShapeDtypeStruct((b, t, SSM_INNER), BF16),
        scratch_shapes=scratch,
        compiler_params=params,
        name="ssd_bwd",
    )(xbc_act, dt2, dtb[1:2], a_neg[1:2], y_f, p_rm, dsk, norm_w.reshape(1, SSM_INNER).astype(F32))


HY_N2 = LANES
HY_CT = LANES
VMEM_LIMIT_HYENA = 60 * 1024 * 1024


def _hy_dims(L):
    n1 = 2 * L // HY_N2
    k1n = n1 // 2 + 1
    k1p = -(-k1n // SUBLANES) * SUBLANES
    return n1, k1n, k1p


def _hy_tables(L, n1_rows):
    n1, k1n, k1p = _hy_dims(L)
    n = 2 * L
    k1 = np.arange(k1n, dtype=np.float64)[None, :, None]
    nn = (HY_N2 * np.arange(n1_rows, dtype=np.float64)[None, None, :] + np.arange(HY_N2, dtype=np.float64)[:, None, None])
    ang = 2.0 * np.pi * ((k1 * nn) % n) / n
    m1 = np.zeros((HY_N2, 2 * k1p, n1_rows), np.float32)
    m1[:, :k1n] = np.cos(ang)
    m1[:, k1p:k1p + k1n] = -np.sin(ang)
    m4 = np.transpose(m1, (0, 2, 1))
    kk = np.arange(HY_N2, dtype=np.float64)
    a2 = 2.0 * np.pi * ((kk[:, None] * kk[None, :]) % HY_N2) / HY_N2
    c, s = np.cos(a2), np.sin(a2)
    f3 = np.block([[c, s], [-s, c]]).astype(np.float32)
    f3i = np.block([[c, -s], [s, c]]).astype(np.float32)
    return jnp.asarray(m1, BF16), jnp.asarray(m4, BF16), jnp.asarray(f3, BF16), jnp.asarray(f3i, BF16)


def _hy_stage1(u_ref, a_ref, m1_ref, n1_rows, k1p):
    def body(n2, carry):
        xs = u_ref[pl.ds(n2, n1_rows, stride=HY_N2), :].astype(BF16)
        a = jnp.dot(m1_ref[n2], xs, preferred_element_type=F32)
        a_ref[pl.ds(n2, k1p, stride=2 * HY_N2), :] = a[:k1p]
        a_ref[pl.ds(HY_N2 + n2, k1p, stride=2 * HY_N2), :] = a[k1p:]
        return carry

    lax.fori_loop(0, HY_N2, body, 0, unroll=8)


def _hy_spectrum_slab(a_ref, f3_ref, k1):
    blk = a_ref[pl.ds(pl.multiple_of(k1 * 2 * HY_N2, 2 * HY_N2), 2 * HY_N2), :].astype(BF16)
    return jnp.dot(f3_ref[...], blk, preferred_element_type=F32)


def _hy_conv(u_ref, yo_ref, a_ref, y_ref, h_ref, order, m1_ref, m4_ref, f3_ref, f3i_ref, n1_rows, k1n, k1p):
    _hy_stage1(u_ref, a_ref, m1_ref, n1_rows, k1p)

    def stage2(k1, carry):
        x = _hy_spectrum_slab(a_ref, f3_ref, k1)
        h = h_ref[order, pl.ds(pl.multiple_of(k1 * 2 * HY_N2, 2 * HY_N2), 2 * HY_N2), :].astype(F32)
        xr, xi, hr, hi = x[:HY_N2], x[HY_N2:], h[:HY_N2], h[HY_N2:]
        z = jnp.concatenate([xr * hr - xi * hi, xr * hi + xi * hr], axis=0).astype(BF16)
        c = jnp.dot(f3i_ref[...], z, preferred_element_type=F32)
        y_ref[pl.ds(k1, HY_N2, stride=2 * k1p), :] = c[:HY_N2]
        y_ref[pl.ds(k1p + k1, HY_N2, stride=2 * k1p), :] = c[HY_N2:]
        return carry

    lax.fori_loop(0, k1n - 1, stage2, 0, unroll=4)
    stage2(k1n - 1, 0)

    def stage3(n2, carry):
        d = y_ref[pl.ds(pl.multiple_of(n2 * 2 * k1p, 2 * k1p), 2 * k1p), :].astype(BF16)
        yo_ref[pl.ds(n2, n1_rows, stride=HY_N2), :] = jnp.dot(m4_ref[n2], d, preferred_element_type=F32)
        return carry

    lax.fori_loop(0, HY_N2, stage3, 0, unroll=8)


def _hyena_kernel(v_ref, x1_ref, x2_ref, h_ref, m1_ref, m4_ref, f3_ref, f3i_ref, bias_ref, o_ref, a_ref, y_ref, u_ref, yo_ref, *, n1_rows, k1n, k1p):
    @pl.when(jnp.logical_and(pl.program_id(0) == 0, pl.program_id(1) == 0))
    def _():
        y_ref[...] = jnp.zeros_like(y_ref)

    u_ref[...] = v_ref[...].astype(F32)
    for order, gate_ref in enumerate((x1_ref, x2_ref)):
        _hy_conv(u_ref, yo_ref, a_ref, y_ref, h_ref, order, m1_ref, m4_ref, f3_ref, f3i_ref, n1_rows, k1n, k1p)
        z = gate_ref[...].astype(F32) * (yo_ref[...] + u_ref[...] * bias_ref[order:order + 1, :])
        if order == 0:
            u_ref[...] = z
        else:
            o_ref[...] = z.astype(o_ref.dtype)


def _hy_filter_kernel(fwd_ref, bwd_ref, m1_ref, f3_ref, o_ref, a_ref, sf_ref, ub_ref, *, n1_rows, k1n, k1p, scale_mid, scale_edge):
    slab = 2 * HY_N2
    ub_ref[...] = bwd_ref[...]
    ub_ref[0:1, :] = jnp.zeros((1, HY_CT), F32)
    energy = jnp.sum(fwd_ref[...] * fwd_ref[...], axis=0, keepdims=True) + jnp.sum(ub_ref[...] * ub_ref[...], axis=0, keepdims=True)
    norm = lax.rsqrt(energy + EPS)

    _hy_stage1(fwd_ref, a_ref, m1_ref, n1_rows, k1p)

    def keep_fwd(k1, carry):
        sf_ref[pl.ds(pl.multiple_of(k1 * slab, slab), slab), :] = _hy_spectrum_slab(a_ref, f3_ref, k1)
        return carry

    lax.fori_loop(0, k1n - 1, keep_fwd, 0, unroll=4)
    keep_fwd(k1n - 1, 0)
    _hy_stage1(ub_ref, a_ref, m1_ref, n1_rows, k1p)

    def combine(k1, carry):
        xb = _hy_spectrum_slab(a_ref, f3_ref, k1)
        xf = sf_ref[pl.ds(pl.multiple_of(k1 * slab, slab), slab), :]
        w = norm * jnp.where(jnp.logical_or(k1 == 0, k1 == k1n - 1), scale_edge, scale_mid)
        h = jnp.concatenate([xf[:HY_N2] + xb[:HY_N2], xf[HY_N2:] - xb[HY_N2:]], axis=0)
        o_ref[pl.ds(pl.multiple_of(k1 * slab, slab), slab), :] = (h * w).astype(o_ref.dtype)
        return carry

    lax.fori_loop(0, k1n - 1, combine, 0, unroll=4)
    combine(k1n - 1, 0)


def _single(block_shape, index_map):
    return pl.BlockSpec(block_shape, index_map, pipeline_mode=pl.Buffered(1))


def _hyena_filter_spectrum_pallas(taps):
    L = taps.shape[0]
    c = taps.shape[1] // (2 * HY_ORDER)
    nct = c // HY_CT
    n = 2 * L
    n1, k1n, k1p = _hy_dims(L)
    n1_rows = L // HY_N2
    m1, _, f3, _ = _hy_tables(L, n1_rows)
    kern = functools.partial(_hy_filter_kernel, n1_rows=n1_rows, k1n=k1n, k1p=k1p, scale_mid=2.0 / n, scale_edge=1.0 / n)
    return pl.pallas_call(
        kern,
        grid=(HY_ORDER, nct),
        in_specs=[
            pl.BlockSpec((L, HY_CT), lambda o, j: (0, o * nct + j)),
            pl.BlockSpec((L, HY_CT), lambda o, j: (0, (HY_ORDER + o) * nct + j)),
            _single((HY_N2, 2 * k1p, n1_rows), lambda o, j: (0, 0, 0)),
            _single((2 * HY_N2, 2 * HY_N2), lambda o, j: (0, 0)),
        ],
        out_specs=pl.BlockSpec((None, k1n * 2 * HY_N2, HY_CT), lambda o, j: (o, 0, j)),
        out_shape=jax.ShapeDtypeStruct((HY_ORDER, k1n * 2 * HY_N2, c), BF16),
        scratch_shapes=[
            pltpu.VMEM((k1p * 2 * HY_N2, HY_CT), F32),
            pltpu.VMEM((k1n * 2 * HY_N2, HY_CT), F32),
            pltpu.VMEM((L, HY_CT), F32),
        ],
        compiler_params=pltpu.CompilerParams(dimension_semantics=("arbitrary", "arbitrary"), vmem_limit_bytes=VMEM_LIMIT_HYENA),
        name="hyena_filter_dft",
    )(taps, taps, m1, f3)


def _hyena_long(hy, h_spec, bias):
    b, L, c3 = hy.shape
    c = c3 // (HY_ORDER + 1)
    nct = c // HY_CT
    n1, k1n, k1p = _hy_dims(L)
    n1_rows = L // HY_N2
    m1, m4, f3, f3i = _hy_tables(L, n1_rows)
    kern = functools.partial(_hyena_kernel, n1_rows=n1_rows, k1n=k1n, k1p=k1p)

    def col(part):
        return _single((None, L, HY_CT), lambda j, bi: (bi, 0, part * nct + j))

    return pl.pallas_call(
        kern,
        grid=(nct, b),
        in_specs=[
            col(0), col(1), col(2),
            _single((HY_ORDER, k1n * 2 * HY_N2, HY_CT), lambda j, bi: (0, 0, j)),
            _single((HY_N2, 2 * k1p, n1_rows), lambda j, bi: (0, 0, 0)),
            _single((HY_N2, n1_rows, 2 * k1p), lambda j, bi: (0, 0, 0)),
            _single((2 * HY_N2, 2 * HY_N2), lambda j, bi: (0, 0)),
            _single((2 * HY_N2, 2 * HY_N2), lambda j, bi: (0, 0)),
            pl.BlockSpec((HY_ORDER, HY_CT), lambda j, bi: (0, j)),
        ],
        out_specs=pl.BlockSpec((None, L, HY_CT), lambda j, bi: (bi, 0, j)),
        out_shape=jax.ShapeDtypeStruct((b, L, c), BF16),
        scratch_shapes=[
            pltpu.VMEM((k1p * 2 * HY_N2, HY_CT), F32),
            pltpu.VMEM((HY_N2 * 2 * k1p, HY_CT), F32),
            pltpu.VMEM((L, HY_CT), F32),
            pltpu.VMEM((L, HY_CT), F32),
        ],
        compiler_params=pltpu.CompilerParams(dimension_semantics=("arbitrary", "arbitrary"), vmem_limit_bytes=VMEM_LIMIT_HYENA),
        name="hyena_long_conv",
    )(hy, hy, hy, h_spec, m1, m4, f3, f3i, bias.astype(F32))


def _split_cols(t, sizes):
    return jnp.split(t, np.cumsum(sizes)[:-1].tolist(), axis=-1)


def _rms_norm(x, w):
    xf = x.astype(F32)
    y = xf * lax.rsqrt(jnp.mean(xf * xf, axis=-1, keepdims=True) + EPS)
    return (y * w.astype(F32)).astype(x.dtype)


def _dwconv_centred(x, w, b):
    k = w.shape[0]
    y = lax.conv_general_dilated(x, w[:, None, :].astype(x.dtype), window_strides=(1,), padding=[(k // 2, k // 2)], dimension_numbers=('NWC', 'WIO', 'NWC'), feature_group_count=x.shape[-1], precision=lax.Precision.HIGHEST)
    return y + b.astype(x.dtype)


def _flip_seq(t):
    return jnp.flip(t, axis=1)


def _to_col_major(t, rows):
    b, rest = t.shape[0], t.shape[2:]
    return jnp.swapaxes(t.reshape((b, rows, GRID_W) + rest), 1, 2).reshape((b, rows * GRID_W) + rest)


def _from_col_major(t, rows):
    b, rest = t.shape[0], t.shape[2:]
    return jnp.swapaxes(t.reshape((b, GRID_W, rows) + rest), 1, 2).reshape((b, rows * GRID_W) + rest)


def _hyena_filter_taps(L, w1, b1, w2, b2, w3, b3, freq):
    hp = lax.Precision.HIGHEST
    t = jnp.linspace(0.0, 1.0, L, dtype=F32)[:, None]
    w = 2.0 * math.pi * jnp.arange(L, dtype=F32)[:, None] / L
    bands = jnp.linspace(1e-4, HY_BANDS - 1, HY_BANDS, dtype=F32)
    feats = jnp.concatenate([t, jnp.cos(bands * w), -jnp.sin(bands * w)], axis=-1)
    h = jnp.sin(freq[0] * (jnp.dot(feats, w1, precision=hp) + b1))
    h = jnp.sin(freq[1] * (jnp.dot(h, w2, precision=hp) + b2))
    h = (jnp.dot(h, w3, precision=hp) + b3).astype(F32).reshape(L, 2, HY_ORDER, HY_WIDTH)
    max_decay = math.log(HY_DECAY_TARGET) / HY_FAST_DECAY
    min_decay = math.log(HY_DECAY_TARGET) / HY_SLOW_DECAY
    deltas = jnp.abs(jnp.linspace(min_decay, max_decay, HY_WIDTH, dtype=F32))
    return h * jnp.exp(-t * deltas)[:, None, None, :]


def _hyena_filter_time(L, w1, b1, w2, b2, w3, b3, freq):
    h = _hyena_filter_taps(L, w1, b1, w2, b2, w3, b3, freq)
    fwd, bwd = h[:, 0], h[:, 1]
    k = jnp.concatenate([fwd, jnp.zeros_like(fwd[:1]), jnp.flip(bwd[1:], axis=0)], axis=0)
    return k * lax.rsqrt(jnp.sum(k * k, axis=0, keepdims=True) + EPS)


def _long_conv(u, k_spec, bias):
    L = u.shape[1]
    uf = u.astype(F32)
    y = jnp.fft.irfft(jnp.fft.rfft(uf, n=2 * L, axis=1) * k_spec[None], n=2 * L, axis=1)[:, :L]
    return (y + uf * bias.astype(F32)).astype(u.dtype)


def _hyena_core(proj, w1, b1, w2, b2, w3, b3, freq, bias):
    L = proj.shape[1]
    v, x1, x2 = jnp.split(proj.astype(F32), HY_ORDER + 1, axis=-1)
    k_spec = jnp.fft.rfft(_hyena_filter_time(L, w1, b1, w2, b2, w3, b3, freq), axis=0)
    z = v
    for o, gate in enumerate((x1, x2)):
        z = gate * _long_conv(z, k_spec[:, o], bias[o])
    return z


def _mixer_branches(h_rm, w_in, lb, n_ctx, ssm_conv_w, ssm_conv_b, ssm_dt_bias, ssm_a_log, ssm_d, ssm_norm, hy_conv_w, hy_conv_b, hy_w1, hy_b1, hy_w2, hy_b2, hy_w3, hy_b3, hy_freq, hy_bias, hg_norm):
    b, t, d = h_rm.shape
    n_lat = t - n_ctx
    rows = n_lat // GRID_W
    w_z, w_xbc, w_dt, w_hy, w_q, w_f, w_i, w_g, w_gate = _split_cols(w_in, IN_SIZES)
    h2 = h_rm.reshape(b * t, d)
    rm_parts = (w_z, w_xbc, w_hy, w_g, w_gate)
    col_z, col_xbc, col_hy, col_g, col_gate = np.cumsum([0] + [w.shape[1] for w in rm_parts[:-1]]).tolist()
    p_rm = _mm(h2, jnp.concatenate(rm_parts, axis=1).astype(BF16), BF16).reshape(b, t, -1)

    zero_pad = jnp.zeros((d, LANES - SSM_HEADS), F32)
    w_dt2 = jnp.concatenate([w_dt[:, :SSM_HEADS], zero_pad, w_dt[:, SSM_HEADS:], zero_pad], axis=1)
    dt2 = _mm(h2, w_dt2.astype(BF16), F32).reshape(b, t, 2 * LANES)
    xbc_act = _dwconv_stream(p_rm, ssm_conv_w, ssm_conv_b, n_ctx, True, c0=col_xbc)
    ym = _ssd_scan(xbc_act, dt2, p_rm, col_z, ssm_dt_bias, ssm_a_log, ssm_d, ssm_norm, n_ctx)

    hy = _dwconv_stream(p_rm, hy_conv_w, hy_conv_b, n_ctx, False, c0=col_hy)
    yh_ctx = _hyena_core(hy[:, :n_ctx], hy_w1, hy_b1, hy_w2, hy_b2, hy_w3, hy_b3, hy_freq, hy_bias).astype(BF16)
    taps = _hyena_filter_taps(n_lat, hy_w1, hy_b1, hy_w2, hy_b2, hy_w3, hy_b3, hy_freq).reshape(n_lat, 2 * HY_ORDER * HY_WIDTH)
    yh_lat = _hyena_long(hy[:, n_ctx:], _hyena_filter_spectrum_pallas(taps), hy_bias)
    yh = jnp.concatenate([yh_ctx, yh_lat], axis=1)

    h_cm = jnp.concatenate([h_rm[:, :n_ctx], _to_col_major(h_rm[:, n_ctx:], rows)], axis=1).reshape(b * t, d)

    p_cm = _mm(h_cm, jnp.concatenate([w_q, w_f, w_i], axis=1).astype(BF16), BF16).reshape(b, t, -1)
    og = _hgrn_scan(p_cm, lb, hg_norm, n_ctx)
    og = jnp.concatenate([og[:, :n_ctx], _from_col_major(og[:, n_ctx:], rows)], axis=1)
    return ym, yh, og, p_rm, col_g, col_gate


ROW_TILE = 256
MOD_ROWS = SUBLANES
M_SHIFT_MIX, M_SCALE_MIX, M_GATE_MIX, M_SHIFT_FFN, M_SCALE_FFN, M_GATE_FFN = range(6)
ROW_PARAMS = pltpu.CompilerParams(dimension_semantics=("arbitrary", "arbitrary"), vmem_limit_bytes=VMEM_LIMIT)


def _rms(x):
    return x * lax.rsqrt(jnp.mean(x * x, axis=-1, keepdims=True) + EPS)


def _mrow(m_ref, r):
    return m_ref[r:r + 1, :]


def _row_spec(width):
    return pl.BlockSpec((None, ROW_TILE, width), lambda bi, i: (bi, i, 0))


def _vec_spec(width):
    return pl.BlockSpec((1, width), lambda bi, i: (0, 0))


def _mat_spec(k, n):
    return pl.BlockSpec((k, n), lambda bi, i: (0, 0))


def _mod_spec(n_ctx):
    return pl.BlockSpec((None, None, MOD_ROWS, D_MODEL), lambda bi, i: (bi, jnp.where(i < n_ctx // ROW_TILE, 0, 1), 0, 0))


def _norm_mod_kernel(x_ref, w_ref, m_ref, o_ref):
    y = _rms(x_ref[...]) * w_ref[...]
    o_ref[...] = (y * (1.0 + _mrow(m_ref, M_SCALE_MIX)) + _mrow(m_ref, M_SHIFT_MIX)).astype(o_ref.dtype)


def _norm_mod(xs, w, mods, n_ctx):
    b, t, d = xs.shape
    return pl.pallas_call(
        _norm_mod_kernel,
        grid=(b, t // ROW_TILE),
        in_specs=[_row_spec(d), _vec_spec(d), _mod_spec(n_ctx)],
        out_specs=_row_spec(d),
        out_shape=jax.ShapeDtypeStruct((b, t, d), BF16),
        compiler_params=ROW_PARAMS,
        name="norm_mod",
    )(xs, w.reshape(1, d), mods)


def _merge_kernel(ym_ref, yh_ref, og_ref, g_ref, gm_ref, gh_ref, gg_ref, x_ref, m_ref, w1_ref, w2_ref, w3_ref, wo_ref, npost_ref, npre_ref, rw_ref, rb_ref, xo_ref, h_ref, lg_ref):
    def sig(ref):
        return jax.nn.sigmoid(ref[...].astype(F32))

    go = g_ref[...].astype(F32)
    yg = (og_ref[...].astype(F32) * (go * jax.nn.sigmoid(go))).astype(BF16)
    merged = sig(gm_ref) * jnp.dot(ym_ref[...], w1_ref[...], preferred_element_type=F32)
    merged = merged + sig(gh_ref) * jnp.dot(yh_ref[...], w2_ref[...], preferred_element_type=F32)
    merged = merged + sig(gg_ref) * jnp.dot(yg, w3_ref[...], preferred_element_type=F32)
    mix = jnp.dot(merged.astype(BF16), wo_ref[...], preferred_element_type=F32)
    x = x_ref[...] + _mrow(m_ref, M_GATE_MIX) * (_rms(mix) * npost_ref[...])
    xo_ref[...] = x
    h = (_rms(x) * npre_ref[...] * (1.0 + _mrow(m_ref, M_SCALE_FFN)) + _mrow(m_ref, M_SHIFT_FFN)).astype(BF16)
    h_ref[...] = h
    lg_ref[...] = jnp.dot(h, rw_ref[...], preferred_element_type=F32) + rb_ref[...]


def _merge(ym, yh, og, p_rm, col_g, col_gate, xs, mods, w_br_ssm, w_br_hy, w_br_hg, w_out, norm_post, norm_ffn_pre, router_w, router_b, n_ctx):
    b, t, d = xs.shape
    rw = jnp.pad(router_w, ((0, 0), (0, LANES - N_EXPERTS))).astype(BF16)
    rb = jnp.pad(router_b, (0, LANES - N_EXPERTS)).reshape(1, LANES).astype(F32)

    def col_spec(col):
        return pl.BlockSpec((None, ROW_TILE, d), lambda bi, i: (bi, i, col // d))

    return pl.pallas_call(
        _merge_kernel,
        grid=(b, t // ROW_TILE),
        in_specs=[_row_spec(d), _row_spec(d), _row_spec(d), col_spec(col_g), col_spec(col_gate), col_spec(col_gate + d), col_spec(col_gate + 2 * d), _row_spec(d), _mod_spec(n_ctx),
                  _mat_spec(d, d), _mat_spec(d, d), _mat_spec(d, d), _mat_spec(d, d), _vec_spec(d), _vec_spec(d), _mat_spec(d, LANES), _vec_spec(LANES)],
        out_specs=[_row_spec(d), _row_spec(d), _row_spec(LANES)],
        out_shape=[jax.ShapeDtypeStruct((b, t, d), F32), jax.ShapeDtypeStruct((b, t, d), BF16), jax.ShapeDtypeStruct((b, t, LANES), F32)],
        compiler_params=ROW_PARAMS,
        name="branch_merge",
    )(ym, yh, og, p_rm, p_rm, p_rm, p_rm, xs, mods, w_br_ssm.astype(BF16), w_br_hy.astype(BF16), w_br_hg.astype(BF16), w_out.astype(BF16),
      norm_post.reshape(1, d), norm_ffn_pre.reshape(1, d), rw, rb)


def _post_ffn_kernel(y0_ref, y1_ref, y2_ref, y3_ref, x_ref, m_ref, w_ref, o_ref):
    f = y0_ref[...].astype(F32) + y1_ref[...].astype(F32) + y2_ref[...].astype(F32) + y3_ref[...].astype(F32)
    o_ref[...] = x_ref[...] + _mrow(m_ref, M_GATE_FFN) * (_rms(f) * w_ref[...])


def _post_ffn(f4, xs, mods, norm_post, n_ctx, skip):
    b, _, d = xs.shape
    t = f4.shape[2]
    i0 = skip // ROW_TILE

    def k_spec(k):
        return pl.BlockSpec((None, None, ROW_TILE, d), lambda bi, i: (k, bi, i, 0))

    return pl.pallas_call(
        _post_ffn_kernel,
        grid=(b, t // ROW_TILE),
        in_specs=[k_spec(k) for k in range(TOP_K)] + [
            pl.BlockSpec((None, ROW_TILE, d), lambda bi, i: (bi, i0 + i, 0)),
            pl.BlockSpec((None, None, MOD_ROWS, d), lambda bi, i: (bi, jnp.where(i0 + i < n_ctx // ROW_TILE, 0, 1), 0, 0)),
            _vec_spec(d)],
        out_specs=_row_spec(d),
        out_shape=jax.ShapeDtypeStruct((b, t, d), F32),
        compiler_params=ROW_PARAMS,
        name="post_ffn",
    )(f4, f4, f4, f4, xs, mods, norm_post.reshape(1, d))


def _moe_ffn(h2, logits, t_per_b, skip, w1, b1, w2, b2):
    t = logits.shape[0]
    d = h2.shape[1]
    n = t * TOP_K
    n_tiles = n // MOE_BLOCK
    top_v, top_e = lax.top_k(logits, TOP_K)
    gate_w = jax.nn.softmax(top_v, axis=-1)
    flat_e = top_e.reshape(n).astype(jnp.int32)
    iota = jnp.arange(n, dtype=jnp.int32)
    _, order, sw = lax.sort((flat_e, iota, gate_w.reshape(n)), num_keys=1, is_stable=True)
    _, inv = lax.sort((order, iota), num_keys=1)
    tok = order // TOP_K
    xs = h2[tok + (tok // t_per_b + 1) * skip]
    counts = jnp.sum((flat_e[:, None] == jnp.arange(N_EXPERTS, dtype=jnp.int32)[None, :]).astype(jnp.int32), axis=0)
    end = jnp.cumsum(counts)
    start = end - counts
    first_tile = start // MOE_BLOCK
    n_items = jnp.where(counts > 0, (end - 1) // MOE_BLOCK - first_tile + 1, 0)
    items_end = jnp.cumsum(n_items)
    w = jnp.arange(n_tiles + N_EXPERTS, dtype=jnp.int32)
    valid = w < items_end[-1]
    e_w = jnp.minimum(jnp.sum((w[:, None] >= items_end[None, :]).astype(jnp.int32), axis=1), N_EXPERTS - 1)
    tile_w = first_tile[e_w] + (w - (items_end[e_w] - n_items[e_w]))
    lo = jnp.where(valid, jnp.maximum(start[e_w], tile_w * MOE_BLOCK), 0)
    hi = jnp.where(valid, jnp.minimum(end[e_w], (tile_w + 1) * MOE_BLOCK), 0)
    tile_w = jnp.where(valid, tile_w, n_tiles - 1)
    first = jnp.concatenate([jnp.ones((1,), jnp.int32), (tile_w[1:] != tile_w[:-1]).astype(jnp.int32)])
    newexp = jnp.concatenate([jnp.ones((1,), jnp.int32), (e_w[1:] != e_w[:-1]).astype(jnp.int32)])
    ys = _moe_experts(xs, sw, tile_w.astype(jnp.int32), e_w, lo.astype(jnp.int32), hi.astype(jnp.int32), first, newexp, w1, b1, w2, b2)
    return ys[inv.reshape(t, TOP_K).T]


def kernel(x, c, ctx, c_ctx, w_mod, b_mod, norm_mix_pre, norm_mix_post, norm_ffn_pre, norm_ffn_post, w_in, ssm_conv_w, ssm_conv_b, ssm_dt_bias, ssm_a_log, ssm_d, ssm_norm, hy_conv_w, hy_conv_b, hy_w1, hy_b1, hy_w2, hy_b2, hy_w3, hy_b3, hy_freq, hy_bias, hg_lb_logits, hg_norm, w_br_ssm, w_br_hy, w_br_hg, w_out, router_w, router_b, exp_w1, exp_b1, exp_w2, exp_b2):
    hp = lax.Precision.HIGHEST
    b, n_lat, d = x.shape
    n_ctx = ctx.shape[1]
    lb = jax.nn.softmax(hg_lb_logits.astype(F32), axis=1)
    lb = jnp.cumsum(lb, axis=1) - lb[:, :1]
    silu_c = jax.nn.silu(c)
    silu_cc = jax.nn.silu(c_ctx)
    xs = jnp.concatenate([ctx, x], axis=1)
    for li in range(DEPTH):
        mx = (jnp.dot(silu_c, w_mod[li], precision=hp) + b_mod[li]).reshape(b, 1, 6, d)
        mc = jnp.broadcast_to((jnp.dot(silu_cc, w_mod[li], precision=hp) + b_mod[li]).reshape(1, 1, 6, d), (b, 1, 6, d))
        mods = jnp.pad(jnp.concatenate([mc, mx], axis=1), ((0, 0), (0, 0), (0, MOD_ROWS - 6), (0, 0)))
        h = _norm_mod(xs, norm_mix_pre[li], mods, n_ctx)
        ym, yh, og, p_rm, col_g, col_gate = _mixer_branches(h, w_in[li], lb[:, li], n_ctx, ssm_conv_w[li], ssm_conv_b[li], ssm_dt_bias[li], ssm_a_log[li], ssm_d[li], ssm_norm[li], hy_conv_w[li], hy_conv_b[li], hy_w1[li], hy_b1[li], hy_w2[li], hy_b2[li], hy_w3[li], hy_b3[li], hy_freq[li], hy_bias[li], hg_norm[li])
        xs, h_ffn, logits = _merge(ym, yh, og, p_rm, col_g, col_gate, xs, mods, w_br_ssm[li], w_br_hy[li], w_br_hg[li], w_out[li], norm_mix_post[li], norm_ffn_pre[li], router_w[li], router_b[li], n_ctx)
        skip = n_ctx if li == DEPTH - 1 else 0
        t = xs.shape[1] - skip
        f4 = _moe_ffn(h_ffn.reshape(-1, d), logits[:, skip:, :N_EXPERTS].reshape(b * t, N_EXPERTS), t, skip, exp_w1[li], exp_b1[li], exp_w2[li], exp_b2[li])
        xs = _post_ffn(f4.reshape(TOP_K, b, t, d), xs, mods, norm_ffn_post[li], n_ctx, skip)
    return xs
```

```python
import functools
import math

import jax
import jax.numpy as jnp
import numpy as np
from jax import lax
from jax.experimental import pallas as pl
from jax.experimental.pallas import tpu as pltpu

D_MODEL = 1024
DEPTH = 2
GRID_W = 64

SSM_HEADS = 16
SSM_HEAD_DIM = 64
SSM_INNER = SSM_HEADS * SSM_HEAD_DIM
SSM_STATE = 128
SSM_GROUPS = 4
SSD_CHUNK = 128
SSM_XBC = SSM_INNER + 2 * SSM_GROUPS * SSM_STATE

HY_WIDTH = D_MODEL
HY_ORDER = 2
HY_BANDS = 16
HY_FAST_DECAY = 0.3
HY_SLOW_DECAY = 1.5
HY_DECAY_TARGET = 1e-2

HG_HEADS = 8
HG_KDIM = 128
HG_VDIM = D_MODEL // HG_HEADS
HG_QK = HG_HEADS * HG_KDIM
HG_V = HG_HEADS * HG_VDIM
HG_CHUNK = 64
F_FLOOR = 1e-20

N_EXPERTS = 32
TOP_K = 4
D_FF = D_MODEL
SWIGLU_LIMIT = 7.0
SWIGLU_ALPHA = 1.702
MOE_BLOCK = 256

N_BRANCHES = 3
IN_SIZES = (SSM_INNER, SSM_XBC, 2 * SSM_HEADS, (HY_ORDER + 1) * HY_WIDTH, HG_QK, 2 * HG_QK, HG_V, HG_V, N_BRANCHES * D_MODEL)
EPS = 1e-6
F32 = jnp.float32
BF16 = jnp.bfloat16

LANES = 128
VMEM_LIMIT = 56 * 1024 * 1024


def _mm_kernel(a_ref, b_ref, o_ref):
    o_ref[...] = jnp.dot(a_ref[...], b_ref[...], preferred_element_type=F32).astype(o_ref.dtype)


def _mm(a, b, out_dtype=F32, tm=1024, tn=1024):
    m, k = a.shape
    n = b.shape[1]
    tm = math.gcd(m, tm)
    tn = math.gcd(n, tn)
    assert tm % SUBLANES == 0 and tn % LANES == 0, (m, n, tm, tn)
    return pl.pallas_call(
        _mm_kernel,
        grid=(n // tn, m // tm),
        in_specs=[pl.BlockSpec((tm, k), lambda j, i: (i, 0)), pl.BlockSpec((k, tn), lambda j, i: (0, j))],
        out_specs=pl.BlockSpec((tm, tn), lambda j, i: (i, j)),
        out_shape=jax.ShapeDtypeStruct((m, n), out_dtype),
        compiler_params=pltpu.CompilerParams(dimension_semantics=("arbitrary", "arbitrary"), vmem_limit_bytes=VMEM_LIMIT),
        name="dense_mm",
    )(a, b)


def _moe_kernel(tile_ref, exp_ref, lo_ref, hi_ref, first_ref, newexp_ref, x_ref, sw_ref, w1_ref, b1_ref, w2_ref, b2_ref, o_ref, w1b_ref, w2b_ref):
    del exp_ref
    w = pl.program_id(0)
    lo, hi = lo_ref[w], hi_ref[w]

    @pl.when(newexp_ref[w] == 1)
    def _():
        def cast_rows(i, carry):
            rows = pl.ds(pl.multiple_of(i * LANES, LANES), LANES)
            w1b_ref[rows, :] = w1_ref[rows, :].astype(BF16)
            w2b_ref[rows, :] = w2_ref[rows, :].astype(BF16)
            return carry

        lax.fori_loop(0, D_MODEL // LANES, cast_rows, 0)

    @pl.when(hi > lo)
    def _():
        hh = jnp.dot(x_ref[...], w1b_ref[...], preferred_element_type=F32) + b1_ref[...]
        g = jnp.minimum(hh[:, :D_FF], SWIGLU_LIMIT)
        u = jnp.clip(hh[:, D_FF:], -SWIGLU_LIMIT, SWIGLU_LIMIT)
        act = (u + 1.0) * g * jax.nn.sigmoid(SWIGLU_ALPHA * g)
        y = jnp.dot(act.astype(BF16), w2b_ref[...], preferred_element_type=F32) + b2_ref[...]
        y = (y * sw_ref[...]).astype(o_ref.dtype)
        rows = tile_ref[w] * MOE_BLOCK + lax.broadcasted_iota(jnp.int32, (MOE_BLOCK, 1), 0)
        mine = jnp.logical_and(rows >= lo, rows < hi)

        @pl.when(first_ref[w] == 1)
        def _():
            o_ref[...] = jnp.where(mine, y, jnp.zeros_like(y))

        @pl.when(first_ref[w] != 1)
        def _():
            o_ref[...] = jnp.where(mine, y, o_ref[...])


def _moe_experts(xs, sw, tile_w, exp_w, lo, hi, first, newexp, li, w1, b1, w2, b2):
    n, d = xs.shape
    grid_spec = pltpu.PrefetchScalarGridSpec(
        num_scalar_prefetch=6,
        grid=(tile_w.shape[0],),
        in_specs=[
            pl.BlockSpec((MOE_BLOCK, d), lambda w, tl, ex, lo_, hi_, fi, ne: (tl[w], 0)),
            pl.BlockSpec((MOE_BLOCK, 1), lambda w, tl, ex, lo_, hi_, fi, ne: (tl[w], 0)),
            pl.BlockSpec((None, None, d, 2 * D_FF), lambda w, tl, ex, lo_, hi_, fi, ne: (li, ex[w], 0, 0)),
            pl.BlockSpec((None, 1, 2 * D_FF), lambda w, tl, ex, lo_, hi_, fi, ne: (ex[w], 0, 0)),
            pl.BlockSpec((None, None, D_FF, d), lambda w, tl, ex, lo_, hi_, fi, ne: (li, ex[w], 0, 0)),
            pl.BlockSpec((None, 1, d), lambda w, tl, ex, lo_, hi_, fi, ne: (ex[w], 0, 0)),
        ],
        out_specs=pl.BlockSpec((MOE_BLOCK, d), lambda w, tl, ex, lo_, hi_, fi, ne: (tl[w], 0)),
        scratch_shapes=[pltpu.VMEM((d, 2 * D_FF), BF16), pltpu.VMEM((D_FF, d), BF16)],
    )
    return pl.pallas_call(
        _moe_kernel,
        grid_spec=grid_spec,
        out_shape=jax.ShapeDtypeStruct((n, d), BF16),
        compiler_params=pltpu.CompilerParams(dimension_semantics=("arbitrary",), vmem_limit_bytes=VMEM_LIMIT),
        name="moe_experts",
    )(tile_w, exp_w, lo, hi, first, newexp, xs, sw.reshape(n, 1), w1, b1.reshape(N_EXPERTS, 1, 2 * D_FF), w2, b2.reshape(N_EXPERTS, 1, d))


SUBLANES = 8
NEG_BIG = -1e30
HIER_LEVELS = (64, 32, 16, 8, 4)
LOG2E = math.log2(math.e)


def _split3(x):
    h1 = x.astype(BF16)
    r1 = x - h1.astype(F32)
    h2 = r1.astype(BF16)
    h3 = (r1 - h2.astype(F32)).astype(BF16)
    return h1, h2, h3


def _dot_nt(a, b):
    return lax.dot_general(a, b, (((1,), (1,)), ((), ())), preferred_element_type=F32)


def _gla_kernel(q_ref, a_ref, v_ref, lb_ref, *rest, reverse):
    if reverse:
        of_ref, w_ref, o_ref, st_ref, at_ref = rest
    else:
        o_ref, st_ref, at_ref = rest
    Q = HG_CHUNK

    @pl.when(pl.program_id(1) == 0)
    def _():
        st_ref[...] = jnp.zeros_like(st_ref)

    row = lax.broadcasted_iota(jnp.int32, (Q, Q), 0)
    col = lax.broadcasted_iota(jnp.int32, (Q, Q), 1)
    tri = jnp.where((col >= row) if reverse else (col <= row), 1.0, 0.0).astype(BF16)
    same_block = {s: (row // s) == (col // s) for s in HIER_LEVELS[1:] + (2, 1)}
    rowk = lax.broadcasted_iota(jnp.int32, (Q, HG_KDIM), 0)
    q_rows = {s: ((rowk % s) < s // 2) if reverse else ((rowk % s) >= s // 2) for s in HIER_LEVELS + (2,)}
    shp3 = (Q // SUBLANES, SUBLANES, HG_KDIM)
    sub3 = lax.broadcasted_iota(jnp.int32, shp3, 1)

    lb = lb_ref[...]
    a = a_ref[...].astype(F32)
    f_all = jnp.maximum(lb + (1.0 - lb) * jax.nn.sigmoid(a), F_FLOOR)
    kk = (1.0 - lb) * jax.nn.sigmoid(-a)
    g_all = sum(jnp.dot(tri, p, preferred_element_type=F32) for p in _split3(jnp.log(f_all) * LOG2E))
    q_all = q_ref[...].astype(F32)
    q_all = q_all * jax.nn.sigmoid(q_all)
    v_all = v_ref[...]
    tot = 0 if reverse else Q - 1

    def level_ref(g, s):
        half = s // 2
        m_off = half if reverse else half - 1
        if s >= 2 * SUBLANES:
            return jnp.concatenate([jnp.broadcast_to(g[b0 + m_off:b0 + m_off + 1, :], (s, HG_KDIM)) for b0 in range(0, Q, s)], axis=0)
        g3 = g.reshape(shp3)
        ref = jnp.broadcast_to(g3[:, m_off:m_off + 1, :], shp3)
        for b0 in range(s, SUBLANES, s):
            ref = jnp.where(sub3 >= b0, jnp.broadcast_to(g3[:, b0 + m_off:b0 + m_off + 1, :], shp3), ref)
        return ref.reshape(Q, HG_KDIM)

    for h in range(HG_HEADS):
        sl = slice(h * HG_KDIM, (h + 1) * HG_KDIM)
        g, qh, kh, fh = g_all[:, sl], q_all[:, sl], kk[:, sl], f_all[:, sl]
        attn = None
        for s in HIER_LEVELS:
            gref = level_ref(g, s)
            eq = jnp.exp2(jnp.where(q_rows[s], g - gref, NEG_BIG))
            ek = jnp.exp2(jnp.where(q_rows[s], NEG_BIG, gref - g))
            lvl = _dot_nt((qh * eq).astype(BF16), (kh * ek).astype(BF16))
            attn = lvl if attn is None else jnp.where(same_block[s], lvl, attn)
        lvl = _dot_nt(jnp.where(q_rows[2], qh * fh, 0.0).astype(BF16), jnp.where(q_rows[2], 0.0, kh).astype(BF16))
        attn = jnp.where(same_block[2], lvl, attn)
        attn = jnp.where(same_block[1], _dot_nt(qh.astype(BF16), kh.astype(BF16)), attn)
        at_ref[h] = attn.astype(BF16)

    for h in range(HG_HEADS):
        sl = slice(h * HG_KDIM, (h + 1) * HG_KDIM)
        g, qh, kh, vb = g_all[:, sl], q_all[:, sl], kk[:, sl], v_all[:, sl]
        g_tot = g[tot:tot + 1, :]
        st = st_ref[h]
        o = _dot_nt((qh * jnp.exp2(g)).astype(BF16), st.astype(BF16))
        o = o + jnp.dot(at_ref[h], vb, preferred_element_type=F32)
        k_st = (kh * jnp.exp2(g_tot - g)).astype(BF16)
        st_ref[h] = st * jnp.exp2(g_tot) + jnp.dot(vb.astype(F32).T.astype(BF16), k_st, preferred_element_type=F32)
        if reverse:
            o = o + of_ref[:, sl]
            o = o * lax.rsqrt(jnp.mean(o * o, axis=-1, keepdims=True) + EPS) * w_ref[:, sl]
        o_ref[:, sl] = o.astype(o_ref.dtype)


def _hgrn_scan(p_cm, lb, norm_w, n_ctx):
    b, t, _ = p_cm.shape
    col_q, col_f, col_i = 0, 1, 3
    nc, ncc = t // HG_CHUNK, n_ctx // HG_CHUNK
    blk = (None, HG_CHUNK, HG_QK)
    scratch = [pltpu.VMEM((HG_HEADS, HG_VDIM, HG_KDIM), F32), pltpu.VMEM((HG_HEADS, HG_CHUNK, HG_CHUNK), BF16)]
    params = pltpu.CompilerParams(dimension_semantics=("arbitrary", "arbitrary"), vmem_limit_bytes=VMEM_LIMIT)
    row_spec = pl.BlockSpec((1, HG_QK), lambda bi, s: (0, 0))

    def fwd_chunk(s):
        return s

    def bwd_chunk(s):
        return jnp.where(s < ncc, ncc - 1 - s, nc + ncc - 1 - s)

    o_f = pl.pallas_call(
        functools.partial(_gla_kernel, reverse=False),
        grid=(b, nc),
        in_specs=[
            pl.BlockSpec(blk, lambda bi, s: (bi, fwd_chunk(s), col_q)),
            pl.BlockSpec(blk, lambda bi, s: (bi, fwd_chunk(s), col_f)),
            pl.BlockSpec(blk, lambda bi, s: (bi, fwd_chunk(s), col_i)),
            row_spec,
        ],
        out_specs=pl.BlockSpec(blk, lambda bi, s: (bi, fwd_chunk(s), 0)),
        out_shape=jax.ShapeDtypeStruct((b, t, HG_V), F32),
        scratch_shapes=scratch,
        compiler_params=params,
        name="gla_fwd",
    )(p_cm, p_cm, p_cm, lb[0:1])
    return pl.pallas_call(
        functools.partial(_gla_kernel, reverse=True),
        grid=(b, nc),
        in_specs=[
            pl.BlockSpec(blk, lambda bi, s: (bi, bwd_chunk(s), col_q)),
            pl.BlockSpec(blk, lambda bi, s: (bi, bwd_chunk(s), col_f + 1)),
            pl.BlockSpec(blk, lambda bi, s: (bi, bwd_chunk(s), col_i)),
            row_spec,
            pl.BlockSpec(blk, lambda bi, s: (bi, bwd_chunk(s), 0)),
            row_spec,
        ],
        out_specs=pl.BlockSpec(blk, lambda bi, s: (bi, bwd_chunk(s), 0)),
        out_shape=jax.ShapeDtypeStruct((b, t, HG_V), BF16),
        scratch_shapes=scratch,
        compiler_params=params,
        name="gla_bwd",
    )(p_cm, p_cm, p_cm, lb[1:2], o_f, norm_w.reshape(1, HG_V))


CONV_TILE = 256
HALO = 16


def _dwconv_kernel(prev_ref, cur_ref, next_ref, w_ref, b_ref, o_ref, *, taps, n_ctx_tiles, n_tiles, silu):
    i = pl.program_id(1)
    first = jnp.logical_or(i == 0, i == n_ctx_tiles)
    last = jnp.logical_or(i == n_ctx_tiles - 1, i == n_tiles - 1)
    pad = taps // 2
    xp = jnp.where(first, 0.0, prev_ref[...].astype(F32))
    xn = jnp.where(last, 0.0, next_ref[...].astype(F32))
    xcat = jnp.concatenate([xp, cur_ref[...].astype(F32), xn], axis=0)
    acc = jnp.broadcast_to(b_ref[...], cur_ref.shape).astype(F32)
    for k in range(taps):
        off = HALO - pad + k
        acc = acc + w_ref[k:k + 1, :] * xcat[off:off + CONV_TILE, :]
    if silu:
        acc = acc * jax.nn.sigmoid(acc)
    o_ref[...] = acc.astype(o_ref.dtype)


def _dwconv_stream(x, w, bias, n_ctx, silu, c0=0, ct=1024):
    b, t, _ = x.shape
    taps, c = w.shape
    n_tiles = t // CONV_TILE
    hb = CONV_TILE // HALO
    n_halo = t // HALO
    j0 = c0 // ct
    kern = functools.partial(_dwconv_kernel, taps=taps, n_ctx_tiles=n_ctx // CONV_TILE, n_tiles=n_tiles, silu=silu)
    return pl.pallas_call(
        kern,
        grid=(b, n_tiles, c // ct),
        in_specs=[
            pl.BlockSpec((None, HALO, ct), lambda bi, i, j: (bi, jnp.maximum(i * hb - 1, 0), j0 + j)),
            pl.BlockSpec((None, CONV_TILE, ct), lambda bi, i, j: (bi, i, j0 + j)),
            pl.BlockSpec((None, HALO, ct), lambda bi, i, j: (bi, jnp.minimum((i + 1) * hb, n_halo - 1), j0 + j)),
            pl.BlockSpec((taps, ct), lambda bi, i, j: (0, j)),
            pl.BlockSpec((1, ct), lambda bi, i, j: (0, j)),
        ],
        out_specs=pl.BlockSpec((None, CONV_TILE, ct), lambda bi, i, j: (bi, i, j)),
        out_shape=jax.ShapeDtypeStruct((b, t, c), BF16),
        compiler_params=pltpu.CompilerParams(dimension_semantics=("arbitrary",) * 3, vmem_limit_bytes=VMEM_LIMIT),
        name="dwconv",
    )(x, x, x, w.astype(F32), bias.reshape(1, c).astype(F32))


SSM_GHEADS = SSM_HEADS // SSM_GROUPS
SSM_GP = SSM_GHEADS * SSM_HEAD_DIM


def _ssd_kernel(xbc_ref, dt_ref, dtb_ref, a_ref, *rest, reverse):
    if reverse:
        yf_ref, z_ref, dsk_ref, nw_ref, o_ref, st_ref, m_ref, xd_ref, xst_ref, y_ref = rest
    else:
        o_ref, st_ref, m_ref, xd_ref, xst_ref, y_ref = rest
    Q = SSD_CHUNK

    @pl.when(pl.program_id(1) == 0)
    def _():
        st_ref[...] = jnp.zeros_like(st_ref)

    row = lax.broadcasted_iota(jnp.int32, (Q, Q), 0)
    col = lax.broadcasted_iota(jnp.int32, (Q, Q), 1)
    keep = (col >= row) if reverse else (col <= row)
    tri = jnp.where(keep, 1.0, 0.0).astype(BF16)
    expand = jnp.where(lax.broadcasted_iota(jnp.int32, (LANES, SSM_INNER), 1) // SSM_HEAD_DIM == lax.broadcasted_iota(jnp.int32, (LANES, SSM_INNER), 0), 1.0, 0.0).astype(BF16)

    dt = jax.nn.softplus(dt_ref[...] + dtb_ref[...])
    a = dt * a_ref[...]
    cs = sum(jnp.dot(tri, p, preferred_element_type=F32) for p in _split3(a))
    tot = 0 if reverse else Q - 1
    cs_tot = cs[tot:tot + 1, :]
    cs_t = cs.T
    dt_e = jnp.dot(dt.astype(BF16), expand, preferred_element_type=F32)
    e_in = jnp.dot(jnp.exp(cs).astype(BF16), expand, preferred_element_type=F32)
    e_st = jnp.dot(jnp.exp(cs_tot - cs).astype(BF16), expand, preferred_element_type=F32)
    e_tot = jnp.dot(jnp.broadcast_to(jnp.exp(cs_tot), (SUBLANES, LANES)).astype(BF16), expand, preferred_element_type=F32)[0:1, :]

    xs = xbc_ref[:, :SSM_INNER].astype(F32)
    xd_ref[...] = (xs * dt_e).astype(BF16)
    xst_ref[...] = (xs * dt_e * e_st).astype(BF16)
    for g in range(SSM_GROUPS):
        bm = xbc_ref[:, SSM_INNER + g * SSM_STATE:SSM_INNER + (g + 1) * SSM_STATE]
        cm = xbc_ref[:, SSM_INNER + (SSM_GROUPS + g) * SSM_STATE:SSM_INNER + (SSM_GROUPS + g + 1) * SSM_STATE]
        cb = _dot_nt(cm, bm)
        for h in range(g * SSM_GHEADS, (g + 1) * SSM_GHEADS):
            diff = jnp.broadcast_to(cs[:, h:h + 1], (Q, Q)) - jnp.broadcast_to(cs_t[h:h + 1, :], (Q, Q))
            m_ref[h] = (cb * jnp.exp(jnp.where(keep, diff, NEG_BIG))).astype(BF16)

    lane_lo = lax.broadcasted_iota(jnp.int32, (Q, LANES), 1) < SSM_HEAD_DIM
    for g in range(SSM_GROUPS):
        bm = xbc_ref[:, SSM_INNER + g * SSM_STATE:SSM_INNER + (g + 1) * SSM_STATE]
        cm = xbc_ref[:, SSM_INNER + (SSM_GROUPS + g) * SSM_STATE:SSM_INNER + (SSM_GROUPS + g + 1) * SSM_STATE]
        gl = slice(g * SSM_GP, (g + 1) * SSM_GP)
        st = st_ref[g]
        y_off = jnp.dot(cm, st.astype(BF16), preferred_element_type=F32) * e_in[:, gl]
        for hp in range(SSM_GHEADS // 2):
            h0 = g * SSM_GHEADS + 2 * hp
            lanes = slice((h0 // 2) * LANES, (h0 // 2 + 1) * LANES)
            pair = [jnp.dot(m_ref[h], xd_ref[:, lanes], preferred_element_type=F32) for h in (h0, h0 + 1)]
            y_ref[:, lanes] = jnp.where(lane_lo, pair[0], pair[1]) + y_off[:, hp * LANES:(hp + 1) * LANES]
        st_ref[g] = st * e_tot[:, gl] + jnp.dot(bm.astype(F32).T.astype(BF16), xst_ref[:, gl], preferred_element_type=F32)
    y = y_ref[...]
    if reverse:
        y = (y + yf_ref[...] + xbc_ref[:, :SSM_INNER].astype(F32) * dsk_ref[...])
        zz = z_ref[...].astype(F32)
        y = y * (zz * jax.nn.sigmoid(zz))
        y = y * lax.rsqrt(jnp.mean(y * y, axis=-1, keepdims=True) + EPS) * nw_ref[...]
    o_ref[...] = y.astype(o_ref.dtype)


def _ssd_scan(xbc_act, dt2, p_rm, z_col, dt_bias, a_log, d_skip, norm_w, n_ctx):
    b, t, _ = xbc_act.shape
    nc, ncc = t // SSD_CHUNK, n_ctx // SSD_CHUNK

    def pad_heads(v):
        return jnp.pad(v.astype(F32), ((0, 0), (0, LANES - SSM_HEADS)))

    dtb = pad_heads(dt_bias.reshape(2, SSM_HEADS))
    a_neg = pad_heads(-jnp.exp(a_log.astype(F32)))
    dsk = jnp.repeat(d_skip.astype(F32), SSM_HEAD_DIM).reshape(1, SSM_INNER)
    scratch = [pltpu.VMEM((SSM_GROUPS, SSM_STATE, SSM_GP), F32), pltpu.VMEM((SSM_HEADS, SSD_CHUNK, SSD_CHUNK), BF16),
               pltpu.VMEM((SSD_CHUNK, SSM_INNER), BF16), pltpu.VMEM((SSD_CHUNK, SSM_INNER), BF16), pltpu.VMEM((SSD_CHUNK, SSM_INNER), F32)]
    params = pltpu.CompilerParams(dimension_semantics=("arbitrary", "arbitrary"), vmem_limit_bytes=VMEM_LIMIT)

    def bwd_chunk(s):
        return jnp.where(s < ncc, ncc - 1 - s, nc + ncc - 1 - s)

    def specs(chunk, d):
        return [
            pl.BlockSpec((None, SSD_CHUNK, SSM_XBC), lambda bi, s: (bi, chunk(s), 0)),
            pl.BlockSpec((None, SSD_CHUNK, LANES), lambda bi, s: (bi, chunk(s), d)),
            pl.BlockSpec((1, LANES), lambda bi, s: (0, 0)),
            pl.BlockSpec((1, LANES), lambda bi, s: (0, 0)),
        ]

    def inner_spec(chunk):
        return pl.BlockSpec((None, SSD_CHUNK, SSM_INNER), lambda bi, s: (bi, chunk(s), 0))

    row_spec = pl.BlockSpec((1, SSM_INNER), lambda bi, s: (0, 0))
    z_spec = pl.BlockSpec((None, SSD_CHUNK, SSM_INNER), lambda bi, s: (bi, bwd_chunk(s), z_col // SSM_INNER))
    y_f = pl.pallas_call(
        functools.partial(_ssd_kernel, reverse=False),
        grid=(b, nc),
        in_specs=specs(lambda s: s, 0),
        out_specs=inner_spec(lambda s: s),
        out_shape=jax.ShapeDtypeStruct((b, t, SSM_INNER), F32),
        scratch_shapes=scratch,
        compiler_params=params,
        name="ssd_fwd",
    )(xbc_act, dt2, dtb[0:1], a_neg[0:1])
    return pl.pallas_call(
        functools.partial(_ssd_kernel, reverse=True),
        grid=(b, nc),
        in_specs=specs(bwd_chunk, 1) + [inner_spec(bwd_chunk), z_spec, row_spec, row_spec],
        out_specs=inner_spec(bwd_chunk),
        out_shape=jax.ShapeDtypeStruct((b, t, SSM_INNER), BF16),
        scratch_shapes=scratch,
        compiler_params=params,
        name="ssd_bwd",
    )(xbc_act, dt2, dtb[1:2], a_neg[1:2], y_f, p_rm, dsk, norm_w.reshape(1, SSM_INNER).astype(F32))


HY_N2 = LANES
HY_CT = LANES
VMEM_LIMIT_HYENA = 60 * 1024 * 1024


def _hy_dims(L):
    n1 = 2 * L // HY_N2
    k1n = n1 // 2 + 1
    k1p = -(-k1n // SUBLANES) * SUBLANES
    return n1, k1n, k1p


def _hy_tables(L, n1_rows):
    n1, k1n, k1p = _hy_dims(L)
    n = 2 * L
    k1 = np.arange(k1n, dtype=np.float64)[None, :, None]
    nn = (HY_N2 * np.arange(n1_rows, dtype=np.float64)[None, None, :] + np.arange(HY_N2, dtype=np.float64)[:, None, None])
    ang = 2.0 * np.pi * ((k1 * nn) % n) / n
    m1 = np.zeros((HY_N2, 2 * k1p, n1_rows), np.float32)
    m1[:, :k1n] = np.cos(ang)
    m1[:, k1p:k1p + k1n] = -np.sin(ang)
    m4 = np.transpose(m1, (0, 2, 1))
    kk = np.arange(HY_N2, dtype=np.float64)
    a2 = 2.0 * np.pi * ((kk[:, None] * kk[None, :]) % HY_N2) / HY_N2
    c, s = np.cos(a2), np.sin(a2)
    f3 = np.block([[c, s], [-s, c]]).astype(np.float32)
    f3i = np.block([[c, -s], [s, c]]).astype(np.float32)
    return jnp.asarray(m1, BF16), jnp.asarray(m4, BF16), jnp.asarray(f3, BF16), jnp.asarray(f3i, BF16)


def _hy_stage1(u_ref, a_ref, m1_ref, n1_rows, k1p):
    def body(n2, carry):
        xs = u_ref[pl.ds(n2, n1_rows, stride=HY_N2), :].astype(BF16)
        a = jnp.dot(m1_ref[n2], xs, preferred_element_type=F32)
        a_ref[pl.ds(n2, k1p, stride=2 * HY_N2), :] = a[:k1p]
        a_ref[pl.ds(HY_N2 + n2, k1p, stride=2 * HY_N2), :] = a[k1p:]
        return carry

    lax.fori_loop(0, HY_N2, body, 0, unroll=8)


def _hy_spectrum_slab(a_ref, f3_ref, k1):
    blk = a_ref[pl.ds(pl.multiple_of(k1 * 2 * HY_N2, 2 * HY_N2), 2 * HY_N2), :].astype(BF16)
    return jnp.dot(f3_ref[...], blk, preferred_element_type=F32)


def _hy_conv(u_ref, yo_ref, a_ref, y_ref, h_ref, order, m1_ref, m4_ref, f3_ref, f3i_ref, n1_rows, k1n, k1p):
    _hy_stage1(u_ref, a_ref, m1_ref, n1_rows, k1p)

    def stage2(k1, carry):
        x = _hy_spectrum_slab(a_ref, f3_ref, k1)
        h = h_ref[order, pl.ds(pl.multiple_of(k1 * 2 * HY_N2, 2 * HY_N2), 2 * HY_N2), :].astype(F32)
        xr, xi, hr, hi = x[:HY_N2], x[HY_N2:], h[:HY_N2], h[HY_N2:]
        z = jnp.concatenate([xr * hr - xi * hi, xr * hi + xi * hr], axis=0).astype(BF16)
        c = jnp.dot(f3i_ref[...], z, preferred_element_type=F32)
        y_ref[pl.ds(k1, HY_N2, stride=2 * k1p), :] = c[:HY_N2]
        y_ref[pl.ds(k1p + k1, HY_N2, stride=2 * k1p), :] = c[HY_N2:]
        return carry

    lax.fori_loop(0, k1n - 1, stage2, 0, unroll=4)
    stage2(k1n - 1, 0)

    def stage3(n2, carry):
        d = y_ref[pl.ds(pl.multiple_of(n2 * 2 * k1p, 2 * k1p), 2 * k1p), :].astype(BF16)
        yo_ref[pl.ds(n2, n1_rows, stride=HY_N2), :] = jnp.dot(m4_ref[n2], d, preferred_element_type=F32)
        return carry

    lax.fori_loop(0, HY_N2, stage3, 0, unroll=8)


def _hyena_kernel(v_ref, x1_ref, x2_ref, h_ref, m1_ref, m4_ref, f3_ref, f3i_ref, bias_ref, o_ref, a_ref, y_ref, u_ref, yo_ref, *, n1_rows, k1n, k1p):
    @pl.when(jnp.logical_and(pl.program_id(0) == 0, pl.program_id(1) == 0))
    def _():
        y_ref[...] = jnp.zeros_like(y_ref)

    u_ref[...] = v_ref[...].astype(F32)
    for order, gate_ref in enumerate((x1_ref, x2_ref)):
        _hy_conv(u_ref, yo_ref, a_ref, y_ref, h_ref, order, m1_ref, m4_ref, f3_ref, f3i_ref, n1_rows, k1n, k1p)
        z = gate_ref[...].astype(F32) * (yo_ref[...] + u_ref[...] * bias_ref[order:order + 1, :])
        if order == 0:
            u_ref[...] = z
        else:
            o_ref[...] = z.astype(o_ref.dtype)


def _hy_filter_kernel(fwd_ref, bwd_ref, m1_ref, f3_ref, o_ref, a_ref, sf_ref, ub_ref, *, n1_rows, k1n, k1p, scale_mid, scale_edge):
    slab = 2 * HY_N2
    ub_ref[...] = bwd_ref[...]
    ub_ref[0:1, :] = jnp.zeros((1, HY_CT), F32)
    energy = jnp.sum(fwd_ref[...] * fwd_ref[...], axis=0, keepdims=True) + jnp.sum(ub_ref[...] * ub_ref[...], axis=0, keepdims=True)
    norm = lax.rsqrt(energy + EPS)

    _hy_stage1(fwd_ref, a_ref, m1_ref, n1_rows, k1p)

    def keep_fwd(k1, carry):
        sf_ref[pl.ds(pl.multiple_of(k1 * slab, slab), slab), :] = _hy_spectrum_slab(a_ref, f3_ref, k1)
        return carry

    lax.fori_loop(0, k1n - 1, keep_fwd, 0, unroll=4)
    keep_fwd(k1n - 1, 0)
    _hy_stage1(ub_ref, a_ref, m1_ref, n1_rows, k1p)

    def combine(k1, carry):
        xb = _hy_spectrum_slab(a_ref, f3_ref, k1)
        xf = sf_ref[pl.ds(pl.multiple_of(k1 * slab, slab), slab), :]
        w = norm * jnp.where(jnp.logical_or(k1 == 0, k1 == k1n - 1), scale_edge, scale_mid)
        h = jnp.concatenate([xf[:HY_N2] + xb[:HY_N2], xf[HY_N2:] - xb[HY_N2:]], axis=0)
        o_ref[pl.ds(pl.multiple_of(k1 * slab, slab), slab), :] = (h * w).astype(o_ref.dtype)
        return carry

    lax.fori_loop(0, k1n - 1, combine, 0, unroll=4)
    combine(k1n - 1, 0)


def _single(block_shape, index_map):
    return pl.BlockSpec(block_shape, index_map, pipeline_mode=pl.Buffered(1))


def _hyena_filter_spectrum_pallas(taps):
    L = taps.shape[0]
    c = taps.shape[1] // (2 * HY_ORDER)
    nct = c // HY_CT
    n = 2 * L
    n1, k1n, k1p = _hy_dims(L)
    n1_rows = L // HY_N2
    m1, _, f3, _ = _hy_tables(L, n1_rows)
    kern = functools.partial(_hy_filter_kernel, n1_rows=n1_rows, k1n=k1n, k1p=k1p, scale_mid=2.0 / n, scale_edge=1.0 / n)
    return pl.pallas_call(
        kern,
        grid=(HY_ORDER, nct),
        in_specs=[
            pl.BlockSpec((L, HY_CT), lambda o, j: (0, o * nct + j)),
            pl.BlockSpec((L, HY_CT), lambda o, j: (0, (HY_ORDER + o) * nct + j)),
            _single((HY_N2, 2 * k1p, n1_rows), lambda o, j: (0, 0, 0)),
            _single((2 * HY_N2, 2 * HY_N2), lambda o, j: (0, 0)),
        ],
        out_specs=pl.BlockSpec((None, k1n * 2 * HY_N2, HY_CT), lambda o, j: (o, 0, j)),
        out_shape=jax.ShapeDtypeStruct((HY_ORDER, k1n * 2 * HY_N2, c), BF16),
        scratch_shapes=[
            pltpu.VMEM((k1p * 2 * HY_N2, HY_CT), F32),
            pltpu.VMEM((k1n * 2 * HY_N2, HY_CT), F32),
            pltpu.VMEM((L, HY_CT), F32),
        ],
        compiler_params=pltpu.CompilerParams(dimension_semantics=("arbitrary", "arbitrary"), vmem_limit_bytes=VMEM_LIMIT_HYENA),
        name="hyena_filter_dft",
    )(taps, taps, m1, f3)


def _hyena_long(hy, h_spec, bias):
    b, L, c3 = hy.shape
    c = c3 // (HY_ORDER + 1)
    nct = c // HY_CT
    n1, k1n, k1p = _hy_dims(L)
    n1_rows = L // HY_N2
    m1, m4, f3, f3i = _hy_tables(L, n1_rows)
    kern = functools.partial(_hyena_kernel, n1_rows=n1_rows, k1n=k1n, k1p=k1p)

    def col(part):
        return _single((None, L, HY_CT), lambda j, bi: (bi, 0, part * nct + j))

    return pl.pallas_call(
        kern,
        grid=(nct, b),
        in_specs=[
            col(0), col(1), col(2),
            _single((HY_ORDER, k1n * 2 * HY_N2, HY_CT), lambda j, bi: (0, 0, j)),
            _single((HY_N2, 2 * k1p, n1_rows), lambda j, bi: (0, 0, 0)),
            _single((HY_N2, n1_rows, 2 * k1p), lambda j, bi: (0, 0, 0)),
            _single((2 * HY_N2, 2 * HY_N2), lambda j, bi: (0, 0)),
            _single((2 * HY_N2, 2 * HY_N2), lambda j, bi: (0, 0)),
            pl.BlockSpec((HY_ORDER, HY_CT), lambda j, bi: (0, j)),
        ],
        out_specs=pl.BlockSpec((None, L, HY_CT), lambda j, bi: (bi, 0, j)),
        out_shape=jax.ShapeDtypeStruct((b, L, c), BF16),
        scratch_shapes=[
            pltpu.VMEM((k1p * 2 * HY_N2, HY_CT), F32),
            pltpu.VMEM((HY_N2 * 2 * k1p, HY_CT), F32),
            pltpu.VMEM((L, HY_CT), F32),
            pltpu.VMEM((L, HY_CT), F32),
        ],
        compiler_params=pltpu.CompilerParams(dimension_semantics=("arbitrary", "arbitrary"), vmem_limit_bytes=VMEM_LIMIT_HYENA),
        name="hyena_long_conv",
    )(hy, hy, hy, h_spec, m1, m4, f3, f3i, bias.astype(F32))


HY_CTX_CT = 256


def _hy_ctx_tables(L):
    n = 2 * L
    kb = L + 1
    kp = -(-kb // LANES) * LANES
    ang = 2.0 * np.pi * ((np.arange(kb, dtype=np.float64)[:, None] * np.arange(L, dtype=np.float64)[None, :]) % n) / n
    fwd = np.zeros((2 * kp, L), np.float32)
    fwd[:kb] = np.cos(ang)
    fwd[kp:kp + kb] = -np.sin(ang)
    return jnp.asarray(fwd, BF16), jnp.asarray(fwd.T, BF16), kb, kp


def _hy_ctx_filter_kernel(fwd_ref, bwd_ref, f_ref, o_ref, *, kb, kp, n):
    fwd = fwd_ref[...]
    bwd = jnp.where(lax.broadcasted_iota(jnp.int32, fwd.shape, 0) == 0, 0.0, bwd_ref[...])
    norm = lax.rsqrt(jnp.sum(fwd * fwd, axis=0, keepdims=True) + jnp.sum(bwd * bwd, axis=0, keepdims=True) + EPS)
    hf = jnp.dot(f_ref[...], fwd.astype(BF16), preferred_element_type=F32)
    hb = jnp.dot(f_ref[...], bwd.astype(BF16), preferred_element_type=F32)
    row = lax.broadcasted_iota(jnp.int32, hf.shape, 0)
    imag = row >= kp
    k = jnp.where(imag, row - kp, row)
    wk = jnp.where(jnp.logical_or(k == 0, k == kb - 1), 1.0 / n, 2.0 / n)
    o_ref[...] = (hf + jnp.where(imag, -hb, hb)) * (wk * norm)


def _hy_ctx_kernel(v_ref, x1_ref, x2_ref, h_ref, f_ref, g_ref, bias_ref, o_ref, *, kp):
    u = v_ref[...].astype(F32)
    for order, gate_ref in enumerate((x1_ref, x2_ref)):
        x = jnp.dot(f_ref[...], u.astype(BF16), preferred_element_type=F32)
        h = h_ref[order]
        xr, xi, hr, hi = x[:kp], x[kp:], h[:kp], h[kp:]
        z = jnp.concatenate([xr * hr - xi * hi, xr * hi + xi * hr], axis=0).astype(BF16)
        y = jnp.dot(g_ref[...], z, preferred_element_type=F32)
        u = gate_ref[...].astype(F32) * (y + u * bias_ref[order:order + 1, :])
    o_ref[...] = u.astype(o_ref.dtype)


def _hyena_ctx(hy, L, taps, bias):
    b = hy.shape[0]
    c = hy.shape[2] // (HY_ORDER + 1)
    ct = HY_CTX_CT
    nct = c // ct
    f_mat, g_mat, kb, kp = _hy_ctx_tables(L)
    params = pltpu.CompilerParams(dimension_semantics=("arbitrary", "arbitrary"), vmem_limit_bytes=VMEM_LIMIT)
    h_spec = pl.pallas_call(
        functools.partial(_hy_ctx_filter_kernel, kb=kb, kp=kp, n=2 * L),
        grid=(HY_ORDER, nct),
        in_specs=[
            pl.BlockSpec((L, ct), lambda o, j: (0, o * nct + j)),
            pl.BlockSpec((L, ct), lambda o, j: (0, (HY_ORDER + o) * nct + j)),
            pl.BlockSpec((2 * kp, L), lambda o, j: (0, 0)),
        ],
        out_specs=pl.BlockSpec((None, 2 * kp, ct), lambda o, j: (o, 0, j)),
        out_shape=jax.ShapeDtypeStruct((HY_ORDER, 2 * kp, c), F32),
        compiler_params=params,
        name="hyena_ctx_filter",
    )(taps, taps, f_mat)

    def col(part):
        return pl.BlockSpec((None, L, ct), lambda j, bi: (bi, 0, part * nct + j))

    return pl.pallas_call(
        functools.partial(_hy_ctx_kernel, kp=kp),
        grid=(nct, b),
        in_specs=[
            col(0), col(1), col(2),
            pl.BlockSpec((HY_ORDER, 2 * kp, ct), lambda j, bi: (0, 0, j)),
            pl.BlockSpec((2 * kp, L), lambda j, bi: (0, 0)),
            pl.BlockSpec((L, 2 * kp), lambda j, bi: (0, 0)),
            pl.BlockSpec((HY_ORDER, ct), lambda j, bi: (0, j)),
        ],
        out_specs=pl.BlockSpec((None, L, ct), lambda j, bi: (bi, 0, j)),
        out_shape=jax.ShapeDtypeStruct((b, L, c), BF16),
        compiler_params=params,
        name="hyena_ctx_conv",
    )(hy, hy, hy, h_spec, f_mat, g_mat, bias.astype(F32))


def _split_cols(t, sizes):
    return jnp.split(t, np.cumsum(sizes)[:-1].tolist(), axis=-1)


def _rms_norm(x, w):
    xf = x.astype(F32)
    y = xf * lax.rsqrt(jnp.mean(xf * xf, axis=-1, keepdims=True) + EPS)
    return (y * w.astype(F32)).astype(x.dtype)


def _dwconv_centred(x, w, b):
    k = w.shape[0]
    y = lax.conv_general_dilated(x, w[:, None, :].astype(x.dtype), window_strides=(1,), padding=[(k // 2, k // 2)], dimension_numbers=('NWC', 'WIO', 'NWC'), feature_group_count=x.shape[-1], precision=lax.Precision.HIGHEST)
    return y + b.astype(x.dtype)


def _flip_seq(t):
    return jnp.flip(t, axis=1)


def _to_col_major(t, rows):
    b, rest = t.shape[0], t.shape[2:]
    return jnp.swapaxes(t.reshape((b, rows, GRID_W) + rest), 1, 2).reshape((b, rows * GRID_W) + rest)


def _from_col_major(t, rows):
    b, rest = t.shape[0], t.shape[2:]
    return jnp.swapaxes(t.reshape((b, GRID_W, rows) + rest), 1, 2).reshape((b, rows * GRID_W) + rest)


def _hyena_filter_taps(L, w1, b1, w2, b2, w3, b3, freq):
    hp = lax.Precision.HIGHEST
    t = jnp.linspace(0.0, 1.0, L, dtype=F32)[:, None]
    w = 2.0 * math.pi * jnp.arange(L, dtype=F32)[:, None] / L
    bands = jnp.linspace(1e-4, HY_BANDS - 1, HY_BANDS, dtype=F32)
    feats = jnp.concatenate([t, jnp.cos(bands * w), -jnp.sin(bands * w)], axis=-1)
    h = jnp.sin(freq[0] * (jnp.dot(feats, w1, precision=hp) + b1))
    h = jnp.sin(freq[1] * (jnp.dot(h, w2, precision=hp) + b2))
    h = (jnp.dot(h, w3, precision=hp) + b3).astype(F32).reshape(L, 2, HY_ORDER, HY_WIDTH)
    max_decay = math.log(HY_DECAY_TARGET) / HY_FAST_DECAY
    min_decay = math.log(HY_DECAY_TARGET) / HY_SLOW_DECAY
    deltas = jnp.abs(jnp.linspace(min_decay, max_decay, HY_WIDTH, dtype=F32))
    return h * jnp.exp(-t * deltas)[:, None, None, :]


def _mixer_branches(h_rm, w_in, lb, n_ctx, ssm_conv_w, ssm_conv_b, ssm_dt_bias, ssm_a_log, ssm_d, ssm_norm, hy_conv_w, hy_conv_b, hy_w1, hy_b1, hy_w2, hy_b2, hy_w3, hy_b3, hy_freq, hy_bias, hg_norm):
    b, t, d = h_rm.shape
    n_lat = t - n_ctx
    rows = n_lat // GRID_W
    w_z, w_xbc, w_dt, w_hy, w_q, w_f, w_i, w_g, w_gate = _split_cols(w_in, IN_SIZES)
    h2 = h_rm.reshape(b * t, d)
    rm_parts = (w_z, w_xbc, w_hy, w_g, w_gate)
    col_z, col_xbc, col_hy, col_g, col_gate = np.cumsum([0] + [w.shape[1] for w in rm_parts[:-1]]).tolist()
    p_rm = _mm(h2, jnp.concatenate(rm_parts, axis=1).astype(BF16), BF16).reshape(b, t, -1)

    zero_pad = jnp.zeros((d, LANES - SSM_HEADS), F32)
    w_dt2 = jnp.concatenate([w_dt[:, :SSM_HEADS], zero_pad, w_dt[:, SSM_HEADS:], zero_pad], axis=1)
    dt2 = _mm(h2, w_dt2.astype(BF16), F32).reshape(b, t, 2 * LANES)
    xbc_act = _dwconv_stream(p_rm, ssm_conv_w, ssm_conv_b, n_ctx, True, c0=col_xbc)
    ym = _ssd_scan(xbc_act, dt2, p_rm, col_z, ssm_dt_bias, ssm_a_log, ssm_d, ssm_norm, n_ctx)

    hy = _dwconv_stream(p_rm, hy_conv_w, hy_conv_b, n_ctx, False, c0=col_hy)
    taps_ctx = _hyena_filter_taps(n_ctx, hy_w1, hy_b1, hy_w2, hy_b2, hy_w3, hy_b3, hy_freq).reshape(n_ctx, 2 * HY_ORDER * HY_WIDTH)
    yh_ctx = _hyena_ctx(hy, n_ctx, taps_ctx, hy_bias)
    taps = _hyena_filter_taps(n_lat, hy_w1, hy_b1, hy_w2, hy_b2, hy_w3, hy_b3, hy_freq).reshape(n_lat, 2 * HY_ORDER * HY_WIDTH)
    yh_lat = _hyena_long(hy[:, n_ctx:], _hyena_filter_spectrum_pallas(taps), hy_bias)
    yh = jnp.concatenate([yh_ctx, yh_lat], axis=1)

    h_cm = jnp.concatenate([h_rm[:, :n_ctx], _to_col_major(h_rm[:, n_ctx:], rows)], axis=1).reshape(b * t, d)

    p_cm = _mm(h_cm, jnp.concatenate([w_q, w_f, w_i], axis=1).astype(BF16), BF16).reshape(b, t, -1)
    og = _hgrn_scan(p_cm, lb, hg_norm, n_ctx)
    og = jnp.concatenate([og[:, :n_ctx], _from_col_major(og[:, n_ctx:], rows)], axis=1)
    return ym, yh, og, p_rm, col_g, col_gate


ROW_TILE = 256
MOD_ROWS = SUBLANES
M_SHIFT_MIX, M_SCALE_MIX, M_GATE_MIX, M_SHIFT_FFN, M_SCALE_FFN, M_GATE_FFN = range(6)
ROW_PARAMS = pltpu.CompilerParams(dimension_semantics=("arbitrary", "arbitrary"), vmem_limit_bytes=VMEM_LIMIT)


def _rms(x):
    return x * lax.rsqrt(jnp.mean(x * x, axis=-1, keepdims=True) + EPS)


def _mrow(m_ref, r):
    return m_ref[r:r + 1, :]


def _row_spec(width):
    return pl.BlockSpec((None, ROW_TILE, width), lambda bi, i: (bi, i, 0))


def _vec_spec(width):
    return pl.BlockSpec((1, width), lambda bi, i: (0, 0))


def _mat_spec(k, n):
    return pl.BlockSpec((k, n), lambda bi, i: (0, 0))


def _mod_spec(n_ctx):
    return pl.BlockSpec((None, None, MOD_ROWS, D_MODEL), lambda bi, i: (bi, jnp.where(i < n_ctx // ROW_TILE, 0, 1), 0, 0))


def _norm_mod_kernel(x_ref, w_ref, m_ref, o_ref):
    y = _rms(x_ref[...]) * w_ref[...]
    o_ref[...] = (y * (1.0 + _mrow(m_ref, M_SCALE_MIX)) + _mrow(m_ref, M_SHIFT_MIX)).astype(o_ref.dtype)


def _norm_mod(xs, w, mods, n_ctx):
    b, t, d = xs.shape
    return pl.pallas_call(
        _norm_mod_kernel,
        grid=(b, t // ROW_TILE),
        in_specs=[_row_spec(d), _vec_spec(d), _mod_spec(n_ctx)],
        out_specs=_row_spec(d),
        out_shape=jax.ShapeDtypeStruct((b, t, d), BF16),
        compiler_params=ROW_PARAMS,
        name="norm_mod",
    )(xs, w.reshape(1, d), mods)


def _merge_kernel(ym_ref, yh_ref, og_ref, g_ref, gm_ref, gh_ref, gg_ref, x_ref, m_ref, w1_ref, w2_ref, w3_ref, wo_ref, npost_ref, npre_ref, rw_ref, rb_ref, xo_ref, h_ref, lg_ref):
    def sig(ref):
        return jax.nn.sigmoid(ref[...].astype(F32))

    go = g_ref[...].astype(F32)
    yg = (og_ref[...].astype(F32) * (go * jax.nn.sigmoid(go))).astype(BF16)
    merged = sig(gm_ref) * jnp.dot(ym_ref[...], w1_ref[...], preferred_element_type=F32)
    merged = merged + sig(gh_ref) * jnp.dot(yh_ref[...], w2_ref[...], preferred_element_type=F32)
    merged = merged + sig(gg_ref) * jnp.dot(yg, w3_ref[...], preferred_element_type=F32)
    mix = jnp.dot(merged.astype(BF16), wo_ref[...], preferred_element_type=F32)
    x = x_ref[...] + _mrow(m_ref, M_GATE_MIX) * (_rms(mix) * npost_ref[...])
    xo_ref[...] = x
    h = (_rms(x) * npre_ref[...] * (1.0 + _mrow(m_ref, M_SCALE_FFN)) + _mrow(m_ref, M_SHIFT_FFN)).astype(BF16)
    h_ref[...] = h
    lg_ref[...] = jnp.dot(h, rw_ref[...], preferred_element_type=F32) + rb_ref[...]


def _merge(ym, yh, og, p_rm, col_g, col_gate, xs, mods, w_br_ssm, w_br_hy, w_br_hg, w_out, norm_post, norm_ffn_pre, router_w, router_b, n_ctx):
    b, t, d = xs.shape
    rw = jnp.pad(router_w, ((0, 0), (0, LANES - N_EXPERTS))).astype(BF16)
    rb = jnp.pad(router_b, (0, LANES - N_EXPERTS)).reshape(1, LANES).astype(F32)

    def col_spec(col):
        return pl.BlockSpec((None, ROW_TILE, d), lambda bi, i: (bi, i, col // d))

    return pl.pallas_call(
        _merge_kernel,
        grid=(b, t // ROW_TILE),
        in_specs=[_row_spec(d), _row_spec(d), _row_spec(d), col_spec(col_g), col_spec(col_gate), col_spec(col_gate + d), col_spec(col_gate + 2 * d), _row_spec(d), _mod_spec(n_ctx),
                  _mat_spec(d, d), _mat_spec(d, d), _mat_spec(d, d), _mat_spec(d, d), _vec_spec(d), _vec_spec(d), _mat_spec(d, LANES), _vec_spec(LANES)],
        out_specs=[_row_spec(d), _row_spec(d), _row_spec(LANES)],
        out_shape=[jax.ShapeDtypeStruct((b, t, d), F32), jax.ShapeDtypeStruct((b, t, d), BF16), jax.ShapeDtypeStruct((b, t, LANES), F32)],
        compiler_params=ROW_PARAMS,
        name="branch_merge",
    )(ym, yh, og, p_rm, p_rm, p_rm, p_rm, xs, mods, w_br_ssm.astype(BF16), w_br_hy.astype(BF16), w_br_hg.astype(BF16), w_out.astype(BF16),
      norm_post.reshape(1, d), norm_ffn_pre.reshape(1, d), rw, rb)


def _post_ffn_kernel(y0_ref, y1_ref, y2_ref, y3_ref, x_ref, m_ref, w_ref, o_ref):
    f = y0_ref[...].astype(F32) + y1_ref[...].astype(F32) + y2_ref[...].astype(F32) + y3_ref[...].astype(F32)
    o_ref[...] = x_ref[...] + _mrow(m_ref, M_GATE_FFN) * (_rms(f) * w_ref[...])


def _post_ffn(f4, xs, mods, norm_post, n_ctx, skip):
    b, _, d = xs.shape
    t = f4.shape[2]
    i0 = skip // ROW_TILE

    def k_spec(k):
        return pl.BlockSpec((None, None, ROW_TILE, d), lambda bi, i: (k, bi, i, 0))

    return pl.pallas_call(
        _post_ffn_kernel,
        grid=(b, t // ROW_TILE),
        in_specs=[k_spec(k) for k in range(TOP_K)] + [
            pl.BlockSpec((None, ROW_TILE, d), lambda bi, i: (bi, i0 + i, 0)),
            pl.BlockSpec((None, None, MOD_ROWS, d), lambda bi, i: (bi, jnp.where(i0 + i < n_ctx // ROW_TILE, 0, 1), 0, 0)),
            _vec_spec(d)],
        out_specs=_row_spec(d),
        out_shape=jax.ShapeDtypeStruct((b, t, d), F32),
        compiler_params=ROW_PARAMS,
        name="post_ffn",
    )(f4, f4, f4, f4, xs, mods, norm_post.reshape(1, d))


def _moe_ffn(h2, logits, t_per_b, skip, li, w1, b1, w2, b2):
    t = logits.shape[0]
    d = h2.shape[1]
    n = t * TOP_K
    n_tiles = n // MOE_BLOCK
    top_v, top_e = lax.top_k(logits, TOP_K)
    gate_w = jax.nn.softmax(top_v, axis=-1)
    flat_e = top_e.reshape(n).astype(jnp.int32)
    iota = jnp.arange(n, dtype=jnp.int32)
    _, order, sw = lax.sort((flat_e, iota, gate_w.reshape(n)), num_keys=1, is_stable=True)
    _, inv = lax.sort((order, iota), num_keys=1)
    tok = order // TOP_K
    xs = h2[tok + (tok // t_per_b + 1) * skip]
    counts = jnp.sum((flat_e[:, None] == jnp.arange(N_EXPERTS, dtype=jnp.int32)[None, :]).astype(jnp.int32), axis=0)
    end = jnp.cumsum(counts)
    start = end - counts
    first_tile = start // MOE_BLOCK
    n_items = jnp.where(counts > 0, (end - 1) // MOE_BLOCK - first_tile + 1, 0)
    items_end = jnp.cumsum(n_items)
    w = jnp.arange(n_tiles + N_EXPERTS, dtype=jnp.int32)
    valid = w < items_end[-1]
    e_w = jnp.minimum(jnp.sum((w[:, None] >= items_end[None, :]).astype(jnp.int32), axis=1), N_EXPERTS - 1)
    tile_w = first_tile[e_w] + (w - (items_end[e_w] - n_items[e_w]))
    lo = jnp.where(valid, jnp.maximum(start[e_w], tile_w * MOE_BLOCK), 0)
    hi = jnp.where(valid, jnp.minimum(end[e_w], (tile_w + 1) * MOE_BLOCK), 0)
    tile_w = jnp.where(valid, tile_w, n_tiles - 1)
    first = jnp.concatenate([jnp.ones((1,), jnp.int32), (tile_w[1:] != tile_w[:-1]).astype(jnp.int32)])
    newexp = jnp.concatenate([jnp.ones((1,), jnp.int32), (e_w[1:] != e_w[:-1]).astype(jnp.int32)])
    ys = _moe_experts(xs, sw, tile_w.astype(jnp.int32), e_w, lo.astype(jnp.int32), hi.astype(jnp.int32), first, newexp, li, w1, b1, w2, b2)
    return ys[inv.reshape(t, TOP_K).T]


def kernel(x, c, ctx, c_ctx, w_mod, b_mod, norm_mix_pre, norm_mix_post, norm_ffn_pre, norm_ffn_post, w_in, ssm_conv_w, ssm_conv_b, ssm_dt_bias, ssm_a_log, ssm_d, ssm_norm, hy_conv_w, hy_conv_b, hy_w1, hy_b1, hy_w2, hy_b2, hy_w3, hy_b3, hy_freq, hy_bias, hg_lb_logits, hg_norm, w_br_ssm, w_br_hy, w_br_hg, w_out, router_w, router_b, exp_w1, exp_b1, exp_w2, exp_b2):
    hp = lax.Precision.HIGHEST
    b, n_lat, d = x.shape
    n_ctx = ctx.shape[1]
    lb = jax.nn.softmax(hg_lb_logits.astype(F32), axis=1)
    lb = jnp.cumsum(lb, axis=1) - lb[:, :1]
    silu_c = jax.nn.silu(c)
    silu_cc = jax.nn.silu(c_ctx)
    xs = jnp.concatenate([ctx, x], axis=1)
    for li in range(DEPTH):
        mx = (jnp.dot(silu_c, w_mod[li], precision=hp) + b_mod[li]).reshape(b, 1, 6, d)
        mc = jnp.broadcast_to((jnp.dot(silu_cc, w_mod[li], precision=hp) + b_mod[li]).reshape(1, 1, 6, d), (b, 1, 6, d))
        mods = jnp.pad(jnp.concatenate([mc, mx], axis=1), ((0, 0), (0, 0), (0, MOD_ROWS - 6), (0, 0)))
        h = _norm_mod(xs, norm_mix_pre[li], mods, n_ctx)
        ym, yh, og, p_rm, col_g, col_gate = _mixer_branches(h, w_in[li], lb[:, li], n_ctx, ssm_conv_w[li], ssm_conv_b[li], ssm_dt_bias[li], ssm_a_log[li], ssm_d[li], ssm_norm[li], hy_conv_w[li], hy_conv_b[li], hy_w1[li], hy_b1[li], hy_w2[li], hy_b2[li], hy_w3[li], hy_b3[li], hy_freq[li], hy_bias[li], hg_norm[li])
        xs, h_ffn, logits = _merge(ym, yh, og, p_rm, col_g, col_gate, xs, mods, w_br_ssm[li], w_br_hy[li], w_br_hg[li], w_out[li], norm_mix_post[li], norm_ffn_pre[li], router_w[li], router_b[li], n_ctx)
        skip = n_ctx if li == DEPTH - 1 else 0
        t = xs.shape[1] - skip
        f4 = _moe_ffn(h_ffn.reshape(-1, d), logits[:, skip:, :N_EXPERTS].reshape(b * t, N_EXPERTS), t, skip, li, exp_w1, exp_b1[li], exp_w2, exp_b2[li])
        xs = _post_ffn(f4.reshape(TOP_K, b, t, d), xs, mods, norm_ffn_post[li], n_ctx, skip)
    return xs
```

```python
import functools
import math

import jax
import jax.numpy as jnp
import numpy as np
from jax import lax
from jax.experimental import pallas as pl
from jax.experimental.pallas import tpu as pltpu

D_MODEL = 1024
DEPTH = 2
GRID_W = 64

SSM_HEADS = 16
SSM_HEAD_DIM = 64
SSM_INNER = SSM_HEADS * SSM_HEAD_DIM
SSM_STATE = 128
SSM_GROUPS = 4
SSD_CHUNK = 128
SSM_XBC = SSM_INNER + 2 * SSM_GROUPS * SSM_STATE

HY_WIDTH = D_MODEL
HY_ORDER = 2
HY_BANDS = 16
HY_FAST_DECAY = 0.3
HY_SLOW_DECAY = 1.5
HY_DECAY_TARGET = 1e-2

HG_HEADS = 8
HG_KDIM = 128
HG_VDIM = D_MODEL // HG_HEADS
HG_QK = HG_HEADS * HG_KDIM
HG_V = HG_HEADS * HG_VDIM
HG_CHUNK = 64
F_FLOOR = 1e-20

N_EXPERTS = 32
TOP_K = 4
D_FF = D_MODEL
SWIGLU_LIMIT = 7.0
SWIGLU_ALPHA = 1.702
MOE_BLOCK = 256

N_BRANCHES = 3
IN_SIZES = (SSM_INNER, SSM_XBC, 2 * SSM_HEADS, (HY_ORDER + 1) * HY_WIDTH, HG_QK, 2 * HG_QK, HG_V, HG_V, N_BRANCHES * D_MODEL)
EPS = 1e-6
F32 = jnp.float32
BF16 = jnp.bfloat16

LANES = 128
VMEM_LIMIT = 56 * 1024 * 1024


def _mm_kernel(a_ref, b_ref, o_ref):
    o_ref[...] = jnp.dot(a_ref[...], b_ref[...], preferred_element_type=F32).astype(o_ref.dtype)


def _mm(a, b, out_dtype=F32, tm=1024, tn=1024):
    m, k = a.shape
    n = b.shape[1]
    tm = math.gcd(m, tm)
    tn = math.gcd(n, tn)
    assert tm % SUBLANES == 0 and tn % LANES == 0, (m, n, tm, tn)
    return pl.pallas_call(
        _mm_kernel,
        grid=(n // tn, m // tm),
        in_specs=[pl.BlockSpec((tm, k), lambda j, i: (i, 0)), pl.BlockSpec((k, tn), lambda j, i: (0, j))],
        out_specs=pl.BlockSpec((tm, tn), lambda j, i: (i, j)),
        out_shape=jax.ShapeDtypeStruct((m, n), out_dtype),
        compiler_params=pltpu.CompilerParams(dimension_semantics=("arbitrary", "arbitrary"), vmem_limit_bytes=VMEM_LIMIT),
        name="dense_mm",
    )(a, b)


def _moe_kernel(tile_ref, exp_ref, lo_ref, hi_ref, first_ref, newexp_ref, x_ref, sw_ref, w1_ref, b1_ref, w2_ref, b2_ref, o_ref, w1b_ref, w2b_ref):
    del exp_ref
    w = pl.program_id(0)
    lo, hi = lo_ref[w], hi_ref[w]

    @pl.when(newexp_ref[w] == 1)
    def _():
        def cast_rows(i, carry):
            rows = pl.ds(pl.multiple_of(i * LANES, LANES), LANES)
            w1b_ref[rows, :] = w1_ref[rows, :].astype(BF16)
            w2b_ref[rows, :] = w2_ref[rows, :].astype(BF16)
            return carry

        lax.fori_loop(0, D_MODEL // LANES, cast_rows, 0)

    @pl.when(hi > lo)
    def _():
        hh = jnp.dot(x_ref[...], w1b_ref[...], preferred_element_type=F32) + b1_ref[...]
        g = jnp.minimum(hh[:, :D_FF], SWIGLU_LIMIT)
        u = jnp.clip(hh[:, D_FF:], -SWIGLU_LIMIT, SWIGLU_LIMIT)
        act = (u + 1.0) * g * jax.nn.sigmoid(SWIGLU_ALPHA * g)
        y = jnp.dot(act.astype(BF16), w2b_ref[...], preferred_element_type=F32) + b2_ref[...]
        y = (y * sw_ref[...]).astype(o_ref.dtype)
        rows = tile_ref[w] * MOE_BLOCK + lax.broadcasted_iota(jnp.int32, (MOE_BLOCK, 1), 0)
        mine = jnp.logical_and(rows >= lo, rows < hi)

        @pl.when(first_ref[w] == 1)
        def _():
            o_ref[...] = jnp.where(mine, y, jnp.zeros_like(y))

        @pl.when(first_ref[w] != 1)
        def _():
            o_ref[...] = jnp.where(mine, y, o_ref[...])


def _moe_experts(xs, sw, tile_w, exp_w, lo, hi, first, newexp, li, w1, b1, w2, b2):
    n, d = xs.shape
    grid_spec = pltpu.PrefetchScalarGridSpec(
        num_scalar_prefetch=6,
        grid=(tile_w.shape[0],),
        in_specs=[
            pl.BlockSpec((MOE_BLOCK, d), lambda w, tl, ex, lo_, hi_, fi, ne: (tl[w], 0)),
            pl.BlockSpec((MOE_BLOCK, 1), lambda w, tl, ex, lo_, hi_, fi, ne: (tl[w], 0)),
            pl.BlockSpec((None, None, d, 2 * D_FF), lambda w, tl, ex, lo_, hi_, fi, ne: (li, ex[w], 0, 0)),
            pl.BlockSpec((None, 1, 2 * D_FF), lambda w, tl, ex, lo_, hi_, fi, ne: (ex[w], 0, 0)),
            pl.BlockSpec((None, None, D_FF, d), lambda w, tl, ex, lo_, hi_, fi, ne: (li, ex[w], 0, 0)),
            pl.BlockSpec((None, 1, d), lambda w, tl, ex, lo_, hi_, fi, ne: (ex[w], 0, 0)),
        ],
        out_specs=pl.BlockSpec((MOE_BLOCK, d), lambda w, tl, ex, lo_, hi_, fi, ne: (tl[w], 0)),
        scratch_shapes=[pltpu.VMEM((d, 2 * D_FF), BF16), pltpu.VMEM((D_FF, d), BF16)],
    )
    return pl.pallas_call(
        _moe_kernel,
        grid_spec=grid_spec,
        out_shape=jax.ShapeDtypeStruct((n, d), BF16),
        compiler_params=pltpu.CompilerParams(dimension_semantics=("arbitrary",), vmem_limit_bytes=VMEM_LIMIT),
        name="moe_experts",
    )(tile_w, exp_w, lo, hi, first, newexp, xs, sw.reshape(n, 1), w1, b1.reshape(N_EXPERTS, 1, 2 * D_FF), w2, b2.reshape(N_EXPERTS, 1, d))


SUBLANES = 8
NEG_BIG = -1e30
HIER_LEVELS = (64, 32, 16, 8, 4)
LOG2E = math.log2(math.e)


def _split3(x):
    h1 = x.astype(BF16)
    r1 = x - h1.astype(F32)
    h2 = r1.astype(BF16)
    h3 = (r1 - h2.astype(F32)).astype(BF16)
    return h1, h2, h3


def _dot_nt(a, b):
    return lax.dot_general(a, b, (((1,), (1,)), ((), ())), preferred_element_type=F32)


def _gla_kernel(q_ref, a_ref, v_ref, lb_ref, *rest, reverse):
    if reverse:
        of_ref, w_ref, o_ref, st_ref, at_ref = rest
    else:
        o_ref, st_ref, at_ref = rest
    Q = HG_CHUNK

    @pl.when(pl.program_id(1) == 0)
    def _():
        st_ref[...] = jnp.zeros_like(st_ref)

    row = lax.broadcasted_iota(jnp.int32, (Q, Q), 0)
    col = lax.broadcasted_iota(jnp.int32, (Q, Q), 1)
    tri = jnp.where((col >= row) if reverse else (col <= row), 1.0, 0.0).astype(BF16)
    same_block = {s: (row // s) == (col // s) for s in HIER_LEVELS[1:] + (2, 1)}
    rowk = lax.broadcasted_iota(jnp.int32, (Q, HG_KDIM), 0)
    q_rows = {s: ((rowk % s) < s // 2) if reverse else ((rowk % s) >= s // 2) for s in HIER_LEVELS + (2,)}
    shp3 = (Q // SUBLANES, SUBLANES, HG_KDIM)
    sub3 = lax.broadcasted_iota(jnp.int32, shp3, 1)

    lb = lb_ref[...]
    a = a_ref[...].astype(F32)
    f_all = jnp.maximum(lb + (1.0 - lb) * jax.nn.sigmoid(a), F_FLOOR)
    kk = (1.0 - lb) * jax.nn.sigmoid(-a)
    g_all = sum(jnp.dot(tri, p, preferred_element_type=F32) for p in _split3(jnp.log(f_all) * LOG2E))
    q_all = q_ref[...].astype(F32)
    q_all = q_all * jax.nn.sigmoid(q_all)
    v_all = v_ref[...]
    tot = 0 if reverse else Q - 1

    def level_ref(g, s):
        half = s // 2
        m_off = half if reverse else half - 1
        if s >= 2 * SUBLANES:
            return jnp.concatenate([jnp.broadcast_to(g[b0 + m_off:b0 + m_off + 1, :], (s, HG_KDIM)) for b0 in range(0, Q, s)], axis=0)
        g3 = g.reshape(shp3)
        ref = jnp.broadcast_to(g3[:, m_off:m_off + 1, :], shp3)
        for b0 in range(s, SUBLANES, s):
            ref = jnp.where(sub3 >= b0, jnp.broadcast_to(g3[:, b0 + m_off:b0 + m_off + 1, :], shp3), ref)
        return ref.reshape(Q, HG_KDIM)

    for h in range(HG_HEADS):
        sl = slice(h * HG_KDIM, (h + 1) * HG_KDIM)
        g, qh, kh, fh = g_all[:, sl], q_all[:, sl], kk[:, sl], f_all[:, sl]
        attn = None
        for s in HIER_LEVELS:
            gref = level_ref(g, s)
            eq = jnp.exp2(jnp.where(q_rows[s], g - gref, NEG_BIG))
            ek = jnp.exp2(jnp.where(q_rows[s], NEG_BIG, gref - g))
            lvl = _dot_nt((qh * eq).astype(BF16), (kh * ek).astype(BF16))
            attn = lvl if attn is None else jnp.where(same_block[s], lvl, attn)
        lvl = _dot_nt(jnp.where(q_rows[2], qh * fh, 0.0).astype(BF16), jnp.where(q_rows[2], 0.0, kh).astype(BF16))
        attn = jnp.where(same_block[2], lvl, attn)
        attn = jnp.where(same_block[1], _dot_nt(qh.astype(BF16), kh.astype(BF16)), attn)
        at_ref[h] = attn.astype(BF16)

    for h in range(HG_HEADS):
        sl = slice(h * HG_KDIM, (h + 1) * HG_KDIM)
        g, qh, kh, vb = g_all[:, sl], q_all[:, sl], kk[:, sl], v_all[:, sl]
        g_tot = g[tot:tot + 1, :]
        st = st_ref[h]
        o = _dot_nt((qh * jnp.exp2(g)).astype(BF16), st.astype(BF16))
        o = o + jnp.dot(at_ref[h], vb, preferred_element_type=F32)
        k_st = (kh * jnp.exp2(g_tot - g)).astype(BF16)
        st_ref[h] = st * jnp.exp2(g_tot) + jnp.dot(vb.astype(F32).T.astype(BF16), k_st, preferred_element_type=F32)
        if reverse:
            o = o + of_ref[:, sl]
            o = o * lax.rsqrt(jnp.mean(o * o, axis=-1, keepdims=True) + EPS) * w_ref[:, sl]
        o_ref[:, sl] = o.astype(o_ref.dtype)


def _hgrn_scan(p_cm, lb, norm_w, n_ctx):
    b, t, _ = p_cm.shape
    col_q, col_f, col_i = 0, 1, 3
    nc, ncc = t // HG_CHUNK, n_ctx // HG_CHUNK
    blk = (None, HG_CHUNK, HG_QK)
    scratch = [pltpu.VMEM((HG_HEADS, HG_VDIM, HG_KDIM), F32), pltpu.VMEM((HG_HEADS, HG_CHUNK, HG_CHUNK), BF16)]
    params = pltpu.CompilerParams(dimension_semantics=("arbitrary", "arbitrary"), vmem_limit_bytes=VMEM_LIMIT)
    row_spec = pl.BlockSpec((1, HG_QK), lambda bi, s: (0, 0))

    def fwd_chunk(s):
        return s

    def bwd_chunk(s):
        return jnp.where(s < ncc, ncc - 1 - s, nc + ncc - 1 - s)

    o_f = pl.pallas_call(
        functools.partial(_gla_kernel, reverse=False),
        grid=(b, nc),
        in_specs=[
            pl.BlockSpec(blk, lambda bi, s: (bi, fwd_chunk(s), col_q)),
            pl.BlockSpec(blk, lambda bi, s: (bi, fwd_chunk(s), col_f)),
            pl.BlockSpec(blk, lambda bi, s: (bi, fwd_chunk(s), col_i)),
            row_spec,
        ],
        out_specs=pl.BlockSpec(blk, lambda bi, s: (bi, fwd_chunk(s), 0)),
        out_shape=jax.ShapeDtypeStruct((b, t, HG_V), F32),
        scratch_shapes=scratch,
        compiler_params=params,
        name="gla_fwd",
    )(p_cm, p_cm, p_cm, lb[0:1])
    return pl.pallas_call(
        functools.partial(_gla_kernel, reverse=True),
        grid=(b, nc),
        in_specs=[
            pl.BlockSpec(blk, lambda bi, s: (bi, bwd_chunk(s), col_q)),
            pl.BlockSpec(blk, lambda bi, s: (bi, bwd_chunk(s), col_f + 1)),
            pl.BlockSpec(blk, lambda bi, s: (bi, bwd_chunk(s), col_i)),
            row_spec,
            pl.BlockSpec(blk, lambda bi, s: (bi, bwd_chunk(s), 0)),
            row_spec,
        ],
        out_specs=pl.BlockSpec(blk, lambda bi, s: (bi, bwd_chunk(s), 0)),
        out_shape=jax.ShapeDtypeStruct((b, t, HG_V), BF16),
        scratch_shapes=scratch,
        compiler_params=params,
        name="gla_bwd",
    )(p_cm, p_cm, p_cm, lb[1:2], o_f, norm_w.reshape(1, HG_V))


CONV_TILE = 256
HALO = 16


def _dwconv_kernel(prev_ref, cur_ref, next_ref, w_ref, b_ref, o_ref, *, taps, n_ctx_tiles, n_tiles, silu):
    i = pl.program_id(1)
    first = jnp.logical_or(i == 0, i == n_ctx_tiles)
    last = jnp.logical_or(i == n_ctx_tiles - 1, i == n_tiles - 1)
    pad = taps // 2
    xp = jnp.where(first, 0.0, prev_ref[...].astype(F32))
    xn = jnp.where(last, 0.0, next_ref[...].astype(F32))
    xcat = jnp.concatenate([xp, cur_ref[...].astype(F32), xn], axis=0)
    acc = jnp.broadcast_to(b_ref[...], cur_ref.shape).astype(F32)
    for k in range(taps):
        off = HALO - pad + k
        acc = acc + w_ref[k:k + 1, :] * xcat[off:off + CONV_TILE, :]
    if silu:
        acc = acc * jax.nn.sigmoid(acc)
    o_ref[...] = acc.astype(o_ref.dtype)


def _dwconv_stream(x, w, bias, n_ctx, silu, c0=0, ct=1024):
    b, t, _ = x.shape
    taps, c = w.shape
    n_tiles = t // CONV_TILE
    hb = CONV_TILE // HALO
    n_halo = t // HALO
    j0 = c0 // ct
    kern = functools.partial(_dwconv_kernel, taps=taps, n_ctx_tiles=n_ctx // CONV_TILE, n_tiles=n_tiles, silu=silu)
    return pl.pallas_call(
        kern,
        grid=(b, n_tiles, c // ct),
        in_specs=[
            pl.BlockSpec((None, HALO, ct), lambda bi, i, j: (bi, jnp.maximum(i * hb - 1, 0), j0 + j)),
            pl.BlockSpec((None, CONV_TILE, ct), lambda bi, i, j: (bi, i, j0 + j)),
            pl.BlockSpec((None, HALO, ct), lambda bi, i, j: (bi, jnp.minimum((i + 1) * hb, n_halo - 1), j0 + j)),
            pl.BlockSpec((taps, ct), lambda bi, i, j: (0, j)),
            pl.BlockSpec((1, ct), lambda bi, i, j: (0, j)),
        ],
        out_specs=pl.BlockSpec((None, CONV_TILE, ct), lambda bi, i, j: (bi, i, j)),
        out_shape=jax.ShapeDtypeStruct((b, t, c), BF16),
        compiler_params=pltpu.CompilerParams(dimension_semantics=("arbitrary",) * 3, vmem_limit_bytes=VMEM_LIMIT),
        name="dwconv",
    )(x, x, x, w.astype(F32), bias.reshape(1, c).astype(F32))


SSM_GHEADS = SSM_HEADS // SSM_GROUPS
SSM_GP = SSM_GHEADS * SSM_HEAD_DIM


def _ssd_kernel(xbc_ref, dt_ref, dtb_ref, a_ref, *rest, reverse):
    if reverse:
        yf_ref, z_ref, dsk_ref, nw_ref, o_ref, st_ref, m_ref, xd_ref, xst_ref, y_ref = rest
    else:
        o_ref, st_ref, m_ref, xd_ref, xst_ref, y_ref = rest
    Q = SSD_CHUNK

    @pl.when(pl.program_id(1) == 0)
    def _():
        st_ref[...] = jnp.zeros_like(st_ref)

    row = lax.broadcasted_iota(jnp.int32, (Q, Q), 0)
    col = lax.broadcasted_iota(jnp.int32, (Q, Q), 1)
    keep = (col >= row) if reverse else (col <= row)
    tri = jnp.where(keep, 1.0, 0.0).astype(BF16)
    expand = jnp.where(lax.broadcasted_iota(jnp.int32, (LANES, SSM_INNER), 1) // SSM_HEAD_DIM == lax.broadcasted_iota(jnp.int32, (LANES, SSM_INNER), 0), 1.0, 0.0).astype(BF16)

    dt = jax.nn.softplus(dt_ref[...] + dtb_ref[...])
    a = dt * a_ref[...]
    cs = sum(jnp.dot(tri, p, preferred_element_type=F32) for p in _split3(a))
    tot = 0 if reverse else Q - 1
    cs_tot = cs[tot:tot + 1, :]
    cs_t = cs.T
    dt_e = jnp.dot(dt.astype(BF16), expand, preferred_element_type=F32)
    e_in = jnp.dot(jnp.exp(cs).astype(BF16), expand, preferred_element_type=F32)
    e_st = jnp.dot(jnp.exp(cs_tot - cs).astype(BF16), expand, preferred_element_type=F32)
    e_tot = jnp.dot(jnp.broadcast_to(jnp.exp(cs_tot), (SUBLANES, LANES)).astype(BF16), expand, preferred_element_type=F32)[0:1, :]

    xs = xbc_ref[:, :SSM_INNER].astype(F32)
    xd_ref[...] = (xs * dt_e).astype(BF16)
    xst_ref[...] = (xs * dt_e * e_st).astype(BF16)
    for g in range(SSM_GROUPS):
        bm = xbc_ref[:, SSM_INNER + g * SSM_STATE:SSM_INNER + (g + 1) * SSM_STATE]
        cm = xbc_ref[:, SSM_INNER + (SSM_GROUPS + g) * SSM_STATE:SSM_INNER + (SSM_GROUPS + g + 1) * SSM_STATE]
        cb = _dot_nt(cm, bm)
        for h in range(g * SSM_GHEADS, (g + 1) * SSM_GHEADS):
            diff = jnp.broadcast_to(cs[:, h:h + 1], (Q, Q)) - jnp.broadcast_to(cs_t[h:h + 1, :], (Q, Q))
            m_ref[h] = (cb * jnp.exp(jnp.where(keep, diff, NEG_BIG))).astype(BF16)

    lane_lo = lax.broadcasted_iota(jnp.int32, (Q, LANES), 1) < SSM_HEAD_DIM
    for g in range(SSM_GROUPS):
        bm = xbc_ref[:, SSM_INNER + g * SSM_STATE:SSM_INNER + (g + 1) * SSM_STATE]
        cm = xbc_ref[:, SSM_INNER + (SSM_GROUPS + g) * SSM_STATE:SSM_INNER + (SSM_GROUPS + g + 1) * SSM_STATE]
        gl = slice(g * SSM_GP, (g + 1) * SSM_GP)
        st = st_ref[g]
        y_off = jnp.dot(cm, st.astype(BF16), preferred_element_type=F32) * e_in[:, gl]
        for hp in range(SSM_GHEADS // 2):
            h0 = g * SSM_GHEADS + 2 * hp
            lanes = slice((h0 // 2) * LANES, (h0 // 2 + 1) * LANES)
            pair = [jnp.dot(m_ref[h], xd_ref[:, lanes], preferred_element_type=F32) for h in (h0, h0 + 1)]
            y_ref[:, lanes] = jnp.where(lane_lo, pair[0], pair[1]) + y_off[:, hp * LANES:(hp + 1) * LANES]
        st_ref[g] = st * e_tot[:, gl] + jnp.dot(bm.astype(F32).T.astype(BF16), xst_ref[:, gl], preferred_element_type=F32)
    y = y_ref[...]
    if reverse:
        y = (y + yf_ref[...] + xbc_ref[:, :SSM_INNER].astype(F32) * dsk_ref[...])
        zz = z_ref[...].astype(F32)
        y = y * (zz * jax.nn.sigmoid(zz))
        y = y * lax.rsqrt(jnp.mean(y * y, axis=-1, keepdims=True) + EPS) * nw_ref[...]
    o_ref[...] = y.astype(o_ref.dtype)


def _ssd_scan(xbc_act, dt2, p_rm, z_col, dt_bias, a_log, d_skip, norm_w, n_ctx):
    b, t, _ = xbc_act.shape
    nc, ncc = t // SSD_CHUNK, n_ctx // SSD_CHUNK

    def pad_heads(v):
        return jnp.pad(v.astype(F32), ((0, 0), (0, LANES - SSM_HEADS)))

    dtb = pad_heads(dt_bias.reshape(2, SSM_HEADS))
    a_neg = pad_heads(-jnp.exp(a_log.astype(F32)))
    dsk = jnp.repeat(d_skip.astype(F32), SSM_HEAD_DIM).reshape(1, SSM_INNER)
    scratch = [pltpu.VMEM((SSM_GROUPS, SSM_STATE, SSM_GP), F32), pltpu.VMEM((SSM_HEADS, SSD_CHUNK, SSD_CHUNK), BF16),
               pltpu.VMEM((SSD_CHUNK, SSM_INNER), BF16), pltpu.VMEM((SSD_CHUNK, SSM_INNER), BF16), pltpu.VMEM((SSD_CHUNK, SSM_INNER), F32)]
    params = pltpu.CompilerParams(dimension_semantics=("arbitrary", "arbitrary"), vmem_limit_bytes=VMEM_LIMIT)

    def bwd_chunk(s):
        return jnp.where(s < ncc, ncc - 1 - s, nc + ncc - 1 - s)

    def specs(chunk, d):
        return [
            pl.BlockSpec((None, SSD_CHUNK, SSM_XBC), lambda bi, s: (bi, chunk(s), 0)),
            pl.BlockSpec((None, SSD_CHUNK, LANES), lambda bi, s: (bi, chunk(s), d)),
            pl.BlockSpec((1, LANES), lambda bi, s: (0, 0)),
            pl.BlockSpec((1, LANES), lambda bi, s: (0, 0)),
        ]

    def inner_spec(chunk):
        return pl.BlockSpec((None, SSD_CHUNK, SSM_INNER), lambda bi, s: (bi, chunk(s), 0))

    row_spec = pl.BlockSpec((1, SSM_INNER), lambda bi, s: (0, 0))
    z_spec = pl.BlockSpec((None, SSD_CHUNK, SSM_INNER), lambda bi, s: (bi, bwd_chunk(s), z_col // SSM_INNER))
    y_f = pl.pallas_call(
        functools.partial(_ssd_kernel, reverse=False),
        grid=(b, nc),
        in_specs=specs(lambda s: s, 0),
        out_specs=inner_spec(lambda s: s),
        out_shape=jax.ShapeDtypeStruct((b, t, SSM_INNER), F32),
        scratch_shapes=scratch,
        compiler_params=params,
        name="ssd_fwd",
    )(xbc_act, dt2, dtb[0:1], a_neg[0:1])
    return pl.pallas_call(
        functools.partial(_ssd_kernel, reverse=True),
        grid=(b, nc),
        in_specs=specs(bwd_chunk, 1) + [inner_spec(bwd_chunk), z_spec, row_spec, row_spec],
        out_specs=inner_spec(bwd_chunk),
        out_shape=jax.ShapeDtypeStruct((b, t, SSM_INNER), BF16),
        scratch_shapes=scratch,
        compiler_params=params,
        name="ssd_bwd",
    )(xbc_act, dt2, dtb[1:2], a_neg[1:2], y_f, p_rm, dsk, norm_w.reshape(1, SSM_INNER).astype(F32))


HY_N2 = LANES
HY_CT = LANES
VMEM_LIMIT_HYENA = 60 * 1024 * 1024


def _hy_dims(L):
    n1 = 2 * L // HY_N2
    k1n = n1 // 2 + 1
    k1p = -(-k1n // SUBLANES) * SUBLANES
    return n1, k1n, k1p


def _hy_tables(L, n1_rows):
    n1, k1n, k1p = _hy_dims(L)
    n = 2 * L
    k1 = np.arange(k1n, dtype=np.float64)[None, :, None]
    nn = (HY_N2 * np.arange(n1_rows, dtype=np.float64)[None, None, :] + np.arange(HY_N2, dtype=np.float64)[:, None, None])
    ang = 2.0 * np.pi * ((k1 * nn) % n) / n
    m1 = np.zeros((HY_N2, 2 * k1p, n1_rows), np.float32)
    m1[:, :k1n] = np.cos(ang)
    m1[:, k1p:k1p + k1n] = -np.sin(ang)
    m4 = np.transpose(m1, (0, 2, 1))
    kk = np.arange(HY_N2, dtype=np.float64)
    a2 = 2.0 * np.pi * ((kk[:, None] * kk[None, :]) % HY_N2) / HY_N2
    c, s = np.cos(a2), np.sin(a2)
    f3 = np.block([[c, s], [-s, c]]).astype(np.float32)
    f3i = np.block([[c, -s], [s, c]]).astype(np.float32)
    return jnp.asarray(m1, BF16), jnp.asarray(m4, BF16), jnp.asarray(f3, BF16), jnp.asarray(f3i, BF16)


def _hy_stage1(u_ref, a_ref, m1_ref, n1_rows, k1p):
    def body(n2, carry):
        xs = u_ref[pl.ds(n2, n1_rows, stride=HY_N2), :].astype(BF16)
        a_ref[pl.ds(pl.multiple_of(n2 * 2 * k1p, 2 * k1p), 2 * k1p), :] = jnp.dot(m1_ref[n2], xs, preferred_element_type=F32)
        return carry

    lax.fori_loop(0, HY_N2, body, 0, unroll=8)


def _hy_spectrum_slab(a_ref, f3_ref, k1, k1p):
    blk = jnp.concatenate([a_ref[pl.ds(k1, HY_N2, stride=2 * k1p), :], a_ref[pl.ds(k1p + k1, HY_N2, stride=2 * k1p), :]], axis=0)
    return jnp.dot(f3_ref[...], blk.astype(BF16), preferred_element_type=F32)


def _hy_conv(u_ref, yo_ref, a_ref, y_ref, h_ref, order, m1_ref, m4_ref, f3_ref, f3i_ref, n1_rows, k1n, k1p):
    _hy_stage1(u_ref, a_ref, m1_ref, n1_rows, k1p)

    def stage2(k1, carry):
        x = _hy_spectrum_slab(a_ref, f3_ref, k1, k1p)
        h = h_ref[order, pl.ds(pl.multiple_of(k1 * 2 * HY_N2, 2 * HY_N2), 2 * HY_N2), :].astype(F32)
        xr, xi, hr, hi = x[:HY_N2], x[HY_N2:], h[:HY_N2], h[HY_N2:]
        z = jnp.concatenate([xr * hr - xi * hi, xr * hi + xi * hr], axis=0).astype(BF16)
        c = jnp.dot(f3i_ref[...], z, preferred_element_type=F32)
        y_ref[pl.ds(k1, HY_N2, stride=2 * k1p), :] = c[:HY_N2]
        y_ref[pl.ds(k1p + k1, HY_N2, stride=2 * k1p), :] = c[HY_N2:]
        return carry

    lax.fori_loop(0, k1n - 1, stage2, 0, unroll=4)
    stage2(k1n - 1, 0)

    def stage3(n2, carry):
        d = y_ref[pl.ds(pl.multiple_of(n2 * 2 * k1p, 2 * k1p), 2 * k1p), :].astype(BF16)
        yo_ref[pl.ds(n2, n1_rows, stride=HY_N2), :] = jnp.dot(m4_ref[n2], d, preferred_element_type=F32)
        return carry

    lax.fori_loop(0, HY_N2, stage3, 0, unroll=8)


def _hyena_kernel(v_ref, x1_ref, x2_ref, h_ref, m1_ref, m4_ref, f3_ref, f3i_ref, bias_ref, o_ref, a_ref, y_ref, u_ref, yo_ref, *, n1_rows, k1n, k1p):
    @pl.when(jnp.logical_and(pl.program_id(0) == 0, pl.program_id(1) == 0))
    def _():
        y_ref[...] = jnp.zeros_like(y_ref)

    u_ref[...] = v_ref[...].astype(F32)
    for order, gate_ref in enumerate((x1_ref, x2_ref)):
        _hy_conv(u_ref, yo_ref, a_ref, y_ref, h_ref, order, m1_ref, m4_ref, f3_ref, f3i_ref, n1_rows, k1n, k1p)
        z = gate_ref[...].astype(F32) * (yo_ref[...] + u_ref[...] * bias_ref[order:order + 1, :])
        if order == 0:
            u_ref[...] = z
        else:
            o_ref[...] = z.astype(o_ref.dtype)


def _hy_filter_kernel(hid_ref, wf_ref, wb_ref, bf_ref, bb_ref, dl_ref, m1_ref, f3_ref, o_ref, a_ref, sf_ref, uf_ref, ub_ref, *, n1_rows, k1n, k1p, scale_mid, scale_edge):
    slab = 2 * HY_N2
    L = uf_ref.shape[0]
    hid_rows = math.gcd(L, 2 * HY_N2)

    def fill(w_ref, b_ref, dst_ref, drop_first):
        w = _split3(w_ref[...])

        def body(i, energy):
            rows = pl.ds(pl.multiple_of(i * hid_rows, hid_rows), hid_rows)
            hid = _split3(hid_ref[rows, :])
            acc = sum(jnp.dot(hid[p], w[q], preferred_element_type=F32) for p in range(2) for q in range(2 - p))
            pos = i * hid_rows + lax.broadcasted_iota(jnp.int32, (hid_rows, HY_CT), 0)
            vals = (acc + b_ref[...]) * jnp.exp(pos.astype(F32) * (-1.0 / (L - 1)) * dl_ref[...])
            if drop_first:
                vals = jnp.where(pos == 0, 0.0, vals)
            dst_ref[rows, :] = vals
            return energy + jnp.sum(vals * vals, axis=0, keepdims=True)

        return lax.fori_loop(0, L // hid_rows, body, jnp.zeros((1, HY_CT), F32), unroll=math.gcd(L // hid_rows, 4))

    norm = lax.rsqrt(fill(wf_ref, bf_ref, uf_ref, False) + fill(wb_ref, bb_ref, ub_ref, True) + EPS)

    _hy_stage1(uf_ref, a_ref, m1_ref, n1_rows, k1p)

    def keep_fwd(k1, carry):
        sf_ref[pl.ds(pl.multiple_of(k1 * slab, slab), slab), :] = _hy_spectrum_slab(a_ref, f3_ref, k1, k1p)
        return carry

    lax.fori_loop(0, k1n - 1, keep_fwd, 0, unroll=4)
    keep_fwd(k1n - 1, 0)
    _hy_stage1(ub_ref, a_ref, m1_ref, n1_rows, k1p)

    def combine(k1, carry):
        xb = _hy_spectrum_slab(a_ref, f3_ref, k1, k1p)
        xf = sf_ref[pl.ds(pl.multiple_of(k1 * slab, slab), slab), :]
        w = norm * jnp.where(jnp.logical_or(k1 == 0, k1 == k1n - 1), scale_edge, scale_mid)
        h = jnp.concatenate([xf[:HY_N2] + xb[:HY_N2], xf[HY_N2:] - xb[HY_N2:]], axis=0)
        o_ref[pl.ds(pl.multiple_of(k1 * slab, slab), slab), :] = (h * w).astype(o_ref.dtype)
        return carry

    lax.fori_loop(0, k1n - 1, combine, 0, unroll=4)
    combine(k1n - 1, 0)


def _single(block_shape, index_map):
    return pl.BlockSpec(block_shape, index_map, pipeline_mode=pl.Buffered(1))


def _hyena_filter_spectrum_pallas(hidden, w3, b3, deltas):
    L, ffn = hidden.shape
    c = deltas.shape[0]
    nct = c // HY_CT
    n = 2 * L
    n1, k1n, k1p = _hy_dims(L)
    n1_rows = L // HY_N2
    m1, _, f3, _ = _hy_tables(L, n1_rows)
    kern = functools.partial(_hy_filter_kernel, n1_rows=n1_rows, k1n=k1n, k1p=k1p, scale_mid=2.0 / n, scale_edge=1.0 / n)
    return pl.pallas_call(
        kern,
        grid=(HY_ORDER, nct),
        in_specs=[
            _single((L, ffn), lambda o, j: (0, 0)),
            pl.BlockSpec((ffn, HY_CT), lambda o, j: (0, o * nct + j)),
            pl.BlockSpec((ffn, HY_CT), lambda o, j: (0, (HY_ORDER + o) * nct + j)),
            pl.BlockSpec((1, HY_CT), lambda o, j: (0, o * nct + j)),
            pl.BlockSpec((1, HY_CT), lambda o, j: (0, (HY_ORDER + o) * nct + j)),
            pl.BlockSpec((1, HY_CT), lambda o, j: (0, j)),
            _single((HY_N2, 2 * k1p, n1_rows), lambda o, j: (0, 0, 0)),
            _single((2 * HY_N2, 2 * HY_N2), lambda o, j: (0, 0)),
        ],
        out_specs=pl.BlockSpec((None, k1n * 2 * HY_N2, HY_CT), lambda o, j: (o, 0, j)),
        out_shape=jax.ShapeDtypeStruct((HY_ORDER, k1n * 2 * HY_N2, c), BF16),
        scratch_shapes=[
            pltpu.VMEM((k1p * 2 * HY_N2, HY_CT), F32),
            pltpu.VMEM((k1n * 2 * HY_N2, HY_CT), F32),
            pltpu.VMEM((L, HY_CT), F32),
            pltpu.VMEM((L, HY_CT), F32),
        ],
        compiler_params=pltpu.CompilerParams(dimension_semantics=("arbitrary", "arbitrary"), vmem_limit_bytes=VMEM_LIMIT_HYENA),
        name="hyena_filter_dft",
    )(hidden, w3, w3, b3.reshape(1, -1), b3.reshape(1, -1), deltas.reshape(1, c), m1, f3)


def _hyena_long(hy, h_spec, bias):
    b, L, c3 = hy.shape
    c = c3 // (HY_ORDER + 1)
    nct = c // HY_CT
    n1, k1n, k1p = _hy_dims(L)
    n1_rows = L // HY_N2
    m1, m4, f3, f3i = _hy_tables(L, n1_rows)
    kern = functools.partial(_hyena_kernel, n1_rows=n1_rows, k1n=k1n, k1p=k1p)

    def col(part):
        return _single((None, L, HY_CT), lambda j, bi: (bi, 0, part * nct + j))

    return pl.pallas_call(
        kern,
        grid=(nct, b),
        in_specs=[
            col(0), col(1), col(2),
            _single((HY_ORDER, k1n * 2 * HY_N2, HY_CT), lambda j, bi: (0, 0, j)),
            _single((HY_N2, 2 * k1p, n1_rows), lambda j, bi: (0, 0, 0)),
            _single((HY_N2, n1_rows, 2 * k1p), lambda j, bi: (0, 0, 0)),
            _single((2 * HY_N2, 2 * HY_N2), lambda j, bi: (0, 0)),
            _single((2 * HY_N2, 2 * HY_N2), lambda j, bi: (0, 0)),
            pl.BlockSpec((HY_ORDER, HY_CT), lambda j, bi: (0, j)),
        ],
        out_specs=pl.BlockSpec((None, L, HY_CT), lambda j, bi: (bi, 0, j)),
        out_shape=jax.ShapeDtypeStruct((b, L, c), BF16),
        scratch_shapes=[
            pltpu.VMEM((k1p * 2 * HY_N2, HY_CT), F32),
            pltpu.VMEM((HY_N2 * 2 * k1p, HY_CT), F32),
            pltpu.VMEM((L, HY_CT), F32),
            pltpu.VMEM((L, HY_CT), F32),
        ],
        compiler_params=pltpu.CompilerParams(dimension_semantics=("arbitrary", "arbitrary"), vmem_limit_bytes=VMEM_LIMIT_HYENA),
        name="hyena_long_conv",
    )(hy, hy, hy, h_spec, m1, m4, f3, f3i, bias.astype(F32))


HY_CTX_CT = 256


def _hy_ctx_tables(L):
    n = 2 * L
    kb = L + 1
    kp = -(-kb // LANES) * LANES
    ang = 2.0 * np.pi * ((np.arange(kb, dtype=np.float64)[:, None] * np.arange(L, dtype=np.float64)[None, :]) % n) / n
    fwd = np.zeros((2 * kp, L), np.float32)
    fwd[:kb] = np.cos(ang)
    fwd[kp:kp + kb] = -np.sin(ang)
    return jnp.asarray(fwd, BF16), jnp.asarray(fwd.T, BF16), kb, kp


def _hy_ctx_filter_kernel(fwd_ref, bwd_ref, f_ref, o_ref, *, kb, kp, n):
    fwd = fwd_ref[...]
    bwd = jnp.where(lax.broadcasted_iota(jnp.int32, fwd.shape, 0) == 0, 0.0, bwd_ref[...])
    norm = lax.rsqrt(jnp.sum(fwd * fwd, axis=0, keepdims=True) + jnp.sum(bwd * bwd, axis=0, keepdims=True) + EPS)
    hf = jnp.dot(f_ref[...], fwd.astype(BF16), preferred_element_type=F32)
    hb = jnp.dot(f_ref[...], bwd.astype(BF16), preferred_element_type=F32)
    row = lax.broadcasted_iota(jnp.int32, hf.shape, 0)
    imag = row >= kp
    k = jnp.where(imag, row - kp, row)
    wk = jnp.where(jnp.logical_or(k == 0, k == kb - 1), 1.0 / n, 2.0 / n)
    o_ref[...] = (hf + jnp.where(imag, -hb, hb)) * (wk * norm)


def _hy_ctx_kernel(v_ref, x1_ref, x2_ref, h_ref, f_ref, g_ref, bias_ref, o_ref, *, kp):
    u = v_ref[...].astype(F32)
    for order, gate_ref in enumerate((x1_ref, x2_ref)):
        x = jnp.dot(f_ref[...], u.astype(BF16), preferred_element_type=F32)
        h = h_ref[order]
        xr, xi, hr, hi = x[:kp], x[kp:], h[:kp], h[kp:]
        z = jnp.concatenate([xr * hr - xi * hi, xr * hi + xi * hr], axis=0).astype(BF16)
        y = jnp.dot(g_ref[...], z, preferred_element_type=F32)
        u = gate_ref[...].astype(F32) * (y + u * bias_ref[order:order + 1, :])
    o_ref[...] = u.astype(o_ref.dtype)


def _hyena_ctx(hy, L, taps, bias):
    b = hy.shape[0]
    c = hy.shape[2] // (HY_ORDER + 1)
    ct = HY_CTX_CT
    nct = c // ct
    f_mat, g_mat, kb, kp = _hy_ctx_tables(L)
    params = pltpu.CompilerParams(dimension_semantics=("arbitrary", "arbitrary"), vmem_limit_bytes=VMEM_LIMIT)
    h_spec = pl.pallas_call(
        functools.partial(_hy_ctx_filter_kernel, kb=kb, kp=kp, n=2 * L),
        grid=(HY_ORDER, nct),
        in_specs=[
            pl.BlockSpec((L, ct), lambda o, j: (0, o * nct + j)),
            pl.BlockSpec((L, ct), lambda o, j: (0, (HY_ORDER + o) * nct + j)),
            pl.BlockSpec((2 * kp, L), lambda o, j: (0, 0)),
        ],
        out_specs=pl.BlockSpec((None, 2 * kp, ct), lambda o, j: (o, 0, j)),
        out_shape=jax.ShapeDtypeStruct((HY_ORDER, 2 * kp, c), F32),
        compiler_params=params,
        name="hyena_ctx_filter",
    )(taps, taps, f_mat)

    def col(part):
        return pl.BlockSpec((None, L, ct), lambda j, bi: (bi, 0, part * nct + j))

    return pl.pallas_call(
        functools.partial(_hy_ctx_kernel, kp=kp),
        grid=(nct, b),
        in_specs=[
            col(0), col(1), col(2),
            pl.BlockSpec((HY_ORDER, 2 * kp, ct), lambda j, bi: (0, 0, j)),
            pl.BlockSpec((2 * kp, L), lambda j, bi: (0, 0)),
            pl.BlockSpec((L, 2 * kp), lambda j, bi: (0, 0)),
            pl.BlockSpec((HY_ORDER, ct), lambda j, bi: (0, j)),
        ],
        out_specs=pl.BlockSpec((None, L, ct), lambda j, bi: (bi, 0, j)),
        out_shape=jax.ShapeDtypeStruct((b, L, c), BF16),
        compiler_params=params,
        name="hyena_ctx_conv",
    )(hy, hy, hy, h_spec, f_mat, g_mat, bias.astype(F32))


def _split_cols(t, sizes):
    return jnp.split(t, np.cumsum(sizes)[:-1].tolist(), axis=-1)


def _rms_norm(x, w):
    xf = x.astype(F32)
    y = xf * lax.rsqrt(jnp.mean(xf * xf, axis=-1, keepdims=True) + EPS)
    return (y * w.astype(F32)).astype(x.dtype)


def _dwconv_centred(x, w, b):
    k = w.shape[0]
    y = lax.conv_general_dilated(x, w[:, None, :].astype(x.dtype), window_strides=(1,), padding=[(k // 2, k // 2)], dimension_numbers=('NWC', 'WIO', 'NWC'), feature_group_count=x.shape[-1], precision=lax.Precision.HIGHEST)
    return y + b.astype(x.dtype)


def _flip_seq(t):
    return jnp.flip(t, axis=1)


def _to_col_major(t, rows):
    b, rest = t.shape[0], t.shape[2:]
    return jnp.swapaxes(t.reshape((b, rows, GRID_W) + rest), 1, 2).reshape((b, rows * GRID_W) + rest)


def _from_col_major(t, rows):
    b, rest = t.shape[0], t.shape[2:]
    return jnp.swapaxes(t.reshape((b, GRID_W, rows) + rest), 1, 2).reshape((b, rows * GRID_W) + rest)


def _hyena_filter_hidden(L, w1, b1, w2, b2, freq):
    hp = lax.Precision.HIGHEST
    t = jnp.linspace(0.0, 1.0, L, dtype=F32)[:, None]
    w = 2.0 * math.pi * jnp.arange(L, dtype=F32)[:, None] / L
    bands = jnp.linspace(1e-4, HY_BANDS - 1, HY_BANDS, dtype=F32)
    feats = jnp.concatenate([t, jnp.cos(bands * w), -jnp.sin(bands * w)], axis=-1)
    h = jnp.sin(freq[0] * (jnp.dot(feats, w1, precision=hp) + b1))
    return jnp.sin(freq[1] * (jnp.dot(h, w2, precision=hp) + b2))


def _hyena_decay_rates():
    max_decay = math.log(HY_DECAY_TARGET) / HY_FAST_DECAY
    min_decay = math.log(HY_DECAY_TARGET) / HY_SLOW_DECAY
    return jnp.abs(jnp.linspace(min_decay, max_decay, HY_WIDTH, dtype=F32))


def _hyena_filter_taps(L, w1, b1, w2, b2, w3, b3, freq):
    h = jnp.dot(_hyena_filter_hidden(L, w1, b1, w2, b2, freq), w3, precision=lax.Precision.HIGHEST) + b3
    t = jnp.linspace(0.0, 1.0, L, dtype=F32)[:, None]
    return h * jnp.tile(jnp.exp(-t * _hyena_decay_rates()), (1, 2 * HY_ORDER))


def _mixer_branches(h_rm, w_in, lb, n_ctx, ssm_conv_w, ssm_conv_b, ssm_dt_bias, ssm_a_log, ssm_d, ssm_norm, hy_conv_w, hy_conv_b, hy_w1, hy_b1, hy_w2, hy_b2, hy_w3, hy_b3, hy_freq, hy_bias, hg_norm):
    b, t, d = h_rm.shape
    n_lat = t - n_ctx
    rows = n_lat // GRID_W
    w_z, w_xbc, w_dt, w_hy, w_q, w_f, w_i, w_g, w_gate = _split_cols(w_in, IN_SIZES)
    h2 = h_rm.reshape(b * t, d)
    rm_parts = (w_z, w_xbc, w_hy, w_g, w_gate)
    col_z, col_xbc, col_hy, col_g, col_gate = np.cumsum([0] + [w.shape[1] for w in rm_parts[:-1]]).tolist()
    p_rm = _mm(h2, jnp.concatenate(rm_parts, axis=1).astype(BF16), BF16).reshape(b, t, -1)

    zero_pad = jnp.zeros((d, LANES - SSM_HEADS), F32)
    w_dt2 = jnp.concatenate([w_dt[:, :SSM_HEADS], zero_pad, w_dt[:, SSM_HEADS:], zero_pad], axis=1)
    dt2 = _mm(h2, w_dt2.astype(BF16), F32).reshape(b, t, 2 * LANES)
    xbc_act = _dwconv_stream(p_rm, ssm_conv_w, ssm_conv_b, n_ctx, True, c0=col_xbc)
    ym = _ssd_scan(xbc_act, dt2, p_rm, col_z, ssm_dt_bias, ssm_a_log, ssm_d, ssm_norm, n_ctx)

    hy = _dwconv_stream(p_rm, hy_conv_w, hy_conv_b, n_ctx, False, c0=col_hy)
    taps_ctx = _hyena_filter_taps(n_ctx, hy_w1, hy_b1, hy_w2, hy_b2, hy_w3, hy_b3, hy_freq)
    yh_ctx = _hyena_ctx(hy, n_ctx, taps_ctx, hy_bias)
    h_spec = _hyena_filter_spectrum_pallas(_hyena_filter_hidden(n_lat, hy_w1, hy_b1, hy_w2, hy_b2, hy_freq), hy_w3, hy_b3, _hyena_decay_rates())
    yh_lat = _hyena_long(hy[:, n_ctx:], h_spec, hy_bias)
    yh = jnp.concatenate([yh_ctx, yh_lat], axis=1)

    h_cm = jnp.concatenate([h_rm[:, :n_ctx], _to_col_major(h_rm[:, n_ctx:], rows)], axis=1).reshape(b * t, d)

    p_cm = _mm(h_cm, jnp.concatenate([w_q, w_f, w_i], axis=1).astype(BF16), BF16).reshape(b, t, -1)
    og = _hgrn_scan(p_cm, lb, hg_norm, n_ctx)
    og = jnp.concatenate([og[:, :n_ctx], _from_col_major(og[:, n_ctx:], rows)], axis=1)
    return ym, yh, og, p_rm, col_g, col_gate


ROW_TILE = 256
MOD_ROWS = SUBLANES
M_SHIFT_MIX, M_SCALE_MIX, M_GATE_MIX, M_SHIFT_FFN, M_SCALE_FFN, M_GATE_FFN = range(6)
ROW_PARAMS = pltpu.CompilerParams(dimension_semantics=("arbitrary", "arbitrary"), vmem_limit_bytes=VMEM_LIMIT)


def _rms(x):
    return x * lax.rsqrt(jnp.mean(x * x, axis=-1, keepdims=True) + EPS)


def _mrow(m_ref, r):
    return m_ref[r:r + 1, :]


def _row_spec(width):
    return pl.BlockSpec((None, ROW_TILE, width), lambda bi, i: (bi, i, 0))


def _vec_spec(width):
    return pl.BlockSpec((1, width), lambda bi, i: (0, 0))


def _mat_spec(k, n):
    return pl.BlockSpec((k, n), lambda bi, i: (0, 0))


def _mod_spec(n_ctx):
    return pl.BlockSpec((None, None, MOD_ROWS, D_MODEL), lambda bi, i: (bi, jnp.where(i < n_ctx // ROW_TILE, 0, 1), 0, 0))


def _norm_mod_kernel(x_ref, w_ref, m_ref, o_ref):
    y = _rms(x_ref[...]) * w_ref[...]
    o_ref[...] = (y * (1.0 + _mrow(m_ref, M_SCALE_MIX)) + _mrow(m_ref, M_SHIFT_MIX)).astype(o_ref.dtype)


def _norm_mod(xs, w, mods, n_ctx):
    b, t, d = xs.shape
    return pl.pallas_call(
        _norm_mod_kernel,
        grid=(b, t // ROW_TILE),
        in_specs=[_row_spec(d), _vec_spec(d), _mod_spec(n_ctx)],
        out_specs=_row_spec(d),
        out_shape=jax.ShapeDtypeStruct((b, t, d), BF16),
        compiler_params=ROW_PARAMS,
        name="norm_mod",
    )(xs, w.reshape(1, d), mods)


def _merge_kernel(ym_ref, yh_ref, og_ref, g_ref, gm_ref, gh_ref, gg_ref, x_ref, m_ref, w1_ref, w2_ref, w3_ref, wo_ref, npost_ref, npre_ref, rw_ref, rb_ref, xo_ref, h_ref, lg_ref):
    def sig(ref):
        return jax.nn.sigmoid(ref[...].astype(F32))

    go = g_ref[...].astype(F32)
    yg = (og_ref[...].astype(F32) * (go * jax.nn.sigmoid(go))).astype(BF16)
    merged = sig(gm_ref) * jnp.dot(ym_ref[...], w1_ref[...], preferred_element_type=F32)
    merged = merged + sig(gh_ref) * jnp.dot(yh_ref[...], w2_ref[...], preferred_element_type=F32)
    merged = merged + sig(gg_ref) * jnp.dot(yg, w3_ref[...], preferred_element_type=F32)
    mix = jnp.dot(merged.astype(BF16), wo_ref[...], preferred_element_type=F32)
    x = x_ref[...] + _mrow(m_ref, M_GATE_MIX) * (_rms(mix) * npost_ref[...])
    xo_ref[...] = x
    h = (_rms(x) * npre_ref[...] * (1.0 + _mrow(m_ref, M_SCALE_FFN)) + _mrow(m_ref, M_SHIFT_FFN)).astype(BF16)
    h_ref[...] = h
    lg_ref[...] = jnp.dot(h, rw_ref[...], preferred_element_type=F32) + rb_ref[...]


def _merge(ym, yh, og, p_rm, col_g, col_gate, xs, mods, w_br_ssm, w_br_hy, w_br_hg, w_out, norm_post, norm_ffn_pre, router_w, router_b, n_ctx):
    b, t, d = xs.shape
    rw = jnp.pad(router_w, ((0, 0), (0, LANES - N_EXPERTS))).astype(BF16)
    rb = jnp.pad(router_b, (0, LANES - N_EXPERTS)).reshape(1, LANES).astype(F32)

    def col_spec(col):
        return pl.BlockSpec((None, ROW_TILE, d), lambda bi, i: (bi, i, col // d))

    return pl.pallas_call(
        _merge_kernel,
        grid=(b, t // ROW_TILE),
        in_specs=[_row_spec(d), _row_spec(d), _row_spec(d), col_spec(col_g), col_spec(col_gate), col_spec(col_gate + d), col_spec(col_gate + 2 * d), _row_spec(d), _mod_spec(n_ctx),
                  _mat_spec(d, d), _mat_spec(d, d), _mat_spec(d, d), _mat_spec(d, d), _vec_spec(d), _vec_spec(d), _mat_spec(d, LANES), _vec_spec(LANES)],
        out_specs=[_row_spec(d), _row_spec(d), _row_spec(LANES)],
        out_shape=[jax.ShapeDtypeStruct((b, t, d), F32), jax.ShapeDtypeStruct((b, t, d), BF16), jax.ShapeDtypeStruct((b, t, LANES), F32)],
        compiler_params=ROW_PARAMS,
        name="branch_merge",
    )(ym, yh, og, p_rm, p_rm, p_rm, p_rm, xs, mods, w_br_ssm.astype(BF16), w_br_hy.astype(BF16), w_br_hg.astype(BF16), w_out.astype(BF16),
      norm_post.reshape(1, d), norm_ffn_pre.reshape(1, d), rw, rb)


def _post_ffn_kernel(y0_ref, y1_ref, y2_ref, y3_ref, x_ref, m_ref, w_ref, o_ref):
    f = y0_ref[...].astype(F32) + y1_ref[...].astype(F32) + y2_ref[...].astype(F32) + y3_ref[...].astype(F32)
    o_ref[...] = x_ref[...] + _mrow(m_ref, M_GATE_FFN) * (_rms(f) * w_ref[...])


def _post_ffn(f4, xs, mods, norm_post, n_ctx, skip):
    b, _, d = xs.shape
    t = f4.shape[2]
    i0 = skip // ROW_TILE

    def k_spec(k):
        return pl.BlockSpec((None, None, ROW_TILE, d), lambda bi, i: (k, bi, i, 0))

    return pl.pallas_call(
        _post_ffn_kernel,
        grid=(b, t // ROW_TILE),
        in_specs=[k_spec(k) for k in range(TOP_K)] + [
            pl.BlockSpec((None, ROW_TILE, d), lambda bi, i: (bi, i0 + i, 0)),
            pl.BlockSpec((None, None, MOD_ROWS, d), lambda bi, i: (bi, jnp.where(i0 + i < n_ctx // ROW_TILE, 0, 1), 0, 0)),
            _vec_spec(d)],
        out_specs=_row_spec(d),
        out_shape=jax.ShapeDtypeStruct((b, t, d), F32),
        compiler_params=ROW_PARAMS,
        name="post_ffn",
    )(f4, f4, f4, f4, xs, mods, norm_post.reshape(1, d))


def _moe_ffn(h2, logits, t_per_b, skip, li, w1, b1, w2, b2):
    t = logits.shape[0]
    d = h2.shape[1]
    n = t * TOP_K
    n_tiles = n // MOE_BLOCK
    top_v, top_e = lax.top_k(logits, TOP_K)
    gate_w = jax.nn.softmax(top_v, axis=-1)
    flat_e = top_e.reshape(n).astype(jnp.int32)
    iota = jnp.arange(n, dtype=jnp.int32)
    _, order, sw = lax.sort((flat_e, iota, gate_w.reshape(n)), num_keys=1, is_stable=True)
    _, inv = lax.sort((order, iota), num_keys=1)
    tok = order // TOP_K
    xs = h2[tok + (tok // t_per_b + 1) * skip]
    counts = jnp.sum((flat_e[:, None] == jnp.arange(N_EXPERTS, dtype=jnp.int32)[None, :]).astype(jnp.int32), axis=0)
    end = jnp.cumsum(counts)
    start = end - counts
    first_tile = start // MOE_BLOCK
    n_items = jnp.where(counts > 0, (end - 1) // MOE_BLOCK - first_tile + 1, 0)
    items_end = jnp.cumsum(n_items)
    w = jnp.arange(n_tiles + N_EXPERTS, dtype=jnp.int32)
    valid = w < items_end[-1]
    e_w = jnp.minimum(jnp.sum((w[:, None] >= items_end[None, :]).astype(jnp.int32), axis=1), N_EXPERTS - 1)
    tile_w = first_tile[e_w] + (w - (items_end[e_w] - n_items[e_w]))
    lo = jnp.where(valid, jnp.maximum(start[e_w], tile_w * MOE_BLOCK), 0)
    hi = jnp.where(valid, jnp.minimum(end[e_w], (tile_w + 1) * MOE_BLOCK), 0)
    tile_w = jnp.where(valid, tile_w, n_tiles - 1)
    first = jnp.concatenate([jnp.ones((1,), jnp.int32), (tile_w[1:] != tile_w[:-1]).astype(jnp.int32)])
    newexp = jnp.concatenate([jnp.ones((1,), jnp.int32), (e_w[1:] != e_w[:-1]).astype(jnp.int32)])
    ys = _moe_experts(xs, sw, tile_w.astype(jnp.int32), e_w, lo.astype(jnp.int32), hi.astype(jnp.int32), first, newexp, li, w1, b1, w2, b2)
    return ys[inv.reshape(t, TOP_K).T]


def kernel(x, c, ctx, c_ctx, w_mod, b_mod, norm_mix_pre, norm_mix_post, norm_ffn_pre, norm_ffn_post, w_in, ssm_conv_w, ssm_conv_b, ssm_dt_bias, ssm_a_log, ssm_d, ssm_norm, hy_conv_w, hy_conv_b, hy_w1, hy_b1, hy_w2, hy_b2, hy_w3, hy_b3, hy_freq, hy_bias, hg_lb_logits, hg_norm, w_br_ssm, w_br_hy, w_br_hg, w_out, router_w, router_b, exp_w1, exp_b1, exp_w2, exp_b2):
    hp = lax.Precision.HIGHEST
    b, n_lat, d = x.shape
    n_ctx = ctx.shape[1]
    lb = jax.nn.softmax(hg_lb_logits.astype(F32), axis=1)
    lb = jnp.cumsum(lb, axis=1) - lb[:, :1]
    silu_c = jax.nn.silu(c)
    silu_cc = jax.nn.silu(c_ctx)
    xs = jnp.concatenate([ctx, x], axis=1)
    for li in range(DEPTH):
        mx = (jnp.dot(silu_c, w_mod[li], precision=hp) + b_mod[li]).reshape(b, 1, 6, d)
        mc = jnp.broadcast_to((jnp.dot(silu_cc, w_mod[li], precision=hp) + b_mod[li]).reshape(1, 1, 6, d), (b, 1, 6, d))
        mods = jnp.pad(jnp.concatenate([mc, mx], axis=1), ((0, 0), (0, 0), (0, MOD_ROWS - 6), (0, 0)))
        h = _norm_mod(xs, norm_mix_pre[li], mods, n_ctx)
        ym, yh, og, p_rm, col_g, col_gate = _mixer_branches(h, w_in[li], lb[:, li], n_ctx, ssm_conv_w[li], ssm_conv_b[li], ssm_dt_bias[li], ssm_a_log[li], ssm_d[li], ssm_norm[li], hy_conv_w[li], hy_conv_b[li], hy_w1[li], hy_b1[li], hy_w2[li], hy_b2[li], hy_w3[li], hy_b3[li], hy_freq[li], hy_bias[li], hg_norm[li])
        xs, h_ffn, logits = _merge(ym, yh, og, p_rm, col_g, col_gate, xs, mods, w_br_ssm[li], w_br_hy[li], w_br_hg[li], w_out[li], norm_mix_post[li], norm_ffn_pre[li], router_w[li], router_b[li], n_ctx)
        skip = n_ctx if li == DEPTH - 1 else 0
        t = xs.shape[1] - skip
        f4 = _moe_ffn(h_ffn.reshape(-1, d), logits[:, skip:, :N_EXPERTS].reshape(b * t, N_EXPERTS), t, skip, li, exp_w1, exp_b1[li], exp_w2, exp_b2[li])
        xs = _post_ffn(f4.reshape(TOP_K, b, t, d), xs, mods, norm_ffn_post[li], n_ctx, skip)
    return xs
```

```python
import functools
import math

import jax
import jax.numpy as jnp
import numpy as np
from jax import lax
from jax.experimental import pallas as pl
from jax.experimental.pallas import tpu as pltpu

D_MODEL = 1024
DEPTH = 2
GRID_W = 64

SSM_HEADS = 16
SSM_HEAD_DIM = 64
SSM_INNER = SSM_HEADS * SSM_HEAD_DIM
SSM_STATE = 128
SSM_GROUPS = 4
SSD_CHUNK = 128
SSM_XBC = SSM_INNER + 2 * SSM_GROUPS * SSM_STATE

HY_WIDTH = D_MODEL
HY_ORDER = 2
HY_BANDS = 16
HY_FAST_DECAY = 0.3
HY_SLOW_DECAY = 1.5
HY_DECAY_TARGET = 1e-2

HG_HEADS = 8
HG_KDIM = 128
HG_VDIM = D_MODEL // HG_HEADS
HG_QK = HG_HEADS * HG_KDIM
HG_V = HG_HEADS * HG_VDIM
HG_CHUNK = 64
F_FLOOR = 1e-20

N_EXPERTS = 32
TOP_K = 4
D_FF = D_MODEL
SWIGLU_LIMIT = 7.0
SWIGLU_ALPHA = 1.702
MOE_BLOCK = 256

N_BRANCHES = 3
IN_SIZES = (SSM_INNER, SSM_XBC, 2 * SSM_HEADS, (HY_ORDER + 1) * HY_WIDTH, HG_QK, 2 * HG_QK, HG_V, HG_V, N_BRANCHES * D_MODEL)
EPS = 1e-6
F32 = jnp.float32
BF16 = jnp.bfloat16

LANES = 128
VMEM_LIMIT = 56 * 1024 * 1024


def _mm_kernel(a_ref, b_ref, o_ref):
    o_ref[...] = jnp.dot(a_ref[...], b_ref[...], preferred_element_type=F32).astype(o_ref.dtype)


def _mm(a, b, out_dtype=F32, tm=1024, tn=1024):
    m, k = a.shape
    n = b.shape[1]
    tm = math.gcd(m, tm)
    tn = math.gcd(n, tn)
    assert tm % SUBLANES == 0 and tn % LANES == 0, (m, n, tm, tn)
    return pl.pallas_call(
        _mm_kernel,
        grid=(n // tn, m // tm),
        in_specs=[pl.BlockSpec((tm, k), lambda j, i: (i, 0)), pl.BlockSpec((k, tn), lambda j, i: (0, j))],
        out_specs=pl.BlockSpec((tm, tn), lambda j, i: (i, j)),
        out_shape=jax.ShapeDtypeStruct((m, n), out_dtype),
        compiler_params=pltpu.CompilerParams(dimension_semantics=("arbitrary", "arbitrary"), vmem_limit_bytes=VMEM_LIMIT),
        name="dense_mm",
    )(a, b)


def _moe_kernel(tile_ref, exp_ref, lo_ref, hi_ref, first_ref, newexp_ref, x_ref, sw_ref, w1_ref, b1_ref, w2_ref, b2_ref, o_ref, w1b_ref, w2b_ref):
    del exp_ref
    w = pl.program_id(0)
    lo, hi = lo_ref[w], hi_ref[w]

    @pl.when(newexp_ref[w] == 1)
    def _():
        def cast_rows(i, carry):
            rows = pl.ds(pl.multiple_of(i * LANES, LANES), LANES)
            w1b_ref[rows, :] = w1_ref[rows, :].astype(BF16)
            w2b_ref[rows, :] = w2_ref[rows, :].astype(BF16)
            return carry

        lax.fori_loop(0, D_MODEL // LANES, cast_rows, 0)

    @pl.when(hi > lo)
    def _():
        hh = jnp.dot(x_ref[...], w1b_ref[...], preferred_element_type=F32) + b1_ref[...]
        g = jnp.minimum(hh[:, :D_FF], SWIGLU_LIMIT)
        u = jnp.clip(hh[:, D_FF:], -SWIGLU_LIMIT, SWIGLU_LIMIT)
        act = (u + 1.0) * g * jax.nn.sigmoid(SWIGLU_ALPHA * g)
        y = jnp.dot(act.astype(BF16), w2b_ref[...], preferred_element_type=F32) + b2_ref[...]
        y = (y * sw_ref[...]).astype(o_ref.dtype)
        rows = tile_ref[w] * MOE_BLOCK + lax.broadcasted_iota(jnp.int32, (MOE_BLOCK, 1), 0)
        mine = jnp.logical_and(rows >= lo, rows < hi)

        @pl.when(first_ref[w] == 1)
        def _():
            o_ref[...] = jnp.where(mine, y, jnp.zeros_like(y))

        @pl.when(first_ref[w] != 1)
        def _():
            o_ref[...] = jnp.where(mine, y, o_ref[...])


def _moe_experts(xs, sw, tile_w, exp_w, lo, hi, first, newexp, li, w1, b1, w2, b2):
    n, d = xs.shape
    grid_spec = pltpu.PrefetchScalarGridSpec(
        num_scalar_prefetch=6,
        grid=(tile_w.shape[0],),
        in_specs=[
            pl.BlockSpec((MOE_BLOCK, d), lambda w, tl, ex, lo_, hi_, fi, ne: (tl[w], 0)),
            pl.BlockSpec((MOE_BLOCK, 1), lambda w, tl, ex, lo_, hi_, fi, ne: (tl[w], 0)),
            pl.BlockSpec((None, None, d, 2 * D_FF), lambda w, tl, ex, lo_, hi_, fi, ne: (li, ex[w], 0, 0)),
            pl.BlockSpec((None, 1, 2 * D_FF), lambda w, tl, ex, lo_, hi_, fi, ne: (ex[w], 0, 0)),
            pl.BlockSpec((None, None, D_FF, d), lambda w, tl, ex, lo_, hi_, fi, ne: (li, ex[w], 0, 0)),
            pl.BlockSpec((None, 1, d), lambda w, tl, ex, lo_, hi_, fi, ne: (ex[w], 0, 0)),
        ],
        out_specs=pl.BlockSpec((MOE_BLOCK, d), lambda w, tl, ex, lo_, hi_, fi, ne: (tl[w], 0)),
        scratch_shapes=[pltpu.VMEM((d, 2 * D_FF), BF16), pltpu.VMEM((D_FF, d), BF16)],
    )
    return pl.pallas_call(
        _moe_kernel,
        grid_spec=grid_spec,
        out_shape=jax.ShapeDtypeStruct((n, d), BF16),
        compiler_params=pltpu.CompilerParams(dimension_semantics=("arbitrary",), vmem_limit_bytes=VMEM_LIMIT),
        name="moe_experts",
    )(tile_w, exp_w, lo, hi, first, newexp, xs, sw.reshape(n, 1), w1, b1.reshape(N_EXPERTS, 1, 2 * D_FF), w2, b2.reshape(N_EXPERTS, 1, d))


SUBLANES = 8
NEG_BIG = -1e30
HIER_LEVELS = (64, 32, 16, 8, 4)
LOG2E = math.log2(math.e)


def _split3(x):
    h1 = x.astype(BF16)
    r1 = x - h1.astype(F32)
    h2 = r1.astype(BF16)
    h3 = (r1 - h2.astype(F32)).astype(BF16)
    return h1, h2, h3


def _dot_nt(a, b):
    return lax.dot_general(a, b, (((1,), (1,)), ((), ())), preferred_element_type=F32)


def _gla_kernel(q_ref, a_ref, v_ref, lb_ref, *rest, reverse):
    if reverse:
        of_ref, w_ref, o_ref, st_ref, at_ref = rest
    else:
        o_ref, st_ref, at_ref = rest
    Q = HG_CHUNK

    @pl.when(pl.program_id(1) == 0)
    def _():
        st_ref[...] = jnp.zeros_like(st_ref)

    row = lax.broadcasted_iota(jnp.int32, (Q, Q), 0)
    col = lax.broadcasted_iota(jnp.int32, (Q, Q), 1)
    tri = jnp.where((col >= row) if reverse else (col <= row), 1.0, 0.0).astype(BF16)
    same_block = {s: (row // s) == (col // s) for s in HIER_LEVELS[1:] + (2, 1)}
    rowk = lax.broadcasted_iota(jnp.int32, (Q, HG_KDIM), 0)
    q_rows = {s: ((rowk % s) < s // 2) if reverse else ((rowk % s) >= s // 2) for s in HIER_LEVELS + (2,)}
    shp3 = (Q // SUBLANES, SUBLANES, HG_KDIM)
    sub3 = lax.broadcasted_iota(jnp.int32, shp3, 1)

    lb = lb_ref[...]
    a = a_ref[...].astype(F32)
    f_all = jnp.maximum(lb + (1.0 - lb) * jax.nn.sigmoid(a), F_FLOOR)
    kk = (1.0 - lb) * jax.nn.sigmoid(-a)
    g_all = sum(jnp.dot(tri, p, preferred_element_type=F32) for p in _split3(jnp.log(f_all) * LOG2E))
    q_all = q_ref[...].astype(F32)
    q_all = q_all * jax.nn.sigmoid(q_all)
    v_all = v_ref[...]
    tot = 0 if reverse else Q - 1

    def level_ref(g, s):
        half = s // 2
        m_off = half if reverse else half - 1
        if s >= 2 * SUBLANES:
            return jnp.concatenate([jnp.broadcast_to(g[b0 + m_off:b0 + m_off + 1, :], (s, HG_KDIM)) for b0 in range(0, Q, s)], axis=0)
        g3 = g.reshape(shp3)
        ref = jnp.broadcast_to(g3[:, m_off:m_off + 1, :], shp3)
        for b0 in range(s, SUBLANES, s):
            ref = jnp.where(sub3 >= b0, jnp.broadcast_to(g3[:, b0 + m_off:b0 + m_off + 1, :], shp3), ref)
        return ref.reshape(Q, HG_KDIM)

    for h in range(HG_HEADS):
        sl = slice(h * HG_KDIM, (h + 1) * HG_KDIM)
        g, qh, kh, fh = g_all[:, sl], q_all[:, sl], kk[:, sl], f_all[:, sl]
        attn = None
        for s in HIER_LEVELS:
            gref = level_ref(g, s)
            eq = jnp.exp2(jnp.where(q_rows[s], g - gref, NEG_BIG))
            ek = jnp.exp2(jnp.where(q_rows[s], NEG_BIG, gref - g))
            lvl = _dot_nt((qh * eq).astype(BF16), (kh * ek).astype(BF16))
            attn = lvl if attn is None else jnp.where(same_block[s], lvl, attn)
        lvl = _dot_nt(jnp.where(q_rows[2], qh * fh, 0.0).astype(BF16), jnp.where(q_rows[2], 0.0, kh).astype(BF16))
        attn = jnp.where(same_block[2], lvl, attn)
        attn = jnp.where(same_block[1], _dot_nt(qh.astype(BF16), kh.astype(BF16)), attn)
        at_ref[h] = attn.astype(BF16)

    for h in range(HG_HEADS):
        sl = slice(h * HG_KDIM, (h + 1) * HG_KDIM)
        g, qh, kh, vb = g_all[:, sl], q_all[:, sl], kk[:, sl], v_all[:, sl]
        g_tot = g[tot:tot + 1, :]
        st = st_ref[h]
        o = _dot_nt((qh * jnp.exp2(g)).astype(BF16), st.astype(BF16))
        o = o + jnp.dot(at_ref[h], vb, preferred_element_type=F32)
        k_st = (kh * jnp.exp2(g_tot - g)).astype(BF16)
        st_ref[h] = st * jnp.exp2(g_tot) + jnp.dot(vb.astype(F32).T.astype(BF16), k_st, preferred_element_type=F32)
        if reverse:
            o = o + of_ref[:, sl]
            o = o * lax.rsqrt(jnp.mean(o * o, axis=-1, keepdims=True) + EPS) * w_ref[:, sl]
        o_ref[:, sl] = o.astype(o_ref.dtype)


def _hgrn_scan(p_cm, lb, norm_w, n_ctx):
    b, t, _ = p_cm.shape
    col_q, col_f, col_i = 0, 1, 3
    nc, ncc = t // HG_CHUNK, n_ctx // HG_CHUNK
    blk = (None, HG_CHUNK, HG_QK)
    scratch = [pltpu.VMEM((HG_HEADS, HG_VDIM, HG_KDIM), F32), pltpu.VMEM((HG_HEADS, HG_CHUNK, HG_CHUNK), BF16)]
    params = pltpu.CompilerParams(dimension_semantics=("arbitrary", "arbitrary"), vmem_limit_bytes=VMEM_LIMIT)
    row_spec = pl.BlockSpec((1, HG_QK), lambda bi, s: (0, 0))

    def fwd_chunk(s):
        return s

    def bwd_chunk(s):
        return jnp.where(s < ncc, ncc - 1 - s, nc + ncc - 1 - s)

    o_f = pl.pallas_call(
        functools.partial(_gla_kernel, reverse=False),
        grid=(b, nc),
        in_specs=[
            pl.BlockSpec(blk, lambda bi, s: (bi, fwd_chunk(s), col_q)),
            pl.BlockSpec(blk, lambda bi, s: (bi, fwd_chunk(s), col_f)),
            pl.BlockSpec(blk, lambda bi, s: (bi, fwd_chunk(s), col_i)),
            row_spec,
        ],
        out_specs=pl.BlockSpec(blk, lambda bi, s: (bi, fwd_chunk(s), 0)),
        out_shape=jax.ShapeDtypeStruct((b, t, HG_V), F32),
        scratch_shapes=scratch,
        compiler_params=params,
        name="gla_fwd",
    )(p_cm, p_cm, p_cm, lb[0:1])
    return pl.pallas_call(
        functools.partial(_gla_kernel, reverse=True),
        grid=(b, nc),
        in_specs=[
            pl.BlockSpec(blk, lambda bi, s: (bi, bwd_chunk(s), col_q)),
            pl.BlockSpec(blk, lambda bi, s: (bi, bwd_chunk(s), col_f + 1)),
            pl.BlockSpec(blk, lambda bi, s: (bi, bwd_chunk(s), col_i)),
            row_spec,
            pl.BlockSpec(blk, lambda bi, s: (bi, bwd_chunk(s), 0)),
            row_spec,
        ],
        out_specs=pl.BlockSpec(blk, lambda bi, s: (bi, bwd_chunk(s), 0)),
        out_shape=jax.ShapeDtypeStruct((b, t, HG_V), BF16),
        scratch_shapes=scratch,
        compiler_params=params,
        name="gla_bwd",
    )(p_cm, p_cm, p_cm, lb[1:2], o_f, norm_w.reshape(1, HG_V))


CONV_TILE = 256
HALO = 16


def _dwconv_kernel(prev_ref, cur_ref, next_ref, w_ref, b_ref, *o_refs, taps, n_ctx_tiles, n_tiles, silu):
    i = pl.program_id(2)
    first = jnp.logical_or(i == 0, i == n_ctx_tiles)
    last = jnp.logical_or(i == n_ctx_tiles - 1, i == n_tiles - 1)
    pad = taps // 2
    xp = jnp.where(first, 0.0, prev_ref[...].astype(F32))
    xn = jnp.where(last, 0.0, next_ref[...].astype(F32))
    xcat = jnp.concatenate([xp, cur_ref[...].astype(F32), xn], axis=0)
    acc = jnp.broadcast_to(b_ref[...], cur_ref.shape).astype(F32)
    for k in range(taps):
        off = HALO - pad + k
        acc = acc + w_ref[k:k + 1, :] * xcat[off:off + CONV_TILE, :]
    if silu:
        acc = acc * jax.nn.sigmoid(acc)
    if len(o_refs) == 1:
        o_refs[0][...] = acc.astype(o_refs[0].dtype)
    else:
        ctx_ref, lat_ref = o_refs

        @pl.when(i < n_ctx_tiles)
        def _():
            ctx_ref[...] = acc.astype(ctx_ref.dtype)

        @pl.when(i >= n_ctx_tiles)
        def _():
            lat_ref[...] = acc.astype(lat_ref.dtype)


def _dwconv_stream(x, w, bias, n_ctx, silu, c0=0, ct=1024, split=False):
    b, t, _ = x.shape
    taps, c = w.shape
    n_tiles = t // CONV_TILE
    nct = n_ctx // CONV_TILE
    hb = CONV_TILE // HALO
    n_halo = t // HALO
    j0 = c0 // ct
    kern = functools.partial(_dwconv_kernel, taps=taps, n_ctx_tiles=nct, n_tiles=n_tiles, silu=silu)
    if split:
        out_specs = [pl.BlockSpec((None, CONV_TILE, ct), lambda bi, j, i: (bi, jnp.minimum(i, nct - 1), j)),
                     pl.BlockSpec((None, CONV_TILE, ct), lambda bi, j, i: (bi, jnp.maximum(i - nct, 0), j))]
        out_shape = [jax.ShapeDtypeStruct((b, n_ctx, c), BF16), jax.ShapeDtypeStruct((b, t - n_ctx, c), BF16)]
    else:
        out_specs = pl.BlockSpec((None, CONV_TILE, ct), lambda bi, j, i: (bi, i, j))
        out_shape = jax.ShapeDtypeStruct((b, t, c), BF16)
    return pl.pallas_call(
        kern,
        grid=(b, c // ct, n_tiles),
        in_specs=[
            pl.BlockSpec((None, HALO, ct), lambda bi, j, i: (bi, jnp.maximum(i * hb - 1, 0), j0 + j)),
            pl.BlockSpec((None, CONV_TILE, ct), lambda bi, j, i: (bi, i, j0 + j)),
            pl.BlockSpec((None, HALO, ct), lambda bi, j, i: (bi, jnp.minimum((i + 1) * hb, n_halo - 1), j0 + j)),
            pl.BlockSpec((taps, ct), lambda bi, j, i: (0, j)),
            pl.BlockSpec((1, ct), lambda bi, j, i: (0, j)),
        ],
        out_specs=out_specs,
        out_shape=out_shape,
        compiler_params=pltpu.CompilerParams(dimension_semantics=("arbitrary",) * 3, vmem_limit_bytes=VMEM_LIMIT),
        name="dwconv",
    )(x, x, x, w.astype(F32), bias.reshape(1, c).astype(F32))


SSM_GHEADS = SSM_HEADS // SSM_GROUPS
SSM_GP = SSM_GHEADS * SSM_HEAD_DIM


def _ssd_kernel(xbc_ref, dt_ref, dtb_ref, a_ref, *rest, reverse):
    if reverse:
        yf_ref, z_ref, dsk_ref, nw_ref, o_ref, st_ref, m_ref, xd_ref, xst_ref, y_ref = rest
    else:
        o_ref, st_ref, m_ref, xd_ref, xst_ref, y_ref = rest
    Q = SSD_CHUNK

    @pl.when(pl.program_id(1) == 0)
    def _():
        st_ref[...] = jnp.zeros_like(st_ref)

    row = lax.broadcasted_iota(jnp.int32, (Q, Q), 0)
    col = lax.broadcasted_iota(jnp.int32, (Q, Q), 1)
    keep = (col >= row) if reverse else (col <= row)
    tri = jnp.where(keep, 1.0, 0.0).astype(BF16)
    expand = jnp.where(lax.broadcasted_iota(jnp.int32, (LANES, SSM_INNER), 1) // SSM_HEAD_DIM == lax.broadcasted_iota(jnp.int32, (LANES, SSM_INNER), 0), 1.0, 0.0).astype(BF16)

    dt = jax.nn.softplus(dt_ref[...] + dtb_ref[...])
    a = dt * a_ref[...]
    cs = sum(jnp.dot(tri, p, preferred_element_type=F32) for p in _split3(a))
    tot = 0 if reverse else Q - 1
    cs_tot = cs[tot:tot + 1, :]
    cs_t = cs.T
    dt_e = jnp.dot(dt.astype(BF16), expand, preferred_element_type=F32)
    e_in = jnp.dot(jnp.exp(cs).astype(BF16), expand, preferred_element_type=F32)
    e_st = jnp.dot(jnp.exp(cs_tot - cs).astype(BF16), expand, preferred_element_type=F32)
    e_tot = jnp.dot(jnp.broadcast_to(jnp.exp(cs_tot), (SUBLANES, LANES)).astype(BF16), expand, preferred_element_type=F32)[0:1, :]

    xs = xbc_ref[:, :SSM_INNER].astype(F32)
    xd_ref[...] = (xs * dt_e).astype(BF16)
    xst_ref[...] = (xs * dt_e * e_st).astype(BF16)
    for g in range(SSM_GROUPS):
        bm = xbc_ref[:, SSM_INNER + g * SSM_STATE:SSM_INNER + (g + 1) * SSM_STATE]
        cm = xbc_ref[:, SSM_INNER + (SSM_GROUPS + g) * SSM_STATE:SSM_INNER + (SSM_GROUPS + g + 1) * SSM_STATE]
        cb = _dot_nt(cm, bm)
        for h in range(g * SSM_GHEADS, (g + 1) * SSM_GHEADS):
            diff = jnp.broadcast_to(cs[:, h:h + 1], (Q, Q)) - jnp.broadcast_to(cs_t[h:h + 1, :], (Q, Q))
            m_ref[h] = (cb * jnp.exp(jnp.where(keep, diff, NEG_BIG))).astype(BF16)

    lane_lo = lax.broadcasted_iota(jnp.int32, (Q, LANES), 1) < SSM_HEAD_DIM
    for g in range(SSM_GROUPS):
        bm = xbc_ref[:, SSM_INNER + g * SSM_STATE:SSM_INNER + (g + 1) * SSM_STATE]
        cm = xbc_ref[:, SSM_INNER + (SSM_GROUPS + g) * SSM_STATE:SSM_INNER + (SSM_GROUPS + g + 1) * SSM_STATE]
        gl = slice(g * SSM_GP, (g + 1) * SSM_GP)
        st = st_ref[g]
        y_off = jnp.dot(cm, st.astype(BF16), preferred_element_type=F32) * e_in[:, gl]
        for hp in range(SSM_GHEADS // 2):
            h0 = g * SSM_GHEADS + 2 * hp
            lanes = slice((h0 // 2) * LANES, (h0 // 2 + 1) * LANES)
            pair = [jnp.dot(m_ref[h], xd_ref[:, lanes], preferred_element_type=F32) for h in (h0, h0 + 1)]
            y_ref[:, lanes] = jnp.where(lane_lo, pair[0], pair[1]) + y_off[:, hp * LANES:(hp + 1) * LANES]
        st_ref[g] = st * e_tot[:, gl] + jnp.dot(bm.astype(F32).T.astype(BF16), xst_ref[:, gl], preferred_element_type=F32)
    y = y_ref[...]
    if reverse:
        y = (y + yf_ref[...] + xbc_ref[:, :SSM_INNER].astype(F32) * dsk_ref[...])
        zz = z_ref[...].astype(F32)
        y = y * (zz * jax.nn.sigmoid(zz))
        y = y * lax.rsqrt(jnp.mean(y * y, axis=-1, keepdims=True) + EPS) * nw_ref[...]
    o_ref[...] = y.astype(o_ref.dtype)


def _ssd_scan(xbc_act, dt2, p_rm, z_col, dt_bias, a_log, d_skip, norm_w, n_ctx):
    b, t, _ = xbc_act.shape
    nc, ncc = t // SSD_CHUNK, n_ctx // SSD_CHUNK

    def pad_heads(v):
        return jnp.pad(v.astype(F32), ((0, 0), (0, LANES - SSM_HEADS)))

    dtb = pad_heads(dt_bias.reshape(2, SSM_HEADS))
    a_neg = pad_heads(-jnp.exp(a_log.astype(F32)))
    dsk = jnp.repeat(d_skip.astype(F32), SSM_HEAD_DIM).reshape(1, SSM_INNER)
    scratch = [pltpu.VMEM((SSM_GROUPS, SSM_STATE, SSM_GP), F32), pltpu.VMEM((SSM_HEADS, SSD_CHUNK, SSD_CHUNK), BF16),
               pltpu.VMEM((SSD_CHUNK, SSM_INNER), BF16), pltpu.VMEM((SSD_CHUNK, SSM_INNER), BF16), pltpu.VMEM((SSD_CHUNK, SSM_INNER), F32)]
    params = pltpu.CompilerParams(dimension_semantics=("arbitrary", "arbitrary"), vmem_limit_bytes=VMEM_LIMIT)

    def bwd_chunk(s):
        return jnp.where(s < ncc, ncc - 1 - s, nc + ncc - 1 - s)

    def specs(chunk, d):
        return [
            pl.BlockSpec((None, SSD_CHUNK, SSM_XBC), lambda bi, s: (bi, chunk(s), 0)),
            pl.BlockSpec((None, SSD_CHUNK, LANES), lambda bi, s: (bi, chunk(s), d)),
            pl.BlockSpec((1, LANES), lambda bi, s: (0, 0)),
            pl.BlockSpec((1, LANES), lambda bi, s: (0, 0)),
        ]

    def inner_spec(chunk):
        return pl.BlockSpec((None, SSD_CHUNK, SSM_INNER), lambda bi, s: (bi, chunk(s), 0))

    row_spec = pl.BlockSpec((1, SSM_INNER), lambda bi, s: (0, 0))
    z_spec = pl.BlockSpec((None, SSD_CHUNK, SSM_INNER), lambda bi, s: (bi, bwd_chunk(s), z_col // SSM_INNER))
    y_f = pl.pallas_call(
        functools.partial(_ssd_kernel, reverse=False),
        grid=(b, nc),
        in_specs=specs(lambda s: s, 0),
        out_specs=inner_spec(lambda s: s),
        out_shape=jax.ShapeDtypeStruct((b, t, SSM_INNER), F32),
        scratch_shapes=scratch,
        compiler_params=params,
        name="ssd_fwd",
    )(xbc_act, dt2, dtb[0:1], a_neg[0:1])
    return pl.pallas_call(
        functools.partial(_ssd_kernel, reverse=True),
        grid=(b, nc),
        in_specs=specs(bwd_chunk, 1) + [inner_spec(bwd_chunk), z_spec, row_spec, row_spec],
        out_specs=inner_spec(bwd_chunk),
        out_shape=jax.ShapeDtypeStruct((b, t, SSM_INNER), BF16),
        scratch_shapes=scratch,
        compiler_params=params,
        name="ssd_bwd",
    )(xbc_act, dt2, dtb[1:2], a_neg[1:2], y_f, p_rm, dsk, norm_w.reshape(1, SSM_INNER).astype(F32))


HY_N2 = LANES
HY_CT = LANES
VMEM_LIMIT_HYENA = 60 * 1024 * 1024


def _hy_dims(L):
    n1 = 2 * L // HY_N2
    k1n = n1 // 2 + 1
    k1p = -(-k1n // SUBLANES) * SUBLANES
    return n1, k1n, k1p


def _hy_tables(L, n1_rows):
    n1, k1n, k1p = _hy_dims(L)
    n = 2 * L
    k1 = np.arange(k1n, dtype=np.float64)[None, :, None]
    nn = (HY_N2 * np.arange(n1_rows, dtype=np.float64)[None, None, :] + np.arange(HY_N2, dtype=np.float64)[:, None, None])
    ang = 2.0 * np.pi * ((k1 * nn) % n) / n
    m1 = np.zeros((HY_N2, 2 * k1p, n1_rows), np.float32)
    m1[:, :k1n] = np.cos(ang)
    m1[:, k1p:k1p + k1n] = -np.sin(ang)
    m4 = np.transpose(m1, (0, 2, 1))
    kk = np.arange(HY_N2, dtype=np.float64)
    a2 = 2.0 * np.pi * ((kk[:, None] * kk[None, :]) % HY_N2) / HY_N2
    c, s = np.cos(a2), np.sin(a2)
    f3 = np.block([[c, s], [-s, c]]).astype(np.float32)
    f3i = np.block([[c, -s], [s, c]]).astype(np.float32)
    return jnp.asarray(m1, BF16), jnp.asarray(m4, BF16), jnp.asarray(f3, BF16), jnp.asarray(f3i, BF16)


def _hy_stage1(u_ref, a_ref, m1_ref, n1_rows, k1p):
    def body(n2, carry):
        xs = u_ref[pl.ds(n2, n1_rows, stride=HY_N2), :].astype(BF16)
        a_ref[pl.ds(pl.multiple_of(n2 * 2 * k1p, 2 * k1p), 2 * k1p), :] = jnp.dot(m1_ref[n2], xs, preferred_element_type=F32)
        return carry

    lax.fori_loop(0, HY_N2, body, 0, unroll=8)


def _hy_spectrum_slab(a_ref, f3_ref, k1s, k1p):
    blk = jnp.concatenate([jnp.concatenate([a_ref[pl.ds(k1, HY_N2, stride=2 * k1p), :], a_ref[pl.ds(k1p + k1, HY_N2, stride=2 * k1p), :]], axis=0) for k1 in k1s], axis=1)
    return jnp.dot(f3_ref[...], blk.astype(BF16), preferred_element_type=F32)


def _hy_k1_loop(body, k1n):
    def pair(p, carry):
        body((2 * p, 2 * p + 1))
        return carry

    lax.fori_loop(0, (k1n - 1) // 2, pair, 0, unroll=math.gcd((k1n - 1) // 2, 4))
    body((k1n - 1,))


def _hy_slab_rows(k1):
    start = k1 * 2 * HY_N2
    return pl.ds(start if isinstance(k1, int) else pl.multiple_of(start, 2 * HY_N2), 2 * HY_N2)


def _hy_conv(u_ref, yo_ref, a_ref, y_ref, h_ref, order, m1_ref, m4_ref, f3_ref, f3i_ref, n1_rows, k1n, k1p):
    _hy_stage1(u_ref, a_ref, m1_ref, n1_rows, k1p)

    def stage2(k1s):
        x = _hy_spectrum_slab(a_ref, f3_ref, k1s, k1p)
        h = jnp.concatenate([h_ref[order, _hy_slab_rows(k1), :] for k1 in k1s], axis=1).astype(F32)
        xr, xi, hr, hi = x[:HY_N2], x[HY_N2:], h[:HY_N2], h[HY_N2:]
        z = jnp.concatenate([xr * hr - xi * hi, xr * hi + xi * hr], axis=0).astype(BF16)
        c = jnp.dot(f3i_ref[...], z, preferred_element_type=F32)
        for i, k1 in enumerate(k1s):
            lanes = slice(i * HY_CT, (i + 1) * HY_CT)
            y_ref[pl.ds(k1, HY_N2, stride=2 * k1p), :] = c[:HY_N2, lanes]
            y_ref[pl.ds(k1p + k1, HY_N2, stride=2 * k1p), :] = c[HY_N2:, lanes]

    _hy_k1_loop(stage2, k1n)

    def stage3(n2, carry):
        d = y_ref[pl.ds(pl.multiple_of(n2 * 2 * k1p, 2 * k1p), 2 * k1p), :].astype(BF16)
        yo_ref[pl.ds(n2, n1_rows, stride=HY_N2), :] = jnp.dot(m4_ref[n2], d, preferred_element_type=F32)
        return carry

    lax.fori_loop(0, HY_N2, stage3, 0, unroll=8)


def _hyena_kernel(v_ref, x1_ref, x2_ref, h_ref, m1_ref, m4_ref, f3_ref, f3i_ref, bias_ref, o_ref, a_ref, y_ref, u_ref, yo_ref, *, n1_rows, k1n, k1p):
    @pl.when(jnp.logical_and(pl.program_id(0) == 0, pl.program_id(1) == 0))
    def _():
        y_ref[...] = jnp.zeros_like(y_ref)

    u_ref[...] = v_ref[...].astype(F32)
    for order, gate_ref in enumerate((x1_ref, x2_ref)):
        _hy_conv(u_ref, yo_ref, a_ref, y_ref, h_ref, order, m1_ref, m4_ref, f3_ref, f3i_ref, n1_rows, k1n, k1p)
        z = gate_ref[...].astype(F32) * (yo_ref[...] + u_ref[...] * bias_ref[order:order + 1, :])
        if order == 0:
            u_ref[...] = z
        else:
            o_ref[...] = z.astype(o_ref.dtype)


def _hy_filter_kernel(hid_ref, wf_ref, wb_ref, bf_ref, bb_ref, dl_ref, m1_ref, f3_ref, o_ref, a_ref, sf_ref, uf_ref, ub_ref, *, n1_rows, k1n, k1p, scale_mid, scale_edge):
    slab = 2 * HY_N2
    L = uf_ref.shape[0]
    hid_rows = math.gcd(L, 2 * HY_N2)

    def fill(w_ref, b_ref, dst_ref, drop_first):
        w = _split3(w_ref[...])

        def body(i, energy):
            rows = pl.ds(pl.multiple_of(i * hid_rows, hid_rows), hid_rows)
            hid = _split3(hid_ref[rows, :])
            acc = sum(jnp.dot(hid[p], w[q], preferred_element_type=F32) for p in range(2) for q in range(2 - p))
            pos = i * hid_rows + lax.broadcasted_iota(jnp.int32, (hid_rows, HY_CT), 0)
            vals = (acc + b_ref[...]) * jnp.exp(pos.astype(F32) * (-1.0 / (L - 1)) * dl_ref[...])
            if drop_first:
                vals = jnp.where(pos == 0, 0.0, vals)
            dst_ref[rows, :] = vals
            return energy + jnp.sum(vals * vals, axis=0, keepdims=True)

        return lax.fori_loop(0, L // hid_rows, body, jnp.zeros((1, HY_CT), F32), unroll=math.gcd(L // hid_rows, 4))

    norm = lax.rsqrt(fill(wf_ref, bf_ref, uf_ref, False) + fill(wb_ref, bb_ref, ub_ref, True) + EPS)

    _hy_stage1(uf_ref, a_ref, m1_ref, n1_rows, k1p)

    def keep_fwd(k1s):
        x = _hy_spectrum_slab(a_ref, f3_ref, k1s, k1p)
        for i, k1 in enumerate(k1s):
            sf_ref[_hy_slab_rows(k1), :] = x[:, i * HY_CT:(i + 1) * HY_CT]

    _hy_k1_loop(keep_fwd, k1n)
    _hy_stage1(ub_ref, a_ref, m1_ref, n1_rows, k1p)

    def combine(k1s):
        xb_all = _hy_spectrum_slab(a_ref, f3_ref, k1s, k1p)
        for i, k1 in enumerate(k1s):
            xb = xb_all[:, i * HY_CT:(i + 1) * HY_CT]
            xf = sf_ref[_hy_slab_rows(k1), :]
            w = norm * jnp.where(jnp.logical_or(k1 == 0, k1 == k1n - 1), scale_edge, scale_mid)
            h = jnp.concatenate([xf[:HY_N2] + xb[:HY_N2], xf[HY_N2:] - xb[HY_N2:]], axis=0)
            o_ref[_hy_slab_rows(k1), :] = (h * w).astype(o_ref.dtype)

    _hy_k1_loop(combine, k1n)


def _single(block_shape, index_map):
    return pl.BlockSpec(block_shape, index_map, pipeline_mode=pl.Buffered(1))


def _hyena_filter_spectrum_pallas(hidden, w3, b3, deltas):
    L, ffn = hidden.shape
    c = deltas.shape[0]
    nct = c // HY_CT
    n = 2 * L
    n1, k1n, k1p = _hy_dims(L)
    n1_rows = L // HY_N2
    m1, _, f3, _ = _hy_tables(L, n1_rows)
    kern = functools.partial(_hy_filter_kernel, n1_rows=n1_rows, k1n=k1n, k1p=k1p, scale_mid=2.0 / n, scale_edge=1.0 / n)
    return pl.pallas_call(
        kern,
        grid=(HY_ORDER, nct),
        in_specs=[
            _single((L, ffn), lambda o, j: (0, 0)),
            pl.BlockSpec((ffn, HY_CT), lambda o, j: (0, o * nct + j)),
            pl.BlockSpec((ffn, HY_CT), lambda o, j: (0, (HY_ORDER + o) * nct + j)),
            pl.BlockSpec((1, HY_CT), lambda o, j: (0, o * nct + j)),
            pl.BlockSpec((1, HY_CT), lambda o, j: (0, (HY_ORDER + o) * nct + j)),
            pl.BlockSpec((1, HY_CT), lambda o, j: (0, j)),
            _single((HY_N2, 2 * k1p, n1_rows), lambda o, j: (0, 0, 0)),
            _single((2 * HY_N2, 2 * HY_N2), lambda o, j: (0, 0)),
        ],
        out_specs=pl.BlockSpec((None, k1n * 2 * HY_N2, HY_CT), lambda o, j: (o, 0, j)),
        out_shape=jax.ShapeDtypeStruct((HY_ORDER, k1n * 2 * HY_N2, c), BF16),
        scratch_shapes=[
            pltpu.VMEM((k1p * 2 * HY_N2, HY_CT), F32),
            pltpu.VMEM((k1n * 2 * HY_N2, HY_CT), F32),
            pltpu.VMEM((L, HY_CT), F32),
            pltpu.VMEM((L, HY_CT), F32),
        ],
        compiler_params=pltpu.CompilerParams(dimension_semantics=("arbitrary", "arbitrary"), vmem_limit_bytes=VMEM_LIMIT_HYENA),
        name="hyena_filter_dft",
    )(hidden, w3, w3, b3.reshape(1, -1), b3.reshape(1, -1), deltas.reshape(1, c), m1, f3)


def _hyena_long(hy, h_spec, bias):
    b, L, c3 = hy.shape
    c = c3 // (HY_ORDER + 1)
    nct = c // HY_CT
    n1, k1n, k1p = _hy_dims(L)
    n1_rows = L // HY_N2
    m1, m4, f3, f3i = _hy_tables(L, n1_rows)
    kern = functools.partial(_hyena_kernel, n1_rows=n1_rows, k1n=k1n, k1p=k1p)

    def col(part):
        return _single((None, L, HY_CT), lambda j, bi: (bi, 0, part * nct + j))

    return pl.pallas_call(
        kern,
        grid=(nct, b),
        in_specs=[
            col(0), col(1), col(2),
            _single((HY_ORDER, k1n * 2 * HY_N2, HY_CT), lambda j, bi: (0, 0, j)),
            _single((HY_N2, 2 * k1p, n1_rows), lambda j, bi: (0, 0, 0)),
            _single((HY_N2, n1_rows, 2 * k1p), lambda j, bi: (0, 0, 0)),
            _single((2 * HY_N2, 2 * HY_N2), lambda j, bi: (0, 0)),
            _single((2 * HY_N2, 2 * HY_N2), lambda j, bi: (0, 0)),
            pl.BlockSpec((HY_ORDER, HY_CT), lambda j, bi: (0, j)),
        ],
        out_specs=pl.BlockSpec((None, L, HY_CT), lambda j, bi: (bi, 0, j)),
        out_shape=jax.ShapeDtypeStruct((b, L, c), BF16),
        scratch_shapes=[
            pltpu.VMEM((k1p * 2 * HY_N2, HY_CT), F32),
            pltpu.VMEM((HY_N2 * 2 * k1p, HY_CT), F32),
            pltpu.VMEM((L, HY_CT), F32),
            pltpu.VMEM((L, HY_CT), F32),
        ],
        compiler_params=pltpu.CompilerParams(dimension_semantics=("arbitrary", "arbitrary"), vmem_limit_bytes=VMEM_LIMIT_HYENA),
        name="hyena_long_conv",
    )(hy, hy, hy, h_spec, m1, m4, f3, f3i, bias.astype(F32))


HY_CTX_CT = 256


def _hy_ctx_tables(L):
    n = 2 * L
    kb = L + 1
    kp = -(-kb // LANES) * LANES
    ang = 2.0 * np.pi * ((np.arange(kb, dtype=np.float64)[:, None] * np.arange(L, dtype=np.float64)[None, :]) % n) / n
    fwd = np.zeros((2 * kp, L), np.float32)
    fwd[:kb] = np.cos(ang)
    fwd[kp:kp + kb] = -np.sin(ang)
    return jnp.asarray(fwd, BF16), jnp.asarray(fwd.T, BF16), kb, kp


def _hy_ctx_filter_kernel(fwd_ref, bwd_ref, f_ref, o_ref, *, kb, kp, n):
    fwd = fwd_ref[...]
    bwd = jnp.where(lax.broadcasted_iota(jnp.int32, fwd.shape, 0) == 0, 0.0, bwd_ref[...])
    norm = lax.rsqrt(jnp.sum(fwd * fwd, axis=0, keepdims=True) + jnp.sum(bwd * bwd, axis=0, keepdims=True) + EPS)
    hf = jnp.dot(f_ref[...], fwd.astype(BF16), preferred_element_type=F32)
    hb = jnp.dot(f_ref[...], bwd.astype(BF16), preferred_element_type=F32)
    row = lax.broadcasted_iota(jnp.int32, hf.shape, 0)
    imag = row >= kp
    k = jnp.where(imag, row - kp, row)
    wk = jnp.where(jnp.logical_or(k == 0, k == kb - 1), 1.0 / n, 2.0 / n)
    o_ref[...] = (hf + jnp.where(imag, -hb, hb)) * (wk * norm)


def _hy_ctx_kernel(v_ref, x1_ref, x2_ref, h_ref, f_ref, g_ref, bias_ref, o_ref, *, kp):
    u = v_ref[...].astype(F32)
    for order, gate_ref in enumerate((x1_ref, x2_ref)):
        x = jnp.dot(f_ref[...], u.astype(BF16), preferred_element_type=F32)
        h = h_ref[order]
        xr, xi, hr, hi = x[:kp], x[kp:], h[:kp], h[kp:]
        z = jnp.concatenate([xr * hr - xi * hi, xr * hi + xi * hr], axis=0).astype(BF16)
        y = jnp.dot(g_ref[...], z, preferred_element_type=F32)
        u = gate_ref[...].astype(F32) * (y + u * bias_ref[order:order + 1, :])
    o_ref[...] = u.astype(o_ref.dtype)


def _hyena_ctx(hy, taps, bias):
    b, L, _ = hy.shape
    c = hy.shape[2] // (HY_ORDER + 1)
    ct = HY_CTX_CT
    nct = c // ct
    f_mat, g_mat, kb, kp = _hy_ctx_tables(L)
    params = pltpu.CompilerParams(dimension_semantics=("arbitrary", "arbitrary"), vmem_limit_bytes=VMEM_LIMIT)
    h_spec = pl.pallas_call(
        functools.partial(_hy_ctx_filter_kernel, kb=kb, kp=kp, n=2 * L),
        grid=(HY_ORDER, nct),
        in_specs=[
            pl.BlockSpec((L, ct), lambda o, j: (0, o * nct + j)),
            pl.BlockSpec((L, ct), lambda o, j: (0, (HY_ORDER + o) * nct + j)),
            pl.BlockSpec((2 * kp, L), lambda o, j: (0, 0)),
        ],
        out_specs=pl.BlockSpec((None, 2 * kp, ct), lambda o, j: (o, 0, j)),
        out_shape=jax.ShapeDtypeStruct((HY_ORDER, 2 * kp, c), F32),
        compiler_params=params,
        name="hyena_ctx_filter",
    )(taps, taps, f_mat)

    def col(part):
        return pl.BlockSpec((None, L, ct), lambda j, bi: (bi, 0, part * nct + j))

    return pl.pallas_call(
        functools.partial(_hy_ctx_kernel, kp=kp),
        grid=(nct, b),
        in_specs=[
            col(0), col(1), col(2),
            pl.BlockSpec((HY_ORDER, 2 * kp, ct), lambda j, bi: (0, 0, j)),
            pl.BlockSpec((2 * kp, L), lambda j, bi: (0, 0)),
            pl.BlockSpec((L, 2 * kp), lambda j, bi: (0, 0)),
            pl.BlockSpec((HY_ORDER, ct), lambda j, bi: (0, j)),
        ],
        out_specs=pl.BlockSpec((None, L, ct), lambda j, bi: (bi, 0, j)),
        out_shape=jax.ShapeDtypeStruct((b, L, c), BF16),
        compiler_params=params,
        name="hyena_ctx_conv",
    )(hy, hy, hy, h_spec, f_mat, g_mat, bias.astype(F32))


def _split_cols(t, sizes):
    return jnp.split(t, np.cumsum(sizes)[:-1].tolist(), axis=-1)


def _rms_norm(x, w):
    xf = x.astype(F32)
    y = xf * lax.rsqrt(jnp.mean(xf * xf, axis=-1, keepdims=True) + EPS)
    return (y * w.astype(F32)).astype(x.dtype)


def _dwconv_centred(x, w, b):
    k = w.shape[0]
    y = lax.conv_general_dilated(x, w[:, None, :].astype(x.dtype), window_strides=(1,), padding=[(k // 2, k // 2)], dimension_numbers=('NWC', 'WIO', 'NWC'), feature_group_count=x.shape[-1], precision=lax.Precision.HIGHEST)
    return y + b.astype(x.dtype)


def _flip_seq(t):
    return jnp.flip(t, axis=1)


def _to_col_major(t, rows):
    b, rest = t.shape[0], t.shape[2:]
    return jnp.swapaxes(t.reshape((b, rows, GRID_W) + rest), 1, 2).reshape((b, rows * GRID_W) + rest)


def _from_col_major(t, rows):
    b, rest = t.shape[0], t.shape[2:]
    return jnp.swapaxes(t.reshape((b, GRID_W, rows) + rest), 1, 2).reshape((b, rows * GRID_W) + rest)


def _hyena_filter_hidden(L, w1, b1, w2, b2, freq):
    hp = lax.Precision.HIGHEST
    t = jnp.linspace(0.0, 1.0, L, dtype=F32)[:, None]
    w = 2.0 * math.pi * jnp.arange(L, dtype=F32)[:, None] / L
    bands = jnp.linspace(1e-4, HY_BANDS - 1, HY_BANDS, dtype=F32)
    feats = jnp.concatenate([t, jnp.cos(bands * w), -jnp.sin(bands * w)], axis=-1)
    h = jnp.sin(freq[0] * (jnp.dot(feats, w1, precision=hp) + b1))
    return jnp.sin(freq[1] * (jnp.dot(h, w2, precision=hp) + b2))


def _hyena_decay_rates():
    max_decay = math.log(HY_DECAY_TARGET) / HY_FAST_DECAY
    min_decay = math.log(HY_DECAY_TARGET) / HY_SLOW_DECAY
    return jnp.abs(jnp.linspace(min_decay, max_decay, HY_WIDTH, dtype=F32))


def _hyena_filter_taps(L, w1, b1, w2, b2, w3, b3, freq):
    h = jnp.dot(_hyena_filter_hidden(L, w1, b1, w2, b2, freq), w3, precision=lax.Precision.HIGHEST) + b3
    t = jnp.linspace(0.0, 1.0, L, dtype=F32)[:, None]
    return h * jnp.tile(jnp.exp(-t * _hyena_decay_rates()), (1, 2 * HY_ORDER))


def _mixer_branches(h_rm, w_in, lb, n_ctx, ssm_conv_w, ssm_conv_b, ssm_dt_bias, ssm_a_log, ssm_d, ssm_norm, hy_conv_w, hy_conv_b, hy_w1, hy_b1, hy_w2, hy_b2, hy_w3, hy_b3, hy_freq, hy_bias, hg_norm):
    b, t, d = h_rm.shape
    n_lat = t - n_ctx
    rows = n_lat // GRID_W
    w_z, w_xbc, w_dt, w_hy, w_q, w_f, w_i, w_g, w_gate = _split_cols(w_in, IN_SIZES)
    h2 = h_rm.reshape(b * t, d)
    rm_parts = (w_z, w_xbc, w_hy, w_g, w_gate)
    col_z, col_xbc, col_hy, col_g, col_gate = np.cumsum([0] + [w.shape[1] for w in rm_parts[:-1]]).tolist()
    p_rm = _mm(h2, jnp.concatenate(rm_parts, axis=1).astype(BF16), BF16).reshape(b, t, -1)

    zero_pad = jnp.zeros((d, LANES - SSM_HEADS), F32)
    w_dt2 = jnp.concatenate([w_dt[:, :SSM_HEADS], zero_pad, w_dt[:, SSM_HEADS:], zero_pad], axis=1)
    dt2 = _mm(h2, w_dt2.astype(BF16), F32).reshape(b, t, 2 * LANES)
    xbc_act = _dwconv_stream(p_rm, ssm_conv_w, ssm_conv_b, n_ctx, True, c0=col_xbc)
    ym = _ssd_scan(xbc_act, dt2, p_rm, col_z, ssm_dt_bias, ssm_a_log, ssm_d, ssm_norm, n_ctx)

    hy_ctx, hy_lat = _dwconv_stream(p_rm, hy_conv_w, hy_conv_b, n_ctx, False, c0=col_hy, split=True)
    taps_ctx = _hyena_filter_taps(n_ctx, hy_w1, hy_b1, hy_w2, hy_b2, hy_w3, hy_b3, hy_freq)
    yh_ctx = _hyena_ctx(hy_ctx, taps_ctx, hy_bias)
    h_spec = _hyena_filter_spectrum_pallas(_hyena_filter_hidden(n_lat, hy_w1, hy_b1, hy_w2, hy_b2, hy_freq), hy_w3, hy_b3, _hyena_decay_rates())
    yh_lat = _hyena_long(hy_lat, h_spec, hy_bias)
    yh = jnp.concatenate([yh_ctx, yh_lat], axis=1)

    h_cm = jnp.concatenate([h_rm[:, :n_ctx], _to_col_major(h_rm[:, n_ctx:], rows)], axis=1).reshape(b * t, d)

    p_cm = _mm(h_cm, jnp.concatenate([w_q, w_f, w_i], axis=1).astype(BF16), BF16).reshape(b, t, -1)
    og = _hgrn_scan(p_cm, lb, hg_norm, n_ctx)
    og = jnp.concatenate([og[:, :n_ctx], _from_col_major(og[:, n_ctx:], rows)], axis=1)
    return ym, yh, og, p_rm, col_g, col_gate


ROW_TILE = 256
MOD_ROWS = SUBLANES
M_SHIFT_MIX, M_SCALE_MIX, M_GATE_MIX, M_SHIFT_FFN, M_SCALE_FFN, M_GATE_FFN = range(6)
ROW_PARAMS = pltpu.CompilerParams(dimension_semantics=("arbitrary", "arbitrary"), vmem_limit_bytes=VMEM_LIMIT)


def _rms(x):
    return x * lax.rsqrt(jnp.mean(x * x, axis=-1, keepdims=True) + EPS)


def _mrow(m_ref, r):
    return m_ref[r:r + 1, :]


def _row_spec(width):
    return pl.BlockSpec((None, ROW_TILE, width), lambda bi, i: (bi, i, 0))


def _vec_spec(width):
    return pl.BlockSpec((1, width), lambda bi, i: (0, 0))


def _mat_spec(k, n):
    return pl.BlockSpec((k, n), lambda bi, i: (0, 0))


def _mod_spec(n_ctx):
    return pl.BlockSpec((None, None, MOD_ROWS, D_MODEL), lambda bi, i: (bi, jnp.where(i < n_ctx // ROW_TILE, 0, 1), 0, 0))


def _norm_mod_kernel(x_ref, w_ref, m_ref, o_ref):
    y = _rms(x_ref[...]) * w_ref[...]
    o_ref[...] = (y * (1.0 + _mrow(m_ref, M_SCALE_MIX)) + _mrow(m_ref, M_SHIFT_MIX)).astype(o_ref.dtype)


def _norm_mod(xs, w, mods, n_ctx):
    b, t, d = xs.shape
    return pl.pallas_call(
        _norm_mod_kernel,
        grid=(b, t // ROW_TILE),
        in_specs=[_row_spec(d), _vec_spec(d), _mod_spec(n_ctx)],
        out_specs=_row_spec(d),
        out_shape=jax.ShapeDtypeStruct((b, t, d), BF16),
        compiler_params=ROW_PARAMS,
        name="norm_mod",
    )(xs, w.reshape(1, d), mods)


def _merge_kernel(ym_ref, yh_ref, og_ref, g_ref, gm_ref, gh_ref, gg_ref, x_ref, m_ref, w1_ref, w2_ref, w3_ref, wo_ref, npost_ref, npre_ref, rw_ref, rb_ref, xo_ref, h_ref, lg_ref):
    def sig(ref):
        return jax.nn.sigmoid(ref[...].astype(F32))

    go = g_ref[...].astype(F32)
    yg = (og_ref[...].astype(F32) * (go * jax.nn.sigmoid(go))).astype(BF16)
    merged = sig(gm_ref) * jnp.dot(ym_ref[...], w1_ref[...], preferred_element_type=F32)
    merged = merged + sig(gh_ref) * jnp.dot(yh_ref[...], w2_ref[...], preferred_element_type=F32)
    merged = merged + sig(gg_ref) * jnp.dot(yg, w3_ref[...], preferred_element_type=F32)
    mix = jnp.dot(merged.astype(BF16), wo_ref[...], preferred_element_type=F32)
    x = x_ref[...] + _mrow(m_ref, M_GATE_MIX) * (_rms(mix) * npost_ref[...])
    xo_ref[...] = x
    h = (_rms(x) * npre_ref[...] * (1.0 + _mrow(m_ref, M_SCALE_FFN)) + _mrow(m_ref, M_SHIFT_FFN)).astype(BF16)
    h_ref[...] = h
    lg_ref[...] = jnp.dot(h, rw_ref[...], preferred_element_type=F32) + rb_ref[...]


def _merge(ym, yh, og, p_rm, col_g, col_gate, xs, mods, w_br_ssm, w_br_hy, w_br_hg, w_out, norm_post, norm_ffn_pre, router_w, router_b, n_ctx):
    b, t, d = xs.shape
    rw = jnp.pad(router_w, ((0, 0), (0, LANES - N_EXPERTS))).astype(BF16)
    rb = jnp.pad(router_b, (0, LANES - N_EXPERTS)).reshape(1, LANES).astype(F32)

    def col_spec(col):
        return pl.BlockSpec((None, ROW_TILE, d), lambda bi, i: (bi, i, col // d))

    return pl.pallas_call(
        _merge_kernel,
        grid=(b, t // ROW_TILE),
        in_specs=[_row_spec(d), _row_spec(d), _row_spec(d), col_spec(col_g), col_spec(col_gate), col_spec(col_gate + d), col_spec(col_gate + 2 * d), _row_spec(d), _mod_spec(n_ctx),
                  _mat_spec(d, d), _mat_spec(d, d), _mat_spec(d, d), _mat_spec(d, d), _vec_spec(d), _vec_spec(d), _mat_spec(d, LANES), _vec_spec(LANES)],
        out_specs=[_row_spec(d), _row_spec(d), _row_spec(LANES)],
        out_shape=[jax.ShapeDtypeStruct((b, t, d), F32), jax.ShapeDtypeStruct((b, t, d), BF16), jax.ShapeDtypeStruct((b, t, LANES), F32)],
        compiler_params=ROW_PARAMS,
        name="branch_merge",
    )(ym, yh, og, p_rm, p_rm, p_rm, p_rm, xs, mods, w_br_ssm.astype(BF16), w_br_hy.astype(BF16), w_br_hg.astype(BF16), w_out.astype(BF16),
      norm_post.reshape(1, d), norm_ffn_pre.reshape(1, d), rw, rb)


def _post_ffn_kernel(y0_ref, y1_ref, y2_ref, y3_ref, x_ref, m_ref, w_ref, o_ref):
    f = y0_ref[...].astype(F32) + y1_ref[...].astype(F32) + y2_ref[...].astype(F32) + y3_ref[...].astype(F32)
    o_ref[...] = x_ref[...] + _mrow(m_ref, M_GATE_FFN) * (_rms(f) * w_ref[...])


def _post_ffn(f4, xs, mods, norm_post, n_ctx, skip):
    b, _, d = xs.shape
    t = f4.shape[2]
    i0 = skip // ROW_TILE

    def k_spec(k):
        return pl.BlockSpec((None, None, ROW_TILE, d), lambda bi, i: (k, bi, i, 0))

    return pl.pallas_call(
        _post_ffn_kernel,
        grid=(b, t // ROW_TILE),
        in_specs=[k_spec(k) for k in range(TOP_K)] + [
            pl.BlockSpec((None, ROW_TILE, d), lambda bi, i: (bi, i0 + i, 0)),
            pl.BlockSpec((None, None, MOD_ROWS, d), lambda bi, i: (bi, jnp.where(i0 + i < n_ctx // ROW_TILE, 0, 1), 0, 0)),
            _vec_spec(d)],
        out_specs=_row_spec(d),
        out_shape=jax.ShapeDtypeStruct((b, t, d), F32),
        compiler_params=ROW_PARAMS,
        name="post_ffn",
    )(f4, f4, f4, f4, xs, mods, norm_post.reshape(1, d))


def _moe_ffn(h2, logits, t_per_b, skip, li, w1, b1, w2, b2):
    t = logits.shape[0]
    d = h2.shape[1]
    n = t * TOP_K
    n_tiles = n // MOE_BLOCK
    top_v, top_e = lax.top_k(logits, TOP_K)
    gate_w = jax.nn.softmax(top_v, axis=-1)
    flat_e = top_e.reshape(n).astype(jnp.int32)
    iota = jnp.arange(n, dtype=jnp.int32)
    _, order, sw = lax.sort((flat_e, iota, gate_w.reshape(n)), num_keys=1, is_stable=True)
    _, inv = lax.sort((order, iota), num_keys=1)
    tok = order // TOP_K
    xs = h2[tok + (tok // t_per_b + 1) * skip]
    counts = jnp.sum((flat_e[:, None] == jnp.arange(N_EXPERTS, dtype=jnp.int32)[None, :]).astype(jnp.int32), axis=0)
    end = jnp.cumsum(counts)
    start = end - counts
    first_tile = start // MOE_BLOCK
    n_items = jnp.where(counts > 0, (end - 1) // MOE_BLOCK - first_tile + 1, 0)
    items_end = jnp.cumsum(n_items)
    w = jnp.arange(n_tiles + N_EXPERTS, dtype=jnp.int32)
    valid = w < items_end[-1]
    e_w = jnp.minimum(jnp.sum((w[:, None] >= items_end[None, :]).astype(jnp.int32), axis=1), N_EXPERTS - 1)
    tile_w = first_tile[e_w] + (w - (items_end[e_w] - n_items[e_w]))
    lo = jnp.where(valid, jnp.maximum(start[e_w], tile_w * MOE_BLOCK), 0)
    hi = jnp.where(valid, jnp.minimum(end[e_w], (tile_w + 1) * MOE_BLOCK), 0)
    tile_w = jnp.where(valid, tile_w, n_tiles - 1)
    first = jnp.concatenate([jnp.ones((1,), jnp.int32), (tile_w[1:] != tile_w[:-1]).astype(jnp.int32)])
    newexp = jnp.concatenate([jnp.ones((1,), jnp.int32), (e_w[1:] != e_w[:-1]).astype(jnp.int32)])
    ys = _moe_experts(xs, sw, tile_w.astype(jnp.int32), e_w, lo.astype(jnp.int32), hi.astype(jnp.int32), first, newexp, li, w1, b1, w2, b2)
    return ys[inv.reshape(t, TOP_K).T]


def kernel(x, c, ctx, c_ctx, w_mod, b_mod, norm_mix_pre, norm_mix_post, norm_ffn_pre, norm_ffn_post, w_in, ssm_conv_w, ssm_conv_b, ssm_dt_bias, ssm_a_log, ssm_d, ssm_norm, hy_conv_w, hy_conv_b, hy_w1, hy_b1, hy_w2, hy_b2, hy_w3, hy_b3, hy_freq, hy_bias, hg_lb_logits, hg_norm, w_br_ssm, w_br_hy, w_br_hg, w_out, router_w, router_b, exp_w1, exp_b1, exp_w2, exp_b2):
    hp = lax.Precision.HIGHEST
    b, n_lat, d = x.shape
    n_ctx = ctx.shape[1]
    lb = jax.nn.softmax(hg_lb_logits.astype(F32), axis=1)
    lb = jnp.cumsum(lb, axis=1) - lb[:, :1]
    silu_c = jax.nn.silu(c)
    silu_cc = jax.nn.silu(c_ctx)
    xs = jnp.concatenate([ctx, x], axis=1)
    for li in range(DEPTH):
        mx = (jnp.dot(silu_c, w_mod[li], precision=hp) + b_mod[li]).reshape(b, 1, 6, d)
        mc = jnp.broadcast_to((jnp.dot(silu_cc, w_mod[li], precision=hp) + b_mod[li]).reshape(1, 1, 6, d), (b, 1, 6, d))
        mods = jnp.pad(jnp.concatenate([mc, mx], axis=1), ((0, 0), (0, 0), (0, MOD_ROWS - 6), (0, 0)))
        h = _norm_mod(xs, norm_mix_pre[li], mods, n_ctx)
        ym, yh, og, p_rm, col_g, col_gate = _mixer_branches(h, w_in[li], lb[:, li], n_ctx, ssm_conv_w[li], ssm_conv_b[li], ssm_dt_bias[li], ssm_a_log[li], ssm_d[li], ssm_norm[li], hy_conv_w[li], hy_conv_b[li], hy_w1[li], hy_b1[li], hy_w2[li], hy_b2[li], hy_w3[li], hy_b3[li], hy_freq[li], hy_bias[li], hg_norm[li])
        xs, h_ffn, logits = _merge(ym, yh, og, p_rm, col_g, col_gate, xs, mods, w_br_ssm[li], w_br_hy[li], w_br_hg[li], w_out[li], norm_mix_post[li], norm_ffn_pre[li], router_w[li], router_b[li], n_ctx)
        skip = n_ctx if li == DEPTH - 1 else 0
        t = xs.shape[1] - skip
        f4 = _moe_ffn(h_ffn.reshape(-1, d), logits[:, skip:, :N_EXPERTS].reshape(b * t, N_EXPERTS), t, skip, li, exp_w1, exp_b1[li], exp_w2, exp_b2[li])
        xs = _post_ffn(f4.reshape(TOP_K, b, t, d), xs, mods, norm_ffn_post[li], n_ctx, skip)
    return xs
```

```python
import functools
import math

import jax
import jax.numpy as jnp
import numpy as np
from jax import lax
from jax.experimental import pallas as pl
from jax.experimental.pallas import tpu as pltpu

D_MODEL = 1024
DEPTH = 2
GRID_W = 64

SSM_HEADS = 16
SSM_HEAD_DIM = 64
SSM_INNER = SSM_HEADS * SSM_HEAD_DIM
SSM_STATE = 128
SSM_GROUPS = 4
SSD_CHUNK = 128
SSM_XBC = SSM_INNER + 2 * SSM_GROUPS * SSM_STATE

HY_WIDTH = D_MODEL
HY_ORDER = 2
HY_BANDS = 16
HY_FAST_DECAY = 0.3
HY_SLOW_DECAY = 1.5
HY_DECAY_TARGET = 1e-2

HG_HEADS = 8
HG_KDIM = 128
HG_VDIM = D_MODEL // HG_HEADS
HG_QK = HG_HEADS * HG_KDIM
HG_V = HG_HEADS * HG_VDIM
HG_CHUNK = 64
F_FLOOR = 1e-20

N_EXPERTS = 32
TOP_K = 4
D_FF = D_MODEL
SWIGLU_LIMIT = 7.0
SWIGLU_ALPHA = 1.702
MOE_BLOCK = 512

N_BRANCHES = 3
IN_SIZES = (SSM_INNER, SSM_XBC, 2 * SSM_HEADS, (HY_ORDER + 1) * HY_WIDTH, HG_QK, 2 * HG_QK, HG_V, HG_V, N_BRANCHES * D_MODEL)
EPS = 1e-6
F32 = jnp.float32
BF16 = jnp.bfloat16

LANES = 128
VMEM_LIMIT = 56 * 1024 * 1024


def _mm_kernel(a_ref, b_ref, o_ref):
    o_ref[...] = jnp.dot(a_ref[...], b_ref[...], preferred_element_type=F32).astype(o_ref.dtype)


def _mm(a, b, out_dtype=F32, tm=1024, tn=1024):
    m, k = a.shape
    n = b.shape[1]
    tm = math.gcd(m, tm)
    tn = math.gcd(n, tn)
    assert tm % SUBLANES == 0 and tn % LANES == 0, (m, n, tm, tn)
    return pl.pallas_call(
        _mm_kernel,
        grid=(n // tn, m // tm),
        in_specs=[pl.BlockSpec((tm, k), lambda j, i: (i, 0)), pl.BlockSpec((k, tn), lambda j, i: (0, j))],
        out_specs=pl.BlockSpec((tm, tn), lambda j, i: (i, j)),
        out_shape=jax.ShapeDtypeStruct((m, n), out_dtype),
        compiler_params=pltpu.CompilerParams(dimension_semantics=("arbitrary", "arbitrary"), vmem_limit_bytes=VMEM_LIMIT),
        name="dense_mm",
    )(a, b)


def _moe_kernel(tile_ref, exp_ref, lo_ref, hi_ref, first_ref, newexp_ref, x_ref, sw_ref, w1_ref, b1_ref, w2_ref, b2_ref, o_ref, w1b_ref, w2b_ref):
    del exp_ref
    w = pl.program_id(0)
    lo, hi = lo_ref[w], hi_ref[w]

    @pl.when(newexp_ref[w] == 1)
    def _():
        def cast_rows(i, carry):
            rows = pl.ds(pl.multiple_of(i * LANES, LANES), LANES)
            w1b_ref[rows, :] = w1_ref[rows, :].astype(BF16)
            w2b_ref[rows, :] = w2_ref[rows, :].astype(BF16)
            return carry

        lax.fori_loop(0, D_MODEL // LANES, cast_rows, 0)

    @pl.when(hi > lo)
    def _():
        hh = jnp.dot(x_ref[...], w1b_ref[...], preferred_element_type=F32) + b1_ref[...]
        g = jnp.minimum(hh[:, :D_FF], SWIGLU_LIMIT)
        u = jnp.clip(hh[:, D_FF:], -SWIGLU_LIMIT, SWIGLU_LIMIT)
        act = (u + 1.0) * g * jax.nn.sigmoid(SWIGLU_ALPHA * g)
        y = jnp.dot(act.astype(BF16), w2b_ref[...], preferred_element_type=F32) + b2_ref[...]
        y = (y * sw_ref[...]).astype(o_ref.dtype)
        rows = tile_ref[w] * MOE_BLOCK + lax.broadcasted_iota(jnp.int32, (MOE_BLOCK, 1), 0)
        mine = jnp.logical_and(rows >= lo, rows < hi)

        @pl.when(first_ref[w] == 1)
        def _():
            o_ref[...] = jnp.where(mine, y, jnp.zeros_like(y))

        @pl.when(first_ref[w] != 1)
        def _():
            o_ref[...] = jnp.where(mine, y, o_ref[...])


def _moe_experts(xs, sw, tile_w, exp_w, lo, hi, first, newexp, li, w1, b1, w2, b2):
    n, d = xs.shape
    grid_spec = pltpu.PrefetchScalarGridSpec(
        num_scalar_prefetch=6,
        grid=(tile_w.shape[0],),
        in_specs=[
            pl.BlockSpec((MOE_BLOCK, d), lambda w, tl, ex, lo_, hi_, fi, ne: (tl[w], 0)),
            pl.BlockSpec((MOE_BLOCK, 1), lambda w, tl, ex, lo_, hi_, fi, ne: (tl[w], 0)),
            pl.BlockSpec((None, None, d, 2 * D_FF), lambda w, tl, ex, lo_, hi_, fi, ne: (li, ex[w], 0, 0)),
            pl.BlockSpec((None, 1, 2 * D_FF), lambda w, tl, ex, lo_, hi_, fi, ne: (ex[w], 0, 0)),
            pl.BlockSpec((None, None, D_FF, d), lambda w, tl, ex, lo_, hi_, fi, ne: (li, ex[w], 0, 0)),
            pl.BlockSpec((None, 1, d), lambda w, tl, ex, lo_, hi_, fi, ne: (ex[w], 0, 0)),
        ],
        out_specs=pl.BlockSpec((MOE_BLOCK, d), lambda w, tl, ex, lo_, hi_, fi, ne: (tl[w], 0)),
        scratch_shapes=[pltpu.VMEM((d, 2 * D_FF), BF16), pltpu.VMEM((D_FF, d), BF16)],
    )
    return pl.pallas_call(
        _moe_kernel,
        grid_spec=grid_spec,
        out_shape=jax.ShapeDtypeStruct((n, d), BF16),
        compiler_params=pltpu.CompilerParams(dimension_semantics=("arbitrary",), vmem_limit_bytes=VMEM_LIMIT),
        name="moe_experts",
    )(tile_w, exp_w, lo, hi, first, newexp, xs, sw.reshape(n, 1), w1, b1.reshape(N_EXPERTS, 1, 2 * D_FF), w2, b2.reshape(N_EXPERTS, 1, d))


SUBLANES = 8
NEG_BIG = -1e30
HG_STEP_CHUNKS = 4
HIER_LEVELS = (64, 32, 16, 8, 4)
LOG2E = math.log2(math.e)


def _split3(x):
    h1 = x.astype(BF16)
    r1 = x - h1.astype(F32)
    h2 = r1.astype(BF16)
    h3 = (r1 - h2.astype(F32)).astype(BF16)
    return h1, h2, h3


def _dot_nt(a, b):
    return lax.dot_general(a, b, (((1,), (1,)), ((), ())), preferred_element_type=F32)


def _gla_kernel(q_ref, a_ref, v_ref, lb_ref, *rest, reverse):
    if reverse:
        of_ref, w_ref, o_ref, st_ref, at_ref = rest
    else:
        o_ref, st_ref, at_ref = rest
    Q = HG_CHUNK

    @pl.when(pl.program_id(1) == 0)
    def _():
        st_ref[...] = jnp.zeros_like(st_ref)

    row = lax.broadcasted_iota(jnp.int32, (Q, Q), 0)
    col = lax.broadcasted_iota(jnp.int32, (Q, Q), 1)
    tri = jnp.where((col >= row) if reverse else (col <= row), 1.0, 0.0).astype(BF16)
    same_block = {s: (row // s) == (col // s) for s in HIER_LEVELS[1:] + (2, 1)}
    rowk = lax.broadcasted_iota(jnp.int32, (Q, HG_KDIM), 0)
    q_rows = {s: ((rowk % s) < s // 2) if reverse else ((rowk % s) >= s // 2) for s in HIER_LEVELS + (2,)}
    of_w = (of_ref, w_ref) if reverse else (None, None)
    for sub in (range(HG_STEP_CHUNKS - 1, -1, -1) if reverse else range(HG_STEP_CHUNKS)):
        _gla_chunk(sub, q_ref, a_ref, v_ref, lb_ref[...], of_w, o_ref, st_ref, at_ref, tri, same_block, q_rows, reverse)


def _gla_chunk(sub, q_ref, a_ref, v_ref, lb, of_w, o_ref, st_ref, at_ref, tri, same_block, q_rows, reverse):
    Q = HG_CHUNK
    rs = slice(sub * Q, (sub + 1) * Q)
    of_ref, w_ref = of_w
    shp3 = (Q // SUBLANES, SUBLANES, HG_KDIM)
    sub3 = lax.broadcasted_iota(jnp.int32, shp3, 1)
    a = a_ref[rs, :].astype(F32)
    f_all = jnp.maximum(lb + (1.0 - lb) * jax.nn.sigmoid(a), F_FLOOR)
    kk = (1.0 - lb) * jax.nn.sigmoid(-a)
    g_all = sum(jnp.dot(tri, p, preferred_element_type=F32) for p in _split3(jnp.log(f_all) * LOG2E))
    q_all = q_ref[rs, :].astype(F32)
    q_all = q_all * jax.nn.sigmoid(q_all)
    v_all = v_ref[rs, :]
    tot = 0 if reverse else Q - 1

    def level_ref(g, s):
        half = s // 2
        m_off = half if reverse else half - 1
        if s >= 2 * SUBLANES:
            return jnp.concatenate([jnp.broadcast_to(g[b0 + m_off:b0 + m_off + 1, :], (s, HG_KDIM)) for b0 in range(0, Q, s)], axis=0)
        g3 = g.reshape(shp3)
        ref = jnp.broadcast_to(g3[:, m_off:m_off + 1, :], shp3)
        for b0 in range(s, SUBLANES, s):
            ref = jnp.where(sub3 >= b0, jnp.broadcast_to(g3[:, b0 + m_off:b0 + m_off + 1, :], shp3), ref)
        return ref.reshape(Q, HG_KDIM)

    for h in range(HG_HEADS):
        sl = slice(h * HG_KDIM, (h + 1) * HG_KDIM)
        g, qh, kh, fh = g_all[:, sl], q_all[:, sl], kk[:, sl], f_all[:, sl]
        attn = None
        for s in HIER_LEVELS:
            gref = level_ref(g, s)
            eq = jnp.exp2(jnp.where(q_rows[s], g - gref, NEG_BIG))
            ek = jnp.exp2(jnp.where(q_rows[s], NEG_BIG, gref - g))
            lvl = _dot_nt((qh * eq).astype(BF16), (kh * ek).astype(BF16))
            attn = lvl if attn is None else jnp.where(same_block[s], lvl, attn)
        lvl = _dot_nt(jnp.where(q_rows[2], qh * fh, 0.0).astype(BF16), jnp.where(q_rows[2], 0.0, kh).astype(BF16))
        attn = jnp.where(same_block[2], lvl, attn)
        attn = jnp.where(same_block[1], _dot_nt(qh.astype(BF16), kh.astype(BF16)), attn)
        at_ref[sub, h] = attn.astype(BF16)

    for h in range(HG_HEADS):
        sl = slice(h * HG_KDIM, (h + 1) * HG_KDIM)
        g, qh, kh, vb = g_all[:, sl], q_all[:, sl], kk[:, sl], v_all[:, sl]
        g_tot = g[tot:tot + 1, :]
        st = st_ref[h]
        o = _dot_nt((qh * jnp.exp2(g)).astype(BF16), st.astype(BF16))
        o = o + jnp.dot(at_ref[sub, h], vb, preferred_element_type=F32)
        k_st = (kh * jnp.exp2(g_tot - g)).astype(BF16)
        st_ref[h] = st * jnp.exp2(g_tot) + jnp.dot(vb.astype(F32).T.astype(BF16), k_st, preferred_element_type=F32)
        if reverse:
            o = o + of_ref[rs, sl]
            o = o * lax.rsqrt(jnp.mean(o * o, axis=-1, keepdims=True) + EPS) * w_ref[:, sl]
        o_ref[rs, sl] = o.astype(o_ref.dtype)


def _hgrn_scan(p_cm, lb, norm_w, n_ctx):
    b, t, _ = p_cm.shape
    col_q, col_f, col_i = 0, 1, 3
    step_rows = HG_STEP_CHUNKS * HG_CHUNK
    assert n_ctx % step_rows == 0 and t % step_rows == 0, (n_ctx, t, step_rows)
    nc, ncc = t // step_rows, n_ctx // step_rows
    blk = (None, step_rows, HG_QK)
    scratch = [pltpu.VMEM((HG_HEADS, HG_VDIM, HG_KDIM), F32), pltpu.VMEM((HG_STEP_CHUNKS, HG_HEADS, HG_CHUNK, HG_CHUNK), BF16)]
    params = pltpu.CompilerParams(dimension_semantics=("arbitrary", "arbitrary"), vmem_limit_bytes=VMEM_LIMIT)
    row_spec = pl.BlockSpec((1, HG_QK), lambda bi, s: (0, 0))

    def fwd_chunk(s):
        return s

    def bwd_chunk(s):
        return jnp.where(s < ncc, ncc - 1 - s, nc + ncc - 1 - s)

    o_f = pl.pallas_call(
        functools.partial(_gla_kernel, reverse=False),
        grid=(b, nc),
        in_specs=[
            pl.BlockSpec(blk, lambda bi, s: (bi, fwd_chunk(s), col_q)),
            pl.BlockSpec(blk, lambda bi, s: (bi, fwd_chunk(s), col_f)),
            pl.BlockSpec(blk, lambda bi, s: (bi, fwd_chunk(s), col_i)),
            row_spec,
        ],
        out_specs=pl.BlockSpec(blk, lambda bi, s: (bi, fwd_chunk(s), 0)),
        out_shape=jax.ShapeDtypeStruct((b, t, HG_V), F32),
        scratch_shapes=scratch,
        compiler_params=params,
        name="gla_fwd",
    )(p_cm, p_cm, p_cm, lb[0:1])
    return pl.pallas_call(
        functools.partial(_gla_kernel, reverse=True),
        grid=(b, nc),
        in_specs=[
            pl.BlockSpec(blk, lambda bi, s: (bi, bwd_chunk(s), col_q)),
            pl.BlockSpec(blk, lambda bi, s: (bi, bwd_chunk(s), col_f + 1)),
            pl.BlockSpec(blk, lambda bi, s: (bi, bwd_chunk(s), col_i)),
            row_spec,
            pl.BlockSpec(blk, lambda bi, s: (bi, bwd_chunk(s), 0)),
            row_spec,
        ],
        out_specs=pl.BlockSpec(blk, lambda bi, s: (bi, bwd_chunk(s), 0)),
        out_shape=jax.ShapeDtypeStruct((b, t, HG_V), BF16),
        scratch_shapes=scratch,
        compiler_params=params,
        name="gla_bwd",
    )(p_cm, p_cm, p_cm, lb[1:2], o_f, norm_w.reshape(1, HG_V))


CONV_TILE = 256
HALO = 16


def _dwconv_kernel(prev_ref, cur_ref, next_ref, w_ref, b_ref, *o_refs, taps, n_ctx_tiles, n_tiles, silu):
    i = pl.program_id(2)
    first = jnp.logical_or(i == 0, i == n_ctx_tiles)
    last = jnp.logical_or(i == n_ctx_tiles - 1, i == n_tiles - 1)
    pad = taps // 2
    xp = jnp.where(first, 0.0, prev_ref[...].astype(F32))
    xn = jnp.where(last, 0.0, next_ref[...].astype(F32))
    xcat = jnp.concatenate([xp, cur_ref[...].astype(F32), xn], axis=0)
    acc = jnp.broadcast_to(b_ref[...], cur_ref.shape).astype(F32)
    for k in range(taps):
        off = HALO - pad + k
        acc = acc + w_ref[k:k + 1, :] * xcat[off:off + CONV_TILE, :]
    if silu:
        acc = acc * jax.nn.sigmoid(acc)
    if len(o_refs) == 1:
        o_refs[0][...] = acc.astype(o_refs[0].dtype)
    else:
        ctx_ref, lat_ref = o_refs

        @pl.when(i < n_ctx_tiles)
        def _():
            ctx_ref[...] = acc.astype(ctx_ref.dtype)

        @pl.when(i >= n_ctx_tiles)
        def _():
            lat_ref[...] = acc.astype(lat_ref.dtype)


def _dwconv_stream(x, w, bias, n_ctx, silu, c0=0, ct=1024, split=False):
    b, t, _ = x.shape
    taps, c = w.shape
    n_tiles = t // CONV_TILE
    nct = n_ctx // CONV_TILE
    hb = CONV_TILE // HALO
    n_halo = t // HALO
    j0 = c0 // ct
    kern = functools.partial(_dwconv_kernel, taps=taps, n_ctx_tiles=nct, n_tiles=n_tiles, silu=silu)
    if split:
        out_specs = [pl.BlockSpec((None, CONV_TILE, ct), lambda bi, j, i: (bi, jnp.minimum(i, nct - 1), j)),
                     pl.BlockSpec((None, CONV_TILE, ct), lambda bi, j, i: (bi, jnp.maximum(i - nct, 0), j))]
        out_shape = [jax.ShapeDtypeStruct((b, n_ctx, c), BF16), jax.ShapeDtypeStruct((b, t - n_ctx, c), BF16)]
    else:
        out_specs = pl.BlockSpec((None, CONV_TILE, ct), lambda bi, j, i: (bi, i, j))
        out_shape = jax.ShapeDtypeStruct((b, t, c), BF16)
    return pl.pallas_call(
        kern,
        grid=(b, c // ct, n_tiles),
        in_specs=[
            pl.BlockSpec((None, HALO, ct), lambda bi, j, i: (bi, jnp.maximum(i * hb - 1, 0), j0 + j)),
            pl.BlockSpec((None, CONV_TILE, ct), lambda bi, j, i: (bi, i, j0 + j)),
            pl.BlockSpec((None, HALO, ct), lambda bi, j, i: (bi, jnp.minimum((i + 1) * hb, n_halo - 1), j0 + j)),
            pl.BlockSpec((taps, ct), lambda bi, j, i: (0, j)),
            pl.BlockSpec((1, ct), lambda bi, j, i: (0, j)),
        ],
        out_specs=out_specs,
        out_shape=out_shape,
        compiler_params=pltpu.CompilerParams(dimension_semantics=("arbitrary",) * 3, vmem_limit_bytes=VMEM_LIMIT),
        name="dwconv",
    )(x, x, x, w.astype(F32), bias.reshape(1, c).astype(F32))


SSM_GHEADS = SSM_HEADS // SSM_GROUPS
SSM_GP = SSM_GHEADS * SSM_HEAD_DIM


def _ssd_kernel(xbc_ref, dt_ref, dtb_ref, a_ref, *rest, reverse):
    if reverse:
        yf_ref, z_ref, dsk_ref, nw_ref, o_ref, st_ref, m_ref, xd_ref, xst_ref, y_ref = rest
    else:
        o_ref, st_ref, m_ref, xd_ref, xst_ref, y_ref = rest
    Q = SSD_CHUNK

    @pl.when(pl.program_id(1) == 0)
    def _():
        st_ref[...] = jnp.zeros_like(st_ref)

    row = lax.broadcasted_iota(jnp.int32, (Q, Q), 0)
    col = lax.broadcasted_iota(jnp.int32, (Q, Q), 1)
    keep = (col >= row) if reverse else (col <= row)
    tri = jnp.where(keep, 1.0, 0.0).astype(BF16)
    expand = jnp.where(lax.broadcasted_iota(jnp.int32, (LANES, SSM_INNER), 1) // SSM_HEAD_DIM == lax.broadcasted_iota(jnp.int32, (LANES, SSM_INNER), 0), 1.0, 0.0).astype(BF16)

    dt = jax.nn.softplus(dt_ref[...] + dtb_ref[...])
    a = dt * a_ref[...]
    cs = sum(jnp.dot(tri, p, preferred_element_type=F32) for p in _split3(a))
    tot = 0 if reverse else Q - 1
    cs_tot = cs[tot:tot + 1, :]
    cs_t = cs.T
    dt_e = jnp.dot(dt.astype(BF16), expand, preferred_element_type=F32)
    e_in = jnp.dot(jnp.exp(cs).astype(BF16), expand, preferred_element_type=F32)
    e_st = jnp.dot(jnp.exp(cs_tot - cs).astype(BF16), expand, preferred_element_type=F32)
    e_tot = jnp.dot(jnp.broadcast_to(jnp.exp(cs_tot), (SUBLANES, LANES)).astype(BF16), expand, preferred_element_type=F32)[0:1, :]

    xs = xbc_ref[:, :SSM_INNER].astype(F32)
    xd_ref[...] = (xs * dt_e).astype(BF16)
    xst_ref[...] = (xs * dt_e * e_st).astype(BF16)
    for g in range(SSM_GROUPS):
        bm = xbc_ref[:, SSM_INNER + g * SSM_STATE:SSM_INNER + (g + 1) * SSM_STATE]
        cm = xbc_ref[:, SSM_INNER + (SSM_GROUPS + g) * SSM_STATE:SSM_INNER + (SSM_GROUPS + g + 1) * SSM_STATE]
        cb = _dot_nt(cm, bm)
        for h in range(g * SSM_GHEADS, (g + 1) * SSM_GHEADS):
            diff = jnp.broadcast_to(cs[:, h:h + 1], (Q, Q)) - jnp.broadcast_to(cs_t[h:h + 1, :], (Q, Q))
            m_ref[h] = (cb * jnp.exp(jnp.where(keep, diff, NEG_BIG))).astype(BF16)

    lane_lo = lax.broadcasted_iota(jnp.int32, (Q, LANES), 1) < SSM_HEAD_DIM
    for g in range(SSM_GROUPS):
        bm = xbc_ref[:, SSM_INNER + g * SSM_STATE:SSM_INNER + (g + 1) * SSM_STATE]
        cm = xbc_ref[:, SSM_INNER + (SSM_GROUPS + g) * SSM_STATE:SSM_INNER + (SSM_GROUPS + g + 1) * SSM_STATE]
        gl = slice(g * SSM_GP, (g + 1) * SSM_GP)
        st = st_ref[g]
        y_off = jnp.dot(cm, st.astype(BF16), preferred_element_type=F32) * e_in[:, gl]
        for hp in range(SSM_GHEADS // 2):
            h0 = g * SSM_GHEADS + 2 * hp
            lanes = slice((h0 // 2) * LANES, (h0 // 2 + 1) * LANES)
            pair = [jnp.dot(m_ref[h], xd_ref[:, lanes], preferred_element_type=F32) for h in (h0, h0 + 1)]
            y_ref[:, lanes] = jnp.where(lane_lo, pair[0], pair[1]) + y_off[:, hp * LANES:(hp + 1) * LANES]
        st_ref[g] = st * e_tot[:, gl] + jnp.dot(bm.astype(F32).T.astype(BF16), xst_ref[:, gl], preferred_element_type=F32)
    y = y_ref[...]
    if reverse:
        y = (y + yf_ref[...] + xbc_ref[:, :SSM_INNER].astype(F32) * dsk_ref[...])
        zz = z_ref[...].astype(F32)
        y = y * (zz * jax.nn.sigmoid(zz))
        y = y * lax.rsqrt(jnp.mean(y * y, axis=-1, keepdims=True) + EPS) * nw_ref[...]
    o_ref[...] = y.astype(o_ref.dtype)


def _ssd_scan(xbc_act, dt2, p_rm, z_col, dt_bias, a_log, d_skip, norm_w, n_ctx):
    b, t, _ = xbc_act.shape
    nc, ncc = t // SSD_CHUNK, n_ctx // SSD_CHUNK

    def pad_heads(v):
        return jnp.pad(v.astype(F32), ((0, 0), (0, LANES - SSM_HEADS)))

    dtb = pad_heads(dt_bias.reshape(2, SSM_HEADS))
    a_neg = pad_heads(-jnp.exp(a_log.astype(F32)))
    dsk = jnp.repeat(d_skip.astype(F32), SSM_HEAD_DIM).reshape(1, SSM_INNER)
    scratch = [pltpu.VMEM((SSM_GROUPS, SSM_STATE, SSM_GP), F32), pltpu.VMEM((SSM_HEADS, SSD_CHUNK, SSD_CHUNK), BF16),
               pltpu.VMEM((SSD_CHUNK, SSM_INNER), BF16), pltpu.VMEM((SSD_CHUNK, SSM_INNER), BF16), pltpu.VMEM((SSD_CHUNK, SSM_INNER), F32)]
    params = pltpu.CompilerParams(dimension_semantics=("arbitrary", "arbitrary"), vmem_limit_bytes=VMEM_LIMIT)

    def bwd_chunk(s):
        return jnp.where(s < ncc, ncc - 1 - s, nc + ncc - 1 - s)

    def specs(chunk, d):
        return [
            pl.BlockSpec((None, SSD_CHUNK, SSM_XBC), lambda bi, s: (bi, chunk(s), 0)),
            pl.BlockSpec((None, SSD_CHUNK, LANES), lambda bi, s: (bi, chunk(s), d)),
            pl.BlockSpec((1, LANES), lambda bi, s: (0, 0)),
            pl.BlockSpec((1, LANES), lambda bi, s: (0, 0)),
        ]

    def inner_spec(chunk):
        return pl.BlockSpec((None, SSD_CHUNK, SSM_INNER), lambda bi, s: (bi, chunk(s), 0))

    row_spec = pl.BlockSpec((1, SSM_INNER), lambda bi, s: (0, 0))
    z_spec = pl.BlockSpec((None, SSD_CHUNK, SSM_INNER), lambda bi, s: (bi, bwd_chunk(s), z_col // SSM_INNER))
    y_f = pl.pallas_call(
        functools.partial(_ssd_kernel, reverse=False),
        grid=(b, nc),
        in_specs=specs(lambda s: s, 0),
        out_specs=inner_spec(lambda s: s),
        out_shape=jax.ShapeDtypeStruct((b, t, SSM_INNER), F32),
        scratch_shapes=scratch,
        compiler_params=params,
        name="ssd_fwd",
    )(xbc_act, dt2, dtb[0:1], a_neg[0:1])
    return pl.pallas_call(
        functools.partial(_ssd_kernel, reverse=True),
        grid=(b, nc),
        in_specs=specs(bwd_chunk, 1) + [inner_spec(bwd_chunk), z_spec, row_spec, row_spec],
        out_specs=inner_spec(bwd_chunk),
        out_shape=jax.ShapeDtypeStruct((b, t, SSM_INNER), BF16),
        scratch_shapes=scratch,
        compiler_params=params,
        name="ssd_bwd",
    )(xbc_act, dt2, dtb[1:2], a_neg[1:2], y_f, p_rm, dsk, norm_w.reshape(1, SSM_INNER).astype(F32))


HY_N2 = LANES
HY_CT = LANES
VMEM_LIMIT_HYENA = 60 * 1024 * 1024


def _hy_dims(L):
    n1 = 2 * L // HY_N2
    k1n = n1 // 2 + 1
    k1p = -(-k1n // SUBLANES) * SUBLANES
    return n1, k1n, k1p


def _hy_tables(L, n1_rows):
    n1, k1n, k1p = _hy_dims(L)
    n = 2 * L
    k1 = np.arange(k1n, dtype=np.float64)[None, :, None]
    nn = (HY_N2 * np.arange(n1_rows, dtype=np.float64)[None, None, :] + np.arange(HY_N2, dtype=np.float64)[:, None, None])
    ang = 2.0 * np.pi * ((k1 * nn) % n) / n
    m1 = np.zeros((HY_N2, 2 * k1p, n1_rows), np.float32)
    m1[:, :k1n] = np.cos(ang)
    m1[:, k1p:k1p + k1n] = -np.sin(ang)
    m4 = np.transpose(m1, (0, 2, 1))
    kk = np.arange(HY_N2, dtype=np.float64)
    a2 = 2.0 * np.pi * ((kk[:, None] * kk[None, :]) % HY_N2) / HY_N2
    c, s = np.cos(a2), np.sin(a2)
    f3 = np.block([[c, s], [-s, c]]).astype(np.float32)
    f3i = np.block([[c, -s], [s, c]]).astype(np.float32)
    return jnp.asarray(m1, BF16), jnp.asarray(m4, BF16), jnp.asarray(f3, BF16), jnp.asarray(f3i, BF16)


def _hy_stage1(u_ref, a_ref, m1_ref, n1_rows, k1p):
    def body(n2, carry):
        xs = u_ref[pl.ds(n2, n1_rows, stride=HY_N2), :].astype(BF16)
        a_ref[pl.ds(pl.multiple_of(n2 * 2 * k1p, 2 * k1p), 2 * k1p), :] = jnp.dot(m1_ref[n2], xs, preferred_element_type=F32)
        return carry

    lax.fori_loop(0, HY_N2, body, 0, unroll=8)


def _hy_spectrum_slab(a_ref, f3_ref, k1s, k1p):
    blk = jnp.concatenate([jnp.concatenate([a_ref[pl.ds(k1, HY_N2, stride=2 * k1p), :], a_ref[pl.ds(k1p + k1, HY_N2, stride=2 * k1p), :]], axis=0) for k1 in k1s], axis=1)
    return jnp.dot(f3_ref[...], blk.astype(BF16), preferred_element_type=F32)


def _hy_k1_loop(body, k1n):
    def pair(p, carry):
        body((2 * p, 2 * p + 1))
        return carry

    lax.fori_loop(0, (k1n - 1) // 2, pair, 0, unroll=math.gcd((k1n - 1) // 2, 4))
    body((k1n - 1,))


def _hy_slab_rows(k1):
    start = k1 * 2 * HY_N2
    return pl.ds(start if isinstance(k1, int) else pl.multiple_of(start, 2 * HY_N2), 2 * HY_N2)


def _hy_conv(u_ref, yo_ref, a_ref, y_ref, h_ref, order, m1_ref, m4_ref, f3_ref, f3i_ref, n1_rows, k1n, k1p):
    _hy_stage1(u_ref, a_ref, m1_ref, n1_rows, k1p)

    def stage2(k1s):
        x = _hy_spectrum_slab(a_ref, f3_ref, k1s, k1p)
        h = jnp.concatenate([h_ref[order, _hy_slab_rows(k1), :] for k1 in k1s], axis=1).astype(F32)
        xr, xi, hr, hi = x[:HY_N2], x[HY_N2:], h[:HY_N2], h[HY_N2:]
        z = jnp.concatenate([xr * hr - xi * hi, xr * hi + xi * hr], axis=0).astype(BF16)
        c = jnp.dot(f3i_ref[...], z, preferred_element_type=F32)
        for i, k1 in enumerate(k1s):
            lanes = slice(i * HY_CT, (i + 1) * HY_CT)
            y_ref[pl.ds(k1, HY_N2, stride=2 * k1p), :] = c[:HY_N2, lanes]
            y_ref[pl.ds(k1p + k1, HY_N2, stride=2 * k1p), :] = c[HY_N2:, lanes]

    _hy_k1_loop(stage2, k1n)

    def stage3(n2, carry):
        d = y_ref[pl.ds(pl.multiple_of(n2 * 2 * k1p, 2 * k1p), 2 * k1p), :].astype(BF16)
        yo_ref[pl.ds(n2, n1_rows, stride=HY_N2), :] = jnp.dot(m4_ref[n2], d, preferred_element_type=F32)
        return carry

    lax.fori_loop(0, HY_N2, stage3, 0, unroll=8)


def _hyena_kernel(v_ref, x1_ref, x2_ref, h_ref, m1_ref, m4_ref, f3_ref, f3i_ref, bias_ref, o_ref, a_ref, y_ref, u_ref, yo_ref, *, n1_rows, k1n, k1p):
    @pl.when(jnp.logical_and(pl.program_id(0) == 0, pl.program_id(1) == 0))
    def _():
        y_ref[...] = jnp.zeros_like(y_ref)

    u_ref[...] = v_ref[...].astype(F32)
    for order, gate_ref in enumerate((x1_ref, x2_ref)):
        _hy_conv(u_ref, yo_ref, a_ref, y_ref, h_ref, order, m1_ref, m4_ref, f3_ref, f3i_ref, n1_rows, k1n, k1p)
        z = gate_ref[...].astype(F32) * (yo_ref[...] + u_ref[...] * bias_ref[order:order + 1, :])
        if order == 0:
            u_ref[...] = z
        else:
            o_ref[...] = z.astype(o_ref.dtype)


def _hy_filter_kernel(hid_ref, wf_ref, wb_ref, bf_ref, bb_ref, dl_ref, m1_ref, f3_ref, o_ref, a_ref, sf_ref, uf_ref, ub_ref, *, n1_rows, k1n, k1p, scale_mid, scale_edge):
    slab = 2 * HY_N2
    L = uf_ref.shape[0]
    hid_rows = math.gcd(L, 2 * HY_N2)

    def fill(w_ref, b_ref, dst_ref, drop_first):
        w = _split3(w_ref[...])

        def body(i, energy):
            rows = pl.ds(pl.multiple_of(i * hid_rows, hid_rows), hid_rows)
            hid = _split3(hid_ref[rows, :])
            acc = sum(jnp.dot(hid[p], w[q], preferred_element_type=F32) for p in range(2) for q in range(2 - p))
            pos = i * hid_rows + lax.broadcasted_iota(jnp.int32, (hid_rows, HY_CT), 0)
            vals = (acc + b_ref[...]) * jnp.exp(pos.astype(F32) * (-1.0 / (L - 1)) * dl_ref[...])
            if drop_first:
                vals = jnp.where(pos == 0, 0.0, vals)
            dst_ref[rows, :] = vals
            return energy + jnp.sum(vals * vals, axis=0, keepdims=True)

        return lax.fori_loop(0, L // hid_rows, body, jnp.zeros((1, HY_CT), F32), unroll=math.gcd(L // hid_rows, 4))

    norm = lax.rsqrt(fill(wf_ref, bf_ref, uf_ref, False) + fill(wb_ref, bb_ref, ub_ref, True) + EPS)

    _hy_stage1(uf_ref, a_ref, m1_ref, n1_rows, k1p)

    def keep_fwd(k1s):
        x = _hy_spectrum_slab(a_ref, f3_ref, k1s, k1p)
        for i, k1 in enumerate(k1s):
            sf_ref[_hy_slab_rows(k1), :] = x[:, i * HY_CT:(i + 1) * HY_CT]

    _hy_k1_loop(keep_fwd, k1n)
    _hy_stage1(ub_ref, a_ref, m1_ref, n1_rows, k1p)

    def combine(k1s):
        xb_all = _hy_spectrum_slab(a_ref, f3_ref, k1s, k1p)
        for i, k1 in enumerate(k1s):
            xb = xb_all[:, i * HY_CT:(i + 1) * HY_CT]
            xf = sf_ref[_hy_slab_rows(k1), :]
            w = norm * jnp.where(jnp.logical_or(k1 == 0, k1 == k1n - 1), scale_edge, scale_mid)
            h = jnp.concatenate([xf[:HY_N2] + xb[:HY_N2], xf[HY_N2:] - xb[HY_N2:]], axis=0)
            o_ref[_hy_slab_rows(k1), :] = (h * w).astype(o_ref.dtype)

    _hy_k1_loop(combine, k1n)


def _single(block_shape, index_map):
    return pl.BlockSpec(block_shape, index_map, pipeline_mode=pl.Buffered(1))


def _hyena_filter_spectrum_pallas(hidden, w3, b3, deltas):
    L, ffn = hidden.shape
    c = deltas.shape[0]
    nct = c // HY_CT
    n = 2 * L
    n1, k1n, k1p = _hy_dims(L)
    n1_rows = L // HY_N2
    m1, _, f3, _ = _hy_tables(L, n1_rows)
    kern = functools.partial(_hy_filter_kernel, n1_rows=n1_rows, k1n=k1n, k1p=k1p, scale_mid=2.0 / n, scale_edge=1.0 / n)
    return pl.pallas_call(
        kern,
        grid=(HY_ORDER, nct),
        in_specs=[
            _single((L, ffn), lambda o, j: (0, 0)),
            pl.BlockSpec((ffn, HY_CT), lambda o, j: (0, o * nct + j)),
            pl.BlockSpec((ffn, HY_CT), lambda o, j: (0, (HY_ORDER + o) * nct + j)),
            pl.BlockSpec((1, HY_CT), lambda o, j: (0, o * nct + j)),
            pl.BlockSpec((1, HY_CT), lambda o, j: (0, (HY_ORDER + o) * nct + j)),
            pl.BlockSpec((1, HY_CT), lambda o, j: (0, j)),
            _single((HY_N2, 2 * k1p, n1_rows), lambda o, j: (0, 0, 0)),
            _single((2 * HY_N2, 2 * HY_N2), lambda o, j: (0, 0)),
        ],
        out_specs=pl.BlockSpec((None, k1n * 2 * HY_N2, HY_CT), lambda o, j: (o, 0, j)),
        out_shape=jax.ShapeDtypeStruct((HY_ORDER, k1n * 2 * HY_N2, c), BF16),
        scratch_shapes=[
            pltpu.VMEM((k1p * 2 * HY_N2, HY_CT), F32),
            pltpu.VMEM((k1n * 2 * HY_N2, HY_CT), F32),
            pltpu.VMEM((L, HY_CT), F32),
            pltpu.VMEM((L, HY_CT), F32),
        ],
        compiler_params=pltpu.CompilerParams(dimension_semantics=("arbitrary", "arbitrary"), vmem_limit_bytes=VMEM_LIMIT_HYENA),
        name="hyena_filter_dft",
    )(hidden, w3, w3, b3.reshape(1, -1), b3.reshape(1, -1), deltas.reshape(1, c), m1, f3)


def _hyena_long(hy, h_spec, bias):
    b, L, c3 = hy.shape
    c = c3 // (HY_ORDER + 1)
    nct = c // HY_CT
    n1, k1n, k1p = _hy_dims(L)
    n1_rows = L // HY_N2
    m1, m4, f3, f3i = _hy_tables(L, n1_rows)
    kern = functools.partial(_hyena_kernel, n1_rows=n1_rows, k1n=k1n, k1p=k1p)

    def col(part):
        return _single((None, L, HY_CT), lambda j, bi: (bi, 0, part * nct + j))

    return pl.pallas_call(
        kern,
        grid=(nct, b),
        in_specs=[
            col(0), col(1), col(2),
            _single((HY_ORDER, k1n * 2 * HY_N2, HY_CT), lambda j, bi: (0, 0, j)),
            _single((HY_N2, 2 * k1p, n1_rows), lambda j, bi: (0, 0, 0)),
            _single((HY_N2, n1_rows, 2 * k1p), lambda j, bi: (0, 0, 0)),
            _single((2 * HY_N2, 2 * HY_N2), lambda j, bi: (0, 0)),
            _single((2 * HY_N2, 2 * HY_N2), lambda j, bi: (0, 0)),
            pl.BlockSpec((HY_ORDER, HY_CT), lambda j, bi: (0, j)),
        ],
        out_specs=pl.BlockSpec((None, L, HY_CT), lambda j, bi: (bi, 0, j)),
        out_shape=jax.ShapeDtypeStruct((b, L, c), BF16),
        scratch_shapes=[
            pltpu.VMEM((k1p * 2 * HY_N2, HY_CT), F32),
            pltpu.VMEM((HY_N2 * 2 * k1p, HY_CT), F32),
            pltpu.VMEM((L, HY_CT), F32),
            pltpu.VMEM((L, HY_CT), F32),
        ],
        compiler_params=pltpu.CompilerParams(dimension_semantics=("arbitrary", "arbitrary"), vmem_limit_bytes=VMEM_LIMIT_HYENA),
        name="hyena_long_conv",
    )(hy, hy, hy, h_spec, m1, m4, f3, f3i, bias.astype(F32))


HY_CTX_CT = 256


def _hy_ctx_tables(L):
    n = 2 * L
    kb = L + 1
    kp = -(-kb // LANES) * LANES
    ang = 2.0 * np.pi * ((np.arange(kb, dtype=np.float64)[:, None] * np.arange(L, dtype=np.float64)[None, :]) % n) / n
    fwd = np.zeros((2 * kp, L), np.float32)
    fwd[:kb] = np.cos(ang)
    fwd[kp:kp + kb] = -np.sin(ang)
    return jnp.asarray(fwd, BF16), jnp.asarray(fwd.T, BF16), kb, kp


def _hy_ctx_filter_kernel(fwd_ref, bwd_ref, f_ref, o_ref, *, kb, kp, n):
    fwd = fwd_ref[...]
    bwd = jnp.where(lax.broadcasted_iota(jnp.int32, fwd.shape, 0) == 0, 0.0, bwd_ref[...])
    norm = lax.rsqrt(jnp.sum(fwd * fwd, axis=0, keepdims=True) + jnp.sum(bwd * bwd, axis=0, keepdims=True) + EPS)
    hf = jnp.dot(f_ref[...], fwd.astype(BF16), preferred_element_type=F32)
    hb = jnp.dot(f_ref[...], bwd.astype(BF16), preferred_element_type=F32)
    row = lax.broadcasted_iota(jnp.int32, hf.shape, 0)
    imag = row >= kp
    k = jnp.where(imag, row - kp, row)
    wk = jnp.where(jnp.logical_or(k == 0, k == kb - 1), 1.0 / n, 2.0 / n)
    o_ref[...] = (hf + jnp.where(imag, -hb, hb)) * (wk * norm)


def _hy_ctx_kernel(v_ref, x1_ref, x2_ref, h_ref, f_ref, g_ref, bias_ref, o_ref, *, kp):
    u = v_ref[...].astype(F32)
    for order, gate_ref in enumerate((x1_ref, x2_ref)):
        x = jnp.dot(f_ref[...], u.astype(BF16), preferred_element_type=F32)
        h = h_ref[order]
        xr, xi, hr, hi = x[:kp], x[kp:], h[:kp], h[kp:]
        z = jnp.concatenate([xr * hr - xi * hi, xr * hi + xi * hr], axis=0).astype(BF16)
        y = jnp.dot(g_ref[...], z, preferred_element_type=F32)
        u = gate_ref[...].astype(F32) * (y + u * bias_ref[order:order + 1, :])
    o_ref[...] = u.astype(o_ref.dtype)


def _hyena_ctx(hy, taps, bias):
    b, L, _ = hy.shape
    c = hy.shape[2] // (HY_ORDER + 1)
    ct = HY_CTX_CT
    nct = c // ct
    f_mat, g_mat, kb, kp = _hy_ctx_tables(L)
    params = pltpu.CompilerParams(dimension_semantics=("arbitrary", "arbitrary"), vmem_limit_bytes=VMEM_LIMIT)
    h_spec = pl.pallas_call(
        functools.partial(_hy_ctx_filter_kernel, kb=kb, kp=kp, n=2 * L),
        grid=(HY_ORDER, nct),
        in_specs=[
            pl.BlockSpec((L, ct), lambda o, j: (0, o * nct + j)),
            pl.BlockSpec((L, ct), lambda o, j: (0, (HY_ORDER + o) * nct + j)),
            pl.BlockSpec((2 * kp, L), lambda o, j: (0, 0)),
        ],
        out_specs=pl.BlockSpec((None, 2 * kp, ct), lambda o, j: (o, 0, j)),
        out_shape=jax.ShapeDtypeStruct((HY_ORDER, 2 * kp, c), F32),
        compiler_params=params,
        name="hyena_ctx_filter",
    )(taps, taps, f_mat)

    def col(part):
        return pl.BlockSpec((None, L, ct), lambda j, bi: (bi, 0, part * nct + j))

    return pl.pallas_call(
        functools.partial(_hy_ctx_kernel, kp=kp),
        grid=(nct, b),
        in_specs=[
            col(0), col(1), col(2),
            pl.BlockSpec((HY_ORDER, 2 * kp, ct), lambda j, bi: (0, 0, j)),
            pl.BlockSpec((2 * kp, L), lambda j, bi: (0, 0)),
            pl.BlockSpec((L, 2 * kp), lambda j, bi: (0, 0)),
            pl.BlockSpec((HY_ORDER, ct), lambda j, bi: (0, j)),
        ],
        out_specs=pl.BlockSpec((None, L, ct), lambda j, bi: (bi, 0, j)),
        out_shape=jax.ShapeDtypeStruct((b, L, c), BF16),
        compiler_params=params,
        name="hyena_ctx_conv",
    )(hy, hy, hy, h_spec, f_mat, g_mat, bias.astype(F32))


def _split_cols(t, sizes):
    return jnp.split(t, np.cumsum(sizes)[:-1].tolist(), axis=-1)


def _rms_norm(x, w):
    xf = x.astype(F32)
    y = xf * lax.rsqrt(jnp.mean(xf * xf, axis=-1, keepdims=True) + EPS)
    return (y * w.astype(F32)).astype(x.dtype)


def _dwconv_centred(x, w, b):
    k = w.shape[0]
    y = lax.conv_general_dilated(x, w[:, None, :].astype(x.dtype), window_strides=(1,), padding=[(k // 2, k // 2)], dimension_numbers=('NWC', 'WIO', 'NWC'), feature_group_count=x.shape[-1], precision=lax.Precision.HIGHEST)
    return y + b.astype(x.dtype)


def _flip_seq(t):
    return jnp.flip(t, axis=1)


def _to_col_major(t, rows):
    b, rest = t.shape[0], t.shape[2:]
    return jnp.swapaxes(t.reshape((b, rows, GRID_W) + rest), 1, 2).reshape((b, rows * GRID_W) + rest)


def _from_col_major(t, rows):
    b, rest = t.shape[0], t.shape[2:]
    return jnp.swapaxes(t.reshape((b, GRID_W, rows) + rest), 1, 2).reshape((b, rows * GRID_W) + rest)


def _hyena_filter_hidden(L, w1, b1, w2, b2, freq):
    hp = lax.Precision.HIGHEST
    t = jnp.linspace(0.0, 1.0, L, dtype=F32)[:, None]
    w = 2.0 * math.pi * jnp.arange(L, dtype=F32)[:, None] / L
    bands = jnp.linspace(1e-4, HY_BANDS - 1, HY_BANDS, dtype=F32)
    feats = jnp.concatenate([t, jnp.cos(bands * w), -jnp.sin(bands * w)], axis=-1)
    h = jnp.sin(freq[0] * (jnp.dot(feats, w1, precision=hp) + b1))
    return jnp.sin(freq[1] * (jnp.dot(h, w2, precision=hp) + b2))


def _hyena_decay_rates():
    max_decay = math.log(HY_DECAY_TARGET) / HY_FAST_DECAY
    min_decay = math.log(HY_DECAY_TARGET) / HY_SLOW_DECAY
    return jnp.abs(jnp.linspace(min_decay, max_decay, HY_WIDTH, dtype=F32))


def _hyena_filter_taps(L, w1, b1, w2, b2, w3, b3, freq):
    h = jnp.dot(_hyena_filter_hidden(L, w1, b1, w2, b2, freq), w3, precision=lax.Precision.HIGHEST) + b3
    t = jnp.linspace(0.0, 1.0, L, dtype=F32)[:, None]
    return h * jnp.tile(jnp.exp(-t * _hyena_decay_rates()), (1, 2 * HY_ORDER))


def _mixer_branches(h_rm, w_in, lb, n_ctx, ssm_conv_w, ssm_conv_b, ssm_dt_bias, ssm_a_log, ssm_d, ssm_norm, hy_conv_w, hy_conv_b, hy_w1, hy_b1, hy_w2, hy_b2, hy_w3, hy_b3, hy_freq, hy_bias, hg_norm):
    b, t, d = h_rm.shape
    n_lat = t - n_ctx
    rows = n_lat // GRID_W
    w_z, w_xbc, w_dt, w_hy, w_q, w_f, w_i, w_g, w_gate = _split_cols(w_in, IN_SIZES)
    h2 = h_rm.reshape(b * t, d)
    rm_parts = (w_z, w_xbc, w_hy, w_g, w_gate)
    col_z, col_xbc, col_hy, col_g, col_gate = np.cumsum([0] + [w.shape[1] for w in rm_parts[:-1]]).tolist()
    p_rm = _mm(h2, jnp.concatenate(rm_parts, axis=1).astype(BF16), BF16).reshape(b, t, -1)

    zero_pad = jnp.zeros((d, LANES - SSM_HEADS), F32)
    w_dt2 = jnp.concatenate([w_dt[:, :SSM_HEADS], zero_pad, w_dt[:, SSM_HEADS:], zero_pad], axis=1)
    dt2 = _mm(h2, w_dt2.astype(BF16), F32).reshape(b, t, 2 * LANES)
    xbc_act = _dwconv_stream(p_rm, ssm_conv_w, ssm_conv_b, n_ctx, True, c0=col_xbc)
    ym = _ssd_scan(xbc_act, dt2, p_rm, col_z, ssm_dt_bias, ssm_a_log, ssm_d, ssm_norm, n_ctx)

    hy_ctx, hy_lat = _dwconv_stream(p_rm, hy_conv_w, hy_conv_b, n_ctx, False, c0=col_hy, split=True)
    taps_ctx = _hyena_filter_taps(n_ctx, hy_w1, hy_b1, hy_w2, hy_b2, hy_w3, hy_b3, hy_freq)
    yh_ctx = _hyena_ctx(hy_ctx, taps_ctx, hy_bias)
    h_spec = _hyena_filter_spectrum_pallas(_hyena_filter_hidden(n_lat, hy_w1, hy_b1, hy_w2, hy_b2, hy_freq), hy_w3, hy_b3, _hyena_decay_rates())
    yh_lat = _hyena_long(hy_lat, h_spec, hy_bias)
    yh = jnp.concatenate([yh_ctx, yh_lat], axis=1)

    h_cm = jnp.concatenate([h_rm[:, :n_ctx], _to_col_major(h_rm[:, n_ctx:], rows)], axis=1).reshape(b * t, d)

    p_cm = _mm(h_cm, jnp.concatenate([w_q, w_f, w_i], axis=1).astype(BF16), BF16).reshape(b, t, -1)
    og = _hgrn_scan(p_cm, lb, hg_norm, n_ctx)
    og = jnp.concatenate([og[:, :n_ctx], _from_col_major(og[:, n_ctx:], rows)], axis=1)
    return ym, yh, og, p_rm, col_g, col_gate


ROW_TILE = 256
MOD_ROWS = SUBLANES
M_SHIFT_MIX, M_SCALE_MIX, M_GATE_MIX, M_SHIFT_FFN, M_SCALE_FFN, M_GATE_FFN = range(6)
ROW_PARAMS = pltpu.CompilerParams(dimension_semantics=("arbitrary", "arbitrary"), vmem_limit_bytes=VMEM_LIMIT)


def _rms(x):
    return x * lax.rsqrt(jnp.mean(x * x, axis=-1, keepdims=True) + EPS)


def _mrow(m_ref, r):
    return m_ref[r:r + 1, :]


def _row_spec(width):
    return pl.BlockSpec((None, ROW_TILE, width), lambda bi, i: (bi, i, 0))


def _vec_spec(width):
    return pl.BlockSpec((1, width), lambda bi, i: (0, 0))


def _mat_spec(k, n):
    return pl.BlockSpec((k, n), lambda bi, i: (0, 0))


def _mod_spec(n_ctx):
    return pl.BlockSpec((None, None, MOD_ROWS, D_MODEL), lambda bi, i: (bi, jnp.where(i < n_ctx // ROW_TILE, 0, 1), 0, 0))


def _norm_mod_kernel(x_ref, w_ref, m_ref, o_ref):
    y = _rms(x_ref[...]) * w_ref[...]
    o_ref[...] = (y * (1.0 + _mrow(m_ref, M_SCALE_MIX)) + _mrow(m_ref, M_SHIFT_MIX)).astype(o_ref.dtype)


def _norm_mod(xs, w, mods, n_ctx):
    b, t, d = xs.shape
    return pl.pallas_call(
        _norm_mod_kernel,
        grid=(b, t // ROW_TILE),
        in_specs=[_row_spec(d), _vec_spec(d), _mod_spec(n_ctx)],
        out_specs=_row_spec(d),
        out_shape=jax.ShapeDtypeStruct((b, t, d), BF16),
        compiler_params=ROW_PARAMS,
        name="norm_mod",
    )(xs, w.reshape(1, d), mods)


def _merge_kernel(ym_ref, yh_ref, og_ref, g_ref, gm_ref, gh_ref, gg_ref, x_ref, m_ref, w1_ref, w2_ref, w3_ref, wo_ref, npost_ref, npre_ref, rw_ref, rb_ref, xo_ref, h_ref, lg_ref):
    def sig(ref):
        return jax.nn.sigmoid(ref[...].astype(F32))

    go = g_ref[...].astype(F32)
    yg = (og_ref[...].astype(F32) * (go * jax.nn.sigmoid(go))).astype(BF16)
    merged = sig(gm_ref) * jnp.dot(ym_ref[...], w1_ref[...], preferred_element_type=F32)
    merged = merged + sig(gh_ref) * jnp.dot(yh_ref[...], w2_ref[...], preferred_element_type=F32)
    merged = merged + sig(gg_ref) * jnp.dot(yg, w3_ref[...], preferred_element_type=F32)
    mix = jnp.dot(merged.astype(BF16), wo_ref[...], preferred_element_type=F32)
    x = x_ref[...] + _mrow(m_ref, M_GATE_MIX) * (_rms(mix) * npost_ref[...])
    xo_ref[...] = x
    h = (_rms(x) * npre_ref[...] * (1.0 + _mrow(m_ref, M_SCALE_FFN)) + _mrow(m_ref, M_SHIFT_FFN)).astype(BF16)
    h_ref[...] = h
    lg_ref[...] = jnp.dot(h, rw_ref[...], preferred_element_type=F32) + rb_ref[...]


def _merge(ym, yh, og, p_rm, col_g, col_gate, xs, mods, w_br_ssm, w_br_hy, w_br_hg, w_out, norm_post, norm_ffn_pre, router_w, router_b, n_ctx):
    b, t, d = xs.shape
    rw = jnp.pad(router_w, ((0, 0), (0, LANES - N_EXPERTS))).astype(BF16)
    rb = jnp.pad(router_b, (0, LANES - N_EXPERTS)).reshape(1, LANES).astype(F32)

    def col_spec(col):
        return pl.BlockSpec((None, ROW_TILE, d), lambda bi, i: (bi, i, col // d))

    return pl.pallas_call(
        _merge_kernel,
        grid=(b, t // ROW_TILE),
        in_specs=[_row_spec(d), _row_spec(d), _row_spec(d), col_spec(col_g), col_spec(col_gate), col_spec(col_gate + d), col_spec(col_gate + 2 * d), _row_spec(d), _mod_spec(n_ctx),
                  _mat_spec(d, d), _mat_spec(d, d), _mat_spec(d, d), _mat_spec(d, d), _vec_spec(d), _vec_spec(d), _mat_spec(d, LANES), _vec_spec(LANES)],
        out_specs=[_row_spec(d), _row_spec(d), _row_spec(LANES)],
        out_shape=[jax.ShapeDtypeStruct((b, t, d), F32), jax.ShapeDtypeStruct((b, t, d), BF16), jax.ShapeDtypeStruct((b, t, LANES), F32)],
        compiler_params=ROW_PARAMS,
        name="branch_merge",
    )(ym, yh, og, p_rm, p_rm, p_rm, p_rm, xs, mods, w_br_ssm.astype(BF16), w_br_hy.astype(BF16), w_br_hg.astype(BF16), w_out.astype(BF16),
      norm_post.reshape(1, d), norm_ffn_pre.reshape(1, d), rw, rb)


def _post_ffn_kernel(y0_ref, y1_ref, y2_ref, y3_ref, x_ref, m_ref, w_ref, o_ref):
    f = y0_ref[...].astype(F32) + y1_ref[...].astype(F32) + y2_ref[...].astype(F32) + y3_ref[...].astype(F32)
    o_ref[...] = x_ref[...] + _mrow(m_ref, M_GATE_FFN) * (_rms(f) * w_ref[...])


def _post_ffn(f4, xs, mods, norm_post, n_ctx, skip):
    b, _, d = xs.shape
    t = f4.shape[2]
    i0 = skip // ROW_TILE

    def k_spec(k):
        return pl.BlockSpec((None, None, ROW_TILE, d), lambda bi, i: (k, bi, i, 0))

    return pl.pallas_call(
        _post_ffn_kernel,
        grid=(b, t // ROW_TILE),
        in_specs=[k_spec(k) for k in range(TOP_K)] + [
            pl.BlockSpec((None, ROW_TILE, d), lambda bi, i: (bi, i0 + i, 0)),
            pl.BlockSpec((None, None, MOD_ROWS, d), lambda bi, i: (bi, jnp.where(i0 + i < n_ctx // ROW_TILE, 0, 1), 0, 0)),
            _vec_spec(d)],
        out_specs=_row_spec(d),
        out_shape=jax.ShapeDtypeStruct((b, t, d), F32),
        compiler_params=ROW_PARAMS,
        name="post_ffn",
    )(f4, f4, f4, f4, xs, mods, norm_post.reshape(1, d))


def _moe_ffn(h2, logits, t_per_b, skip, li, w1, b1, w2, b2):
    t = logits.shape[0]
    d = h2.shape[1]
    n = t * TOP_K
    n_tiles = n // MOE_BLOCK
    top_v, top_e = lax.top_k(logits, TOP_K)
    gate_w = jax.nn.softmax(top_v, axis=-1)
    flat_e = top_e.reshape(n).astype(jnp.int32)
    iota = jnp.arange(n, dtype=jnp.int32)
    _, order, sw = lax.sort((flat_e, iota, gate_w.reshape(n)), num_keys=1, is_stable=True)
    _, inv = lax.sort((order, iota), num_keys=1)
    tok = order // TOP_K
    xs = h2[tok + (tok // t_per_b + 1) * skip]
    counts = jnp.sum((flat_e[:, None] == jnp.arange(N_EXPERTS, dtype=jnp.int32)[None, :]).astype(jnp.int32), axis=0)
    end = jnp.cumsum(counts)
    start = end - counts
    first_tile = start // MOE_BLOCK
    n_items = jnp.where(counts > 0, (end - 1) // MOE_BLOCK - first_tile + 1, 0)
    items_end = jnp.cumsum(n_items)
    w = jnp.arange(n_tiles + N_EXPERTS, dtype=jnp.int32)
    valid = w < items_end[-1]
    e_w = jnp.minimum(jnp.sum((w[:, None] >= items_end[None, :]).astype(jnp.int32), axis=1), N_EXPERTS - 1)
    tile_w = first_tile[e_w] + (w - (items_end[e_w] - n_items[e_w]))
    lo = jnp.where(valid, jnp.maximum(start[e_w], tile_w * MOE_BLOCK), 0)
    hi = jnp.where(valid, jnp.minimum(end[e_w], (tile_w + 1) * MOE_BLOCK), 0)
    tile_w = jnp.where(valid, tile_w, n_tiles - 1)
    first = jnp.concatenate([jnp.ones((1,), jnp.int32), (tile_w[1:] != tile_w[:-1]).astype(jnp.int32)])
    newexp = jnp.concatenate([jnp.ones((1,), jnp.int32), (e_w[1:] != e_w[:-1]).astype(jnp.int32)])
    ys = _moe_experts(xs, sw, tile_w.astype(jnp.int32), e_w, lo.astype(jnp.int32), hi.astype(jnp.int32), first, newexp, li, w1, b1, w2, b2)
    return ys[inv.reshape(t, TOP_K).T]


def kernel(x, c, ctx, c_ctx, w_mod, b_mod, norm_mix_pre, norm_mix_post, norm_ffn_pre, norm_ffn_post, w_in, ssm_conv_w, ssm_conv_b, ssm_dt_bias, ssm_a_log, ssm_d, ssm_norm, hy_conv_w, hy_conv_b, hy_w1, hy_b1, hy_w2, hy_b2, hy_w3, hy_b3, hy_freq, hy_bias, hg_lb_logits, hg_norm, w_br_ssm, w_br_hy, w_br_hg, w_out, router_w, router_b, exp_w1, exp_b1, exp_w2, exp_b2):
    hp = lax.Precision.HIGHEST
    b, n_lat, d = x.shape
    n_ctx = ctx.shape[1]
    lb = jax.nn.softmax(hg_lb_logits.astype(F32), axis=1)
    lb = jnp.cumsum(lb, axis=1) - lb[:, :1]
    silu_c = jax.nn.silu(c)
    silu_cc = jax.nn.silu(c_ctx)
    xs = jnp.concatenate([ctx, x], axis=1)
    for li in range(DEPTH):
        mx = (jnp.dot(silu_c, w_mod[li], precision=hp) + b_mod[li]).reshape(b, 1, 6, d)
        mc = jnp.broadcast_to((jnp.dot(silu_cc, w_mod[li], precision=hp) + b_mod[li]).reshape(1, 1, 6, d), (b, 1, 6, d))
        mods = jnp.pad(jnp.concatenate([mc, mx], axis=1), ((0, 0), (0, 0), (0, MOD_ROWS - 6), (0, 0)))
        h = _norm_mod(xs, norm_mix_pre[li], mods, n_ctx)
        ym, yh, og, p_rm, col_g, col_gate = _mixer_branches(h, w_in[li], lb[:, li], n_ctx, ssm_conv_w[li], ssm_conv_b[li], ssm_dt_bias[li], ssm_a_log[li], ssm_d[li], ssm_norm[li], hy_conv_w[li], hy_conv_b[li], hy_w1[li], hy_b1[li], hy_w2[li], hy_b2[li], hy_w3[li], hy_b3[li], hy_freq[li], hy_bias[li], hg_norm[li])
        xs, h_ffn, logits = _merge(ym, yh, og, p_rm, col_g, col_gate, xs, mods, w_br_ssm[li], w_br_hy[li], w_br_hg[li], w_out[li], norm_mix_post[li], norm_ffn_pre[li], router_w[li], router_b[li], n_ctx)
        skip = n_ctx if li == DEPTH - 1 else 0
        t = xs.shape[1] - skip
        f4 = _moe_ffn(h_ffn.reshape(-1, d), logits[:, skip:, :N_EXPERTS].reshape(b * t, N_EXPERTS), t, skip, li, exp_w1, exp_b1[li], exp_w2, exp_b2[li])
        xs = _post_ffn(f4.reshape(TOP_K, b, t, d), xs, mods, norm_ffn_post[li], n_ctx, skip)
    return xs
```

```python
import functools
import math

import jax
import jax.numpy as jnp
import numpy as np
from jax import lax
from jax.experimental import pallas as pl
from jax.experimental.pallas import tpu as pltpu

D_MODEL = 1024
DEPTH = 2
GRID_W = 64

SSM_HEADS = 16
SSM_HEAD_DIM = 64
SSM_INNER = SSM_HEADS * SSM_HEAD_DIM
SSM_STATE = 128
SSM_GROUPS = 4
SSD_CHUNK = 128
SSM_XBC = SSM_INNER + 2 * SSM_GROUPS * SSM_STATE

HY_WIDTH = D_MODEL
HY_ORDER = 2
HY_BANDS = 16
HY_FAST_DECAY = 0.3
HY_SLOW_DECAY = 1.5
HY_DECAY_TARGET = 1e-2

HG_HEADS = 8
HG_KDIM = 128
HG_VDIM = D_MODEL // HG_HEADS
HG_QK = HG_HEADS * HG_KDIM
HG_V = HG_HEADS * HG_VDIM
HG_CHUNK = 64
F_FLOOR = 1e-20

N_EXPERTS = 32
TOP_K = 4
D_FF = D_MODEL
SWIGLU_LIMIT = 7.0
SWIGLU_ALPHA = 1.702
MOE_BLOCK = 512

N_BRANCHES = 3
IN_SIZES = (SSM_INNER, SSM_XBC, 2 * SSM_HEADS, (HY_ORDER + 1) * HY_WIDTH, HG_QK, 2 * HG_QK, HG_V, HG_V, N_BRANCHES * D_MODEL)
EPS = 1e-6
F32 = jnp.float32
BF16 = jnp.bfloat16

LANES = 128
VMEM_LIMIT = 56 * 1024 * 1024


def _mm_kernel(a_ref, b_ref, o_ref):
    o_ref[...] = jnp.dot(a_ref[...], b_ref[...], preferred_element_type=F32).astype(o_ref.dtype)


def _mm(a, b, out_dtype=F32, tm=1024, tn=2048):
    m, k = a.shape
    n = b.shape[1]
    tm = math.gcd(m, tm)
    tn = math.gcd(n, tn)
    assert tm % SUBLANES == 0 and tn % LANES == 0, (m, n, tm, tn)
    return pl.pallas_call(
        _mm_kernel,
        grid=(n // tn, m // tm),
        in_specs=[pl.BlockSpec((tm, k), lambda j, i: (i, 0)), pl.BlockSpec((k, tn), lambda j, i: (0, j))],
        out_specs=pl.BlockSpec((tm, tn), lambda j, i: (i, j)),
        out_shape=jax.ShapeDtypeStruct((m, n), out_dtype),
        compiler_params=pltpu.CompilerParams(dimension_semantics=("arbitrary", "arbitrary"), vmem_limit_bytes=VMEM_LIMIT),
        name="dense_mm",
    )(a, b)


def _moe_kernel(tile_ref, exp_ref, lo_ref, hi_ref, first_ref, newexp_ref, x_ref, sw_ref, w1_ref, b1_ref, w2_ref, b2_ref, o_ref, w1b_ref, w2b_ref):
    del exp_ref
    w = pl.program_id(0)
    lo, hi = lo_ref[w], hi_ref[w]

    @pl.when(newexp_ref[w] == 1)
    def _():
        def cast_rows(i, carry):
            rows = pl.ds(pl.multiple_of(i * LANES, LANES), LANES)
            w1b_ref[rows, :] = w1_ref[rows, :].astype(BF16)
            w2b_ref[rows, :] = w2_ref[rows, :].astype(BF16)
            return carry

        lax.fori_loop(0, D_MODEL // LANES, cast_rows, 0)

    @pl.when(hi > lo)
    def _():
        hh = jnp.dot(x_ref[...], w1b_ref[...], preferred_element_type=F32) + b1_ref[...]
        g = jnp.minimum(hh[:, :D_FF], SWIGLU_LIMIT)
        u = jnp.clip(hh[:, D_FF:], -SWIGLU_LIMIT, SWIGLU_LIMIT)
        act = (u + 1.0) * g * jax.nn.sigmoid(SWIGLU_ALPHA * g)
        y = jnp.dot(act.astype(BF16), w2b_ref[...], preferred_element_type=F32) + b2_ref[...]
        y = (y * sw_ref[...]).astype(o_ref.dtype)
        rows = tile_ref[w] * MOE_BLOCK + lax.broadcasted_iota(jnp.int32, (MOE_BLOCK, 1), 0)
        mine = jnp.logical_and(rows >= lo, rows < hi)

        @pl.when(first_ref[w] == 1)
        def _():
            o_ref[...] = jnp.where(mine, y, jnp.zeros_like(y))

        @pl.when(first_ref[w] != 1)
        def _():
            o_ref[...] = jnp.where(mine, y, o_ref[...])


def _moe_experts(xs, sw, tile_w, exp_w, lo, hi, first, newexp, li, w1, b1, w2, b2):
    n, d = xs.shape
    grid_spec = pltpu.PrefetchScalarGridSpec(
        num_scalar_prefetch=6,
        grid=(tile_w.shape[0],),
        in_specs=[
            pl.BlockSpec((MOE_BLOCK, d), lambda w, tl, ex, lo_, hi_, fi, ne: (tl[w], 0)),
            pl.BlockSpec((MOE_BLOCK, 1), lambda w, tl, ex, lo_, hi_, fi, ne: (tl[w], 0)),
            pl.BlockSpec((None, None, d, 2 * D_FF), lambda w, tl, ex, lo_, hi_, fi, ne: (li, ex[w], 0, 0)),
            pl.BlockSpec((None, 1, 2 * D_FF), lambda w, tl, ex, lo_, hi_, fi, ne: (ex[w], 0, 0)),
            pl.BlockSpec((None, None, D_FF, d), lambda w, tl, ex, lo_, hi_, fi, ne: (li, ex[w], 0, 0)),
            pl.BlockSpec((None, 1, d), lambda w, tl, ex, lo_, hi_, fi, ne: (ex[w], 0, 0)),
        ],
        out_specs=pl.BlockSpec((MOE_BLOCK, d), lambda w, tl, ex, lo_, hi_, fi, ne: (tl[w], 0)),
        scratch_shapes=[pltpu.VMEM((d, 2 * D_FF), BF16), pltpu.VMEM((D_FF, d), BF16)],
    )
    return pl.pallas_call(
        _moe_kernel,
        grid_spec=grid_spec,
        out_shape=jax.ShapeDtypeStruct((n, d), BF16),
        compiler_params=pltpu.CompilerParams(dimension_semantics=("arbitrary",), vmem_limit_bytes=VMEM_LIMIT),
        name="moe_experts",
    )(tile_w, exp_w, lo, hi, first, newexp, xs, sw.reshape(n, 1), w1, b1.reshape(N_EXPERTS, 1, 2 * D_FF), w2, b2.reshape(N_EXPERTS, 1, d))


SUBLANES = 8
NEG_BIG = -1e30
HG_STEP_CHUNKS = 4
HIER_LEVELS = (64, 32, 16, 8, 4)
LOG2E = math.log2(math.e)


def _split3(x):
    h1 = x.astype(BF16)
    r1 = x - h1.astype(F32)
    h2 = r1.astype(BF16)
    h3 = (r1 - h2.astype(F32)).astype(BF16)
    return h1, h2, h3


def _dot_nt(a, b):
    return lax.dot_general(a, b, (((1,), (1,)), ((), ())), preferred_element_type=F32)


def _gla_kernel(q_ref, a_ref, v_ref, lb_ref, *rest, reverse):
    if reverse:
        of_ref, w_ref, o_ref, st_ref, at_ref = rest
    else:
        o_ref, st_ref, at_ref = rest
    Q = HG_CHUNK

    @pl.when(pl.program_id(1) == 0)
    def _():
        st_ref[...] = jnp.zeros_like(st_ref)

    row = lax.broadcasted_iota(jnp.int32, (Q, Q), 0)
    col = lax.broadcasted_iota(jnp.int32, (Q, Q), 1)
    tri = jnp.where((col >= row) if reverse else (col <= row), 1.0, 0.0).astype(BF16)
    same_block = {s: (row // s) == (col // s) for s in HIER_LEVELS[1:] + (2, 1)}
    rowk = lax.broadcasted_iota(jnp.int32, (Q, HG_KDIM), 0)
    q_rows = {s: ((rowk % s) < s // 2) if reverse else ((rowk % s) >= s // 2) for s in HIER_LEVELS + (2,)}
    of_w = (of_ref, w_ref) if reverse else (None, None)
    for sub in (range(HG_STEP_CHUNKS - 1, -1, -1) if reverse else range(HG_STEP_CHUNKS)):
        _gla_chunk(sub, q_ref, a_ref, v_ref, lb_ref[...], of_w, o_ref, st_ref, at_ref, tri, same_block, q_rows, reverse)


def _gla_chunk(sub, q_ref, a_ref, v_ref, lb, of_w, o_ref, st_ref, at_ref, tri, same_block, q_rows, reverse):
    Q = HG_CHUNK
    rs = slice(sub * Q, (sub + 1) * Q)
    of_ref, w_ref = of_w
    shp3 = (Q // SUBLANES, SUBLANES, HG_KDIM)
    sub3 = lax.broadcasted_iota(jnp.int32, shp3, 1)
    a = a_ref[rs, :].astype(F32)
    f_all = jnp.maximum(lb + (1.0 - lb) * jax.nn.sigmoid(a), F_FLOOR)
    kk = (1.0 - lb) * jax.nn.sigmoid(-a)
    g_all = sum(jnp.dot(tri, p, preferred_element_type=F32) for p in _split3(jnp.log(f_all) * LOG2E))
    q_all = q_ref[rs, :].astype(F32)
    q_all = q_all * jax.nn.sigmoid(q_all)
    v_all = v_ref[rs, :]
    tot = 0 if reverse else Q - 1

    def level_ref(g, s):
        half = s // 2
        m_off = half if reverse else half - 1
        if s >= 2 * SUBLANES:
            return jnp.concatenate([jnp.broadcast_to(g[b0 + m_off:b0 + m_off + 1, :], (s, HG_KDIM)) for b0 in range(0, Q, s)], axis=0)
        g3 = g.reshape(shp3)
        ref = jnp.broadcast_to(g3[:, m_off:m_off + 1, :], shp3)
        for b0 in range(s, SUBLANES, s):
            ref = jnp.where(sub3 >= b0, jnp.broadcast_to(g3[:, b0 + m_off:b0 + m_off + 1, :], shp3), ref)
        return ref.reshape(Q, HG_KDIM)

    for h in range(HG_HEADS):
        sl = slice(h * HG_KDIM, (h + 1) * HG_KDIM)
        g, qh, kh, fh = g_all[:, sl], q_all[:, sl], kk[:, sl], f_all[:, sl]
        attn = None
        for s in HIER_LEVELS:
            gref = level_ref(g, s)
            eq = jnp.exp2(jnp.where(q_rows[s], g - gref, NEG_BIG))
            ek = jnp.exp2(jnp.where(q_rows[s], NEG_BIG, gref - g))
            lvl = _dot_nt((qh * eq).astype(BF16), (kh * ek).astype(BF16))
            attn = lvl if attn is None else jnp.where(same_block[s], lvl, attn)
        lvl = _dot_nt(jnp.where(q_rows[2], qh * fh, 0.0).astype(BF16), jnp.where(q_rows[2], 0.0, kh).astype(BF16))
        attn = jnp.where(same_block[2], lvl, attn)
        attn = jnp.where(same_block[1], _dot_nt(qh.astype(BF16), kh.astype(BF16)), attn)
        at_ref[sub, h] = attn.astype(BF16)

    for h in range(HG_HEADS):
        sl = slice(h * HG_KDIM, (h + 1) * HG_KDIM)
        g, qh, kh, vb = g_all[:, sl], q_all[:, sl], kk[:, sl], v_all[:, sl]
        g_tot = g[tot:tot + 1, :]
        st = st_ref[h]
        o = _dot_nt((qh * jnp.exp2(g)).astype(BF16), st.astype(BF16))
        o = o + jnp.dot(at_ref[sub, h], vb, preferred_element_type=F32)
        k_st = (kh * jnp.exp2(g_tot - g)).astype(BF16)
        st_ref[h] = st * jnp.exp2(g_tot) + jnp.dot(vb.astype(F32).T.astype(BF16), k_st, preferred_element_type=F32)
        if reverse:
            o = o + of_ref[rs, sl]
            o = o * lax.rsqrt(jnp.mean(o * o, axis=-1, keepdims=True) + EPS) * w_ref[:, sl]
        o_ref[rs, sl] = o.astype(o_ref.dtype)


def _hgrn_scan(p_cm, lb, norm_w, n_ctx):
    b, t, _ = p_cm.shape
    col_q, col_f, col_i = 0, 1, 3
    step_rows = HG_STEP_CHUNKS * HG_CHUNK
    assert n_ctx % step_rows == 0 and t % step_rows == 0, (n_ctx, t, step_rows)
    nc, ncc = t // step_rows, n_ctx // step_rows
    blk = (None, step_rows, HG_QK)
    scratch = [pltpu.VMEM((HG_HEADS, HG_VDIM, HG_KDIM), F32), pltpu.VMEM((HG_STEP_CHUNKS, HG_HEADS, HG_CHUNK, HG_CHUNK), BF16)]
    params = pltpu.CompilerParams(dimension_semantics=("arbitrary", "arbitrary"), vmem_limit_bytes=VMEM_LIMIT)
    row_spec = pl.BlockSpec((1, HG_QK), lambda bi, s: (0, 0))

    def fwd_chunk(s):
        return s

    def bwd_chunk(s):
        return jnp.where(s < ncc, ncc - 1 - s, nc + ncc - 1 - s)

    o_f = pl.pallas_call(
        functools.partial(_gla_kernel, reverse=False),
        grid=(b, nc),
        in_specs=[
            pl.BlockSpec(blk, lambda bi, s: (bi, fwd_chunk(s), col_q)),
            pl.BlockSpec(blk, lambda bi, s: (bi, fwd_chunk(s), col_f)),
            pl.BlockSpec(blk, lambda bi, s: (bi, fwd_chunk(s), col_i)),
            row_spec,
        ],
        out_specs=pl.BlockSpec(blk, lambda bi, s: (bi, fwd_chunk(s), 0)),
        out_shape=jax.ShapeDtypeStruct((b, t, HG_V), F32),
        scratch_shapes=scratch,
        compiler_params=params,
        name="gla_fwd",
    )(p_cm, p_cm, p_cm, lb[0:1])
    return pl.pallas_call(
        functools.partial(_gla_kernel, reverse=True),
        grid=(b, nc),
        in_specs=[
            pl.BlockSpec(blk, lambda bi, s: (bi, bwd_chunk(s), col_q)),
            pl.BlockSpec(blk, lambda bi, s: (bi, bwd_chunk(s), col_f + 1)),
            pl.BlockSpec(blk, lambda bi, s: (bi, bwd_chunk(s), col_i)),
            row_spec,
            pl.BlockSpec(blk, lambda bi, s: (bi, bwd_chunk(s), 0)),
            row_spec,
        ],
        out_specs=pl.BlockSpec(blk, lambda bi, s: (bi, bwd_chunk(s), 0)),
        out_shape=jax.ShapeDtypeStruct((b, t, HG_V), BF16),
        scratch_shapes=scratch,
        compiler_params=params,
        name="gla_bwd",
    )(p_cm, p_cm, p_cm, lb[1:2], o_f, norm_w.reshape(1, HG_V))


CONV_TILE = 256
HALO = 16


def _dwconv_kernel(prev_ref, cur_ref, next_ref, w_ref, b_ref, *o_refs, taps, n_ctx_tiles, n_tiles, silu):
    i = pl.program_id(2)
    first = jnp.logical_or(i == 0, i == n_ctx_tiles)
    last = jnp.logical_or(i == n_ctx_tiles - 1, i == n_tiles - 1)
    pad = taps // 2
    xp = jnp.where(first, 0.0, prev_ref[...].astype(F32))
    xn = jnp.where(last, 0.0, next_ref[...].astype(F32))
    xcat = jnp.concatenate([xp, cur_ref[...].astype(F32), xn], axis=0)
    acc = jnp.broadcast_to(b_ref[...], cur_ref.shape).astype(F32)
    for k in range(taps):
        off = HALO - pad + k
        acc = acc + w_ref[k:k + 1, :] * xcat[off:off + CONV_TILE, :]
    if silu:
        acc = acc * jax.nn.sigmoid(acc)
    if len(o_refs) == 1:
        o_refs[0][...] = acc.astype(o_refs[0].dtype)
    else:
        ctx_ref, lat_ref = o_refs

        @pl.when(i < n_ctx_tiles)
        def _():
            ctx_ref[...] = acc.astype(ctx_ref.dtype)

        @pl.when(i >= n_ctx_tiles)
        def _():
            lat_ref[...] = acc.astype(lat_ref.dtype)


def _dwconv_stream(x, w, bias, n_ctx, silu, c0=0, ct=1024, split=False):
    b, t, _ = x.shape
    taps, c = w.shape
    n_tiles = t // CONV_TILE
    nct = n_ctx // CONV_TILE
    hb = CONV_TILE // HALO
    n_halo = t // HALO
    j0 = c0 // ct
    kern = functools.partial(_dwconv_kernel, taps=taps, n_ctx_tiles=nct, n_tiles=n_tiles, silu=silu)
    if split:
        out_specs = [pl.BlockSpec((None, CONV_TILE, ct), lambda bi, j, i: (bi, jnp.minimum(i, nct - 1), j)),
                     pl.BlockSpec((None, CONV_TILE, ct), lambda bi, j, i: (bi, jnp.maximum(i - nct, 0), j))]
        out_shape = [jax.ShapeDtypeStruct((b, n_ctx, c), BF16), jax.ShapeDtypeStruct((b, t - n_ctx, c), BF16)]
    else:
        out_specs = pl.BlockSpec((None, CONV_TILE, ct), lambda bi, j, i: (bi, i, j))
        out_shape = jax.ShapeDtypeStruct((b, t, c), BF16)
    return pl.pallas_call(
        kern,
        grid=(b, c // ct, n_tiles),
        in_specs=[
            pl.BlockSpec((None, HALO, ct), lambda bi, j, i: (bi, jnp.maximum(i * hb - 1, 0), j0 + j)),
            pl.BlockSpec((None, CONV_TILE, ct), lambda bi, j, i: (bi, i, j0 + j)),
            pl.BlockSpec((None, HALO, ct), lambda bi, j, i: (bi, jnp.minimum((i + 1) * hb, n_halo - 1), j0 + j)),
            pl.BlockSpec((taps, ct), lambda bi, j, i: (0, j)),
            pl.BlockSpec((1, ct), lambda bi, j, i: (0, j)),
        ],
        out_specs=out_specs,
        out_shape=out_shape,
        compiler_params=pltpu.CompilerParams(dimension_semantics=("arbitrary",) * 3, vmem_limit_bytes=VMEM_LIMIT),
        name="dwconv",
    )(x, x, x, w.astype(F32), bias.reshape(1, c).astype(F32))


SSD_STEP_CHUNKS = 2
SSM_GHEADS = SSM_HEADS // SSM_GROUPS
SSM_GP = SSM_GHEADS * SSM_HEAD_DIM


def _ssd_kernel(xbc_ref, dt_ref, dtb_ref, a_ref, *rest, reverse):
    if reverse:
        yf_ref, z_ref, dsk_ref, nw_ref, o_ref, st_ref, m_ref, xd_ref, xst_ref, y_ref = rest
    else:
        o_ref, st_ref, m_ref, xd_ref, xst_ref, y_ref = rest
    Q = SSD_CHUNK

    @pl.when(pl.program_id(1) == 0)
    def _():
        st_ref[...] = jnp.zeros_like(st_ref)

    row = lax.broadcasted_iota(jnp.int32, (Q, Q), 0)
    col = lax.broadcasted_iota(jnp.int32, (Q, Q), 1)
    keep = (col >= row) if reverse else (col <= row)
    tri = jnp.where(keep, 1.0, 0.0).astype(BF16)
    expand = jnp.where(lax.broadcasted_iota(jnp.int32, (LANES, SSM_INNER), 1) // SSM_HEAD_DIM == lax.broadcasted_iota(jnp.int32, (LANES, SSM_INNER), 0), 1.0, 0.0).astype(BF16)

    rev_refs = (yf_ref, z_ref, dsk_ref, nw_ref) if reverse else None
    for sub in (range(SSD_STEP_CHUNKS - 1, -1, -1) if reverse else range(SSD_STEP_CHUNKS)):
        _ssd_chunk(sub, xbc_ref, dt_ref, dtb_ref[...], a_ref[...], rev_refs, o_ref, st_ref, m_ref, xd_ref, xst_ref, y_ref, keep, tri, expand, reverse)


def _ssd_chunk(sub, xbc_ref, dt_ref, dt_bias, a_neg, rev_refs, o_ref, st_ref, m_ref, xd_ref, xst_ref, y_ref, keep, tri, expand, reverse):
    Q = SSD_CHUNK
    rs = slice(sub * Q, (sub + 1) * Q)
    dt = jax.nn.softplus(dt_ref[rs, :] + dt_bias)
    a = dt * a_neg
    cs = sum(jnp.dot(tri, p, preferred_element_type=F32) for p in _split3(a))
    tot = 0 if reverse else Q - 1
    cs_tot = cs[tot:tot + 1, :]
    cs_t = cs.T
    dt_e = jnp.dot(dt.astype(BF16), expand, preferred_element_type=F32)
    e_in = jnp.dot(jnp.exp(cs).astype(BF16), expand, preferred_element_type=F32)
    e_st = jnp.dot(jnp.exp(cs_tot - cs).astype(BF16), expand, preferred_element_type=F32)
    e_tot = jnp.dot(jnp.broadcast_to(jnp.exp(cs_tot), (SUBLANES, LANES)).astype(BF16), expand, preferred_element_type=F32)[0:1, :]

    xs = xbc_ref[rs, :SSM_INNER].astype(F32)
    xd_ref[sub] = (xs * dt_e).astype(BF16)
    xst_ref[sub] = (xs * dt_e * e_st).astype(BF16)
    for g in range(SSM_GROUPS):
        bm = xbc_ref[rs, SSM_INNER + g * SSM_STATE:SSM_INNER + (g + 1) * SSM_STATE]
        cm = xbc_ref[rs, SSM_INNER + (SSM_GROUPS + g) * SSM_STATE:SSM_INNER + (SSM_GROUPS + g + 1) * SSM_STATE]
        cb = _dot_nt(cm, bm)
        for h in range(g * SSM_GHEADS, (g + 1) * SSM_GHEADS):
            diff = jnp.broadcast_to(cs[:, h:h + 1], (Q, Q)) - jnp.broadcast_to(cs_t[h:h + 1, :], (Q, Q))
            m_ref[sub, h] = (cb * jnp.exp(jnp.where(keep, diff, NEG_BIG))).astype(BF16)

    lane_lo = lax.broadcasted_iota(jnp.int32, (Q, LANES), 1) < SSM_HEAD_DIM
    for g in range(SSM_GROUPS):
        bm = xbc_ref[rs, SSM_INNER + g * SSM_STATE:SSM_INNER + (g + 1) * SSM_STATE]
        cm = xbc_ref[rs, SSM_INNER + (SSM_GROUPS + g) * SSM_STATE:SSM_INNER + (SSM_GROUPS + g + 1) * SSM_STATE]
        gl = slice(g * SSM_GP, (g + 1) * SSM_GP)
        st = st_ref[g]
        y_off = jnp.dot(cm, st.astype(BF16), preferred_element_type=F32) * e_in[:, gl]
        for hp in range(SSM_GHEADS // 2):
            h0 = g * SSM_GHEADS + 2 * hp
            lanes = slice((h0 // 2) * LANES, (h0 // 2 + 1) * LANES)
            pair = [jnp.dot(m_ref[sub, h], xd_ref[sub, :, lanes], preferred_element_type=F32) for h in (h0, h0 + 1)]
            y_ref[sub, :, lanes] = jnp.where(lane_lo, pair[0], pair[1]) + y_off[:, hp * LANES:(hp + 1) * LANES]
        st_ref[g] = st * e_tot[:, gl] + jnp.dot(bm.astype(F32).T.astype(BF16), xst_ref[sub, :, gl], preferred_element_type=F32)
    y = y_ref[sub]
    if reverse:
        yf_ref, z_ref, dsk_ref, nw_ref = rev_refs
        y = (y + yf_ref[rs, :] + xbc_ref[rs, :SSM_INNER].astype(F32) * dsk_ref[...])
        zz = z_ref[rs, :].astype(F32)
        y = y * (zz * jax.nn.sigmoid(zz))
        y = y * lax.rsqrt(jnp.mean(y * y, axis=-1, keepdims=True) + EPS) * nw_ref[...]
    o_ref[rs, :] = y.astype(o_ref.dtype)


def _ssd_scan(xbc_act, dt2, p_rm, z_col, dt_bias, a_log, d_skip, norm_w, n_ctx):
    b, t, _ = xbc_act.shape
    step_rows = SSD_STEP_CHUNKS * SSD_CHUNK
    assert n_ctx % step_rows == 0 and t % step_rows == 0, (n_ctx, t, step_rows)
    nc, ncc = t // step_rows, n_ctx // step_rows

    def pad_heads(v):
        return jnp.pad(v.astype(F32), ((0, 0), (0, LANES - SSM_HEADS)))

    dtb = pad_heads(dt_bias.reshape(2, SSM_HEADS))
    a_neg = pad_heads(-jnp.exp(a_log.astype(F32)))
    dsk = jnp.repeat(d_skip.astype(F32), SSM_HEAD_DIM).reshape(1, SSM_INNER)
    scratch = [pltpu.VMEM((SSM_GROUPS, SSM_STATE, SSM_GP), F32), pltpu.VMEM((SSD_STEP_CHUNKS, SSM_HEADS, SSD_CHUNK, SSD_CHUNK), BF16),
               pltpu.VMEM((SSD_STEP_CHUNKS, SSD_CHUNK, SSM_INNER), BF16), pltpu.VMEM((SSD_STEP_CHUNKS, SSD_CHUNK, SSM_INNER), BF16),
               pltpu.VMEM((SSD_STEP_CHUNKS, SSD_CHUNK, SSM_INNER), F32)]
    params = pltpu.CompilerParams(dimension_semantics=("arbitrary", "arbitrary"), vmem_limit_bytes=VMEM_LIMIT)

    def bwd_chunk(s):
        return jnp.where(s < ncc, ncc - 1 - s, nc + ncc - 1 - s)

    def specs(chunk, d):
        return [
            pl.BlockSpec((None, step_rows, SSM_XBC), lambda bi, s: (bi, chunk(s), 0)),
            pl.BlockSpec((None, step_rows, LANES), lambda bi, s: (bi, chunk(s), d)),
            pl.BlockSpec((1, LANES), lambda bi, s: (0, 0)),
            pl.BlockSpec((1, LANES), lambda bi, s: (0, 0)),
        ]

    def inner_spec(chunk):
        return pl.BlockSpec((None, step_rows, SSM_INNER), lambda bi, s: (bi, chunk(s), 0))

    row_spec = pl.BlockSpec((1, SSM_INNER), lambda bi, s: (0, 0))
    z_spec = pl.BlockSpec((None, step_rows, SSM_INNER), lambda bi, s: (bi, bwd_chunk(s), z_col // SSM_INNER))
    y_f = pl.pallas_call(
        functools.partial(_ssd_kernel, reverse=False),
        grid=(b, nc),
        in_specs=specs(lambda s: s, 0),
        out_specs=inner_spec(lambda s: s),
        out_shape=jax.ShapeDtypeStruct((b, t, SSM_INNER), F32),
        scratch_shapes=scratch,
        compiler_params=params,
        name="ssd_fwd",
    )(xbc_act, dt2, dtb[0:1], a_neg[0:1])
    return pl.pallas_call(
        functools.partial(_ssd_kernel, reverse=True),
        grid=(b, nc),
        in_specs=specs(bwd_chunk, 1) + [inner_spec(bwd_chunk), z_spec, row_spec, row_spec],
        out_specs=inner_spec(bwd_chunk),
        out_shape=jax.ShapeDtypeStruct((b, t, SSM_INNER), BF16),
        scratch_shapes=scratch,
        compiler_params=params,
        name="ssd_bwd",
    )(xbc_act, dt2, dtb[1:2], a_neg[1:2], y_f, p_rm, dsk, norm_w.reshape(1, SSM_INNER).astype(F32))


HY_N2 = LANES
HY_CT = LANES
VMEM_LIMIT_HYENA = 60 * 1024 * 1024


def _hy_dims(L):
    n1 = 2 * L // HY_N2
    k1n = n1 // 2 + 1
    k1p = -(-k1n // SUBLANES) * SUBLANES
    return n1, k1n, k1p


def _hy_tables(L, n1_rows):
    n1, k1n, k1p = _hy_dims(L)
    n = 2 * L
    k1 = np.arange(k1n, dtype=np.float64)[None, :, None]
    nn = (HY_N2 * np.arange(n1_rows, dtype=np.float64)[None, None, :] + np.arange(HY_N2, dtype=np.float64)[:, None, None])
    ang = 2.0 * np.pi * ((k1 * nn) % n) / n
    m1 = np.zeros((HY_N2, 2 * k1p, n1_rows), np.float32)
    m1[:, :k1n] = np.cos(ang)
    m1[:, k1p:k1p + k1n] = -np.sin(ang)
    m4 = np.transpose(m1, (0, 2, 1))
    kk = np.arange(HY_N2, dtype=np.float64)
    a2 = 2.0 * np.pi * ((kk[:, None] * kk[None, :]) % HY_N2) / HY_N2
    c, s = np.cos(a2), np.sin(a2)
    f3 = np.block([[c, s], [-s, c]]).astype(np.float32)
    f3i = np.block([[c, -s], [s, c]]).astype(np.float32)
    return jnp.asarray(m1, BF16), jnp.asarray(m4, BF16), jnp.asarray(f3, BF16), jnp.asarray(f3i, BF16)


def _hy_stage1(u_ref, a_ref, m1_ref, n1_rows, k1p):
    def body(n2, carry):
        xs = u_ref[pl.ds(n2, n1_rows, stride=HY_N2), :].astype(BF16)
        a_ref[pl.ds(pl.multiple_of(n2 * 2 * k1p, 2 * k1p), 2 * k1p), :] = jnp.dot(m1_ref[n2], xs, preferred_element_type=F32)
        return carry

    lax.fori_loop(0, HY_N2, body, 0, unroll=8)


def _hy_spectrum_slab(a_ref, f3_ref, k1s, k1p):
    blk = jnp.concatenate([jnp.concatenate([a_ref[pl.ds(k1, HY_N2, stride=2 * k1p), :], a_ref[pl.ds(k1p + k1, HY_N2, stride=2 * k1p), :]], axis=0) for k1 in k1s], axis=1)
    return jnp.dot(f3_ref[...], blk.astype(BF16), preferred_element_type=F32)


def _hy_k1_loop(body, k1n):
    def pair(p, carry):
        body((2 * p, 2 * p + 1))
        return carry

    lax.fori_loop(0, (k1n - 1) // 2, pair, 0, unroll=math.gcd((k1n - 1) // 2, 4))
    body((k1n - 1,))


def _hy_slab_rows(k1):
    start = k1 * 2 * HY_N2
    return pl.ds(start if isinstance(k1, int) else pl.multiple_of(start, 2 * HY_N2), 2 * HY_N2)


def _hy_conv(u_ref, yo_ref, a_ref, y_ref, h_ref, order, m1_ref, m4_ref, f3_ref, f3i_ref, n1_rows, k1n, k1p):
    _hy_stage1(u_ref, a_ref, m1_ref, n1_rows, k1p)

    def stage2(k1s):
        x = _hy_spectrum_slab(a_ref, f3_ref, k1s, k1p)
        h = jnp.concatenate([h_ref[order, _hy_slab_rows(k1), :] for k1 in k1s], axis=1).astype(F32)
        xr, xi, hr, hi = x[:HY_N2], x[HY_N2:], h[:HY_N2], h[HY_N2:]
        z = jnp.concatenate([xr * hr - xi * hi, xr * hi + xi * hr], axis=0).astype(BF16)
        c = jnp.dot(f3i_ref[...], z, preferred_element_type=F32)
        for i, k1 in enumerate(k1s):
            lanes = slice(i * HY_CT, (i + 1) * HY_CT)
            y_ref[pl.ds(k1, HY_N2, stride=2 * k1p), :] = c[:HY_N2, lanes]
            y_ref[pl.ds(k1p + k1, HY_N2, stride=2 * k1p), :] = c[HY_N2:, lanes]

    _hy_k1_loop(stage2, k1n)

    def stage3(n2, carry):
        d = y_ref[pl.ds(pl.multiple_of(n2 * 2 * k1p, 2 * k1p), 2 * k1p), :].astype(BF16)
        yo_ref[pl.ds(n2, n1_rows, stride=HY_N2), :] = jnp.dot(m4_ref[n2], d, preferred_element_type=F32)
        return carry

    lax.fori_loop(0, HY_N2, stage3, 0, unroll=8)


def _hyena_kernel(v_ref, x1_ref, x2_ref, h_ref, m1_ref, m4_ref, f3_ref, f3i_ref, bias_ref, o_ref, a_ref, y_ref, u_ref, yo_ref, *, n1_rows, k1n, k1p):
    @pl.when(jnp.logical_and(pl.program_id(0) == 0, pl.program_id(1) == 0))
    def _():
        y_ref[...] = jnp.zeros_like(y_ref)

    u_ref[...] = v_ref[...].astype(F32)
    for order, gate_ref in enumerate((x1_ref, x2_ref)):
        _hy_conv(u_ref, yo_ref, a_ref, y_ref, h_ref, order, m1_ref, m4_ref, f3_ref, f3i_ref, n1_rows, k1n, k1p)
        z = gate_ref[...].astype(F32) * (yo_ref[...] + u_ref[...] * bias_ref[order:order + 1, :])
        if order == 0:
            u_ref[...] = z
        else:
            o_ref[...] = z.astype(o_ref.dtype)


def _hy_filter_kernel(hid_ref, wf_ref, wb_ref, bf_ref, bb_ref, dl_ref, m1_ref, f3_ref, o_ref, a_ref, sf_ref, uf_ref, ub_ref, *, n1_rows, k1n, k1p, scale_mid, scale_edge):
    slab = 2 * HY_N2
    L = uf_ref.shape[0]
    hid_rows = math.gcd(L, 2 * HY_N2)

    def fill(w_ref, b_ref, dst_ref, drop_first):
        w = _split3(w_ref[...])

        def body(i, energy):
            rows = pl.ds(pl.multiple_of(i * hid_rows, hid_rows), hid_rows)
            hid = _split3(hid_ref[rows, :])
            acc = sum(jnp.dot(hid[p], w[q], preferred_element_type=F32) for p in range(2) for q in range(2 - p))
            pos = i * hid_rows + lax.broadcasted_iota(jnp.int32, (hid_rows, HY_CT), 0)
            vals = (acc + b_ref[...]) * jnp.exp(pos.astype(F32) * (-1.0 / (L - 1)) * dl_ref[...])
            if drop_first:
                vals = jnp.where(pos == 0, 0.0, vals)
            dst_ref[rows, :] = vals
            return energy + jnp.sum(vals * vals, axis=0, keepdims=True)

        return lax.fori_loop(0, L // hid_rows, body, jnp.zeros((1, HY_CT), F32), unroll=math.gcd(L // hid_rows, 4))

    norm = lax.rsqrt(fill(wf_ref, bf_ref, uf_ref, False) + fill(wb_ref, bb_ref, ub_ref, True) + EPS)

    _hy_stage1(uf_ref, a_ref, m1_ref, n1_rows, k1p)

    def keep_fwd(k1s):
        x = _hy_spectrum_slab(a_ref, f3_ref, k1s, k1p)
        for i, k1 in enumerate(k1s):
            sf_ref[_hy_slab_rows(k1), :] = x[:, i * HY_CT:(i + 1) * HY_CT]

    _hy_k1_loop(keep_fwd, k1n)
    _hy_stage1(ub_ref, a_ref, m1_ref, n1_rows, k1p)

    def combine(k1s):
        xb_all = _hy_spectrum_slab(a_ref, f3_ref, k1s, k1p)
        for i, k1 in enumerate(k1s):
            xb = xb_all[:, i * HY_CT:(i + 1) * HY_CT]
            xf = sf_ref[_hy_slab_rows(k1), :]
            w = norm * jnp.where(jnp.logical_or(k1 == 0, k1 == k1n - 1), scale_edge, scale_mid)
            h = jnp.concatenate([xf[:HY_N2] + xb[:HY_N2], xf[HY_N2:] - xb[HY_N2:]], axis=0)
            o_ref[_hy_slab_rows(k1), :] = (h * w).astype(o_ref.dtype)

    _hy_k1_loop(combine, k1n)


def _single(block_shape, index_map):
    return pl.BlockSpec(block_shape, index_map, pipeline_mode=pl.Buffered(1))


def _hyena_filter_spectrum_pallas(hidden, w3, b3, deltas):
    L, ffn = hidden.shape
    c = deltas.shape[0]
    nct = c // HY_CT
    n = 2 * L
    n1, k1n, k1p = _hy_dims(L)
    n1_rows = L // HY_N2
    m1, _, f3, _ = _hy_tables(L, n1_rows)
    kern = functools.partial(_hy_filter_kernel, n1_rows=n1_rows, k1n=k1n, k1p=k1p, scale_mid=2.0 / n, scale_edge=1.0 / n)
    return pl.pallas_call(
        kern,
        grid=(HY_ORDER, nct),
        in_specs=[
            _single((L, ffn), lambda o, j: (0, 0)),
            pl.BlockSpec((ffn, HY_CT), lambda o, j: (0, o * nct + j)),
            pl.BlockSpec((ffn, HY_CT), lambda o, j: (0, (HY_ORDER + o) * nct + j)),
            pl.BlockSpec((1, HY_CT), lambda o, j: (0, o * nct + j)),
            pl.BlockSpec((1, HY_CT), lambda o, j: (0, (HY_ORDER + o) * nct + j)),
            pl.BlockSpec((1, HY_CT), lambda o, j: (0, j)),
            _single((HY_N2, 2 * k1p, n1_rows), lambda o, j: (0, 0, 0)),
            _single((2 * HY_N2, 2 * HY_N2), lambda o, j: (0, 0)),
        ],
        out_specs=pl.BlockSpec((None, k1n * 2 * HY_N2, HY_CT), lambda o, j: (o, 0, j)),
        out_shape=jax.ShapeDtypeStruct((HY_ORDER, k1n * 2 * HY_N2, c), BF16),
        scratch_shapes=[
            pltpu.VMEM((k1p * 2 * HY_N2, HY_CT), F32),
            pltpu.VMEM((k1n * 2 * HY_N2, HY_CT), F32),
            pltpu.VMEM((L, HY_CT), F32),
            pltpu.VMEM((L, HY_CT), F32),
        ],
        compiler_params=pltpu.CompilerParams(dimension_semantics=("arbitrary", "arbitrary"), vmem_limit_bytes=VMEM_LIMIT_HYENA),
        name="hyena_filter_dft",
    )(hidden, w3, w3, b3.reshape(1, -1), b3.reshape(1, -1), deltas.reshape(1, c), m1, f3)


def _hyena_long(hy, h_spec, bias):
    b, L, c3 = hy.shape
    c = c3 // (HY_ORDER + 1)
    nct = c // HY_CT
    n1, k1n, k1p = _hy_dims(L)
    n1_rows = L // HY_N2
    m1, m4, f3, f3i = _hy_tables(L, n1_rows)
    kern = functools.partial(_hyena_kernel, n1_rows=n1_rows, k1n=k1n, k1p=k1p)

    def col(part):
        return _single((None, L, HY_CT), lambda j, bi: (bi, 0, part * nct + j))

    return pl.pallas_call(
        kern,
        grid=(nct, b),
        in_specs=[
            col(0), col(1), col(2),
            _single((HY_ORDER, k1n * 2 * HY_N2, HY_CT), lambda j, bi: (0, 0, j)),
            _single((HY_N2, 2 * k1p, n1_rows), lambda j, bi: (0, 0, 0)),
            _single((HY_N2, n1_rows, 2 * k1p), lambda j, bi: (0, 0, 0)),
            _single((2 * HY_N2, 2 * HY_N2), lambda j, bi: (0, 0)),
            _single((2 * HY_N2, 2 * HY_N2), lambda j, bi: (0, 0)),
            pl.BlockSpec((HY_ORDER, HY_CT), lambda j, bi: (0, j)),
        ],
        out_specs=pl.BlockSpec((None, L, HY_CT), lambda j, bi: (bi, 0, j)),
        out_shape=jax.ShapeDtypeStruct((b, L, c), BF16),
        scratch_shapes=[
            pltpu.VMEM((k1p * 2 * HY_N2, HY_CT), F32),
            pltpu.VMEM((HY_N2 * 2 * k1p, HY_CT), F32),
            pltpu.VMEM((L, HY_CT), F32),
            pltpu.VMEM((L, HY_CT), F32),
        ],
        compiler_params=pltpu.CompilerParams(dimension_semantics=("arbitrary", "arbitrary"), vmem_limit_bytes=VMEM_LIMIT_HYENA),
        name="hyena_long_conv",
    )(hy, hy, hy, h_spec, m1, m4, f3, f3i, bias.astype(F32))


HY_CTX_CT = 256


def _hy_ctx_tables(L):
    n = 2 * L
    kb = L + 1
    kp = -(-kb // LANES) * LANES
    ang = 2.0 * np.pi * ((np.arange(kb, dtype=np.float64)[:, None] * np.arange(L, dtype=np.float64)[None, :]) % n) / n
    fwd = np.zeros((2 * kp, L), np.float32)
    fwd[:kb] = np.cos(ang)
    fwd[kp:kp + kb] = -np.sin(ang)
    return jnp.asarray(fwd, BF16), jnp.asarray(fwd.T, BF16), kb, kp


def _hy_ctx_filter_kernel(fwd_ref, bwd_ref, f_ref, o_ref, *, kb, kp, n):
    fwd = fwd_ref[...]
    bwd = jnp.where(lax.broadcasted_iota(jnp.int32, fwd.shape, 0) == 0, 0.0, bwd_ref[...])
    norm = lax.rsqrt(jnp.sum(fwd * fwd, axis=0, keepdims=True) + jnp.sum(bwd * bwd, axis=0, keepdims=True) + EPS)
    hf = jnp.dot(f_ref[...], fwd.astype(BF16), preferred_element_type=F32)
    hb = jnp.dot(f_ref[...], bwd.astype(BF16), preferred_element_type=F32)
    row = lax.broadcasted_iota(jnp.int32, hf.shape, 0)
    imag = row >= kp
    k = jnp.where(imag, row - kp, row)
    wk = jnp.where(jnp.logical_or(k == 0, k == kb - 1), 1.0 / n, 2.0 / n)
    o_ref[...] = (hf + jnp.where(imag, -hb, hb)) * (wk * norm)


def _hy_ctx_kernel(v_ref, x1_ref, x2_ref, h_ref, f_ref, g_ref, bias_ref, o_ref, *, kp):
    u = v_ref[...].astype(F32)
    for order, gate_ref in enumerate((x1_ref, x2_ref)):
        x = jnp.dot(f_ref[...], u.astype(BF16), preferred_element_type=F32)
        h = h_ref[order]
        xr, xi, hr, hi = x[:kp], x[kp:], h[:kp], h[kp:]
        z = jnp.concatenate([xr * hr - xi * hi, xr * hi + xi * hr], axis=0).astype(BF16)
        y = jnp.dot(g_ref[...], z, preferred_element_type=F32)
        u = gate_ref[...].astype(F32) * (y + u * bias_ref[order:order + 1, :])
    o_ref[...] = u.astype(o_ref.dtype)


def _hyena_ctx(hy, taps, bias):
    b, L, _ = hy.shape
    c = hy.shape[2] // (HY_ORDER + 1)
    ct = HY_CTX_CT
    nct = c // ct
    f_mat, g_mat, kb, kp = _hy_ctx_tables(L)
    params = pltpu.CompilerParams(dimension_semantics=("arbitrary", "arbitrary"), vmem_limit_bytes=VMEM_LIMIT)
    h_spec = pl.pallas_call(
        functools.partial(_hy_ctx_filter_kernel, kb=kb, kp=kp, n=2 * L),
        grid=(HY_ORDER, nct),
        in_specs=[
            pl.BlockSpec((L, ct), lambda o, j: (0, o * nct + j)),
            pl.BlockSpec((L, ct), lambda o, j: (0, (HY_ORDER + o) * nct + j)),
            pl.BlockSpec((2 * kp, L), lambda o, j: (0, 0)),
        ],
        out_specs=pl.BlockSpec((None, 2 * kp, ct), lambda o, j: (o, 0, j)),
        out_shape=jax.ShapeDtypeStruct((HY_ORDER, 2 * kp, c), F32),
        compiler_params=params,
        name="hyena_ctx_filter",
    )(taps, taps, f_mat)

    def col(part):
        return pl.BlockSpec((None, L, ct), lambda j, bi: (bi, 0, part * nct + j))

    return pl.pallas_call(
        functools.partial(_hy_ctx_kernel, kp=kp),
        grid=(nct, b),
        in_specs=[
            col(0), col(1), col(2),
            pl.BlockSpec((HY_ORDER, 2 * kp, ct), lambda j, bi: (0, 0, j)),
            pl.BlockSpec((2 * kp, L), lambda j, bi: (0, 0)),
            pl.BlockSpec((L, 2 * kp), lambda j, bi: (0, 0)),
            pl.BlockSpec((HY_ORDER, ct), lambda j, bi: (0, j)),
        ],
        out_specs=pl.BlockSpec((None, L, ct), lambda j, bi: (bi, 0, j)),
        out_shape=jax.ShapeDtypeStruct((b, L, c), BF16),
        compiler_params=params,
        name="hyena_ctx_conv",
    )(hy, hy, hy, h_spec, f_mat, g_mat, bias.astype(F32))


def _split_cols(t, sizes):
    return jnp.split(t, np.cumsum(sizes)[:-1].tolist(), axis=-1)


def _rms_norm(x, w):
    xf = x.astype(F32)
    y = xf * lax.rsqrt(jnp.mean(xf * xf, axis=-1, keepdims=True) + EPS)
    return (y * w.astype(F32)).astype(x.dtype)


def _dwconv_centred(x, w, b):
    k = w.shape[0]
    y = lax.conv_general_dilated(x, w[:, None, :].astype(x.dtype), window_strides=(1,), padding=[(k // 2, k // 2)], dimension_numbers=('NWC', 'WIO', 'NWC'), feature_group_count=x.shape[-1], precision=lax.Precision.HIGHEST)
    return y + b.astype(x.dtype)


def _flip_seq(t):
    return jnp.flip(t, axis=1)


def _to_col_major(t, rows):
    b, rest = t.shape[0], t.shape[2:]
    return jnp.swapaxes(t.reshape((b, rows, GRID_W) + rest), 1, 2).reshape((b, rows * GRID_W) + rest)


def _from_col_major(t, rows):
    b, rest = t.shape[0], t.shape[2:]
    return jnp.swapaxes(t.reshape((b, GRID_W, rows) + rest), 1, 2).reshape((b, rows * GRID_W) + rest)


def _hyena_filter_hidden(L, w1, b1, w2, b2, freq):
    hp = lax.Precision.HIGHEST
    t = jnp.linspace(0.0, 1.0, L, dtype=F32)[:, None]
    w = 2.0 * math.pi * jnp.arange(L, dtype=F32)[:, None] / L
    bands = jnp.linspace(1e-4, HY_BANDS - 1, HY_BANDS, dtype=F32)
    feats = jnp.concatenate([t, jnp.cos(bands * w), -jnp.sin(bands * w)], axis=-1)
    h = jnp.sin(freq[0] * (jnp.dot(feats, w1, precision=hp) + b1))
    return jnp.sin(freq[1] * (jnp.dot(h, w2, precision=hp) + b2))


def _hyena_decay_rates():
    max_decay = math.log(HY_DECAY_TARGET) / HY_FAST_DECAY
    min_decay = math.log(HY_DECAY_TARGET) / HY_SLOW_DECAY
    return jnp.abs(jnp.linspace(min_decay, max_decay, HY_WIDTH, dtype=F32))


def _hyena_filter_taps(L, w1, b1, w2, b2, w3, b3, freq):
    h = jnp.dot(_hyena_filter_hidden(L, w1, b1, w2, b2, freq), w3, precision=lax.Precision.HIGHEST) + b3
    t = jnp.linspace(0.0, 1.0, L, dtype=F32)[:, None]
    return h * jnp.tile(jnp.exp(-t * _hyena_decay_rates()), (1, 2 * HY_ORDER))


def _mixer_branches(h_rm, w_in, lb, n_ctx, ssm_conv_w, ssm_conv_b, ssm_dt_bias, ssm_a_log, ssm_d, ssm_norm, hy_conv_w, hy_conv_b, hy_w1, hy_b1, hy_w2, hy_b2, hy_w3, hy_b3, hy_freq, hy_bias, hg_norm):
    b, t, d = h_rm.shape
    n_lat = t - n_ctx
    rows = n_lat // GRID_W
    w_z, w_xbc, w_dt, w_hy, w_q, w_f, w_i, w_g, w_gate = _split_cols(w_in, IN_SIZES)
    h2 = h_rm.reshape(b * t, d)
    rm_parts = (w_z, w_xbc, w_hy, w_g, w_gate)
    col_z, col_xbc, col_hy, col_g, col_gate = np.cumsum([0] + [w.shape[1] for w in rm_parts[:-1]]).tolist()
    p_rm = _mm(h2, jnp.concatenate(rm_parts, axis=1).astype(BF16), BF16).reshape(b, t, -1)

    zero_pad = jnp.zeros((d, LANES - SSM_HEADS), F32)
    w_dt2 = jnp.concatenate([w_dt[:, :SSM_HEADS], zero_pad, w_dt[:, SSM_HEADS:], zero_pad], axis=1)
    dt2 = _mm(h2, w_dt2.astype(BF16), F32).reshape(b, t, 2 * LANES)
    xbc_act = _dwconv_stream(p_rm, ssm_conv_w, ssm_conv_b, n_ctx, True, c0=col_xbc)
    ym = _ssd_scan(xbc_act, dt2, p_rm, col_z, ssm_dt_bias, ssm_a_log, ssm_d, ssm_norm, n_ctx)

    hy_ctx, hy_lat = _dwconv_stream(p_rm, hy_conv_w, hy_conv_b, n_ctx, False, c0=col_hy, split=True)
    taps_ctx = _hyena_filter_taps(n_ctx, hy_w1, hy_b1, hy_w2, hy_b2, hy_w3, hy_b3, hy_freq)
    yh_ctx = _hyena_ctx(hy_ctx, taps_ctx, hy_bias)
    h_spec = _hyena_filter_spectrum_pallas(_hyena_filter_hidden(n_lat, hy_w1, hy_b1, hy_w2, hy_b2, hy_freq), hy_w3, hy_b3, _hyena_decay_rates())
    yh_lat = _hyena_long(hy_lat, h_spec, hy_bias)
    yh = jnp.concatenate([yh_ctx, yh_lat], axis=1)

    h_cm = jnp.concatenate([h_rm[:, :n_ctx], _to_col_major(h_rm[:, n_ctx:], rows)], axis=1).reshape(b * t, d)

    p_cm = _mm(h_cm, jnp.concatenate([w_q, w_f, w_i], axis=1).astype(BF16), BF16).reshape(b, t, -1)
    og = _hgrn_scan(p_cm, lb, hg_norm, n_ctx)
    og = jnp.concatenate([og[:, :n_ctx], _from_col_major(og[:, n_ctx:], rows)], axis=1)
    return ym, yh, og, p_rm, col_g, col_gate


ROW_TILE = 256
MOD_ROWS = SUBLANES
M_SHIFT_MIX, M_SCALE_MIX, M_GATE_MIX, M_SHIFT_FFN, M_SCALE_FFN, M_GATE_FFN = range(6)
ROW_PARAMS = pltpu.CompilerParams(dimension_semantics=("arbitrary", "arbitrary"), vmem_limit_bytes=VMEM_LIMIT)


def _rms(x):
    return x * lax.rsqrt(jnp.mean(x * x, axis=-1, keepdims=True) + EPS)


def _mrow(m_ref, r):
    return m_ref[r:r + 1, :]


def _row_spec(width):
    return pl.BlockSpec((None, ROW_TILE, width), lambda bi, i: (bi, i, 0))


def _vec_spec(width):
    return pl.BlockSpec((1, width), lambda bi, i: (0, 0))


def _mat_spec(k, n):
    return pl.BlockSpec((k, n), lambda bi, i: (0, 0))


def _mod_spec(n_ctx):
    return pl.BlockSpec((None, None, MOD_ROWS, D_MODEL), lambda bi, i: (bi, jnp.where(i < n_ctx // ROW_TILE, 0, 1), 0, 0))


def _norm_mod_kernel(x_ref, w_ref, m_ref, o_ref):
    y = _rms(x_ref[...]) * w_ref[...]
    o_ref[...] = (y * (1.0 + _mrow(m_ref, M_SCALE_MIX)) + _mrow(m_ref, M_SHIFT_MIX)).astype(o_ref.dtype)


def _norm_mod(xs, w, mods, n_ctx):
    b, t, d = xs.shape
    return pl.pallas_call(
        _norm_mod_kernel,
        grid=(b, t // ROW_TILE),
        in_specs=[_row_spec(d), _vec_spec(d), _mod_spec(n_ctx)],
        out_specs=_row_spec(d),
        out_shape=jax.ShapeDtypeStruct((b, t, d), BF16),
        compiler_params=ROW_PARAMS,
        name="norm_mod",
    )(xs, w.reshape(1, d), mods)


def _merge_kernel(ym_ref, yh_ref, og_ref, g_ref, gm_ref, gh_ref, gg_ref, x_ref, m_ref, w1_ref, w2_ref, w3_ref, wo_ref, npost_ref, npre_ref, rw_ref, rb_ref, xo_ref, h_ref, lg_ref):
    def sig(ref):
        return jax.nn.sigmoid(ref[...].astype(F32))

    go = g_ref[...].astype(F32)
    yg = (og_ref[...].astype(F32) * (go * jax.nn.sigmoid(go))).astype(BF16)
    merged = sig(gm_ref) * jnp.dot(ym_ref[...], w1_ref[...], preferred_element_type=F32)
    merged = merged + sig(gh_ref) * jnp.dot(yh_ref[...], w2_ref[...], preferred_element_type=F32)
    merged = merged + sig(gg_ref) * jnp.dot(yg, w3_ref[...], preferred_element_type=F32)
    mix = jnp.dot(merged.astype(BF16), wo_ref[...], preferred_element_type=F32)
    x = x_ref[...] + _mrow(m_ref, M_GATE_MIX) * (_rms(mix) * npost_ref[...])
    xo_ref[...] = x
    h = (_rms(x) * npre_ref[...] * (1.0 + _mrow(m_ref, M_SCALE_FFN)) + _mrow(m_ref, M_SHIFT_FFN)).astype(BF16)
    h_ref[...] = h
    lg_ref[...] = jnp.dot(h, rw_ref[...], preferred_element_type=F32) + rb_ref[...]


def _merge(ym, yh, og, p_rm, col_g, col_gate, xs, mods, w_br_ssm, w_br_hy, w_br_hg, w_out, norm_post, norm_ffn_pre, router_w, router_b, n_ctx):
    b, t, d = xs.shape
    rw = jnp.pad(router_w, ((0, 0), (0, LANES - N_EXPERTS))).astype(BF16)
    rb = jnp.pad(router_b, (0, LANES - N_EXPERTS)).reshape(1, LANES).astype(F32)

    def col_spec(col):
        return pl.BlockSpec((None, ROW_TILE, d), lambda bi, i: (bi, i, col // d))

    return pl.pallas_call(
        _merge_kernel,
        grid=(b, t // ROW_TILE),
        in_specs=[_row_spec(d), _row_spec(d), _row_spec(d), col_spec(col_g), col_spec(col_gate), col_spec(col_gate + d), col_spec(col_gate + 2 * d), _row_spec(d), _mod_spec(n_ctx),
                  _mat_spec(d, d), _mat_spec(d, d), _mat_spec(d, d), _mat_spec(d, d), _vec_spec(d), _vec_spec(d), _mat_spec(d, LANES), _vec_spec(LANES)],
        out_specs=[_row_spec(d), _row_spec(d), _row_spec(LANES)],
        out_shape=[jax.ShapeDtypeStruct((b, t, d), F32), jax.ShapeDtypeStruct((b, t, d), BF16), jax.ShapeDtypeStruct((b, t, LANES), F32)],
        compiler_params=ROW_PARAMS,
        name="branch_merge",
    )(ym, yh, og, p_rm, p_rm, p_rm, p_rm, xs, mods, w_br_ssm.astype(BF16), w_br_hy.astype(BF16), w_br_hg.astype(BF16), w_out.astype(BF16),
      norm_post.reshape(1, d), norm_ffn_pre.reshape(1, d), rw, rb)


def _post_ffn_kernel(y0_ref, y1_ref, y2_ref, y3_ref, x_ref, m_ref, w_ref, o_ref):
    f = y0_ref[...].astype(F32) + y1_ref[...].astype(F32) + y2_ref[...].astype(F32) + y3_ref[...].astype(F32)
    o_ref[...] = x_ref[...] + _mrow(m_ref, M_GATE_FFN) * (_rms(f) * w_ref[...])


def _post_ffn(f4, xs, mods, norm_post, n_ctx, skip):
    b, _, d = xs.shape
    t = f4.shape[2]
    i0 = skip // ROW_TILE

    def k_spec(k):
        return pl.BlockSpec((None, None, ROW_TILE, d), lambda bi, i: (k, bi, i, 0))

    return pl.pallas_call(
        _post_ffn_kernel,
        grid=(b, t // ROW_TILE),
        in_specs=[k_spec(k) for k in range(TOP_K)] + [
            pl.BlockSpec((None, ROW_TILE, d), lambda bi, i: (bi, i0 + i, 0)),
            pl.BlockSpec((None, None, MOD_ROWS, d), lambda bi, i: (bi, jnp.where(i0 + i < n_ctx // ROW_TILE, 0, 1), 0, 0)),
            _vec_spec(d)],
        out_specs=_row_spec(d),
        out_shape=jax.ShapeDtypeStruct((b, t, d), F32),
        compiler_params=ROW_PARAMS,
        name="post_ffn",
    )(f4, f4, f4, f4, xs, mods, norm_post.reshape(1, d))


def _moe_ffn(h2, logits, t_per_b, skip, li, w1, b1, w2, b2):
    t = logits.shape[0]
    d = h2.shape[1]
    n = t * TOP_K
    n_tiles = n // MOE_BLOCK
    top_v, top_e = lax.top_k(logits, TOP_K)
    gate_w = jax.nn.softmax(top_v, axis=-1)
    flat_e = top_e.reshape(n).astype(jnp.int32)
    iota = jnp.arange(n, dtype=jnp.int32)
    _, order, sw = lax.sort((flat_e, iota, gate_w.reshape(n)), num_keys=1, is_stable=True)
    _, inv = lax.sort((order, iota), num_keys=1)
    tok = order // TOP_K
    xs = h2[tok + (tok // t_per_b + 1) * skip]
    counts = jnp.sum((flat_e[:, None] == jnp.arange(N_EXPERTS, dtype=jnp.int32)[None, :]).astype(jnp.int32), axis=0)
    end = jnp.cumsum(counts)
    start = end - counts
    first_tile = start // MOE_BLOCK
    n_items = jnp.where(counts > 0, (end - 1) // MOE_BLOCK - first_tile + 1, 0)
    items_end = jnp.cumsum(n_items)
    w = jnp.arange(n_tiles + N_EXPERTS, dtype=jnp.int32)
    valid = w < items_end[-1]
    e_w = jnp.minimum(jnp.sum((w[:, None] >= items_end[None, :]).astype(jnp.int32), axis=1), N_EXPERTS - 1)
    tile_w = first_tile[e_w] + (w - (items_end[e_w] - n_items[e_w]))
    lo = jnp.where(valid, jnp.maximum(start[e_w], tile_w * MOE_BLOCK), 0)
    hi = jnp.where(valid, jnp.minimum(end[e_w], (tile_w + 1) * MOE_BLOCK), 0)
    tile_w = jnp.where(valid, tile_w, n_tiles - 1)
    first = jnp.concatenate([jnp.ones((1,), jnp.int32), (tile_w[1:] != tile_w[:-1]).astype(jnp.int32)])
    newexp = jnp.concatenate([jnp.ones((1,), jnp.int32), (e_w[1:] != e_w[:-1]).astype(jnp.int32)])
    ys = _moe_experts(xs, sw, tile_w.astype(jnp.int32), e_w, lo.astype(jnp.int32), hi.astype(jnp.int32), first, newexp, li, w1, b1, w2, b2)
    return ys[inv.reshape(t, TOP_K).T]


def kernel(x, c, ctx, c_ctx, w_mod, b_mod, norm_mix_pre, norm_mix_post, norm_ffn_pre, norm_ffn_post, w_in, ssm_conv_w, ssm_conv_b, ssm_dt_bias, ssm_a_log, ssm_d, ssm_norm, hy_conv_w, hy_conv_b, hy_w1, hy_b1, hy_w2, hy_b2, hy_w3, hy_b3, hy_freq, hy_bias, hg_lb_logits, hg_norm, w_br_ssm, w_br_hy, w_br_hg, w_out, router_w, router_b, exp_w1, exp_b1, exp_w2, exp_b2):
    hp = lax.Precision.HIGHEST
    b, n_lat, d = x.shape
    n_ctx = ctx.shape[1]
    lb = jax.nn.softmax(hg_lb_logits.astype(F32), axis=1)
    lb = jnp.cumsum(lb, axis=1) - lb[:, :1]
    silu_c = jax.nn.silu(c)
    silu_cc = jax.nn.silu(c_ctx)
    xs = jnp.concatenate([ctx, x], axis=1)
    for li in range(DEPTH):
        mx = (jnp.dot(silu_c, w_mod[li], precision=hp) + b_mod[li]).reshape(b, 1, 6, d)
        mc = jnp.broadcast_to((jnp.dot(silu_cc, w_mod[li], precision=hp) + b_mod[li]).reshape(1, 1, 6, d), (b, 1, 6, d))
        mods = jnp.pad(jnp.concatenate([mc, mx], axis=1), ((0, 0), (0, 0), (0, MOD_ROWS - 6), (0, 0)))
        h = _norm_mod(xs, norm_mix_pre[li], mods, n_ctx)
        ym, yh, og, p_rm, col_g, col_gate = _mixer_branches(h, w_in[li], lb[:, li], n_ctx, ssm_conv_w[li], ssm_conv_b[li], ssm_dt_bias[li], ssm_a_log[li], ssm_d[li], ssm_norm[li], hy_conv_w[li], hy_conv_b[li], hy_w1[li], hy_b1[li], hy_w2[li], hy_b2[li], hy_w3[li], hy_b3[li], hy_freq[li], hy_bias[li], hg_norm[li])
        xs, h_ffn, logits = _merge(ym, yh, og, p_rm, col_g, col_gate, xs, mods, w_br_ssm[li], w_br_hy[li], w_br_hg[li], w_out[li], norm_mix_post[li], norm_ffn_pre[li], router_w[li], router_b[li], n_ctx)
        skip = n_ctx if li == DEPTH - 1 else 0
        t = xs.shape[1] - skip
        f4 = _moe_ffn(h_ffn.reshape(-1, d), logits[:, skip:, :N_EXPERTS].reshape(b * t, N_EXPERTS), t, skip, li, exp_w1, exp_b1[li], exp_w2, exp_b2[li])
        xs = _post_ffn(f4.reshape(TOP_K, b, t, d), xs, mods, norm_ffn_post[li], n_ctx, skip)
    return xs
```

```python
import functools
import math

import jax
import jax.numpy as jnp
import numpy as np
from jax import lax
from jax.experimental import pallas as pl
from jax.experimental.pallas import tpu as pltpu

D_MODEL = 1024
DEPTH = 2
GRID_W = 64

SSM_HEADS = 16
SSM_HEAD_DIM = 64
SSM_INNER = SSM_HEADS * SSM_HEAD_DIM
SSM_STATE = 128
SSM_GROUPS = 4
SSD_CHUNK = 128
SSM_XBC = SSM_INNER + 2 * SSM_GROUPS * SSM_STATE

HY_WIDTH = D_MODEL
HY_ORDER = 2
HY_BANDS = 16
HY_FAST_DECAY = 0.3
HY_SLOW_DECAY = 1.5
HY_DECAY_TARGET = 1e-2

HG_HEADS = 8
HG_KDIM = 128
HG_VDIM = D_MODEL // HG_HEADS
HG_QK = HG_HEADS * HG_KDIM
HG_V = HG_HEADS * HG_VDIM
HG_CHUNK = 64
F_FLOOR = 1e-20

N_EXPERTS = 32
TOP_K = 4
D_FF = D_MODEL
SWIGLU_LIMIT = 7.0
SWIGLU_ALPHA = 1.702
MOE_BLOCK = 512

N_BRANCHES = 3
IN_SIZES = (SSM_INNER, SSM_XBC, 2 * SSM_HEADS, (HY_ORDER + 1) * HY_WIDTH, HG_QK, 2 * HG_QK, HG_V, HG_V, N_BRANCHES * D_MODEL)
EPS = 1e-6
F32 = jnp.float32
BF16 = jnp.bfloat16

LANES = 128
VMEM_LIMIT = 56 * 1024 * 1024


def _mm_kernel(a_ref, b_ref, o_ref):
    o_ref[...] = jnp.dot(a_ref[...], b_ref[...], preferred_element_type=F32).astype(o_ref.dtype)


def _mm(a, b, out_dtype=F32, tm=1024, tn=2048):
    m, k = a.shape
    n = b.shape[1]
    tm = math.gcd(m, tm)
    tn = math.gcd(n, tn)
    assert tm % SUBLANES == 0 and tn % LANES == 0, (m, n, tm, tn)
    return pl.pallas_call(
        _mm_kernel,
        grid=(n // tn, m // tm),
        in_specs=[pl.BlockSpec((tm, k), lambda j, i: (i, 0)), pl.BlockSpec((k, tn), lambda j, i: (0, j))],
        out_specs=pl.BlockSpec((tm, tn), lambda j, i: (i, j)),
        out_shape=jax.ShapeDtypeStruct((m, n), out_dtype),
        compiler_params=pltpu.CompilerParams(dimension_semantics=("arbitrary", "arbitrary"), vmem_limit_bytes=VMEM_LIMIT),
        name="dense_mm",
    )(a, b)


def _moe_kernel(tile_ref, exp_ref, lo_ref, hi_ref, first_ref, newexp_ref, x_ref, sw_ref, w1_ref, b1_ref, w2_ref, b2_ref, o_ref, w1b_ref, w2b_ref):
    del exp_ref
    w = pl.program_id(0)
    lo, hi = lo_ref[w], hi_ref[w]

    @pl.when(newexp_ref[w] == 1)
    def _():
        def cast_rows(i, carry):
            rows = pl.ds(pl.multiple_of(i * LANES, LANES), LANES)
            w1b_ref[rows, :] = w1_ref[rows, :].astype(BF16)
            w2b_ref[rows, :] = w2_ref[rows, :].astype(BF16)
            return carry

        lax.fori_loop(0, D_MODEL // LANES, cast_rows, 0)

    @pl.when(hi > lo)
    def _():
        hh = jnp.dot(x_ref[...], w1b_ref[...], preferred_element_type=F32) + b1_ref[...]
        g = jnp.minimum(hh[:, :D_FF], SWIGLU_LIMIT)
        u = jnp.clip(hh[:, D_FF:], -SWIGLU_LIMIT, SWIGLU_LIMIT)
        act = (u + 1.0) * g * jax.nn.sigmoid(SWIGLU_ALPHA * g)
        y = jnp.dot(act.astype(BF16), w2b_ref[...], preferred_element_type=F32) + b2_ref[...]
        y = (y * sw_ref[...]).astype(o_ref.dtype)
        rows = tile_ref[w] * MOE_BLOCK + lax.broadcasted_iota(jnp.int32, (MOE_BLOCK, 1), 0)
        mine = jnp.logical_and(rows >= lo, rows < hi)

        @pl.when(first_ref[w] == 1)
        def _():
            o_ref[...] = jnp.where(mine, y, jnp.zeros_like(y))

        @pl.when(first_ref[w] != 1)
        def _():
            o_ref[...] = jnp.where(mine, y, o_ref[...])


def _moe_experts(xs, sw, tile_w, exp_w, lo, hi, first, newexp, li, w1, b1, w2, b2):
    n, d = xs.shape
    grid_spec = pltpu.PrefetchScalarGridSpec(
        num_scalar_prefetch=6,
        grid=(tile_w.shape[0],),
        in_specs=[
            pl.BlockSpec((MOE_BLOCK, d), lambda w, tl, ex, lo_, hi_, fi, ne: (tl[w], 0)),
            pl.BlockSpec((MOE_BLOCK, 1), lambda w, tl, ex, lo_, hi_, fi, ne: (tl[w], 0)),
            pl.BlockSpec((None, None, d, 2 * D_FF), lambda w, tl, ex, lo_, hi_, fi, ne: (li, ex[w], 0, 0)),
            pl.BlockSpec((None, 1, 2 * D_FF), lambda w, tl, ex, lo_, hi_, fi, ne: (ex[w], 0, 0)),
            pl.BlockSpec((None, None, D_FF, d), lambda w, tl, ex, lo_, hi_, fi, ne: (li, ex[w], 0, 0)),
            pl.BlockSpec((None, 1, d), lambda w, tl, ex, lo_, hi_, fi, ne: (ex[w], 0, 0)),
        ],
        out_specs=pl.BlockSpec((MOE_BLOCK, d), lambda w, tl, ex, lo_, hi_, fi, ne: (tl[w], 0)),
        scratch_shapes=[pltpu.VMEM((d, 2 * D_FF), BF16), pltpu.VMEM((D_FF, d), BF16)],
    )
    return pl.pallas_call(
        _moe_kernel,
        grid_spec=grid_spec,
        out_shape=jax.ShapeDtypeStruct((n, d), BF16),
        compiler_params=pltpu.CompilerParams(dimension_semantics=("arbitrary",), vmem_limit_bytes=VMEM_LIMIT),
        name="moe_experts",
    )(tile_w, exp_w, lo, hi, first, newexp, xs, sw.reshape(n, 1), w1, b1.reshape(N_EXPERTS, 1, 2 * D_FF), w2, b2.reshape(N_EXPERTS, 1, d))


SUBLANES = 8
NEG_BIG = -1e30
HG_STEP_CHUNKS = 4
HIER_LEVELS = (64, 32, 16, 8, 4)
LOG2E = math.log2(math.e)


def _split3(x):
    h1 = x.astype(BF16)
    r1 = x - h1.astype(F32)
    h2 = r1.astype(BF16)
    h3 = (r1 - h2.astype(F32)).astype(BF16)
    return h1, h2, h3


def _dot_nt(a, b):
    return lax.dot_general(a, b, (((1,), (1,)), ((), ())), preferred_element_type=F32)


def _gla_kernel(q_ref, a_ref, v_ref, lb_ref, *rest, reverse):
    if reverse:
        of_ref, w_ref, o_ref, st_ref, at_ref = rest
    else:
        o_ref, st_ref, at_ref = rest
    Q = HG_CHUNK

    @pl.when(pl.program_id(1) == 0)
    def _():
        st_ref[...] = jnp.zeros_like(st_ref)

    row = lax.broadcasted_iota(jnp.int32, (Q, Q), 0)
    col = lax.broadcasted_iota(jnp.int32, (Q, Q), 1)
    tri = jnp.where((col >= row) if reverse else (col <= row), 1.0, 0.0).astype(BF16)
    same_block = {s: (row // s) == (col // s) for s in HIER_LEVELS[1:] + (2, 1)}
    rowk = lax.broadcasted_iota(jnp.int32, (Q, HG_KDIM), 0)
    q_rows = {s: ((rowk % s) < s // 2) if reverse else ((rowk % s) >= s // 2) for s in HIER_LEVELS + (2,)}
    of_w = (of_ref, w_ref) if reverse else (None, None)
    for sub in (range(HG_STEP_CHUNKS - 1, -1, -1) if reverse else range(HG_STEP_CHUNKS)):
        _gla_chunk(sub, q_ref, a_ref, v_ref, lb_ref[...], of_w, o_ref, st_ref, at_ref, tri, same_block, q_rows, reverse)


def _gla_chunk(sub, q_ref, a_ref, v_ref, lb, of_w, o_ref, st_ref, at_ref, tri, same_block, q_rows, reverse):
    Q = HG_CHUNK
    rs = slice(sub * Q, (sub + 1) * Q)
    of_ref, w_ref = of_w
    shp3 = (Q // SUBLANES, SUBLANES, HG_KDIM)
    sub3 = lax.broadcasted_iota(jnp.int32, shp3, 1)
    a = a_ref[rs, :].astype(F32)
    f_all = jnp.maximum(lb + (1.0 - lb) * jax.nn.sigmoid(a), F_FLOOR)
    kk = (1.0 - lb) * jax.nn.sigmoid(-a)
    g_all = sum(jnp.dot(tri, p, preferred_element_type=F32) for p in _split3(jnp.log(f_all) * LOG2E))
    q_all = q_ref[rs, :].astype(F32)
    q_all = q_all * jax.nn.sigmoid(q_all)
    v_all = v_ref[rs, :]
    tot = 0 if reverse else Q - 1

    def level_ref(g, s):
        half = s // 2
        m_off = half if reverse else half - 1
        if s >= 2 * SUBLANES:
            return jnp.concatenate([jnp.broadcast_to(g[b0 + m_off:b0 + m_off + 1, :], (s, HG_KDIM)) for b0 in range(0, Q, s)], axis=0)
        g3 = g.reshape(shp3)
        ref = jnp.broadcast_to(g3[:, m_off:m_off + 1, :], shp3)
        for b0 in range(s, SUBLANES, s):
            ref = jnp.where(sub3 >= b0, jnp.broadcast_to(g3[:, b0 + m_off:b0 + m_off + 1, :], shp3), ref)
        return ref.reshape(Q, HG_KDIM)

    for h in range(HG_HEADS):
        sl = slice(h * HG_KDIM, (h + 1) * HG_KDIM)
        g, qh, kh, fh = g_all[:, sl], q_all[:, sl], kk[:, sl], f_all[:, sl]
        attn = None
        for s in HIER_LEVELS:
            gref = level_ref(g, s)
            eq = jnp.exp2(jnp.where(q_rows[s], g - gref, NEG_BIG))
            ek = jnp.exp2(jnp.where(q_rows[s], NEG_BIG, gref - g))
            lvl = _dot_nt((qh * eq).astype(BF16), (kh * ek).astype(BF16))
            attn = lvl if attn is None else jnp.where(same_block[s], lvl, attn)
        lvl = _dot_nt(jnp.where(q_rows[2], qh * fh, 0.0).astype(BF16), jnp.where(q_rows[2], 0.0, kh).astype(BF16))
        attn = jnp.where(same_block[2], lvl, attn)
        attn = jnp.where(same_block[1], _dot_nt(qh.astype(BF16), kh.astype(BF16)), attn)
        at_ref[sub, h] = attn.astype(BF16)

    for h in range(HG_HEADS):
        sl = slice(h * HG_KDIM, (h + 1) * HG_KDIM)
        g, qh, kh, vb = g_all[:, sl], q_all[:, sl], kk[:, sl], v_all[:, sl]
        g_tot = g[tot:tot + 1, :]
        st = st_ref[h]
        o = _dot_nt((qh * jnp.exp2(g)).astype(BF16), st.astype(BF16))
        o = o + jnp.dot(at_ref[sub, h], vb, preferred_element_type=F32)
        k_st = (kh * jnp.exp2(g_tot - g)).astype(BF16)
        st_ref[h] = st * jnp.exp2(g_tot) + jnp.dot(vb.astype(F32).T.astype(BF16), k_st, preferred_element_type=F32)
        if reverse:
            o = o + of_ref[rs, sl]
            o = o * lax.rsqrt(jnp.mean(o * o, axis=-1, keepdims=True) + EPS) * w_ref[:, sl]
        o_ref[rs, sl] = o.astype(o_ref.dtype)


def _hgrn_scan(p_cm, lb, norm_w, n_ctx):
    b, t, _ = p_cm.shape
    col_q, col_f, col_i = 0, 1, 3
    step_rows = HG_STEP_CHUNKS * HG_CHUNK
    assert n_ctx % step_rows == 0 and t % step_rows == 0, (n_ctx, t, step_rows)
    nc, ncc = t // step_rows, n_ctx // step_rows
    blk = (None, step_rows, HG_QK)
    scratch = [pltpu.VMEM((HG_HEADS, HG_VDIM, HG_KDIM), F32), pltpu.VMEM((HG_STEP_CHUNKS, HG_HEADS, HG_CHUNK, HG_CHUNK), BF16)]
    params = pltpu.CompilerParams(dimension_semantics=("arbitrary", "arbitrary"), vmem_limit_bytes=VMEM_LIMIT)
    row_spec = pl.BlockSpec((1, HG_QK), lambda bi, s: (0, 0))

    def fwd_chunk(s):
        return s

    def bwd_chunk(s):
        return jnp.where(s < ncc, ncc - 1 - s, nc + ncc - 1 - s)

    o_f = pl.pallas_call(
        functools.partial(_gla_kernel, reverse=False),
        grid=(b, nc),
        in_specs=[
            pl.BlockSpec(blk, lambda bi, s: (bi, fwd_chunk(s), col_q)),
            pl.BlockSpec(blk, lambda bi, s: (bi, fwd_chunk(s), col_f)),
            pl.BlockSpec(blk, lambda bi, s: (bi, fwd_chunk(s), col_i)),
            row_spec,
        ],
        out_specs=pl.BlockSpec(blk, lambda bi, s: (bi, fwd_chunk(s), 0)),
        out_shape=jax.ShapeDtypeStruct((b, t, HG_V), F32),
        scratch_shapes=scratch,
        compiler_params=params,
        name="gla_fwd",
    )(p_cm, p_cm, p_cm, lb[0:1])
    return pl.pallas_call(
        functools.partial(_gla_kernel, reverse=True),
        grid=(b, nc),
        in_specs=[
            pl.BlockSpec(blk, lambda bi, s: (bi, bwd_chunk(s), col_q)),
            pl.BlockSpec(blk, lambda bi, s: (bi, bwd_chunk(s), col_f + 1)),
            pl.BlockSpec(blk, lambda bi, s: (bi, bwd_chunk(s), col_i)),
            row_spec,
            pl.BlockSpec(blk, lambda bi, s: (bi, bwd_chunk(s), 0)),
            row_spec,
        ],
        out_specs=pl.BlockSpec(blk, lambda bi, s: (bi, bwd_chunk(s), 0)),
        out_shape=jax.ShapeDtypeStruct((b, t, HG_V), BF16),
        scratch_shapes=scratch,
        compiler_params=params,
        name="gla_bwd",
    )(p_cm, p_cm, p_cm, lb[1:2], o_f, norm_w.reshape(1, HG_V))


CONV_TILE = 256
HALO = 16


def _dwconv_kernel(prev_ref, cur_ref, next_ref, w_ref, b_ref, *o_refs, taps, n_ctx_tiles, n_tiles, silu):
    i = pl.program_id(2)
    first = jnp.logical_or(i == 0, i == n_ctx_tiles)
    last = jnp.logical_or(i == n_ctx_tiles - 1, i == n_tiles - 1)
    pad = taps // 2
    xp = jnp.where(first, 0.0, prev_ref[...].astype(F32))
    xn = jnp.where(last, 0.0, next_ref[...].astype(F32))
    xcat = jnp.concatenate([xp, cur_ref[...].astype(F32), xn], axis=0)
    acc = jnp.broadcast_to(b_ref[...], cur_ref.shape).astype(F32)
    for k in range(taps):
        off = HALO - pad + k
        acc = acc + w_ref[k:k + 1, :] * xcat[off:off + CONV_TILE, :]
    if silu:
        acc = acc * jax.nn.sigmoid(acc)
    if len(o_refs) == 1:
        o_refs[0][...] = acc.astype(o_refs[0].dtype)
    else:
        ctx_ref, lat_ref = o_refs

        @pl.when(i < n_ctx_tiles)
        def _():
            ctx_ref[...] = acc.astype(ctx_ref.dtype)

        @pl.when(i >= n_ctx_tiles)
        def _():
            lat_ref[...] = acc.astype(lat_ref.dtype)


def _dwconv_stream(x, w, bias, n_ctx, silu, c0=0, ct=1024, split=False):
    b, t, _ = x.shape
    taps, c = w.shape
    n_tiles = t // CONV_TILE
    nct = n_ctx // CONV_TILE
    hb = CONV_TILE // HALO
    n_halo = t // HALO
    j0 = c0 // ct
    kern = functools.partial(_dwconv_kernel, taps=taps, n_ctx_tiles=nct, n_tiles=n_tiles, silu=silu)
    if split:
        out_specs = [pl.BlockSpec((None, CONV_TILE, ct), lambda bi, j, i: (bi, jnp.minimum(i, nct - 1), j)),
                     pl.BlockSpec((None, CONV_TILE, ct), lambda bi, j, i: (bi, jnp.maximum(i - nct, 0), j))]
        out_shape = [jax.ShapeDtypeStruct((b, n_ctx, c), BF16), jax.ShapeDtypeStruct((b, t - n_ctx, c), BF16)]
    else:
        out_specs = pl.BlockSpec((None, CONV_TILE, ct), lambda bi, j, i: (bi, i, j))
        out_shape = jax.ShapeDtypeStruct((b, t, c), BF16)
    return pl.pallas_call(
        kern,
        grid=(b, c // ct, n_tiles),
        in_specs=[
            pl.BlockSpec((None, HALO, ct), lambda bi, j, i: (bi, jnp.maximum(i * hb - 1, 0), j0 + j)),
            pl.BlockSpec((None, CONV_TILE, ct), lambda bi, j, i: (bi, i, j0 + j)),
            pl.BlockSpec((None, HALO, ct), lambda bi, j, i: (bi, jnp.minimum((i + 1) * hb, n_halo - 1), j0 + j)),
            pl.BlockSpec((taps, ct), lambda bi, j, i: (0, j)),
            pl.BlockSpec((1, ct), lambda bi, j, i: (0, j)),
        ],
        out_specs=out_specs,
        out_shape=out_shape,
        compiler_params=pltpu.CompilerParams(dimension_semantics=("arbitrary",) * 3, vmem_limit_bytes=VMEM_LIMIT),
        name="dwconv",
    )(x, x, x, w.astype(F32), bias.reshape(1, c).astype(F32))


SSD_STEP_CHUNKS = 2
SSM_GHEADS = SSM_HEADS // SSM_GROUPS
SSM_GP = SSM_GHEADS * SSM_HEAD_DIM


def _ssd_kernel(xbc_ref, dt_ref, dtb_ref, a_ref, *rest, reverse):
    if reverse:
        yf_ref, z_ref, dsk_ref, nw_ref, o_ref, st_ref, m_ref, xd_ref, xst_ref, y_ref = rest
    else:
        o_ref, st_ref, m_ref, xd_ref, xst_ref, y_ref = rest
    Q = SSD_CHUNK

    @pl.when(pl.program_id(1) == 0)
    def _():
        st_ref[...] = jnp.zeros_like(st_ref)

    row = lax.broadcasted_iota(jnp.int32, (Q, Q), 0)
    col = lax.broadcasted_iota(jnp.int32, (Q, Q), 1)
    keep = (col >= row) if reverse else (col <= row)
    tri = jnp.where(keep, 1.0, 0.0).astype(BF16)
    expand = jnp.where(lax.broadcasted_iota(jnp.int32, (LANES, SSM_INNER), 1) // SSM_HEAD_DIM == lax.broadcasted_iota(jnp.int32, (LANES, SSM_INNER), 0), 1.0, 0.0).astype(BF16)

    rev_refs = (yf_ref, z_ref, dsk_ref, nw_ref) if reverse else None
    for sub in (range(SSD_STEP_CHUNKS - 1, -1, -1) if reverse else range(SSD_STEP_CHUNKS)):
        _ssd_chunk(sub, xbc_ref, dt_ref, dtb_ref[...], a_ref[...], rev_refs, o_ref, st_ref, m_ref, xd_ref, xst_ref, y_ref, keep, tri, expand, reverse)


def _ssd_chunk(sub, xbc_ref, dt_ref, dt_bias, a_neg, rev_refs, o_ref, st_ref, m_ref, xd_ref, xst_ref, y_ref, keep, tri, expand, reverse):
    Q = SSD_CHUNK
    rs = slice(sub * Q, (sub + 1) * Q)
    dt = jax.nn.softplus(dt_ref[rs, :] + dt_bias)
    a = dt * a_neg
    cs = sum(jnp.dot(tri, p, preferred_element_type=F32) for p in _split3(a))
    tot = 0 if reverse else Q - 1
    cs_tot = cs[tot:tot + 1, :]
    cs_t = cs.T
    dt_e = jnp.dot(dt.astype(BF16), expand, preferred_element_type=F32)
    e_in = jnp.dot(jnp.exp(cs).astype(BF16), expand, preferred_element_type=F32)
    e_st = jnp.dot(jnp.exp(cs_tot - cs).astype(BF16), expand, preferred_element_type=F32)
    e_tot = jnp.dot(jnp.broadcast_to(jnp.exp(cs_tot), (SUBLANES, LANES)).astype(BF16), expand, preferred_element_type=F32)[0:1, :]

    xs = xbc_ref[rs, :SSM_INNER].astype(F32)
    xd_ref[sub] = (xs * dt_e).astype(BF16)
    xst_ref[sub] = (xs * dt_e * e_st).astype(BF16)
    for g in range(SSM_GROUPS):
        bm = xbc_ref[rs, SSM_INNER + g * SSM_STATE:SSM_INNER + (g + 1) * SSM_STATE]
        cm = xbc_ref[rs, SSM_INNER + (SSM_GROUPS + g) * SSM_STATE:SSM_INNER + (SSM_GROUPS + g + 1) * SSM_STATE]
        cb = _dot_nt(cm, bm)
        for h in range(g * SSM_GHEADS, (g + 1) * SSM_GHEADS):
            diff = jnp.broadcast_to(cs[:, h:h + 1], (Q, Q)) - jnp.broadcast_to(cs_t[h:h + 1, :], (Q, Q))
            m_ref[sub, h] = (cb * jnp.exp(jnp.where(keep, diff, NEG_BIG))).astype(BF16)

    lane_lo = lax.broadcasted_iota(jnp.int32, (Q, LANES), 1) < SSM_HEAD_DIM
    for g in range(SSM_GROUPS):
        bm = xbc_ref[rs, SSM_INNER + g * SSM_STATE:SSM_INNER + (g + 1) * SSM_STATE]
        cm = xbc_ref[rs, SSM_INNER + (SSM_GROUPS + g) * SSM_STATE:SSM_INNER + (SSM_GROUPS + g + 1) * SSM_STATE]
        gl = slice(g * SSM_GP, (g + 1) * SSM_GP)
        st = st_ref[g]
        y_off = jnp.dot(cm, st.astype(BF16), preferred_element_type=F32) * e_in[:, gl]
        for hp in range(SSM_GHEADS // 2):
            h0 = g * SSM_GHEADS + 2 * hp
            lanes = slice((h0 // 2) * LANES, (h0 // 2 + 1) * LANES)
            pair = [jnp.dot(m_ref[sub, h], xd_ref[sub, :, lanes], preferred_element_type=F32) for h in (h0, h0 + 1)]
            y_ref[sub, :, lanes] = jnp.where(lane_lo, pair[0], pair[1]) + y_off[:, hp * LANES:(hp + 1) * LANES]
        st_ref[g] = st * e_tot[:, gl] + jnp.dot(bm.astype(F32).T.astype(BF16), xst_ref[sub, :, gl], preferred_element_type=F32)
    y = y_ref[sub]
    if reverse:
        yf_ref, z_ref, dsk_ref, nw_ref = rev_refs
        y = (y + yf_ref[rs, :] + xbc_ref[rs, :SSM_INNER].astype(F32) * dsk_ref[...])
        zz = z_ref[rs, :].astype(F32)
        y = y * (zz * jax.nn.sigmoid(zz))
        y = y * lax.rsqrt(jnp.mean(y * y, axis=-1, keepdims=True) + EPS) * nw_ref[...]
    o_ref[rs, :] = y.astype(o_ref.dtype)


def _ssd_scan(xbc_act, dt2, p_rm, z_col, dt_bias, a_log, d_skip, norm_w, n_ctx):
    b, t, _ = xbc_act.shape
    step_rows = SSD_STEP_CHUNKS * SSD_CHUNK
    assert n_ctx % step_rows == 0 and t % step_rows == 0, (n_ctx, t, step_rows)
    nc, ncc = t // step_rows, n_ctx // step_rows

    def pad_heads(v):
        return jnp.pad(v.astype(F32), ((0, 0), (0, LANES - SSM_HEADS)))

    dtb = pad_heads(dt_bias.reshape(2, SSM_HEADS))
    a_neg = pad_heads(-jnp.exp(a_log.astype(F32)))
    dsk = jnp.repeat(d_skip.astype(F32), SSM_HEAD_DIM).reshape(1, SSM_INNER)
    scratch = [pltpu.VMEM((SSM_GROUPS, SSM_STATE, SSM_GP), F32), pltpu.VMEM((SSD_STEP_CHUNKS, SSM_HEADS, SSD_CHUNK, SSD_CHUNK), BF16),
               pltpu.VMEM((SSD_STEP_CHUNKS, SSD_CHUNK, SSM_INNER), BF16), pltpu.VMEM((SSD_STEP_CHUNKS, SSD_CHUNK, SSM_INNER), BF16),
               pltpu.VMEM((SSD_STEP_CHUNKS, SSD_CHUNK, SSM_INNER), F32)]
    params = pltpu.CompilerParams(dimension_semantics=("arbitrary", "arbitrary"), vmem_limit_bytes=VMEM_LIMIT)

    def bwd_chunk(s):
        return jnp.where(s < ncc, ncc - 1 - s, nc + ncc - 1 - s)

    def specs(chunk, d):
        return [
            pl.BlockSpec((None, step_rows, SSM_XBC), lambda bi, s: (bi, chunk(s), 0)),
            pl.BlockSpec((None, step_rows, LANES), lambda bi, s: (bi, chunk(s), d)),
            pl.BlockSpec((1, LANES), lambda bi, s: (0, 0)),
            pl.BlockSpec((1, LANES), lambda bi, s: (0, 0)),
        ]

    def inner_spec(chunk):
        return pl.BlockSpec((None, step_rows, SSM_INNER), lambda bi, s: (bi, chunk(s), 0))

    row_spec = pl.BlockSpec((1, SSM_INNER), lambda bi, s: (0, 0))
    z_spec = pl.BlockSpec((None, step_rows, SSM_INNER), lambda bi, s: (bi, bwd_chunk(s), z_col // SSM_INNER))
    y_f = pl.pallas_call(
        functools.partial(_ssd_kernel, reverse=False),
        grid=(b, nc),
        in_specs=specs(lambda s: s, 0),
        out_specs=inner_spec(lambda s: s),
        out_shape=jax.ShapeDtypeStruct((b, t, SSM_INNER), F32),
        scratch_shapes=scratch,
        compiler_params=params,
        name="ssd_fwd",
    )(xbc_act, dt2, dtb[0:1], a_neg[0:1])
    return pl.pallas_call(
        functools.partial(_ssd_kernel, reverse=True),
        grid=(b, nc),
        in_specs=specs(bwd_chunk, 1) + [inner_spec(bwd_chunk), z_spec, row_spec, row_spec],
        out_specs=inner_spec(bwd_chunk),
        out_shape=jax.ShapeDtypeStruct((b, t, SSM_INNER), BF16),
        scratch_shapes=scratch,
        compiler_params=params,
        name="ssd_bwd",
    )(xbc_act, dt2, dtb[1:2], a_neg[1:2], y_f, p_rm, dsk, norm_w.reshape(1, SSM_INNER).astype(F32))


HY_N2 = LANES
HY_CT = LANES
VMEM_LIMIT_HYENA = 60 * 1024 * 1024


def _hy_dims(L):
    n1 = 2 * L // HY_N2
    k1n = n1 // 2 + 1
    k1p = -(-k1n // SUBLANES) * SUBLANES
    return n1, k1n, k1p


def _hy_tables(L, n1_rows):
    n1, k1n, k1p = _hy_dims(L)
    n = 2 * L
    k1 = np.arange(k1n, dtype=np.float64)[None, :, None]
    nn = (HY_N2 * np.arange(n1_rows, dtype=np.float64)[None, None, :] + np.arange(HY_N2, dtype=np.float64)[:, None, None])
    ang = 2.0 * np.pi * ((k1 * nn) % n) / n
    m1 = np.zeros((HY_N2, 2 * k1p, n1_rows), np.float32)
    m1[:, :k1n] = np.cos(ang)
    m1[:, k1p:k1p + k1n] = -np.sin(ang)
    m4 = np.transpose(m1, (0, 2, 1))
    kk = np.arange(HY_N2, dtype=np.float64)
    a2 = 2.0 * np.pi * ((kk[:, None] * kk[None, :]) % HY_N2) / HY_N2
    c, s = np.cos(a2), np.sin(a2)
    f3 = np.block([[c, s], [-s, c]]).astype(np.float32)
    f3i = np.block([[c, -s], [s, c]]).astype(np.float32)
    return jnp.asarray(m1, BF16), jnp.asarray(m4, BF16), jnp.asarray(f3, BF16), jnp.asarray(f3i, BF16)


def _hy_stage1(u_ref, a_ref, m1_ref, n1_rows, k1p):
    def body(n2, carry):
        xs = u_ref[pl.ds(n2, n1_rows, stride=HY_N2), :].astype(BF16)
        a_ref[pl.ds(pl.multiple_of(n2 * 2 * k1p, 2 * k1p), 2 * k1p), :] = jnp.dot(m1_ref[n2], xs, preferred_element_type=F32)
        return carry

    lax.fori_loop(0, HY_N2, body, 0, unroll=8)


def _hy_spectrum_slab(a_ref, f3_ref, k1s, k1p):
    blk = jnp.concatenate([jnp.concatenate([a_ref[pl.ds(k1, HY_N2, stride=2 * k1p), :], a_ref[pl.ds(k1p + k1, HY_N2, stride=2 * k1p), :]], axis=0) for k1 in k1s], axis=1)
    return jnp.dot(f3_ref[...], blk.astype(BF16), preferred_element_type=F32)


def _hy_k1_loop(body, k1n):
    def pair(p, carry):
        body((2 * p, 2 * p + 1))
        return carry

    lax.fori_loop(0, (k1n - 1) // 2, pair, 0, unroll=math.gcd((k1n - 1) // 2, 4))
    body((k1n - 1,))


def _hy_slab_rows(k1):
    start = k1 * 2 * HY_N2
    return pl.ds(start if isinstance(k1, int) else pl.multiple_of(start, 2 * HY_N2), 2 * HY_N2)


def _hy_conv(u_ref, yo_ref, a_ref, y_ref, h_ref, order, m1_ref, m4_ref, f3_ref, f3i_ref, n1_rows, k1n, k1p):
    _hy_stage1(u_ref, a_ref, m1_ref, n1_rows, k1p)

    def stage2(k1s):
        x = _hy_spectrum_slab(a_ref, f3_ref, k1s, k1p)
        h = jnp.concatenate([h_ref[order, _hy_slab_rows(k1), :] for k1 in k1s], axis=1).astype(F32)
        xr, xi, hr, hi = x[:HY_N2], x[HY_N2:], h[:HY_N2], h[HY_N2:]
        z = jnp.concatenate([xr * hr - xi * hi, xr * hi + xi * hr], axis=0).astype(BF16)
        c = jnp.dot(f3i_ref[...], z, preferred_element_type=F32)
        for i, k1 in enumerate(k1s):
            lanes = slice(i * HY_CT, (i + 1) * HY_CT)
            y_ref[pl.ds(k1, HY_N2, stride=2 * k1p), :] = c[:HY_N2, lanes]
            y_ref[pl.ds(k1p + k1, HY_N2, stride=2 * k1p), :] = c[HY_N2:, lanes]

    _hy_k1_loop(stage2, k1n)

    def stage3(n2, carry):
        d = y_ref[pl.ds(pl.multiple_of(n2 * 2 * k1p, 2 * k1p), 2 * k1p), :].astype(BF16)
        yo_ref[pl.ds(n2, n1_rows, stride=HY_N2), :] = jnp.dot(m4_ref[n2], d, preferred_element_type=F32)
        return carry

    lax.fori_loop(0, HY_N2, stage3, 0, unroll=8)


def _hyena_kernel(v_ref, x1_ref, x2_ref, h_ref, m1_ref, m4_ref, f3_ref, f3i_ref, bias_ref, o_ref, a_ref, y_ref, u_ref, yo_ref, *, n1_rows, k1n, k1p):
    @pl.when(jnp.logical_and(pl.program_id(0) == 0, pl.program_id(1) == 0))
    def _():
        y_ref[...] = jnp.zeros_like(y_ref)

    u_ref[...] = v_ref[...].astype(F32)
    for order, gate_ref in enumerate((x1_ref, x2_ref)):
        _hy_conv(u_ref, yo_ref, a_ref, y_ref, h_ref, order, m1_ref, m4_ref, f3_ref, f3i_ref, n1_rows, k1n, k1p)
        z = gate_ref[...].astype(F32) * (yo_ref[...] + u_ref[...] * bias_ref[order:order + 1, :])
        if order == 0:
            u_ref[...] = z
        else:
            o_ref[...] = z.astype(o_ref.dtype)


def _hy_filter_kernel(hid_ref, wf_ref, wb_ref, bf_ref, bb_ref, dl_ref, m1_ref, f3_ref, o_ref, a_ref, sf_ref, uf_ref, ub_ref, *, n1_rows, k1n, k1p, scale_mid, scale_edge):
    L = uf_ref.shape[0]
    hid_rows = math.gcd(L, 2 * HY_N2)

    def fill(w_ref, b_ref, dst_ref, drop_first):
        w = _split3(w_ref[...])

        def body(i, energy):
            rows = pl.ds(pl.multiple_of(i * hid_rows, hid_rows), hid_rows)
            hid = _split3(hid_ref[rows, :])
            acc = sum(jnp.dot(hid[p], w[q], preferred_element_type=F32) for p in range(2) for q in range(2 - p))
            pos = i * hid_rows + lax.broadcasted_iota(jnp.int32, (hid_rows, HY_CT), 0)
            vals = (acc + b_ref[...]) * jnp.exp(pos.astype(F32) * (-1.0 / (L - 1)) * dl_ref[...])
            if drop_first:
                vals = jnp.where(pos == 0, 0.0, vals)
            dst_ref[rows, :] = vals
            return energy + jnp.sum(vals * vals, axis=0, keepdims=True)

        return lax.fori_loop(0, L // hid_rows, body, jnp.zeros((1, HY_CT), F32), unroll=math.gcd(L // hid_rows, 4))

    norm = lax.rsqrt(fill(wf_ref, bf_ref, uf_ref, False) + fill(wb_ref, bb_ref, ub_ref, True) + EPS)

    _hy_stage1(uf_ref, a_ref, m1_ref, n1_rows, k1p)

    def keep_fwd(k1s):
        x = _hy_spectrum_slab(a_ref, f3_ref, k1s, k1p)
        for i, k1 in enumerate(k1s):
            sf_ref[_hy_slab_rows(k1), :] = x[:, i * HY_CT:(i + 1) * HY_CT]

    _hy_k1_loop(keep_fwd, k1n)
    _hy_stage1(ub_ref, a_ref, m1_ref, n1_rows, k1p)

    def combine(k1s):
        xb_all = _hy_spectrum_slab(a_ref, f3_ref, k1s, k1p)
        for i, k1 in enumerate(k1s):
            xb = xb_all[:, i * HY_CT:(i + 1) * HY_CT]
            xf = sf_ref[_hy_slab_rows(k1), :]
            w = norm * jnp.where(jnp.logical_or(k1 == 0, k1 == k1n - 1), scale_edge, scale_mid)
            h = jnp.concatenate([xf[:HY_N2] + xb[:HY_N2], xf[HY_N2:] - xb[HY_N2:]], axis=0)
            o_ref[_hy_slab_rows(k1), :] = (h * w).astype(o_ref.dtype)

    _hy_k1_loop(combine, k1n)


def _single(block_shape, index_map):
    return pl.BlockSpec(block_shape, index_map, pipeline_mode=pl.Buffered(1))


def _hyena_filter_spectrum_pallas(hidden, w3, b3, deltas):
    L, ffn = hidden.shape
    c = deltas.shape[0]
    nct = c // HY_CT
    n = 2 * L
    n1, k1n, k1p = _hy_dims(L)
    n1_rows = L // HY_N2
    m1, _, f3, _ = _hy_tables(L, n1_rows)
    kern = functools.partial(_hy_filter_kernel, n1_rows=n1_rows, k1n=k1n, k1p=k1p, scale_mid=2.0 / n, scale_edge=1.0 / n)
    return pl.pallas_call(
        kern,
        grid=(HY_ORDER, nct),
        in_specs=[
            _single((L, ffn), lambda o, j: (0, 0)),
            pl.BlockSpec((ffn, HY_CT), lambda o, j: (0, o * nct + j)),
            pl.BlockSpec((ffn, HY_CT), lambda o, j: (0, (HY_ORDER + o) * nct + j)),
            pl.BlockSpec((1, HY_CT), lambda o, j: (0, o * nct + j)),
            pl.BlockSpec((1, HY_CT), lambda o, j: (0, (HY_ORDER + o) * nct + j)),
            pl.BlockSpec((1, HY_CT), lambda o, j: (0, j)),
            _single((HY_N2, 2 * k1p, n1_rows), lambda o, j: (0, 0, 0)),
            _single((2 * HY_N2, 2 * HY_N2), lambda o, j: (0, 0)),
        ],
        out_specs=pl.BlockSpec((None, k1n * 2 * HY_N2, HY_CT), lambda o, j: (o, 0, j)),
        out_shape=jax.ShapeDtypeStruct((HY_ORDER, k1n * 2 * HY_N2, c), BF16),
        scratch_shapes=[
            pltpu.VMEM((k1p * 2 * HY_N2, HY_CT), F32),
            pltpu.VMEM((k1n * 2 * HY_N2, HY_CT), F32),
            pltpu.VMEM((L, HY_CT), F32),
            pltpu.VMEM((L, HY_CT), F32),
        ],
        compiler_params=pltpu.CompilerParams(dimension_semantics=("arbitrary", "arbitrary"), vmem_limit_bytes=VMEM_LIMIT_HYENA),
        name="hyena_filter_dft",
    )(hidden, w3, w3, b3.reshape(1, -1), b3.reshape(1, -1), deltas.reshape(1, c), m1, f3)


def _hyena_long(hy, h_spec, bias):
    b, L, c3 = hy.shape
    c = c3 // (HY_ORDER + 1)
    nct = c // HY_CT
    n1, k1n, k1p = _hy_dims(L)
    n1_rows = L // HY_N2
    m1, m4, f3, f3i = _hy_tables(L, n1_rows)
    kern = functools.partial(_hyena_kernel, n1_rows=n1_rows, k1n=k1n, k1p=k1p)

    def col(part):
        return _single((None, L, HY_CT), lambda j, bi: (bi, 0, part * nct + j))

    return pl.pallas_call(
        kern,
        grid=(nct, b),
        in_specs=[
            col(0), col(1), col(2),
            _single((HY_ORDER, k1n * 2 * HY_N2, HY_CT), lambda j, bi: (0, 0, j)),
            _single((HY_N2, 2 * k1p, n1_rows), lambda j, bi: (0, 0, 0)),
            _single((HY_N2, n1_rows, 2 * k1p), lambda j, bi: (0, 0, 0)),
            _single((2 * HY_N2, 2 * HY_N2), lambda j, bi: (0, 0)),
            _single((2 * HY_N2, 2 * HY_N2), lambda j, bi: (0, 0)),
            pl.BlockSpec((HY_ORDER, HY_CT), lambda j, bi: (0, j)),
        ],
        out_specs=pl.BlockSpec((None, L, HY_CT), lambda j, bi: (bi, 0, j)),
        out_shape=jax.ShapeDtypeStruct((b, L, c), BF16),
        scratch_shapes=[
            pltpu.VMEM((k1p * 2 * HY_N2, HY_CT), F32),
            pltpu.VMEM((HY_N2 * 2 * k1p, HY_CT), F32),
            pltpu.VMEM((L, HY_CT), F32),
            pltpu.VMEM((L, HY_CT), F32),
        ],
        compiler_params=pltpu.CompilerParams(dimension_semantics=("arbitrary", "arbitrary"), vmem_limit_bytes=VMEM_LIMIT_HYENA),
        name="hyena_long_conv",
    )(hy, hy, hy, h_spec, m1, m4, f3, f3i, bias.astype(F32))


HY_CTX_CT = 256


def _hy_ctx_tables(L):
    n = 2 * L
    kb = L + 1
    kp = -(-kb // LANES) * LANES
    ang = 2.0 * np.pi * ((np.arange(kb, dtype=np.float64)[:, None] * np.arange(L, dtype=np.float64)[None, :]) % n) / n
    fwd = np.zeros((2 * kp, L), np.float32)
    fwd[:kb] = np.cos(ang)
    fwd[kp:kp + kb] = -np.sin(ang)
    return jnp.asarray(fwd, BF16), jnp.asarray(fwd.T, BF16), kb, kp


def _hy_ctx_filter_kernel(fwd_ref, bwd_ref, f_ref, o_ref, *, kb, kp, n):
    fwd = fwd_ref[...]
    bwd = jnp.where(lax.broadcasted_iota(jnp.int32, fwd.shape, 0) == 0, 0.0, bwd_ref[...])
    norm = lax.rsqrt(jnp.sum(fwd * fwd, axis=0, keepdims=True) + jnp.sum(bwd * bwd, axis=0, keepdims=True) + EPS)
    hf = jnp.dot(f_ref[...], fwd.astype(BF16), preferred_element_type=F32)
    hb = jnp.dot(f_ref[...], bwd.astype(BF16), preferred_element_type=F32)
    row = lax.broadcasted_iota(jnp.int32, hf.shape, 0)
    imag = row >= kp
    k = jnp.where(imag, row - kp, row)
    wk = jnp.where(jnp.logical_or(k == 0, k == kb - 1), 1.0 / n, 2.0 / n)
    o_ref[...] = (hf + jnp.where(imag, -hb, hb)) * (wk * norm)


def _hy_ctx_kernel(v_ref, x1_ref, x2_ref, h_ref, f_ref, g_ref, bias_ref, o_ref, *, kp):
    u = v_ref[...].astype(F32)
    for order, gate_ref in enumerate((x1_ref, x2_ref)):
        x = jnp.dot(f_ref[...], u.astype(BF16), preferred_element_type=F32)
        h = h_ref[order]
        xr, xi, hr, hi = x[:kp], x[kp:], h[:kp], h[kp:]
        z = jnp.concatenate([xr * hr - xi * hi, xr * hi + xi * hr], axis=0).astype(BF16)
        y = jnp.dot(g_ref[...], z, preferred_element_type=F32)
        u = gate_ref[...].astype(F32) * (y + u * bias_ref[order:order + 1, :])
    o_ref[...] = u.astype(o_ref.dtype)


def _hyena_ctx(hy, taps, bias):
    b, L, _ = hy.shape
    c = hy.shape[2] // (HY_ORDER + 1)
    ct = HY_CTX_CT
    nct = c // ct
    f_mat, g_mat, kb, kp = _hy_ctx_tables(L)
    params = pltpu.CompilerParams(dimension_semantics=("arbitrary", "arbitrary"), vmem_limit_bytes=VMEM_LIMIT)
    h_spec = pl.pallas_call(
        functools.partial(_hy_ctx_filter_kernel, kb=kb, kp=kp, n=2 * L),
        grid=(HY_ORDER, nct),
        in_specs=[
            pl.BlockSpec((L, ct), lambda o, j: (0, o * nct + j)),
            pl.BlockSpec((L, ct), lambda o, j: (0, (HY_ORDER + o) * nct + j)),
            pl.BlockSpec((2 * kp, L), lambda o, j: (0, 0)),
        ],
        out_specs=pl.BlockSpec((None, 2 * kp, ct), lambda o, j: (o, 0, j)),
        out_shape=jax.ShapeDtypeStruct((HY_ORDER, 2 * kp, c), F32),
        compiler_params=params,
        name="hyena_ctx_filter",
    )(taps, taps, f_mat)

    def col(part):
        return pl.BlockSpec((None, L, ct), lambda j, bi: (bi, 0, part * nct + j))

    return pl.pallas_call(
        functools.partial(_hy_ctx_kernel, kp=kp),
        grid=(nct, b),
        in_specs=[
            col(0), col(1), col(2),
            pl.BlockSpec((HY_ORDER, 2 * kp, ct), lambda j, bi: (0, 0, j)),
            pl.BlockSpec((2 * kp, L), lambda j, bi: (0, 0)),
            pl.BlockSpec((L, 2 * kp), lambda j, bi: (0, 0)),
            pl.BlockSpec((HY_ORDER, ct), lambda j, bi: (0, j)),
        ],
        out_specs=pl.BlockSpec((None, L, ct), lambda j, bi: (bi, 0, j)),
        out_shape=jax.ShapeDtypeStruct((b, L, c), BF16),
        compiler_params=params,
        name="hyena_ctx_conv",
    )(hy, hy, hy, h_spec, f_mat, g_mat, bias.astype(F32))


def _split_cols(t, sizes):
    return jnp.split(t, np.cumsum(sizes)[:-1].tolist(), axis=-1)


def _to_col_major(t, rows):
    b, rest = t.shape[0], t.shape[2:]
    return jnp.swapaxes(t.reshape((b, rows, GRID_W) + rest), 1, 2).reshape((b, rows * GRID_W) + rest)


def _from_col_major(t, rows):
    b, rest = t.shape[0], t.shape[2:]
    return jnp.swapaxes(t.reshape((b, GRID_W, rows) + rest), 1, 2).reshape((b, rows * GRID_W) + rest)


def _hyena_filter_hidden(L, w1, b1, w2, b2, freq):
    hp = lax.Precision.HIGHEST
    t = jnp.linspace(0.0, 1.0, L, dtype=F32)[:, None]
    w = 2.0 * math.pi * jnp.arange(L, dtype=F32)[:, None] / L
    bands = jnp.linspace(1e-4, HY_BANDS - 1, HY_BANDS, dtype=F32)
    feats = jnp.concatenate([t, jnp.cos(bands * w), -jnp.sin(bands * w)], axis=-1)
    h = jnp.sin(freq[0] * (jnp.dot(feats, w1, precision=hp) + b1))
    return jnp.sin(freq[1] * (jnp.dot(h, w2, precision=hp) + b2))


def _hyena_decay_rates():
    max_decay = math.log(HY_DECAY_TARGET) / HY_FAST_DECAY
    min_decay = math.log(HY_DECAY_TARGET) / HY_SLOW_DECAY
    return jnp.abs(jnp.linspace(min_decay, max_decay, HY_WIDTH, dtype=F32))


def _hyena_filter_taps(L, w1, b1, w2, b2, w3, b3, freq):
    h = jnp.dot(_hyena_filter_hidden(L, w1, b1, w2, b2, freq), w3, precision=lax.Precision.HIGHEST) + b3
    t = jnp.linspace(0.0, 1.0, L, dtype=F32)[:, None]
    return h * jnp.tile(jnp.exp(-t * _hyena_decay_rates()), (1, 2 * HY_ORDER))


def _mixer_branches(h_rm, w_in, lb, n_ctx, ssm_conv_w, ssm_conv_b, ssm_dt_bias, ssm_a_log, ssm_d, ssm_norm, hy_conv_w, hy_conv_b, hy_w1, hy_b1, hy_w2, hy_b2, hy_w3, hy_b3, hy_freq, hy_bias, hg_norm):
    b, t, d = h_rm.shape
    n_lat = t - n_ctx
    rows = n_lat // GRID_W
    w_z, w_xbc, w_dt, w_hy, w_q, w_f, w_i, w_g, w_gate = _split_cols(w_in, IN_SIZES)
    h2 = h_rm.reshape(b * t, d)
    rm_parts = (w_z, w_xbc, w_hy, w_g, w_gate)
    col_z, col_xbc, col_hy, col_g, col_gate = np.cumsum([0] + [w.shape[1] for w in rm_parts[:-1]]).tolist()
    p_rm = _mm(h2, jnp.concatenate(rm_parts, axis=1).astype(BF16), BF16).reshape(b, t, -1)

    zero_pad = jnp.zeros((d, LANES - SSM_HEADS), F32)
    w_dt2 = jnp.concatenate([w_dt[:, :SSM_HEADS], zero_pad, w_dt[:, SSM_HEADS:], zero_pad], axis=1)
    dt2 = _mm(h2, w_dt2.astype(BF16), F32).reshape(b, t, 2 * LANES)
    xbc_act = _dwconv_stream(p_rm, ssm_conv_w, ssm_conv_b, n_ctx, True, c0=col_xbc)
    ym = _ssd_scan(xbc_act, dt2, p_rm, col_z, ssm_dt_bias, ssm_a_log, ssm_d, ssm_norm, n_ctx)

    hy_ctx, hy_lat = _dwconv_stream(p_rm, hy_conv_w, hy_conv_b, n_ctx, False, c0=col_hy, split=True)
    taps_ctx = _hyena_filter_taps(n_ctx, hy_w1, hy_b1, hy_w2, hy_b2, hy_w3, hy_b3, hy_freq)
    yh_ctx = _hyena_ctx(hy_ctx, taps_ctx, hy_bias)
    h_spec = _hyena_filter_spectrum_pallas(_hyena_filter_hidden(n_lat, hy_w1, hy_b1, hy_w2, hy_b2, hy_freq), hy_w3, hy_b3, _hyena_decay_rates())
    yh_lat = _hyena_long(hy_lat, h_spec, hy_bias)
    yh = jnp.concatenate([yh_ctx, yh_lat], axis=1)

    h_cm = jnp.concatenate([h_rm[:, :n_ctx], _to_col_major(h_rm[:, n_ctx:], rows)], axis=1).reshape(b * t, d)

    p_cm = _mm(h_cm, jnp.concatenate([w_q, w_f, w_i], axis=1).astype(BF16), BF16).reshape(b, t, -1)
    og = _hgrn_scan(p_cm, lb, hg_norm, n_ctx)
    og = jnp.concatenate([og[:, :n_ctx], _from_col_major(og[:, n_ctx:], rows)], axis=1)
    return ym, yh, og, p_rm, col_g, col_gate


ROW_TILE = 256
MOD_ROWS = SUBLANES
M_SHIFT_MIX, M_SCALE_MIX, M_GATE_MIX, M_SHIFT_FFN, M_SCALE_FFN, M_GATE_FFN = range(6)
ROW_PARAMS = pltpu.CompilerParams(dimension_semantics=("arbitrary", "arbitrary"), vmem_limit_bytes=VMEM_LIMIT)


def _rms(x):
    return x * lax.rsqrt(jnp.mean(x * x, axis=-1, keepdims=True) + EPS)


def _mrow(m_ref, r):
    return m_ref[r:r + 1, :]


def _row_spec(width):
    return pl.BlockSpec((None, ROW_TILE, width), lambda bi, i: (bi, i, 0))


def _vec_spec(width):
    return pl.BlockSpec((1, width), lambda bi, i: (0, 0))


def _mat_spec(k, n):
    return pl.BlockSpec((k, n), lambda bi, i: (0, 0))


def _mod_spec(n_ctx):
    return pl.BlockSpec((None, None, MOD_ROWS, D_MODEL), lambda bi, i: (bi, jnp.where(i < n_ctx // ROW_TILE, 0, 1), 0, 0))


def _norm_mod_kernel(x_ref, w_ref, m_ref, o_ref):
    y = _rms(x_ref[...]) * w_ref[...]
    o_ref[...] = (y * (1.0 + _mrow(m_ref, M_SCALE_MIX)) + _mrow(m_ref, M_SHIFT_MIX)).astype(o_ref.dtype)


def _norm_mod(xs, w, mods, n_ctx):
    b, t, d = xs.shape
    return pl.pallas_call(
        _norm_mod_kernel,
        grid=(b, t // ROW_TILE),
        in_specs=[_row_spec(d), _vec_spec(d), _mod_spec(n_ctx)],
        out_specs=_row_spec(d),
        out_shape=jax.ShapeDtypeStruct((b, t, d), BF16),
        compiler_params=ROW_PARAMS,
        name="norm_mod",
    )(xs, w.reshape(1, d), mods)


def _merge_kernel(ym_ref, yh_ref, og_ref, g_ref, gm_ref, gh_ref, gg_ref, x_ref, m_ref, w1_ref, w2_ref, w3_ref, wo_ref, npost_ref, npre_ref, rw_ref, rb_ref, xo_ref, h_ref, lg_ref):
    def sig(ref):
        return jax.nn.sigmoid(ref[...].astype(F32))

    go = g_ref[...].astype(F32)
    yg = (og_ref[...].astype(F32) * (go * jax.nn.sigmoid(go))).astype(BF16)
    merged = sig(gm_ref) * jnp.dot(ym_ref[...], w1_ref[...], preferred_element_type=F32)
    merged = merged + sig(gh_ref) * jnp.dot(yh_ref[...], w2_ref[...], preferred_element_type=F32)
    merged = merged + sig(gg_ref) * jnp.dot(yg, w3_ref[...], preferred_element_type=F32)
    mix = jnp.dot(merged.astype(BF16), wo_ref[...], preferred_element_type=F32)
    x = x_ref[...] + _mrow(m_ref, M_GATE_MIX) * (_rms(mix) * npost_ref[...])
    xo_ref[...] = x
    h = (_rms(x) * npre_ref[...] * (1.0 + _mrow(m_ref, M_SCALE_FFN)) + _mrow(m_ref, M_SHIFT_FFN)).astype(BF16)
    h_ref[...] = h
    lg_ref[...] = jnp.dot(h, rw_ref[...], preferred_element_type=F32) + rb_ref[...]


def _merge(ym, yh, og, p_rm, col_g, col_gate, xs, mods, w_br_ssm, w_br_hy, w_br_hg, w_out, norm_post, norm_ffn_pre, router_w, router_b, n_ctx):
    b, t, d = xs.shape
    rw = jnp.pad(router_w, ((0, 0), (0, LANES - N_EXPERTS))).astype(BF16)
    rb = jnp.pad(router_b, (0, LANES - N_EXPERTS)).reshape(1, LANES).astype(F32)

    def col_spec(col):
        return pl.BlockSpec((None, ROW_TILE, d), lambda bi, i: (bi, i, col // d))

    return pl.pallas_call(
        _merge_kernel,
        grid=(b, t // ROW_TILE),
        in_specs=[_row_spec(d), _row_spec(d), _row_spec(d), col_spec(col_g), col_spec(col_gate), col_spec(col_gate + d), col_spec(col_gate + 2 * d), _row_spec(d), _mod_spec(n_ctx),
                  _mat_spec(d, d), _mat_spec(d, d), _mat_spec(d, d), _mat_spec(d, d), _vec_spec(d), _vec_spec(d), _mat_spec(d, LANES), _vec_spec(LANES)],
        out_specs=[_row_spec(d), _row_spec(d), _row_spec(LANES)],
        out_shape=[jax.ShapeDtypeStruct((b, t, d), F32), jax.ShapeDtypeStruct((b, t, d), BF16), jax.ShapeDtypeStruct((b, t, LANES), F32)],
        compiler_params=ROW_PARAMS,
        name="branch_merge",
    )(ym, yh, og, p_rm, p_rm, p_rm, p_rm, xs, mods, w_br_ssm.astype(BF16), w_br_hy.astype(BF16), w_br_hg.astype(BF16), w_out.astype(BF16),
      norm_post.reshape(1, d), norm_ffn_pre.reshape(1, d), rw, rb)


def _post_ffn_kernel(y0_ref, y1_ref, y2_ref, y3_ref, x_ref, m_ref, w_ref, o_ref):
    f = y0_ref[...].astype(F32) + y1_ref[...].astype(F32) + y2_ref[...].astype(F32) + y3_ref[...].astype(F32)
    o_ref[...] = x_ref[...] + _mrow(m_ref, M_GATE_FFN) * (_rms(f) * w_ref[...])


def _post_ffn(f4, xs, mods, norm_post, n_ctx, skip):
    b, _, d = xs.shape
    t = f4.shape[2]
    i0 = skip // ROW_TILE

    def k_spec(k):
        return pl.BlockSpec((None, None, ROW_TILE, d), lambda bi, i: (k, bi, i, 0))

    return pl.pallas_call(
        _post_ffn_kernel,
        grid=(b, t // ROW_TILE),
        in_specs=[k_spec(k) for k in range(TOP_K)] + [
            pl.BlockSpec((None, ROW_TILE, d), lambda bi, i: (bi, i0 + i, 0)),
            pl.BlockSpec((None, None, MOD_ROWS, d), lambda bi, i: (bi, jnp.where(i0 + i < n_ctx // ROW_TILE, 0, 1), 0, 0)),
            _vec_spec(d)],
        out_specs=_row_spec(d),
        out_shape=jax.ShapeDtypeStruct((b, t, d), F32),
        compiler_params=ROW_PARAMS,
        name="post_ffn",
    )(f4, f4, f4, f4, xs, mods, norm_post.reshape(1, d))


def _moe_ffn(h2, logits, t_per_b, skip, li, w1, b1, w2, b2):
    t = logits.shape[0]
    d = h2.shape[1]
    n = t * TOP_K
    n_tiles = n // MOE_BLOCK
    top_v, top_e = lax.top_k(logits, TOP_K)
    gate_w = jax.nn.softmax(top_v, axis=-1)
    flat_e = top_e.reshape(n).astype(jnp.int32)
    iota = jnp.arange(n, dtype=jnp.int32)
    _, order, sw = lax.sort((flat_e, iota, gate_w.reshape(n)), num_keys=1, is_stable=True)
    _, inv = lax.sort((order, iota), num_keys=1)
    tok = order // TOP_K
    xs = h2[tok + (tok // t_per_b + 1) * skip]
    counts = jnp.sum((flat_e[:, None] == jnp.arange(N_EXPERTS, dtype=jnp.int32)[None, :]).astype(jnp.int32), axis=0)
    end = jnp.cumsum(counts)
    start = end - counts
    first_tile = start // MOE_BLOCK
    n_items = jnp.where(counts > 0, (end - 1) // MOE_BLOCK - first_tile + 1, 0)
    items_end = jnp.cumsum(n_items)
    w = jnp.arange(n_tiles + N_EXPERTS, dtype=jnp.int32)
    valid = w < items_end[-1]
    e_w = jnp.minimum(jnp.sum((w[:, None] >= items_end[None, :]).astype(jnp.int32), axis=1), N_EXPERTS - 1)
    tile_w = first_tile[e_w] + (w - (items_end[e_w] - n_items[e_w]))
    lo = jnp.where(valid, jnp.maximum(start[e_w], tile_w * MOE_BLOCK), 0)
    hi = jnp.where(valid, jnp.minimum(end[e_w], (tile_w + 1) * MOE_BLOCK), 0)
    tile_w = jnp.where(valid, tile_w, n_tiles - 1)
    first = jnp.concatenate([jnp.ones((1,), jnp.int32), (tile_w[1:] != tile_w[:-1]).astype(jnp.int32)])
    newexp = jnp.concatenate([jnp.ones((1,), jnp.int32), (e_w[1:] != e_w[:-1]).astype(jnp.int32)])
    ys = _moe_experts(xs, sw, tile_w.astype(jnp.int32), e_w, lo.astype(jnp.int32), hi.astype(jnp.int32), first, newexp, li, w1, b1, w2, b2)
    return ys[inv.reshape(t, TOP_K).T]


def kernel(x, c, ctx, c_ctx, w_mod, b_mod, norm_mix_pre, norm_mix_post, norm_ffn_pre, norm_ffn_post, w_in, ssm_conv_w, ssm_conv_b, ssm_dt_bias, ssm_a_log, ssm_d, ssm_norm, hy_conv_w, hy_conv_b, hy_w1, hy_b1, hy_w2, hy_b2, hy_w3, hy_b3, hy_freq, hy_bias, hg_lb_logits, hg_norm, w_br_ssm, w_br_hy, w_br_hg, w_out, router_w, router_b, exp_w1, exp_b1, exp_w2, exp_b2):
    hp = lax.Precision.HIGHEST
    b, n_lat, d = x.shape
    n_ctx = ctx.shape[1]
    lb = jax.nn.softmax(hg_lb_logits.astype(F32), axis=1)
    lb = jnp.cumsum(lb, axis=1) - lb[:, :1]
    silu_c = jax.nn.silu(c)
    silu_cc = jax.nn.silu(c_ctx)
    xs = jnp.concatenate([ctx, x], axis=1)
    for li in range(DEPTH):
        mx = (jnp.dot(silu_c, w_mod[li], precision=hp) + b_mod[li]).reshape(b, 1, 6, d)
        mc = jnp.broadcast_to((jnp.dot(silu_cc, w_mod[li], precision=hp) + b_mod[li]).reshape(1, 1, 6, d), (b, 1, 6, d))
        mods = jnp.pad(jnp.concatenate([mc, mx], axis=1), ((0, 0), (0, 0), (0, MOD_ROWS - 6), (0, 0)))
        h = _norm_mod(xs, norm_mix_pre[li], mods, n_ctx)
        ym, yh, og, p_rm, col_g, col_gate = _mixer_branches(h, w_in[li], lb[:, li], n_ctx, ssm_conv_w[li], ssm_conv_b[li], ssm_dt_bias[li], ssm_a_log[li], ssm_d[li], ssm_norm[li], hy_conv_w[li], hy_conv_b[li], hy_w1[li], hy_b1[li], hy_w2[li], hy_b2[li], hy_w3[li], hy_b3[li], hy_freq[li], hy_bias[li], hg_norm[li])
        xs, h_ffn, logits = _merge(ym, yh, og, p_rm, col_g, col_gate, xs, mods, w_br_ssm[li], w_br_hy[li], w_br_hg[li], w_out[li], norm_mix_post[li], norm_ffn_pre[li], router_w[li], router_b[li], n_ctx)
        skip = n_ctx if li == DEPTH - 1 else 0
        t = xs.shape[1] - skip
        f4 = _moe_ffn(h_ffn.reshape(-1, d), logits[:, skip:, :N_EXPERTS].reshape(b * t, N_EXPERTS), t, skip, li, exp_w1, exp_b1[li], exp_w2, exp_b2[li])
        xs = _post_ffn(f4.reshape(TOP_K, b, t, d), xs, mods, norm_ffn_post[li], n_ctx, skip)
    return xs
```

```python
import functools
import math

import jax
import jax.numpy as jnp
import numpy as np
from jax import lax
from jax.experimental import pallas as pl
from jax.experimental.pallas import tpu as pltpu

D_MODEL = 1024
DEPTH = 2
GRID_W = 64

SSM_HEADS = 16
SSM_HEAD_DIM = 64
SSM_INNER = SSM_HEADS * SSM_HEAD_DIM
SSM_STATE = 128
SSM_GROUPS = 4
SSD_CHUNK = 128
SSM_XBC = SSM_INNER + 2 * SSM_GROUPS * SSM_STATE

HY_WIDTH = D_MODEL
HY_ORDER = 2
HY_BANDS = 16
HY_FAST_DECAY = 0.3
HY_SLOW_DECAY = 1.5
HY_DECAY_TARGET = 1e-2

HG_HEADS = 8
HG_KDIM = 128
HG_VDIM = D_MODEL // HG_HEADS
HG_QK = HG_HEADS * HG_KDIM
HG_V = HG_HEADS * HG_VDIM
HG_CHUNK = 64
F_FLOOR = 1e-20

N_EXPERTS = 32
TOP_K = 4
D_FF = D_MODEL
SWIGLU_LIMIT = 7.0
SWIGLU_ALPHA = 1.702
MOE_BLOCK = 512

N_BRANCHES = 3
IN_SIZES = (SSM_INNER, SSM_XBC, 2 * SSM_HEADS, (HY_ORDER + 1) * HY_WIDTH, HG_QK, 2 * HG_QK, HG_V, HG_V, N_BRANCHES * D_MODEL)
EPS = 1e-6
F32 = jnp.float32
BF16 = jnp.bfloat16

LANES = 128
VMEM_LIMIT = 56 * 1024 * 1024


def _mm_kernel(a_ref, b_ref, o_ref):
    o_ref[...] = jnp.dot(a_ref[...], b_ref[...], preferred_element_type=F32).astype(o_ref.dtype)


def _mm(a, b, out_dtype=F32, tm=1024, tn=2048):
    m, k = a.shape
    n = b.shape[1]
    tm = math.gcd(m, tm)
    tn = math.gcd(n, tn)
    assert tm % SUBLANES == 0 and tn % LANES == 0, (m, n, tm, tn)
    return pl.pallas_call(
        _mm_kernel,
        grid=(n // tn, m // tm),
        in_specs=[pl.BlockSpec((tm, k), lambda j, i: (i, 0)), pl.BlockSpec((k, tn), lambda j, i: (0, j))],
        out_specs=pl.BlockSpec((tm, tn), lambda j, i: (i, j)),
        out_shape=jax.ShapeDtypeStruct((m, n), out_dtype),
        compiler_params=pltpu.CompilerParams(dimension_semantics=("arbitrary", "arbitrary"), vmem_limit_bytes=VMEM_LIMIT),
        name="dense_mm",
    )(a, b)


def _moe_kernel(tile_ref, exp_ref, lo_ref, hi_ref, first_ref, newexp_ref, x_ref, sw_ref, w1_ref, b1_ref, w2_ref, b2_ref, o_ref, w1b_ref, w2b_ref):
    del exp_ref
    w = pl.program_id(0)
    lo, hi = lo_ref[w], hi_ref[w]

    @pl.when(newexp_ref[w] == 1)
    def _():
        def cast_rows(i, carry):
            rows = pl.ds(pl.multiple_of(i * LANES, LANES), LANES)
            w1b_ref[rows, :] = w1_ref[rows, :].astype(BF16)
            w2b_ref[rows, :] = w2_ref[rows, :].astype(BF16)
            return carry

        lax.fori_loop(0, D_MODEL // LANES, cast_rows, 0)

    @pl.when(hi > lo)
    def _():
        hh = jnp.dot(x_ref[...], w1b_ref[...], preferred_element_type=F32) + b1_ref[...]
        g = jnp.minimum(hh[:, :D_FF], SWIGLU_LIMIT)
        u = jnp.clip(hh[:, D_FF:], -SWIGLU_LIMIT, SWIGLU_LIMIT)
        act = (u + 1.0) * g * jax.nn.sigmoid(SWIGLU_ALPHA * g)
        y = jnp.dot(act.astype(BF16), w2b_ref[...], preferred_element_type=F32) + b2_ref[...]
        y = (y * sw_ref[...]).astype(o_ref.dtype)
        rows = tile_ref[w] * MOE_BLOCK + lax.broadcasted_iota(jnp.int32, (MOE_BLOCK, 1), 0)
        mine = jnp.logical_and(rows >= lo, rows < hi)

        @pl.when(first_ref[w] == 1)
        def _():
            o_ref[...] = jnp.where(mine, y, jnp.zeros_like(y))

        @pl.when(first_ref[w] != 1)
        def _():
            o_ref[...] = jnp.where(mine, y, o_ref[...])


def _moe_experts(xs, sw, tile_w, exp_w, lo, hi, first, newexp, li, w1, b1, w2, b2):
    n, d = xs.shape
    grid_spec = pltpu.PrefetchScalarGridSpec(
        num_scalar_prefetch=6,
        grid=(tile_w.shape[0],),
        in_specs=[
            pl.BlockSpec((MOE_BLOCK, d), lambda w, tl, ex, lo_, hi_, fi, ne: (tl[w], 0)),
            pl.BlockSpec((MOE_BLOCK, 1), lambda w, tl, ex, lo_, hi_, fi, ne: (tl[w], 0)),
            pl.BlockSpec((None, None, d, 2 * D_FF), lambda w, tl, ex, lo_, hi_, fi, ne: (li, ex[w], 0, 0)),
            pl.BlockSpec((None, 1, 2 * D_FF), lambda w, tl, ex, lo_, hi_, fi, ne: (ex[w], 0, 0)),
            pl.BlockSpec((None, None, D_FF, d), lambda w, tl, ex, lo_, hi_, fi, ne: (li, ex[w], 0, 0)),
            pl.BlockSpec((None, 1, d), lambda w, tl, ex, lo_, hi_, fi, ne: (ex[w], 0, 0)),
        ],
        out_specs=pl.BlockSpec((MOE_BLOCK, d), lambda w, tl, ex, lo_, hi_, fi, ne: (tl[w], 0)),
        scratch_shapes=[pltpu.VMEM((d, 2 * D_FF), BF16), pltpu.VMEM((D_FF, d), BF16)],
    )
    return pl.pallas_call(
        _moe_kernel,
        grid_spec=grid_spec,
        out_shape=jax.ShapeDtypeStruct((n, d), BF16),
        compiler_params=pltpu.CompilerParams(dimension_semantics=("arbitrary",), vmem_limit_bytes=VMEM_LIMIT),
        name="moe_experts",
    )(tile_w, exp_w, lo, hi, first, newexp, xs, sw.reshape(n, 1), w1, b1.reshape(N_EXPERTS, 1, 2 * D_FF), w2, b2.reshape(N_EXPERTS, 1, d))


SUBLANES = 8
NEG_BIG = -1e30
HG_STEP_CHUNKS = 4
HIER_LEVELS = (64, 32, 16, 8, 4)
LOG2E = math.log2(math.e)


def _split3(x):
    h1 = x.astype(BF16)
    r1 = x - h1.astype(F32)
    h2 = r1.astype(BF16)
    h3 = (r1 - h2.astype(F32)).astype(BF16)
    return h1, h2, h3


def _dot_nt(a, b):
    return lax.dot_general(a, b, (((1,), (1,)), ((), ())), preferred_element_type=F32)


def _gla_kernel(q_ref, a_ref, v_ref, lb_ref, *rest, reverse):
    if reverse:
        of_ref, w_ref, o_ref, st_ref, at_ref = rest
    else:
        o_ref, st_ref, at_ref = rest
    Q = HG_CHUNK

    @pl.when(pl.program_id(1) == 0)
    def _():
        st_ref[...] = jnp.zeros_like(st_ref)

    row = lax.broadcasted_iota(jnp.int32, (Q, Q), 0)
    col = lax.broadcasted_iota(jnp.int32, (Q, Q), 1)
    tri = jnp.where((col >= row) if reverse else (col <= row), 1.0, 0.0).astype(BF16)
    same_block = {s: (row // s) == (col // s) for s in HIER_LEVELS[1:] + (2, 1)}
    rowk = lax.broadcasted_iota(jnp.int32, (Q, HG_KDIM), 0)
    q_rows = {s: ((rowk % s) < s // 2) if reverse else ((rowk % s) >= s // 2) for s in HIER_LEVELS + (2,)}
    of_w = (of_ref, w_ref) if reverse else (None, None)
    for sub in (range(HG_STEP_CHUNKS - 1, -1, -1) if reverse else range(HG_STEP_CHUNKS)):
        _gla_chunk(sub, q_ref, a_ref, v_ref, lb_ref[...], of_w, o_ref, st_ref, at_ref, tri, same_block, q_rows, reverse)


def _gla_chunk(sub, q_ref, a_ref, v_ref, lb, of_w, o_ref, st_ref, at_ref, tri, same_block, q_rows, reverse):
    Q = HG_CHUNK
    rs = slice(sub * Q, (sub + 1) * Q)
    of_ref, w_ref = of_w
    shp3 = (Q // SUBLANES, SUBLANES, HG_KDIM)
    sub3 = lax.broadcasted_iota(jnp.int32, shp3, 1)
    a = a_ref[rs, :].astype(F32)
    f_all = jnp.maximum(lb + (1.0 - lb) * jax.nn.sigmoid(a), F_FLOOR)
    kk = (1.0 - lb) * jax.nn.sigmoid(-a)
    g_all = sum(jnp.dot(tri, p, preferred_element_type=F32) for p in _split3(jnp.log(f_all) * LOG2E))
    q_all = q_ref[rs, :].astype(F32)
    q_all = q_all * jax.nn.sigmoid(q_all)
    v_all = v_ref[rs, :]
    tot = 0 if reverse else Q - 1

    def level_ref(g, s):
        half = s // 2
        m_off = half if reverse else half - 1
        if s >= 2 * SUBLANES:
            return jnp.concatenate([jnp.broadcast_to(g[b0 + m_off:b0 + m_off + 1, :], (s, HG_KDIM)) for b0 in range(0, Q, s)], axis=0)
        g3 = g.reshape(shp3)
        ref = jnp.broadcast_to(g3[:, m_off:m_off + 1, :], shp3)
        for b0 in range(s, SUBLANES, s):
            ref = jnp.where(sub3 >= b0, jnp.broadcast_to(g3[:, b0 + m_off:b0 + m_off + 1, :], shp3), ref)
        return ref.reshape(Q, HG_KDIM)

    for h in range(HG_HEADS):
        sl = slice(h * HG_KDIM, (h + 1) * HG_KDIM)
        g, qh, kh, fh = g_all[:, sl], q_all[:, sl], kk[:, sl], f_all[:, sl]
        attn = None
        for s in HIER_LEVELS:
            gref = level_ref(g, s)
            eq = jnp.exp2(jnp.where(q_rows[s], g - gref, NEG_BIG))
            ek = jnp.exp2(jnp.where(q_rows[s], NEG_BIG, gref - g))
            lvl = _dot_nt((qh * eq).astype(BF16), (kh * ek).astype(BF16))
            attn = lvl if attn is None else jnp.where(same_block[s], lvl, attn)
        lvl = _dot_nt(jnp.where(q_rows[2], qh * fh, 0.0).astype(BF16), jnp.where(q_rows[2], 0.0, kh).astype(BF16))
        attn = jnp.where(same_block[2], lvl, attn)
        attn = jnp.where(same_block[1], _dot_nt(qh.astype(BF16), kh.astype(BF16)), attn)
        at_ref[sub, h] = attn.astype(BF16)

    for h in range(HG_HEADS):
        sl = slice(h * HG_KDIM, (h + 1) * HG_KDIM)
        g, qh, kh, vb = g_all[:, sl], q_all[:, sl], kk[:, sl], v_all[:, sl]
        g_tot = g[tot:tot + 1, :]
        st = st_ref[h]
        o = _dot_nt((qh * jnp.exp2(g)).astype(BF16), st.astype(BF16))
        o = o + jnp.dot(at_ref[sub, h], vb, preferred_element_type=F32)
        k_st = (kh * jnp.exp2(g_tot - g)).astype(BF16)
        st_ref[h] = st * jnp.exp2(g_tot) + jnp.dot(vb.astype(F32).T.astype(BF16), k_st, preferred_element_type=F32)
        if reverse:
            o = o + of_ref[rs, sl]
            o = o * lax.rsqrt(jnp.mean(o * o, axis=-1, keepdims=True) + EPS) * w_ref[:, sl]
        o_ref[rs, sl] = o.astype(o_ref.dtype)


def _hgrn_scan(p_cm, lb, norm_w, n_ctx):
    b, t, _ = p_cm.shape
    col_q, col_f, col_i = 0, 1, 3
    step_rows = HG_STEP_CHUNKS * HG_CHUNK
    assert n_ctx % step_rows == 0 and t % step_rows == 0, (n_ctx, t, step_rows)
    nc, ncc = t // step_rows, n_ctx // step_rows
    blk = (None, step_rows, HG_QK)
    scratch = [pltpu.VMEM((HG_HEADS, HG_VDIM, HG_KDIM), F32), pltpu.VMEM((HG_STEP_CHUNKS, HG_HEADS, HG_CHUNK, HG_CHUNK), BF16)]
    params = pltpu.CompilerParams(dimension_semantics=("arbitrary", "arbitrary"), vmem_limit_bytes=VMEM_LIMIT)
    row_spec = pl.BlockSpec((1, HG_QK), lambda bi, s: (0, 0))

    def fwd_chunk(s):
        return s

    def bwd_chunk(s):
        return jnp.where(s < ncc, ncc - 1 - s, nc + ncc - 1 - s)

    o_f = pl.pallas_call(
        functools.partial(_gla_kernel, reverse=False),
        grid=(b, nc),
        in_specs=[
            pl.BlockSpec(blk, lambda bi, s: (bi, fwd_chunk(s), col_q)),
            pl.BlockSpec(blk, lambda bi, s: (bi, fwd_chunk(s), col_f)),
            pl.BlockSpec(blk, lambda bi, s: (bi, fwd_chunk(s), col_i)),
            row_spec,
        ],
        out_specs=pl.BlockSpec(blk, lambda bi, s: (bi, fwd_chunk(s), 0)),
        out_shape=jax.ShapeDtypeStruct((b, t, HG_V), F32),
        scratch_shapes=scratch,
        compiler_params=params,
        name="gla_fwd",
    )(p_cm, p_cm, p_cm, lb[0:1])
    return pl.pallas_call(
        functools.partial(_gla_kernel, reverse=True),
        grid=(b, nc),
        in_specs=[
            pl.BlockSpec(blk, lambda bi, s: (bi, bwd_chunk(s), col_q)),
            pl.BlockSpec(blk, lambda bi, s: (bi, bwd_chunk(s), col_f + 1)),
            pl.BlockSpec(blk, lambda bi, s: (bi, bwd_chunk(s), col_i)),
            row_spec,
            pl.BlockSpec(blk, lambda bi, s: (bi, bwd_chunk(s), 0)),
            row_spec,
        ],
        out_specs=pl.BlockSpec(blk, lambda bi, s: (bi, bwd_chunk(s), 0)),
        out_shape=jax.ShapeDtypeStruct((b, t, HG_V), BF16),
        scratch_shapes=scratch,
        compiler_params=params,
        name="gla_bwd",
    )(p_cm, p_cm, p_cm, lb[1:2], o_f, norm_w.reshape(1, HG_V))


CONV_TILE = 256
HALO = 16


def _dwconv_kernel(prev_ref, cur_ref, next_ref, w_ref, b_ref, *o_refs, taps, n_ctx_tiles, n_tiles, silu):
    i = pl.program_id(2)
    first = jnp.logical_or(i == 0, i == n_ctx_tiles)
    last = jnp.logical_or(i == n_ctx_tiles - 1, i == n_tiles - 1)
    pad = taps // 2
    xp = jnp.where(first, 0.0, prev_ref[...].astype(F32))
    xn = jnp.where(last, 0.0, next_ref[...].astype(F32))
    xcat = jnp.concatenate([xp, cur_ref[...].astype(F32), xn], axis=0)
    acc = jnp.broadcast_to(b_ref[...], cur_ref.shape).astype(F32)
    for k in range(taps):
        off = HALO - pad + k
        acc = acc + w_ref[k:k + 1, :] * xcat[off:off + CONV_TILE, :]
    if silu:
        acc = acc * jax.nn.sigmoid(acc)
    if len(o_refs) == 1:
        o_refs[0][...] = acc.astype(o_refs[0].dtype)
    else:
        ctx_ref, lat_ref = o_refs

        @pl.when(i < n_ctx_tiles)
        def _():
            ctx_ref[...] = acc.astype(ctx_ref.dtype)

        @pl.when(i >= n_ctx_tiles)
        def _():
            lat_ref[...] = acc.astype(lat_ref.dtype)


def _dwconv_stream(x, w, bias, n_ctx, silu, c0=0, ct=1024, split=False):
    b, t, _ = x.shape
    taps, c = w.shape
    n_tiles = t // CONV_TILE
    nct = n_ctx // CONV_TILE
    hb = CONV_TILE // HALO
    n_halo = t // HALO
    j0 = c0 // ct
    kern = functools.partial(_dwconv_kernel, taps=taps, n_ctx_tiles=nct, n_tiles=n_tiles, silu=silu)
    if split:
        out_specs = [pl.BlockSpec((None, CONV_TILE, ct), lambda bi, j, i: (bi, jnp.minimum(i, nct - 1), j)),
                     pl.BlockSpec((None, CONV_TILE, ct), lambda bi, j, i: (bi, jnp.maximum(i - nct, 0), j))]
        out_shape = [jax.ShapeDtypeStruct((b, n_ctx, c), BF16), jax.ShapeDtypeStruct((b, t - n_ctx, c), BF16)]
    else:
        out_specs = pl.BlockSpec((None, CONV_TILE, ct), lambda bi, j, i: (bi, i, j))
        out_shape = jax.ShapeDtypeStruct((b, t, c), BF16)
    return pl.pallas_call(
        kern,
        grid=(b, c // ct, n_tiles),
        in_specs=[
            pl.BlockSpec((None, HALO, ct), lambda bi, j, i: (bi, jnp.maximum(i * hb - 1, 0), j0 + j)),
            pl.BlockSpec((None, CONV_TILE, ct), lambda bi, j, i: (bi, i, j0 + j)),
            pl.BlockSpec((None, HALO, ct), lambda bi, j, i: (bi, jnp.minimum((i + 1) * hb, n_halo - 1), j0 + j)),
            pl.BlockSpec((taps, ct), lambda bi, j, i: (0, j)),
            pl.BlockSpec((1, ct), lambda bi, j, i: (0, j)),
        ],
        out_specs=out_specs,
        out_shape=out_shape,
        compiler_params=pltpu.CompilerParams(dimension_semantics=("arbitrary",) * 3, vmem_limit_bytes=VMEM_LIMIT),
        name="dwconv",
    )(x, x, x, w.astype(F32), bias.reshape(1, c).astype(F32))


SSD_STEP_CHUNKS = 2
SSM_GHEADS = SSM_HEADS // SSM_GROUPS
SSM_GP = SSM_GHEADS * SSM_HEAD_DIM


def _ssd_kernel(xbc_ref, dt_ref, dtb_ref, a_ref, *rest, reverse):
    if reverse:
        yf_ref, z_ref, dsk_ref, nw_ref, o_ref, st_ref, m_ref, xd_ref, xst_ref, y_ref = rest
    else:
        o_ref, st_ref, m_ref, xd_ref, xst_ref, y_ref = rest
    Q = SSD_CHUNK

    @pl.when(pl.program_id(1) == 0)
    def _():
        st_ref[...] = jnp.zeros_like(st_ref)

    row = lax.broadcasted_iota(jnp.int32, (Q, Q), 0)
    col = lax.broadcasted_iota(jnp.int32, (Q, Q), 1)
    keep = (col >= row) if reverse else (col <= row)
    tri = jnp.where(keep, 1.0, 0.0).astype(BF16)
    expand = jnp.where(lax.broadcasted_iota(jnp.int32, (LANES, SSM_INNER), 1) // SSM_HEAD_DIM == lax.broadcasted_iota(jnp.int32, (LANES, SSM_INNER), 0), 1.0, 0.0).astype(BF16)

    rev_refs = (yf_ref, z_ref, dsk_ref, nw_ref) if reverse else None
    for sub in (range(SSD_STEP_CHUNKS - 1, -1, -1) if reverse else range(SSD_STEP_CHUNKS)):
        _ssd_chunk(sub, xbc_ref, dt_ref, dtb_ref[...], a_ref[...], rev_refs, o_ref, st_ref, m_ref, xd_ref, xst_ref, y_ref, keep, tri, expand, reverse)


def _ssd_chunk(sub, xbc_ref, dt_ref, dt_bias, a_neg, rev_refs, o_ref, st_ref, m_ref, xd_ref, xst_ref, y_ref, keep, tri, expand, reverse):
    Q = SSD_CHUNK
    rs = slice(sub * Q, (sub + 1) * Q)
    dt = jax.nn.softplus(dt_ref[rs, :] + dt_bias)
    a = dt * a_neg
    cs = sum(jnp.dot(tri, p, preferred_element_type=F32) for p in _split3(a))
    tot = 0 if reverse else Q - 1
    cs_tot = cs[tot:tot + 1, :]
    cs_t = cs.T
    dt_e = jnp.dot(dt.astype(BF16), expand, preferred_element_type=F32)
    e_in = jnp.dot(jnp.exp(cs).astype(BF16), expand, preferred_element_type=F32)
    e_st = jnp.dot(jnp.exp(cs_tot - cs).astype(BF16), expand, preferred_element_type=F32)
    e_tot = jnp.dot(jnp.broadcast_to(jnp.exp(cs_tot), (SUBLANES, LANES)).astype(BF16), expand, preferred_element_type=F32)[0:1, :]

    xs = xbc_ref[rs, :SSM_INNER].astype(F32)
    xd_ref[sub] = (xs * dt_e).astype(BF16)
    xst_ref[sub] = (xs * dt_e * e_st).astype(BF16)
    for g in range(SSM_GROUPS):
        bm = xbc_ref[rs, SSM_INNER + g * SSM_STATE:SSM_INNER + (g + 1) * SSM_STATE]
        cm = xbc_ref[rs, SSM_INNER + (SSM_GROUPS + g) * SSM_STATE:SSM_INNER + (SSM_GROUPS + g + 1) * SSM_STATE]
        cb = _dot_nt(cm, bm)
        for h in range(g * SSM_GHEADS, (g + 1) * SSM_GHEADS):
            diff = jnp.broadcast_to(cs[:, h:h + 1], (Q, Q)) - jnp.broadcast_to(cs_t[h:h + 1, :], (Q, Q))
            m_ref[sub, h] = (cb * jnp.exp(jnp.where(keep, diff, NEG_BIG))).astype(BF16)

    lane_lo = lax.broadcasted_iota(jnp.int32, (Q, LANES), 1) < SSM_HEAD_DIM
    for g in range(SSM_GROUPS):
        bm = xbc_ref[rs, SSM_INNER + g * SSM_STATE:SSM_INNER + (g + 1) * SSM_STATE]
        cm = xbc_ref[rs, SSM_INNER + (SSM_GROUPS + g) * SSM_STATE:SSM_INNER + (SSM_GROUPS + g + 1) * SSM_STATE]
        gl = slice(g * SSM_GP, (g + 1) * SSM_GP)
        st = st_ref[g]
        y_off = jnp.dot(cm, st.astype(BF16), preferred_element_type=F32) * e_in[:, gl]
        for hp in range(SSM_GHEADS // 2):
            h0 = g * SSM_GHEADS + 2 * hp
            lanes = slice((h0 // 2) * LANES, (h0 // 2 + 1) * LANES)
            pair = [jnp.dot(m_ref[sub, h], xd_ref[sub, :, lanes], preferred_element_type=F32) for h in (h0, h0 + 1)]
            y_ref[sub, :, lanes] = jnp.where(lane_lo, pair[0], pair[1]) + y_off[:, hp * LANES:(hp + 1) * LANES]
        st_ref[g] = st * e_tot[:, gl] + jnp.dot(bm.astype(F32).T.astype(BF16), xst_ref[sub, :, gl], preferred_element_type=F32)
    y = y_ref[sub]
    if reverse:
        yf_ref, z_ref, dsk_ref, nw_ref = rev_refs
        y = (y + yf_ref[rs, :] + xbc_ref[rs, :SSM_INNER].astype(F32) * dsk_ref[...])
        zz = z_ref[rs, :].astype(F32)
        y = y * (zz * jax.nn.sigmoid(zz))
        y = y * lax.rsqrt(jnp.mean(y * y, axis=-1, keepdims=True) + EPS) * nw_ref[...]
    o_ref[rs, :] = y.astype(o_ref.dtype)


def _ssd_scan(xbc_act, dt2, p_rm, z_col, dt_bias, a_log, d_skip, norm_w, n_ctx):
    b, t, _ = xbc_act.shape
    step_rows = SSD_STEP_CHUNKS * SSD_CHUNK
    assert n_ctx % step_rows == 0 and t % step_rows == 0, (n_ctx, t, step_rows)
    nc, ncc = t // step_rows, n_ctx // step_rows

    def pad_heads(v):
        return jnp.pad(v.astype(F32), ((0, 0), (0, LANES - SSM_HEADS)))

    dtb = pad_heads(dt_bias.reshape(2, SSM_HEADS))
    a_neg = pad_heads(-jnp.exp(a_log.astype(F32)))
    dsk = jnp.repeat(d_skip.astype(F32), SSM_HEAD_DIM).reshape(1, SSM_INNER)
    scratch = [pltpu.VMEM((SSM_GROUPS, SSM_STATE, SSM_GP), F32), pltpu.VMEM((SSD_STEP_CHUNKS, SSM_HEADS, SSD_CHUNK, SSD_CHUNK), BF16),
               pltpu.VMEM((SSD_STEP_CHUNKS, SSD_CHUNK, SSM_INNER), BF16), pltpu.VMEM((SSD_STEP_CHUNKS, SSD_CHUNK, SSM_INNER), BF16),
               pltpu.VMEM((SSD_STEP_CHUNKS, SSD_CHUNK, SSM_INNER), F32)]
    params = pltpu.CompilerParams(dimension_semantics=("arbitrary", "arbitrary"), vmem_limit_bytes=VMEM_LIMIT)

    def bwd_chunk(s):
        return jnp.where(s < ncc, ncc - 1 - s, nc + ncc - 1 - s)

    def specs(chunk, d):
        return [
            pl.BlockSpec((None, step_rows, SSM_XBC), lambda bi, s: (bi, chunk(s), 0)),
            pl.BlockSpec((None, step_rows, LANES), lambda bi, s: (bi, chunk(s), d)),
            pl.BlockSpec((1, LANES), lambda bi, s: (0, 0)),
            pl.BlockSpec((1, LANES), lambda bi, s: (0, 0)),
        ]

    def inner_spec(chunk):
        return pl.BlockSpec((None, step_rows, SSM_INNER), lambda bi, s: (bi, chunk(s), 0))

    row_spec = pl.BlockSpec((1, SSM_INNER), lambda bi, s: (0, 0))
    z_spec = pl.BlockSpec((None, step_rows, SSM_INNER), lambda bi, s: (bi, bwd_chunk(s), z_col // SSM_INNER))
    y_f = pl.pallas_call(
        functools.partial(_ssd_kernel, reverse=False),
        grid=(b, nc),
        in_specs=specs(lambda s: s, 0),
        out_specs=inner_spec(lambda s: s),
        out_shape=jax.ShapeDtypeStruct((b, t, SSM_INNER), F32),
        scratch_shapes=scratch,
        compiler_params=params,
        name="ssd_fwd",
    )(xbc_act, dt2, dtb[0:1], a_neg[0:1])
    return pl.pallas_call(
        functools.partial(_ssd_kernel, reverse=True),
        grid=(b, nc),
        in_specs=specs(bwd_chunk, 1) + [inner_spec(bwd_chunk), z_spec, row_spec, row_spec],
        out_specs=inner_spec(bwd_chunk),
        out_shape=jax.ShapeDtypeStruct((b, t, SSM_INNER), BF16),
        scratch_shapes=scratch,
        compiler_params=params,
        name="ssd_bwd",
    )(xbc_act, dt2, dtb[1:2], a_neg[1:2], y_f, p_rm, dsk, norm_w.reshape(1, SSM_INNER).astype(F32))


HY_N2 = LANES
HY_CT = LANES
VMEM_LIMIT_HYENA = 60 * 1024 * 1024


def _hy_dims(L):
    n1 = 2 * L // HY_N2
    k1n = n1 // 2 + 1
    k1p = -(-k1n // SUBLANES) * SUBLANES
    return n1, k1n, k1p


def _hy_tables(L, n1_rows):
    n1, k1n, k1p = _hy_dims(L)
    n = 2 * L
    k1 = np.arange(k1n, dtype=np.float64)[None, :, None]
    nn = (HY_N2 * np.arange(n1_rows, dtype=np.float64)[None, None, :] + np.arange(HY_N2, dtype=np.float64)[:, None, None])
    ang = 2.0 * np.pi * ((k1 * nn) % n) / n
    m1 = np.zeros((HY_N2, 2 * k1p, n1_rows), np.float32)
    m1[:, :k1n] = np.cos(ang)
    m1[:, k1p:k1p + k1n] = -np.sin(ang)
    m4 = np.transpose(m1, (0, 2, 1))
    kk = np.arange(HY_N2, dtype=np.float64)
    a2 = 2.0 * np.pi * ((kk[:, None] * kk[None, :]) % HY_N2) / HY_N2
    c, s = np.cos(a2), np.sin(a2)
    f3 = np.block([[c, s], [-s, c]]).astype(np.float32)
    f3i = np.block([[c, -s], [s, c]]).astype(np.float32)
    return jnp.asarray(m1, BF16), jnp.asarray(m4, BF16), jnp.asarray(f3, BF16), jnp.asarray(f3i, BF16)


def _hy_stage1(u_ref, a_ref, m1_ref, n1_rows, k1p):
    def body(n2, carry):
        xs = u_ref[pl.ds(n2, n1_rows, stride=HY_N2), :].astype(BF16)
        a_ref[pl.ds(pl.multiple_of(n2 * 2 * k1p, 2 * k1p), 2 * k1p), :] = jnp.dot(m1_ref[n2], xs, preferred_element_type=F32)
        return carry

    lax.fori_loop(0, HY_N2, body, 0, unroll=8)


def _hy_spectrum_slab(a_ref, f3_ref, k1s, k1p):
    blk = jnp.concatenate([jnp.concatenate([a_ref[pl.ds(k1, HY_N2, stride=2 * k1p), :], a_ref[pl.ds(k1p + k1, HY_N2, stride=2 * k1p), :]], axis=0) for k1 in k1s], axis=1)
    return jnp.dot(f3_ref[...], blk.astype(BF16), preferred_element_type=F32)


def _hy_k1_loop(body, k1n):
    def pair(p, carry):
        body((2 * p, 2 * p + 1))
        return carry

    lax.fori_loop(0, (k1n - 1) // 2, pair, 0, unroll=math.gcd((k1n - 1) // 2, 4))
    body((k1n - 1,))


def _hy_slab_rows(k1):
    start = k1 * 2 * HY_N2
    return pl.ds(start if isinstance(k1, int) else pl.multiple_of(start, 2 * HY_N2), 2 * HY_N2)


def _hy_conv(u_ref, yo_ref, a_ref, y_ref, h_ref, order, m1_ref, m4_ref, f3_ref, f3i_ref, n1_rows, k1n, k1p):
    _hy_stage1(u_ref, a_ref, m1_ref, n1_rows, k1p)

    def stage2(k1s):
        x = _hy_spectrum_slab(a_ref, f3_ref, k1s, k1p)
        h = jnp.concatenate([h_ref[order, _hy_slab_rows(k1), :] for k1 in k1s], axis=1).astype(F32)
        xr, xi, hr, hi = x[:HY_N2], x[HY_N2:], h[:HY_N2], h[HY_N2:]
        z = jnp.concatenate([xr * hr - xi * hi, xr * hi + xi * hr], axis=0).astype(BF16)
        c = jnp.dot(f3i_ref[...], z, preferred_element_type=F32)
        for i, k1 in enumerate(k1s):
            lanes = slice(i * HY_CT, (i + 1) * HY_CT)
            y_ref[pl.ds(k1, HY_N2, stride=2 * k1p), :] = c[:HY_N2, lanes]
            y_ref[pl.ds(k1p + k1, HY_N2, stride=2 * k1p), :] = c[HY_N2:, lanes]

    _hy_k1_loop(stage2, k1n)

    def stage3(n2, carry):
        d = y_ref[pl.ds(pl.multiple_of(n2 * 2 * k1p, 2 * k1p), 2 * k1p), :].astype(BF16)
        yo_ref[pl.ds(n2, n1_rows, stride=HY_N2), :] = jnp.dot(m4_ref[n2], d, preferred_element_type=F32)
        return carry

    lax.fori_loop(0, HY_N2, stage3, 0, unroll=8)


def _hyena_kernel(v_ref, x1_ref, x2_ref, h_ref, m1_ref, m4_ref, f3_ref, f3i_ref, bias_ref, o_ref, a_ref, y_ref, u_ref, yo_ref, *, n1_rows, k1n, k1p):
    @pl.when(jnp.logical_and(pl.program_id(0) == 0, pl.program_id(1) == 0))
    def _():
        y_ref[...] = jnp.zeros_like(y_ref)

    u_ref[...] = v_ref[...].astype(F32)
    for order, gate_ref in enumerate((x1_ref, x2_ref)):
        _hy_conv(u_ref, yo_ref, a_ref, y_ref, h_ref, order, m1_ref, m4_ref, f3_ref, f3i_ref, n1_rows, k1n, k1p)
        z = gate_ref[...].astype(F32) * (yo_ref[...] + u_ref[...] * bias_ref[order:order + 1, :])
        if order == 0:
            u_ref[...] = z
        else:
            o_ref[...] = z.astype(o_ref.dtype)


def _hy_filter_kernel(hid_ref, wf_ref, wb_ref, bf_ref, bb_ref, dl_ref, m1_ref, f3_ref, o_ref, a_ref, sf_ref, uf_ref, ub_ref, *, n1_rows, k1n, k1p, scale_mid, scale_edge):
    L = uf_ref.shape[0]
    hid_rows = math.gcd(L, 2 * HY_N2)

    def fill(w_ref, b_ref, dst_ref, drop_first):
        w = _split3(w_ref[...])

        def body(i, energy):
            rows = pl.ds(pl.multiple_of(i * hid_rows, hid_rows), hid_rows)
            hid = _split3(hid_ref[rows, :])
            acc = sum(jnp.dot(hid[p], w[q], preferred_element_type=F32) for p in range(2) for q in range(2 - p))
            pos = i * hid_rows + lax.broadcasted_iota(jnp.int32, (hid_rows, HY_CT), 0)
            vals = (acc + b_ref[...]) * jnp.exp(pos.astype(F32) * (-1.0 / (L - 1)) * dl_ref[...])
            if drop_first:
                vals = jnp.where(pos == 0, 0.0, vals)
            dst_ref[rows, :] = vals
            return energy + jnp.sum(vals * vals, axis=0, keepdims=True)

        return lax.fori_loop(0, L // hid_rows, body, jnp.zeros((1, HY_CT), F32), unroll=math.gcd(L // hid_rows, 4))

    norm = lax.rsqrt(fill(wf_ref, bf_ref, uf_ref, False) + fill(wb_ref, bb_ref, ub_ref, True) + EPS)

    _hy_stage1(uf_ref, a_ref, m1_ref, n1_rows, k1p)

    def keep_fwd(k1s):
        x = _hy_spectrum_slab(a_ref, f3_ref, k1s, k1p)
        for i, k1 in enumerate(k1s):
            sf_ref[_hy_slab_rows(k1), :] = x[:, i * HY_CT:(i + 1) * HY_CT]

    _hy_k1_loop(keep_fwd, k1n)
    _hy_stage1(ub_ref, a_ref, m1_ref, n1_rows, k1p)

    def combine(k1s):
        xb_all = _hy_spectrum_slab(a_ref, f3_ref, k1s, k1p)
        for i, k1 in enumerate(k1s):
            xb = xb_all[:, i * HY_CT:(i + 1) * HY_CT]
            xf = sf_ref[_hy_slab_rows(k1), :]
            w = norm * jnp.where(jnp.logical_or(k1 == 0, k1 == k1n - 1), scale_edge, scale_mid)
            h = jnp.concatenate([xf[:HY_N2] + xb[:HY_N2], xf[HY_N2:] - xb[HY_N2:]], axis=0)
            o_ref[_hy_slab_rows(k1), :] = (h * w).astype(o_ref.dtype)

    _hy_k1_loop(combine, k1n)


def _single(block_shape, index_map):
    return pl.BlockSpec(block_shape, index_map, pipeline_mode=pl.Buffered(1))


def _hyena_filter_spectrum_pallas(hidden, w3, b3, deltas):
    L, ffn = hidden.shape
    c = deltas.shape[0]
    nct = c // HY_CT
    n = 2 * L
    n1, k1n, k1p = _hy_dims(L)
    n1_rows = L // HY_N2
    m1, _, f3, _ = _hy_tables(L, n1_rows)
    kern = functools.partial(_hy_filter_kernel, n1_rows=n1_rows, k1n=k1n, k1p=k1p, scale_mid=2.0 / n, scale_edge=1.0 / n)
    return pl.pallas_call(
        kern,
        grid=(HY_ORDER, nct),
        in_specs=[
            _single((L, ffn), lambda o, j: (0, 0)),
            pl.BlockSpec((ffn, HY_CT), lambda o, j: (0, o * nct + j)),
            pl.BlockSpec((ffn, HY_CT), lambda o, j: (0, (HY_ORDER + o) * nct + j)),
            pl.BlockSpec((1, HY_CT), lambda o, j: (0, o * nct + j)),
            pl.BlockSpec((1, HY_CT), lambda o, j: (0, (HY_ORDER + o) * nct + j)),
            pl.BlockSpec((1, HY_CT), lambda o, j: (0, j)),
            _single((HY_N2, 2 * k1p, n1_rows), lambda o, j: (0, 0, 0)),
            _single((2 * HY_N2, 2 * HY_N2), lambda o, j: (0, 0)),
        ],
        out_specs=pl.BlockSpec((None, k1n * 2 * HY_N2, HY_CT), lambda o, j: (o, 0, j)),
        out_shape=jax.ShapeDtypeStruct((HY_ORDER, k1n * 2 * HY_N2, c), BF16),
        scratch_shapes=[
            pltpu.VMEM((k1p * 2 * HY_N2, HY_CT), F32),
            pltpu.VMEM((k1n * 2 * HY_N2, HY_CT), F32),
            pltpu.VMEM((L, HY_CT), F32),
            pltpu.VMEM((L, HY_CT), F32),
        ],
        compiler_params=pltpu.CompilerParams(dimension_semantics=("arbitrary", "arbitrary"), vmem_limit_bytes=VMEM_LIMIT_HYENA),
        name="hyena_filter_dft",
    )(hidden, w3, w3, b3.reshape(1, -1), b3.reshape(1, -1), deltas.reshape(1, c), m1, f3)


def _hyena_long(hy, h_spec, bias):
    b, L, c3 = hy.shape
    c = c3 // (HY_ORDER + 1)
    nct = c // HY_CT
    n1, k1n, k1p = _hy_dims(L)
    n1_rows = L // HY_N2
    m1, m4, f3, f3i = _hy_tables(L, n1_rows)
    kern = functools.partial(_hyena_kernel, n1_rows=n1_rows, k1n=k1n, k1p=k1p)

    def col(part):
        return _single((None, L, HY_CT), lambda j, bi: (bi, 0, part * nct + j))

    return pl.pallas_call(
        kern,
        grid=(nct, b),
        in_specs=[
            col(0), col(1), col(2),
            _single((HY_ORDER, k1n * 2 * HY_N2, HY_CT), lambda j, bi: (0, 0, j)),
            _single((HY_N2, 2 * k1p, n1_rows), lambda j, bi: (0, 0, 0)),
            _single((HY_N2, n1_rows, 2 * k1p), lambda j, bi: (0, 0, 0)),
            _single((2 * HY_N2, 2 * HY_N2), lambda j, bi: (0, 0)),
            _single((2 * HY_N2, 2 * HY_N2), lambda j, bi: (0, 0)),
            pl.BlockSpec((HY_ORDER, HY_CT), lambda j, bi: (0, j)),
        ],
        out_specs=pl.BlockSpec((None, L, HY_CT), lambda j, bi: (bi, 0, j)),
        out_shape=jax.ShapeDtypeStruct((b, L, c), BF16),
        scratch_shapes=[
            pltpu.VMEM((k1p * 2 * HY_N2, HY_CT), F32),
            pltpu.VMEM((HY_N2 * 2 * k1p, HY_CT), F32),
            pltpu.VMEM((L, HY_CT), F32),
            pltpu.VMEM((L, HY_CT), F32),
        ],
        compiler_params=pltpu.CompilerParams(dimension_semantics=("arbitrary", "arbitrary"), vmem_limit_bytes=VMEM_LIMIT_HYENA),
        name="hyena_long_conv",
    )(hy, hy, hy, h_spec, m1, m4, f3, f3i, bias.astype(F32))


HY_CTX_CT = 256


def _hy_ctx_tables(L):
    n = 2 * L
    kb = L + 1
    kp = -(-kb // LANES) * LANES
    ang = 2.0 * np.pi * ((np.arange(kb, dtype=np.float64)[:, None] * np.arange(L, dtype=np.float64)[None, :]) % n) / n
    fwd = np.zeros((2 * kp, L), np.float32)
    fwd[:kb] = np.cos(ang)
    fwd[kp:kp + kb] = -np.sin(ang)
    return jnp.asarray(fwd, BF16), jnp.asarray(fwd.T, BF16), kb, kp


def _hy_ctx_filter_kernel(fwd_ref, bwd_ref, f_ref, o_ref, *, kb, kp, n):
    fwd = fwd_ref[...]
    bwd = jnp.where(lax.broadcasted_iota(jnp.int32, fwd.shape, 0) == 0, 0.0, bwd_ref[...])
    norm = lax.rsqrt(jnp.sum(fwd * fwd, axis=0, keepdims=True) + jnp.sum(bwd * bwd, axis=0, keepdims=True) + EPS)
    hf = jnp.dot(f_ref[...], fwd.astype(BF16), preferred_element_type=F32)
    hb = jnp.dot(f_ref[...], bwd.astype(BF16), preferred_element_type=F32)
    row = lax.broadcasted_iota(jnp.int32, hf.shape, 0)
    imag = row >= kp
    k = jnp.where(imag, row - kp, row)
    wk = jnp.where(jnp.logical_or(k == 0, k == kb - 1), 1.0 / n, 2.0 / n)
    o_ref[...] = (hf + jnp.where(imag, -hb, hb)) * (wk * norm)


def _hy_ctx_kernel(v_ref, x1_ref, x2_ref, h_ref, f_ref, g_ref, bias_ref, o_ref, *, kp):
    u = v_ref[...].astype(F32)
    for order, gate_ref in enumerate((x1_ref, x2_ref)):
        x = jnp.dot(f_ref[...], u.astype(BF16), preferred_element_type=F32)
        h = h_ref[order]
        xr, xi, hr, hi = x[:kp], x[kp:], h[:kp], h[kp:]
        z = jnp.concatenate([xr * hr - xi * hi, xr * hi + xi * hr], axis=0).astype(BF16)
        y = jnp.dot(g_ref[...], z, preferred_element_type=F32)
        u = gate_ref[...].astype(F32) * (y + u * bias_ref[order:order + 1, :])
    o_ref[...] = u.astype(o_ref.dtype)


def _hyena_ctx(hy, taps, bias):
    b, L, _ = hy.shape
    c = hy.shape[2] // (HY_ORDER + 1)
    ct = HY_CTX_CT
    nct = c // ct
    f_mat, g_mat, kb, kp = _hy_ctx_tables(L)
    params = pltpu.CompilerParams(dimension_semantics=("arbitrary", "arbitrary"), vmem_limit_bytes=VMEM_LIMIT)
    h_spec = pl.pallas_call(
        functools.partial(_hy_ctx_filter_kernel, kb=kb, kp=kp, n=2 * L),
        grid=(HY_ORDER, nct),
        in_specs=[
            pl.BlockSpec((L, ct), lambda o, j: (0, o * nct + j)),
            pl.BlockSpec((L, ct), lambda o, j: (0, (HY_ORDER + o) * nct + j)),
            pl.BlockSpec((2 * kp, L), lambda o, j: (0, 0)),
        ],
        out_specs=pl.BlockSpec((None, 2 * kp, ct), lambda o, j: (o, 0, j)),
        out_shape=jax.ShapeDtypeStruct((HY_ORDER, 2 * kp, c), F32),
        compiler_params=params,
        name="hyena_ctx_filter",
    )(taps, taps, f_mat)

    def col(part):
        return pl.BlockSpec((None, L, ct), lambda j, bi: (bi, 0, part * nct + j))

    return pl.pallas_call(
        functools.partial(_hy_ctx_kernel, kp=kp),
        grid=(nct, b),
        in_specs=[
            col(0), col(1), col(2),
            pl.BlockSpec((HY_ORDER, 2 * kp, ct), lambda j, bi: (0, 0, j)),
            pl.BlockSpec((2 * kp, L), lambda j, bi: (0, 0)),
            pl.BlockSpec((L, 2 * kp), lambda j, bi: (0, 0)),
            pl.BlockSpec((HY_ORDER, ct), lambda j, bi: (0, j)),
        ],
        out_specs=pl.BlockSpec((None, L, ct), lambda j, bi: (bi, 0, j)),
        out_shape=jax.ShapeDtypeStruct((b, L, c), BF16),
        compiler_params=params,
        name="hyena_ctx_conv",
    )(hy, hy, hy, h_spec, f_mat, g_mat, bias.astype(F32))


def _split_cols(t, sizes):
    return jnp.split(t, np.cumsum(sizes)[:-1].tolist(), axis=-1)


def _to_col_major(t, rows):
    b, rest = t.shape[0], t.shape[2:]
    return jnp.swapaxes(t.reshape((b, rows, GRID_W) + rest), 1, 2).reshape((b, rows * GRID_W) + rest)


def _from_col_major(t, rows):
    b, rest = t.shape[0], t.shape[2:]
    return jnp.swapaxes(t.reshape((b, GRID_W, rows) + rest), 1, 2).reshape((b, rows * GRID_W) + rest)


def _hyena_filter_hidden(L, w1, b1, w2, b2, freq):
    hp = lax.Precision.HIGHEST
    t = jnp.linspace(0.0, 1.0, L, dtype=F32)[:, None]
    w = 2.0 * math.pi * jnp.arange(L, dtype=F32)[:, None] / L
    bands = jnp.linspace(1e-4, HY_BANDS - 1, HY_BANDS, dtype=F32)
    feats = jnp.concatenate([t, jnp.cos(bands * w), -jnp.sin(bands * w)], axis=-1)
    h = jnp.sin(freq[0] * (jnp.dot(feats, w1, precision=hp) + b1))
    return jnp.sin(freq[1] * (jnp.dot(h, w2, precision=hp) + b2))


def _hyena_decay_rates():
    max_decay = math.log(HY_DECAY_TARGET) / HY_FAST_DECAY
    min_decay = math.log(HY_DECAY_TARGET) / HY_SLOW_DECAY
    return jnp.abs(jnp.linspace(min_decay, max_decay, HY_WIDTH, dtype=F32))


def _hyena_filter_taps(L, w1, b1, w2, b2, w3, b3, freq):
    h = jnp.dot(_hyena_filter_hidden(L, w1, b1, w2, b2, freq), w3, precision=lax.Precision.HIGHEST) + b3
    t = jnp.linspace(0.0, 1.0, L, dtype=F32)[:, None]
    return h * jnp.tile(jnp.exp(-t * _hyena_decay_rates()), (1, 2 * HY_ORDER))


def _mixer_branches(h_rm, w_in, lb, n_ctx, ssm_conv_w, ssm_conv_b, ssm_dt_bias, ssm_a_log, ssm_d, ssm_norm, hy_conv_w, hy_conv_b, hy_w1, hy_b1, hy_w2, hy_b2, hy_w3, hy_b3, hy_freq, hy_bias, hg_norm):
    b, t, d = h_rm.shape
    n_lat = t - n_ctx
    rows = n_lat // GRID_W
    w_z, w_xbc, w_dt, w_hy, w_q, w_f, w_i, w_g, w_gate = _split_cols(w_in, IN_SIZES)
    h2 = h_rm.reshape(b * t, d)
    rm_parts = (w_z, w_xbc, w_hy, w_g, w_gate)
    col_z, col_xbc, col_hy, col_g, col_gate = np.cumsum([0] + [w.shape[1] for w in rm_parts[:-1]]).tolist()
    p_rm = _mm(h2, jnp.concatenate(rm_parts, axis=1).astype(BF16), BF16).reshape(b, t, -1)

    zero_pad = jnp.zeros((d, LANES - SSM_HEADS), F32)
    w_dt2 = jnp.concatenate([w_dt[:, :SSM_HEADS], zero_pad, w_dt[:, SSM_HEADS:], zero_pad], axis=1)
    dt2 = _mm(h2, w_dt2.astype(BF16), F32).reshape(b, t, 2 * LANES)
    xbc_act = _dwconv_stream(p_rm, ssm_conv_w, ssm_conv_b, n_ctx, True, c0=col_xbc)
    ym = _ssd_scan(xbc_act, dt2, p_rm, col_z, ssm_dt_bias, ssm_a_log, ssm_d, ssm_norm, n_ctx)

    hy_ctx, hy_lat = _dwconv_stream(p_rm, hy_conv_w, hy_conv_b, n_ctx, False, c0=col_hy, split=True)
    taps_ctx = _hyena_filter_taps(n_ctx, hy_w1, hy_b1, hy_w2, hy_b2, hy_w3, hy_b3, hy_freq)
    yh_ctx = _hyena_ctx(hy_ctx, taps_ctx, hy_bias)
    h_spec = _hyena_filter_spectrum_pallas(_hyena_filter_hidden(n_lat, hy_w1, hy_b1, hy_w2, hy_b2, hy_freq), hy_w3, hy_b3, _hyena_decay_rates())
    yh_lat = _hyena_long(hy_lat, h_spec, hy_bias)

    h_cm = jnp.concatenate([h_rm[:, :n_ctx], _to_col_major(h_rm[:, n_ctx:], rows)], axis=1).reshape(b * t, d)

    p_cm = _mm(h_cm, jnp.concatenate([w_q, w_f, w_i], axis=1).astype(BF16), BF16).reshape(b, t, -1)
    og = _hgrn_scan(p_cm, lb, hg_norm, n_ctx)
    return ym, (yh_ctx, yh_lat), (og, _from_col_major(og[:, n_ctx:], rows)), p_rm, col_g, col_gate


ROW_TILE = 256
MOD_ROWS = SUBLANES
M_SHIFT_MIX, M_SCALE_MIX, M_GATE_MIX, M_SHIFT_FFN, M_SCALE_FFN, M_GATE_FFN = range(6)
ROW_PARAMS = pltpu.CompilerParams(dimension_semantics=("arbitrary", "arbitrary"), vmem_limit_bytes=VMEM_LIMIT)


def _rms(x):
    return x * lax.rsqrt(jnp.mean(x * x, axis=-1, keepdims=True) + EPS)


def _mrow(m_ref, r):
    return m_ref[r:r + 1, :]


def _row_spec(width):
    return pl.BlockSpec((None, ROW_TILE, width), lambda bi, i: (bi, i, 0))


def _vec_spec(width):
    return pl.BlockSpec((1, width), lambda bi, i: (0, 0))


def _mat_spec(k, n):
    return pl.BlockSpec((k, n), lambda bi, i: (0, 0))


def _mod_spec(n_ctx):
    return pl.BlockSpec((None, None, MOD_ROWS, D_MODEL), lambda bi, i: (bi, jnp.where(i < n_ctx // ROW_TILE, 0, 1), 0, 0))


def _norm_mod_kernel(x_ref, w_ref, m_ref, o_ref):
    y = _rms(x_ref[...]) * w_ref[...]
    o_ref[...] = (y * (1.0 + _mrow(m_ref, M_SCALE_MIX)) + _mrow(m_ref, M_SHIFT_MIX)).astype(o_ref.dtype)


def _norm_mod(xs, w, mods, n_ctx):
    b, t, d = xs.shape
    return pl.pallas_call(
        _norm_mod_kernel,
        grid=(b, t // ROW_TILE),
        in_specs=[_row_spec(d), _vec_spec(d), _mod_spec(n_ctx)],
        out_specs=_row_spec(d),
        out_shape=jax.ShapeDtypeStruct((b, t, d), BF16),
        compiler_params=ROW_PARAMS,
        name="norm_mod",
    )(xs, w.reshape(1, d), mods)


def _merge_kernel(ym_ref, yhc_ref, yhl_ref, ogc_ref, ogl_ref, g_ref, gm_ref, gh_ref, gg_ref, x_ref, m_ref, w1_ref, w2_ref, w3_ref, wo_ref, npost_ref, npre_ref, rw_ref, rb_ref, xo_ref, h_ref, lg_ref, *, n_ctx_tiles):
    is_ctx = pl.program_id(1) < n_ctx_tiles
    yh = jnp.where(is_ctx, yhc_ref[...], yhl_ref[...])
    og = jnp.where(is_ctx, ogc_ref[...], ogl_ref[...])
    def sig(ref):
        return jax.nn.sigmoid(ref[...].astype(F32))

    go = g_ref[...].astype(F32)
    yg = (og.astype(F32) * (go * jax.nn.sigmoid(go))).astype(BF16)
    merged = sig(gm_ref) * jnp.dot(ym_ref[...], w1_ref[...], preferred_element_type=F32)
    merged = merged + sig(gh_ref) * jnp.dot(yh, w2_ref[...], preferred_element_type=F32)
    merged = merged + sig(gg_ref) * jnp.dot(yg, w3_ref[...], preferred_element_type=F32)
    mix = jnp.dot(merged.astype(BF16), wo_ref[...], preferred_element_type=F32)
    x = x_ref[...] + _mrow(m_ref, M_GATE_MIX) * (_rms(mix) * npost_ref[...])
    xo_ref[...] = x
    h = (_rms(x) * npre_ref[...] * (1.0 + _mrow(m_ref, M_SCALE_FFN)) + _mrow(m_ref, M_SHIFT_FFN)).astype(BF16)
    h_ref[...] = h
    lg_ref[...] = jnp.dot(h, rw_ref[...], preferred_element_type=F32) + rb_ref[...]


def _merge(ym, yh_parts, og_parts, p_rm, col_g, col_gate, xs, mods, w_br_ssm, w_br_hy, w_br_hg, w_out, norm_post, norm_ffn_pre, router_w, router_b, n_ctx):
    b, t, d = xs.shape
    rw = jnp.pad(router_w, ((0, 0), (0, LANES - N_EXPERTS))).astype(BF16)
    rb = jnp.pad(router_b, (0, LANES - N_EXPERTS)).reshape(1, LANES).astype(F32)

    def col_spec(col):
        return pl.BlockSpec((None, ROW_TILE, d), lambda bi, i: (bi, i, col // d))

    nct = n_ctx // ROW_TILE
    ctx_spec = pl.BlockSpec((None, ROW_TILE, d), lambda bi, i: (bi, jnp.minimum(i, nct - 1), 0))
    lat_spec = pl.BlockSpec((None, ROW_TILE, d), lambda bi, i: (bi, jnp.maximum(i - nct, 0), 0))

    return pl.pallas_call(
        functools.partial(_merge_kernel, n_ctx_tiles=nct),
        grid=(b, t // ROW_TILE),
        in_specs=[_row_spec(d), ctx_spec, lat_spec, ctx_spec, lat_spec, col_spec(col_g), col_spec(col_gate), col_spec(col_gate + d), col_spec(col_gate + 2 * d), _row_spec(d), _mod_spec(n_ctx),
                  _mat_spec(d, d), _mat_spec(d, d), _mat_spec(d, d), _mat_spec(d, d), _vec_spec(d), _vec_spec(d), _mat_spec(d, LANES), _vec_spec(LANES)],
        out_specs=[_row_spec(d), _row_spec(d), _row_spec(LANES)],
        out_shape=[jax.ShapeDtypeStruct((b, t, d), F32), jax.ShapeDtypeStruct((b, t, d), BF16), jax.ShapeDtypeStruct((b, t, LANES), F32)],
        compiler_params=ROW_PARAMS,
        name="branch_merge",
    )(ym, yh_parts[0], yh_parts[1], og_parts[0], og_parts[1], p_rm, p_rm, p_rm, p_rm, xs, mods, w_br_ssm.astype(BF16), w_br_hy.astype(BF16), w_br_hg.astype(BF16), w_out.astype(BF16),
      norm_post.reshape(1, d), norm_ffn_pre.reshape(1, d), rw, rb)


def _post_ffn_kernel(y0_ref, y1_ref, y2_ref, y3_ref, x_ref, m_ref, w_ref, o_ref):
    f = y0_ref[...].astype(F32) + y1_ref[...].astype(F32) + y2_ref[...].astype(F32) + y3_ref[...].astype(F32)
    o_ref[...] = x_ref[...] + _mrow(m_ref, M_GATE_FFN) * (_rms(f) * w_ref[...])


def _post_ffn(f4, xs, mods, norm_post, n_ctx, skip):
    b, _, d = xs.shape
    t = f4.shape[2]
    i0 = skip // ROW_TILE

    def k_spec(k):
        return pl.BlockSpec((None, None, ROW_TILE, d), lambda bi, i: (k, bi, i, 0))

    return pl.pallas_call(
        _post_ffn_kernel,
        grid=(b, t // ROW_TILE),
        in_specs=[k_spec(k) for k in range(TOP_K)] + [
            pl.BlockSpec((None, ROW_TILE, d), lambda bi, i: (bi, i0 + i, 0)),
            pl.BlockSpec((None, None, MOD_ROWS, d), lambda bi, i: (bi, jnp.where(i0 + i < n_ctx // ROW_TILE, 0, 1), 0, 0)),
            _vec_spec(d)],
        out_specs=_row_spec(d),
        out_shape=jax.ShapeDtypeStruct((b, t, d), F32),
        compiler_params=ROW_PARAMS,
        name="post_ffn",
    )(f4, f4, f4, f4, xs, mods, norm_post.reshape(1, d))


def _moe_ffn(h2, logits, t_per_b, skip, li, w1, b1, w2, b2):
    t = logits.shape[0]
    d = h2.shape[1]
    n = t * TOP_K
    n_tiles = n // MOE_BLOCK
    top_v, top_e = lax.top_k(logits, TOP_K)
    gate_w = jax.nn.softmax(top_v, axis=-1)
    flat_e = top_e.reshape(n).astype(jnp.int32)
    iota = jnp.arange(n, dtype=jnp.int32)
    _, order, sw = lax.sort((flat_e, iota, gate_w.reshape(n)), num_keys=1, is_stable=True)
    _, inv = lax.sort((order, iota), num_keys=1)
    tok = order // TOP_K
    xs = h2[tok + (tok // t_per_b + 1) * skip]
    counts = jnp.sum((flat_e[:, None] == jnp.arange(N_EXPERTS, dtype=jnp.int32)[None, :]).astype(jnp.int32), axis=0)
    end = jnp.cumsum(counts)
    start = end - counts
    first_tile = start // MOE_BLOCK
    n_items = jnp.where(counts > 0, (end - 1) // MOE_BLOCK - first_tile + 1, 0)
    items_end = jnp.cumsum(n_items)
    w = jnp.arange(n_tiles + N_EXPERTS, dtype=jnp.int32)
    valid = w < items_end[-1]
    e_w = jnp.minimum(jnp.sum((w[:, None] >= items_end[None, :]).astype(jnp.int32), axis=1), N_EXPERTS - 1)
    tile_w = first_tile[e_w] + (w - (items_end[e_w] - n_items[e_w]))
    lo = jnp.where(valid, jnp.maximum(start[e_w], tile_w * MOE_BLOCK), 0)
    hi = jnp.where(valid, jnp.minimum(end[e_w], (tile_w + 1) * MOE_BLOCK), 0)
    tile_w = jnp.where(valid, tile_w, n_tiles - 1)
    first = jnp.concatenate([jnp.ones((1,), jnp.int32), (tile_w[1:] != tile_w[:-1]).astype(jnp.int32)])
    newexp = jnp.concatenate([jnp.ones((1,), jnp.int32), (e_w[1:] != e_w[:-1]).astype(jnp.int32)])
    ys = _moe_experts(xs, sw, tile_w.astype(jnp.int32), e_w, lo.astype(jnp.int32), hi.astype(jnp.int32), first, newexp, li, w1, b1, w2, b2)
    return ys[inv.reshape(t, TOP_K).T]


def kernel(x, c, ctx, c_ctx, w_mod, b_mod, norm_mix_pre, norm_mix_post, norm_ffn_pre, norm_ffn_post, w_in, ssm_conv_w, ssm_conv_b, ssm_dt_bias, ssm_a_log, ssm_d, ssm_norm, hy_conv_w, hy_conv_b, hy_w1, hy_b1, hy_w2, hy_b2, hy_w3, hy_b3, hy_freq, hy_bias, hg_lb_logits, hg_norm, w_br_ssm, w_br_hy, w_br_hg, w_out, router_w, router_b, exp_w1, exp_b1, exp_w2, exp_b2):
    hp = lax.Precision.HIGHEST
    b, n_lat, d = x.shape
    n_ctx = ctx.shape[1]
    lb = jax.nn.softmax(hg_lb_logits.astype(F32), axis=1)
    lb = jnp.cumsum(lb, axis=1) - lb[:, :1]
    silu_c = jax.nn.silu(c)
    silu_cc = jax.nn.silu(c_ctx)
    xs = jnp.concatenate([ctx, x], axis=1)
    for li in range(DEPTH):
        mx = (jnp.dot(silu_c, w_mod[li], precision=hp) + b_mod[li]).reshape(b, 1, 6, d)
        mc = jnp.broadcast_to((jnp.dot(silu_cc, w_mod[li], precision=hp) + b_mod[li]).reshape(1, 1, 6, d), (b, 1, 6, d))
        mods = jnp.pad(jnp.concatenate([mc, mx], axis=1), ((0, 0), (0, 0), (0, MOD_ROWS - 6), (0, 0)))
        h = _norm_mod(xs, norm_mix_pre[li], mods, n_ctx)
        ym, yh_parts, og_parts, p_rm, col_g, col_gate = _mixer_branches(h, w_in[li], lb[:, li], n_ctx, ssm_conv_w[li], ssm_conv_b[li], ssm_dt_bias[li], ssm_a_log[li], ssm_d[li], ssm_norm[li], hy_conv_w[li], hy_conv_b[li], hy_w1[li], hy_b1[li], hy_w2[li], hy_b2[li], hy_w3[li], hy_b3[li], hy_freq[li], hy_bias[li], hg_norm[li])
        xs, h_ffn, logits = _merge(ym, yh_parts, og_parts, p_rm, col_g, col_gate, xs, mods, w_br_ssm[li], w_br_hy[li], w_br_hg[li], w_out[li], norm_mix_post[li], norm_ffn_pre[li], router_w[li], router_b[li], n_ctx)
        skip = n_ctx if li == DEPTH - 1 else 0
        t = xs.shape[1] - skip
        f4 = _moe_ffn(h_ffn.reshape(-1, d), logits[:, skip:, :N_EXPERTS].reshape(b * t, N_EXPERTS), t, skip, li, exp_w1, exp_b1[li], exp_w2, exp_b2[li])
        xs = _post_ffn(f4.reshape(TOP_K, b, t, d), xs, mods, norm_ffn_post[li], n_ctx, skip)
    return xs
```

```python
import functools
import math

import jax
import jax.numpy as jnp
import numpy as np
from jax import lax
from jax.experimental import pallas as pl
from jax.experimental.pallas import tpu as pltpu

D_MODEL = 1024
DEPTH = 2
GRID_W = 64

SSM_HEADS = 16
SSM_HEAD_DIM = 64
SSM_INNER = SSM_HEADS * SSM_HEAD_DIM
SSM_STATE = 128
SSM_GROUPS = 4
SSD_CHUNK = 128
SSM_XBC = SSM_INNER + 2 * SSM_GROUPS * SSM_STATE

HY_WIDTH = D_MODEL
HY_ORDER = 2
HY_BANDS = 16
HY_FAST_DECAY = 0.3
HY_SLOW_DECAY = 1.5
HY_DECAY_TARGET = 1e-2

HG_HEADS = 8
HG_KDIM = 128
HG_VDIM = D_MODEL // HG_HEADS
HG_QK = HG_HEADS * HG_KDIM
HG_V = HG_HEADS * HG_VDIM
HG_CHUNK = 64
F_FLOOR = 1e-20

N_EXPERTS = 32
TOP_K = 4
D_FF = D_MODEL
SWIGLU_LIMIT = 7.0
SWIGLU_ALPHA = 1.702
MOE_BLOCK = 512

N_BRANCHES = 3
IN_SIZES = (SSM_INNER, SSM_XBC, 2 * SSM_HEADS, (HY_ORDER + 1) * HY_WIDTH, HG_QK, 2 * HG_QK, HG_V, HG_V, N_BRANCHES * D_MODEL)
EPS = 1e-6
F32 = jnp.float32
BF16 = jnp.bfloat16

LANES = 128
VMEM_LIMIT = 56 * 1024 * 1024


def _mm_kernel(a_ref, b_ref, o_ref):
    o_ref[...] = jnp.dot(a_ref[...], b_ref[...], preferred_element_type=F32).astype(o_ref.dtype)


def _mm(a, b, out_dtype=F32, tm=1024, tn=2048):
    m, k = a.shape
    n = b.shape[1]
    tm = math.gcd(m, tm)
    tn = math.gcd(n, tn)
    assert tm % SUBLANES == 0 and tn % LANES == 0, (m, n, tm, tn)
    return pl.pallas_call(
        _mm_kernel,
        grid=(n // tn, m // tm),
        in_specs=[pl.BlockSpec((tm, k), lambda j, i: (i, 0)), pl.BlockSpec((k, tn), lambda j, i: (0, j))],
        out_specs=pl.BlockSpec((tm, tn), lambda j, i: (i, j)),
        out_shape=jax.ShapeDtypeStruct((m, n), out_dtype),
        compiler_params=pltpu.CompilerParams(dimension_semantics=("arbitrary", "arbitrary"), vmem_limit_bytes=VMEM_LIMIT),
        name="dense_mm",
    )(a, b)


def _moe_kernel(tile_ref, exp_ref, lo_ref, hi_ref, first_ref, newexp_ref, x_ref, sw_ref, w1_ref, b1_ref, w2_ref, b2_ref, o_ref, w1b_ref, w2b_ref):
    del exp_ref
    w = pl.program_id(0)
    lo, hi = lo_ref[w], hi_ref[w]

    @pl.when(newexp_ref[w] == 1)
    def _():
        def cast_rows(i, carry):
            rows = pl.ds(pl.multiple_of(i * LANES, LANES), LANES)
            w1b_ref[rows, :] = w1_ref[rows, :].astype(BF16)
            w2b_ref[rows, :] = w2_ref[rows, :].astype(BF16)
            return carry

        lax.fori_loop(0, D_MODEL // LANES, cast_rows, 0)

    @pl.when(hi > lo)
    def _():
        hh = jnp.dot(x_ref[...], w1b_ref[...], preferred_element_type=F32) + b1_ref[...]
        g = jnp.minimum(hh[:, :D_FF], SWIGLU_LIMIT)
        u = jnp.clip(hh[:, D_FF:], -SWIGLU_LIMIT, SWIGLU_LIMIT)
        act = (u + 1.0) * g * jax.nn.sigmoid(SWIGLU_ALPHA * g)
        y = jnp.dot(act.astype(BF16), w2b_ref[...], preferred_element_type=F32) + b2_ref[...]
        y = (y * sw_ref[...]).astype(o_ref.dtype)
        rows = tile_ref[w] * MOE_BLOCK + lax.broadcasted_iota(jnp.int32, (MOE_BLOCK, 1), 0)
        mine = jnp.logical_and(rows >= lo, rows < hi)

        @pl.when(first_ref[w] == 1)
        def _():
            o_ref[...] = jnp.where(mine, y, jnp.zeros_like(y))

        @pl.when(first_ref[w] != 1)
        def _():
            o_ref[...] = jnp.where(mine, y, o_ref[...])


def _moe_experts(xs, sw, tile_w, exp_w, lo, hi, first, newexp, li, w1, b1, w2, b2):
    n, d = xs.shape
    grid_spec = pltpu.PrefetchScalarGridSpec(
        num_scalar_prefetch=6,
        grid=(tile_w.shape[0],),
        in_specs=[
            pl.BlockSpec((MOE_BLOCK, d), lambda w, tl, ex, lo_, hi_, fi, ne: (tl[w], 0)),
            pl.BlockSpec((MOE_BLOCK, 1), lambda w, tl, ex, lo_, hi_, fi, ne: (tl[w], 0)),
            pl.BlockSpec((None, None, d, 2 * D_FF), lambda w, tl, ex, lo_, hi_, fi, ne: (li, ex[w], 0, 0)),
            pl.BlockSpec((None, 1, 2 * D_FF), lambda w, tl, ex, lo_, hi_, fi, ne: (ex[w], 0, 0)),
            pl.BlockSpec((None, None, D_FF, d), lambda w, tl, ex, lo_, hi_, fi, ne: (li, ex[w], 0, 0)),
            pl.BlockSpec((None, 1, d), lambda w, tl, ex, lo_, hi_, fi, ne: (ex[w], 0, 0)),
        ],
        out_specs=pl.BlockSpec((MOE_BLOCK, d), lambda w, tl, ex, lo_, hi_, fi, ne: (tl[w], 0)),
        scratch_shapes=[pltpu.VMEM((d, 2 * D_FF), BF16), pltpu.VMEM((D_FF, d), BF16)],
    )
    return pl.pallas_call(
        _moe_kernel,
        grid_spec=grid_spec,
        out_shape=jax.ShapeDtypeStruct((n, d), BF16),
        compiler_params=pltpu.CompilerParams(dimension_semantics=("arbitrary",), vmem_limit_bytes=VMEM_LIMIT),
        name="moe_experts",
    )(tile_w, exp_w, lo, hi, first, newexp, xs, sw.reshape(n, 1), w1, b1.reshape(N_EXPERTS, 1, 2 * D_FF), w2, b2.reshape(N_EXPERTS, 1, d))


SUBLANES = 8
NEG_BIG = -1e30
HG_STEP_CHUNKS = 4
HIER_LEVELS = (64, 32, 16, 8, 4)
LOG2E = math.log2(math.e)


def _split3(x):
    h1 = x.astype(BF16)
    r1 = x - h1.astype(F32)
    h2 = r1.astype(BF16)
    h3 = (r1 - h2.astype(F32)).astype(BF16)
    return h1, h2, h3


def _dot_nt(a, b):
    return lax.dot_general(a, b, (((1,), (1,)), ((), ())), preferred_element_type=F32)


def _gla_kernel(q_ref, a_ref, v_ref, lb_ref, *rest, reverse):
    if reverse:
        of_ref, w_ref, o_ref, st_ref, at_ref = rest
    else:
        o_ref, st_ref, at_ref = rest
    Q = HG_CHUNK

    @pl.when(pl.program_id(1) == 0)
    def _():
        st_ref[...] = jnp.zeros_like(st_ref)

    row = lax.broadcasted_iota(jnp.int32, (Q, Q), 0)
    col = lax.broadcasted_iota(jnp.int32, (Q, Q), 1)
    tri = jnp.where((col >= row) if reverse else (col <= row), 1.0, 0.0).astype(BF16)
    same_block = {s: (row // s) == (col // s) for s in HIER_LEVELS[1:] + (2, 1)}
    rowk = lax.broadcasted_iota(jnp.int32, (Q, HG_KDIM), 0)
    q_rows = {s: ((rowk % s) < s // 2) if reverse else ((rowk % s) >= s // 2) for s in HIER_LEVELS + (2,)}
    of_w = (of_ref, w_ref) if reverse else (None, None)
    for sub in (range(HG_STEP_CHUNKS - 1, -1, -1) if reverse else range(HG_STEP_CHUNKS)):
        _gla_chunk(sub, q_ref, a_ref, v_ref, lb_ref[...], of_w, o_ref, st_ref, at_ref, tri, same_block, q_rows, reverse)


def _gla_chunk(sub, q_ref, a_ref, v_ref, lb, of_w, o_ref, st_ref, at_ref, tri, same_block, q_rows, reverse):
    Q = HG_CHUNK
    rs = slice(sub * Q, (sub + 1) * Q)
    of_ref, w_ref = of_w
    shp3 = (Q // SUBLANES, SUBLANES, HG_KDIM)
    sub3 = lax.broadcasted_iota(jnp.int32, shp3, 1)
    a = a_ref[rs, :].astype(F32)
    f_all = jnp.maximum(lb + (1.0 - lb) * jax.nn.sigmoid(a), F_FLOOR)
    kk = (1.0 - lb) * jax.nn.sigmoid(-a)
    g_all = sum(jnp.dot(tri, p, preferred_element_type=F32) for p in _split3(jnp.log(f_all) * LOG2E))
    q_all = q_ref[rs, :].astype(F32)
    q_all = q_all * jax.nn.sigmoid(q_all)
    v_all = v_ref[rs, :]
    tot = 0 if reverse else Q - 1

    def level_ref(g, s):
        half = s // 2
        m_off = half if reverse else half - 1
        if s >= 2 * SUBLANES:
            return jnp.concatenate([jnp.broadcast_to(g[b0 + m_off:b0 + m_off + 1, :], (s, HG_KDIM)) for b0 in range(0, Q, s)], axis=0)
        g3 = g.reshape(shp3)
        ref = jnp.broadcast_to(g3[:, m_off:m_off + 1, :], shp3)
        for b0 in range(s, SUBLANES, s):
            ref = jnp.where(sub3 >= b0, jnp.broadcast_to(g3[:, b0 + m_off:b0 + m_off + 1, :], shp3), ref)
        return ref.reshape(Q, HG_KDIM)

    for h in range(HG_HEADS):
        sl = slice(h * HG_KDIM, (h + 1) * HG_KDIM)
        g, qh, kh, fh = g_all[:, sl], q_all[:, sl], kk[:, sl], f_all[:, sl]
        attn = None
        for s in HIER_LEVELS:
            gref = level_ref(g, s)
            eq = jnp.exp2(jnp.where(q_rows[s], g - gref, NEG_BIG))
            ek = jnp.exp2(jnp.where(q_rows[s], NEG_BIG, gref - g))
            lvl = _dot_nt((qh * eq).astype(BF16), (kh * ek).astype(BF16))
            attn = lvl if attn is None else jnp.where(same_block[s], lvl, attn)
        lvl = _dot_nt(jnp.where(q_rows[2], qh * fh, 0.0).astype(BF16), jnp.where(q_rows[2], 0.0, kh).astype(BF16))
        attn = jnp.where(same_block[2], lvl, attn)
        attn = jnp.where(same_block[1], _dot_nt(qh.astype(BF16), kh.astype(BF16)), attn)
        at_ref[sub, h] = attn.astype(BF16)

    for h in range(HG_HEADS):
        sl = slice(h * HG_KDIM, (h + 1) * HG_KDIM)
        g, qh, kh, vb = g_all[:, sl], q_all[:, sl], kk[:, sl], v_all[:, sl]
        g_tot = g[tot:tot + 1, :]
        st = st_ref[h]
        o = _dot_nt((qh * jnp.exp2(g)).astype(BF16), st.astype(BF16))
        o = o + jnp.dot(at_ref[sub, h], vb, preferred_element_type=F32)
        k_st = (kh * jnp.exp2(g_tot - g)).astype(BF16)
        st_ref[h] = st * jnp.exp2(g_tot) + jnp.dot(vb.astype(F32).T.astype(BF16), k_st, preferred_element_type=F32)
        if reverse:
            o = o + of_ref[rs, sl]
            o = o * lax.rsqrt(jnp.mean(o * o, axis=-1, keepdims=True) + EPS) * w_ref[:, sl]
        o_ref[rs, sl] = o.astype(o_ref.dtype)


def _hgrn_scan(p_cm, lb, norm_w, n_ctx):
    b, t, _ = p_cm.shape
    col_q, col_f, col_i = 0, 1, 3
    step_rows = HG_STEP_CHUNKS * HG_CHUNK
    assert n_ctx % step_rows == 0 and t % step_rows == 0, (n_ctx, t, step_rows)
    nc, ncc = t // step_rows, n_ctx // step_rows
    blk = (None, step_rows, HG_QK)
    scratch = [pltpu.VMEM((HG_HEADS, HG_VDIM, HG_KDIM), F32), pltpu.VMEM((HG_STEP_CHUNKS, HG_HEADS, HG_CHUNK, HG_CHUNK), BF16)]
    params = pltpu.CompilerParams(dimension_semantics=("arbitrary", "arbitrary"), vmem_limit_bytes=VMEM_LIMIT)
    row_spec = pl.BlockSpec((1, HG_QK), lambda bi, s: (0, 0))

    def fwd_chunk(s):
        return s

    def bwd_chunk(s):
        return jnp.where(s < ncc, ncc - 1 - s, nc + ncc - 1 - s)

    o_f = pl.pallas_call(
        functools.partial(_gla_kernel, reverse=False),
        grid=(b, nc),
        in_specs=[
            pl.BlockSpec(blk, lambda bi, s: (bi, fwd_chunk(s), col_q)),
            pl.BlockSpec(blk, lambda bi, s: (bi, fwd_chunk(s), col_f)),
            pl.BlockSpec(blk, lambda bi, s: (bi, fwd_chunk(s), col_i)),
            row_spec,
        ],
        out_specs=pl.BlockSpec(blk, lambda bi, s: (bi, fwd_chunk(s), 0)),
        out_shape=jax.ShapeDtypeStruct((b, t, HG_V), F32),
        scratch_shapes=scratch,
        compiler_params=params,
        name="gla_fwd",
    )(p_cm, p_cm, p_cm, lb[0:1])
    return pl.pallas_call(
        functools.partial(_gla_kernel, reverse=True),
        grid=(b, nc),
        in_specs=[
            pl.BlockSpec(blk, lambda bi, s: (bi, bwd_chunk(s), col_q)),
            pl.BlockSpec(blk, lambda bi, s: (bi, bwd_chunk(s), col_f + 1)),
            pl.BlockSpec(blk, lambda bi, s: (bi, bwd_chunk(s), col_i)),
            row_spec,
            pl.BlockSpec(blk, lambda bi, s: (bi, bwd_chunk(s), 0)),
            row_spec,
        ],
        out_specs=pl.BlockSpec(blk, lambda bi, s: (bi, bwd_chunk(s), 0)),
        out_shape=jax.ShapeDtypeStruct((b, t, HG_V), BF16),
        scratch_shapes=scratch,
        compiler_params=params,
        name="gla_bwd",
    )(p_cm, p_cm, p_cm, lb[1:2], o_f, norm_w.reshape(1, HG_V))


CONV_TILE = 256
HALO = 16


def _dwconv_kernel(prev_ref, cur_ref, next_ref, w_ref, b_ref, *o_refs, taps, n_ctx_tiles, n_tiles, silu):
    i = pl.program_id(2)
    first = jnp.logical_or(i == 0, i == n_ctx_tiles)
    last = jnp.logical_or(i == n_ctx_tiles - 1, i == n_tiles - 1)
    pad = taps // 2
    xp = jnp.where(first, 0.0, prev_ref[...].astype(F32))
    xn = jnp.where(last, 0.0, next_ref[...].astype(F32))
    xcat = jnp.concatenate([xp, cur_ref[...].astype(F32), xn], axis=0)
    acc = jnp.broadcast_to(b_ref[...], cur_ref.shape).astype(F32)
    for k in range(taps):
        off = HALO - pad + k
        acc = acc + w_ref[k:k + 1, :] * xcat[off:off + CONV_TILE, :]
    if silu:
        acc = acc * jax.nn.sigmoid(acc)
    if len(o_refs) == 1:
        o_refs[0][...] = acc.astype(o_refs[0].dtype)
    else:
        ctx_ref, lat_ref = o_refs

        @pl.when(i < n_ctx_tiles)
        def _():
            ctx_ref[...] = acc.astype(ctx_ref.dtype)

        @pl.when(i >= n_ctx_tiles)
        def _():
            lat_ref[...] = acc.astype(lat_ref.dtype)


def _dwconv_stream(x, w, bias, n_ctx, silu, c0=0, ct=1024, split=False):
    b, t, _ = x.shape
    taps, c = w.shape
    n_tiles = t // CONV_TILE
    nct = n_ctx // CONV_TILE
    hb = CONV_TILE // HALO
    n_halo = t // HALO
    j0 = c0 // ct
    kern = functools.partial(_dwconv_kernel, taps=taps, n_ctx_tiles=nct, n_tiles=n_tiles, silu=silu)
    if split:
        out_specs = [pl.BlockSpec((None, CONV_TILE, ct), lambda bi, j, i: (bi, jnp.minimum(i, nct - 1), j)),
                     pl.BlockSpec((None, CONV_TILE, ct), lambda bi, j, i: (bi, jnp.maximum(i - nct, 0), j))]
        out_shape = [jax.ShapeDtypeStruct((b, n_ctx, c), BF16), jax.ShapeDtypeStruct((b, t - n_ctx, c), BF16)]
    else:
        out_specs = pl.BlockSpec((None, CONV_TILE, ct), lambda bi, j, i: (bi, i, j))
        out_shape = jax.ShapeDtypeStruct((b, t, c), BF16)
    return pl.pallas_call(
        kern,
        grid=(b, c // ct, n_tiles),
        in_specs=[
            pl.BlockSpec((None, HALO, ct), lambda bi, j, i: (bi, jnp.maximum(i * hb - 1, 0), j0 + j)),
            pl.BlockSpec((None, CONV_TILE, ct), lambda bi, j, i: (bi, i, j0 + j)),
            pl.BlockSpec((None, HALO, ct), lambda bi, j, i: (bi, jnp.minimum((i + 1) * hb, n_halo - 1), j0 + j)),
            pl.BlockSpec((taps, ct), lambda bi, j, i: (0, j)),
            pl.BlockSpec((1, ct), lambda bi, j, i: (0, j)),
        ],
        out_specs=out_specs,
        out_shape=out_shape,
        compiler_params=pltpu.CompilerParams(dimension_semantics=("arbitrary",) * 3, vmem_limit_bytes=VMEM_LIMIT),
        name="dwconv",
    )(x, x, x, w.astype(F32), bias.reshape(1, c).astype(F32))


SSD_STEP_CHUNKS = 2
SSM_GHEADS = SSM_HEADS // SSM_GROUPS
SSM_GP = SSM_GHEADS * SSM_HEAD_DIM


def _ssd_kernel(xbc_ref, dt_ref, dtb_ref, a_ref, *rest, reverse):
    if reverse:
        yf_ref, z_ref, dsk_ref, nw_ref, o_ref, st_ref, m_ref, xd_ref, xst_ref, y_ref = rest
    else:
        o_ref, st_ref, m_ref, xd_ref, xst_ref, y_ref = rest
    Q = SSD_CHUNK

    @pl.when(pl.program_id(1) == 0)
    def _():
        st_ref[...] = jnp.zeros_like(st_ref)

    row = lax.broadcasted_iota(jnp.int32, (Q, Q), 0)
    col = lax.broadcasted_iota(jnp.int32, (Q, Q), 1)
    keep = (col >= row) if reverse else (col <= row)
    tri = jnp.where(keep, 1.0, 0.0).astype(BF16)
    expand = jnp.where(lax.broadcasted_iota(jnp.int32, (LANES, SSM_INNER), 1) // SSM_HEAD_DIM == lax.broadcasted_iota(jnp.int32, (LANES, SSM_INNER), 0), 1.0, 0.0).astype(BF16)

    rev_refs = (yf_ref, z_ref, dsk_ref, nw_ref) if reverse else None
    for sub in (range(SSD_STEP_CHUNKS - 1, -1, -1) if reverse else range(SSD_STEP_CHUNKS)):
        _ssd_chunk(sub, xbc_ref, dt_ref, dtb_ref[...], a_ref[...], rev_refs, o_ref, st_ref, m_ref, xd_ref, xst_ref, y_ref, keep, tri, expand, reverse)


def _ssd_chunk(sub, xbc_ref, dt_ref, dt_bias, a_neg, rev_refs, o_ref, st_ref, m_ref, xd_ref, xst_ref, y_ref, keep, tri, expand, reverse):
    Q = SSD_CHUNK
    rs = slice(sub * Q, (sub + 1) * Q)
    dt = jax.nn.softplus(dt_ref[rs, :] + dt_bias)
    a = dt * a_neg
    cs = sum(jnp.dot(tri, p, preferred_element_type=F32) for p in _split3(a))
    tot = 0 if reverse else Q - 1
    cs_tot = cs[tot:tot + 1, :]
    cs_t = cs.T
    dt_e = jnp.dot(dt.astype(BF16), expand, preferred_element_type=F32)
    e_in = jnp.dot(jnp.exp(cs).astype(BF16), expand, preferred_element_type=F32)
    e_st = jnp.dot(jnp.exp(cs_tot - cs).astype(BF16), expand, preferred_element_type=F32)
    e_tot = jnp.dot(jnp.broadcast_to(jnp.exp(cs_tot), (SUBLANES, LANES)).astype(BF16), expand, preferred_element_type=F32)[0:1, :]

    xs = xbc_ref[rs, :SSM_INNER].astype(F32)
    xd_ref[sub] = (xs * dt_e).astype(BF16)
    xst_ref[sub] = (xs * dt_e * e_st).astype(BF16)
    for g in range(SSM_GROUPS):
        bm = xbc_ref[rs, SSM_INNER + g * SSM_STATE:SSM_INNER + (g + 1) * SSM_STATE]
        cm = xbc_ref[rs, SSM_INNER + (SSM_GROUPS + g) * SSM_STATE:SSM_INNER + (SSM_GROUPS + g + 1) * SSM_STATE]
        cb = _dot_nt(cm, bm)
        for h in range(g * SSM_GHEADS, (g + 1) * SSM_GHEADS):
            diff = jnp.broadcast_to(cs[:, h:h + 1], (Q, Q)) - jnp.broadcast_to(cs_t[h:h + 1, :], (Q, Q))
            m_ref[sub, h] = (cb * jnp.exp(jnp.where(keep, diff, NEG_BIG))).astype(BF16)

    lane_lo = lax.broadcasted_iota(jnp.int32, (Q, LANES), 1) < SSM_HEAD_DIM
    for g in range(SSM_GROUPS):
        bm = xbc_ref[rs, SSM_INNER + g * SSM_STATE:SSM_INNER + (g + 1) * SSM_STATE]
        cm = xbc_ref[rs, SSM_INNER + (SSM_GROUPS + g) * SSM_STATE:SSM_INNER + (SSM_GROUPS + g + 1) * SSM_STATE]
        gl = slice(g * SSM_GP, (g + 1) * SSM_GP)
        st = st_ref[g]
        y_off = jnp.dot(cm, st.astype(BF16), preferred_element_type=F32) * e_in[:, gl]
        for hp in range(SSM_GHEADS // 2):
            h0 = g * SSM_GHEADS + 2 * hp
            lanes = slice((h0 // 2) * LANES, (h0 // 2 + 1) * LANES)
            pair = [jnp.dot(m_ref[sub, h], xd_ref[sub, :, lanes], preferred_element_type=F32) for h in (h0, h0 + 1)]
            y_ref[sub, :, lanes] = jnp.where(lane_lo, pair[0], pair[1]) + y_off[:, hp * LANES:(hp + 1) * LANES]
        st_ref[g] = st * e_tot[:, gl] + jnp.dot(bm.astype(F32).T.astype(BF16), xst_ref[sub, :, gl], preferred_element_type=F32)
    y = y_ref[sub]
    if reverse:
        yf_ref, z_ref, dsk_ref, nw_ref = rev_refs
        y = (y + yf_ref[rs, :] + xbc_ref[rs, :SSM_INNER].astype(F32) * dsk_ref[...])
        zz = z_ref[rs, :].astype(F32)
        y = y * (zz * jax.nn.sigmoid(zz))
        y = y * lax.rsqrt(jnp.mean(y * y, axis=-1, keepdims=True) + EPS) * nw_ref[...]
    o_ref[rs, :] = y.astype(o_ref.dtype)


def _ssd_scan(xbc_act, dt2, p_rm, z_col, dt_bias, a_log, d_skip, norm_w, n_ctx):
    b, t, _ = xbc_act.shape
    step_rows = SSD_STEP_CHUNKS * SSD_CHUNK
    assert n_ctx % step_rows == 0 and t % step_rows == 0, (n_ctx, t, step_rows)
    nc, ncc = t // step_rows, n_ctx // step_rows

    def pad_heads(v):
        return jnp.pad(v.astype(F32), ((0, 0), (0, LANES - SSM_HEADS)))

    dtb = pad_heads(dt_bias.reshape(2, SSM_HEADS))
    a_neg = pad_heads(-jnp.exp(a_log.astype(F32)))
    dsk = jnp.repeat(d_skip.astype(F32), SSM_HEAD_DIM).reshape(1, SSM_INNER)
    scratch = [pltpu.VMEM((SSM_GROUPS, SSM_STATE, SSM_GP), F32), pltpu.VMEM((SSD_STEP_CHUNKS, SSM_HEADS, SSD_CHUNK, SSD_CHUNK), BF16),
               pltpu.VMEM((SSD_STEP_CHUNKS, SSD_CHUNK, SSM_INNER), BF16), pltpu.VMEM((SSD_STEP_CHUNKS, SSD_CHUNK, SSM_INNER), BF16),
               pltpu.VMEM((SSD_STEP_CHUNKS, SSD_CHUNK, SSM_INNER), F32)]
    params = pltpu.CompilerParams(dimension_semantics=("arbitrary", "arbitrary"), vmem_limit_bytes=VMEM_LIMIT)

    def bwd_chunk(s):
        return jnp.where(s < ncc, ncc - 1 - s, nc + ncc - 1 - s)

    def specs(chunk, d):
        return [
            pl.BlockSpec((None, step_rows, SSM_XBC), lambda bi, s: (bi, chunk(s), 0)),
            pl.BlockSpec((None, step_rows, LANES), lambda bi, s: (bi, chunk(s), d)),
            pl.BlockSpec((1, LANES), lambda bi, s: (0, 0)),
            pl.BlockSpec((1, LANES), lambda bi, s: (0, 0)),
        ]

    def inner_spec(chunk):
        return pl.BlockSpec((None, step_rows, SSM_INNER), lambda bi, s: (bi, chunk(s), 0))

    row_spec = pl.BlockSpec((1, SSM_INNER), lambda bi, s: (0, 0))
    z_spec = pl.BlockSpec((None, step_rows, SSM_INNER), lambda bi, s: (bi, bwd_chunk(s), z_col // SSM_INNER))
    y_f = pl.pallas_call(
        functools.partial(_ssd_kernel, reverse=False),
        grid=(b, nc),
        in_specs=specs(lambda s: s, 0),
        out_specs=inner_spec(lambda s: s),
        out_shape=jax.ShapeDtypeStruct((b, t, SSM_INNER), F32),
        scratch_shapes=scratch,
        compiler_params=params,
        name="ssd_fwd",
    )(xbc_act, dt2, dtb[0:1], a_neg[0:1])
    return pl.pallas_call(
        functools.partial(_ssd_kernel, reverse=True),
        grid=(b, nc),
        in_specs=specs(bwd_chunk, 1) + [inner_spec(bwd_chunk), z_spec, row_spec, row_spec],
        out_specs=inner_spec(bwd_chunk),
        out_shape=jax.ShapeDtypeStruct((b, t, SSM_INNER), BF16),
        scratch_shapes=scratch,
        compiler_params=params,
        name="ssd_bwd",
    )(xbc_act, dt2, dtb[1:2], a_neg[1:2], y_f, p_rm, dsk, norm_w.reshape(1, SSM_INNER).astype(F32))


HY_N2 = LANES
HY_CT = LANES
VMEM_LIMIT_HYENA = 60 * 1024 * 1024


def _hy_dims(L):
    n1 = 2 * L // HY_N2
    k1n = n1 // 2 + 1
    k1p = -(-k1n // SUBLANES) * SUBLANES
    return n1, k1n, k1p


def _hy_tables(L, n1_rows):
    n1, k1n, k1p = _hy_dims(L)
    n = 2 * L
    k1 = np.arange(k1n, dtype=np.float64)[None, :, None]
    nn = (HY_N2 * np.arange(n1_rows, dtype=np.float64)[None, None, :] + np.arange(HY_N2, dtype=np.float64)[:, None, None])
    ang = 2.0 * np.pi * ((k1 * nn) % n) / n
    m1 = np.zeros((HY_N2, 2 * k1p, n1_rows), np.float32)
    m1[:, :k1n] = np.cos(ang)
    m1[:, k1p:k1p + k1n] = -np.sin(ang)
    m4 = np.transpose(m1, (0, 2, 1))
    kk = np.arange(HY_N2, dtype=np.float64)
    a2 = 2.0 * np.pi * ((kk[:, None] * kk[None, :]) % HY_N2) / HY_N2
    c, s = np.cos(a2), np.sin(a2)
    f3 = np.block([[c, s], [-s, c]]).astype(np.float32)
    f3i = np.block([[c, -s], [s, c]]).astype(np.float32)
    return jnp.asarray(m1, BF16), jnp.asarray(m4, BF16), jnp.asarray(f3, BF16), jnp.asarray(f3i, BF16)


def _hy_stage1(u_refs, a_refs, m1_ref, n1_rows, k1p):
    def body(n2, carry):
        xs = jnp.concatenate([u_ref[pl.ds(n2, n1_rows, stride=HY_N2), :] for u_ref in u_refs], axis=1).astype(BF16)
        a = jnp.dot(m1_ref[n2], xs, preferred_element_type=F32)
        for i, a_ref in enumerate(a_refs):
            a_ref[pl.ds(pl.multiple_of(n2 * 2 * k1p, 2 * k1p), 2 * k1p), :] = a[:, i * HY_CT:(i + 1) * HY_CT]
        return carry

    lax.fori_loop(0, HY_N2, body, 0, unroll=8)


def _hy_spectrum_slab(a_refs, f3_ref, k1s, k1p):
    blk = jnp.concatenate([jnp.concatenate([a_ref[pl.ds(k1, HY_N2, stride=2 * k1p), :], a_ref[pl.ds(k1p + k1, HY_N2, stride=2 * k1p), :]], axis=0) for k1 in k1s for a_ref in a_refs], axis=1)
    return jnp.dot(f3_ref[...], blk.astype(BF16), preferred_element_type=F32)


def _hy_k1_loop(body, k1n):
    def pair(p, carry):
        body((2 * p, 2 * p + 1))
        return carry

    lax.fori_loop(0, (k1n - 1) // 2, pair, 0, unroll=math.gcd((k1n - 1) // 2, 4))
    body((k1n - 1,))


def _hy_slab_rows(k1):
    start = k1 * 2 * HY_N2
    return pl.ds(start if isinstance(k1, int) else pl.multiple_of(start, 2 * HY_N2), 2 * HY_N2)


def _hy_conv(u_ref, yo_ref, a_ref, y_ref, h_ref, order, m1_ref, m4_ref, f3_ref, f3i_ref, n1_rows, k1n, k1p):
    _hy_stage1((u_ref,), (a_ref,), m1_ref, n1_rows, k1p)

    def stage2(k1s):
        x = _hy_spectrum_slab((a_ref,), f3_ref, k1s, k1p)
        h = jnp.concatenate([h_ref[order, _hy_slab_rows(k1), :] for k1 in k1s], axis=1).astype(F32)
        xr, xi, hr, hi = x[:HY_N2], x[HY_N2:], h[:HY_N2], h[HY_N2:]
        z = jnp.concatenate([xr * hr - xi * hi, xr * hi + xi * hr], axis=0).astype(BF16)
        c = jnp.dot(f3i_ref[...], z, preferred_element_type=F32)
        for i, k1 in enumerate(k1s):
            lanes = slice(i * HY_CT, (i + 1) * HY_CT)
            y_ref[pl.ds(k1, HY_N2, stride=2 * k1p), :] = c[:HY_N2, lanes]
            y_ref[pl.ds(k1p + k1, HY_N2, stride=2 * k1p), :] = c[HY_N2:, lanes]

    _hy_k1_loop(stage2, k1n)

    def stage3(n2, carry):
        d = y_ref[pl.ds(pl.multiple_of(n2 * 2 * k1p, 2 * k1p), 2 * k1p), :].astype(BF16)
        yo_ref[pl.ds(n2, n1_rows, stride=HY_N2), :] = jnp.dot(m4_ref[n2], d, preferred_element_type=F32)
        return carry

    lax.fori_loop(0, HY_N2, stage3, 0, unroll=8)


def _hyena_kernel(v_ref, x1_ref, x2_ref, h_ref, m1_ref, m4_ref, f3_ref, f3i_ref, bias_ref, o_ref, a_ref, y_ref, u_ref, yo_ref, *, n1_rows, k1n, k1p):
    @pl.when(jnp.logical_and(pl.program_id(0) == 0, pl.program_id(1) == 0))
    def _():
        y_ref[...] = jnp.zeros_like(y_ref)

    u_ref[...] = v_ref[...].astype(F32)
    for order, gate_ref in enumerate((x1_ref, x2_ref)):
        _hy_conv(u_ref, yo_ref, a_ref, y_ref, h_ref, order, m1_ref, m4_ref, f3_ref, f3i_ref, n1_rows, k1n, k1p)
        z = gate_ref[...].astype(F32) * (yo_ref[...] + u_ref[...] * bias_ref[order:order + 1, :])
        if order == 0:
            u_ref[...] = z
        else:
            o_ref[...] = z.astype(o_ref.dtype)


def _hy_filter_kernel(hid_ref, wf_ref, wb_ref, bf_ref, bb_ref, dl_ref, m1_ref, f3_ref, o_ref, af_ref, ab_ref, uf_ref, ub_ref, *, n1_rows, k1n, k1p, scale_mid, scale_edge):
    L = uf_ref.shape[0]
    hid_rows = math.gcd(L, 2 * HY_N2)

    def fill(w_ref, b_ref, dst_ref, drop_first):
        w = _split3(w_ref[...])

        def body(i, energy):
            rows = pl.ds(pl.multiple_of(i * hid_rows, hid_rows), hid_rows)
            hid = _split3(hid_ref[rows, :])
            acc = sum(jnp.dot(hid[p], w[q], preferred_element_type=F32) for p in range(2) for q in range(2 - p))
            pos = i * hid_rows + lax.broadcasted_iota(jnp.int32, (hid_rows, HY_CT), 0)
            vals = (acc + b_ref[...]) * jnp.exp(pos.astype(F32) * (-1.0 / (L - 1)) * dl_ref[...])
            if drop_first:
                vals = jnp.where(pos == 0, 0.0, vals)
            dst_ref[rows, :] = vals
            return energy + jnp.sum(vals * vals, axis=0, keepdims=True)

        return lax.fori_loop(0, L // hid_rows, body, jnp.zeros((1, HY_CT), F32), unroll=math.gcd(L // hid_rows, 4))

    norm = lax.rsqrt(fill(wf_ref, bf_ref, uf_ref, False) + fill(wb_ref, bb_ref, ub_ref, True) + EPS)

    _hy_stage1((uf_ref, ub_ref), (af_ref, ab_ref), m1_ref, n1_rows, k1p)

    def combine(k1s):
        x_all = _hy_spectrum_slab((af_ref, ab_ref), f3_ref, k1s, k1p)
        for i, k1 in enumerate(k1s):
            xf = x_all[:, 2 * i * HY_CT:(2 * i + 1) * HY_CT]
            xb = x_all[:, (2 * i + 1) * HY_CT:(2 * i + 2) * HY_CT]
            w = norm * jnp.where(jnp.logical_or(k1 == 0, k1 == k1n - 1), scale_edge, scale_mid)
            h = jnp.concatenate([xf[:HY_N2] + xb[:HY_N2], xf[HY_N2:] - xb[HY_N2:]], axis=0)
            o_ref[_hy_slab_rows(k1), :] = (h * w).astype(o_ref.dtype)

    _hy_k1_loop(combine, k1n)


def _single(block_shape, index_map):
    return pl.BlockSpec(block_shape, index_map, pipeline_mode=pl.Buffered(1))


def _hyena_filter_spectrum_pallas(hidden, w3, b3, deltas):
    L, ffn = hidden.shape
    c = deltas.shape[0]
    nct = c // HY_CT
    n = 2 * L
    n1, k1n, k1p = _hy_dims(L)
    n1_rows = L // HY_N2
    m1, _, f3, _ = _hy_tables(L, n1_rows)
    kern = functools.partial(_hy_filter_kernel, n1_rows=n1_rows, k1n=k1n, k1p=k1p, scale_mid=2.0 / n, scale_edge=1.0 / n)
    return pl.pallas_call(
        kern,
        grid=(HY_ORDER, nct),
        in_specs=[
            _single((L, ffn), lambda o, j: (0, 0)),
            pl.BlockSpec((ffn, HY_CT), lambda o, j: (0, o * nct + j)),
            pl.BlockSpec((ffn, HY_CT), lambda o, j: (0, (HY_ORDER + o) * nct + j)),
            pl.BlockSpec((1, HY_CT), lambda o, j: (0, o * nct + j)),
            pl.BlockSpec((1, HY_CT), lambda o, j: (0, (HY_ORDER + o) * nct + j)),
            pl.BlockSpec((1, HY_CT), lambda o, j: (0, j)),
            _single((HY_N2, 2 * k1p, n1_rows), lambda o, j: (0, 0, 0)),
            _single((2 * HY_N2, 2 * HY_N2), lambda o, j: (0, 0)),
        ],
        out_specs=pl.BlockSpec((None, k1n * 2 * HY_N2, HY_CT), lambda o, j: (o, 0, j)),
        out_shape=jax.ShapeDtypeStruct((HY_ORDER, k1n * 2 * HY_N2, c), BF16),
        scratch_shapes=[
            pltpu.VMEM((k1p * 2 * HY_N2, HY_CT), F32),
            pltpu.VMEM((k1p * 2 * HY_N2, HY_CT), F32),
            pltpu.VMEM((L, HY_CT), F32),
            pltpu.VMEM((L, HY_CT), F32),
        ],
        compiler_params=pltpu.CompilerParams(dimension_semantics=("arbitrary", "arbitrary"), vmem_limit_bytes=VMEM_LIMIT_HYENA),
        name="hyena_filter_dft",
    )(hidden, w3, w3, b3.reshape(1, -1), b3.reshape(1, -1), deltas.reshape(1, c), m1, f3)


def _hyena_long(hy, h_spec, bias):
    b, L, c3 = hy.shape
    c = c3 // (HY_ORDER + 1)
    nct = c // HY_CT
    n1, k1n, k1p = _hy_dims(L)
    n1_rows = L // HY_N2
    m1, m4, f3, f3i = _hy_tables(L, n1_rows)
    kern = functools.partial(_hyena_kernel, n1_rows=n1_rows, k1n=k1n, k1p=k1p)

    def col(part):
        return _single((None, L, HY_CT), lambda j, bi: (bi, 0, part * nct + j))

    return pl.pallas_call(
        kern,
        grid=(nct, b),
        in_specs=[
            col(0), col(1), col(2),
            _single((HY_ORDER, k1n * 2 * HY_N2, HY_CT), lambda j, bi: (0, 0, j)),
            _single((HY_N2, 2 * k1p, n1_rows), lambda j, bi: (0, 0, 0)),
            _single((HY_N2, n1_rows, 2 * k1p), lambda j, bi: (0, 0, 0)),
            _single((2 * HY_N2, 2 * HY_N2), lambda j, bi: (0, 0)),
            _single((2 * HY_N2, 2 * HY_N2), lambda j, bi: (0, 0)),
            pl.BlockSpec((HY_ORDER, HY_CT), lambda j, bi: (0, j)),
        ],
        out_specs=pl.BlockSpec((None, L, HY_CT), lambda j, bi: (bi, 0, j)),
        out_shape=jax.ShapeDtypeStruct((b, L, c), BF16),
        scratch_shapes=[
            pltpu.VMEM((k1p * 2 * HY_N2, HY_CT), F32),
            pltpu.VMEM((HY_N2 * 2 * k1p, HY_CT), F32),
            pltpu.VMEM((L, HY_CT), F32),
            pltpu.VMEM((L, HY_CT), F32),
        ],
        compiler_params=pltpu.CompilerParams(dimension_semantics=("arbitrary", "arbitrary"), vmem_limit_bytes=VMEM_LIMIT_HYENA),
        name="hyena_long_conv",
    )(hy, hy, hy, h_spec, m1, m4, f3, f3i, bias.astype(F32))


HY_CTX_CT = 256


def _hy_ctx_tables(L):
    n = 2 * L
    kb = L + 1
    kp = -(-kb // LANES) * LANES
    ang = 2.0 * np.pi * ((np.arange(kb, dtype=np.float64)[:, None] * np.arange(L, dtype=np.float64)[None, :]) % n) / n
    fwd = np.zeros((2 * kp, L), np.float32)
    fwd[:kb] = np.cos(ang)
    fwd[kp:kp + kb] = -np.sin(ang)
    return jnp.asarray(fwd, BF16), jnp.asarray(fwd.T, BF16), kb, kp


def _hy_ctx_filter_kernel(fwd_ref, bwd_ref, f_ref, o_ref, *, kb, kp, n):
    fwd = fwd_ref[...]
    bwd = jnp.where(lax.broadcasted_iota(jnp.int32, fwd.shape, 0) == 0, 0.0, bwd_ref[...])
    norm = lax.rsqrt(jnp.sum(fwd * fwd, axis=0, keepdims=True) + jnp.sum(bwd * bwd, axis=0, keepdims=True) + EPS)
    hf = jnp.dot(f_ref[...], fwd.astype(BF16), preferred_element_type=F32)
    hb = jnp.dot(f_ref[...], bwd.astype(BF16), preferred_element_type=F32)
    row = lax.broadcasted_iota(jnp.int32, hf.shape, 0)
    imag = row >= kp
    k = jnp.where(imag, row - kp, row)
    wk = jnp.where(jnp.logical_or(k == 0, k == kb - 1), 1.0 / n, 2.0 / n)
    o_ref[...] = (hf + jnp.where(imag, -hb, hb)) * (wk * norm)


def _hy_ctx_kernel(v_ref, x1_ref, x2_ref, h_ref, f_ref, g_ref, bias_ref, o_ref, *, kp):
    u = v_ref[...].astype(F32)
    for order, gate_ref in enumerate((x1_ref, x2_ref)):
        x = jnp.dot(f_ref[...], u.astype(BF16), preferred_element_type=F32)
        h = h_ref[order]
        xr, xi, hr, hi = x[:kp], x[kp:], h[:kp], h[kp:]
        z = jnp.concatenate([xr * hr - xi * hi, xr * hi + xi * hr], axis=0).astype(BF16)
        y = jnp.dot(g_ref[...], z, preferred_element_type=F32)
        u = gate_ref[...].astype(F32) * (y + u * bias_ref[order:order + 1, :])
    o_ref[...] = u.astype(o_ref.dtype)


def _hyena_ctx(hy, taps, bias):
    b, L, _ = hy.shape
    c = hy.shape[2] // (HY_ORDER + 1)
    ct = HY_CTX_CT
    nct = c // ct
    f_mat, g_mat, kb, kp = _hy_ctx_tables(L)
    params = pltpu.CompilerParams(dimension_semantics=("arbitrary", "arbitrary"), vmem_limit_bytes=VMEM_LIMIT)
    h_spec = pl.pallas_call(
        functools.partial(_hy_ctx_filter_kernel, kb=kb, kp=kp, n=2 * L),
        grid=(HY_ORDER, nct),
        in_specs=[
            pl.BlockSpec((L, ct), lambda o, j: (0, o * nct + j)),
            pl.BlockSpec((L, ct), lambda o, j: (0, (HY_ORDER + o) * nct + j)),
            pl.BlockSpec((2 * kp, L), lambda o, j: (0, 0)),
        ],
        out_specs=pl.BlockSpec((None, 2 * kp, ct), lambda o, j: (o, 0, j)),
        out_shape=jax.ShapeDtypeStruct((HY_ORDER, 2 * kp, c), F32),
        compiler_params=params,
        name="hyena_ctx_filter",
    )(taps, taps, f_mat)

    def col(part):
        return pl.BlockSpec((None, L, ct), lambda j, bi: (bi, 0, part * nct + j))

    return pl.pallas_call(
        functools.partial(_hy_ctx_kernel, kp=kp),
        grid=(nct, b),
        in_specs=[
            col(0), col(1), col(2),
            pl.BlockSpec((HY_ORDER, 2 * kp, ct), lambda j, bi: (0, 0, j)),
            pl.BlockSpec((2 * kp, L), lambda j, bi: (0, 0)),
            pl.BlockSpec((L, 2 * kp), lambda j, bi: (0, 0)),
            pl.BlockSpec((HY_ORDER, ct), lambda j, bi: (0, j)),
        ],
        out_specs=pl.BlockSpec((None, L, ct), lambda j, bi: (bi, 0, j)),
        out_shape=jax.ShapeDtypeStruct((b, L, c), BF16),
        compiler_params=params,
        name="hyena_ctx_conv",
    )(hy, hy, hy, h_spec, f_mat, g_mat, bias.astype(F32))


def _split_cols(t, sizes):
    return jnp.split(t, np.cumsum(sizes)[:-1].tolist(), axis=-1)


def _to_col_major(t, rows):
    b, rest = t.shape[0], t.shape[2:]
    return jnp.swapaxes(t.reshape((b, rows, GRID_W) + rest), 1, 2).reshape((b, rows * GRID_W) + rest)


def _from_col_major(t, rows):
    b, rest = t.shape[0], t.shape[2:]
    return jnp.swapaxes(t.reshape((b, GRID_W, rows) + rest), 1, 2).reshape((b, rows * GRID_W) + rest)


def _hyena_filter_hidden(L, w1, b1, w2, b2, freq):
    hp = lax.Precision.HIGHEST
    t = jnp.linspace(0.0, 1.0, L, dtype=F32)[:, None]
    w = 2.0 * math.pi * jnp.arange(L, dtype=F32)[:, None] / L
    bands = jnp.linspace(1e-4, HY_BANDS - 1, HY_BANDS, dtype=F32)
    feats = jnp.concatenate([t, jnp.cos(bands * w), -jnp.sin(bands * w)], axis=-1)
    h = jnp.sin(freq[0] * (jnp.dot(feats, w1, precision=hp) + b1))
    return jnp.sin(freq[1] * (jnp.dot(h, w2, precision=hp) + b2))


def _hyena_decay_rates():
    max_decay = math.log(HY_DECAY_TARGET) / HY_FAST_DECAY
    min_decay = math.log(HY_DECAY_TARGET) / HY_SLOW_DECAY
    return jnp.abs(jnp.linspace(min_decay, max_decay, HY_WIDTH, dtype=F32))


def _hyena_filter_taps(L, w1, b1, w2, b2, w3, b3, freq):
    h = jnp.dot(_hyena_filter_hidden(L, w1, b1, w2, b2, freq), w3, precision=lax.Precision.HIGHEST) + b3
    t = jnp.linspace(0.0, 1.0, L, dtype=F32)[:, None]
    return h * jnp.tile(jnp.exp(-t * _hyena_decay_rates()), (1, 2 * HY_ORDER))


def _mixer_branches(h_rm, w_in, lb, n_ctx, ssm_conv_w, ssm_conv_b, ssm_dt_bias, ssm_a_log, ssm_d, ssm_norm, hy_conv_w, hy_conv_b, hy_w1, hy_b1, hy_w2, hy_b2, hy_w3, hy_b3, hy_freq, hy_bias, hg_norm):
    b, t, d = h_rm.shape
    n_lat = t - n_ctx
    rows = n_lat // GRID_W
    w_z, w_xbc, w_dt, w_hy, w_q, w_f, w_i, w_g, w_gate = _split_cols(w_in, IN_SIZES)
    h2 = h_rm.reshape(b * t, d)
    rm_parts = (w_z, w_xbc, w_hy, w_g, w_gate)
    col_z, col_xbc, col_hy, col_g, col_gate = np.cumsum([0] + [w.shape[1] for w in rm_parts[:-1]]).tolist()
    p_rm = _mm(h2, jnp.concatenate(rm_parts, axis=1).astype(BF16), BF16).reshape(b, t, -1)

    zero_pad = jnp.zeros((d, LANES - SSM_HEADS), F32)
    w_dt2 = jnp.concatenate([w_dt[:, :SSM_HEADS], zero_pad, w_dt[:, SSM_HEADS:], zero_pad], axis=1)
    dt2 = _mm(h2, w_dt2.astype(BF16), F32).reshape(b, t, 2 * LANES)
    xbc_act = _dwconv_stream(p_rm, ssm_conv_w, ssm_conv_b, n_ctx, True, c0=col_xbc)
    ym = _ssd_scan(xbc_act, dt2, p_rm, col_z, ssm_dt_bias, ssm_a_log, ssm_d, ssm_norm, n_ctx)

    hy_ctx, hy_lat = _dwconv_stream(p_rm, hy_conv_w, hy_conv_b, n_ctx, False, c0=col_hy, split=True)
    taps_ctx = _hyena_filter_taps(n_ctx, hy_w1, hy_b1, hy_w2, hy_b2, hy_w3, hy_b3, hy_freq)
    yh_ctx = _hyena_ctx(hy_ctx, taps_ctx, hy_bias)
    h_spec = _hyena_filter_spectrum_pallas(_hyena_filter_hidden(n_lat, hy_w1, hy_b1, hy_w2, hy_b2, hy_freq), hy_w3, hy_b3, _hyena_decay_rates())
    yh_lat = _hyena_long(hy_lat, h_spec, hy_bias)

    h_cm = jnp.concatenate([h_rm[:, :n_ctx], _to_col_major(h_rm[:, n_ctx:], rows)], axis=1).reshape(b * t, d)

    p_cm = _mm(h_cm, jnp.concatenate([w_q, w_f, w_i], axis=1).astype(BF16), BF16).reshape(b, t, -1)
    og = _hgrn_scan(p_cm, lb, hg_norm, n_ctx)
    return ym, (yh_ctx, yh_lat), (og, _from_col_major(og[:, n_ctx:], rows)), p_rm, col_g, col_gate


ROW_TILE = 256
MOD_ROWS = SUBLANES
M_SHIFT_MIX, M_SCALE_MIX, M_GATE_MIX, M_SHIFT_FFN, M_SCALE_FFN, M_GATE_FFN = range(6)
ROW_PARAMS = pltpu.CompilerParams(dimension_semantics=("arbitrary", "arbitrary"), vmem_limit_bytes=VMEM_LIMIT)


def _rms(x):
    return x * lax.rsqrt(jnp.mean(x * x, axis=-1, keepdims=True) + EPS)


def _mrow(m_ref, r):
    return m_ref[r:r + 1, :]


def _row_spec(width):
    return pl.BlockSpec((None, ROW_TILE, width), lambda bi, i: (bi, i, 0))


def _vec_spec(width):
    return pl.BlockSpec((1, width), lambda bi, i: (0, 0))


def _mat_spec(k, n):
    return pl.BlockSpec((k, n), lambda bi, i: (0, 0))


def _mod_spec(n_ctx):
    return pl.BlockSpec((None, None, MOD_ROWS, D_MODEL), lambda bi, i: (bi, jnp.where(i < n_ctx // ROW_TILE, 0, 1), 0, 0))


def _norm_mod_kernel(x_ref, w_ref, m_ref, o_ref):
    y = _rms(x_ref[...]) * w_ref[...]
    o_ref[...] = (y * (1.0 + _mrow(m_ref, M_SCALE_MIX)) + _mrow(m_ref, M_SHIFT_MIX)).astype(o_ref.dtype)


def _norm_mod(xs, w, mods, n_ctx):
    b, t, d = xs.shape
    return pl.pallas_call(
        _norm_mod_kernel,
        grid=(b, t // ROW_TILE),
        in_specs=[_row_spec(d), _vec_spec(d), _mod_spec(n_ctx)],
        out_specs=_row_spec(d),
        out_shape=jax.ShapeDtypeStruct((b, t, d), BF16),
        compiler_params=ROW_PARAMS,
        name="norm_mod",
    )(xs, w.reshape(1, d), mods)


def _merge_kernel(ym_ref, yhc_ref, yhl_ref, ogc_ref, ogl_ref, g_ref, gm_ref, gh_ref, gg_ref, x_ref, m_ref, w1_ref, w2_ref, w3_ref, wo_ref, npost_ref, npre_ref, rw_ref, rb_ref, xo_ref, h_ref, lg_ref, *, n_ctx_tiles):
    is_ctx = pl.program_id(1) < n_ctx_tiles
    yh = jnp.where(is_ctx, yhc_ref[...], yhl_ref[...])
    og = jnp.where(is_ctx, ogc_ref[...], ogl_ref[...])
    def sig(ref):
        return jax.nn.sigmoid(ref[...].astype(F32))

    go = g_ref[...].astype(F32)
    yg = (og.astype(F32) * (go * jax.nn.sigmoid(go))).astype(BF16)
    merged = sig(gm_ref) * jnp.dot(ym_ref[...], w1_ref[...], preferred_element_type=F32)
    merged = merged + sig(gh_ref) * jnp.dot(yh, w2_ref[...], preferred_element_type=F32)
    merged = merged + sig(gg_ref) * jnp.dot(yg, w3_ref[...], preferred_element_type=F32)
    mix = jnp.dot(merged.astype(BF16), wo_ref[...], preferred_element_type=F32)
    x = x_ref[...] + _mrow(m_ref, M_GATE_MIX) * (_rms(mix) * npost_ref[...])
    xo_ref[...] = x
    h = (_rms(x) * npre_ref[...] * (1.0 + _mrow(m_ref, M_SCALE_FFN)) + _mrow(m_ref, M_SHIFT_FFN)).astype(BF16)
    h_ref[...] = h
    lg_ref[...] = jnp.dot(h, rw_ref[...], preferred_element_type=F32) + rb_ref[...]


def _merge(ym, yh_parts, og_parts, p_rm, col_g, col_gate, xs, mods, w_br_ssm, w_br_hy, w_br_hg, w_out, norm_post, norm_ffn_pre, router_w, router_b, n_ctx):
    b, t, d = xs.shape
    rw = jnp.pad(router_w, ((0, 0), (0, LANES - N_EXPERTS))).astype(BF16)
    rb = jnp.pad(router_b, (0, LANES - N_EXPERTS)).reshape(1, LANES).astype(F32)

    def col_spec(col):
        return pl.BlockSpec((None, ROW_TILE, d), lambda bi, i: (bi, i, col // d))

    nct = n_ctx // ROW_TILE
    ctx_spec = pl.BlockSpec((None, ROW_TILE, d), lambda bi, i: (bi, jnp.minimum(i, nct - 1), 0))
    lat_spec = pl.BlockSpec((None, ROW_TILE, d), lambda bi, i: (bi, jnp.maximum(i - nct, 0), 0))

    return pl.pallas_call(
        functools.partial(_merge_kernel, n_ctx_tiles=nct),
        grid=(b, t // ROW_TILE),
        in_specs=[_row_spec(d), ctx_spec, lat_spec, ctx_spec, lat_spec, col_spec(col_g), col_spec(col_gate), col_spec(col_gate + d), col_spec(col_gate + 2 * d), _row_spec(d), _mod_spec(n_ctx),
                  _mat_spec(d, d), _mat_spec(d, d), _mat_spec(d, d), _mat_spec(d, d), _vec_spec(d), _vec_spec(d), _mat_spec(d, LANES), _vec_spec(LANES)],
        out_specs=[_row_spec(d), _row_spec(d), _row_spec(LANES)],
        out_shape=[jax.ShapeDtypeStruct((b, t, d), F32), jax.ShapeDtypeStruct((b, t, d), BF16), jax.ShapeDtypeStruct((b, t, LANES), F32)],
        compiler_params=ROW_PARAMS,
        name="branch_merge",
    )(ym, yh_parts[0], yh_parts[1], og_parts[0], og_parts[1], p_rm, p_rm, p_rm, p_rm, xs, mods, w_br_ssm.astype(BF16), w_br_hy.astype(BF16), w_br_hg.astype(BF16), w_out.astype(BF16),
      norm_post.reshape(1, d), norm_ffn_pre.reshape(1, d), rw, rb)


def _post_ffn_kernel(y0_ref, y1_ref, y2_ref, y3_ref, x_ref, m_ref, w_ref, o_ref):
    f = y0_ref[...].astype(F32) + y1_ref[...].astype(F32) + y2_ref[...].astype(F32) + y3_ref[...].astype(F32)
    o_ref[...] = x_ref[...] + _mrow(m_ref, M_GATE_FFN) * (_rms(f) * w_ref[...])


def _post_ffn(f4, xs, mods, norm_post, n_ctx, skip):
    b, _, d = xs.shape
    t = f4.shape[2]
    i0 = skip // ROW_TILE

    def k_spec(k):
        return pl.BlockSpec((None, None, ROW_TILE, d), lambda bi, i: (k, bi, i, 0))

    return pl.pallas_call(
        _post_ffn_kernel,
        grid=(b, t // ROW_TILE),
        in_specs=[k_spec(k) for k in range(TOP_K)] + [
            pl.BlockSpec((None, ROW_TILE, d), lambda bi, i: (bi, i0 + i, 0)),
            pl.BlockSpec((None, None, MOD_ROWS, d), lambda bi, i: (bi, jnp.where(i0 + i < n_ctx // ROW_TILE, 0, 1), 0, 0)),
            _vec_spec(d)],
        out_specs=_row_spec(d),
        out_shape=jax.ShapeDtypeStruct((b, t, d), F32),
        compiler_params=ROW_PARAMS,
        name="post_ffn",
    )(f4, f4, f4, f4, xs, mods, norm_post.reshape(1, d))


def _moe_ffn(h2, logits, t_per_b, skip, li, w1, b1, w2, b2):
    t = logits.shape[0]
    d = h2.shape[1]
    n = t * TOP_K
    n_tiles = n // MOE_BLOCK
    top_v, top_e = lax.top_k(logits, TOP_K)
    gate_w = jax.nn.softmax(top_v, axis=-1)
    flat_e = top_e.reshape(n).astype(jnp.int32)
    iota = jnp.arange(n, dtype=jnp.int32)
    _, order, sw = lax.sort((flat_e, iota, gate_w.reshape(n)), num_keys=1, is_stable=True)
    _, inv = lax.sort((order, iota), num_keys=1)
    tok = order // TOP_K
    xs = h2[tok + (tok // t_per_b + 1) * skip]
    counts = jnp.sum((flat_e[:, None] == jnp.arange(N_EXPERTS, dtype=jnp.int32)[None, :]).astype(jnp.int32), axis=0)
    end = jnp.cumsum(counts)
    start = end - counts
    first_tile = start // MOE_BLOCK
    n_items = jnp.where(counts > 0, (end - 1) // MOE_BLOCK - first_tile + 1, 0)
    items_end = jnp.cumsum(n_items)
    w = jnp.arange(n_tiles + N_EXPERTS, dtype=jnp.int32)
    valid = w < items_end[-1]
    e_w = jnp.minimum(jnp.sum((w[:, None] >= items_end[None, :]).astype(jnp.int32), axis=1), N_EXPERTS - 1)
    tile_w = first_tile[e_w] + (w - (items_end[e_w] - n_items[e_w]))
    lo = jnp.where(valid, jnp.maximum(start[e_w], tile_w * MOE_BLOCK), 0)
    hi = jnp.where(valid, jnp.minimum(end[e_w], (tile_w + 1) * MOE_BLOCK), 0)
    tile_w = jnp.where(valid, tile_w, n_tiles - 1)
    first = jnp.concatenate([jnp.ones((1,), jnp.int32), (tile_w[1:] != tile_w[:-1]).astype(jnp.int32)])
    newexp = jnp.concatenate([jnp.ones((1,), jnp.int32), (e_w[1:] != e_w[:-1]).astype(jnp.int32)])
    ys = _moe_experts(xs, sw, tile_w.astype(jnp.int32), e_w, lo.astype(jnp.int32), hi.astype(jnp.int32), first, newexp, li, w1, b1, w2, b2)
    return ys[inv.reshape(t, TOP_K).T]


def kernel(x, c, ctx, c_ctx, w_mod, b_mod, norm_mix_pre, norm_mix_post, norm_ffn_pre, norm_ffn_post, w_in, ssm_conv_w, ssm_conv_b, ssm_dt_bias, ssm_a_log, ssm_d, ssm_norm, hy_conv_w, hy_conv_b, hy_w1, hy_b1, hy_w2, hy_b2, hy_w3, hy_b3, hy_freq, hy_bias, hg_lb_logits, hg_norm, w_br_ssm, w_br_hy, w_br_hg, w_out, router_w, router_b, exp_w1, exp_b1, exp_w2, exp_b2):
    hp = lax.Precision.HIGHEST
    b, n_lat, d = x.shape
    n_ctx = ctx.shape[1]
    lb = jax.nn.softmax(hg_lb_logits.astype(F32), axis=1)
    lb = jnp.cumsum(lb, axis=1) - lb[:, :1]
    silu_c = jax.nn.silu(c)
    silu_cc = jax.nn.silu(c_ctx)
    xs = jnp.concatenate([ctx, x], axis=1)
    for li in range(DEPTH):
        mx = (jnp.dot(silu_c, w_mod[li], precision=hp) + b_mod[li]).reshape(b, 1, 6, d)
        mc = jnp.broadcast_to((jnp.dot(silu_cc, w_mod[li], precision=hp) + b_mod[li]).reshape(1, 1, 6, d), (b, 1, 6, d))
        mods = jnp.pad(jnp.concatenate([mc, mx], axis=1), ((0, 0), (0, 0), (0, MOD_ROWS - 6), (0, 0)))
        h = _norm_mod(xs, norm_mix_pre[li], mods, n_ctx)
        ym, yh_parts, og_parts, p_rm, col_g, col_gate = _mixer_branches(h, w_in[li], lb[:, li], n_ctx, ssm_conv_w[li], ssm_conv_b[li], ssm_dt_bias[li], ssm_a_log[li], ssm_d[li], ssm_norm[li], hy_conv_w[li], hy_conv_b[li], hy_w1[li], hy_b1[li], hy_w2[li], hy_b2[li], hy_w3[li], hy_b3[li], hy_freq[li], hy_bias[li], hg_norm[li])
        xs, h_ffn, logits = _merge(ym, yh_parts, og_parts, p_rm, col_g, col_gate, xs, mods, w_br_ssm[li], w_br_hy[li], w_br_hg[li], w_out[li], norm_mix_post[li], norm_ffn_pre[li], router_w[li], router_b[li], n_ctx)
        skip = n_ctx if li == DEPTH - 1 else 0
        t = xs.shape[1] - skip
        f4 = _moe_ffn(h_ffn.reshape(-1, d), logits[:, skip:, :N_EXPERTS].reshape(b * t, N_EXPERTS), t, skip, li, exp_w1, exp_b1[li], exp_w2, exp_b2[li])
        xs = _post_ffn(f4.reshape(TOP_K, b, t, d), xs, mods, norm_ffn_post[li], n_ctx, skip)
    return xs
```

```python
import functools
import math

import jax
import jax.numpy as jnp
import numpy as np
from jax import lax
from jax.experimental import pallas as pl
from jax.experimental.pallas import tpu as pltpu

D_MODEL = 1024
DEPTH = 2
GRID_W = 64

SSM_HEADS = 16
SSM_HEAD_DIM = 64
SSM_INNER = SSM_HEADS * SSM_HEAD_DIM
SSM_STATE = 128
SSM_GROUPS = 4
SSD_CHUNK = 128
SSM_XBC = SSM_INNER + 2 * SSM_GROUPS * SSM_STATE

HY_WIDTH = D_MODEL
HY_ORDER = 2
HY_BANDS = 16
HY_FAST_DECAY = 0.3
HY_SLOW_DECAY = 1.5
HY_DECAY_TARGET = 1e-2

HG_HEADS = 8
HG_KDIM = 128
HG_VDIM = D_MODEL // HG_HEADS
HG_QK = HG_HEADS * HG_KDIM
HG_V = HG_HEADS * HG_VDIM
HG_CHUNK = 64
F_FLOOR = 1e-20

N_EXPERTS = 32
TOP_K = 4
D_FF = D_MODEL
SWIGLU_LIMIT = 7.0
SWIGLU_ALPHA = 1.702
MOE_BLOCK = 512

N_BRANCHES = 3
IN_SIZES = (SSM_INNER, SSM_XBC, 2 * SSM_HEADS, (HY_ORDER + 1) * HY_WIDTH, HG_QK, 2 * HG_QK, HG_V, HG_V, N_BRANCHES * D_MODEL)
EPS = 1e-6
F32 = jnp.float32
BF16 = jnp.bfloat16

LANES = 128
VMEM_LIMIT = 56 * 1024 * 1024


def _mm_kernel(a_ref, b_ref, o_ref):
    o_ref[...] = jnp.dot(a_ref[...], b_ref[...], preferred_element_type=F32).astype(o_ref.dtype)


def _mm(a, b, out_dtype=F32, tm=1024, tn=2048):
    m, k = a.shape
    n = b.shape[1]
    tm = math.gcd(m, tm)
    tn = math.gcd(n, tn)
    assert tm % SUBLANES == 0 and tn % LANES == 0, (m, n, tm, tn)
    return pl.pallas_call(
        _mm_kernel,
        grid=(n // tn, m // tm),
        in_specs=[pl.BlockSpec((tm, k), lambda j, i: (i, 0)), pl.BlockSpec((k, tn), lambda j, i: (0, j))],
        out_specs=pl.BlockSpec((tm, tn), lambda j, i: (i, j)),
        out_shape=jax.ShapeDtypeStruct((m, n), out_dtype),
        compiler_params=pltpu.CompilerParams(dimension_semantics=("arbitrary", "arbitrary"), vmem_limit_bytes=VMEM_LIMIT),
        name="dense_mm",
    )(a, b)


def _moe_kernel(tile_ref, exp_ref, lo_ref, hi_ref, first_ref, newexp_ref, x_ref, sw_ref, w1_ref, b1_ref, w2_ref, b2_ref, o_ref, w1b_ref, w2b_ref):
    del exp_ref
    w = pl.program_id(0)
    lo, hi = lo_ref[w], hi_ref[w]

    @pl.when(newexp_ref[w] == 1)
    def _():
        def cast_rows(i, carry):
            rows = pl.ds(pl.multiple_of(i * LANES, LANES), LANES)
            w1b_ref[rows, :] = w1_ref[rows, :].astype(BF16)
            w2b_ref[rows, :] = w2_ref[rows, :].astype(BF16)
            return carry

        lax.fori_loop(0, D_MODEL // LANES, cast_rows, 0)

    @pl.when(hi > lo)
    def _():
        hh = jnp.dot(x_ref[...], w1b_ref[...], preferred_element_type=F32) + b1_ref[...]
        g = jnp.minimum(hh[:, :D_FF], SWIGLU_LIMIT)
        u = jnp.clip(hh[:, D_FF:], -SWIGLU_LIMIT, SWIGLU_LIMIT)
        act = (u + 1.0) * g * jax.nn.sigmoid(SWIGLU_ALPHA * g)
        y = jnp.dot(act.astype(BF16), w2b_ref[...], preferred_element_type=F32) + b2_ref[...]
        y = (y * sw_ref[...]).astype(o_ref.dtype)
        rows = tile_ref[w] * MOE_BLOCK + lax.broadcasted_iota(jnp.int32, (MOE_BLOCK, 1), 0)
        mine = jnp.logical_and(rows >= lo, rows < hi)

        @pl.when(first_ref[w] == 1)
        def _():
            o_ref[...] = jnp.where(mine, y, jnp.zeros_like(y))

        @pl.when(first_ref[w] != 1)
        def _():
            o_ref[...] = jnp.where(mine, y, o_ref[...])


def _moe_experts(xs, sw, tile_w, exp_w, lo, hi, first, newexp, li, w1, b1, w2, b2):
    n, d = xs.shape
    grid_spec = pltpu.PrefetchScalarGridSpec(
        num_scalar_prefetch=6,
        grid=(tile_w.shape[0],),
        in_specs=[
            pl.BlockSpec((MOE_BLOCK, d), lambda w, tl, ex, lo_, hi_, fi, ne: (tl[w], 0)),
            pl.BlockSpec((MOE_BLOCK, 1), lambda w, tl, ex, lo_, hi_, fi, ne: (tl[w], 0)),
            pl.BlockSpec((None, None, d, 2 * D_FF), lambda w, tl, ex, lo_, hi_, fi, ne: (li, ex[w], 0, 0)),
            pl.BlockSpec((None, 1, 2 * D_FF), lambda w, tl, ex, lo_, hi_, fi, ne: (ex[w], 0, 0)),
            pl.BlockSpec((None, None, D_FF, d), lambda w, tl, ex, lo_, hi_, fi, ne: (li, ex[w], 0, 0)),
            pl.BlockSpec((None, 1, d), lambda w, tl, ex, lo_, hi_, fi, ne: (ex[w], 0, 0)),
        ],
        out_specs=pl.BlockSpec((MOE_BLOCK, d), lambda w, tl, ex, lo_, hi_, fi, ne: (tl[w], 0)),
        scratch_shapes=[pltpu.VMEM((d, 2 * D_FF), BF16), pltpu.VMEM((D_FF, d), BF16)],
    )
    return pl.pallas_call(
        _moe_kernel,
        grid_spec=grid_spec,
        out_shape=jax.ShapeDtypeStruct((n, d), BF16),
        compiler_params=pltpu.CompilerParams(dimension_semantics=("arbitrary",), vmem_limit_bytes=VMEM_LIMIT),
        name="moe_experts",
    )(tile_w, exp_w, lo, hi, first, newexp, xs, sw.reshape(n, 1), w1, b1.reshape(N_EXPERTS, 1, 2 * D_FF), w2, b2.reshape(N_EXPERTS, 1, d))


SUBLANES = 8
NEG_BIG = -1e30
HG_STEP_CHUNKS = 4
HIER_LEVELS = (64, 32, 16, 8, 4)
LOG2E = math.log2(math.e)


def _split3(x):
    h1 = x.astype(BF16)
    r1 = x - h1.astype(F32)
    h2 = r1.astype(BF16)
    h3 = (r1 - h2.astype(F32)).astype(BF16)
    return h1, h2, h3


def _dot_nt(a, b):
    return lax.dot_general(a, b, (((1,), (1,)), ((), ())), preferred_element_type=F32)


def _gla_kernel(q_ref, a_ref, v_ref, lb_ref, *rest, reverse):
    if reverse:
        of_ref, w_ref, o_ref, st_ref, at_ref = rest
    else:
        o_ref, st_ref, at_ref = rest
    Q = HG_CHUNK

    @pl.when(pl.program_id(1) == 0)
    def _():
        st_ref[...] = jnp.zeros_like(st_ref)

    row = lax.broadcasted_iota(jnp.int32, (Q, Q), 0)
    col = lax.broadcasted_iota(jnp.int32, (Q, Q), 1)
    tri = jnp.where((col >= row) if reverse else (col <= row), 1.0, 0.0).astype(BF16)
    same_block = {s: (row // s) == (col // s) for s in HIER_LEVELS[1:] + (2, 1)}
    rowk = lax.broadcasted_iota(jnp.int32, (Q, HG_KDIM), 0)
    q_rows = {s: ((rowk % s) < s // 2) if reverse else ((rowk % s) >= s // 2) for s in HIER_LEVELS + (2,)}
    of_w = (of_ref, w_ref) if reverse else (None, None)
    for sub in (range(HG_STEP_CHUNKS - 1, -1, -1) if reverse else range(HG_STEP_CHUNKS)):
        _gla_chunk(sub, q_ref, a_ref, v_ref, lb_ref[...], of_w, o_ref, st_ref, at_ref, tri, same_block, q_rows, reverse)


def _gla_chunk(sub, q_ref, a_ref, v_ref, lb, of_w, o_ref, st_ref, at_ref, tri, same_block, q_rows, reverse):
    Q = HG_CHUNK
    rs = slice(sub * Q, (sub + 1) * Q)
    of_ref, w_ref = of_w
    shp3 = (Q // SUBLANES, SUBLANES, HG_KDIM)
    sub3 = lax.broadcasted_iota(jnp.int32, shp3, 1)
    a = a_ref[rs, :].astype(F32)
    f_all = jnp.maximum(lb + (1.0 - lb) * jax.nn.sigmoid(a), F_FLOOR)
    kk = (1.0 - lb) * jax.nn.sigmoid(-a)
    g_all = sum(jnp.dot(tri, p, preferred_element_type=F32) for p in _split3(jnp.log(f_all) * LOG2E))
    q_all = q_ref[rs, :].astype(F32)
    q_all = q_all * jax.nn.sigmoid(q_all)
    v_all = v_ref[rs, :]
    tot = 0 if reverse else Q - 1

    def level_ref(g, s):
        half = s // 2
        m_off = half if reverse else half - 1
        if s >= 2 * SUBLANES:
            return jnp.concatenate([jnp.broadcast_to(g[b0 + m_off:b0 + m_off + 1, :], (s, HG_KDIM)) for b0 in range(0, Q, s)], axis=0)
        g3 = g.reshape(shp3)
        ref = jnp.broadcast_to(g3[:, m_off:m_off + 1, :], shp3)
        for b0 in range(s, SUBLANES, s):
            ref = jnp.where(sub3 >= b0, jnp.broadcast_to(g3[:, b0 + m_off:b0 + m_off + 1, :], shp3), ref)
        return ref.reshape(Q, HG_KDIM)

    for h in range(HG_HEADS):
        sl = slice(h * HG_KDIM, (h + 1) * HG_KDIM)
        g, qh, kh, fh = g_all[:, sl], q_all[:, sl], kk[:, sl], f_all[:, sl]
        attn = None
        for s in HIER_LEVELS:
            gref = level_ref(g, s)
            eq = jnp.exp2(jnp.where(q_rows[s], g - gref, NEG_BIG))
            ek = jnp.exp2(jnp.where(q_rows[s], NEG_BIG, gref - g))
            lvl = _dot_nt((qh * eq).astype(BF16), (kh * ek).astype(BF16))
            attn = lvl if attn is None else jnp.where(same_block[s], lvl, attn)
        lvl = _dot_nt(jnp.where(q_rows[2], qh * fh, 0.0).astype(BF16), jnp.where(q_rows[2], 0.0, kh).astype(BF16))
        attn = jnp.where(same_block[2], lvl, attn)
        attn = jnp.where(same_block[1], _dot_nt(qh.astype(BF16), kh.astype(BF16)), attn)
        at_ref[sub, h] = attn.astype(BF16)

    for h in range(HG_HEADS):
        sl = slice(h * HG_KDIM, (h + 1) * HG_KDIM)
        g, qh, kh, vb = g_all[:, sl], q_all[:, sl], kk[:, sl], v_all[:, sl]
        g_tot = g[tot:tot + 1, :]
        st = st_ref[h]
        o = _dot_nt((qh * jnp.exp2(g)).astype(BF16), st.astype(BF16))
        o = o + jnp.dot(at_ref[sub, h], vb, preferred_element_type=F32)
        k_st = (kh * jnp.exp2(g_tot - g)).astype(BF16)
        st_ref[h] = st * jnp.exp2(g_tot) + jnp.dot(vb.astype(F32).T.astype(BF16), k_st, preferred_element_type=F32)
        if reverse:
            o = o + of_ref[rs, sl]
            o = o * lax.rsqrt(jnp.mean(o * o, axis=-1, keepdims=True) + EPS) * w_ref[:, sl]
        o_ref[rs, sl] = o.astype(o_ref.dtype)


def _hgrn_scan(p_cm, lb, norm_w, n_ctx):
    b, t, _ = p_cm.shape
    col_q, col_f, col_i = 0, 1, 3
    step_rows = HG_STEP_CHUNKS * HG_CHUNK
    assert n_ctx % step_rows == 0 and t % step_rows == 0, (n_ctx, t, step_rows)
    nc, ncc = t // step_rows, n_ctx // step_rows
    blk = (None, step_rows, HG_QK)
    scratch = [pltpu.VMEM((HG_HEADS, HG_VDIM, HG_KDIM), F32), pltpu.VMEM((HG_STEP_CHUNKS, HG_HEADS, HG_CHUNK, HG_CHUNK), BF16)]
    params = pltpu.CompilerParams(dimension_semantics=("arbitrary", "arbitrary"), vmem_limit_bytes=VMEM_LIMIT)
    row_spec = pl.BlockSpec((1, HG_QK), lambda bi, s: (0, 0))

    def fwd_chunk(s):
        return s

    def bwd_chunk(s):
        return jnp.where(s < ncc, ncc - 1 - s, nc + ncc - 1 - s)

    o_f = pl.pallas_call(
        functools.partial(_gla_kernel, reverse=False),
        grid=(b, nc),
        in_specs=[
            pl.BlockSpec(blk, lambda bi, s: (bi, fwd_chunk(s), col_q)),
            pl.BlockSpec(blk, lambda bi, s: (bi, fwd_chunk(s), col_f)),
            pl.BlockSpec(blk, lambda bi, s: (bi, fwd_chunk(s), col_i)),
            row_spec,
        ],
        out_specs=pl.BlockSpec(blk, lambda bi, s: (bi, fwd_chunk(s), 0)),
        out_shape=jax.ShapeDtypeStruct((b, t, HG_V), F32),
        scratch_shapes=scratch,
        compiler_params=params,
        name="gla_fwd",
    )(p_cm, p_cm, p_cm, lb[0:1])
    return pl.pallas_call(
        functools.partial(_gla_kernel, reverse=True),
        grid=(b, nc),
        in_specs=[
            pl.BlockSpec(blk, lambda bi, s: (bi, bwd_chunk(s), col_q)),
            pl.BlockSpec(blk, lambda bi, s: (bi, bwd_chunk(s), col_f + 1)),
            pl.BlockSpec(blk, lambda bi, s: (bi, bwd_chunk(s), col_i)),
            row_spec,
            pl.BlockSpec(blk, lambda bi, s: (bi, bwd_chunk(s), 0)),
            row_spec,
        ],
        out_specs=pl.BlockSpec(blk, lambda bi, s: (bi, bwd_chunk(s), 0)),
        out_shape=jax.ShapeDtypeStruct((b, t, HG_V), BF16),
        scratch_shapes=scratch,
        compiler_params=params,
        name="gla_bwd",
    )(p_cm, p_cm, p_cm, lb[1:2], o_f, norm_w.reshape(1, HG_V))


CONV_TILE = 256
HALO = 16


def _dwconv_kernel(prev_ref, cur_ref, next_ref, w_ref, b_ref, *o_refs, taps, n_ctx_tiles, n_tiles, silu):
    i = pl.program_id(2)
    first = jnp.logical_or(i == 0, i == n_ctx_tiles)
    last = jnp.logical_or(i == n_ctx_tiles - 1, i == n_tiles - 1)
    pad = taps // 2
    xp = jnp.where(first, 0.0, prev_ref[...].astype(F32))
    xn = jnp.where(last, 0.0, next_ref[...].astype(F32))
    xcat = jnp.concatenate([xp, cur_ref[...].astype(F32), xn], axis=0)
    acc = jnp.broadcast_to(b_ref[...], cur_ref.shape).astype(F32)
    for k in range(taps):
        off = HALO - pad + k
        acc = acc + w_ref[k:k + 1, :] * xcat[off:off + CONV_TILE, :]
    if silu:
        acc = acc * jax.nn.sigmoid(acc)
    if len(o_refs) == 1:
        o_refs[0][...] = acc.astype(o_refs[0].dtype)
    else:
        ctx_ref, lat_ref = o_refs

        @pl.when(i < n_ctx_tiles)
        def _():
            ctx_ref[...] = acc.astype(ctx_ref.dtype)

        @pl.when(i >= n_ctx_tiles)
        def _():
            lat_ref[...] = acc.astype(lat_ref.dtype)


def _dwconv_stream(x, w, bias, n_ctx, silu, c0=0, ct=1024, split=False):
    b, t, _ = x.shape
    taps, c = w.shape
    n_tiles = t // CONV_TILE
    nct = n_ctx // CONV_TILE
    hb = CONV_TILE // HALO
    n_halo = t // HALO
    j0 = c0 // ct
    kern = functools.partial(_dwconv_kernel, taps=taps, n_ctx_tiles=nct, n_tiles=n_tiles, silu=silu)
    if split:
        out_specs = [pl.BlockSpec((None, CONV_TILE, ct), lambda bi, j, i: (bi, jnp.minimum(i, nct - 1), j)),
                     pl.BlockSpec((None, CONV_TILE, ct), lambda bi, j, i: (bi, jnp.maximum(i - nct, 0), j))]
        out_shape = [jax.ShapeDtypeStruct((b, n_ctx, c), BF16), jax.ShapeDtypeStruct((b, t - n_ctx, c), BF16)]
    else:
        out_specs = pl.BlockSpec((None, CONV_TILE, ct), lambda bi, j, i: (bi, i, j))
        out_shape = jax.ShapeDtypeStruct((b, t, c), BF16)
    return pl.pallas_call(
        kern,
        grid=(b, c // ct, n_tiles),
        in_specs=[
            pl.BlockSpec((None, HALO, ct), lambda bi, j, i: (bi, jnp.maximum(i * hb - 1, 0), j0 + j)),
            pl.BlockSpec((None, CONV_TILE, ct), lambda bi, j, i: (bi, i, j0 + j)),
            pl.BlockSpec((None, HALO, ct), lambda bi, j, i: (bi, jnp.minimum((i + 1) * hb, n_halo - 1), j0 + j)),
            pl.BlockSpec((taps, ct), lambda bi, j, i: (0, j)),
            pl.BlockSpec((1, ct), lambda bi, j, i: (0, j)),
        ],
        out_specs=out_specs,
        out_shape=out_shape,
        compiler_params=pltpu.CompilerParams(dimension_semantics=("arbitrary",) * 3, vmem_limit_bytes=VMEM_LIMIT),
        name="dwconv",
    )(x, x, x, w.astype(F32), bias.reshape(1, c).astype(F32))


SSD_STEP_CHUNKS = 2
SSM_GHEADS = SSM_HEADS // SSM_GROUPS
SSM_GP = SSM_GHEADS * SSM_HEAD_DIM


def _ssd_kernel(xbc_ref, dt_ref, dtb_ref, a_ref, *rest, reverse):
    if reverse:
        yf_ref, z_ref, dsk_ref, nw_ref, o_ref, st_ref, m_ref, xd_ref, xst_ref, y_ref = rest
    else:
        o_ref, st_ref, m_ref, xd_ref, xst_ref, y_ref = rest
    Q = SSD_CHUNK

    @pl.when(pl.program_id(1) == 0)
    def _():
        st_ref[...] = jnp.zeros_like(st_ref)

    row = lax.broadcasted_iota(jnp.int32, (Q, Q), 0)
    col = lax.broadcasted_iota(jnp.int32, (Q, Q), 1)
    keep = (col >= row) if reverse else (col <= row)
    tri = jnp.where(keep, 1.0, 0.0).astype(BF16)
    expand = jnp.where(lax.broadcasted_iota(jnp.int32, (LANES, SSM_INNER), 1) // SSM_HEAD_DIM == lax.broadcasted_iota(jnp.int32, (LANES, SSM_INNER), 0), 1.0, 0.0).astype(BF16)

    rev_refs = (yf_ref, z_ref, dsk_ref, nw_ref) if reverse else None
    for sub in (range(SSD_STEP_CHUNKS - 1, -1, -1) if reverse else range(SSD_STEP_CHUNKS)):
        _ssd_chunk(sub, xbc_ref, dt_ref, dtb_ref[...], a_ref[...], rev_refs, o_ref, st_ref, m_ref, xd_ref, xst_ref, y_ref, keep, tri, expand, reverse)


def _ssd_chunk(sub, xbc_ref, dt_ref, dt_bias, a_neg, rev_refs, o_ref, st_ref, m_ref, xd_ref, xst_ref, y_ref, keep, tri, expand, reverse):
    Q = SSD_CHUNK
    rs = slice(sub * Q, (sub + 1) * Q)
    dt = jax.nn.softplus(dt_ref[rs, :] + dt_bias)
    a = dt * a_neg
    cs = sum(jnp.dot(tri, p, preferred_element_type=F32) for p in _split3(a))
    tot = 0 if reverse else Q - 1
    cs_tot = cs[tot:tot + 1, :]
    cs_t = cs.T
    dt_e = jnp.dot(dt.astype(BF16), expand, preferred_element_type=F32)
    e_in = jnp.dot(jnp.exp(cs).astype(BF16), expand, preferred_element_type=F32)
    e_st = jnp.dot(jnp.exp(cs_tot - cs).astype(BF16), expand, preferred_element_type=F32)
    e_tot = jnp.dot(jnp.broadcast_to(jnp.exp(cs_tot), (SUBLANES, LANES)).astype(BF16), expand, preferred_element_type=F32)[0:1, :]

    xs = xbc_ref[rs, :SSM_INNER].astype(F32)
    xd_ref[sub] = (xs * dt_e).astype(BF16)
    xst_ref[sub] = (xs * dt_e * e_st).astype(BF16)
    for g in range(SSM_GROUPS):
        bm = xbc_ref[rs, SSM_INNER + g * SSM_STATE:SSM_INNER + (g + 1) * SSM_STATE]
        cm = xbc_ref[rs, SSM_INNER + (SSM_GROUPS + g) * SSM_STATE:SSM_INNER + (SSM_GROUPS + g + 1) * SSM_STATE]
        cb = _dot_nt(cm, bm)
        for h in range(g * SSM_GHEADS, (g + 1) * SSM_GHEADS):
            diff = jnp.broadcast_to(cs[:, h:h + 1], (Q, Q)) - jnp.broadcast_to(cs_t[h:h + 1, :], (Q, Q))
            m_ref[sub, h] = (cb * jnp.exp(jnp.where(keep, diff, NEG_BIG))).astype(BF16)

    lane_lo = lax.broadcasted_iota(jnp.int32, (Q, LANES), 1) < SSM_HEAD_DIM
    for g in range(SSM_GROUPS):
        bm = xbc_ref[rs, SSM_INNER + g * SSM_STATE:SSM_INNER + (g + 1) * SSM_STATE]
        cm = xbc_ref[rs, SSM_INNER + (SSM_GROUPS + g) * SSM_STATE:SSM_INNER + (SSM_GROUPS + g + 1) * SSM_STATE]
        gl = slice(g * SSM_GP, (g + 1) * SSM_GP)
        st = st_ref[g]
        y_off = jnp.dot(cm, st.astype(BF16), preferred_element_type=F32) * e_in[:, gl]
        for hp in range(SSM_GHEADS // 2):
            h0 = g * SSM_GHEADS + 2 * hp
            lanes = slice((h0 // 2) * LANES, (h0 // 2 + 1) * LANES)
            pair = [jnp.dot(m_ref[sub, h], xd_ref[sub, :, lanes], preferred_element_type=F32) for h in (h0, h0 + 1)]
            y_ref[sub, :, lanes] = jnp.where(lane_lo, pair[0], pair[1]) + y_off[:, hp * LANES:(hp + 1) * LANES]
        st_ref[g] = st * e_tot[:, gl] + jnp.dot(bm.astype(F32).T.astype(BF16), xst_ref[sub, :, gl], preferred_element_type=F32)
    y = y_ref[sub]
    if reverse:
        yf_ref, z_ref, dsk_ref, nw_ref = rev_refs
        y = (y + yf_ref[rs, :] + xbc_ref[rs, :SSM_INNER].astype(F32) * dsk_ref[...])
        zz = z_ref[rs, :].astype(F32)
        y = y * (zz * jax.nn.sigmoid(zz))
        y = y * lax.rsqrt(jnp.mean(y * y, axis=-1, keepdims=True) + EPS) * nw_ref[...]
    o_ref[rs, :] = y.astype(o_ref.dtype)


def _ssd_scan(xbc_act, dt2, p_rm, z_col, dt_bias, a_log, d_skip, norm_w, n_ctx):
    b, t, _ = xbc_act.shape
    step_rows = SSD_STEP_CHUNKS * SSD_CHUNK
    assert n_ctx % step_rows == 0 and t % step_rows == 0, (n_ctx, t, step_rows)
    nc, ncc = t // step_rows, n_ctx // step_rows

    def pad_heads(v):
        return jnp.pad(v.astype(F32), ((0, 0), (0, LANES - SSM_HEADS)))

    dtb = pad_heads(dt_bias.reshape(2, SSM_HEADS))
    a_neg = pad_heads(-jnp.exp(a_log.astype(F32)))
    dsk = jnp.repeat(d_skip.astype(F32), SSM_HEAD_DIM).reshape(1, SSM_INNER)
    scratch = [pltpu.VMEM((SSM_GROUPS, SSM_STATE, SSM_GP), F32), pltpu.VMEM((SSD_STEP_CHUNKS, SSM_HEADS, SSD_CHUNK, SSD_CHUNK), BF16),
               pltpu.VMEM((SSD_STEP_CHUNKS, SSD_CHUNK, SSM_INNER), BF16), pltpu.VMEM((SSD_STEP_CHUNKS, SSD_CHUNK, SSM_INNER), BF16),
               pltpu.VMEM((SSD_STEP_CHUNKS, SSD_CHUNK, SSM_INNER), F32)]
    params = pltpu.CompilerParams(dimension_semantics=("arbitrary", "arbitrary"), vmem_limit_bytes=VMEM_LIMIT)

    def bwd_chunk(s):
        return jnp.where(s < ncc, ncc - 1 - s, nc + ncc - 1 - s)

    def specs(chunk, d):
        return [
            pl.BlockSpec((None, step_rows, SSM_XBC), lambda bi, s: (bi, chunk(s), 0)),
            pl.BlockSpec((None, step_rows, LANES), lambda bi, s: (bi, chunk(s), d)),
            pl.BlockSpec((1, LANES), lambda bi, s: (0, 0)),
            pl.BlockSpec((1, LANES), lambda bi, s: (0, 0)),
        ]

    def inner_spec(chunk):
        return pl.BlockSpec((None, step_rows, SSM_INNER), lambda bi, s: (bi, chunk(s), 0))

    row_spec = pl.BlockSpec((1, SSM_INNER), lambda bi, s: (0, 0))
    z_spec = pl.BlockSpec((None, step_rows, SSM_INNER), lambda bi, s: (bi, bwd_chunk(s), z_col // SSM_INNER))
    y_f = pl.pallas_call(
        functools.partial(_ssd_kernel, reverse=False),
        grid=(b, nc),
        in_specs=specs(lambda s: s, 0),
        out_specs=inner_spec(lambda s: s),
        out_shape=jax.ShapeDtypeStruct((b, t, SSM_INNER), F32),
        scratch_shapes=scratch,
        compiler_params=params,
        name="ssd_fwd",
    )(xbc_act, dt2, dtb[0:1], a_neg[0:1])
    return pl.pallas_call(
        functools.partial(_ssd_kernel, reverse=True),
        grid=(b, nc),
        in_specs=specs(bwd_chunk, 1) + [inner_spec(bwd_chunk), z_spec, row_spec, row_spec],
        out_specs=inner_spec(bwd_chunk),
        out_shape=jax.ShapeDtypeStruct((b, t, SSM_INNER), BF16),
        scratch_shapes=scratch,
        compiler_params=params,
        name="ssd_bwd",
    )(xbc_act, dt2, dtb[1:2], a_neg[1:2], y_f, p_rm, dsk, norm_w.reshape(1, SSM_INNER).astype(F32))


HY_N2 = LANES
HY_CT = LANES
VMEM_LIMIT_HYENA = 60 * 1024 * 1024


def _hy_dims(L):
    n1 = 2 * L // HY_N2
    k1n = n1 // 2 + 1
    k1p = -(-k1n // SUBLANES) * SUBLANES
    return n1, k1n, k1p


def _hy_tables(L, n1_rows):
    n1, k1n, k1p = _hy_dims(L)
    n = 2 * L
    k1 = np.arange(k1n, dtype=np.float64)[None, :, None]
    nn = (HY_N2 * np.arange(n1_rows, dtype=np.float64)[None, None, :] + np.arange(HY_N2, dtype=np.float64)[:, None, None])
    ang = 2.0 * np.pi * ((k1 * nn) % n) / n
    m1 = np.zeros((HY_N2, 2 * k1p, n1_rows), np.float32)
    m1[:, :k1n] = np.cos(ang)
    m1[:, k1p:k1p + k1n] = -np.sin(ang)
    m4 = np.transpose(m1, (0, 2, 1))
    kk = np.arange(HY_N2, dtype=np.float64)
    a2 = 2.0 * np.pi * ((kk[:, None] * kk[None, :]) % HY_N2) / HY_N2
    c, s = np.cos(a2), np.sin(a2)
    f3 = np.block([[c, s], [-s, c]]).astype(np.float32)
    f3i = np.block([[c, -s], [s, c]]).astype(np.float32)
    return jnp.asarray(m1, BF16), jnp.asarray(m4, BF16), jnp.asarray(f3, BF16), jnp.asarray(f3i, BF16)


def _hy_stage1(u_refs, a_refs, m1_ref, n1_rows, k1p):
    def body(n2, carry):
        xs = jnp.concatenate([u_ref[pl.ds(n2, n1_rows, stride=HY_N2), :] for u_ref in u_refs], axis=1).astype(BF16)
        a = jnp.dot(m1_ref[n2], xs, preferred_element_type=F32)
        for i, a_ref in enumerate(a_refs):
            a_ref[pl.ds(pl.multiple_of(n2 * 2 * k1p, 2 * k1p), 2 * k1p), :] = a[:, i * HY_CT:(i + 1) * HY_CT]
        return carry

    lax.fori_loop(0, HY_N2, body, 0, unroll=8)


def _hy_spectrum_slab(a_refs, f3_ref, k1s, k1p):
    blk = jnp.concatenate([jnp.concatenate([a_ref[pl.ds(k1, HY_N2, stride=2 * k1p), :], a_ref[pl.ds(k1p + k1, HY_N2, stride=2 * k1p), :]], axis=0) for k1 in k1s for a_ref in a_refs], axis=1)
    return jnp.dot(f3_ref[...], blk.astype(BF16), preferred_element_type=F32)


def _hy_k1_loop(body, k1n):
    def pair(p, carry):
        body((2 * p, 2 * p + 1))
        return carry

    lax.fori_loop(0, (k1n - 1) // 2, pair, 0, unroll=math.gcd((k1n - 1) // 2, 4))
    body((k1n - 1,))


def _hy_slab_rows(k1):
    start = k1 * 2 * HY_N2
    return pl.ds(start if isinstance(k1, int) else pl.multiple_of(start, 2 * HY_N2), 2 * HY_N2)


def _hy_conv(u_ref, yo_ref, a_ref, y_ref, h_ref, order, m1_ref, m4_ref, f3_ref, f3i_ref, n1_rows, k1n, k1p):
    _hy_stage1((u_ref,), (a_ref,), m1_ref, n1_rows, k1p)

    def stage2(k1s):
        x = _hy_spectrum_slab((a_ref,), f3_ref, k1s, k1p)
        h = jnp.concatenate([h_ref[order, _hy_slab_rows(k1), :] for k1 in k1s], axis=1).astype(F32)
        xr, xi, hr, hi = x[:HY_N2], x[HY_N2:], h[:HY_N2], h[HY_N2:]
        z = jnp.concatenate([xr * hr - xi * hi, xr * hi + xi * hr], axis=0).astype(BF16)
        c = jnp.dot(f3i_ref[...], z, preferred_element_type=F32)
        for i, k1 in enumerate(k1s):
            lanes = slice(i * HY_CT, (i + 1) * HY_CT)
            y_ref[pl.ds(k1, HY_N2, stride=2 * k1p), :] = c[:HY_N2, lanes]
            y_ref[pl.ds(k1p + k1, HY_N2, stride=2 * k1p), :] = c[HY_N2:, lanes]

    _hy_k1_loop(stage2, k1n)

    def stage3(n2, carry):
        d = y_ref[pl.ds(pl.multiple_of(n2 * 2 * k1p, 2 * k1p), 2 * k1p), :].astype(BF16)
        yo_ref[pl.ds(n2, n1_rows, stride=HY_N2), :] = jnp.dot(m4_ref[n2], d, preferred_element_type=F32)
        return carry

    lax.fori_loop(0, HY_N2, stage3, 0, unroll=8)


def _hyena_kernel(v_ref, x1_ref, x2_ref, h_ref, m1_ref, m4_ref, f3_ref, f3i_ref, bias_ref, o_ref, a_ref, y_ref, u_ref, yo_ref, *, n1_rows, k1n, k1p):
    @pl.when(jnp.logical_and(pl.program_id(0) == 0, pl.program_id(1) == 0))
    def _():
        y_ref[...] = jnp.zeros_like(y_ref)

    u_ref[...] = v_ref[...].astype(F32)
    for order, gate_ref in enumerate((x1_ref, x2_ref)):
        _hy_conv(u_ref, yo_ref, a_ref, y_ref, h_ref, order, m1_ref, m4_ref, f3_ref, f3i_ref, n1_rows, k1n, k1p)
        z = gate_ref[...].astype(F32) * (yo_ref[...] + u_ref[...] * bias_ref[order:order + 1, :])
        if order == 0:
            u_ref[...] = z
        else:
            o_ref[...] = z.astype(o_ref.dtype)


def _hy_filter_kernel(hid_ref, wf_ref, wb_ref, bf_ref, bb_ref, dl_ref, m1_ref, f3_ref, o_ref, af_ref, ab_ref, uf_ref, ub_ref, *, n1_rows, k1n, k1p, scale_mid, scale_edge):
    L = uf_ref.shape[0]
    hid_rows = math.gcd(L, 2 * HY_N2)

    def fill(w_ref, b_ref, dst_ref, drop_first):
        w = _split3(w_ref[...])

        def body(i, energy):
            rows = pl.ds(pl.multiple_of(i * hid_rows, hid_rows), hid_rows)
            hid = _split3(hid_ref[rows, :])
            acc = sum(jnp.dot(hid[p], w[q], preferred_element_type=F32) for p in range(2) for q in range(2 - p))
            pos = i * hid_rows + lax.broadcasted_iota(jnp.int32, (hid_rows, HY_CT), 0)
            vals = (acc + b_ref[...]) * jnp.exp(pos.astype(F32) * (-1.0 / (L - 1)) * dl_ref[...])
            if drop_first:
                vals = jnp.where(pos == 0, 0.0, vals)
            dst_ref[rows, :] = vals
            return energy + jnp.sum(vals * vals, axis=0, keepdims=True)

        return lax.fori_loop(0, L // hid_rows, body, jnp.zeros((1, HY_CT), F32), unroll=math.gcd(L // hid_rows, 4))

    norm = lax.rsqrt(fill(wf_ref, bf_ref, uf_ref, False) + fill(wb_ref, bb_ref, ub_ref, True) + EPS)

    _hy_stage1((uf_ref, ub_ref), (af_ref, ab_ref), m1_ref, n1_rows, k1p)

    def combine(k1s):
        x_all = _hy_spectrum_slab((af_ref, ab_ref), f3_ref, k1s, k1p)
        for i, k1 in enumerate(k1s):
            xf = x_all[:, 2 * i * HY_CT:(2 * i + 1) * HY_CT]
            xb = x_all[:, (2 * i + 1) * HY_CT:(2 * i + 2) * HY_CT]
            w = norm * jnp.where(jnp.logical_or(k1 == 0, k1 == k1n - 1), scale_edge, scale_mid)
            h = jnp.concatenate([xf[:HY_N2] + xb[:HY_N2], xf[HY_N2:] - xb[HY_N2:]], axis=0)
            o_ref[_hy_slab_rows(k1), :] = (h * w).astype(o_ref.dtype)

    _hy_k1_loop(combine, k1n)


def _single(block_shape, index_map):
    return pl.BlockSpec(block_shape, index_map, pipeline_mode=pl.Buffered(1))


def _hyena_filter_spectrum_pallas(hidden, w3, b3, deltas):
    L, ffn = hidden.shape
    c = deltas.shape[0]
    nct = c // HY_CT
    n = 2 * L
    n1, k1n, k1p = _hy_dims(L)
    n1_rows = L // HY_N2
    m1, _, f3, _ = _hy_tables(L, n1_rows)
    kern = functools.partial(_hy_filter_kernel, n1_rows=n1_rows, k1n=k1n, k1p=k1p, scale_mid=2.0 / n, scale_edge=1.0 / n)
    return pl.pallas_call(
        kern,
        grid=(HY_ORDER, nct),
        in_specs=[
            _single((L, ffn), lambda o, j: (0, 0)),
            pl.BlockSpec((ffn, HY_CT), lambda o, j: (0, o * nct + j)),
            pl.BlockSpec((ffn, HY_CT), lambda o, j: (0, (HY_ORDER + o) * nct + j)),
            pl.BlockSpec((1, HY_CT), lambda o, j: (0, o * nct + j)),
            pl.BlockSpec((1, HY_CT), lambda o, j: (0, (HY_ORDER + o) * nct + j)),
            pl.BlockSpec((1, HY_CT), lambda o, j: (0, j)),
            _single((HY_N2, 2 * k1p, n1_rows), lambda o, j: (0, 0, 0)),
            _single((2 * HY_N2, 2 * HY_N2), lambda o, j: (0, 0)),
        ],
        out_specs=pl.BlockSpec((None, k1n * 2 * HY_N2, HY_CT), lambda o, j: (o, 0, j)),
        out_shape=jax.ShapeDtypeStruct((HY_ORDER, k1n * 2 * HY_N2, c), BF16),
        scratch_shapes=[
            pltpu.VMEM((k1p * 2 * HY_N2, HY_CT), F32),
            pltpu.VMEM((k1p * 2 * HY_N2, HY_CT), F32),
            pltpu.VMEM((L, HY_CT), F32),
            pltpu.VMEM((L, HY_CT), F32),
        ],
        compiler_params=pltpu.CompilerParams(dimension_semantics=("arbitrary", "arbitrary"), vmem_limit_bytes=VMEM_LIMIT_HYENA),
        name="hyena_filter_dft",
    )(hidden, w3, w3, b3.reshape(1, -1), b3.reshape(1, -1), deltas.reshape(1, c), m1, f3)


def _hyena_long(hy, h_spec, bias):
    b, L, c3 = hy.shape
    c = c3 // (HY_ORDER + 1)
    nct = c // HY_CT
    n1, k1n, k1p = _hy_dims(L)
    n1_rows = L // HY_N2
    m1, m4, f3, f3i = _hy_tables(L, n1_rows)
    kern = functools.partial(_hyena_kernel, n1_rows=n1_rows, k1n=k1n, k1p=k1p)

    def col(part):
        return _single((None, L, HY_CT), lambda j, bi: (bi, 0, part * nct + j))

    return pl.pallas_call(
        kern,
        grid=(nct, b),
        in_specs=[
            col(0), col(1), col(2),
            _single((HY_ORDER, k1n * 2 * HY_N2, HY_CT), lambda j, bi: (0, 0, j)),
            _single((HY_N2, 2 * k1p, n1_rows), lambda j, bi: (0, 0, 0)),
            _single((HY_N2, n1_rows, 2 * k1p), lambda j, bi: (0, 0, 0)),
            _single((2 * HY_N2, 2 * HY_N2), lambda j, bi: (0, 0)),
            _single((2 * HY_N2, 2 * HY_N2), lambda j, bi: (0, 0)),
            pl.BlockSpec((HY_ORDER, HY_CT), lambda j, bi: (0, j)),
        ],
        out_specs=pl.BlockSpec((None, L, HY_CT), lambda j, bi: (bi, 0, j)),
        out_shape=jax.ShapeDtypeStruct((b, L, c), BF16),
        scratch_shapes=[
            pltpu.VMEM((k1p * 2 * HY_N2, HY_CT), F32),
            pltpu.VMEM((HY_N2 * 2 * k1p, HY_CT), F32),
            pltpu.VMEM((L, HY_CT), F32),
            pltpu.VMEM((L, HY_CT), F32),
        ],
        compiler_params=pltpu.CompilerParams(dimension_semantics=("arbitrary", "arbitrary"), vmem_limit_bytes=VMEM_LIMIT_HYENA),
        name="hyena_long_conv",
    )(hy, hy, hy, h_spec, m1, m4, f3, f3i, bias.astype(F32))


HY_CTX_CT = 256


def _hy_ctx_tables(L):
    n = 2 * L
    kb = L + 1
    kp = -(-kb // LANES) * LANES
    ang = 2.0 * np.pi * ((np.arange(kb, dtype=np.float64)[:, None] * np.arange(L, dtype=np.float64)[None, :]) % n) / n
    fwd = np.zeros((2 * kp, L), np.float32)
    fwd[:kb] = np.cos(ang)
    fwd[kp:kp + kb] = -np.sin(ang)
    return jnp.asarray(fwd, BF16), jnp.asarray(fwd.T, BF16), kb, kp


def _hy_ctx_filter_kernel(fwd_ref, bwd_ref, f_ref, o_ref, *, kb, kp, n):
    fwd = fwd_ref[...]
    bwd = jnp.where(lax.broadcasted_iota(jnp.int32, fwd.shape, 0) == 0, 0.0, bwd_ref[...])
    norm = lax.rsqrt(jnp.sum(fwd * fwd, axis=0, keepdims=True) + jnp.sum(bwd * bwd, axis=0, keepdims=True) + EPS)
    hf = jnp.dot(f_ref[...], fwd.astype(BF16), preferred_element_type=F32)
    hb = jnp.dot(f_ref[...], bwd.astype(BF16), preferred_element_type=F32)
    row = lax.broadcasted_iota(jnp.int32, hf.shape, 0)
    imag = row >= kp
    k = jnp.where(imag, row - kp, row)
    wk = jnp.where(jnp.logical_or(k == 0, k == kb - 1), 1.0 / n, 2.0 / n)
    o_ref[...] = (hf + jnp.where(imag, -hb, hb)) * (wk * norm)


def _hy_ctx_kernel(v_ref, x1_ref, x2_ref, h_ref, f_ref, g_ref, bias_ref, o_ref, *, kp):
    u = v_ref[...].astype(F32)
    for order, gate_ref in enumerate((x1_ref, x2_ref)):
        x = jnp.dot(f_ref[...], u.astype(BF16), preferred_element_type=F32)
        h = h_ref[order]
        xr, xi, hr, hi = x[:kp], x[kp:], h[:kp], h[kp:]
        z = jnp.concatenate([xr * hr - xi * hi, xr * hi + xi * hr], axis=0).astype(BF16)
        y = jnp.dot(g_ref[...], z, preferred_element_type=F32)
        u = gate_ref[...].astype(F32) * (y + u * bias_ref[order:order + 1, :])
    o_ref[...] = u.astype(o_ref.dtype)


def _hyena_ctx(hy, taps, bias):
    b, L, _ = hy.shape
    c = hy.shape[2] // (HY_ORDER + 1)
    ct = HY_CTX_CT
    nct = c // ct
    f_mat, g_mat, kb, kp = _hy_ctx_tables(L)
    params = pltpu.CompilerParams(dimension_semantics=("arbitrary", "arbitrary"), vmem_limit_bytes=VMEM_LIMIT)
    h_spec = pl.pallas_call(
        functools.partial(_hy_ctx_filter_kernel, kb=kb, kp=kp, n=2 * L),
        grid=(HY_ORDER, nct),
        in_specs=[
            pl.BlockSpec((L, ct), lambda o, j: (0, o * nct + j)),
            pl.BlockSpec((L, ct), lambda o, j: (0, (HY_ORDER + o) * nct + j)),
            pl.BlockSpec((2 * kp, L), lambda o, j: (0, 0)),
        ],
        out_specs=pl.BlockSpec((None, 2 * kp, ct), lambda o, j: (o, 0, j)),
        out_shape=jax.ShapeDtypeStruct((HY_ORDER, 2 * kp, c), F32),
        compiler_params=params,
        name="hyena_ctx_filter",
    )(taps, taps, f_mat)

    def col(part):
        return pl.BlockSpec((None, L, ct), lambda j, bi: (bi, 0, part * nct + j))

    return pl.pallas_call(
        functools.partial(_hy_ctx_kernel, kp=kp),
        grid=(nct, b),
        in_specs=[
            col(0), col(1), col(2),
            pl.BlockSpec((HY_ORDER, 2 * kp, ct), lambda j, bi: (0, 0, j)),
            pl.BlockSpec((2 * kp, L), lambda j, bi: (0, 0)),
            pl.BlockSpec((L, 2 * kp), lambda j, bi: (0, 0)),
            pl.BlockSpec((HY_ORDER, ct), lambda j, bi: (0, j)),
        ],
        out_specs=pl.BlockSpec((None, L, ct), lambda j, bi: (bi, 0, j)),
        out_shape=jax.ShapeDtypeStruct((b, L, c), BF16),
        compiler_params=params,
        name="hyena_ctx_conv",
    )(hy, hy, hy, h_spec, f_mat, g_mat, bias.astype(F32))


def _split_cols(t, sizes):
    return jnp.split(t, np.cumsum(sizes)[:-1].tolist(), axis=-1)


def _to_col_major(t, rows):
    b, rest = t.shape[0], t.shape[2:]
    return jnp.swapaxes(t.reshape((b, rows, GRID_W) + rest), 1, 2).reshape((b, rows * GRID_W) + rest)


def _from_col_major(t, rows):
    b, rest = t.shape[0], t.shape[2:]
    return jnp.swapaxes(t.reshape((b, GRID_W, rows) + rest), 1, 2).reshape((b, rows * GRID_W) + rest)


def _hyena_filter_hidden(L, w1, b1, w2, b2, freq):
    hp = lax.Precision.HIGHEST
    t = jnp.linspace(0.0, 1.0, L, dtype=F32)[:, None]
    w = 2.0 * math.pi * jnp.arange(L, dtype=F32)[:, None] / L
    bands = jnp.linspace(1e-4, HY_BANDS - 1, HY_BANDS, dtype=F32)
    feats = jnp.concatenate([t, jnp.cos(bands * w), -jnp.sin(bands * w)], axis=-1)
    h = jnp.sin(freq[0] * (jnp.dot(feats, w1, precision=hp) + b1))
    return jnp.sin(freq[1] * (jnp.dot(h, w2, precision=hp) + b2))


def _hyena_decay_rates():
    max_decay = math.log(HY_DECAY_TARGET) / HY_FAST_DECAY
    min_decay = math.log(HY_DECAY_TARGET) / HY_SLOW_DECAY
    return jnp.abs(jnp.linspace(min_decay, max_decay, HY_WIDTH, dtype=F32))


def _hyena_filter_taps(L, w1, b1, w2, b2, w3, b3, freq):
    h = jnp.dot(_hyena_filter_hidden(L, w1, b1, w2, b2, freq), w3, precision=lax.Precision.HIGHEST) + b3
    t = jnp.linspace(0.0, 1.0, L, dtype=F32)[:, None]
    return h * jnp.tile(jnp.exp(-t * _hyena_decay_rates()), (1, 2 * HY_ORDER))


def _mixer_branches(h_rm, w_in, lb, n_ctx, ssm_conv_w, ssm_conv_b, ssm_dt_bias, ssm_a_log, ssm_d, ssm_norm, hy_conv_w, hy_conv_b, hy_w1, hy_b1, hy_w2, hy_b2, hy_w3, hy_b3, hy_freq, hy_bias, hg_norm):
    b, t, d = h_rm.shape
    n_lat = t - n_ctx
    rows = n_lat // GRID_W
    w_z, w_xbc, w_dt, w_hy, w_q, w_f, w_i, w_g, w_gate = _split_cols(w_in, IN_SIZES)
    h2 = h_rm.reshape(b * t, d)
    rm_parts = (w_z, w_xbc, w_hy, w_g, w_gate)
    col_z, col_xbc, col_hy, col_g, col_gate = np.cumsum([0] + [w.shape[1] for w in rm_parts[:-1]]).tolist()
    p_rm = _mm(h2, jnp.concatenate(rm_parts, axis=1).astype(BF16), BF16).reshape(b, t, -1)

    zero_pad = jnp.zeros((d, LANES - SSM_HEADS), F32)
    w_dt2 = jnp.concatenate([w_dt[:, :SSM_HEADS], zero_pad, w_dt[:, SSM_HEADS:], zero_pad], axis=1)
    dt2 = _mm(h2, w_dt2.astype(BF16), F32).reshape(b, t, 2 * LANES)
    xbc_act = _dwconv_stream(p_rm, ssm_conv_w, ssm_conv_b, n_ctx, True, c0=col_xbc)
    ym = _ssd_scan(xbc_act, dt2, p_rm, col_z, ssm_dt_bias, ssm_a_log, ssm_d, ssm_norm, n_ctx)

    hy_ctx, hy_lat = _dwconv_stream(p_rm, hy_conv_w, hy_conv_b, n_ctx, False, c0=col_hy, split=True)
    taps_ctx = _hyena_filter_taps(n_ctx, hy_w1, hy_b1, hy_w2, hy_b2, hy_w3, hy_b3, hy_freq)
    yh_ctx = _hyena_ctx(hy_ctx, taps_ctx, hy_bias)
    h_spec = _hyena_filter_spectrum_pallas(_hyena_filter_hidden(n_lat, hy_w1, hy_b1, hy_w2, hy_b2, hy_freq), hy_w3, hy_b3, _hyena_decay_rates())
    yh_lat = _hyena_long(hy_lat, h_spec, hy_bias)

    h_cm = jnp.concatenate([h_rm[:, :n_ctx], _to_col_major(h_rm[:, n_ctx:], rows)], axis=1).reshape(b * t, d)

    p_cm = _mm(h_cm, jnp.concatenate([w_q, w_f, w_i], axis=1).astype(BF16), BF16).reshape(b, t, -1)
    og = _hgrn_scan(p_cm, lb, hg_norm, n_ctx)
    return ym, (yh_ctx, yh_lat), (og, _from_col_major(og[:, n_ctx:], rows)), p_rm, col_g, col_gate


ROW_TILE = 256
MOD_ROWS = SUBLANES
M_SHIFT_MIX, M_SCALE_MIX, M_GATE_MIX, M_SHIFT_FFN, M_SCALE_FFN, M_GATE_FFN = range(6)
ROW_PARAMS = pltpu.CompilerParams(dimension_semantics=("arbitrary", "arbitrary"), vmem_limit_bytes=VMEM_LIMIT)


def _rms(x):
    return x * lax.rsqrt(jnp.mean(x * x, axis=-1, keepdims=True) + EPS)


def _mrow(m_ref, r):
    return m_ref[r:r + 1, :]


def _row_spec(width):
    return pl.BlockSpec((None, ROW_TILE, width), lambda bi, i: (bi, i, 0))


def _vec_spec(width):
    return pl.BlockSpec((1, width), lambda bi, i: (0, 0))


def _mat_spec(k, n):
    return pl.BlockSpec((k, n), lambda bi, i: (0, 0))


def _mod_spec(n_ctx):
    return pl.BlockSpec((None, None, MOD_ROWS, D_MODEL), lambda bi, i: (bi, jnp.where(i < n_ctx // ROW_TILE, 0, 1), 0, 0))


def _norm_mod_kernel(x_ref, w_ref, m_ref, o_ref):
    y = _rms(x_ref[...]) * w_ref[...]
    o_ref[...] = (y * (1.0 + _mrow(m_ref, M_SCALE_MIX)) + _mrow(m_ref, M_SHIFT_MIX)).astype(o_ref.dtype)


def _norm_mod(xs, w, mods, n_ctx):
    b, t, d = xs.shape
    return pl.pallas_call(
        _norm_mod_kernel,
        grid=(b, t // ROW_TILE),
        in_specs=[_row_spec(d), _vec_spec(d), _mod_spec(n_ctx)],
        out_specs=_row_spec(d),
        out_shape=jax.ShapeDtypeStruct((b, t, d), BF16),
        compiler_params=ROW_PARAMS,
        name="norm_mod",
    )(xs, w.reshape(1, d), mods)


def _merge_kernel(ym_ref, yhc_ref, yhl_ref, ogc_ref, ogl_ref, g_ref, gm_ref, gh_ref, gg_ref, x_ref, m_ref, w1_ref, w2_ref, w3_ref, wo_ref, npost_ref, npre_ref, rw_ref, rb_ref, xo_ref, h_ref, lg_ref, *, n_ctx_tiles):
    is_ctx = pl.program_id(1) < n_ctx_tiles
    yh = jnp.where(is_ctx, yhc_ref[...], yhl_ref[...])
    og = jnp.where(is_ctx, ogc_ref[...], ogl_ref[...])
    def sig(ref):
        return jax.nn.sigmoid(ref[...].astype(F32))

    go = g_ref[...].astype(F32)
    yg = (og.astype(F32) * (go * jax.nn.sigmoid(go))).astype(BF16)
    merged = sig(gm_ref) * jnp.dot(ym_ref[...], w1_ref[...], preferred_element_type=F32)
    merged = merged + sig(gh_ref) * jnp.dot(yh, w2_ref[...], preferred_element_type=F32)
    merged = merged + sig(gg_ref) * jnp.dot(yg, w3_ref[...], preferred_element_type=F32)
    mix = jnp.dot(merged.astype(BF16), wo_ref[...], preferred_element_type=F32)
    x = x_ref[...] + _mrow(m_ref, M_GATE_MIX) * (_rms(mix) * npost_ref[...])
    xo_ref[...] = x
    h = (_rms(x) * npre_ref[...] * (1.0 + _mrow(m_ref, M_SCALE_FFN)) + _mrow(m_ref, M_SHIFT_FFN)).astype(BF16)
    h_ref[...] = h
    lg_ref[...] = jnp.dot(h, rw_ref[...], preferred_element_type=F32) + rb_ref[...]


def _merge(ym, yh_parts, og_parts, p_rm, col_g, col_gate, xs, mods, w_br_ssm, w_br_hy, w_br_hg, w_out, norm_post, norm_ffn_pre, router_w, router_b, n_ctx):
    b, t, d = xs.shape
    rw = jnp.pad(router_w, ((0, 0), (0, LANES - N_EXPERTS))).astype(BF16)
    rb = jnp.pad(router_b, (0, LANES - N_EXPERTS)).reshape(1, LANES).astype(F32)

    def col_spec(col):
        return pl.BlockSpec((None, ROW_TILE, d), lambda bi, i: (bi, i, col // d))

    nct = n_ctx // ROW_TILE
    ctx_spec = pl.BlockSpec((None, ROW_TILE, d), lambda bi, i: (bi, jnp.minimum(i, nct - 1), 0))
    lat_spec = pl.BlockSpec((None, ROW_TILE, d), lambda bi, i: (bi, jnp.maximum(i - nct, 0), 0))

    return pl.pallas_call(
        functools.partial(_merge_kernel, n_ctx_tiles=nct),
        grid=(b, t // ROW_TILE),
        in_specs=[_row_spec(d), ctx_spec, lat_spec, ctx_spec, lat_spec, col_spec(col_g), col_spec(col_gate), col_spec(col_gate + d), col_spec(col_gate + 2 * d), _row_spec(d), _mod_spec(n_ctx),
                  _mat_spec(d, d), _mat_spec(d, d), _mat_spec(d, d), _mat_spec(d, d), _vec_spec(d), _vec_spec(d), _mat_spec(d, LANES), _vec_spec(LANES)],
        out_specs=[_row_spec(d), _row_spec(d), _row_spec(LANES)],
        out_shape=[jax.ShapeDtypeStruct((b, t, d), F32), jax.ShapeDtypeStruct((b, t, d), BF16), jax.ShapeDtypeStruct((b, t, LANES), F32)],
        compiler_params=ROW_PARAMS,
        name="branch_merge",
    )(ym, yh_parts[0], yh_parts[1], og_parts[0], og_parts[1], p_rm, p_rm, p_rm, p_rm, xs, mods, w_br_ssm.astype(BF16), w_br_hy.astype(BF16), w_br_hg.astype(BF16), w_out.astype(BF16),
      norm_post.reshape(1, d), norm_ffn_pre.reshape(1, d), rw, rb)


def _post_ffn_kernel(y0_ref, y1_ref, y2_ref, y3_ref, x_ref, m_ref, w_ref, *rest):
    f = y0_ref[...].astype(F32) + y1_ref[...].astype(F32) + y2_ref[...].astype(F32) + y3_ref[...].astype(F32)
    x = x_ref[...] + _mrow(m_ref, M_GATE_FFN) * (_rms(f) * w_ref[...])
    if len(rest) == 1:
        rest[0][...] = x
    else:
        nm_ref, nw_ref, o_ref, h_ref = rest
        o_ref[...] = x
        h_ref[...] = (_rms(x) * nw_ref[...] * (1.0 + _mrow(nm_ref, M_SCALE_MIX)) + _mrow(nm_ref, M_SHIFT_MIX)).astype(h_ref.dtype)


def _post_ffn(f4, xs, mods, norm_post, n_ctx, skip, nxt=None):
    b, _, d = xs.shape
    t = f4.shape[2]
    i0 = skip // ROW_TILE

    def k_spec(k):
        return pl.BlockSpec((None, None, ROW_TILE, d), lambda bi, i: (k, bi, i, 0))

    mod_spec = pl.BlockSpec((None, None, MOD_ROWS, d), lambda bi, i: (bi, jnp.where(i0 + i < n_ctx // ROW_TILE, 0, 1), 0, 0))
    in_specs = [k_spec(k) for k in range(TOP_K)] + [pl.BlockSpec((None, ROW_TILE, d), lambda bi, i: (bi, i0 + i, 0)), mod_spec, _vec_spec(d)]
    args = [f4, f4, f4, f4, xs, mods, norm_post.reshape(1, d)]
    out_specs, out_shape = _row_spec(d), jax.ShapeDtypeStruct((b, t, d), F32)
    if nxt is not None:
        in_specs += [mod_spec, _vec_spec(d)]
        args += [nxt[0], nxt[1].reshape(1, d)]
        out_specs, out_shape = [out_specs, _row_spec(d)], [out_shape, jax.ShapeDtypeStruct((b, t, d), BF16)]
    return pl.pallas_call(
        _post_ffn_kernel,
        grid=(b, t // ROW_TILE),
        in_specs=in_specs,
        out_specs=out_specs,
        out_shape=out_shape,
        compiler_params=ROW_PARAMS,
        name="post_ffn",
    )(*args)


def _moe_ffn(h2, logits, t_per_b, skip, li, w1, b1, w2, b2):
    t = logits.shape[0]
    d = h2.shape[1]
    n = t * TOP_K
    n_tiles = n // MOE_BLOCK
    top_v, top_e = lax.top_k(logits, TOP_K)
    gate_w = jax.nn.softmax(top_v, axis=-1)
    flat_e = top_e.reshape(n).astype(jnp.int32)
    iota = jnp.arange(n, dtype=jnp.int32)
    _, order, sw = lax.sort((flat_e, iota, gate_w.reshape(n)), num_keys=1, is_stable=True)
    _, inv = lax.sort((order, iota), num_keys=1)
    tok = order // TOP_K
    xs = h2[tok + (tok // t_per_b + 1) * skip]
    counts = jnp.sum((flat_e[:, None] == jnp.arange(N_EXPERTS, dtype=jnp.int32)[None, :]).astype(jnp.int32), axis=0)
    end = jnp.cumsum(counts)
    start = end - counts
    first_tile = start // MOE_BLOCK
    n_items = jnp.where(counts > 0, (end - 1) // MOE_BLOCK - first_tile + 1, 0)
    items_end = jnp.cumsum(n_items)
    w = jnp.arange(n_tiles + N_EXPERTS, dtype=jnp.int32)
    valid = w < items_end[-1]
    e_w = jnp.minimum(jnp.sum((w[:, None] >= items_end[None, :]).astype(jnp.int32), axis=1), N_EXPERTS - 1)
    tile_w = first_tile[e_w] + (w - (items_end[e_w] - n_items[e_w]))
    lo = jnp.where(valid, jnp.maximum(start[e_w], tile_w * MOE_BLOCK), 0)
    hi = jnp.where(valid, jnp.minimum(end[e_w], (tile_w + 1) * MOE_BLOCK), 0)
    tile_w = jnp.where(valid, tile_w, n_tiles - 1)
    first = jnp.concatenate([jnp.ones((1,), jnp.int32), (tile_w[1:] != tile_w[:-1]).astype(jnp.int32)])
    newexp = jnp.concatenate([jnp.ones((1,), jnp.int32), (e_w[1:] != e_w[:-1]).astype(jnp.int32)])
    ys = _moe_experts(xs, sw, tile_w.astype(jnp.int32), e_w, lo.astype(jnp.int32), hi.astype(jnp.int32), first, newexp, li, w1, b1, w2, b2)
    return ys[inv.reshape(t, TOP_K).T]


def kernel(x, c, ctx, c_ctx, w_mod, b_mod, norm_mix_pre, norm_mix_post, norm_ffn_pre, norm_ffn_post, w_in, ssm_conv_w, ssm_conv_b, ssm_dt_bias, ssm_a_log, ssm_d, ssm_norm, hy_conv_w, hy_conv_b, hy_w1, hy_b1, hy_w2, hy_b2, hy_w3, hy_b3, hy_freq, hy_bias, hg_lb_logits, hg_norm, w_br_ssm, w_br_hy, w_br_hg, w_out, router_w, router_b, exp_w1, exp_b1, exp_w2, exp_b2):
    hp = lax.Precision.HIGHEST
    b, n_lat, d = x.shape
    n_ctx = ctx.shape[1]
    lb = jax.nn.softmax(hg_lb_logits.astype(F32), axis=1)
    lb = jnp.cumsum(lb, axis=1) - lb[:, :1]
    silu_c = jax.nn.silu(c)
    silu_cc = jax.nn.silu(c_ctx)
    xs = jnp.concatenate([ctx, x], axis=1)
    all_mods = []
    for li in range(DEPTH):
        mx = (jnp.dot(silu_c, w_mod[li], precision=hp) + b_mod[li]).reshape(b, 1, 6, d)
        mc = jnp.broadcast_to((jnp.dot(silu_cc, w_mod[li], precision=hp) + b_mod[li]).reshape(1, 1, 6, d), (b, 1, 6, d))
        all_mods.append(jnp.pad(jnp.concatenate([mc, mx], axis=1), ((0, 0), (0, 0), (0, MOD_ROWS - 6), (0, 0))))
    h = _norm_mod(xs, norm_mix_pre[0], all_mods[0], n_ctx)
    for li in range(DEPTH):
        mods = all_mods[li]
        ym, yh_parts, og_parts, p_rm, col_g, col_gate = _mixer_branches(h, w_in[li], lb[:, li], n_ctx, ssm_conv_w[li], ssm_conv_b[li], ssm_dt_bias[li], ssm_a_log[li], ssm_d[li], ssm_norm[li], hy_conv_w[li], hy_conv_b[li], hy_w1[li], hy_b1[li], hy_w2[li], hy_b2[li], hy_w3[li], hy_b3[li], hy_freq[li], hy_bias[li], hg_norm[li])
        xs, h_ffn, logits = _merge(ym, yh_parts, og_parts, p_rm, col_g, col_gate, xs, mods, w_br_ssm[li], w_br_hy[li], w_br_hg[li], w_out[li], norm_mix_post[li], norm_ffn_pre[li], router_w[li], router_b[li], n_ctx)
        skip = n_ctx if li == DEPTH - 1 else 0
        t = xs.shape[1] - skip
        f4 = _moe_ffn(h_ffn.reshape(-1, d), logits[:, skip:, :N_EXPERTS].reshape(b * t, N_EXPERTS), t, skip, li, exp_w1, exp_b1[li], exp_w2, exp_b2[li])
        if li == DEPTH - 1:
            xs = _post_ffn(f4.reshape(TOP_K, b, t, d), xs, mods, norm_ffn_post[li], n_ctx, skip)
        else:
            xs, h = _post_ffn(f4.reshape(TOP_K, b, t, d), xs, mods, norm_ffn_post[li], n_ctx, skip, nxt=(all_mods[li + 1], norm_mix_pre[li + 1]))
    return xs
```

```python
import functools
import math

import jax
import jax.numpy as jnp
import numpy as np
from jax import lax
from jax.experimental import pallas as pl
from jax.experimental.pallas import tpu as pltpu

D_MODEL = 1024
DEPTH = 2
GRID_W = 64

SSM_HEADS = 16
SSM_HEAD_DIM = 64
SSM_INNER = SSM_HEADS * SSM_HEAD_DIM
SSM_STATE = 128
SSM_GROUPS = 4
SSD_CHUNK = 128
SSM_XBC = SSM_INNER + 2 * SSM_GROUPS * SSM_STATE

HY_WIDTH = D_MODEL
HY_ORDER = 2
HY_BANDS = 16
HY_FAST_DECAY = 0.3
HY_SLOW_DECAY = 1.5
HY_DECAY_TARGET = 1e-2

HG_HEADS = 8
HG_KDIM = 128
HG_VDIM = D_MODEL // HG_HEADS
HG_QK = HG_HEADS * HG_KDIM
HG_V = HG_HEADS * HG_VDIM
HG_CHUNK = 64
F_FLOOR = 1e-20

N_EXPERTS = 32
TOP_K = 4
D_FF = D_MODEL
SWIGLU_LIMIT = 7.0
SWIGLU_ALPHA = 1.702
MOE_BLOCK = 512

N_BRANCHES = 3
IN_SIZES = (SSM_INNER, SSM_XBC, 2 * SSM_HEADS, (HY_ORDER + 1) * HY_WIDTH, HG_QK, 2 * HG_QK, HG_V, HG_V, N_BRANCHES * D_MODEL)
EPS = 1e-6
F32 = jnp.float32
BF16 = jnp.bfloat16

LANES = 128
VMEM_LIMIT = 56 * 1024 * 1024


def _mm_kernel(a_ref, b_ref, o_ref):
    o_ref[...] = jnp.dot(a_ref[...], b_ref[...], preferred_element_type=F32).astype(o_ref.dtype)


def _mm(a, b, out_dtype=F32, tm=1024, tn=2048):
    m, k = a.shape
    n = b.shape[1]
    tm = math.gcd(m, tm)
    tn = math.gcd(n, tn)
    assert tm % SUBLANES == 0 and tn % LANES == 0, (m, n, tm, tn)
    return pl.pallas_call(
        _mm_kernel,
        grid=(n // tn, m // tm),
        in_specs=[pl.BlockSpec((tm, k), lambda j, i: (i, 0)), pl.BlockSpec((k, tn), lambda j, i: (0, j))],
        out_specs=pl.BlockSpec((tm, tn), lambda j, i: (i, j)),
        out_shape=jax.ShapeDtypeStruct((m, n), out_dtype),
        compiler_params=pltpu.CompilerParams(dimension_semantics=("arbitrary", "arbitrary"), vmem_limit_bytes=VMEM_LIMIT),
        name="dense_mm",
    )(a, b)


def _moe_kernel(tile_ref, exp_ref, lo_ref, hi_ref, first_ref, newexp_ref, x_ref, sw_ref, w1_ref, b1_ref, w2_ref, b2_ref, o_ref, w1b_ref, w2b_ref):
    del exp_ref
    w = pl.program_id(0)
    lo, hi = lo_ref[w], hi_ref[w]

    @pl.when(newexp_ref[w] == 1)
    def _():
        def cast_rows(i, carry):
            rows = pl.ds(pl.multiple_of(i * LANES, LANES), LANES)
            w1b_ref[rows, :] = w1_ref[rows, :].astype(BF16)
            w2b_ref[rows, :] = w2_ref[rows, :].astype(BF16)
            return carry

        lax.fori_loop(0, D_MODEL // LANES, cast_rows, 0)

    @pl.when(hi > lo)
    def _():
        hh = jnp.dot(x_ref[...], w1b_ref[...], preferred_element_type=F32) + b1_ref[...]
        g = jnp.minimum(hh[:, :D_FF], SWIGLU_LIMIT)
        u = jnp.clip(hh[:, D_FF:], -SWIGLU_LIMIT, SWIGLU_LIMIT)
        act = (u + 1.0) * g * jax.nn.sigmoid(SWIGLU_ALPHA * g)
        y = jnp.dot(act.astype(BF16), w2b_ref[...], preferred_element_type=F32) + b2_ref[...]
        y = (y * sw_ref[...]).astype(o_ref.dtype)
        rows = tile_ref[w] * MOE_BLOCK + lax.broadcasted_iota(jnp.int32, (MOE_BLOCK, 1), 0)
        mine = jnp.logical_and(rows >= lo, rows < hi)

        @pl.when(first_ref[w] == 1)
        def _():
            o_ref[...] = jnp.where(mine, y, jnp.zeros_like(y))

        @pl.when(first_ref[w] != 1)
        def _():
            o_ref[...] = jnp.where(mine, y, o_ref[...])


def _moe_experts(xs, sw, tile_w, exp_w, lo, hi, first, newexp, li, w1, b1, w2, b2):
    n, d = xs.shape
    grid_spec = pltpu.PrefetchScalarGridSpec(
        num_scalar_prefetch=6,
        grid=(tile_w.shape[0],),
        in_specs=[
            pl.BlockSpec((MOE_BLOCK, d), lambda w, tl, ex, lo_, hi_, fi, ne: (tl[w], 0)),
            pl.BlockSpec((MOE_BLOCK, 1), lambda w, tl, ex, lo_, hi_, fi, ne: (tl[w], 0)),
            pl.BlockSpec((None, None, d, 2 * D_FF), lambda w, tl, ex, lo_, hi_, fi, ne: (li, ex[w], 0, 0)),
            pl.BlockSpec((None, 1, 2 * D_FF), lambda w, tl, ex, lo_, hi_, fi, ne: (ex[w], 0, 0)),
            pl.BlockSpec((None, None, D_FF, d), lambda w, tl, ex, lo_, hi_, fi, ne: (li, ex[w], 0, 0)),
            pl.BlockSpec((None, 1, d), lambda w, tl, ex, lo_, hi_, fi, ne: (ex[w], 0, 0)),
        ],
        out_specs=pl.BlockSpec((MOE_BLOCK, d), lambda w, tl, ex, lo_, hi_, fi, ne: (tl[w], 0)),
        scratch_shapes=[pltpu.VMEM((d, 2 * D_FF), BF16), pltpu.VMEM((D_FF, d), BF16)],
    )
    return pl.pallas_call(
        _moe_kernel,
        grid_spec=grid_spec,
        out_shape=jax.ShapeDtypeStruct((n, d), BF16),
        compiler_params=pltpu.CompilerParams(dimension_semantics=("arbitrary",), vmem_limit_bytes=VMEM_LIMIT),
        name="moe_experts",
    )(tile_w, exp_w, lo, hi, first, newexp, xs, sw.reshape(n, 1), w1, b1.reshape(N_EXPERTS, 1, 2 * D_FF), w2, b2.reshape(N_EXPERTS, 1, d))


SUBLANES = 8
NEG_BIG = -1e30
HG_STEP_CHUNKS = 4
HIER_LEVELS = (64, 32, 16, 8, 4)
LOG2E = math.log2(math.e)


def _split3(x):
    h1 = x.astype(BF16)
    r1 = x - h1.astype(F32)
    h2 = r1.astype(BF16)
    h3 = (r1 - h2.astype(F32)).astype(BF16)
    return h1, h2, h3


def _dot_nt(a, b):
    return lax.dot_general(a, b, (((1,), (1,)), ((), ())), preferred_element_type=F32)


def _gla_kernel(q_ref, a_ref, v_ref, lb_ref, *rest, reverse):
    if reverse:
        of_ref, w_ref, o_ref, st_ref, at_ref = rest
    else:
        o_ref, st_ref, at_ref = rest
    Q = HG_CHUNK

    @pl.when(pl.program_id(1) == 0)
    def _():
        st_ref[...] = jnp.zeros_like(st_ref)

    row = lax.broadcasted_iota(jnp.int32, (Q, Q), 0)
    col = lax.broadcasted_iota(jnp.int32, (Q, Q), 1)
    tri = jnp.where((col >= row) if reverse else (col <= row), 1.0, 0.0).astype(BF16)
    same_block = {s: (row // s) == (col // s) for s in HIER_LEVELS[1:] + (2, 1)}
    rowk = lax.broadcasted_iota(jnp.int32, (Q, HG_KDIM), 0)
    q_rows = {s: ((rowk % s) < s // 2) if reverse else ((rowk % s) >= s // 2) for s in HIER_LEVELS + (2,)}
    of_w = (of_ref, w_ref) if reverse else (None, None)
    for sub in (range(HG_STEP_CHUNKS - 1, -1, -1) if reverse else range(HG_STEP_CHUNKS)):
        _gla_chunk(sub, q_ref, a_ref, v_ref, lb_ref[...], of_w, o_ref, st_ref, at_ref, tri, same_block, q_rows, reverse)


def _gla_chunk(sub, q_ref, a_ref, v_ref, lb, of_w, o_ref, st_ref, at_ref, tri, same_block, q_rows, reverse):
    Q = HG_CHUNK
    rs = slice(sub * Q, (sub + 1) * Q)
    of_ref, w_ref = of_w
    shp3 = (Q // SUBLANES, SUBLANES, HG_KDIM)
    sub3 = lax.broadcasted_iota(jnp.int32, shp3, 1)
    a = a_ref[rs, :].astype(F32)
    f_all = jnp.maximum(lb + (1.0 - lb) * jax.nn.sigmoid(a), F_FLOOR)
    kk = (1.0 - lb) * jax.nn.sigmoid(-a)
    g_all = sum(jnp.dot(tri, p, preferred_element_type=F32) for p in _split3(jnp.log(f_all) * LOG2E))
    q_all = q_ref[rs, :].astype(F32)
    q_all = q_all * jax.nn.sigmoid(q_all)
    v_all = v_ref[rs, :]
    tot = 0 if reverse else Q - 1

    def level_ref(g, s):
        half = s // 2
        m_off = half if reverse else half - 1
        if s >= 2 * SUBLANES:
            return jnp.concatenate([jnp.broadcast_to(g[b0 + m_off:b0 + m_off + 1, :], (s, HG_KDIM)) for b0 in range(0, Q, s)], axis=0)
        g3 = g.reshape(shp3)
        ref = jnp.broadcast_to(g3[:, m_off:m_off + 1, :], shp3)
        for b0 in range(s, SUBLANES, s):
            ref = jnp.where(sub3 >= b0, jnp.broadcast_to(g3[:, b0 + m_off:b0 + m_off + 1, :], shp3), ref)
        return ref.reshape(Q, HG_KDIM)

    for h in range(HG_HEADS):
        sl = slice(h * HG_KDIM, (h + 1) * HG_KDIM)
        g, qh, kh, fh = g_all[:, sl], q_all[:, sl], kk[:, sl], f_all[:, sl]
        attn = None
        for s in HIER_LEVELS:
            gref = level_ref(g, s)
            eq = jnp.exp2(jnp.where(q_rows[s], g - gref, NEG_BIG))
            ek = jnp.exp2(jnp.where(q_rows[s], NEG_BIG, gref - g))
            lvl = _dot_nt((qh * eq).astype(BF16), (kh * ek).astype(BF16))
            attn = lvl if attn is None else jnp.where(same_block[s], lvl, attn)
        lvl = _dot_nt(jnp.where(q_rows[2], qh * fh, 0.0).astype(BF16), jnp.where(q_rows[2], 0.0, kh).astype(BF16))
        attn = jnp.where(same_block[2], lvl, attn)
        attn = jnp.where(same_block[1], _dot_nt(qh.astype(BF16), kh.astype(BF16)), attn)
        at_ref[sub, h] = attn.astype(BF16)

    for h in range(HG_HEADS):
        sl = slice(h * HG_KDIM, (h + 1) * HG_KDIM)
        g, qh, kh, vb = g_all[:, sl], q_all[:, sl], kk[:, sl], v_all[:, sl]
        g_tot = g[tot:tot + 1, :]
        st = st_ref[h]
        o = _dot_nt((qh * jnp.exp2(g)).astype(BF16), st.astype(BF16))
        o = o + jnp.dot(at_ref[sub, h], vb, preferred_element_type=F32)
        k_st = (kh * jnp.exp2(g_tot - g)).astype(BF16)
        st_ref[h] = st * jnp.exp2(g_tot) + jnp.dot(vb.astype(F32).T.astype(BF16), k_st, preferred_element_type=F32)
        if reverse:
            o = o + of_ref[rs, sl]
            o = o * lax.rsqrt(jnp.mean(o * o, axis=-1, keepdims=True) + EPS) * w_ref[:, sl]
        o_ref[rs, sl] = o.astype(o_ref.dtype)


def _hgrn_scan(p_cm, lb, norm_w, n_ctx):
    b, t, _ = p_cm.shape
    col_q, col_f, col_i = 0, 1, 3
    step_rows = HG_STEP_CHUNKS * HG_CHUNK
    assert n_ctx % step_rows == 0 and t % step_rows == 0, (n_ctx, t, step_rows)
    nc, ncc = t // step_rows, n_ctx // step_rows
    blk = (None, step_rows, HG_QK)
    scratch = [pltpu.VMEM((HG_HEADS, HG_VDIM, HG_KDIM), F32), pltpu.VMEM((HG_STEP_CHUNKS, HG_HEADS, HG_CHUNK, HG_CHUNK), BF16)]
    params = pltpu.CompilerParams(dimension_semantics=("arbitrary", "arbitrary"), vmem_limit_bytes=VMEM_LIMIT)
    row_spec = pl.BlockSpec((1, HG_QK), lambda bi, s: (0, 0))

    def fwd_chunk(s):
        return s

    def bwd_chunk(s):
        return jnp.where(s < ncc, ncc - 1 - s, nc + ncc - 1 - s)

    o_f = pl.pallas_call(
        functools.partial(_gla_kernel, reverse=False),
        grid=(b, nc),
        in_specs=[
            pl.BlockSpec(blk, lambda bi, s: (bi, fwd_chunk(s), col_q)),
            pl.BlockSpec(blk, lambda bi, s: (bi, fwd_chunk(s), col_f)),
            pl.BlockSpec(blk, lambda bi, s: (bi, fwd_chunk(s), col_i)),
            row_spec,
        ],
        out_specs=pl.BlockSpec(blk, lambda bi, s: (bi, fwd_chunk(s), 0)),
        out_shape=jax.ShapeDtypeStruct((b, t, HG_V), F32),
        scratch_shapes=scratch,
        compiler_params=params,
        name="gla_fwd",
    )(p_cm, p_cm, p_cm, lb[0:1])
    return pl.pallas_call(
        functools.partial(_gla_kernel, reverse=True),
        grid=(b, nc),
        in_specs=[
            pl.BlockSpec(blk, lambda bi, s: (bi, bwd_chunk(s), col_q)),
            pl.BlockSpec(blk, lambda bi, s: (bi, bwd_chunk(s), col_f + 1)),
            pl.BlockSpec(blk, lambda bi, s: (bi, bwd_chunk(s), col_i)),
            row_spec,
            pl.BlockSpec(blk, lambda bi, s: (bi, bwd_chunk(s), 0)),
            row_spec,
        ],
        out_specs=pl.BlockSpec(blk, lambda bi, s: (bi, bwd_chunk(s), 0)),
        out_shape=jax.ShapeDtypeStruct((b, t, HG_V), BF16),
        scratch_shapes=scratch,
        compiler_params=params,
        name="gla_bwd",
    )(p_cm, p_cm, p_cm, lb[1:2], o_f, norm_w.reshape(1, HG_V))


CONV_TILE = 256
HALO = 16


def _dwconv_kernel(prev_ref, cur_ref, next_ref, w_ref, b_ref, *o_refs, taps, n_ctx_tiles, n_tiles, silu):
    i = pl.program_id(2)
    first = jnp.logical_or(i == 0, i == n_ctx_tiles)
    last = jnp.logical_or(i == n_ctx_tiles - 1, i == n_tiles - 1)
    pad = taps // 2
    xp = jnp.where(first, 0.0, prev_ref[...].astype(F32))
    xn = jnp.where(last, 0.0, next_ref[...].astype(F32))
    xcat = jnp.concatenate([xp, cur_ref[...].astype(F32), xn], axis=0)
    acc = jnp.broadcast_to(b_ref[...], cur_ref.shape).astype(F32)
    for k in range(taps):
        off = HALO - pad + k
        acc = acc + w_ref[k:k + 1, :] * xcat[off:off + CONV_TILE, :]
    if silu:
        acc = acc * jax.nn.sigmoid(acc)
    if len(o_refs) == 1:
        o_refs[0][...] = acc.astype(o_refs[0].dtype)
    else:
        ctx_ref, lat_ref = o_refs

        @pl.when(i < n_ctx_tiles)
        def _():
            ctx_ref[...] = acc.astype(ctx_ref.dtype)

        @pl.when(i >= n_ctx_tiles)
        def _():
            lat_ref[...] = acc.astype(lat_ref.dtype)


def _dwconv_stream(x, w, bias, n_ctx, silu, c0=0, ct=1024, split=False):
    b, t, _ = x.shape
    taps, c = w.shape
    n_tiles = t // CONV_TILE
    nct = n_ctx // CONV_TILE
    hb = CONV_TILE // HALO
    n_halo = t // HALO
    j0 = c0 // ct
    kern = functools.partial(_dwconv_kernel, taps=taps, n_ctx_tiles=nct, n_tiles=n_tiles, silu=silu)
    if split:
        out_specs = [pl.BlockSpec((None, CONV_TILE, ct), lambda bi, j, i: (bi, jnp.minimum(i, nct - 1), j)),
                     pl.BlockSpec((None, CONV_TILE, ct), lambda bi, j, i: (bi, jnp.maximum(i - nct, 0), j))]
        out_shape = [jax.ShapeDtypeStruct((b, n_ctx, c), BF16), jax.ShapeDtypeStruct((b, t - n_ctx, c), BF16)]
    else:
        out_specs = pl.BlockSpec((None, CONV_TILE, ct), lambda bi, j, i: (bi, i, j))
        out_shape = jax.ShapeDtypeStruct((b, t, c), BF16)
    return pl.pallas_call(
        kern,
        grid=(b, c // ct, n_tiles),
        in_specs=[
            pl.BlockSpec((None, HALO, ct), lambda bi, j, i: (bi, jnp.maximum(i * hb - 1, 0), j0 + j)),
            pl.BlockSpec((None, CONV_TILE, ct), lambda bi, j, i: (bi, i, j0 + j)),
            pl.BlockSpec((None, HALO, ct), lambda bi, j, i: (bi, jnp.minimum((i + 1) * hb, n_halo - 1), j0 + j)),
            pl.BlockSpec((taps, ct), lambda bi, j, i: (0, j)),
            pl.BlockSpec((1, ct), lambda bi, j, i: (0, j)),
        ],
        out_specs=out_specs,
        out_shape=out_shape,
        compiler_params=pltpu.CompilerParams(dimension_semantics=("arbitrary",) * 3, vmem_limit_bytes=VMEM_LIMIT),
        name="dwconv",
    )(x, x, x, w.astype(F32), bias.reshape(1, c).astype(F32))


SSD_STEP_CHUNKS = 2
SSM_GHEADS = SSM_HEADS // SSM_GROUPS
SSM_GP = SSM_GHEADS * SSM_HEAD_DIM


def _ssd_kernel(xbc_ref, dt_ref, dtb_ref, a_ref, *rest, reverse):
    if reverse:
        yf_ref, z_ref, dsk_ref, nw_ref, o_ref, st_ref, m_ref, xd_ref, xst_ref, y_ref = rest
    else:
        o_ref, st_ref, m_ref, xd_ref, xst_ref, y_ref = rest
    Q = SSD_CHUNK

    @pl.when(pl.program_id(1) == 0)
    def _():
        st_ref[...] = jnp.zeros_like(st_ref)

    row = lax.broadcasted_iota(jnp.int32, (Q, Q), 0)
    col = lax.broadcasted_iota(jnp.int32, (Q, Q), 1)
    keep = (col >= row) if reverse else (col <= row)
    tri = jnp.where(keep, 1.0, 0.0).astype(BF16)
    expand = jnp.where(lax.broadcasted_iota(jnp.int32, (LANES, SSM_INNER), 1) // SSM_HEAD_DIM == lax.broadcasted_iota(jnp.int32, (LANES, SSM_INNER), 0), 1.0, 0.0).astype(BF16)

    rev_refs = (yf_ref, z_ref, dsk_ref, nw_ref) if reverse else None
    for sub in (range(SSD_STEP_CHUNKS - 1, -1, -1) if reverse else range(SSD_STEP_CHUNKS)):
        _ssd_chunk(sub, xbc_ref, dt_ref, dtb_ref[...], a_ref[...], rev_refs, o_ref, st_ref, m_ref, xd_ref, xst_ref, y_ref, keep, tri, expand, reverse)


def _ssd_chunk(sub, xbc_ref, dt_ref, dt_bias, a_neg, rev_refs, o_ref, st_ref, m_ref, xd_ref, xst_ref, y_ref, keep, tri, expand, reverse):
    Q = SSD_CHUNK
    rs = slice(sub * Q, (sub + 1) * Q)
    dt = jax.nn.softplus(dt_ref[rs, :] + dt_bias)
    a = dt * a_neg
    cs = sum(jnp.dot(tri, p, preferred_element_type=F32) for p in _split3(a))
    tot = 0 if reverse else Q - 1
    cs_tot = cs[tot:tot + 1, :]
    cs_t = cs.T
    dt_e = jnp.dot(dt.astype(BF16), expand, preferred_element_type=F32)
    e_in = jnp.dot(jnp.exp(cs).astype(BF16), expand, preferred_element_type=F32)
    e_st = jnp.dot(jnp.exp(cs_tot - cs).astype(BF16), expand, preferred_element_type=F32)
    e_tot = jnp.dot(jnp.broadcast_to(jnp.exp(cs_tot), (SUBLANES, LANES)).astype(BF16), expand, preferred_element_type=F32)[0:1, :]

    xs = xbc_ref[rs, :SSM_INNER].astype(F32)
    xd_ref[sub] = (xs * dt_e).astype(BF16)
    xst_ref[sub] = (xs * dt_e * e_st).astype(BF16)
    for g in range(SSM_GROUPS):
        bm = xbc_ref[rs, SSM_INNER + g * SSM_STATE:SSM_INNER + (g + 1) * SSM_STATE]
        cm = xbc_ref[rs, SSM_INNER + (SSM_GROUPS + g) * SSM_STATE:SSM_INNER + (SSM_GROUPS + g + 1) * SSM_STATE]
        cb = _dot_nt(cm, bm)
        for h in range(g * SSM_GHEADS, (g + 1) * SSM_GHEADS):
            diff = jnp.broadcast_to(cs[:, h:h + 1], (Q, Q)) - jnp.broadcast_to(cs_t[h:h + 1, :], (Q, Q))
            m_ref[sub, h] = (cb * jnp.exp(jnp.where(keep, diff, NEG_BIG))).astype(BF16)

    lane_lo = lax.broadcasted_iota(jnp.int32, (Q, LANES), 1) < SSM_HEAD_DIM
    for g in range(SSM_GROUPS):
        bm = xbc_ref[rs, SSM_INNER + g * SSM_STATE:SSM_INNER + (g + 1) * SSM_STATE]
        cm = xbc_ref[rs, SSM_INNER + (SSM_GROUPS + g) * SSM_STATE:SSM_INNER + (SSM_GROUPS + g + 1) * SSM_STATE]
        gl = slice(g * SSM_GP, (g + 1) * SSM_GP)
        st = st_ref[g]
        y_off = jnp.dot(cm, st.astype(BF16), preferred_element_type=F32) * e_in[:, gl]
        for hp in range(SSM_GHEADS // 2):
            h0 = g * SSM_GHEADS + 2 * hp
            lanes = slice((h0 // 2) * LANES, (h0 // 2 + 1) * LANES)
            pair = [jnp.dot(m_ref[sub, h], xd_ref[sub, :, lanes], preferred_element_type=F32) for h in (h0, h0 + 1)]
            y_ref[sub, :, lanes] = jnp.where(lane_lo, pair[0], pair[1]) + y_off[:, hp * LANES:(hp + 1) * LANES]
        st_ref[g] = st * e_tot[:, gl] + jnp.dot(bm.astype(F32).T.astype(BF16), xst_ref[sub, :, gl], preferred_element_type=F32)
    y = y_ref[sub]
    if reverse:
        yf_ref, z_ref, dsk_ref, nw_ref = rev_refs
        y = (y + yf_ref[rs, :] + xbc_ref[rs, :SSM_INNER].astype(F32) * dsk_ref[...])
        zz = z_ref[rs, :].astype(F32)
        y = y * (zz * jax.nn.sigmoid(zz))
        y = y * lax.rsqrt(jnp.mean(y * y, axis=-1, keepdims=True) + EPS) * nw_ref[...]
    o_ref[rs, :] = y.astype(o_ref.dtype)


def _ssd_scan(xbc_act, dt2, p_rm, z_col, dt_bias, a_log, d_skip, norm_w, n_ctx):
    b, t, _ = xbc_act.shape
    step_rows = SSD_STEP_CHUNKS * SSD_CHUNK
    assert n_ctx % step_rows == 0 and t % step_rows == 0, (n_ctx, t, step_rows)
    nc, ncc = t // step_rows, n_ctx // step_rows

    def pad_heads(v):
        return jnp.pad(v.astype(F32), ((0, 0), (0, LANES - SSM_HEADS)))

    dtb = pad_heads(dt_bias.reshape(2, SSM_HEADS))
    a_neg = pad_heads(-jnp.exp(a_log.astype(F32)))
    dsk = jnp.repeat(d_skip.astype(F32), SSM_HEAD_DIM).reshape(1, SSM_INNER)
    scratch = [pltpu.VMEM((SSM_GROUPS, SSM_STATE, SSM_GP), F32), pltpu.VMEM((SSD_STEP_CHUNKS, SSM_HEADS, SSD_CHUNK, SSD_CHUNK), BF16),
               pltpu.VMEM((SSD_STEP_CHUNKS, SSD_CHUNK, SSM_INNER), BF16), pltpu.VMEM((SSD_STEP_CHUNKS, SSD_CHUNK, SSM_INNER), BF16),
               pltpu.VMEM((SSD_STEP_CHUNKS, SSD_CHUNK, SSM_INNER), F32)]
    params = pltpu.CompilerParams(dimension_semantics=("arbitrary", "arbitrary"), vmem_limit_bytes=VMEM_LIMIT)

    def bwd_chunk(s):
        return jnp.where(s < ncc, ncc - 1 - s, nc + ncc - 1 - s)

    def specs(chunk, d):
        return [
            pl.BlockSpec((None, step_rows, SSM_XBC), lambda bi, s: (bi, chunk(s), 0)),
            pl.BlockSpec((None, step_rows, LANES), lambda bi, s: (bi, chunk(s), d)),
            pl.BlockSpec((1, LANES), lambda bi, s: (0, 0)),
            pl.BlockSpec((1, LANES), lambda bi, s: (0, 0)),
        ]

    def inner_spec(chunk):
        return pl.BlockSpec((None, step_rows, SSM_INNER), lambda bi, s: (bi, chunk(s), 0))

    row_spec = pl.BlockSpec((1, SSM_INNER), lambda bi, s: (0, 0))
    z_spec = pl.BlockSpec((None, step_rows, SSM_INNER), lambda bi, s: (bi, bwd_chunk(s), z_col // SSM_INNER))
    y_f = pl.pallas_call(
        functools.partial(_ssd_kernel, reverse=False),
        grid=(b, nc),
        in_specs=specs(lambda s: s, 0),
        out_specs=inner_spec(lambda s: s),
        out_shape=jax.ShapeDtypeStruct((b, t, SSM_INNER), F32),
        scratch_shapes=scratch,
        compiler_params=params,
        name="ssd_fwd",
    )(xbc_act, dt2, dtb[0:1], a_neg[0:1])
    return pl.pallas_call(
        functools.partial(_ssd_kernel, reverse=True),
        grid=(b, nc),
        in_specs=specs(bwd_chunk, 1) + [inner_spec(bwd_chunk), z_spec, row_spec, row_spec],
        out_specs=inner_spec(bwd_chunk),
        out_shape=jax.ShapeDtypeStruct((b, t, SSM_INNER), BF16),
        scratch_shapes=scratch,
        compiler_params=params,
        name="ssd_bwd",
    )(xbc_act, dt2, dtb[1:2], a_neg[1:2], y_f, p_rm, dsk, norm_w.reshape(1, SSM_INNER).astype(F32))


HY_N2 = LANES
HY_CT = LANES
VMEM_LIMIT_HYENA = 60 * 1024 * 1024


def _hy_dims(L):
    n1 = 2 * L // HY_N2
    k1n = n1 // 2 + 1
    k1p = -(-k1n // SUBLANES) * SUBLANES
    return n1, k1n, k1p


def _hy_tables(L, n1_rows):
    n1, k1n, k1p = _hy_dims(L)
    n = 2 * L
    k1 = np.arange(k1n, dtype=np.float64)[None, :, None]
    nn = (HY_N2 * np.arange(n1_rows, dtype=np.float64)[None, None, :] + np.arange(HY_N2, dtype=np.float64)[:, None, None])
    ang = 2.0 * np.pi * ((k1 * nn) % n) / n
    m1 = np.zeros((HY_N2, 2 * k1p, n1_rows), np.float32)
    m1[:, :k1n] = np.cos(ang)
    m1[:, k1p:k1p + k1n] = -np.sin(ang)
    m4 = np.transpose(m1, (0, 2, 1))
    kk = np.arange(HY_N2, dtype=np.float64)
    a2 = 2.0 * np.pi * ((kk[:, None] * kk[None, :]) % HY_N2) / HY_N2
    c, s = np.cos(a2), np.sin(a2)
    f3 = np.block([[c, s], [-s, c]]).astype(np.float32)
    f3i = np.block([[c, -s], [s, c]]).astype(np.float32)
    return jnp.asarray(m1, BF16), jnp.asarray(m4, BF16), jnp.asarray(f3, BF16), jnp.asarray(f3i, BF16)


def _hy_stage1(u_refs, a_refs, m1_ref, n1_rows, k1p):
    def body(n2, carry):
        xs = jnp.concatenate([u_ref[pl.ds(n2, n1_rows, stride=HY_N2), :] for u_ref in u_refs], axis=1).astype(BF16)
        a = jnp.dot(m1_ref[n2], xs, preferred_element_type=F32)
        for i, a_ref in enumerate(a_refs):
            a_ref[pl.ds(pl.multiple_of(n2 * 2 * k1p, 2 * k1p), 2 * k1p), :] = a[:, i * HY_CT:(i + 1) * HY_CT]
        return carry

    lax.fori_loop(0, HY_N2, body, 0, unroll=16)


def _hy_spectrum_slab(a_refs, f3_ref, k1s, k1p):
    blk = jnp.concatenate([jnp.concatenate([a_ref[pl.ds(k1, HY_N2, stride=2 * k1p), :], a_ref[pl.ds(k1p + k1, HY_N2, stride=2 * k1p), :]], axis=0) for k1 in k1s for a_ref in a_refs], axis=1)
    return jnp.dot(f3_ref[...], blk.astype(BF16), preferred_element_type=F32)


def _hy_k1_loop(body, k1n):
    def pair(p, carry):
        body((2 * p, 2 * p + 1))
        return carry

    lax.fori_loop(0, (k1n - 1) // 2, pair, 0, unroll=math.gcd((k1n - 1) // 2, 4))
    body((k1n - 1,))


def _hy_slab_rows(k1):
    start = k1 * 2 * HY_N2
    return pl.ds(start if isinstance(k1, int) else pl.multiple_of(start, 2 * HY_N2), 2 * HY_N2)


def _hy_conv(u_ref, yo_ref, a_ref, y_ref, h_ref, order, m1_ref, m4_ref, f3_ref, f3i_ref, n1_rows, k1n, k1p):
    _hy_stage1((u_ref,), (a_ref,), m1_ref, n1_rows, k1p)

    def stage2(k1s):
        x = _hy_spectrum_slab((a_ref,), f3_ref, k1s, k1p)
        h = jnp.concatenate([h_ref[order, _hy_slab_rows(k1), :] for k1 in k1s], axis=1).astype(F32)
        xr, xi, hr, hi = x[:HY_N2], x[HY_N2:], h[:HY_N2], h[HY_N2:]
        z = jnp.concatenate([xr * hr - xi * hi, xr * hi + xi * hr], axis=0).astype(BF16)
        c = jnp.dot(f3i_ref[...], z, preferred_element_type=F32)
        for i, k1 in enumerate(k1s):
            lanes = slice(i * HY_CT, (i + 1) * HY_CT)
            y_ref[pl.ds(k1, HY_N2, stride=2 * k1p), :] = c[:HY_N2, lanes]
            y_ref[pl.ds(k1p + k1, HY_N2, stride=2 * k1p), :] = c[HY_N2:, lanes]

    _hy_k1_loop(stage2, k1n)

    def stage3(n2, carry):
        d = y_ref[pl.ds(pl.multiple_of(n2 * 2 * k1p, 2 * k1p), 2 * k1p), :].astype(BF16)
        yo_ref[pl.ds(n2, n1_rows, stride=HY_N2), :] = jnp.dot(m4_ref[n2], d, preferred_element_type=F32)
        return carry

    lax.fori_loop(0, HY_N2, stage3, 0, unroll=16)


def _hyena_kernel(v_ref, x1_ref, x2_ref, h_ref, m1_ref, m4_ref, f3_ref, f3i_ref, bias_ref, o_ref, a_ref, y_ref, u_ref, yo_ref, *, n1_rows, k1n, k1p):
    @pl.when(jnp.logical_and(pl.program_id(0) == 0, pl.program_id(1) == 0))
    def _():
        y_ref[...] = jnp.zeros_like(y_ref)

    u_ref[...] = v_ref[...].astype(F32)
    for order, gate_ref in enumerate((x1_ref, x2_ref)):
        _hy_conv(u_ref, yo_ref, a_ref, y_ref, h_ref, order, m1_ref, m4_ref, f3_ref, f3i_ref, n1_rows, k1n, k1p)
        z = gate_ref[...].astype(F32) * (yo_ref[...] + u_ref[...] * bias_ref[order:order + 1, :])
        if order == 0:
            u_ref[...] = z
        else:
            o_ref[...] = z.astype(o_ref.dtype)


def _hy_filter_kernel(hid_ref, wf_ref, wb_ref, bf_ref, bb_ref, dl_ref, m1_ref, f3_ref, o_ref, af_ref, ab_ref, uf_ref, ub_ref, *, n1_rows, k1n, k1p, scale_mid, scale_edge):
    L = uf_ref.shape[0]
    hid_rows = math.gcd(L, 2 * HY_N2)

    def fill(w_ref, b_ref, dst_ref, drop_first):
        w = _split3(w_ref[...])

        def body(i, energy):
            rows = pl.ds(pl.multiple_of(i * hid_rows, hid_rows), hid_rows)
            hid = _split3(hid_ref[rows, :])
            acc = sum(jnp.dot(hid[p], w[q], preferred_element_type=F32) for p in range(2) for q in range(2 - p))
            pos = i * hid_rows + lax.broadcasted_iota(jnp.int32, (hid_rows, HY_CT), 0)
            vals = (acc + b_ref[...]) * jnp.exp(pos.astype(F32) * (-1.0 / (L - 1)) * dl_ref[...])
            if drop_first:
                vals = jnp.where(pos == 0, 0.0, vals)
            dst_ref[rows, :] = vals
            return energy + jnp.sum(vals * vals, axis=0, keepdims=True)

        return lax.fori_loop(0, L // hid_rows, body, jnp.zeros((1, HY_CT), F32), unroll=math.gcd(L // hid_rows, 4))

    norm = lax.rsqrt(fill(wf_ref, bf_ref, uf_ref, False) + fill(wb_ref, bb_ref, ub_ref, True) + EPS)

    _hy_stage1((uf_ref, ub_ref), (af_ref, ab_ref), m1_ref, n1_rows, k1p)

    def combine(k1s):
        x_all = _hy_spectrum_slab((af_ref, ab_ref), f3_ref, k1s, k1p)
        for i, k1 in enumerate(k1s):
            xf = x_all[:, 2 * i * HY_CT:(2 * i + 1) * HY_CT]
            xb = x_all[:, (2 * i + 1) * HY_CT:(2 * i + 2) * HY_CT]
            w = norm * jnp.where(jnp.logical_or(k1 == 0, k1 == k1n - 1), scale_edge, scale_mid)
            h = jnp.concatenate([xf[:HY_N2] + xb[:HY_N2], xf[HY_N2:] - xb[HY_N2:]], axis=0)
            o_ref[_hy_slab_rows(k1), :] = (h * w).astype(o_ref.dtype)

    _hy_k1_loop(combine, k1n)


def _single(block_shape, index_map):
    return pl.BlockSpec(block_shape, index_map, pipeline_mode=pl.Buffered(1))


def _hyena_filter_spectrum_pallas(hidden, w3, b3, deltas):
    L, ffn = hidden.shape
    c = deltas.shape[0]
    nct = c // HY_CT
    n = 2 * L
    n1, k1n, k1p = _hy_dims(L)
    n1_rows = L // HY_N2
    m1, _, f3, _ = _hy_tables(L, n1_rows)
    kern = functools.partial(_hy_filter_kernel, n1_rows=n1_rows, k1n=k1n, k1p=k1p, scale_mid=2.0 / n, scale_edge=1.0 / n)
    return pl.pallas_call(
        kern,
        grid=(HY_ORDER, nct),
        in_specs=[
            _single((L, ffn), lambda o, j: (0, 0)),
            pl.BlockSpec((ffn, HY_CT), lambda o, j: (0, o * nct + j)),
            pl.BlockSpec((ffn, HY_CT), lambda o, j: (0, (HY_ORDER + o) * nct + j)),
            pl.BlockSpec((1, HY_CT), lambda o, j: (0, o * nct + j)),
            pl.BlockSpec((1, HY_CT), lambda o, j: (0, (HY_ORDER + o) * nct + j)),
            pl.BlockSpec((1, HY_CT), lambda o, j: (0, j)),
            _single((HY_N2, 2 * k1p, n1_rows), lambda o, j: (0, 0, 0)),
            _single((2 * HY_N2, 2 * HY_N2), lambda o, j: (0, 0)),
        ],
        out_specs=pl.BlockSpec((None, k1n * 2 * HY_N2, HY_CT), lambda o, j: (o, 0, j)),
        out_shape=jax.ShapeDtypeStruct((HY_ORDER, k1n * 2 * HY_N2, c), BF16),
        scratch_shapes=[
            pltpu.VMEM((k1p * 2 * HY_N2, HY_CT), F32),
            pltpu.VMEM((k1p * 2 * HY_N2, HY_CT), F32),
            pltpu.VMEM((L, HY_CT), F32),
            pltpu.VMEM((L, HY_CT), F32),
        ],
        compiler_params=pltpu.CompilerParams(dimension_semantics=("arbitrary", "arbitrary"), vmem_limit_bytes=VMEM_LIMIT_HYENA),
        name="hyena_filter_dft",
    )(hidden, w3, w3, b3.reshape(1, -1), b3.reshape(1, -1), deltas.reshape(1, c), m1, f3)


def _hyena_long(hy, h_spec, bias):
    b, L, c3 = hy.shape
    c = c3 // (HY_ORDER + 1)
    nct = c // HY_CT
    n1, k1n, k1p = _hy_dims(L)
    n1_rows = L // HY_N2
    m1, m4, f3, f3i = _hy_tables(L, n1_rows)
    kern = functools.partial(_hyena_kernel, n1_rows=n1_rows, k1n=k1n, k1p=k1p)

    def col(part):
        return _single((None, L, HY_CT), lambda j, bi: (bi, 0, part * nct + j))

    return pl.pallas_call(
        kern,
        grid=(nct, b),
        in_specs=[
            col(0), col(1), col(2),
            _single((HY_ORDER, k1n * 2 * HY_N2, HY_CT), lambda j, bi: (0, 0, j)),
            _single((HY_N2, 2 * k1p, n1_rows), lambda j, bi: (0, 0, 0)),
            _single((HY_N2, n1_rows, 2 * k1p), lambda j, bi: (0, 0, 0)),
            _single((2 * HY_N2, 2 * HY_N2), lambda j, bi: (0, 0)),
            _single((2 * HY_N2, 2 * HY_N2), lambda j, bi: (0, 0)),
            pl.BlockSpec((HY_ORDER, HY_CT), lambda j, bi: (0, j)),
        ],
        out_specs=pl.BlockSpec((None, L, HY_CT), lambda j, bi: (bi, 0, j)),
        out_shape=jax.ShapeDtypeStruct((b, L, c), BF16),
        scratch_shapes=[
            pltpu.VMEM((k1p * 2 * HY_N2, HY_CT), F32),
            pltpu.VMEM((HY_N2 * 2 * k1p, HY_CT), F32),
            pltpu.VMEM((L, HY_CT), F32),
            pltpu.VMEM((L, HY_CT), F32),
        ],
        compiler_params=pltpu.CompilerParams(dimension_semantics=("arbitrary", "arbitrary"), vmem_limit_bytes=VMEM_LIMIT_HYENA),
        name="hyena_long_conv",
    )(hy, hy, hy, h_spec, m1, m4, f3, f3i, bias.astype(F32))


HY_CTX_CT = 256


def _hy_ctx_tables(L):
    n = 2 * L
    kb = L + 1
    kp = -(-kb // LANES) * LANES
    ang = 2.0 * np.pi * ((np.arange(kb, dtype=np.float64)[:, None] * np.arange(L, dtype=np.float64)[None, :]) % n) / n
    fwd = np.zeros((2 * kp, L), np.float32)
    fwd[:kb] = np.cos(ang)
    fwd[kp:kp + kb] = -np.sin(ang)
    return jnp.asarray(fwd, BF16), jnp.asarray(fwd.T, BF16), kb, kp


def _hy_ctx_filter_kernel(fwd_ref, bwd_ref, f_ref, o_ref, *, kb, kp, n):
    fwd = fwd_ref[...]
    bwd = jnp.where(lax.broadcasted_iota(jnp.int32, fwd.shape, 0) == 0, 0.0, bwd_ref[...])
    norm = lax.rsqrt(jnp.sum(fwd * fwd, axis=0, keepdims=True) + jnp.sum(bwd * bwd, axis=0, keepdims=True) + EPS)
    hf = jnp.dot(f_ref[...], fwd.astype(BF16), preferred_element_type=F32)
    hb = jnp.dot(f_ref[...], bwd.astype(BF16), preferred_element_type=F32)
    row = lax.broadcasted_iota(jnp.int32, hf.shape, 0)
    imag = row >= kp
    k = jnp.where(imag, row - kp, row)
    wk = jnp.where(jnp.logical_or(k == 0, k == kb - 1), 1.0 / n, 2.0 / n)
    o_ref[...] = (hf + jnp.where(imag, -hb, hb)) * (wk * norm)


def _hy_ctx_kernel(v_ref, x1_ref, x2_ref, h_ref, f_ref, g_ref, bias_ref, o_ref, *, kp):
    u = v_ref[...].astype(F32)
    for order, gate_ref in enumerate((x1_ref, x2_ref)):
        x = jnp.dot(f_ref[...], u.astype(BF16), preferred_element_type=F32)
        h = h_ref[order]
        xr, xi, hr, hi = x[:kp], x[kp:], h[:kp], h[kp:]
        z = jnp.concatenate([xr * hr - xi * hi, xr * hi + xi * hr], axis=0).astype(BF16)
        y = jnp.dot(g_ref[...], z, preferred_element_type=F32)
        u = gate_ref[...].astype(F32) * (y + u * bias_ref[order:order + 1, :])
    o_ref[...] = u.astype(o_ref.dtype)


def _hyena_ctx(hy, taps, bias):
    b, L, _ = hy.shape
    c = hy.shape[2] // (HY_ORDER + 1)
    ct = HY_CTX_CT
    nct = c // ct
    f_mat, g_mat, kb, kp = _hy_ctx_tables(L)
    params = pltpu.CompilerParams(dimension_semantics=("arbitrary", "arbitrary"), vmem_limit_bytes=VMEM_LIMIT)
    h_spec = pl.pallas_call(
        functools.partial(_hy_ctx_filter_kernel, kb=kb, kp=kp, n=2 * L),
        grid=(HY_ORDER, nct),
        in_specs=[
            pl.BlockSpec((L, ct), lambda o, j: (0, o * nct + j)),
            pl.BlockSpec((L, ct), lambda o, j: (0, (HY_ORDER + o) * nct + j)),
            pl.BlockSpec((2 * kp, L), lambda o, j: (0, 0)),
        ],
        out_specs=pl.BlockSpec((None, 2 * kp, ct), lambda o, j: (o, 0, j)),
        out_shape=jax.ShapeDtypeStruct((HY_ORDER, 2 * kp, c), F32),
        compiler_params=params,
        name="hyena_ctx_filter",
    )(taps, taps, f_mat)

    def col(part):
        return pl.BlockSpec((None, L, ct), lambda j, bi: (bi, 0, part * nct + j))

    return pl.pallas_call(
        functools.partial(_hy_ctx_kernel, kp=kp),
        grid=(nct, b),
        in_specs=[
            col(0), col(1), col(2),
            pl.BlockSpec((HY_ORDER, 2 * kp, ct), lambda j, bi: (0, 0, j)),
            pl.BlockSpec((2 * kp, L), lambda j, bi: (0, 0)),
            pl.BlockSpec((L, 2 * kp), lambda j, bi: (0, 0)),
            pl.BlockSpec((HY_ORDER, ct), lambda j, bi: (0, j)),
        ],
        out_specs=pl.BlockSpec((None, L, ct), lambda j, bi: (bi, 0, j)),
        out_shape=jax.ShapeDtypeStruct((b, L, c), BF16),
        compiler_params=params,
        name="hyena_ctx_conv",
    )(hy, hy, hy, h_spec, f_mat, g_mat, bias.astype(F32))


def _split_cols(t, sizes):
    return jnp.split(t, np.cumsum(sizes)[:-1].tolist(), axis=-1)


def _to_col_major(t, rows):
    b, rest = t.shape[0], t.shape[2:]
    return jnp.swapaxes(t.reshape((b, rows, GRID_W) + rest), 1, 2).reshape((b, rows * GRID_W) + rest)


def _from_col_major(t, rows):
    b, rest = t.shape[0], t.shape[2:]
    return jnp.swapaxes(t.reshape((b, GRID_W, rows) + rest), 1, 2).reshape((b, rows * GRID_W) + rest)


def _hyena_filter_hidden(L, w1, b1, w2, b2, freq):
    hp = lax.Precision.HIGHEST
    t = jnp.linspace(0.0, 1.0, L, dtype=F32)[:, None]
    w = 2.0 * math.pi * jnp.arange(L, dtype=F32)[:, None] / L
    bands = jnp.linspace(1e-4, HY_BANDS - 1, HY_BANDS, dtype=F32)
    feats = jnp.concatenate([t, jnp.cos(bands * w), -jnp.sin(bands * w)], axis=-1)
    h = jnp.sin(freq[0] * (jnp.dot(feats, w1, precision=hp) + b1))
    return jnp.sin(freq[1] * (jnp.dot(h, w2, precision=hp) + b2))


def _hyena_decay_rates():
    max_decay = math.log(HY_DECAY_TARGET) / HY_FAST_DECAY
    min_decay = math.log(HY_DECAY_TARGET) / HY_SLOW_DECAY
    return jnp.abs(jnp.linspace(min_decay, max_decay, HY_WIDTH, dtype=F32))


def _hyena_filter_taps(L, w1, b1, w2, b2, w3, b3, freq):
    h = jnp.dot(_hyena_filter_hidden(L, w1, b1, w2, b2, freq), w3, precision=lax.Precision.HIGHEST) + b3
    t = jnp.linspace(0.0, 1.0, L, dtype=F32)[:, None]
    return h * jnp.tile(jnp.exp(-t * _hyena_decay_rates()), (1, 2 * HY_ORDER))


def _mixer_branches(h_rm, w_in, lb, n_ctx, ssm_conv_w, ssm_conv_b, ssm_dt_bias, ssm_a_log, ssm_d, ssm_norm, hy_conv_w, hy_conv_b, hy_w1, hy_b1, hy_w2, hy_b2, hy_w3, hy_b3, hy_freq, hy_bias, hg_norm):
    b, t, d = h_rm.shape
    n_lat = t - n_ctx
    rows = n_lat // GRID_W
    w_z, w_xbc, w_dt, w_hy, w_q, w_f, w_i, w_g, w_gate = _split_cols(w_in, IN_SIZES)
    h2 = h_rm.reshape(b * t, d)
    rm_parts = (w_z, w_xbc, w_hy, w_g, w_gate)
    col_z, col_xbc, col_hy, col_g, col_gate = np.cumsum([0] + [w.shape[1] for w in rm_parts[:-1]]).tolist()
    p_rm = _mm(h2, jnp.concatenate(rm_parts, axis=1).astype(BF16), BF16).reshape(b, t, -1)

    zero_pad = jnp.zeros((d, LANES - SSM_HEADS), F32)
    w_dt2 = jnp.concatenate([w_dt[:, :SSM_HEADS], zero_pad, w_dt[:, SSM_HEADS:], zero_pad], axis=1)
    dt2 = _mm(h2, w_dt2.astype(BF16), F32).reshape(b, t, 2 * LANES)
    xbc_act = _dwconv_stream(p_rm, ssm_conv_w, ssm_conv_b, n_ctx, True, c0=col_xbc)
    ym = _ssd_scan(xbc_act, dt2, p_rm, col_z, ssm_dt_bias, ssm_a_log, ssm_d, ssm_norm, n_ctx)

    hy_ctx, hy_lat = _dwconv_stream(p_rm, hy_conv_w, hy_conv_b, n_ctx, False, c0=col_hy, split=True)
    taps_ctx = _hyena_filter_taps(n_ctx, hy_w1, hy_b1, hy_w2, hy_b2, hy_w3, hy_b3, hy_freq)
    yh_ctx = _hyena_ctx(hy_ctx, taps_ctx, hy_bias)
    h_spec = _hyena_filter_spectrum_pallas(_hyena_filter_hidden(n_lat, hy_w1, hy_b1, hy_w2, hy_b2, hy_freq), hy_w3, hy_b3, _hyena_decay_rates())
    yh_lat = _hyena_long(hy_lat, h_spec, hy_bias)

    h_cm = jnp.concatenate([h_rm[:, :n_ctx], _to_col_major(h_rm[:, n_ctx:], rows)], axis=1).reshape(b * t, d)

    p_cm = _mm(h_cm, jnp.concatenate([w_q, w_f, w_i], axis=1).astype(BF16), BF16).reshape(b, t, -1)
    og = _hgrn_scan(p_cm, lb, hg_norm, n_ctx)
    return ym, (yh_ctx, yh_lat), (og, _from_col_major(og[:, n_ctx:], rows)), p_rm, col_g, col_gate


ROW_TILE = 256
MOD_ROWS = SUBLANES
M_SHIFT_MIX, M_SCALE_MIX, M_GATE_MIX, M_SHIFT_FFN, M_SCALE_FFN, M_GATE_FFN = range(6)
ROW_PARAMS = pltpu.CompilerParams(dimension_semantics=("arbitrary", "arbitrary"), vmem_limit_bytes=VMEM_LIMIT)


def _rms(x):
    return x * lax.rsqrt(jnp.mean(x * x, axis=-1, keepdims=True) + EPS)


def _mrow(m_ref, r):
    return m_ref[r:r + 1, :]


def _row_spec(width):
    return pl.BlockSpec((None, ROW_TILE, width), lambda bi, i: (bi, i, 0))


def _vec_spec(width):
    return pl.BlockSpec((1, width), lambda bi, i: (0, 0))


def _mat_spec(k, n):
    return pl.BlockSpec((k, n), lambda bi, i: (0, 0))


def _mod_spec(n_ctx):
    return pl.BlockSpec((None, None, MOD_ROWS, D_MODEL), lambda bi, i: (bi, jnp.where(i < n_ctx // ROW_TILE, 0, 1), 0, 0))


def _norm_mod_kernel(x_ref, w_ref, m_ref, o_ref):
    y = _rms(x_ref[...]) * w_ref[...]
    o_ref[...] = (y * (1.0 + _mrow(m_ref, M_SCALE_MIX)) + _mrow(m_ref, M_SHIFT_MIX)).astype(o_ref.dtype)


def _norm_mod(xs, w, mods, n_ctx):
    b, t, d = xs.shape
    return pl.pallas_call(
        _norm_mod_kernel,
        grid=(b, t // ROW_TILE),
        in_specs=[_row_spec(d), _vec_spec(d), _mod_spec(n_ctx)],
        out_specs=_row_spec(d),
        out_shape=jax.ShapeDtypeStruct((b, t, d), BF16),
        compiler_params=ROW_PARAMS,
        name="norm_mod",
    )(xs, w.reshape(1, d), mods)


def _merge_kernel(ym_ref, yhc_ref, yhl_ref, ogc_ref, ogl_ref, g_ref, gm_ref, gh_ref, gg_ref, x_ref, m_ref, w1_ref, w2_ref, w3_ref, wo_ref, npost_ref, npre_ref, rw_ref, rb_ref, xo_ref, h_ref, lg_ref, *, n_ctx_tiles):
    is_ctx = pl.program_id(1) < n_ctx_tiles
    yh = jnp.where(is_ctx, yhc_ref[...], yhl_ref[...])
    og = jnp.where(is_ctx, ogc_ref[...], ogl_ref[...])
    def sig(ref):
        return jax.nn.sigmoid(ref[...].astype(F32))

    go = g_ref[...].astype(F32)
    yg = (og.astype(F32) * (go * jax.nn.sigmoid(go))).astype(BF16)
    merged = sig(gm_ref) * jnp.dot(ym_ref[...], w1_ref[...], preferred_element_type=F32)
    merged = merged + sig(gh_ref) * jnp.dot(yh, w2_ref[...], preferred_element_type=F32)
    merged = merged + sig(gg_ref) * jnp.dot(yg, w3_ref[...], preferred_element_type=F32)
    mix = jnp.dot(merged.astype(BF16), wo_ref[...], preferred_element_type=F32)
    x = x_ref[...] + _mrow(m_ref, M_GATE_MIX) * (_rms(mix) * npost_ref[...])
    xo_ref[...] = x
    h = (_rms(x) * npre_ref[...] * (1.0 + _mrow(m_ref, M_SCALE_FFN)) + _mrow(m_ref, M_SHIFT_FFN)).astype(BF16)
    h_ref[...] = h
    lg_ref[...] = jnp.dot(h, rw_ref[...], preferred_element_type=F32) + rb_ref[...]


def _merge(ym, yh_parts, og_parts, p_rm, col_g, col_gate, xs, mods, w_br_ssm, w_br_hy, w_br_hg, w_out, norm_post, norm_ffn_pre, router_w, router_b, n_ctx):
    b, t, d = xs.shape
    rw = jnp.pad(router_w, ((0, 0), (0, LANES - N_EXPERTS))).astype(BF16)
    rb = jnp.pad(router_b, (0, LANES - N_EXPERTS)).reshape(1, LANES).astype(F32)

    def col_spec(col):
        return pl.BlockSpec((None, ROW_TILE, d), lambda bi, i: (bi, i, col // d))

    nct = n_ctx // ROW_TILE
    ctx_spec = pl.BlockSpec((None, ROW_TILE, d), lambda bi, i: (bi, jnp.minimum(i, nct - 1), 0))
    lat_spec = pl.BlockSpec((None, ROW_TILE, d), lambda bi, i: (bi, jnp.maximum(i - nct, 0), 0))

    return pl.pallas_call(
        functools.partial(_merge_kernel, n_ctx_tiles=nct),
        grid=(b, t // ROW_TILE),
        in_specs=[_row_spec(d), ctx_spec, lat_spec, ctx_spec, lat_spec, col_spec(col_g), col_spec(col_gate), col_spec(col_gate + d), col_spec(col_gate + 2 * d), _row_spec(d), _mod_spec(n_ctx),
                  _mat_spec(d, d), _mat_spec(d, d), _mat_spec(d, d), _mat_spec(d, d), _vec_spec(d), _vec_spec(d), _mat_spec(d, LANES), _vec_spec(LANES)],
        out_specs=[_row_spec(d), _row_spec(d), _row_spec(LANES)],
        out_shape=[jax.ShapeDtypeStruct((b, t, d), F32), jax.ShapeDtypeStruct((b, t, d), BF16), jax.ShapeDtypeStruct((b, t, LANES), F32)],
        compiler_params=ROW_PARAMS,
        name="branch_merge",
    )(ym, yh_parts[0], yh_parts[1], og_parts[0], og_parts[1], p_rm, p_rm, p_rm, p_rm, xs, mods, w_br_ssm.astype(BF16), w_br_hy.astype(BF16), w_br_hg.astype(BF16), w_out.astype(BF16),
      norm_post.reshape(1, d), norm_ffn_pre.reshape(1, d), rw, rb)


def _post_ffn_kernel(y0_ref, y1_ref, y2_ref, y3_ref, x_ref, m_ref, w_ref, *rest):
    f = y0_ref[...].astype(F32) + y1_ref[...].astype(F32) + y2_ref[...].astype(F32) + y3_ref[...].astype(F32)
    x = x_ref[...] + _mrow(m_ref, M_GATE_FFN) * (_rms(f) * w_ref[...])
    if len(rest) == 1:
        rest[0][...] = x
    else:
        nm_ref, nw_ref, o_ref, h_ref = rest
        o_ref[...] = x
        h_ref[...] = (_rms(x) * nw_ref[...] * (1.0 + _mrow(nm_ref, M_SCALE_MIX)) + _mrow(nm_ref, M_SHIFT_MIX)).astype(h_ref.dtype)


def _post_ffn(f4, xs, mods, norm_post, n_ctx, skip, nxt=None):
    b, _, d = xs.shape
    t = f4.shape[2]
    i0 = skip // ROW_TILE

    def k_spec(k):
        return pl.BlockSpec((None, None, ROW_TILE, d), lambda bi, i: (k, bi, i, 0))

    mod_spec = pl.BlockSpec((None, None, MOD_ROWS, d), lambda bi, i: (bi, jnp.where(i0 + i < n_ctx // ROW_TILE, 0, 1), 0, 0))
    in_specs = [k_spec(k) for k in range(TOP_K)] + [pl.BlockSpec((None, ROW_TILE, d), lambda bi, i: (bi, i0 + i, 0)), mod_spec, _vec_spec(d)]
    args = [f4, f4, f4, f4, xs, mods, norm_post.reshape(1, d)]
    out_specs, out_shape = _row_spec(d), jax.ShapeDtypeStruct((b, t, d), F32)
    if nxt is not None:
        in_specs += [mod_spec, _vec_spec(d)]
        args += [nxt[0], nxt[1].reshape(1, d)]
        out_specs, out_shape = [out_specs, _row_spec(d)], [out_shape, jax.ShapeDtypeStruct((b, t, d), BF16)]
    return pl.pallas_call(
        _post_ffn_kernel,
        grid=(b, t // ROW_TILE),
        in_specs=in_specs,
        out_specs=out_specs,
        out_shape=out_shape,
        compiler_params=ROW_PARAMS,
        name="post_ffn",
    )(*args)


def _moe_ffn(h2, logits, t_per_b, skip, li, w1, b1, w2, b2):
    t = logits.shape[0]
    d = h2.shape[1]
    n = t * TOP_K
    n_tiles = n // MOE_BLOCK
    top_v, top_e = lax.top_k(logits, TOP_K)
    gate_w = jax.nn.softmax(top_v, axis=-1)
    flat_e = top_e.reshape(n).astype(jnp.int32)
    iota = jnp.arange(n, dtype=jnp.int32)
    _, order, sw = lax.sort((flat_e, iota, gate_w.reshape(n)), num_keys=1, is_stable=True)
    _, inv = lax.sort((order, iota), num_keys=1)
    tok = order // TOP_K
    xs = h2[tok + (tok // t_per_b + 1) * skip]
    counts = jnp.sum((flat_e[:, None] == jnp.arange(N_EXPERTS, dtype=jnp.int32)[None, :]).astype(jnp.int32), axis=0)
    end = jnp.cumsum(counts)
    start = end - counts
    first_tile = start // MOE_BLOCK
    n_items = jnp.where(counts > 0, (end - 1) // MOE_BLOCK - first_tile + 1, 0)
    items_end = jnp.cumsum(n_items)
    w = jnp.arange(n_tiles + N_EXPERTS, dtype=jnp.int32)
    valid = w < items_end[-1]
    e_w = jnp.minimum(jnp.sum((w[:, None] >= items_end[None, :]).astype(jnp.int32), axis=1), N_EXPERTS - 1)
    tile_w = first_tile[e_w] + (w - (items_end[e_w] - n_items[e_w]))
    lo = jnp.where(valid, jnp.maximum(start[e_w], tile_w * MOE_BLOCK), 0)
    hi = jnp.where(valid, jnp.minimum(end[e_w], (tile_w + 1) * MOE_BLOCK), 0)
    tile_w = jnp.where(valid, tile_w, n_tiles - 1)
    first = jnp.concatenate([jnp.ones((1,), jnp.int32), (tile_w[1:] != tile_w[:-1]).astype(jnp.int32)])
    newexp = jnp.concatenate([jnp.ones((1,), jnp.int32), (e_w[1:] != e_w[:-1]).astype(jnp.int32)])
    ys = _moe_experts(xs, sw, tile_w.astype(jnp.int32), e_w, lo.astype(jnp.int32), hi.astype(jnp.int32), first, newexp, li, w1, b1, w2, b2)
    return ys[inv.reshape(t, TOP_K).T]


def kernel(x, c, ctx, c_ctx, w_mod, b_mod, norm_mix_pre, norm_mix_post, norm_ffn_pre, norm_ffn_post, w_in, ssm_conv_w, ssm_conv_b, ssm_dt_bias, ssm_a_log, ssm_d, ssm_norm, hy_conv_w, hy_conv_b, hy_w1, hy_b1, hy_w2, hy_b2, hy_w3, hy_b3, hy_freq, hy_bias, hg_lb_logits, hg_norm, w_br_ssm, w_br_hy, w_br_hg, w_out, router_w, router_b, exp_w1, exp_b1, exp_w2, exp_b2):
    hp = lax.Precision.HIGHEST
    b, n_lat, d = x.shape
    n_ctx = ctx.shape[1]
    lb = jax.nn.softmax(hg_lb_logits.astype(F32), axis=1)
    lb = jnp.cumsum(lb, axis=1) - lb[:, :1]
    silu_c = jax.nn.silu(c)
    silu_cc = jax.nn.silu(c_ctx)
    xs = jnp.concatenate([ctx, x], axis=1)
    all_mods = []
    for li in range(DEPTH):
        mx = (jnp.dot(silu_c, w_mod[li], precision=hp) + b_mod[li]).reshape(b, 1, 6, d)
        mc = jnp.broadcast_to((jnp.dot(silu_cc, w_mod[li], precision=hp) + b_mod[li]).reshape(1, 1, 6, d), (b, 1, 6, d))
        all_mods.append(jnp.pad(jnp.concatenate([mc, mx], axis=1), ((0, 0), (0, 0), (0, MOD_ROWS - 6), (0, 0))))
    h = _norm_mod(xs, norm_mix_pre[0], all_mods[0], n_ctx)
    for li in range(DEPTH):
        mods = all_mods[li]
        ym, yh_parts, og_parts, p_rm, col_g, col_gate = _mixer_branches(h, w_in[li], lb[:, li], n_ctx, ssm_conv_w[li], ssm_conv_b[li], ssm_dt_bias[li], ssm_a_log[li], ssm_d[li], ssm_norm[li], hy_conv_w[li], hy_conv_b[li], hy_w1[li], hy_b1[li], hy_w2[li], hy_b2[li], hy_w3[li], hy_b3[li], hy_freq[li], hy_bias[li], hg_norm[li])
        xs, h_ffn, logits = _merge(ym, yh_parts, og_parts, p_rm, col_g, col_gate, xs, mods, w_br_ssm[li], w_br_hy[li], w_br_hg[li], w_out[li], norm_mix_post[li], norm_ffn_pre[li], router_w[li], router_b[li], n_ctx)
        skip = n_ctx if li == DEPTH - 1 else 0
        t = xs.shape[1] - skip
        f4 = _moe_ffn(h_ffn.reshape(-1, d), logits[:, skip:, :N_EXPERTS].reshape(b * t, N_EXPERTS), t, skip, li, exp_w1, exp_b1[li], exp_w2, exp_b2[li])
        if li == DEPTH - 1:
            xs = _post_ffn(f4.reshape(TOP_K, b, t, d), xs, mods, norm_ffn_post[li], n_ctx, skip)
        else:
            xs, h = _post_ffn(f4.reshape(TOP_K, b, t, d), xs, mods, norm_ffn_post[li], n_ctx, skip, nxt=(all_mods[li + 1], norm_mix_pre[li + 1]))
    return xs
```

```python
import functools
import math

import jax
import jax.numpy as jnp
import numpy as np
from jax import lax
from jax.experimental import pallas as pl
from jax.experimental.pallas import tpu as pltpu

D_MODEL = 1024
DEPTH = 2
GRID_W = 64

SSM_HEADS = 16
SSM_HEAD_DIM = 64
SSM_INNER = SSM_HEADS * SSM_HEAD_DIM
SSM_STATE = 128
SSM_GROUPS = 4
SSD_CHUNK = 128
SSM_XBC = SSM_INNER + 2 * SSM_GROUPS * SSM_STATE

HY_WIDTH = D_MODEL
HY_ORDER = 2
HY_BANDS = 16
HY_FAST_DECAY = 0.3
HY_SLOW_DECAY = 1.5
HY_DECAY_TARGET = 1e-2

HG_HEADS = 8
HG_KDIM = 128
HG_VDIM = D_MODEL // HG_HEADS
HG_QK = HG_HEADS * HG_KDIM
HG_V = HG_HEADS * HG_VDIM
HG_CHUNK = 64
F_FLOOR = 1e-20

N_EXPERTS = 32
TOP_K = 4
D_FF = D_MODEL
SWIGLU_LIMIT = 7.0
SWIGLU_ALPHA = 1.702
MOE_BLOCK = 512

N_BRANCHES = 3
IN_SIZES = (SSM_INNER, SSM_XBC, 2 * SSM_HEADS, (HY_ORDER + 1) * HY_WIDTH, HG_QK, 2 * HG_QK, HG_V, HG_V, N_BRANCHES * D_MODEL)
EPS = 1e-6
F32 = jnp.float32
BF16 = jnp.bfloat16

LANES = 128
VMEM_LIMIT = 56 * 1024 * 1024


def _mm_kernel(a_ref, b_ref, o_ref):
    o_ref[...] = jnp.dot(a_ref[...], b_ref[...], preferred_element_type=F32).astype(o_ref.dtype)


def _mm(a, b, out_dtype=F32, tm=1024, tn=2048):
    m, k = a.shape
    n = b.shape[1]
    tm = math.gcd(m, tm)
    tn = math.gcd(n, tn)
    assert tm % SUBLANES == 0 and tn % LANES == 0, (m, n, tm, tn)
    return pl.pallas_call(
        _mm_kernel,
        grid=(n // tn, m // tm),
        in_specs=[pl.BlockSpec((tm, k), lambda j, i: (i, 0)), pl.BlockSpec((k, tn), lambda j, i: (0, j))],
        out_specs=pl.BlockSpec((tm, tn), lambda j, i: (i, j)),
        out_shape=jax.ShapeDtypeStruct((m, n), out_dtype),
        compiler_params=pltpu.CompilerParams(dimension_semantics=("arbitrary", "arbitrary"), vmem_limit_bytes=VMEM_LIMIT),
        name="dense_mm",
    )(a, b)


def _moe_kernel(tile_ref, exp_ref, lo_ref, hi_ref, first_ref, newexp_ref, x_ref, sw_ref, w1_ref, b1_ref, w2_ref, b2_ref, o_ref, w1b_ref, w2b_ref):
    del exp_ref
    w = pl.program_id(0)
    lo, hi = lo_ref[w], hi_ref[w]

    @pl.when(newexp_ref[w] == 1)
    def _():
        def cast_rows(i, carry):
            rows = pl.ds(pl.multiple_of(i * LANES, LANES), LANES)
            w1b_ref[rows, :] = w1_ref[rows, :].astype(BF16)
            w2b_ref[rows, :] = w2_ref[rows, :].astype(BF16)
            return carry

        lax.fori_loop(0, D_MODEL // LANES, cast_rows, 0)

    @pl.when(hi > lo)
    def _():
        hh = jnp.dot(x_ref[...], w1b_ref[...], preferred_element_type=F32) + b1_ref[...]
        g = jnp.minimum(hh[:, :D_FF], SWIGLU_LIMIT)
        u = jnp.clip(hh[:, D_FF:], -SWIGLU_LIMIT, SWIGLU_LIMIT)
        act = (u + 1.0) * g * jax.nn.sigmoid(SWIGLU_ALPHA * g)
        y = jnp.dot(act.astype(BF16), w2b_ref[...], preferred_element_type=F32) + b2_ref[...]
        y = (y * sw_ref[...]).astype(o_ref.dtype)
        rows = tile_ref[w] * MOE_BLOCK + lax.broadcasted_iota(jnp.int32, (MOE_BLOCK, 1), 0)
        mine = jnp.logical_and(rows >= lo, rows < hi)

        @pl.when(first_ref[w] == 1)
        def _():
            o_ref[...] = jnp.where(mine, y, jnp.zeros_like(y))

        @pl.when(first_ref[w] != 1)
        def _():
            o_ref[...] = jnp.where(mine, y, o_ref[...])


def _moe_experts(xs, sw, tile_w, exp_w, lo, hi, first, newexp, li, w1, b1, w2, b2):
    n, d = xs.shape
    grid_spec = pltpu.PrefetchScalarGridSpec(
        num_scalar_prefetch=6,
        grid=(tile_w.shape[0],),
        in_specs=[
            pl.BlockSpec((MOE_BLOCK, d), lambda w, tl, ex, lo_, hi_, fi, ne: (tl[w], 0)),
            pl.BlockSpec((MOE_BLOCK, 1), lambda w, tl, ex, lo_, hi_, fi, ne: (tl[w], 0)),
            pl.BlockSpec((None, None, d, 2 * D_FF), lambda w, tl, ex, lo_, hi_, fi, ne: (li, ex[w], 0, 0)),
            pl.BlockSpec((None, 1, 2 * D_FF), lambda w, tl, ex, lo_, hi_, fi, ne: (ex[w], 0, 0)),
            pl.BlockSpec((None, None, D_FF, d), lambda w, tl, ex, lo_, hi_, fi, ne: (li, ex[w], 0, 0)),
            pl.BlockSpec((None, 1, d), lambda w, tl, ex, lo_, hi_, fi, ne: (ex[w], 0, 0)),
        ],
        out_specs=pl.BlockSpec((MOE_BLOCK, d), lambda w, tl, ex, lo_, hi_, fi, ne: (tl[w], 0)),
        scratch_shapes=[pltpu.VMEM((d, 2 * D_FF), BF16), pltpu.VMEM((D_FF, d), BF16)],
    )
    return pl.pallas_call(
        _moe_kernel,
        grid_spec=grid_spec,
        out_shape=jax.ShapeDtypeStruct((n, d), BF16),
        compiler_params=pltpu.CompilerParams(dimension_semantics=("arbitrary",), vmem_limit_bytes=VMEM_LIMIT),
        name="moe_experts",
    )(tile_w, exp_w, lo, hi, first, newexp, xs, sw.reshape(n, 1), w1, b1.reshape(N_EXPERTS, 1, 2 * D_FF), w2, b2.reshape(N_EXPERTS, 1, d))


SUBLANES = 8
NEG_BIG = -1e30
HG_STEP_CHUNKS = 4
HIER_LEVELS = (64, 32, 16, 8, 4)
LOG2E = math.log2(math.e)


def _split3(x):
    h1 = x.astype(BF16)
    r1 = x - h1.astype(F32)
    h2 = r1.astype(BF16)
    h3 = (r1 - h2.astype(F32)).astype(BF16)
    return h1, h2, h3


def _dot_nt(a, b):
    return lax.dot_general(a, b, (((1,), (1,)), ((), ())), preferred_element_type=F32)


def _gla_kernel(q_ref, a_ref, v_ref, lb_ref, *rest, reverse):
    if reverse:
        of_ref, w_ref, o_ref, st_ref, at_ref = rest
    else:
        o_ref, st_ref, at_ref = rest
    Q = HG_CHUNK

    @pl.when(pl.program_id(1) == 0)
    def _():
        st_ref[...] = jnp.zeros_like(st_ref)

    row = lax.broadcasted_iota(jnp.int32, (Q, Q), 0)
    col = lax.broadcasted_iota(jnp.int32, (Q, Q), 1)
    tri = jnp.where((col >= row) if reverse else (col <= row), 1.0, 0.0).astype(BF16)
    same_block = {s: (row // s) == (col // s) for s in HIER_LEVELS[1:] + (2, 1)}
    rowk = lax.broadcasted_iota(jnp.int32, (Q, HG_KDIM), 0)
    q_rows = {s: ((rowk % s) < s // 2) if reverse else ((rowk % s) >= s // 2) for s in HIER_LEVELS + (2,)}
    of_w = (of_ref, w_ref) if reverse else (None, None)
    for sub in (range(HG_STEP_CHUNKS - 1, -1, -1) if reverse else range(HG_STEP_CHUNKS)):
        _gla_chunk(sub, q_ref, a_ref, v_ref, lb_ref[...], of_w, o_ref, st_ref, at_ref, tri, same_block, q_rows, reverse)


def _gla_chunk(sub, q_ref, a_ref, v_ref, lb, of_w, o_ref, st_ref, at_ref, tri, same_block, q_rows, reverse):
    Q = HG_CHUNK
    rs = slice(sub * Q, (sub + 1) * Q)
    of_ref, w_ref = of_w
    shp3 = (Q // SUBLANES, SUBLANES, HG_KDIM)
    sub3 = lax.broadcasted_iota(jnp.int32, shp3, 1)
    a = a_ref[rs, :].astype(F32)
    f_all = jnp.maximum(lb + (1.0 - lb) * jax.nn.sigmoid(a), F_FLOOR)
    kk = (1.0 - lb) * jax.nn.sigmoid(-a)
    g_all = sum(jnp.dot(tri, p, preferred_element_type=F32) for p in _split3(jnp.log(f_all) * LOG2E))
    q_all = q_ref[rs, :].astype(F32)
    q_all = q_all * jax.nn.sigmoid(q_all)
    v_all = v_ref[rs, :]
    tot = 0 if reverse else Q - 1

    def level_ref(g, s):
        half = s // 2
        m_off = half if reverse else half - 1
        if s >= 2 * SUBLANES:
            return jnp.concatenate([jnp.broadcast_to(g[b0 + m_off:b0 + m_off + 1, :], (s, HG_KDIM)) for b0 in range(0, Q, s)], axis=0)
        g3 = g.reshape(shp3)
        ref = jnp.broadcast_to(g3[:, m_off:m_off + 1, :], shp3)
        for b0 in range(s, SUBLANES, s):
            ref = jnp.where(sub3 >= b0, jnp.broadcast_to(g3[:, b0 + m_off:b0 + m_off + 1, :], shp3), ref)
        return ref.reshape(Q, HG_KDIM)

    for h in range(HG_HEADS):
        sl = slice(h * HG_KDIM, (h + 1) * HG_KDIM)
        g, qh, kh, fh = g_all[:, sl], q_all[:, sl], kk[:, sl], f_all[:, sl]
        attn = None
        for s in HIER_LEVELS:
            gref = level_ref(g, s)
            eq = jnp.exp2(jnp.where(q_rows[s], g - gref, NEG_BIG))
            ek = jnp.exp2(jnp.where(q_rows[s], NEG_BIG, gref - g))
            lvl = _dot_nt((qh * eq).astype(BF16), (kh * ek).astype(BF16))
            attn = lvl if attn is None else jnp.where(same_block[s], lvl, attn)
        lvl = _dot_nt(jnp.where(q_rows[2], qh * fh, 0.0).astype(BF16), jnp.where(q_rows[2], 0.0, kh).astype(BF16))
        attn = jnp.where(same_block[2], lvl, attn)
        attn = jnp.where(same_block[1], _dot_nt(qh.astype(BF16), kh.astype(BF16)), attn)
        at_ref[sub, h] = attn.astype(BF16)

    for h in range(HG_HEADS):
        sl = slice(h * HG_KDIM, (h + 1) * HG_KDIM)
        g, qh, kh, vb = g_all[:, sl], q_all[:, sl], kk[:, sl], v_all[:, sl]
        g_tot = g[tot:tot + 1, :]
        st = st_ref[h]
        o = _dot_nt((qh * jnp.exp2(g)).astype(BF16), st.astype(BF16))
        o = o + jnp.dot(at_ref[sub, h], vb, preferred_element_type=F32)
        k_st = (kh * jnp.exp2(g_tot - g)).astype(BF16)
        st_ref[h] = st * jnp.exp2(g_tot) + jnp.dot(vb.astype(F32).T.astype(BF16), k_st, preferred_element_type=F32)
        if reverse:
            o = o + of_ref[rs, sl]
            o = o * lax.rsqrt(jnp.mean(o * o, axis=-1, keepdims=True) + EPS) * w_ref[:, sl]
        o_ref[rs, sl] = o.astype(o_ref.dtype)


def _hgrn_scan(p_cm, lb, norm_w, n_ctx):
    b, t, _ = p_cm.shape
    col_q, col_f, col_i = 0, 1, 3
    step_rows = HG_STEP_CHUNKS * HG_CHUNK
    assert n_ctx % step_rows == 0 and t % step_rows == 0, (n_ctx, t, step_rows)
    nc, ncc = t // step_rows, n_ctx // step_rows
    blk = (None, step_rows, HG_QK)
    scratch = [pltpu.VMEM((HG_HEADS, HG_VDIM, HG_KDIM), F32), pltpu.VMEM((HG_STEP_CHUNKS, HG_HEADS, HG_CHUNK, HG_CHUNK), BF16)]
    params = pltpu.CompilerParams(dimension_semantics=("arbitrary", "arbitrary"), vmem_limit_bytes=VMEM_LIMIT)
    row_spec = pl.BlockSpec((1, HG_QK), lambda bi, s: (0, 0))

    def fwd_chunk(s):
        return s

    def bwd_chunk(s):
        return jnp.where(s < ncc, ncc - 1 - s, nc + ncc - 1 - s)

    o_f = pl.pallas_call(
        functools.partial(_gla_kernel, reverse=False),
        grid=(b, nc),
        in_specs=[
            pl.BlockSpec(blk, lambda bi, s: (bi, fwd_chunk(s), col_q)),
            pl.BlockSpec(blk, lambda bi, s: (bi, fwd_chunk(s), col_f)),
            pl.BlockSpec(blk, lambda bi, s: (bi, fwd_chunk(s), col_i)),
            row_spec,
        ],
        out_specs=pl.BlockSpec(blk, lambda bi, s: (bi, fwd_chunk(s), 0)),
        out_shape=jax.ShapeDtypeStruct((b, t, HG_V), F32),
        scratch_shapes=scratch,
        compiler_params=params,
        name="gla_fwd",
    )(p_cm, p_cm, p_cm, lb[0:1])
    return pl.pallas_call(
        functools.partial(_gla_kernel, reverse=True),
        grid=(b, nc),
        in_specs=[
            pl.BlockSpec(blk, lambda bi, s: (bi, bwd_chunk(s), col_q)),
            pl.BlockSpec(blk, lambda bi, s: (bi, bwd_chunk(s), col_f + 1)),
            pl.BlockSpec(blk, lambda bi, s: (bi, bwd_chunk(s), col_i)),
            row_spec,
            pl.BlockSpec(blk, lambda bi, s: (bi, bwd_chunk(s), 0)),
            row_spec,
        ],
        out_specs=pl.BlockSpec(blk, lambda bi, s: (bi, bwd_chunk(s), 0)),
        out_shape=jax.ShapeDtypeStruct((b, t, HG_V), BF16),
        scratch_shapes=scratch,
        compiler_params=params,
        name="gla_bwd",
    )(p_cm, p_cm, p_cm, lb[1:2], o_f, norm_w.reshape(1, HG_V))


CONV_TILE = 256
HALO = 16


def _dwconv_kernel(prev_ref, cur_ref, next_ref, w_ref, b_ref, *o_refs, taps, n_ctx_tiles, n_tiles, silu):
    i = pl.program_id(2)
    first = jnp.logical_or(i == 0, i == n_ctx_tiles)
    last = jnp.logical_or(i == n_ctx_tiles - 1, i == n_tiles - 1)
    pad = taps // 2
    xp = jnp.where(first, 0.0, prev_ref[...].astype(F32))
    xn = jnp.where(last, 0.0, next_ref[...].astype(F32))
    xcat = jnp.concatenate([xp, cur_ref[...].astype(F32), xn], axis=0)
    acc = jnp.broadcast_to(b_ref[...], cur_ref.shape).astype(F32)
    for k in range(taps):
        off = HALO - pad + k
        acc = acc + w_ref[k:k + 1, :] * xcat[off:off + CONV_TILE, :]
    if silu:
        acc = acc * jax.nn.sigmoid(acc)
    if len(o_refs) == 1:
        o_refs[0][...] = acc.astype(o_refs[0].dtype)
    else:
        ctx_ref, lat_ref = o_refs

        @pl.when(i < n_ctx_tiles)
        def _():
            ctx_ref[...] = acc.astype(ctx_ref.dtype)

        @pl.when(i >= n_ctx_tiles)
        def _():
            lat_ref[...] = acc.astype(lat_ref.dtype)


def _dwconv_stream(x, w, bias, n_ctx, silu, c0=0, ct=1024, split=False):
    b, t, _ = x.shape
    taps, c = w.shape
    n_tiles = t // CONV_TILE
    nct = n_ctx // CONV_TILE
    hb = CONV_TILE // HALO
    n_halo = t // HALO
    j0 = c0 // ct
    kern = functools.partial(_dwconv_kernel, taps=taps, n_ctx_tiles=nct, n_tiles=n_tiles, silu=silu)
    if split:
        out_specs = [pl.BlockSpec((None, CONV_TILE, ct), lambda bi, j, i: (bi, jnp.minimum(i, nct - 1), j)),
                     pl.BlockSpec((None, CONV_TILE, ct), lambda bi, j, i: (bi, jnp.maximum(i - nct, 0), j))]
        out_shape = [jax.ShapeDtypeStruct((b, n_ctx, c), BF16), jax.ShapeDtypeStruct((b, t - n_ctx, c), BF16)]
    else:
        out_specs = pl.BlockSpec((None, CONV_TILE, ct), lambda bi, j, i: (bi, i, j))
        out_shape = jax.ShapeDtypeStruct((b, t, c), BF16)
    return pl.pallas_call(
        kern,
        grid=(b, c // ct, n_tiles),
        in_specs=[
            pl.BlockSpec((None, HALO, ct), lambda bi, j, i: (bi, jnp.maximum(i * hb - 1, 0), j0 + j)),
            pl.BlockSpec((None, CONV_TILE, ct), lambda bi, j, i: (bi, i, j0 + j)),
            pl.BlockSpec((None, HALO, ct), lambda bi, j, i: (bi, jnp.minimum((i + 1) * hb, n_halo - 1), j0 + j)),
            pl.BlockSpec((taps, ct), lambda bi, j, i: (0, j)),
            pl.BlockSpec((1, ct), lambda bi, j, i: (0, j)),
        ],
        out_specs=out_specs,
        out_shape=out_shape,
        compiler_params=pltpu.CompilerParams(dimension_semantics=("arbitrary",) * 3, vmem_limit_bytes=VMEM_LIMIT),
        name="dwconv",
    )(x, x, x, w.astype(F32), bias.reshape(1, c).astype(F32))


SSD_STEP_CHUNKS = 2
SSM_GHEADS = SSM_HEADS // SSM_GROUPS
SSM_GP = SSM_GHEADS * SSM_HEAD_DIM


def _ssd_kernel(xbc_ref, dt_ref, dtb_ref, a_ref, *rest, reverse):
    if reverse:
        yf_ref, z_ref, dsk_ref, nw_ref, o_ref, st_ref, m_ref, xd_ref, xst_ref, y_ref = rest
    else:
        o_ref, st_ref, m_ref, xd_ref, xst_ref, y_ref = rest
    Q = SSD_CHUNK

    @pl.when(pl.program_id(1) == 0)
    def _():
        st_ref[...] = jnp.zeros_like(st_ref)

    row = lax.broadcasted_iota(jnp.int32, (Q, Q), 0)
    col = lax.broadcasted_iota(jnp.int32, (Q, Q), 1)
    keep = (col >= row) if reverse else (col <= row)
    tri = jnp.where(keep, 1.0, 0.0).astype(BF16)
    expand = jnp.where(lax.broadcasted_iota(jnp.int32, (LANES, SSM_INNER), 1) // SSM_HEAD_DIM == lax.broadcasted_iota(jnp.int32, (LANES, SSM_INNER), 0), 1.0, 0.0).astype(BF16)

    rev_refs = (yf_ref, z_ref, dsk_ref, nw_ref) if reverse else None
    for sub in (range(SSD_STEP_CHUNKS - 1, -1, -1) if reverse else range(SSD_STEP_CHUNKS)):
        _ssd_chunk(sub, xbc_ref, dt_ref, dtb_ref[...], a_ref[...], rev_refs, o_ref, st_ref, m_ref, xd_ref, xst_ref, y_ref, keep, tri, expand, reverse)


def _ssd_chunk(sub, xbc_ref, dt_ref, dt_bias, a_neg, rev_refs, o_ref, st_ref, m_ref, xd_ref, xst_ref, y_ref, keep, tri, expand, reverse):
    Q = SSD_CHUNK
    rs = slice(sub * Q, (sub + 1) * Q)
    dt = jax.nn.softplus(dt_ref[rs, :] + dt_bias)
    a = dt * a_neg
    cs = sum(jnp.dot(tri, p, preferred_element_type=F32) for p in _split3(a))
    tot = 0 if reverse else Q - 1
    cs_tot = cs[tot:tot + 1, :]
    cs_t = cs.T
    dt_e = jnp.dot(dt.astype(BF16), expand, preferred_element_type=F32)
    e_in = jnp.dot(jnp.exp(cs).astype(BF16), expand, preferred_element_type=F32)
    e_st = jnp.dot(jnp.exp(cs_tot - cs).astype(BF16), expand, preferred_element_type=F32)
    e_tot = jnp.dot(jnp.broadcast_to(jnp.exp(cs_tot), (SUBLANES, LANES)).astype(BF16), expand, preferred_element_type=F32)[0:1, :]

    xs = xbc_ref[rs, :SSM_INNER].astype(F32)
    xd_ref[sub] = (xs * dt_e).astype(BF16)
    xst_ref[sub] = (xs * dt_e * e_st).astype(BF16)
    for g in range(SSM_GROUPS):
        bm = xbc_ref[rs, SSM_INNER + g * SSM_STATE:SSM_INNER + (g + 1) * SSM_STATE]
        cm = xbc_ref[rs, SSM_INNER + (SSM_GROUPS + g) * SSM_STATE:SSM_INNER + (SSM_GROUPS + g + 1) * SSM_STATE]
        cb = _dot_nt(cm, bm)
        for h in range(g * SSM_GHEADS, (g + 1) * SSM_GHEADS):
            diff = jnp.broadcast_to(cs[:, h:h + 1], (Q, Q)) - jnp.broadcast_to(cs_t[h:h + 1, :], (Q, Q))
            m_ref[sub, h] = (cb * jnp.exp(jnp.where(keep, diff, NEG_BIG))).astype(BF16)

    lane_lo = lax.broadcasted_iota(jnp.int32, (Q, LANES), 1) < SSM_HEAD_DIM
    for g in range(SSM_GROUPS):
        bm = xbc_ref[rs, SSM_INNER + g * SSM_STATE:SSM_INNER + (g + 1) * SSM_STATE]
        cm = xbc_ref[rs, SSM_INNER + (SSM_GROUPS + g) * SSM_STATE:SSM_INNER + (SSM_GROUPS + g + 1) * SSM_STATE]
        gl = slice(g * SSM_GP, (g + 1) * SSM_GP)
        st = st_ref[g]
        y_off = jnp.dot(cm, st.astype(BF16), preferred_element_type=F32) * e_in[:, gl]
        for hp in range(SSM_GHEADS // 2):
            h0 = g * SSM_GHEADS + 2 * hp
            lanes = slice((h0 // 2) * LANES, (h0 // 2 + 1) * LANES)
            pair = [jnp.dot(m_ref[sub, h], xd_ref[sub, :, lanes], preferred_element_type=F32) for h in (h0, h0 + 1)]
            y_ref[sub, :, lanes] = jnp.where(lane_lo, pair[0], pair[1]) + y_off[:, hp * LANES:(hp + 1) * LANES]
        st_ref[g] = st * e_tot[:, gl] + jnp.dot(bm.astype(F32).T.astype(BF16), xst_ref[sub, :, gl], preferred_element_type=F32)
    y = y_ref[sub]
    if reverse:
        yf_ref, z_ref, dsk_ref, nw_ref = rev_refs
        y = (y + yf_ref[rs, :] + xbc_ref[rs, :SSM_INNER].astype(F32) * dsk_ref[...])
        zz = z_ref[rs, :].astype(F32)
        y = y * (zz * jax.nn.sigmoid(zz))
        y = y * lax.rsqrt(jnp.mean(y * y, axis=-1, keepdims=True) + EPS) * nw_ref[...]
    o_ref[rs, :] = y.astype(o_ref.dtype)


def _ssd_scan(xbc_act, dt2, p_rm, z_col, dt_bias, a_log, d_skip, norm_w, n_ctx):
    b, t, _ = xbc_act.shape
    step_rows = SSD_STEP_CHUNKS * SSD_CHUNK
    assert n_ctx % step_rows == 0 and t % step_rows == 0, (n_ctx, t, step_rows)
    nc, ncc = t // step_rows, n_ctx // step_rows

    def pad_heads(v):
        return jnp.pad(v.astype(F32), ((0, 0), (0, LANES - SSM_HEADS)))

    dtb = pad_heads(dt_bias.reshape(2, SSM_HEADS))
    a_neg = pad_heads(-jnp.exp(a_log.astype(F32)))
    dsk = jnp.repeat(d_skip.astype(F32), SSM_HEAD_DIM).reshape(1, SSM_INNER)
    scratch = [pltpu.VMEM((SSM_GROUPS, SSM_STATE, SSM_GP), F32), pltpu.VMEM((SSD_STEP_CHUNKS, SSM_HEADS, SSD_CHUNK, SSD_CHUNK), BF16),
               pltpu.VMEM((SSD_STEP_CHUNKS, SSD_CHUNK, SSM_INNER), BF16), pltpu.VMEM((SSD_STEP_CHUNKS, SSD_CHUNK, SSM_INNER), BF16),
               pltpu.VMEM((SSD_STEP_CHUNKS, SSD_CHUNK, SSM_INNER), F32)]
    params = pltpu.CompilerParams(dimension_semantics=("arbitrary", "arbitrary"), vmem_limit_bytes=VMEM_LIMIT)

    def bwd_chunk(s):
        return jnp.where(s < ncc, ncc - 1 - s, nc + ncc - 1 - s)

    def specs(chunk, d):
        return [
            pl.BlockSpec((None, step_rows, SSM_XBC), lambda bi, s: (bi, chunk(s), 0)),
            pl.BlockSpec((None, step_rows, LANES), lambda bi, s: (bi, chunk(s), d)),
            pl.BlockSpec((1, LANES), lambda bi, s: (0, 0)),
            pl.BlockSpec((1, LANES), lambda bi, s: (0, 0)),
        ]

    def inner_spec(chunk):
        return pl.BlockSpec((None, step_rows, SSM_INNER), lambda bi, s: (bi, chunk(s), 0))

    row_spec = pl.BlockSpec((1, SSM_INNER), lambda bi, s: (0, 0))
    z_spec = pl.BlockSpec((None, step_rows, SSM_INNER), lambda bi, s: (bi, bwd_chunk(s), z_col // SSM_INNER))
    y_f = pl.pallas_call(
        functools.partial(_ssd_kernel, reverse=False),
        grid=(b, nc),
        in_specs=specs(lambda s: s, 0),
        out_specs=inner_spec(lambda s: s),
        out_shape=jax.ShapeDtypeStruct((b, t, SSM_INNER), F32),
        scratch_shapes=scratch,
        compiler_params=params,
        name="ssd_fwd",
    )(xbc_act, dt2, dtb[0:1], a_neg[0:1])
    return pl.pallas_call(
        functools.partial(_ssd_kernel, reverse=True),
        grid=(b, nc),
        in_specs=specs(bwd_chunk, 1) + [inner_spec(bwd_chunk), z_spec, row_spec, row_spec],
        out_specs=inner_spec(bwd_chunk),
        out_shape=jax.ShapeDtypeStruct((b, t, SSM_INNER), BF16),
        scratch_shapes=scratch,
        compiler_params=params,
        name="ssd_bwd",
    )(xbc_act, dt2, dtb[1:2], a_neg[1:2], y_f, p_rm, dsk, norm_w.reshape(1, SSM_INNER).astype(F32))


HY_N2 = LANES
HY_CT = LANES
VMEM_LIMIT_HYENA = 60 * 1024 * 1024


def _hy_dims(L):
    n1 = 2 * L // HY_N2
    k1n = n1 // 2 + 1
    k1p = -(-k1n // SUBLANES) * SUBLANES
    return n1, k1n, k1p


def _hy_tables(L, n1_rows):
    n1, k1n, k1p = _hy_dims(L)
    n = 2 * L
    k1 = np.arange(k1n, dtype=np.float64)[None, :, None]
    nn = (HY_N2 * np.arange(n1_rows, dtype=np.float64)[None, None, :] + np.arange(HY_N2, dtype=np.float64)[:, None, None])
    ang = 2.0 * np.pi * ((k1 * nn) % n) / n
    m1 = np.zeros((HY_N2, 2 * k1p, n1_rows), np.float32)
    m1[:, :k1n] = np.cos(ang)
    m1[:, k1p:k1p + k1n] = -np.sin(ang)
    m4 = np.transpose(m1, (0, 2, 1))
    kk = np.arange(HY_N2, dtype=np.float64)
    a2 = 2.0 * np.pi * ((kk[:, None] * kk[None, :]) % HY_N2) / HY_N2
    c, s = np.cos(a2), np.sin(a2)
    f3 = np.block([[c, s], [-s, c]]).astype(np.float32)
    f3i = np.block([[c, -s], [s, c]]).astype(np.float32)
    return jnp.asarray(m1, BF16), jnp.asarray(m4, BF16), jnp.asarray(f3, BF16), jnp.asarray(f3i, BF16)


def _hy_stage1(u_refs, a_refs, m1_ref, n1_rows, k1p):
    def body(n2, carry):
        xs = jnp.concatenate([u_ref[pl.ds(n2, n1_rows, stride=HY_N2), :] for u_ref in u_refs], axis=1).astype(BF16)
        a = jnp.dot(m1_ref[n2], xs, preferred_element_type=F32)
        for i, a_ref in enumerate(a_refs):
            a_ref[pl.ds(pl.multiple_of(n2 * 2 * k1p, 2 * k1p), 2 * k1p), :] = a[:, i * HY_CT:(i + 1) * HY_CT]
        return carry

    lax.fori_loop(0, HY_N2, body, 0, unroll=16)


def _hy_spectrum_slab(a_refs, f3_ref, k1s, k1p):
    blk = jnp.concatenate([jnp.concatenate([a_ref[pl.ds(k1, HY_N2, stride=2 * k1p), :], a_ref[pl.ds(k1p + k1, HY_N2, stride=2 * k1p), :]], axis=0) for k1 in k1s for a_ref in a_refs], axis=1)
    return jnp.dot(f3_ref[...], blk.astype(BF16), preferred_element_type=F32)


def _hy_k1_loop(body, k1n):
    def pair(p, carry):
        body((2 * p, 2 * p + 1))
        return carry

    lax.fori_loop(0, (k1n - 1) // 2, pair, 0, unroll=math.gcd((k1n - 1) // 2, 8))
    body((k1n - 1,))


def _hy_slab_rows(k1):
    start = k1 * 2 * HY_N2
    return pl.ds(start if isinstance(k1, int) else pl.multiple_of(start, 2 * HY_N2), 2 * HY_N2)


def _hy_conv(u_ref, yo_ref, a_ref, y_ref, h_ref, order, m1_ref, m4_ref, f3_ref, f3i_ref, n1_rows, k1n, k1p):
    _hy_stage1((u_ref,), (a_ref,), m1_ref, n1_rows, k1p)

    def stage2(k1s):
        x = _hy_spectrum_slab((a_ref,), f3_ref, k1s, k1p)
        h = jnp.concatenate([h_ref[order, _hy_slab_rows(k1), :] for k1 in k1s], axis=1).astype(F32)
        xr, xi, hr, hi = x[:HY_N2], x[HY_N2:], h[:HY_N2], h[HY_N2:]
        z = jnp.concatenate([xr * hr - xi * hi, xr * hi + xi * hr], axis=0).astype(BF16)
        c = jnp.dot(f3i_ref[...], z, preferred_element_type=F32)
        for i, k1 in enumerate(k1s):
            lanes = slice(i * HY_CT, (i + 1) * HY_CT)
            y_ref[pl.ds(k1, HY_N2, stride=2 * k1p), :] = c[:HY_N2, lanes]
            y_ref[pl.ds(k1p + k1, HY_N2, stride=2 * k1p), :] = c[HY_N2:, lanes]

    _hy_k1_loop(stage2, k1n)

    def stage3(n2, carry):
        d = y_ref[pl.ds(pl.multiple_of(n2 * 2 * k1p, 2 * k1p), 2 * k1p), :].astype(BF16)
        yo_ref[pl.ds(n2, n1_rows, stride=HY_N2), :] = jnp.dot(m4_ref[n2], d, preferred_element_type=F32)
        return carry

    lax.fori_loop(0, HY_N2, stage3, 0, unroll=16)


def _hyena_kernel(v_ref, x1_ref, x2_ref, h_ref, m1_ref, m4_ref, f3_ref, f3i_ref, bias_ref, o_ref, a_ref, y_ref, u_ref, yo_ref, *, n1_rows, k1n, k1p):
    @pl.when(jnp.logical_and(pl.program_id(0) == 0, pl.program_id(1) == 0))
    def _():
        y_ref[...] = jnp.zeros_like(y_ref)

    u_ref[...] = v_ref[...].astype(F32)
    for order, gate_ref in enumerate((x1_ref, x2_ref)):
        _hy_conv(u_ref, yo_ref, a_ref, y_ref, h_ref, order, m1_ref, m4_ref, f3_ref, f3i_ref, n1_rows, k1n, k1p)
        z = gate_ref[...].astype(F32) * (yo_ref[...] + u_ref[...] * bias_ref[order:order + 1, :])
        if order == 0:
            u_ref[...] = z
        else:
            o_ref[...] = z.astype(o_ref.dtype)


def _hy_filter_kernel(hid_ref, wf_ref, wb_ref, bf_ref, bb_ref, dl_ref, m1_ref, f3_ref, o_ref, af_ref, ab_ref, uf_ref, ub_ref, *, n1_rows, k1n, k1p, scale_mid, scale_edge):
    L = uf_ref.shape[0]
    hid_rows = math.gcd(L, 2 * HY_N2)

    def fill(w_ref, b_ref, dst_ref, drop_first):
        w = _split3(w_ref[...])

        def body(i, energy):
            rows = pl.ds(pl.multiple_of(i * hid_rows, hid_rows), hid_rows)
            hid = _split3(hid_ref[rows, :])
            acc = sum(jnp.dot(hid[p], w[q], preferred_element_type=F32) for p in range(2) for q in range(2 - p))
            pos = i * hid_rows + lax.broadcasted_iota(jnp.int32, (hid_rows, HY_CT), 0)
            vals = (acc + b_ref[...]) * jnp.exp(pos.astype(F32) * (-1.0 / (L - 1)) * dl_ref[...])
            if drop_first:
                vals = jnp.where(pos == 0, 0.0, vals)
            dst_ref[rows, :] = vals
            return energy + jnp.sum(vals * vals, axis=0, keepdims=True)

        return lax.fori_loop(0, L // hid_rows, body, jnp.zeros((1, HY_CT), F32), unroll=math.gcd(L // hid_rows, 4))

    norm = lax.rsqrt(fill(wf_ref, bf_ref, uf_ref, False) + fill(wb_ref, bb_ref, ub_ref, True) + EPS)

    _hy_stage1((uf_ref, ub_ref), (af_ref, ab_ref), m1_ref, n1_rows, k1p)

    def combine(k1s):
        x_all = _hy_spectrum_slab((af_ref, ab_ref), f3_ref, k1s, k1p)
        for i, k1 in enumerate(k1s):
            xf = x_all[:, 2 * i * HY_CT:(2 * i + 1) * HY_CT]
            xb = x_all[:, (2 * i + 1) * HY_CT:(2 * i + 2) * HY_CT]
            w = norm * jnp.where(jnp.logical_or(k1 == 0, k1 == k1n - 1), scale_edge, scale_mid)
            h = jnp.concatenate([xf[:HY_N2] + xb[:HY_N2], xf[HY_N2:] - xb[HY_N2:]], axis=0)
            o_ref[_hy_slab_rows(k1), :] = (h * w).astype(o_ref.dtype)

    _hy_k1_loop(combine, k1n)


def _single(block_shape, index_map):
    return pl.BlockSpec(block_shape, index_map, pipeline_mode=pl.Buffered(1))


def _hyena_filter_spectrum_pallas(hidden, w3, b3, deltas):
    L, ffn = hidden.shape
    c = deltas.shape[0]
    nct = c // HY_CT
    n = 2 * L
    n1, k1n, k1p = _hy_dims(L)
    n1_rows = L // HY_N2
    m1, _, f3, _ = _hy_tables(L, n1_rows)
    kern = functools.partial(_hy_filter_kernel, n1_rows=n1_rows, k1n=k1n, k1p=k1p, scale_mid=2.0 / n, scale_edge=1.0 / n)
    return pl.pallas_call(
        kern,
        grid=(HY_ORDER, nct),
        in_specs=[
            _single((L, ffn), lambda o, j: (0, 0)),
            pl.BlockSpec((ffn, HY_CT), lambda o, j: (0, o * nct + j)),
            pl.BlockSpec((ffn, HY_CT), lambda o, j: (0, (HY_ORDER + o) * nct + j)),
            pl.BlockSpec((1, HY_CT), lambda o, j: (0, o * nct + j)),
            pl.BlockSpec((1, HY_CT), lambda o, j: (0, (HY_ORDER + o) * nct + j)),
            pl.BlockSpec((1, HY_CT), lambda o, j: (0, j)),
            _single((HY_N2, 2 * k1p, n1_rows), lambda o, j: (0, 0, 0)),
            _single((2 * HY_N2, 2 * HY_N2), lambda o, j: (0, 0)),
        ],
        out_specs=pl.BlockSpec((None, k1n * 2 * HY_N2, HY_CT), lambda o, j: (o, 0, j)),
        out_shape=jax.ShapeDtypeStruct((HY_ORDER, k1n * 2 * HY_N2, c), BF16),
        scratch_shapes=[
            pltpu.VMEM((k1p * 2 * HY_N2, HY_CT), F32),
            pltpu.VMEM((k1p * 2 * HY_N2, HY_CT), F32),
            pltpu.VMEM((L, HY_CT), F32),
            pltpu.VMEM((L, HY_CT), F32),
        ],
        compiler_params=pltpu.CompilerParams(dimension_semantics=("arbitrary", "arbitrary"), vmem_limit_bytes=VMEM_LIMIT_HYENA),
        name="hyena_filter_dft",
    )(hidden, w3, w3, b3.reshape(1, -1), b3.reshape(1, -1), deltas.reshape(1, c), m1, f3)


def _hyena_long(hy, h_spec, bias):
    b, L, c3 = hy.shape
    c = c3 // (HY_ORDER + 1)
    nct = c // HY_CT
    n1, k1n, k1p = _hy_dims(L)
    n1_rows = L // HY_N2
    m1, m4, f3, f3i = _hy_tables(L, n1_rows)
    kern = functools.partial(_hyena_kernel, n1_rows=n1_rows, k1n=k1n, k1p=k1p)

    def col(part):
        return _single((None, L, HY_CT), lambda j, bi: (bi, 0, part * nct + j))

    return pl.pallas_call(
        kern,
        grid=(nct, b),
        in_specs=[
            col(0), col(1), col(2),
            _single((HY_ORDER, k1n * 2 * HY_N2, HY_CT), lambda j, bi: (0, 0, j)),
            _single((HY_N2, 2 * k1p, n1_rows), lambda j, bi: (0, 0, 0)),
            _single((HY_N2, n1_rows, 2 * k1p), lambda j, bi: (0, 0, 0)),
            _single((2 * HY_N2, 2 * HY_N2), lambda j, bi: (0, 0)),
            _single((2 * HY_N2, 2 * HY_N2), lambda j, bi: (0, 0)),
            pl.BlockSpec((HY_ORDER, HY_CT), lambda j, bi: (0, j)),
        ],
        out_specs=pl.BlockSpec((None, L, HY_CT), lambda j, bi: (bi, 0, j)),
        out_shape=jax.ShapeDtypeStruct((b, L, c), BF16),
        scratch_shapes=[
            pltpu.VMEM((k1p * 2 * HY_N2, HY_CT), F32),
            pltpu.VMEM((HY_N2 * 2 * k1p, HY_CT), F32),
            pltpu.VMEM((L, HY_CT), F32),
            pltpu.VMEM((L, HY_CT), F32),
        ],
        compiler_params=pltpu.CompilerParams(dimension_semantics=("arbitrary", "arbitrary"), vmem_limit_bytes=VMEM_LIMIT_HYENA),
        name="hyena_long_conv",
    )(hy, hy, hy, h_spec, m1, m4, f3, f3i, bias.astype(F32))


HY_CTX_CT = 256


def _hy_ctx_tables(L):
    n = 2 * L
    kb = L + 1
    kp = -(-kb // LANES) * LANES
    ang = 2.0 * np.pi * ((np.arange(kb, dtype=np.float64)[:, None] * np.arange(L, dtype=np.float64)[None, :]) % n) / n
    fwd = np.zeros((2 * kp, L), np.float32)
    fwd[:kb] = np.cos(ang)
    fwd[kp:kp + kb] = -np.sin(ang)
    return jnp.asarray(fwd, BF16), jnp.asarray(fwd.T, BF16), kb, kp


def _hy_ctx_filter_kernel(fwd_ref, bwd_ref, f_ref, o_ref, *, kb, kp, n):
    fwd = fwd_ref[...]
    bwd = jnp.where(lax.broadcasted_iota(jnp.int32, fwd.shape, 0) == 0, 0.0, bwd_ref[...])
    norm = lax.rsqrt(jnp.sum(fwd * fwd, axis=0, keepdims=True) + jnp.sum(bwd * bwd, axis=0, keepdims=True) + EPS)
    hf = jnp.dot(f_ref[...], fwd.astype(BF16), preferred_element_type=F32)
    hb = jnp.dot(f_ref[...], bwd.astype(BF16), preferred_element_type=F32)
    row = lax.broadcasted_iota(jnp.int32, hf.shape, 0)
    imag = row >= kp
    k = jnp.where(imag, row - kp, row)
    wk = jnp.where(jnp.logical_or(k == 0, k == kb - 1), 1.0 / n, 2.0 / n)
    o_ref[...] = (hf + jnp.where(imag, -hb, hb)) * (wk * norm)


def _hy_ctx_kernel(v_ref, x1_ref, x2_ref, h_ref, f_ref, g_ref, bias_ref, o_ref, *, kp):
    u = v_ref[...].astype(F32)
    for order, gate_ref in enumerate((x1_ref, x2_ref)):
        x = jnp.dot(f_ref[...], u.astype(BF16), preferred_element_type=F32)
        h = h_ref[order]
        xr, xi, hr, hi = x[:kp], x[kp:], h[:kp], h[kp:]
        z = jnp.concatenate([xr * hr - xi * hi, xr * hi + xi * hr], axis=0).astype(BF16)
        y = jnp.dot(g_ref[...], z, preferred_element_type=F32)
        u = gate_ref[...].astype(F32) * (y + u * bias_ref[order:order + 1, :])
    o_ref[...] = u.astype(o_ref.dtype)


def _hyena_ctx(hy, taps, bias):
    b, L, _ = hy.shape
    c = hy.shape[2] // (HY_ORDER + 1)
    ct = HY_CTX_CT
    nct = c // ct
    f_mat, g_mat, kb, kp = _hy_ctx_tables(L)
    params = pltpu.CompilerParams(dimension_semantics=("arbitrary", "arbitrary"), vmem_limit_bytes=VMEM_LIMIT)
    h_spec = pl.pallas_call(
        functools.partial(_hy_ctx_filter_kernel, kb=kb, kp=kp, n=2 * L),
        grid=(HY_ORDER, nct),
        in_specs=[
            pl.BlockSpec((L, ct), lambda o, j: (0, o * nct + j)),
            pl.BlockSpec((L, ct), lambda o, j: (0, (HY_ORDER + o) * nct + j)),
            pl.BlockSpec((2 * kp, L), lambda o, j: (0, 0)),
        ],
        out_specs=pl.BlockSpec((None, 2 * kp, ct), lambda o, j: (o, 0, j)),
        out_shape=jax.ShapeDtypeStruct((HY_ORDER, 2 * kp, c), F32),
        compiler_params=params,
        name="hyena_ctx_filter",
    )(taps, taps, f_mat)

    def col(part):
        return pl.BlockSpec((None, L, ct), lambda j, bi: (bi, 0, part * nct + j))

    return pl.pallas_call(
        functools.partial(_hy_ctx_kernel, kp=kp),
        grid=(nct, b),
        in_specs=[
            col(0), col(1), col(2),
            pl.BlockSpec((HY_ORDER, 2 * kp, ct), lambda j, bi: (0, 0, j)),
            pl.BlockSpec((2 * kp, L), lambda j, bi: (0, 0)),
            pl.BlockSpec((L, 2 * kp), lambda j, bi: (0, 0)),
            pl.BlockSpec((HY_ORDER, ct), lambda j, bi: (0, j)),
        ],
        out_specs=pl.BlockSpec((None, L, ct), lambda j, bi: (bi, 0, j)),
        out_shape=jax.ShapeDtypeStruct((b, L, c), BF16),
        compiler_params=params,
        name="hyena_ctx_conv",
    )(hy, hy, hy, h_spec, f_mat, g_mat, bias.astype(F32))


def _split_cols(t, sizes):
    return jnp.split(t, np.cumsum(sizes)[:-1].tolist(), axis=-1)


def _to_col_major(t, rows):
    b, rest = t.shape[0], t.shape[2:]
    return jnp.swapaxes(t.reshape((b, rows, GRID_W) + rest), 1, 2).reshape((b, rows * GRID_W) + rest)


def _from_col_major(t, rows):
    b, rest = t.shape[0], t.shape[2:]
    return jnp.swapaxes(t.reshape((b, GRID_W, rows) + rest), 1, 2).reshape((b, rows * GRID_W) + rest)


def _hyena_filter_hidden(L, w1, b1, w2, b2, freq):
    hp = lax.Precision.HIGHEST
    t = jnp.linspace(0.0, 1.0, L, dtype=F32)[:, None]
    w = 2.0 * math.pi * jnp.arange(L, dtype=F32)[:, None] / L
    bands = jnp.linspace(1e-4, HY_BANDS - 1, HY_BANDS, dtype=F32)
    feats = jnp.concatenate([t, jnp.cos(bands * w), -jnp.sin(bands * w)], axis=-1)
    h = jnp.sin(freq[0] * (jnp.dot(feats, w1, precision=hp) + b1))
    return jnp.sin(freq[1] * (jnp.dot(h, w2, precision=hp) + b2))


def _hyena_decay_rates():
    max_decay = math.log(HY_DECAY_TARGET) / HY_FAST_DECAY
    min_decay = math.log(HY_DECAY_TARGET) / HY_SLOW_DECAY
    return jnp.abs(jnp.linspace(min_decay, max_decay, HY_WIDTH, dtype=F32))


def _hyena_filter_taps(L, w1, b1, w2, b2, w3, b3, freq):
    h = jnp.dot(_hyena_filter_hidden(L, w1, b1, w2, b2, freq), w3, precision=lax.Precision.HIGHEST) + b3
    t = jnp.linspace(0.0, 1.0, L, dtype=F32)[:, None]
    return h * jnp.tile(jnp.exp(-t * _hyena_decay_rates()), (1, 2 * HY_ORDER))


def _mixer_branches(h_rm, w_in, lb, n_ctx, ssm_conv_w, ssm_conv_b, ssm_dt_bias, ssm_a_log, ssm_d, ssm_norm, hy_conv_w, hy_conv_b, hy_w1, hy_b1, hy_w2, hy_b2, hy_w3, hy_b3, hy_freq, hy_bias, hg_norm):
    b, t, d = h_rm.shape
    n_lat = t - n_ctx
    rows = n_lat // GRID_W
    w_z, w_xbc, w_dt, w_hy, w_q, w_f, w_i, w_g, w_gate = _split_cols(w_in, IN_SIZES)
    h2 = h_rm.reshape(b * t, d)
    rm_parts = (w_z, w_xbc, w_hy, w_g, w_gate)
    col_z, col_xbc, col_hy, col_g, col_gate = np.cumsum([0] + [w.shape[1] for w in rm_parts[:-1]]).tolist()
    p_rm = _mm(h2, jnp.concatenate(rm_parts, axis=1).astype(BF16), BF16).reshape(b, t, -1)

    zero_pad = jnp.zeros((d, LANES - SSM_HEADS), F32)
    w_dt2 = jnp.concatenate([w_dt[:, :SSM_HEADS], zero_pad, w_dt[:, SSM_HEADS:], zero_pad], axis=1)
    dt2 = _mm(h2, w_dt2.astype(BF16), F32).reshape(b, t, 2 * LANES)
    xbc_act = _dwconv_stream(p_rm, ssm_conv_w, ssm_conv_b, n_ctx, True, c0=col_xbc)
    ym = _ssd_scan(xbc_act, dt2, p_rm, col_z, ssm_dt_bias, ssm_a_log, ssm_d, ssm_norm, n_ctx)

    hy_ctx, hy_lat = _dwconv_stream(p_rm, hy_conv_w, hy_conv_b, n_ctx, False, c0=col_hy, split=True)
    taps_ctx = _hyena_filter_taps(n_ctx, hy_w1, hy_b1, hy_w2, hy_b2, hy_w3, hy_b3, hy_freq)
    yh_ctx = _hyena_ctx(hy_ctx, taps_ctx, hy_bias)
    h_spec = _hyena_filter_spectrum_pallas(_hyena_filter_hidden(n_lat, hy_w1, hy_b1, hy_w2, hy_b2, hy_freq), hy_w3, hy_b3, _hyena_decay_rates())
    yh_lat = _hyena_long(hy_lat, h_spec, hy_bias)

    h_cm = jnp.concatenate([h_rm[:, :n_ctx], _to_col_major(h_rm[:, n_ctx:], rows)], axis=1).reshape(b * t, d)

    p_cm = _mm(h_cm, jnp.concatenate([w_q, w_f, w_i], axis=1).astype(BF16), BF16).reshape(b, t, -1)
    og = _hgrn_scan(p_cm, lb, hg_norm, n_ctx)
    return ym, (yh_ctx, yh_lat), (og, _from_col_major(og[:, n_ctx:], rows)), p_rm, col_g, col_gate


ROW_TILE = 256
MOD_ROWS = SUBLANES
M_SHIFT_MIX, M_SCALE_MIX, M_GATE_MIX, M_SHIFT_FFN, M_SCALE_FFN, M_GATE_FFN = range(6)
ROW_PARAMS = pltpu.CompilerParams(dimension_semantics=("arbitrary", "arbitrary"), vmem_limit_bytes=VMEM_LIMIT)


def _rms(x):
    return x * lax.rsqrt(jnp.mean(x * x, axis=-1, keepdims=True) + EPS)


def _mrow(m_ref, r):
    return m_ref[r:r + 1, :]


def _row_spec(width):
    return pl.BlockSpec((None, ROW_TILE, width), lambda bi, i: (bi, i, 0))


def _vec_spec(width):
    return pl.BlockSpec((1, width), lambda bi, i: (0, 0))


def _mat_spec(k, n):
    return pl.BlockSpec((k, n), lambda bi, i: (0, 0))


def _mod_spec(n_ctx):
    return pl.BlockSpec((None, None, MOD_ROWS, D_MODEL), lambda bi, i: (bi, jnp.where(i < n_ctx // ROW_TILE, 0, 1), 0, 0))


def _norm_mod_kernel(x_ref, w_ref, m_ref, o_ref):
    y = _rms(x_ref[...]) * w_ref[...]
    o_ref[...] = (y * (1.0 + _mrow(m_ref, M_SCALE_MIX)) + _mrow(m_ref, M_SHIFT_MIX)).astype(o_ref.dtype)


def _norm_mod(xs, w, mods, n_ctx):
    b, t, d = xs.shape
    return pl.pallas_call(
        _norm_mod_kernel,
        grid=(b, t // ROW_TILE),
        in_specs=[_row_spec(d), _vec_spec(d), _mod_spec(n_ctx)],
        out_specs=_row_spec(d),
        out_shape=jax.ShapeDtypeStruct((b, t, d), BF16),
        compiler_params=ROW_PARAMS,
        name="norm_mod",
    )(xs, w.reshape(1, d), mods)


def _merge_kernel(ym_ref, yhc_ref, yhl_ref, ogc_ref, ogl_ref, g_ref, gm_ref, gh_ref, gg_ref, x_ref, m_ref, w1_ref, w2_ref, w3_ref, wo_ref, npost_ref, npre_ref, rw_ref, rb_ref, xo_ref, h_ref, lg_ref, *, n_ctx_tiles):
    is_ctx = pl.program_id(1) < n_ctx_tiles
    yh = jnp.where(is_ctx, yhc_ref[...], yhl_ref[...])
    og = jnp.where(is_ctx, ogc_ref[...], ogl_ref[...])
    def sig(ref):
        return jax.nn.sigmoid(ref[...].astype(F32))

    go = g_ref[...].astype(F32)
    yg = (og.astype(F32) * (go * jax.nn.sigmoid(go))).astype(BF16)
    merged = sig(gm_ref) * jnp.dot(ym_ref[...], w1_ref[...], preferred_element_type=F32)
    merged = merged + sig(gh_ref) * jnp.dot(yh, w2_ref[...], preferred_element_type=F32)
    merged = merged + sig(gg_ref) * jnp.dot(yg, w3_ref[...], preferred_element_type=F32)
    mix = jnp.dot(merged.astype(BF16), wo_ref[...], preferred_element_type=F32)
    x = x_ref[...] + _mrow(m_ref, M_GATE_MIX) * (_rms(mix) * npost_ref[...])
    xo_ref[...] = x
    h = (_rms(x) * npre_ref[...] * (1.0 + _mrow(m_ref, M_SCALE_FFN)) + _mrow(m_ref, M_SHIFT_FFN)).astype(BF16)
    h_ref[...] = h
    lg_ref[...] = jnp.dot(h, rw_ref[...], preferred_element_type=F32) + rb_ref[...]


def _merge(ym, yh_parts, og_parts, p_rm, col_g, col_gate, xs, mods, w_br_ssm, w_br_hy, w_br_hg, w_out, norm_post, norm_ffn_pre, router_w, router_b, n_ctx):
    b, t, d = xs.shape
    rw = jnp.pad(router_w, ((0, 0), (0, LANES - N_EXPERTS))).astype(BF16)
    rb = jnp.pad(router_b, (0, LANES - N_EXPERTS)).reshape(1, LANES).astype(F32)

    def col_spec(col):
        return pl.BlockSpec((None, ROW_TILE, d), lambda bi, i: (bi, i, col // d))

    nct = n_ctx // ROW_TILE
    ctx_spec = pl.BlockSpec((None, ROW_TILE, d), lambda bi, i: (bi, jnp.minimum(i, nct - 1), 0))
    lat_spec = pl.BlockSpec((None, ROW_TILE, d), lambda bi, i: (bi, jnp.maximum(i - nct, 0), 0))

    return pl.pallas_call(
        functools.partial(_merge_kernel, n_ctx_tiles=nct),
        grid=(b, t // ROW_TILE),
        in_specs=[_row_spec(d), ctx_spec, lat_spec, ctx_spec, lat_spec, col_spec(col_g), col_spec(col_gate), col_spec(col_gate + d), col_spec(col_gate + 2 * d), _row_spec(d), _mod_spec(n_ctx),
                  _mat_spec(d, d), _mat_spec(d, d), _mat_spec(d, d), _mat_spec(d, d), _vec_spec(d), _vec_spec(d), _mat_spec(d, LANES), _vec_spec(LANES)],
        out_specs=[_row_spec(d), _row_spec(d), _row_spec(LANES)],
        out_shape=[jax.ShapeDtypeStruct((b, t, d), F32), jax.ShapeDtypeStruct((b, t, d), BF16), jax.ShapeDtypeStruct((b, t, LANES), F32)],
        compiler_params=ROW_PARAMS,
        name="branch_merge",
    )(ym, yh_parts[0], yh_parts[1], og_parts[0], og_parts[1], p_rm, p_rm, p_rm, p_rm, xs, mods, w_br_ssm.astype(BF16), w_br_hy.astype(BF16), w_br_hg.astype(BF16), w_out.astype(BF16),
      norm_post.reshape(1, d), norm_ffn_pre.reshape(1, d), rw, rb)


def _post_ffn_kernel(y0_ref, y1_ref, y2_ref, y3_ref, x_ref, m_ref, w_ref, *rest):
    f = y0_ref[...].astype(F32) + y1_ref[...].astype(F32) + y2_ref[...].astype(F32) + y3_ref[...].astype(F32)
    x = x_ref[...] + _mrow(m_ref, M_GATE_FFN) * (_rms(f) * w_ref[...])
    if len(rest) == 1:
        rest[0][...] = x
    else:
        nm_ref, nw_ref, o_ref, h_ref = rest
        o_ref[...] = x
        h_ref[...] = (_rms(x) * nw_ref[...] * (1.0 + _mrow(nm_ref, M_SCALE_MIX)) + _mrow(nm_ref, M_SHIFT_MIX)).astype(h_ref.dtype)


def _post_ffn(f4, xs, mods, norm_post, n_ctx, skip, nxt=None):
    b, _, d = xs.shape
    t = f4.shape[2]
    i0 = skip // ROW_TILE

    def k_spec(k):
        return pl.BlockSpec((None, None, ROW_TILE, d), lambda bi, i: (k, bi, i, 0))

    mod_spec = pl.BlockSpec((None, None, MOD_ROWS, d), lambda bi, i: (bi, jnp.where(i0 + i < n_ctx // ROW_TILE, 0, 1), 0, 0))
    in_specs = [k_spec(k) for k in range(TOP_K)] + [pl.BlockSpec((None, ROW_TILE, d), lambda bi, i: (bi, i0 + i, 0)), mod_spec, _vec_spec(d)]
    args = [f4, f4, f4, f4, xs, mods, norm_post.reshape(1, d)]
    out_specs, out_shape = _row_spec(d), jax.ShapeDtypeStruct((b, t, d), F32)
    if nxt is not None:
        in_specs += [mod_spec, _vec_spec(d)]
        args += [nxt[0], nxt[1].reshape(1, d)]
        out_specs, out_shape = [out_specs, _row_spec(d)], [out_shape, jax.ShapeDtypeStruct((b, t, d), BF16)]
    return pl.pallas_call(
        _post_ffn_kernel,
        grid=(b, t // ROW_TILE),
        in_specs=in_specs,
        out_specs=out_specs,
        out_shape=out_shape,
        compiler_params=ROW_PARAMS,
        name="post_ffn",
    )(*args)


def _moe_ffn(h2, logits, t_per_b, skip, li, w1, b1, w2, b2):
    t = logits.shape[0]
    d = h2.shape[1]
    n = t * TOP_K
    n_tiles = n // MOE_BLOCK
    top_v, top_e = lax.top_k(logits, TOP_K)
    gate_w = jax.nn.softmax(top_v, axis=-1)
    flat_e = top_e.reshape(n).astype(jnp.int32)
    iota = jnp.arange(n, dtype=jnp.int32)
    _, order, sw = lax.sort((flat_e, iota, gate_w.reshape(n)), num_keys=1, is_stable=True)
    _, inv = lax.sort((order, iota), num_keys=1)
    tok = order // TOP_K
    xs = h2[tok + (tok // t_per_b + 1) * skip]
    counts = jnp.sum((flat_e[:, None] == jnp.arange(N_EXPERTS, dtype=jnp.int32)[None, :]).astype(jnp.int32), axis=0)
    end = jnp.cumsum(counts)
    start = end - counts
    first_tile = start // MOE_BLOCK
    n_items = jnp.where(counts > 0, (end - 1) // MOE_BLOCK - first_tile + 1, 0)
    items_end = jnp.cumsum(n_items)
    w = jnp.arange(n_tiles + N_EXPERTS, dtype=jnp.int32)
    valid = w < items_end[-1]
    e_w = jnp.minimum(jnp.sum((w[:, None] >= items_end[None, :]).astype(jnp.int32), axis=1), N_EXPERTS - 1)
    tile_w = first_tile[e_w] + (w - (items_end[e_w] - n_items[e_w]))
    lo = jnp.where(valid, jnp.maximum(start[e_w], tile_w * MOE_BLOCK), 0)
    hi = jnp.where(valid, jnp.minimum(end[e_w], (tile_w + 1) * MOE_BLOCK), 0)
    tile_w = jnp.where(valid, tile_w, n_tiles - 1)
    first = jnp.concatenate([jnp.ones((1,), jnp.int32), (tile_w[1:] != tile_w[:-1]).astype(jnp.int32)])
    newexp = jnp.concatenate([jnp.ones((1,), jnp.int32), (e_w[1:] != e_w[:-1]).astype(jnp.int32)])
    ys = _moe_experts(xs, sw, tile_w.astype(jnp.int32), e_w, lo.astype(jnp.int32), hi.astype(jnp.int32), first, newexp, li, w1, b1, w2, b2)
    return ys[inv.reshape(t, TOP_K).T]


def kernel(x, c, ctx, c_ctx, w_mod, b_mod, norm_mix_pre, norm_mix_post, norm_ffn_pre, norm_ffn_post, w_in, ssm_conv_w, ssm_conv_b, ssm_dt_bias, ssm_a_log, ssm_d, ssm_norm, hy_conv_w, hy_conv_b, hy_w1, hy_b1, hy_w2, hy_b2, hy_w3, hy_b3, hy_freq, hy_bias, hg_lb_logits, hg_norm, w_br_ssm, w_br_hy, w_br_hg, w_out, router_w, router_b, exp_w1, exp_b1, exp_w2, exp_b2):
    hp = lax.Precision.HIGHEST
    b, n_lat, d = x.shape
    n_ctx = ctx.shape[1]
    lb = jax.nn.softmax(hg_lb_logits.astype(F32), axis=1)
    lb = jnp.cumsum(lb, axis=1) - lb[:, :1]
    silu_c = jax.nn.silu(c)
    silu_cc = jax.nn.silu(c_ctx)
    xs = jnp.concatenate([ctx, x], axis=1)
    all_mods = []
    for li in range(DEPTH):
        mx = (jnp.dot(silu_c, w_mod[li], precision=hp) + b_mod[li]).reshape(b, 1, 6, d)
        mc = jnp.broadcast_to((jnp.dot(silu_cc, w_mod[li], precision=hp) + b_mod[li]).reshape(1, 1, 6, d), (b, 1, 6, d))
        all_mods.append(jnp.pad(jnp.concatenate([mc, mx], axis=1), ((0, 0), (0, 0), (0, MOD_ROWS - 6), (0, 0))))
    h = _norm_mod(xs, norm_mix_pre[0], all_mods[0], n_ctx)
    for li in range(DEPTH):
        mods = all_mods[li]
        ym, yh_parts, og_parts, p_rm, col_g, col_gate = _mixer_branches(h, w_in[li], lb[:, li], n_ctx, ssm_conv_w[li], ssm_conv_b[li], ssm_dt_bias[li], ssm_a_log[li], ssm_d[li], ssm_norm[li], hy_conv_w[li], hy_conv_b[li], hy_w1[li], hy_b1[li], hy_w2[li], hy_b2[li], hy_w3[li], hy_b3[li], hy_freq[li], hy_bias[li], hg_norm[li])
        xs, h_ffn, logits = _merge(ym, yh_parts, og_parts, p_rm, col_g, col_gate, xs, mods, w_br_ssm[li], w_br_hy[li], w_br_hg[li], w_out[li], norm_mix_post[li], norm_ffn_pre[li], router_w[li], router_b[li], n_ctx)
        skip = n_ctx if li == DEPTH - 1 else 0
        t = xs.shape[1] - skip
        f4 = _moe_ffn(h_ffn.reshape(-1, d), logits[:, skip:, :N_EXPERTS].reshape(b * t, N_EXPERTS), t, skip, li, exp_w1, exp_b1[li], exp_w2, exp_b2[li])
        if li == DEPTH - 1:
            xs = _post_ffn(f4.reshape(TOP_K, b, t, d), xs, mods, norm_ffn_post[li], n_ctx, skip)
        else:
            xs, h = _post_ffn(f4.reshape(TOP_K, b, t, d), xs, mods, norm_ffn_post[li], n_ctx, skip, nxt=(all_mods[li + 1], norm_mix_pre[li + 1]))
    return xs
```
